```python
import math
import jax, jax.numpy as jnp
from jax import lax
import numpy as np

D_MODEL = 1024
BATCH = 8
SEQ = 8192
DEPTH = 2

N_META = 16
DN_HEAD_DIM = 128
DN_HEADS = D_MODEL // DN_HEAD_DIM
DN_KEY_DIM = DN_HEADS * DN_HEAD_DIM
DN_VAL_DIM = DN_HEADS * DN_HEAD_DIM
QKV_DIM = 2 * DN_KEY_DIM + DN_VAL_DIM
DN_CONV = 4
CHUNK = 64
LEAD_PAD = (-N_META) % CHUNK
POOL_GROUPS = 4
POOL_WINDOWS = (2, 4, 8, 16)
POOL_GROUP_DIM = D_MODEL // 8
POOL_WIDTH = POOL_GROUPS * POOL_GROUP_DIM
POOL_OUT_GROUP = D_MODEL // POOL_GROUPS
N_BRANCHES = 2
SPLIT_SIZES = (QKV_DIM, DN_VAL_DIM, DN_HEADS, DN_HEADS, POOL_WIDTH, N_BRANCHES * D_MODEL)
IN_DIM = QKV_DIM + DN_VAL_DIM + 2 * DN_HEADS + POOL_WIDTH + N_BRANCHES * D_MODEL
D_FF = 256 * ((8 * D_MODEL // 3 + 255) // 256)
FFN_CONV = 3
NORM_EPS = 1e-6

kernel_name = 'hybrid_gdn_pool_meta_block'


def rms_norm(x, gain):
    xf = x.astype(jnp.float32)
    y = xf * lax.rsqrt(jnp.mean(xf * xf, axis=-1, keepdims=True) + NORM_EPS) * gain.astype(jnp.float32)
    return y.astype(x.dtype)


def l2_normalize(x):
    xf = x.astype(jnp.float32)
    return xf * lax.rsqrt(jnp.sum(xf * xf, axis=-1, keepdims=True) + NORM_EPS)


def causal_depthwise_conv(x, w):
    k_width = w.shape[0]
    length = x.shape[1]
    xp = jnp.pad(x, ((0, 0), (k_width - 1, 0), (0, 0)))
    out = xp[:, 0:length] * w[0]
    for j in range(1, k_width):
        out = out + xp[:, j:j + length] * w[j]
    return out


def chunk_gated_delta_rule(q, k, v, g, beta):
    b, t, h, dk = k.shape
    dv = v.shape[-1]
    n = t // CHUNK

    def chunks(a):
        a = a.reshape((b, n, CHUNK, h) + a.shape[3:])
        return jnp.moveaxis(a, 3, 1)

    q = chunks(q) * (dk ** -0.5)
    k = chunks(k)
    v = chunks(v)
    beta = chunks(beta)
    g = lax.cumsum(chunks(g), axis=3)
    idx = jnp.arange(CHUNK)
    causal = idx[:, None] >= idx[None, :]
    strict = idx[:, None] > idx[None, :]
    decay = jnp.exp(jnp.where(causal, g[..., :, None] - g[..., None, :], -jnp.inf))
    kb = k * beta[..., None]
    lower = jnp.where(strict, jnp.einsum('bhncd,bhnsd->bhncs', kb, k) * decay, 0.0)
    eye = jnp.eye(CHUNK, dtype=lower.dtype)
    tinv = lax.linalg.triangular_solve(lower + eye, jnp.broadcast_to(eye, lower.shape),
                                       left_side=True, lower=True, unit_diagonal=True)
    u = jnp.einsum('bhncs,bhnsv->bhncv', tinv, v * beta[..., None])
    w = jnp.einsum('bhncs,bhnsd->bhncd', tinv, kb * jnp.exp(g)[..., None])
    qk = jnp.where(causal, jnp.einsum('bhncd,bhnsd->bhncs', q, k) * decay, 0.0)
    q_dec = q * jnp.exp(g)[..., None]
    k_dec = k * jnp.exp(g[..., -1:] - g)[..., None]
    g_tot = jnp.exp(g[..., -1])

    def step(state, inp):
        q_i, k_i, u_i, w_i, qk_i, gt_i = inp
        v_new = u_i - jnp.einsum('bhck,bhkv->bhcv', w_i, state)
        o_i = jnp.einsum('bhck,bhkv->bhcv', q_i, state) + jnp.einsum('bhcs,bhsv->bhcv', qk_i, v_new)
        state = state * gt_i[..., None, None] + jnp.einsum('bhck,bhcv->bhkv', k_i, v_new)
        return state, o_i

    xs = (jnp.moveaxis(q_dec, 2, 0), jnp.moveaxis(k_dec, 2, 0), jnp.moveaxis(u, 2, 0),
          jnp.moveaxis(w, 2, 0), jnp.moveaxis(qk, 2, 0), jnp.moveaxis(g_tot, 2, 0))
    state0 = jnp.zeros((b, h, dk, dv), jnp.float32)
    _, o = lax.scan(step, state0, xs)
    o = jnp.moveaxis(o, 0, 2)
    return jnp.moveaxis(o, 1, 3).reshape(b, t, h, dv)


def multiscale_causal_pool(p, w_pool, pool_scale):
    b, l, _ = p.shape
    pg = p.astype(jnp.float32).reshape(b, l, POOL_GROUPS, POOL_GROUP_DIM)
    csum = lax.cumsum(pg, axis=1)
    pos = jnp.arange(l)
    outs = []
    for gi, win in enumerate(POOL_WINDOWS):
        c = csum[:, :, gi]
        lagged = jnp.pad(c, ((0, 0), (win, 0), (0, 0)))[:, :l]
        count = jnp.minimum(pos + 1, win).astype(jnp.float32)[None, :, None]
        outs.append((c - lagged) / count - pg[:, :, gi])
    pooled = jnp.stack(outs, axis=2)
    y = jnp.einsum('blgc,gcd->blgd', pooled, w_pool.astype(jnp.float32)).reshape(b, l, D_MODEL)
    return (y * pool_scale.astype(jnp.float32)).astype(p.dtype)


def hybrid_mixer(u, w_in, conv_qkv, a_log, dt_bias, head_norm, w_pool, pool_scale, w_out):
    b, l, _ = u.shape
    proj = jnp.einsum('bld,dp->blp', u, w_in)
    offsets = [int(o) for o in np.cumsum(SPLIT_SIZES)[:-1]]
    qkv, z, b_raw, a_raw, pool_in, gate_pre = jnp.split(proj, offsets, axis=-1)
    qkv = jax.nn.silu(causal_depthwise_conv(qkv, conv_qkv))
    q, k, v = jnp.split(qkv, [DN_KEY_DIM, 2 * DN_KEY_DIM], axis=-1)
    q = l2_normalize(q.reshape(b, l, DN_HEADS, DN_HEAD_DIM))
    k = l2_normalize(k.reshape(b, l, DN_HEADS, DN_HEAD_DIM))
    v = v.reshape(b, l, DN_HEADS, DN_HEAD_DIM).astype(jnp.float32)
    beta = jax.nn.sigmoid(b_raw.astype(jnp.float32))
    g = -jnp.exp(a_log.astype(jnp.float32)) * jax.nn.softplus(
        a_raw.astype(jnp.float32) + dt_bias.astype(jnp.float32))
    pad_r = (-(LEAD_PAD + l)) % CHUNK

    def pad_t(a):
        return jnp.pad(a, [(0, 0), (LEAD_PAD, pad_r)] + [(0, 0)] * (a.ndim - 2))

    o = chunk_gated_delta_rule(pad_t(q), pad_t(k), pad_t(v), pad_t(g), pad_t(beta))
    o = o[:, LEAD_PAD:LEAD_PAD + l]
    zf = z.astype(jnp.float32).reshape(b, l, DN_HEADS, DN_HEAD_DIM)
    o = (o * lax.rsqrt(jnp.mean(o * o, axis=-1, keepdims=True) + NORM_EPS)
         * head_norm.astype(jnp.float32) * jax.nn.silu(zf))
    y_a = o.reshape(b, l, DN_VAL_DIM).astype(u.dtype)
    y_b = multiscale_causal_pool(pool_in, w_pool, pool_scale)
    g_a, g_b = jnp.split(jax.nn.sigmoid(gate_pre), 2, axis=-1)
    y = g_a * y_a + g_b * y_b
    return jnp.einsum('bld,de->ble', y, w_out)


def conv_gated_mlp(u, w_up, conv_ffn, w_down):
    hid = jnp.einsum('bld,df->blf', u, w_up)
    hid = causal_depthwise_conv(hid, conv_ffn)
    gate, val = jnp.split(hid, 2, axis=-1)
    return jnp.einsum('blf,fd->bld', jax.nn.silu(gate) * val, w_down)


def _fwd_setup_inputs(seed: int = 0) -> dict:
    key = jax.random.key(seed)
    ks = jax.random.split(key, 16)
    f32 = jnp.float32

    def normal(k, shape, scale):
        return jax.random.normal(k, shape, f32) * scale

    x = normal(ks[0], (BATCH, SEQ, D_MODEL), 1.0)
    meta_tokens = normal(ks[1], (N_META, D_MODEL), 1.0)
    norm_mix = 1.0 + normal(ks[2], (DEPTH, D_MODEL), 0.02)
    w_in = normal(ks[3], (DEPTH, D_MODEL, IN_DIM), D_MODEL ** -0.5)
    conv_qkv = normal(ks[4], (DEPTH, DN_CONV, QKV_DIM), DN_CONV ** -0.5)
    a_log = jnp.log(jax.random.uniform(ks[5], (DEPTH, DN_HEADS), f32, 1.0, 16.0))
    dt = jnp.exp(jax.random.uniform(ks[6], (DEPTH, DN_HEADS), f32, math.log(1e-3), math.log(1e-1)))
    dt_bias = dt + jnp.log(-jnp.expm1(-dt))
    head_norm = 1.0 + normal(ks[7], (DEPTH, DN_HEAD_DIM), 0.02)
    w_pool = normal(ks[8], (DEPTH, POOL_GROUPS, POOL_GROUP_DIM, POOL_OUT_GROUP), POOL_GROUP_DIM ** -0.5)
    pool_scale = 1.0 + normal(ks[9], (DEPTH, D_MODEL), 0.02)
    w_out = normal(ks[10], (DEPTH, D_MODEL, D_MODEL), D_MODEL ** -0.5)
    norm_ffn = 1.0 + normal(ks[11], (DEPTH, D_MODEL), 0.02)
    w_up = normal(ks[12], (DEPTH, D_MODEL, 2 * D_FF), D_MODEL ** -0.5)
    conv_ffn = normal(ks[13], (DEPTH, FFN_CONV, 2 * D_FF), FFN_CONV ** -0.5)
    w_down = normal(ks[14], (DEPTH, D_FF, D_MODEL), D_FF ** -0.5)
    norm_final = 1.0 + normal(ks[15], (D_MODEL,), 0.02)
    return {'x': x, 'meta_tokens': meta_tokens, 'norm_mix': norm_mix, 'w_in': w_in,
            'conv_qkv': conv_qkv, 'a_log': a_log, 'dt_bias': dt_bias, 'head_norm': head_norm,
            'w_pool': w_pool, 'pool_scale': pool_scale, 'w_out': w_out, 'norm_ffn': norm_ffn,
            'w_up': w_up, 'conv_ffn': conv_ffn, 'w_down': w_down, 'norm_final': norm_final}


def _fwd_reference(x, meta_tokens, norm_mix, w_in, conv_qkv, a_log, dt_bias, head_norm, w_pool,
              pool_scale, w_out, norm_ffn, w_up, conv_ffn, w_down, norm_final):
    b = x.shape[0]
    meta = jnp.broadcast_to(meta_tokens.astype(x.dtype)[None], (b, N_META, D_MODEL))
    h = jnp.concatenate([meta, x], axis=1)
    for layer in range(DEPTH):
        h = h + hybrid_mixer(rms_norm(h, norm_mix[layer]), w_in[layer], conv_qkv[layer], a_log[layer],
                             dt_bias[layer], head_norm[layer], w_pool[layer], pool_scale[layer], w_out[layer])
        h = h + conv_gated_mlp(rms_norm(h, norm_ffn[layer]), w_up[layer], conv_ffn[layer], w_down[layer])
    h = rms_norm(h, norm_final)
    return h[:, N_META:]


import jax as _jax
import jax.numpy as _jnp

TWIN_FORMAT = 'train_step'
FWD_PARAMS = ['x', 'meta_tokens', 'norm_mix', 'w_in', 'conv_qkv', 'a_log', 'dt_bias', 'head_norm', 'w_pool', 'pool_scale', 'w_out', 'norm_ffn', 'w_up', 'conv_ffn', 'w_down', 'norm_final']
TWIN_WEIGHTS = ['meta_tokens', 'norm_mix', 'w_in', 'conv_qkv', 'a_log', 'dt_bias', 'head_norm', 'w_pool', 'pool_scale', 'w_out', 'norm_ffn', 'w_up', 'conv_ffn', 'w_down', 'norm_final']
TWIN_DIFF_INPUT = 'x'
TWIN_INPUTS = ['x', 'meta_tokens', 'norm_mix', 'w_in', 'conv_qkv', 'a_log', 'dt_bias', 'head_norm', 'w_pool', 'pool_scale', 'w_out', 'norm_ffn', 'w_up', 'conv_ffn', 'w_down', 'norm_final', 'loss_target', 'm_meta_tokens', 'm_norm_mix', 'm_w_in', 'm_conv_qkv', 'm_a_log', 'm_dt_bias', 'm_head_norm', 'm_w_pool', 'm_pool_scale', 'm_w_out', 'm_norm_ffn', 'm_w_up', 'm_conv_ffn', 'm_w_down', 'm_norm_final', 'v_meta_tokens', 'v_norm_mix', 'v_w_in', 'v_conv_qkv', 'v_a_log', 'v_dt_bias', 'v_head_norm', 'v_w_pool', 'v_pool_scale', 'v_w_out', 'v_norm_ffn', 'v_w_up', 'v_conv_ffn', 'v_w_down', 'v_norm_final']
TWIN_OUTPUTS = ['loss', 'grad_x', 'grad_meta_tokens', 'grad_norm_mix', 'grad_w_in', 'grad_conv_qkv', 'grad_a_log', 'grad_dt_bias', 'grad_head_norm', 'grad_w_pool', 'grad_pool_scale', 'grad_w_out', 'grad_norm_ffn', 'grad_w_up', 'grad_conv_ffn', 'grad_w_down', 'grad_norm_final', 'delta_meta_tokens', 'delta_norm_mix', 'delta_w_in', 'delta_conv_qkv', 'delta_a_log', 'delta_dt_bias', 'delta_head_norm', 'delta_w_pool', 'delta_pool_scale', 'delta_w_out', 'delta_norm_ffn', 'delta_w_up', 'delta_conv_ffn', 'delta_w_down', 'delta_norm_final', 'new_m_meta_tokens', 'new_m_norm_mix', 'new_m_w_in', 'new_m_conv_qkv', 'new_m_a_log', 'new_m_dt_bias', 'new_m_head_norm', 'new_m_w_pool', 'new_m_pool_scale', 'new_m_w_out', 'new_m_norm_ffn', 'new_m_w_up', 'new_m_conv_ffn', 'new_m_w_down', 'new_m_norm_final', 'new_v_meta_tokens', 'new_v_norm_mix', 'new_v_w_in', 'new_v_conv_qkv', 'new_v_a_log', 'new_v_dt_bias', 'new_v_head_norm', 'new_v_w_pool', 'new_v_pool_scale', 'new_v_w_out', 'new_v_norm_ffn', 'new_v_w_up', 'new_v_conv_ffn', 'new_v_w_down', 'new_v_norm_final']
TWIN_LEAF_KINDS = {'loss': 'loss', 'grad_x': 'grad_x', 'grad_meta_tokens': 'grad_w', 'grad_norm_mix': 'grad_w', 'grad_w_in': 'grad_w', 'grad_conv_qkv': 'grad_w', 'grad_a_log': 'grad_w', 'grad_dt_bias': 'grad_w', 'grad_head_norm': 'grad_w', 'grad_w_pool': 'grad_w', 'grad_pool_scale': 'grad_w', 'grad_w_out': 'grad_w', 'grad_norm_ffn': 'grad_w', 'grad_w_up': 'grad_w', 'grad_conv_ffn': 'grad_w', 'grad_w_down': 'grad_w', 'grad_norm_final': 'grad_w', 'delta_meta_tokens': 'delta_w', 'delta_norm_mix': 'delta_w', 'delta_w_in': 'delta_w', 'delta_conv_qkv': 'delta_w', 'delta_a_log': 'delta_w', 'delta_dt_bias': 'delta_w', 'delta_head_norm': 'delta_w', 'delta_w_pool': 'delta_w', 'delta_pool_scale': 'delta_w', 'delta_w_out': 'delta_w', 'delta_norm_ffn': 'delta_w', 'delta_w_up': 'delta_w', 'delta_conv_ffn': 'delta_w', 'delta_w_down': 'delta_w', 'delta_norm_final': 'delta_w', 'new_m_meta_tokens': 'new_m', 'new_m_norm_mix': 'new_m', 'new_m_w_in': 'new_m', 'new_m_conv_qkv': 'new_m', 'new_m_a_log': 'new_m', 'new_m_dt_bias': 'new_m', 'new_m_head_norm': 'new_m', 'new_m_w_pool': 'new_m', 'new_m_pool_scale': 'new_m', 'new_m_w_out': 'new_m', 'new_m_norm_ffn': 'new_m', 'new_m_w_up': 'new_m', 'new_m_conv_ffn': 'new_m', 'new_m_w_down': 'new_m', 'new_m_norm_final': 'new_m', 'new_v_meta_tokens': 'new_v', 'new_v_norm_mix': 'new_v', 'new_v_w_in': 'new_v', 'new_v_conv_qkv': 'new_v', 'new_v_a_log': 'new_v', 'new_v_dt_bias': 'new_v', 'new_v_head_norm': 'new_v', 'new_v_w_pool': 'new_v', 'new_v_pool_scale': 'new_v', 'new_v_w_out': 'new_v', 'new_v_norm_ffn': 'new_v', 'new_v_w_up': 'new_v', 'new_v_conv_ffn': 'new_v', 'new_v_w_down': 'new_v', 'new_v_norm_final': 'new_v'}


def _forward(args):
    return _fwd_reference(*[args[k] for k in FWD_PARAMS])


def _output_shape():
    def fwd():
        inp = _fwd_setup_inputs(0)
        return _fwd_reference(*[inp[k] for k in FWD_PARAMS])
    out = _jax.eval_shape(fwd)
    return out.shape, out.dtype

N_MICROBATCH = 1
ADAM_LR = 0.001
ADAM_B1 = 0.9
ADAM_B2 = 0.999
ADAM_EPS = 1e-08
ADAM_WD = 0.01
ADAM_STEP = 10
PER_EXAMPLE_BATCH_AXIS = {'x': 0, 'loss_target': 0}
SHARED_INPUTS = []
_WEIGHT_DTYPES = {'meta_tokens': _jnp.float32, 'norm_mix': _jnp.float32, 'w_in': _jnp.float32, 'conv_qkv': _jnp.float32, 'a_log': _jnp.float32, 'dt_bias': _jnp.float32, 'head_norm': _jnp.float32, 'w_pool': _jnp.float32, 'pool_scale': _jnp.float32, 'w_out': _jnp.float32, 'norm_ffn': _jnp.float32, 'w_up': _jnp.float32, 'conv_ffn': _jnp.float32, 'w_down': _jnp.float32, 'norm_final': _jnp.float32}
MOMENT_SCALE = {'meta_tokens': 6.827511e-03, 'norm_mix': 1.856119e-01, 'w_in': 6.972444e-02, 'conv_qkv': 5.815552e-02, 'a_log': 2.873028e-01, 'dt_bias': 2.751391e-01, 'head_norm': 2.183512e-01, 'w_pool': 1.126472e-01, 'pool_scale': 1.145057e-01, 'w_out': 1.352635e-01, 'norm_ffn': 1.784914e-01, 'w_up': 7.338687e-02, 'conv_ffn': 7.298907e-02, 'w_down': 1.199048e-01, 'norm_final': 6.398148e+01}


def _to_microbatches(a, axis):
    t = _jnp.moveaxis(a, axis, 0)
    t = t.reshape((N_MICROBATCH, t.shape[0] // N_MICROBATCH) + t.shape[1:])
    return _jnp.moveaxis(t, 1, axis + 1)


def setup_inputs(seed: int = 0) -> dict:
    inp = _fwd_setup_inputs(seed)
    key = _jax.random.fold_in(_jax.random.key(seed), 7919)
    shape, _ = _output_shape()
    out = dict(inp)
    out["loss_target"] = _jax.random.normal(_jax.random.fold_in(key, 0), shape, _jnp.float32)
    for i, name in enumerate(TWIN_WEIGHTS):
        w = inp[name].astype(_jnp.float32)
        if MOMENT_SCALE is None:
            s = _jnp.sqrt(_jnp.mean(_jnp.square(w)) + 1e-30)
        else:
            s = MOMENT_SCALE[name]
        km, kv = _jax.random.split(_jax.random.fold_in(key, i + 1))
        out[name] = w
        out["m_" + name] = s * _jax.random.normal(km, w.shape, _jnp.float32)
        out["v_" + name] = (s * s) * _jax.random.uniform(kv, w.shape, _jnp.float32, 0.5, 1.5)
    if N_MICROBATCH > 1:
        for name, axis in PER_EXAMPLE_BATCH_AXIS.items():
            out[name] = _to_microbatches(out[name], axis)
    return {'x': out['x'], 'meta_tokens': out['meta_tokens'], 'norm_mix': out['norm_mix'], 'w_in': out['w_in'], 'conv_qkv': out['conv_qkv'], 'a_log': out['a_log'], 'dt_bias': out['dt_bias'], 'head_norm': out['head_norm'], 'w_pool': out['w_pool'], 'pool_scale': out['pool_scale'], 'w_out': out['w_out'], 'norm_ffn': out['norm_ffn'], 'w_up': out['w_up'], 'conv_ffn': out['conv_ffn'], 'w_down': out['w_down'], 'norm_final': out['norm_final'], 'loss_target': out['loss_target'], 'm_meta_tokens': out['m_meta_tokens'], 'm_norm_mix': out['m_norm_mix'], 'm_w_in': out['m_w_in'], 'm_conv_qkv': out['m_conv_qkv'], 'm_a_log': out['m_a_log'], 'm_dt_bias': out['m_dt_bias'], 'm_head_norm': out['m_head_norm'], 'm_w_pool': out['m_w_pool'], 'm_pool_scale': out['m_pool_scale'], 'm_w_out': out['m_w_out'], 'm_norm_ffn': out['m_norm_ffn'], 'm_w_up': out['m_w_up'], 'm_conv_ffn': out['m_conv_ffn'], 'm_w_down': out['m_w_down'], 'm_norm_final': out['m_norm_final'], 'v_meta_tokens': out['v_meta_tokens'], 'v_norm_mix': out['v_norm_mix'], 'v_w_in': out['v_w_in'], 'v_conv_qkv': out['v_conv_qkv'], 'v_a_log': out['v_a_log'], 'v_dt_bias': out['v_dt_bias'], 'v_head_norm': out['v_head_norm'], 'v_w_pool': out['v_w_pool'], 'v_pool_scale': out['v_pool_scale'], 'v_w_out': out['v_w_out'], 'v_norm_ffn': out['v_norm_ffn'], 'v_w_up': out['v_w_up'], 'v_conv_ffn': out['v_conv_ffn'], 'v_w_down': out['v_w_down'], 'v_norm_final': out['v_norm_final']}


def _loss(weights, diff, rest, loss_target):
    with _jax.named_scope("forward"):
        args = {**rest, TWIN_DIFF_INPUT: diff, **{k: w.astype(_WEIGHT_DTYPES[k]) for k, w in weights.items()}}
        y = _forward(args)
    with _jax.named_scope("loss_head"):
        err = _jnp.square(y.astype(_jnp.float32) - loss_target)
        return 0.5 * _jnp.sum(_jnp.mean(err, axis=-1)) if err.ndim else 0.5 * err


def _adamw(w, g, m, v):
    m = ADAM_B1 * m + (1.0 - ADAM_B1) * g
    v = ADAM_B2 * v + (1.0 - ADAM_B2) * _jnp.square(g)
    m_hat = m / (1.0 - ADAM_B1 ** ADAM_STEP)
    v_hat = v / (1.0 - ADAM_B2 ** ADAM_STEP)
    delta = -ADAM_LR * (m_hat / (_jnp.sqrt(v_hat) + ADAM_EPS) + ADAM_WD * w)
    return delta, m, v


def reference(x, meta_tokens, norm_mix, w_in, conv_qkv, a_log, dt_bias, head_norm, w_pool, pool_scale, w_out, norm_ffn, w_up, conv_ffn, w_down, norm_final, loss_target, m_meta_tokens, m_norm_mix, m_w_in, m_conv_qkv, m_a_log, m_dt_bias, m_head_norm, m_w_pool, m_pool_scale, m_w_out, m_norm_ffn, m_w_up, m_conv_ffn, m_w_down, m_norm_final, v_meta_tokens, v_norm_mix, v_w_in, v_conv_qkv, v_a_log, v_dt_bias, v_head_norm, v_w_pool, v_pool_scale, v_w_out, v_norm_ffn, v_w_up, v_conv_ffn, v_w_down, v_norm_final):
    given = dict(x=x, meta_tokens=meta_tokens, norm_mix=norm_mix, w_in=w_in, conv_qkv=conv_qkv, a_log=a_log, dt_bias=dt_bias, head_norm=head_norm, w_pool=w_pool, pool_scale=pool_scale, w_out=w_out, norm_ffn=norm_ffn, w_up=w_up, conv_ffn=conv_ffn, w_down=w_down, norm_final=norm_final, loss_target=loss_target, m_meta_tokens=m_meta_tokens, m_norm_mix=m_norm_mix, m_w_in=m_w_in, m_conv_qkv=m_conv_qkv, m_a_log=m_a_log, m_dt_bias=m_dt_bias, m_head_norm=m_head_norm, m_w_pool=m_w_pool, m_pool_scale=m_pool_scale, m_w_out=m_w_out, m_norm_ffn=m_norm_ffn, m_w_up=m_w_up, m_conv_ffn=m_conv_ffn, m_w_down=m_w_down, m_norm_final=m_norm_final, v_meta_tokens=v_meta_tokens, v_norm_mix=v_norm_mix, v_w_in=v_w_in, v_conv_qkv=v_conv_qkv, v_a_log=v_a_log, v_dt_bias=v_dt_bias, v_head_norm=v_head_norm, v_w_pool=v_w_pool, v_pool_scale=v_pool_scale, v_w_out=v_w_out, v_norm_ffn=v_norm_ffn, v_w_up=v_w_up, v_conv_ffn=v_conv_ffn, v_w_down=v_w_down, v_norm_final=v_norm_final)
    weights = {n: given[n] for n in TWIN_WEIGHTS}
    shared = {n: given[n] for n in SHARED_INPUTS}
    per_example = {n: given[n] for n in ['x']}
    grad_fn = _jax.value_and_grad(_loss, argnums=(0, 1))

    def one_microbatch(ex, loss_target):
        ex = dict(ex)
        diff = ex.pop(TWIN_DIFF_INPUT)
        return grad_fn(weights, diff, {**shared, **ex}, loss_target)

    if N_MICROBATCH == 1:
        loss, (grad_w, grad_x) = one_microbatch(per_example, given["loss_target"])
    else:
        def body(carry, xs):
            loss_sum, grad_sum = carry
            l_k, (gw_k, gx_k) = one_microbatch(xs[0], xs[1])
            with _jax.named_scope("update"):
                return (loss_sum + l_k, _jax.tree.map(_jnp.add, grad_sum, gw_k)), gx_k

        init = (_jnp.zeros((), _jnp.float32), _jax.tree.map(_jnp.zeros_like, weights))
        (loss, grad_w), grad_x = _jax.lax.scan(body, init, (per_example, given["loss_target"]))
    with _jax.named_scope("update"):
        delta_w, new_m, new_v = {}, {}, {}
        for n in TWIN_WEIGHTS:
            delta_w[n], new_m[n], new_v[n] = _adamw(weights[n], grad_w[n], given["m_" + n], given["v_" + n])
    return (loss, grad_x, *[grad_w[n] for n in TWIN_WEIGHTS], *[delta_w[n] for n in TWIN_WEIGHTS],
            *[new_m[n] for n in TWIN_WEIGHTS], *[new_v[n] for n in TWIN_WEIGHTS])
```

```python
import functools

import jax
import jax.numpy as jnp
from jax import lax
from jax.experimental import pallas as pl
from jax.experimental.pallas import tpu as pltpu

F32 = jnp.float32
BF16 = jnp.bfloat16
WIRE = jnp.bfloat16

D_MODEL = 1024
HEADS = 8
HEAD_DIM = 128
CHUNK = 64
N_META = 16
LEAD = 48
TAIL = 64
QKV_DIM = 3072
D_FF = 2816
POOL_WIDTH = 512
POOL_WINDOWS = (2, 4, 8, 16)
DEPTH = 2
NORM_EPS = 1e-6
ADAM_LR, ADAM_B1, ADAM_B2, ADAM_EPS, ADAM_WD, ADAM_STEP = 0.001, 0.9, 0.999, 1e-08, 0.01, 10
VMEM_LIMIT_BYTES = 48 * 1024 * 1024


def _params(*sem):
    return pltpu.CompilerParams(dimension_semantics=sem if sem else None, vmem_limit_bytes=VMEM_LIMIT_BYTES)


def _tile(n, cap, mult):
    best = None
    for t in range(mult, min(n, cap) + 1, mult):
        if n % t == 0:
            best = t
    assert best is not None, (n, cap, mult)
    return best


def _silu(x):
    return x * jax.nn.sigmoid(x)


def _softplus(x):
    return jnp.maximum(x, 0.0) + jnp.log(1.0 + jnp.exp(-jnp.abs(x)))


def _dg(a, b, ca, cb, hi):
    dims = (((ca,), (cb,)), ((), ()))
    if hi:
        return lax.dot_general(a, b, dims, precision=lax.Precision.HIGHEST, preferred_element_type=F32)
    return lax.dot_general(a.astype(BF16), b.astype(BF16), dims, preferred_element_type=F32)


def _make_dots(hi):
    @jax.custom_vjp
    def nn(a, b):
        return _dg(a, b, 1, 0, hi)

    @jax.custom_vjp
    def nt(a, b):
        return _dg(a, b, 1, 1, hi)

    @jax.custom_vjp
    def tn(a, b):
        return _dg(a, b, 0, 0, hi)

    nn.defvjp(lambda a, b: (nn(a, b), (a, b)), lambda r, g: (nt(g, r[1]), tn(r[0], g)))
    nt.defvjp(lambda a, b: (nt(a, b), (a, b)), lambda r, g: (nn(g, r[1]), tn(g, r[0])))
    tn.defvjp(lambda a, b: (tn(a, b), (a, b)), lambda r, g: (nt(r[1], g), nn(r[0], g)))
    return nn, nt, tn


_nn, _nt, _tn = _make_dots(False)
_hnn, _hnt, _htn = _make_dots(True)


def mm(a, b, *, ta=False, tb=False, add=None, out_dtype=F32, name):
    (kdim, m) = a.shape if ta else a.shape[::-1]
    (n, kb) = b.shape if tb else b.shape[::-1]
    assert kdim == kb, (a.shape, b.shape, ta, tb)
    tm = _tile(m, 1408, 128) if ta else _tile(m, 640, 64)
    tn = _tile(n, 1536, 128)
    tk = _tile(kdim, 1536 if not ta else 640, 64)
    nk = kdim // tk
    dims = (((0 if ta else 1,), (1 if tb else 0,)), ((), ()))

    def body(*refs):
        if add is not None:
            a_ref, b_ref, add_ref, o_ref, acc = refs
        else:
            a_ref, b_ref, o_ref, acc = refs
        k = pl.program_id(2)
        part = lax.dot_general(a_ref[...].astype(BF16), b_ref[...].astype(BF16), dims, preferred_element_type=F32)

        def finish(r):
            if add is not None:
                r = r + add_ref[...]
            o_ref[...] = r.astype(out_dtype)

        if nk == 1:
            finish(part)
        else:
            @pl.when(k == 0)
            def _():
                acc[...] = part

            @pl.when(jnp.logical_and(k > 0, k < nk - 1))
            def _():
                acc[...] += part

            @pl.when(k == nk - 1)
            def _():
                finish(acc[...] + part)

    a_spec = pl.BlockSpec((tk, tm), lambda i, j, k: (k, i)) if ta else pl.BlockSpec((tm, tk), lambda i, j, k: (i, k))
    b_spec = pl.BlockSpec((tn, tk), lambda i, j, k: (j, k)) if tb else pl.BlockSpec((tk, tn), lambda i, j, k: (k, j))
    in_specs = [a_spec, b_spec]
    args = [a, b]
    if add is not None:
        in_specs.append(pl.BlockSpec((tm, tn), lambda i, j, k: (i, j)))
        args.append(add)
    return pl.pallas_call(
        body, grid=(m // tm, n // tn, nk), in_specs=in_specs,
        out_specs=pl.BlockSpec((tm, tn), lambda i, j, k: (i, j)),
        out_shape=jax.ShapeDtypeStruct((m, n), out_dtype),
        scratch_shapes=[pltpu.VMEM((tm, tn) if nk > 1 else (8, 128), F32)],
        compiler_params=_params("parallel", "parallel", "arbitrary"), name=name,
    )(*args)


def _rms(x, gain):
    return x * lax.rsqrt(jnp.mean(x * x, axis=-1, keepdims=True) + NORM_EPS) * gain


def rms_fwd(h, gain, name):
    t = h.shape[0]
    ts = _tile(t, 640, 64)

    def body(h_ref, g_ref, u_ref):
        u_ref[...] = _rms(h_ref[...], g_ref[...]).astype(BF16)

    return pl.pallas_call(
        body, grid=(t // ts,),
        in_specs=[pl.BlockSpec((ts, D_MODEL), lambda i: (i, 0)), pl.BlockSpec((1, D_MODEL), lambda i: (0, 0))],
        out_specs=pl.BlockSpec((ts, D_MODEL), lambda i: (i, 0)),
        out_shape=jax.ShapeDtypeStruct((t, D_MODEL), BF16),
        compiler_params=_params("parallel"), name=name,
    )(h, gain)


def rms_bwd(h, gain, du, dres, name):
    t = h.shape[0]
    ts = _tile(t, 640, 64)

    def body(h_ref, g_ref, du_ref, dres_ref, dh_ref, dg_ref):
        i = pl.program_id(0)
        _, vjp = jax.vjp(_rms, h_ref[...], g_ref[...])
        dx, dg = vjp(du_ref[...])
        row = i * ts + lax.broadcasted_iota(jnp.int32, (ts, 1), 0)
        dh_ref[...] = jnp.where(row >= LEAD, dx + dres_ref[...], 0.0)

        @pl.when(i == 0)
        def _():
            dg_ref[...] = jnp.zeros_like(dg_ref)

        dg_ref[...] += dg

    row_spec = pl.BlockSpec((ts, D_MODEL), lambda i: (i, 0))
    vec_spec = pl.BlockSpec((1, D_MODEL), lambda i: (0, 0))
    return pl.pallas_call(
        body, grid=(t // ts,), in_specs=[row_spec, vec_spec, row_spec, row_spec],
        out_specs=(row_spec, vec_spec),
        out_shape=(jax.ShapeDtypeStruct((t, D_MODEL), F32), jax.ShapeDtypeStruct((1, D_MODEL), F32)),
        compiler_params=_params("arbitrary"), name=name,
    )(h, gain, du, dres)


def loss_head(h, gain, target, name):
    t = h.shape[0]
    ts = _tile(t, 640, 64)

    def body(h_ref, g_ref, t_ref, loss_ref, dh_ref, dg_ref):
        i = pl.program_id(0)
        row = i * ts + lax.broadcasted_iota(jnp.int32, (ts, 1), 0)
        keep = jnp.logical_and(row >= LEAD + N_META, row < t - TAIL)
        tgt = t_ref[...]

        def f(x, g):
            err = jnp.where(keep, _rms(x, g) - tgt, 0.0)
            per_row = jnp.mean(err * err, axis=-1, keepdims=True)
            return 0.5 * jnp.sum(per_row, axis=0, keepdims=True)

        val, vjp = jax.vjp(f, h_ref[...], g_ref[...])
        dx, dg = vjp(jnp.ones((1, 1), F32))
        dh_ref[...] = dx

        @pl.when(i == 0)
        def _():
            dg_ref[...] = jnp.zeros_like(dg_ref)
            loss_ref[...] = jnp.zeros_like(loss_ref)

        dg_ref[...] += dg
        loss_ref[...] += jnp.broadcast_to(val, (1, 128))

    row_spec = pl.BlockSpec((ts, D_MODEL), lambda i: (i, 0))
    vec_spec = pl.BlockSpec((1, D_MODEL), lambda i: (0, 0))
    return pl.pallas_call(
        body, grid=(t // ts,), in_specs=[row_spec, vec_spec, row_spec],
        out_specs=(pl.BlockSpec((1, 128), lambda i: (0, 0)), row_spec, vec_spec),
        out_shape=(jax.ShapeDtypeStruct((1, 128), F32), jax.ShapeDtypeStruct((t, D_MODEL), F32),
                   jax.ShapeDtypeStruct((1, D_MODEL), F32)),
        compiler_params=_params("arbitrary"), name=name,
    )(h, gain, target)


def conv_fwd(x, w, name):
    t, width = x.shape
    k = w.shape[0]
    ts = _tile(t, 640, 64)
    tw = _tile(width, 1536, 128)
    hb = ts // 8

    def body(x_ref, halo_ref, w_ref, o_ref, buf):
        i = pl.program_id(0)
        buf[0:8, :] = jnp.where(i > 0, halo_ref[...], 0.0)
        buf[8:, :] = x_ref[...]
        wv = w_ref[...]
        acc = buf[pl.ds(8 - (k - 1), ts), :] * wv[0:1, :]
        for j in range(1, k):
            acc = acc + buf[pl.ds(8 - (k - 1) + j, ts), :] * wv[j:j + 1, :]
        o_ref[...] = acc

    return pl.pallas_call(
        body, grid=(t // ts, width // tw),
        in_specs=[pl.BlockSpec((ts, tw), lambda i, j: (i, j)),
                  pl.BlockSpec((8, tw), lambda i, j: (jnp.maximum(i * hb - 1, 0), j)),
                  pl.BlockSpec((k, tw), lambda i, j: (0, j))],
        out_specs=pl.BlockSpec((ts, tw), lambda i, j: (i, j)),
        out_shape=jax.ShapeDtypeStruct((t, width), F32),
        scratch_shapes=[pltpu.VMEM((ts + 8, tw), F32)],
        compiler_params=_params("parallel", "parallel"), name=name,
    )(x, x, w)


def conv_bwd(x, dc, w, name):
    t, width = x.shape
    k = w.shape[0]
    ts = _tile(t, 640, 64)
    tw = _tile(width, 1536, 128)
    hb = ts // 8
    nt = t // ts

    def body(x_ref, xh_ref, dc_ref, dch_ref, w_ref, dx_ref, dw_ref, xbuf, dbuf):
        i = pl.program_id(1)
        xbuf[0:8, :] = jnp.where(i > 0, xh_ref[...], 0.0)
        xbuf[8:, :] = x_ref[...]
        d = dc_ref[...]
        dbuf[0:ts, :] = d
        dbuf[ts:, :] = jnp.where(i < nt - 1, dch_ref[...], 0.0)
        wv = w_ref[...]
        acc = dbuf[pl.ds(k - 1, ts), :] * wv[0:1, :]
        for j in range(1, k):
            acc = acc + dbuf[pl.ds(k - 1 - j, ts), :] * wv[j:j + 1, :]
        dx_ref[...] = acc

        @pl.when(i == 0)
        def _():
            dw_ref[...] = jnp.zeros_like(dw_ref)

        for j in range(k):
            dw_ref[j:j + 1, :] += jnp.sum(d * xbuf[pl.ds(8 - (k - 1) + j, ts), :], axis=0, keepdims=True)

    return pl.pallas_call(
        body, grid=(width // tw, nt),
        in_specs=[pl.BlockSpec((ts, tw), lambda j, i: (i, j)),
                  pl.BlockSpec((8, tw), lambda j, i: (jnp.maximum(i * hb - 1, 0), j)),
                  pl.BlockSpec((ts, tw), lambda j, i: (i, j)),
                  pl.BlockSpec((8, tw), lambda j, i: (jnp.minimum((i + 1) * hb, t // 8 - 1), j)),
                  pl.BlockSpec((k, tw), lambda j, i: (0, j))],
        out_specs=(pl.BlockSpec((ts, tw), lambda j, i: (i, j)), pl.BlockSpec((8, tw), lambda j, i: (0, j))),
        out_shape=(jax.ShapeDtypeStruct((t, width), F32), jax.ShapeDtypeStruct((8, width), F32)),
        scratch_shapes=[pltpu.VMEM((ts + 8, tw), F32), pltpu.VMEM((ts + 8, tw), F32)],
        compiler_params=_params("parallel", "arbitrary"), name=name,
    )(x, x, dc, dc, w)


def _pool_count(pos, win):
    return jnp.clip(pos + 1, 1, win).astype(F32)


def poolwin_fwd(p, name):
    t = p.shape[0]
    ts = _tile(t, 640, 64)
    hb = ts // 16

    def body(p_ref, halo_ref, o_ref, buf):
        i = pl.program_id(0)
        buf[0:16, :] = jnp.where(i > 0, halo_ref[...], 0.0)
        buf[16:, :] = p_ref[...]
        pos = i * ts + lax.broadcasted_iota(jnp.int32, (ts, 1), 0) - LEAD
        for gi, win in enumerate(POOL_WINDOWS):
            cols = slice(gi * 128, (gi + 1) * 128)
            own = buf[pl.ds(16, ts), cols]
            acc = own
            for j in range(1, win):
                acc = acc + buf[pl.ds(16 - j, ts), cols]
            o_ref[:, cols] = acc / _pool_count(pos, win) - own

    return pl.pallas_call(
        body, grid=(t // ts,),
        in_specs=[pl.BlockSpec((ts, POOL_WIDTH), lambda i: (i, 0)),
                  pl.BlockSpec((16, POOL_WIDTH), lambda i: (jnp.maximum(i * hb - 1, 0), 0))],
        out_specs=pl.BlockSpec((ts, POOL_WIDTH), lambda i: (i, 0)),
        out_shape=jax.ShapeDtypeStruct((t, POOL_WIDTH), F32),
        scratch_shapes=[pltpu.VMEM((ts + 16, POOL_WIDTH), F32)],
        compiler_params=_params("parallel"), name=name,
    )(p, p)


def poolwin_bwd(dpooled, name):
    t = dpooled.shape[0]
    ts = _tile(t, 640, 64)
    hb = ts // 16
    nt = t // ts

    def body(d_ref, halo_ref, o_ref, buf):
        i = pl.program_id(0)
        buf[0:ts, :] = d_ref[...]
        buf[ts:, :] = jnp.where(i < nt - 1, halo_ref[...], 0.0)
        pos = i * ts + lax.broadcasted_iota(jnp.int32, (ts, 1), 0) - LEAD
        for gi, win in enumerate(POOL_WINDOWS):
            cols = slice(gi * 128, (gi + 1) * 128)
            own = buf[pl.ds(0, ts), cols]
            acc = own / _pool_count(pos, win)
            for j in range(1, win):
                acc = acc + buf[pl.ds(j, ts), cols] / _pool_count(pos + j, win)
            o_ref[:, cols] = acc - own

    return pl.pallas_call(
        body, grid=(nt,),
        in_specs=[pl.BlockSpec((ts, POOL_WIDTH), lambda i: (i, 0)),
                  pl.BlockSpec((16, POOL_WIDTH), lambda i: (jnp.minimum((i + 1) * hb, t // 16 - 1), 0))],
        out_specs=pl.BlockSpec((ts, POOL_WIDTH), lambda i: (i, 0)),
        out_shape=jax.ShapeDtypeStruct((t, POOL_WIDTH), F32),
        scratch_shapes=[pltpu.VMEM((ts + 16, POOL_WIDTH), F32)],
        compiler_params=_params("parallel"), name=name,
    )(dpooled, dpooled)


def _mix(y_a, gpre, pooled, w_pool, scale):
    parts = [_nn(pooled[:, g * 128:(g + 1) * 128], w_pool[g]) for g in range(4)]
    y_b = jnp.concatenate(parts, axis=1) * scale
    return jax.nn.sigmoid(gpre[:, :D_MODEL]) * y_a + jax.nn.sigmoid(gpre[:, D_MODEL:]) * y_b


def _mix_specs(ts):
    return [pl.BlockSpec((ts, D_MODEL), lambda i: (i, 0)), pl.BlockSpec((ts, 2 * D_MODEL), lambda i: (i, 0)),
            pl.BlockSpec((ts, POOL_WIDTH), lambda i: (i, 0)), pl.BlockSpec((4, 128, 256), lambda i: (0, 0, 0)),
            pl.BlockSpec((1, D_MODEL), lambda i: (0, 0))]


def mix_fwd(y_a, gpre, pooled, w_pool, scale, name):
    t = y_a.shape[0]
    ts = _tile(t, 320, 64)

    def body(ya_ref, g_ref, p_ref, w_ref, s_ref, o_ref):
        o_ref[...] = _mix(ya_ref[...], g_ref[...], p_ref[...], w_ref[...], s_ref[...]).astype(BF16)

    return pl.pallas_call(
        body, grid=(t // ts,), in_specs=_mix_specs(ts),
        out_specs=pl.BlockSpec((ts, D_MODEL), lambda i: (i, 0)),
        out_shape=jax.ShapeDtypeStruct((t, D_MODEL), BF16),
        compiler_params=_params("parallel"), name=name,
    )(y_a, gpre, pooled, w_pool, scale)


def mix_bwd(y_a, gpre, pooled, w_pool, scale, dy, name):
    t = y_a.shape[0]
    ts = _tile(t, 320, 64)

    def body(ya_ref, g_ref, p_ref, w_ref, s_ref, dy_ref, dya_ref, dg_ref, dp_ref, dw_ref, ds_ref):
        i = pl.program_id(0)
        _, vjp = jax.vjp(_mix, ya_ref[...], g_ref[...], p_ref[...], w_ref[...], s_ref[...])
        dya, dg, dp, dw, ds = vjp(dy_ref[...])
        dya_ref[...] = dya
        dg_ref[...] = dg
        dp_ref[...] = dp

        @pl.when(i == 0)
        def _():
            dw_ref[...] = jnp.zeros_like(dw_ref)
            ds_ref[...] = jnp.zeros_like(ds_ref)

        dw_ref[...] += dw
        ds_ref[...] += ds

    specs = _mix_specs(ts)
    return pl.pallas_call(
        body, grid=(t // ts,), in_specs=specs + [specs[0]],
        out_specs=(specs[0], specs[1], specs[2], specs[3], specs[4]),
        out_shape=(jax.ShapeDtypeStruct((t, D_MODEL), F32), jax.ShapeDtypeStruct((t, 2 * D_MODEL), F32),
                   jax.ShapeDtypeStruct((t, POOL_WIDTH), F32), jax.ShapeDtypeStruct((4, 128, 256), F32),
                   jax.ShapeDtypeStruct((1, D_MODEL), F32)),
        compiler_params=_params("arbitrary"), name=name,
    )(y_a, gpre, pooled, w_pool, scale, dy)


def _ffn_act(cg, cv):
    return _silu(cg) * cv


def ffnact_fwd(cg, cv, name):
    t, width = cg.shape
    ts = _tile(t, 640, 64)
    tw = _tile(width, 1536, 128)
    spec = pl.BlockSpec((ts, tw), lambda i, j: (i, j))

    def body(g_ref, v_ref, o_ref):
        o_ref[...] = _ffn_act(g_ref[...], v_ref[...]).astype(BF16)

    return pl.pallas_call(
        body, grid=(t // ts, width // tw), in_specs=[spec, spec], out_specs=spec,
        out_shape=jax.ShapeDtypeStruct((t, width), BF16),
        compiler_params=_params("parallel", "parallel"), name=name,
    )(cg, cv)


def ffnact_bwd(cg, cv, dact, name):
    t, width = cg.shape
    ts = _tile(t, 640, 64)
    tw = _tile(width, 1536, 128)
    spec = pl.BlockSpec((ts, tw), lambda i, j: (i, j))

    def body(g_ref, v_ref, d_ref, dg_ref, dv_ref):
        _, vjp = jax.vjp(_ffn_act, g_ref[...], v_ref[...])
        dg_ref[...], dv_ref[...] = vjp(d_ref[...])

    return pl.pallas_call(
        body, grid=(t // ts, width // tw), in_specs=[spec, spec, spec], out_specs=(spec, spec),
        out_shape=(jax.ShapeDtypeStruct((t, width), F32), jax.ShapeDtypeStruct((t, width), F32)),
        compiler_params=_params("parallel", "parallel"), name=name,
    )(cg, cv, dact)


def _gdn_chunk(c, z, ba, pa, pdt, hn, s, *, valid):
    r = lax.broadcasted_iota(jnp.int32, (CHUNK, CHUNK), 0)
    q_ = lax.broadcasted_iota(jnp.int32, (CHUNK, CHUNK), 1)
    causal = r >= q_
    strict = r > q_
    tril = causal.astype(F32)
    triu = (r <= q_).astype(F32)
    eye = (r == q_).astype(F32)
    lane = lax.broadcasted_iota(jnp.int32, (CHUNK, 128), 1)

    decay_log = -jnp.exp(pa) * _softplus(ba + pdt)
    bg = jnp.where(lane < HEADS, jax.nn.sigmoid(ba), jnp.where(lane < 2 * HEADS, decay_log, 0.0))
    bg = jnp.where(valid, bg, 0.0)
    gc = _hnn(tril, bg)
    gct = _hnn(bg.T, triu)
    eg = jnp.exp(gc)
    glast = gc[CHUNK - 1:CHUNK, :]
    ekd = jnp.exp(glast - gc)
    gtot = jnp.exp(glast)

    ys, states = [], []
    for h in range(HEADS):
        hs = slice(h * HEAD_DIM, (h + 1) * HEAD_DIM)
        q = _silu(c[:, hs])
        k = _silu(c[:, D_MODEL + h * HEAD_DIM:D_MODEL + (h + 1) * HEAD_DIM])
        v = _silu(c[:, 2 * D_MODEL + h * HEAD_DIM:2 * D_MODEL + (h + 1) * HEAD_DIM])
        q = q * lax.rsqrt(jnp.sum(q * q, axis=-1, keepdims=True) + NORM_EPS) * (HEAD_DIM ** -0.5)
        k = k * lax.rsqrt(jnp.sum(k * k, axis=-1, keepdims=True) + NORM_EPS)
        beta = bg[:, h:h + 1]
        gl = slice(HEADS + h, HEADS + h + 1)
        decay = jnp.exp(jnp.where(causal, gc[:, gl] - gct[gl, :], -1e30))
        kb = k * beta
        a = jnp.where(strict, _nt(kb, k) * decay, 0.0)
        x = eye - a
        p = a
        for _ in range(5):
            p = _hnn(p, p)
            x = x + _hnn(x, p)
        u = _nn(x, v * beta)
        w = _nn(x, kb * eg[:, gl])
        qk = jnp.where(causal, _nt(q, k) * decay, 0.0)
        sh = s[h]
        v_new = u - _nn(w, sh)
        o = _nn(q * eg[:, gl], sh) + _nn(qk, v_new)
        states.append(sh * gtot[:, gl] + _tn(k * ekd[:, gl], v_new))
        o = o * lax.rsqrt(jnp.mean(o * o, axis=-1, keepdims=True) + NORM_EPS) * hn * _silu(z[:, hs])
        ys.append(o)
    return jnp.concatenate(ys, axis=1), tuple(states)


def _chunk_valid(n, t):
    row = n * CHUNK + lax.broadcasted_iota(jnp.int32, (CHUNK, 1), 0)
    return jnp.logical_and(row >= LEAD, row < t - TAIL)


def gdn_fwd(c, z, ba, pa, pdt, hn, name):
    t = c.shape[0]
    n_chunks = t // CHUNK

    def body(c_ref, z_ref, ba_ref, pa_ref, pdt_ref, hn_ref, y_ref, ss_ref, state):
        n = pl.program_id(0)

        @pl.when(n == 0)
        def _():
            state[...] = jnp.zeros_like(state)

        s0 = tuple(state[h] for h in range(HEADS))
        for h in range(HEADS):
            ss_ref[0, h] = s0[h]
        y, s1 = _gdn_chunk(c_ref[...], z_ref[...], ba_ref[...], pa_ref[...], pdt_ref[...], hn_ref[...], s0,
                           valid=_chunk_valid(n, t))
        y_ref[...] = y
        for h in range(HEADS):
            state[h] = s1[h]

    vec = pl.BlockSpec((1, 128), lambda n: (0, 0))
    return pl.pallas_call(
        body, grid=(n_chunks,),
        in_specs=[pl.BlockSpec((CHUNK, QKV_DIM), lambda n: (n, 0)), pl.BlockSpec((CHUNK, D_MODEL), lambda n: (n, 0)),
                  pl.BlockSpec((CHUNK, 128), lambda n: (n, 0)), vec, vec, vec],
        out_specs=(pl.BlockSpec((CHUNK, D_MODEL), lambda n: (n, 0)),
                   pl.BlockSpec((1, HEADS, HEAD_DIM, HEAD_DIM), lambda n: (n, 0, 0, 0))),
        out_shape=(jax.ShapeDtypeStruct((t, D_MODEL), F32),
                   jax.ShapeDtypeStruct((n_chunks, HEADS, HEAD_DIM, HEAD_DIM), F32)),
        scratch_shapes=[pltpu.VMEM((HEADS, HEAD_DIM, HEAD_DIM), F32)],
        compiler_params=_params("arbitrary"), name=name,
    )(c, z, ba, pa, pdt, hn)


def gdn_bwd(c, z, ba, pa, pdt, hn, starts, dy, name):
    t = c.shape[0]
    n_chunks = t // CHUNK

    def body(c_ref, z_ref, ba_ref, pa_ref, pdt_ref, hn_ref, ss_ref, dy_ref,
             dc_ref, dz_ref, dba_ref, dpa_ref, dpdt_ref, dhn_ref, dstate):
        step = pl.program_id(0)
        n = n_chunks - 1 - step

        @pl.when(step == 0)
        def _():
            dstate[...] = jnp.zeros_like(dstate)
            dpa_ref[...] = jnp.zeros_like(dpa_ref)
            dpdt_ref[...] = jnp.zeros_like(dpdt_ref)
            dhn_ref[...] = jnp.zeros_like(dhn_ref)

        f = functools.partial(_gdn_chunk, valid=_chunk_valid(n, t))
        _, vjp = jax.vjp(f, c_ref[...], z_ref[...], ba_ref[...], pa_ref[...], pdt_ref[...], hn_ref[...],
                         tuple(ss_ref[0, h] for h in range(HEADS)))
        dc, dz, dba, dpa, dpdt, dhn, ds = vjp((dy_ref[...], tuple(dstate[h] for h in range(HEADS))))
        dc_ref[...] = dc
        dz_ref[...] = dz
        dba_ref[...] = dba
        dpa_ref[...] += dpa
        dpdt_ref[...] += dpdt
        dhn_ref[...] += dhn
        for h in range(HEADS):
            dstate[h] = ds[h]

    def rev(width):
        return pl.BlockSpec((CHUNK, width), lambda s: (n_chunks - 1 - s, 0))

    vec = pl.BlockSpec((1, 128), lambda s: (0, 0))
    vec_shape = jax.ShapeDtypeStruct((1, 128), F32)
    return pl.pallas_call(
        body, grid=(n_chunks,),
        in_specs=[rev(QKV_DIM), rev(D_MODEL), rev(128), vec, vec, vec,
                  pl.BlockSpec((1, HEADS, HEAD_DIM, HEAD_DIM), lambda s: (n_chunks - 1 - s, 0, 0, 0)), rev(D_MODEL)],
        out_specs=(rev(QKV_DIM), rev(D_MODEL), rev(128), vec, vec, vec),
        out_shape=(jax.ShapeDtypeStruct((t, QKV_DIM), F32), jax.ShapeDtypeStruct((t, D_MODEL), F32),
                   jax.ShapeDtypeStruct((t, 128), F32), vec_shape, vec_shape, vec_shape),
        scratch_shapes=[pltpu.VMEM((HEADS, HEAD_DIM, HEAD_DIM), F32)],
        compiler_params=_params("arbitrary"), name=name,
    )(c, z, ba, pa, pdt, hn, starts, dy)


def _layer_fwd(h, w, tag):
    u = rms_fwd(h, w["norm_mix"], f"{tag}_rms_mix")
    pq = mm(u, w["wqkv"], name=f"{tag}_mm_qkv")
    pz = mm(u, w["wz"], name=f"{tag}_mm_z")
    pg = mm(u, w["wg"], name=f"{tag}_mm_gate")
    pp = mm(u, w["wpl"], name=f"{tag}_mm_pool")
    pba = mm(u, w["wba"], name=f"{tag}_mm_ba")
    cq = conv_fwd(pq, w["conv_qkv"], f"{tag}_conv_qkv")
    ya, starts = gdn_fwd(cq, pz, pba, w["pa"], w["pdt"], w["head_norm"], f"{tag}_gdn")
    pooled = poolwin_fwd(pp, f"{tag}_poolwin")
    y = mix_fwd(ya, pg, pooled, w["w_pool"], w["pool_scale"], f"{tag}_mix")
    h1 = mm(y, w["wout"], add=h, name=f"{tag}_mm_out")
    u2 = rms_fwd(h1, w["norm_ffn"], f"{tag}_rms_ffn")
    hg = mm(u2, w["wupg"], name=f"{tag}_mm_upg")
    hv = mm(u2, w["wupv"], name=f"{tag}_mm_upv")
    cg = conv_fwd(hg, w["conv_g"], f"{tag}_conv_g")
    cv = conv_fwd(hv, w["conv_v"], f"{tag}_conv_v")
    act = ffnact_fwd(cg, cv, f"{tag}_act")
    h2 = mm(act, w["wdown"], add=h1, name=f"{tag}_mm_down")
    saved = dict(h=h, u=u, pq=pq, pz=pz, pg=pg, pba=pba, cq=cq, ya=ya, starts=starts, pooled=pooled, y=y, h1=h1,
                 u2=u2, hg=hg, hv=hv, cg=cg, cv=cv, act=act)
    return h2, saved


def _layer_bwd(dh2, w, s, tag):
    g = {}
    dact = mm(dh2, w["wdown"], tb=True, name=f"{tag}_bmm_down_x")
    g["wdown"] = mm(s["act"], dh2, ta=True, name=f"{tag}_bmm_down_w")
    dcg, dcv = ffnact_bwd(s["cg"], s["cv"], dact, f"{tag}_act_b")
    dhg, g["conv_g"] = conv_bwd(s["hg"], dcg, w["conv_g"], f"{tag}_conv_g_b")
    dhv, g["conv_v"] = conv_bwd(s["hv"], dcv, w["conv_v"], f"{tag}_conv_v_b")
    du2 = mm(dhg, w["wupg"], tb=True, name=f"{tag}_bmm_upg_x")
    du2 = mm(dhv, w["wupv"], tb=True, add=du2, name=f"{tag}_bmm_upv_x")
    g["wupg"] = mm(s["u2"], dhg, ta=True, name=f"{tag}_bmm_upg_w")
    g["wupv"] = mm(s["u2"], dhv, ta=True, name=f"{tag}_bmm_upv_w")
    dh1, g["norm_ffn"] = rms_bwd(s["h1"], w["norm_ffn"], du2, dh2, f"{tag}_rms_ffn_b")
    dy = mm(dh1, w["wout"], tb=True, name=f"{tag}_bmm_out_x")
    g["wout"] = mm(s["y"], dh1, ta=True, name=f"{tag}_bmm_out_w")
    dya, dpg, dpooled, g["w_pool"], g["pool_scale"] = mix_bwd(
        s["ya"], s["pg"], s["pooled"], w["w_pool"], w["pool_scale"], dy, f"{tag}_mix_b")
    dpp = poolwin_bwd(dpooled, f"{tag}_poolwin_b")
    dcq, dpz, dpba, g["pa"], g["pdt"], g["head_norm"] = gdn_bwd(
        s["cq"], s["pz"], s["pba"], w["pa"], w["pdt"], w["head_norm"], s["starts"], dya, f"{tag}_gdn_b")
    dpq, g["conv_qkv"] = conv_bwd(s["pq"], dcq, w["conv_qkv"], f"{tag}_conv_qkv_b")
    du = mm(dpq, w["wqkv"], tb=True, name=f"{tag}_bmm_qkv_x")
    du = mm(dpz, w["wz"], tb=True, add=du, name=f"{tag}_bmm_z_x")
    du = mm(dpg, w["wg"], tb=True, add=du, name=f"{tag}_bmm_gate_x")
    du = mm(dpp, w["wpl"], tb=True, add=du, name=f"{tag}_bmm_pool_x")
    du = mm(dpba, w["wba"], tb=True, add=du, name=f"{tag}_bmm_ba_x")
    g["wqkv"] = mm(s["u"], dpq, ta=True, name=f"{tag}_bmm_qkv_w")
    g["wz"] = mm(s["u"], dpz, ta=True, name=f"{tag}_bmm_z_w")
    g["wg"] = mm(s["u"], dpg, ta=True, name=f"{tag}_bmm_gate_w")
    g["wpl"] = mm(s["u"], dpp, ta=True, name=f"{tag}_bmm_pool_w")
    g["wba"] = mm(s["u"], dpba, ta=True, name=f"{tag}_bmm_ba_w")
    dh, g["norm_mix"] = rms_bwd(s["h"], w["norm_mix"], du, dh1, f"{tag}_rms_mix_b")
    return dh, g


def local_step(h0, target, layers, norm_final):
    h = h0
    saved = []
    for li, w in enumerate(layers):
        h, s = _layer_fwd(h, w, f"l{li}")
        saved.append(s)
    loss, dh, dnf = loss_head(h, norm_final, target, "loss_head")
    grads = [None] * len(layers)
    for li in reversed(range(len(layers))):
        dh, grads[li] = _layer_bwd(dh, layers[li], saved[li], f"l{li}")
    return loss, dh, grads, dnf


_Z0, _B0, _P0, _G0, _IN_DIM = 3072, 4096, 4112, 4624, 6672


def _lanes_8_to_15(v):
    return jnp.pad(v.reshape(1, HEADS).astype(F32), ((0, 0), (HEADS, 128 - 2 * HEADS)))


def prep_layer(p):
    w_in = p["w_in"].astype(BF16)
    w_up = p["w_up"].astype(BF16)
    row = lambda v: v.reshape(1, -1).astype(F32)
    return dict(
        wqkv=w_in[:, :_Z0], wz=w_in[:, _Z0:_B0], wba=jnp.pad(w_in[:, _B0:_P0], ((0, 0), (0, 128 - 2 * HEADS))),
        wpl=w_in[:, _P0:_G0], wg=w_in[:, _G0:], wout=p["w_out"].astype(BF16),
        wupg=w_up[:, :D_FF], wupv=w_up[:, D_FF:], wdown=p["w_down"].astype(BF16),
        conv_qkv=p["conv_qkv"].astype(F32), conv_g=p["conv_ffn"][:, :D_FF].astype(F32),
        conv_v=p["conv_ffn"][:, D_FF:].astype(F32), w_pool=p["w_pool"].astype(F32),
        pool_scale=row(p["pool_scale"]), head_norm=row(p["head_norm"]), norm_mix=row(p["norm_mix"]),
        norm_ffn=row(p["norm_ffn"]), pa=_lanes_8_to_15(p["a_log"]), pdt=_lanes_8_to_15(p["dt_bias"]))


def layer_grads(g):
    return dict(
        w_in=jnp.concatenate([g["wqkv"], g["wz"], g["wba"][:, :2 * HEADS], g["wpl"], g["wg"]], axis=1),
        conv_qkv=g["conv_qkv"][:4], a_log=g["pa"][0, HEADS:2 * HEADS], dt_bias=g["pdt"][0, HEADS:2 * HEADS],
        head_norm=g["head_norm"][0], w_pool=g["w_pool"], pool_scale=g["pool_scale"][0], w_out=g["wout"],
        norm_mix=g["norm_mix"][0], norm_ffn=g["norm_ffn"][0],
        w_up=jnp.concatenate([g["wupg"], g["wupv"]], axis=1),
        conv_ffn=jnp.concatenate([g["conv_g"][:3], g["conv_v"][:3]], axis=1), w_down=g["wdown"])


LAYER_PARAMS = ("norm_mix", "w_in", "conv_qkv", "a_log", "dt_bias", "head_norm", "w_pool", "pool_scale", "w_out",
                "norm_ffn", "w_up", "conv_ffn", "w_down")


def pad_rows(meta, x):
    return jnp.concatenate([jnp.zeros((LEAD, D_MODEL), F32), meta.astype(F32), x.astype(F32),
                            jnp.zeros((TAIL, D_MODEL), F32)], axis=0)


MESH = pl.DeviceIdType.MESH
ANY = pl.BlockSpec(memory_space=pl.ANY)


def _place():
    x, y, c = lax.axis_index("x"), lax.axis_index("y"), lax.axis_index("c")
    return x, y, c, [(1 - x, y), (x, 1 - y), (1 - x, 1 - y)]


def gather_shards(pack):
    _, rows, lanes = pack.shape

    def body(p_ref, o_ref, send_sems, recv_sems, local_sem):
        x, y, c, chips = _place()
        me = 2 * x + y

        def copy(k, chip, half, to, src=None):
            dst = o_ref.at[2 * chip[0] + chip[1], half]
            return pltpu.make_async_remote_copy(src_ref=dst if src is None else src, dst_ref=dst,
                                                send_sem=send_sems.at[k], recv_sem=recv_sems.at[k],
                                                device_id=to, device_id_type=MESH)

        own = pltpu.make_async_copy(p_ref, o_ref.at[me], local_sem)
        own.start()
        first = [copy(j, (x, y), c, (*chip, c), src=p_ref.at[c]) for j, chip in enumerate(chips)]
        for cp in first:
            cp.start()
        passed = [copy(3 + j, chip, c, (x, y, 1 - c)) for j, chip in enumerate(chips)]
        for j, chip in enumerate(chips):
            copy(j, chip, c, (x, y, c)).wait_recv()
            passed[j].start()
        for j, chip in enumerate(chips):
            copy(3 + j, chip, 1 - c, (x, y, c)).wait_recv()
        for cp in first + passed:
            cp.wait_send()
        own.wait()

    return pl.pallas_call(
        body, in_specs=[ANY], out_specs=ANY,
        out_shape=jax.ShapeDtypeStruct((4, 2, rows, lanes), pack.dtype),
        scratch_shapes=[pltpu.SemaphoreType.DMA((6,)), pltpu.SemaphoreType.DMA((6,)), pltpu.SemaphoreType.DMA],
        name="gather_shards",
    )(pack)


def swap_other_halves(p):
    _, _, rows, lanes = p.shape

    def body(p_ref, o_ref, send_sems, recv_sems):
        x, y, c, _ = _place()
        copies = [pltpu.make_async_remote_copy(src_ref=p_ref.at[s, 1 - c], dst_ref=o_ref.at[s],
                                               send_sem=send_sems.at[s], recv_sem=recv_sems.at[s],
                                               device_id=(x, y, 1 - c), device_id_type=MESH) for s in range(4)]
        for cp in copies:
            cp.start()
        for cp in copies:
            cp.wait()

    return pl.pallas_call(
        body, in_specs=[ANY], out_specs=ANY, out_shape=jax.ShapeDtypeStruct((4, rows, lanes), p.dtype),
        scratch_shapes=[pltpu.SemaphoreType.DMA((4,)), pltpu.SemaphoreType.DMA((4,))], name="swap_other_halves",
    )(p)


def scatter_to_chips(q):
    _, rows, lanes = q.shape

    def body(q_ref, o_ref, send_sems, recv_sems, local_sem):
        x, y, c, chips = _place()
        me = 2 * x + y
        own = pltpu.make_async_copy(q_ref.at[me], o_ref.at[me], local_sem)
        own.start()
        copies = [pltpu.make_async_remote_copy(src_ref=q_ref.at[2 * chip[0] + chip[1]], dst_ref=o_ref.at[me],
                                               send_sem=send_sems.at[j], recv_sem=recv_sems.at[j],
                                               device_id=(*chip, c), device_id_type=MESH)
                  for j, chip in enumerate(chips)]
        for cp in copies:
            cp.start()
        for j, chip in enumerate(chips):
            slot = o_ref.at[2 * chip[0] + chip[1]]
            pltpu.make_async_remote_copy(src_ref=slot, dst_ref=slot, send_sem=send_sems.at[j], recv_sem=recv_sems.at[j],
                                         device_id=(x, y, c), device_id_type=MESH).wait_recv()
        for cp in copies:
            cp.wait_send()
        own.wait()

    return pl.pallas_call(
        body, in_specs=[ANY], out_specs=ANY, out_shape=jax.ShapeDtypeStruct((4, rows, lanes), q.dtype),
        scratch_shapes=[pltpu.SemaphoreType.DMA((3,)), pltpu.SemaphoreType.DMA((3,)), pltpu.SemaphoreType.DMA],
        name="scatter_to_chips",
    )(q)


def join_halves(g):
    rows, lanes = g.shape

    def body(g_ref, o_ref, send_sem, recv_sem, local_sem):
        x, y, c, _ = _place()
        own = pltpu.make_async_copy(g_ref, o_ref.at[c], local_sem)
        own.start()
        cp = pltpu.make_async_remote_copy(src_ref=g_ref, dst_ref=o_ref.at[c], send_sem=send_sem, recv_sem=recv_sem,
                                          device_id=(x, y, 1 - c), device_id_type=MESH)
        cp.start()
        other = o_ref.at[1 - c]
        pltpu.make_async_remote_copy(src_ref=other, dst_ref=other, send_sem=send_sem, recv_sem=recv_sem,
                                     device_id=(x, y, c), device_id_type=MESH).wait_recv()
        cp.wait_send()
        own.wait()

    return pl.pallas_call(
        body, in_specs=[ANY], out_specs=ANY, out_shape=jax.ShapeDtypeStruct((2, rows, lanes), g.dtype),
        scratch_shapes=[pltpu.SemaphoreType.DMA, pltpu.SemaphoreType.DMA, pltpu.SemaphoreType.DMA], name="join_halves",
    )(g)


def add_own_half(p, other, c):
    _, _, rows, lanes = p.shape
    tr = _tile(rows, 4096, 8)

    def body(c_ref, p_ref, o_ref, out_ref):
        out_ref[...] = p_ref[...] + o_ref[...]

    return pl.pallas_call(
        body,
        grid_spec=pltpu.PrefetchScalarGridSpec(
            num_scalar_prefetch=1, grid=(4, rows // tr),
            in_specs=[pl.BlockSpec((None, None, tr, lanes), lambda s, i, c_ref: (s, c_ref[0], i, 0)),
                      pl.BlockSpec((None, tr, lanes), lambda s, i, c_ref: (s, i, 0))],
            out_specs=pl.BlockSpec((None, tr, lanes), lambda s, i, c_ref: (s, i, 0))),
        out_shape=jax.ShapeDtypeStruct((4, rows, lanes), F32),
        compiler_params=_params("parallel", "parallel"), name="add_own_half",
    )(c, p, other)


def sum_chips(b):
    _, rows, lanes = b.shape
    tr = _tile(rows, 4096, 8)

    def body(b_ref, out_ref):
        out_ref[...] = ((b_ref[0] + b_ref[1]) + b_ref[2]) + b_ref[3]

    return pl.pallas_call(
        body, grid=(rows // tr,), in_specs=[pl.BlockSpec((4, tr, lanes), lambda i: (0, i, 0))],
        out_specs=pl.BlockSpec((tr, lanes), lambda i: (i, 0)), out_shape=jax.ShapeDtypeStruct((rows, lanes), F32),
        compiler_params=_params("parallel"), name="sum_chips",
    )(b)


def all_reduce_to_shards(p, c):
    q = add_own_half(p, swap_other_halves(p), c)
    g = sum_chips(scatter_to_chips(q))
    both = join_halves(g)
    return both.reshape(2 * both.shape[1], both.shape[2])


def adamw(w, g, m, v, name):
    shape = w.shape
    cols = shape[-1]
    w2, g2, m2, v2 = (a.reshape(-1, cols) for a in (w, g, m, v))
    rows = w2.shape[0]
    tr = _tile(rows, max(8, 262144 // cols), 8) if rows % 8 == 0 else rows
    c1 = 1.0 - ADAM_B1 ** ADAM_STEP
    c2 = 1.0 - ADAM_B2 ** ADAM_STEP

    def body(w_ref, g_ref, m_ref, v_ref, d_ref, mo_ref, vo_ref):
        gv = g_ref[...]
        mn = ADAM_B1 * m_ref[...] + (1.0 - ADAM_B1) * gv
        vn = ADAM_B2 * v_ref[...] + (1.0 - ADAM_B2) * jnp.square(gv)
        d_ref[...] = -ADAM_LR * ((mn / c1) / (jnp.sqrt(vn / c2) + ADAM_EPS) + ADAM_WD * w_ref[...])
        mo_ref[...] = mn
        vo_ref[...] = vn

    spec = pl.BlockSpec((tr, cols), lambda i: (i, 0))
    out = jax.ShapeDtypeStruct((rows, cols), F32)
    d, mn, vn = pl.pallas_call(
        body, grid=(rows // tr,), in_specs=[spec] * 4, out_specs=(spec,) * 3, out_shape=(out,) * 3,
        compiler_params=_params("parallel"), name=name,
    )(w2, g2, m2, v2)
    return d.reshape(shape), mn.reshape(shape), vn.reshape(shape)


SHARDED = ("w_in", "w_up", "w_out", "w_down", "w_pool", "conv_qkv", "conv_ffn", "meta_tokens")
MATMUL_WEIGHTS = ("w_in", "w_up", "w_out", "w_down", "w_pool")
REPLICATED = ("norm_mix", "a_log", "dt_bias", "head_norm", "pool_scale", "norm_ffn", "norm_final")
SHARD_AXIS = {"w_in": 2, "w_up": 2, "w_out": 1, "w_down": 1, "w_pool": 3, "conv_qkv": 2, "conv_ffn": 2, "meta_tokens": 1}


def _rows_of(a):
    return a.reshape(-1, 128)


SEGMENT_ROWS = 16


def _segment(n_rows):
    return -(-n_rows // SEGMENT_ROWS) * SEGMENT_ROWS


def _pad_segment(a):
    pad = [(0, 0)] * a.ndim
    pad[-2] = (0, _segment(a.shape[-2]) - a.shape[-2])
    return jnp.pad(a, pad)


def _unshard(stacked, axis):
    full = jnp.moveaxis(stacked, 0, axis)
    shape = list(full.shape)
    shape[axis:axis + 2] = [shape[axis] * shape[axis + 1]]
    return full.reshape(shape)


def _shard_stack(full, axis):
    shape = list(full.shape)
    shape[axis:axis + 1] = [4, shape[axis] // 4]
    return jnp.moveaxis(full.reshape(shape), axis, 0)


def pack_weights(shards):
    parts = [_rows_of(shards[k].astype(WIRE)) for k in MATMUL_WEIGHTS]
    parts += [lax.bitcast_convert_type(_rows_of(shards[k].astype(F32)), WIRE).reshape(-1, 128)
              for k in SHARDED if k not in MATMUL_WEIGHTS]
    parts = [_pad_segment(p) for p in parts]
    rows = sum(p.shape[0] for p in parts)
    if rows % (2 * SEGMENT_ROWS):
        parts.append(jnp.zeros((SEGMENT_ROWS, 128), WIRE))
        rows += SEGMENT_ROWS
    return jnp.concatenate(parts, axis=0).reshape(2, rows // 2, 128)


def unpack_weights(gathered, shard_shapes):
    flat = gathered.reshape(4, -1, 128)
    out, at = {}, 0
    for k in SHARDED:
        shp = shard_shapes[k]
        n = 1
        for e in shp:
            n *= e
        if k in MATMUL_WEIGHTS:
            r = n // 128
            stacked = flat[:, at:at + r].reshape((4,) + tuple(shp))
        else:
            r = 2 * n // 128
            stacked = lax.bitcast_convert_type(flat[:, at:at + r].reshape(4, n // 128, 128, 2), F32).reshape((4,) + tuple(shp))
        out[k] = _unshard(stacked, SHARD_AXIS[k])
        at += _segment(r)
    return out


def pack_grads(full, repl):
    parts = [_shard_stack(full[k], SHARD_AXIS[k]).reshape(4, -1, 128) for k in SHARDED]
    r = jnp.concatenate([repl[k].reshape(-1) for k in REPLICATED])
    r = jnp.pad(r, (0, -r.shape[0] % 128)).reshape(1, -1, 128)
    parts.append(jnp.broadcast_to(r, (4,) + r.shape[1:]))
    parts = [_pad_segment(p) for p in parts]
    rows = sum(p.shape[1] for p in parts)
    if rows % (2 * SEGMENT_ROWS):
        parts.append(jnp.zeros((4, SEGMENT_ROWS, 128), F32))
        rows += SEGMENT_ROWS
    return jnp.concatenate(parts, axis=1).reshape(4, 2, rows // 2, 128)


def unpack_grads(flat, shard_shapes, repl_shapes):
    out, at = {}, 0
    for k in SHARDED:
        n = 1
        for e in shard_shapes[k]:
            n *= e
        out[k] = flat[at:at + n // 128].reshape(shard_shapes[k])
        at += _segment(n // 128)
    r = flat[at:].reshape(-1)
    at = 0
    for k in REPLICATED:
        n = 1
        for e in repl_shapes[k]:
            n *= e
        out[k] = r[at:at + n].reshape(repl_shapes[k])
        at += n
    return out


WEIGHT_ORDER = ("meta_tokens", "norm_mix", "w_in", "conv_qkv", "a_log", "dt_bias", "head_norm", "w_pool", "pool_scale",
                "w_out", "norm_ffn", "w_up", "conv_ffn", "w_down", "norm_final")


def kernel(x, meta_tokens, norm_mix, w_in, conv_qkv, a_log, dt_bias, head_norm, w_pool, pool_scale, w_out, norm_ffn, w_up, conv_ffn, w_down, norm_final, loss_target, m_meta_tokens, m_norm_mix, m_w_in, m_conv_qkv, m_a_log, m_dt_bias, m_head_norm, m_w_pool, m_pool_scale, m_w_out, m_norm_ffn, m_w_up, m_conv_ffn, m_w_down, m_norm_final, v_meta_tokens, v_norm_mix, v_w_in, v_conv_qkv, v_a_log, v_dt_bias, v_head_norm, v_w_pool, v_pool_scale, v_w_out, v_norm_ffn, v_w_up, v_conv_ffn, v_w_down, v_norm_final):
    weights = dict(meta_tokens=meta_tokens, norm_mix=norm_mix, w_in=w_in, conv_qkv=conv_qkv, a_log=a_log,
                   dt_bias=dt_bias, head_norm=head_norm, w_pool=w_pool, pool_scale=pool_scale, w_out=w_out,
                   norm_ffn=norm_ffn, w_up=w_up, conv_ffn=conv_ffn, w_down=w_down, norm_final=norm_final)
    m_in = dict(zip(WEIGHT_ORDER, (m_meta_tokens, m_norm_mix, m_w_in, m_conv_qkv, m_a_log, m_dt_bias, m_head_norm,
                                   m_w_pool, m_pool_scale, m_w_out, m_norm_ffn, m_w_up, m_conv_ffn, m_w_down, m_norm_final)))
    v_in = dict(zip(WEIGHT_ORDER, (v_meta_tokens, v_norm_mix, v_w_in, v_conv_qkv, v_a_log, v_dt_bias, v_head_norm,
                                   v_w_pool, v_pool_scale, v_w_out, v_norm_ffn, v_w_up, v_conv_ffn, v_w_down, v_norm_final)))
    shard_shapes = {k: weights[k].shape for k in SHARDED}
    repl_shapes = {k: weights[k].shape for k in REPLICATED}
    core = lax.axis_index("c").astype(jnp.int32).reshape(1)

    full = unpack_weights(gather_shards(pack_weights({k: weights[k] for k in SHARDED})), shard_shapes)
    layers = []
    for li in range(DEPTH):
        p = {k: (full[k][li] if k in full else weights[k][li]) for k in LAYER_PARAMS}
        layers.append(prep_layer(p))

    h0 = pad_rows(full["meta_tokens"], x[0])
    target = pad_rows(jnp.zeros((N_META, D_MODEL), F32), loss_target[0])
    loss, dh0, grads, d_norm_final = local_step(h0, target, layers, norm_final.reshape(1, D_MODEL))
    seq = x.shape[1]
    grad_x = dh0[LEAD + N_META:LEAD + N_META + seq][None]

    per_layer = [layer_grads(g) for g in grads]
    g_all = {k: jnp.stack([pl_[k] for pl_ in per_layer]) for k in LAYER_PARAMS}
    g_all["meta_tokens"] = dh0[LEAD:LEAD + N_META]
    g_all["norm_final"] = d_norm_final[0]
    packed = pack_grads({k: g_all[k] for k in SHARDED}, {k: g_all[k] for k in REPLICATED})
    g_mine = unpack_grads(all_reduce_to_shards(packed, core), shard_shapes, repl_shapes)

    loss_sum = lax.psum(loss[0, 0], ("x", "y", "c"))
    deltas, new_m, new_v = {}, {}, {}
    for k in WEIGHT_ORDER:
        deltas[k], new_m[k], new_v[k] = adamw(weights[k], g_mine[k], m_in[k], v_in[k], f"adamw_{k}")
    return (loss_sum, grad_x, *[g_mine[k] for k in WEIGHT_ORDER], *[deltas[k] for k in WEIGHT_ORDER],
            *[new_m[k] for k in WEIGHT_ORDER], *[new_v[k] for k in WEIGHT_ORDER])
```

```python
import functools

import jax
import jax.numpy as jnp
from jax import lax
from jax.experimental import pallas as pl
from jax.experimental.pallas import tpu as pltpu

F32 = jnp.float32
BF16 = jnp.bfloat16
WIRE = jnp.bfloat16

D_MODEL = 1024
HEADS = 8
HEAD_DIM = 128
CHUNK = 64
N_META = 16
LEAD = 48
TAIL = 64
QKV_DIM = 3072
D_FF = 2816
POOL_WIDTH = 512
POOL_WINDOWS = (2, 4, 8, 16)
DEPTH = 2
NORM_EPS = 1e-6
ADAM_LR, ADAM_B1, ADAM_B2, ADAM_EPS, ADAM_WD, ADAM_STEP = 0.001, 0.9, 0.999, 1e-08, 0.01, 10
VMEM_LIMIT_BYTES = 48 * 1024 * 1024


def _params(*sem):
    return pltpu.CompilerParams(dimension_semantics=sem if sem else None, vmem_limit_bytes=VMEM_LIMIT_BYTES)


def _tile(n, cap, mult):
    best = None
    for t in range(mult, min(n, cap) + 1, mult):
        if n % t == 0:
            best = t
    assert best is not None, (n, cap, mult)
    return best


def _silu(x):
    return x * jax.nn.sigmoid(x)


def _softplus(x):
    return jnp.maximum(x, 0.0) + jnp.log(1.0 + jnp.exp(-jnp.abs(x)))


def _split_bf16(a):
    hi = a.astype(BF16)
    return hi, (a - hi.astype(F32)).astype(BF16)


def _dg(a, b, ca, cb, hi):
    dims = (((ca,), (cb,)), ((), ()))
    if hi is True:
        return lax.dot_general(a, b, dims, precision=lax.Precision.HIGHEST, preferred_element_type=F32)
    if hi == 3:
        (ah, al), (bh, bl) = _split_bf16(a), _split_bf16(b)
        dot = lambda x, y: lax.dot_general(x, y, dims, preferred_element_type=F32)
        return dot(ah, bh) + (dot(ah, bl) + dot(al, bh))
    return lax.dot_general(a.astype(BF16), b.astype(BF16), dims, preferred_element_type=F32)


def _make_dots(hi):
    @jax.custom_vjp
    def nn(a, b):
        return _dg(a, b, 1, 0, hi)

    @jax.custom_vjp
    def nt(a, b):
        return _dg(a, b, 1, 1, hi)

    @jax.custom_vjp
    def tn(a, b):
        return _dg(a, b, 0, 0, hi)

    nn.defvjp(lambda a, b: (nn(a, b), (a, b)), lambda r, g: (nt(g, r[1]), tn(r[0], g)))
    nt.defvjp(lambda a, b: (nt(a, b), (a, b)), lambda r, g: (nn(g, r[1]), tn(g, r[0])))
    tn.defvjp(lambda a, b: (tn(a, b), (a, b)), lambda r, g: (nt(r[1], g), nn(r[0], g)))
    return nn, nt, tn


_nn, _nt, _tn = _make_dots(False)
_hnn, _hnt, _htn = _make_dots(True)
_nn3, _nt3, _tn3 = _make_dots(3)


def mm(a, b, *, tb=False, add=None, out_dtype=F32, name):
    m, kdim = a.shape
    (n, kb) = b.shape if tb else b.shape[::-1]
    assert kdim == kb, (a.shape, b.shape, tb)
    tm = _tile(m, 1408, 128) if m % 128 == 0 and m <= 4096 else _tile(m, 640, 64)
    tn = _tile(n, 1536, 128)
    tk = kdim if kdim <= 3072 else _tile(kdim, 1664 if a.dtype == b.dtype == BF16 else 640, 128)
    nk = kdim // tk
    dims = (((1,), (1 if tb else 0,)), ((), ()))

    def body(*refs):
        if add is not None:
            a_ref, b_ref, add_ref, o_ref, acc = refs
        else:
            a_ref, b_ref, o_ref, acc = refs
        k = pl.program_id(2)
        part = lax.dot_general(a_ref[...].astype(BF16), b_ref[...].astype(BF16), dims, preferred_element_type=F32)

        def finish(r):
            if add is not None:
                r = r + add_ref[...]
            o_ref[...] = r.astype(out_dtype)

        if nk == 1:
            finish(part)
        else:
            @pl.when(k == 0)
            def _():
                acc[...] = part

            @pl.when(jnp.logical_and(k > 0, k < nk - 1))
            def _():
                acc[...] += part

            @pl.when(k == nk - 1)
            def _():
                finish(acc[...] + part)

    a_spec = pl.BlockSpec((tm, tk), lambda j, i, k: (i, k))
    b_spec = pl.BlockSpec((tn, tk), lambda j, i, k: (j, k)) if tb else pl.BlockSpec((tk, tn), lambda j, i, k: (k, j))
    in_specs = [a_spec, b_spec]
    args = [a, b]
    if add is not None:
        in_specs.append(pl.BlockSpec((tm, tn), lambda j, i, k: (i, j)))
        args.append(add)
    return pl.pallas_call(
        body, grid=(n // tn, m // tm, nk), in_specs=in_specs,
        out_specs=pl.BlockSpec((tm, tn), lambda j, i, k: (i, j)),
        out_shape=jax.ShapeDtypeStruct((m, n), out_dtype),
        scratch_shapes=[pltpu.VMEM((tm, tn) if nk > 1 else (8, 128), F32)],
        compiler_params=_params("parallel", "parallel", "arbitrary"), name=name,
    )(*args)


def _rms(x, gain):
    return x * lax.rsqrt(jnp.mean(x * x, axis=-1, keepdims=True) + NORM_EPS) * gain


def rms_fwd(h, gain, name):
    t = h.shape[0]
    ts = _tile(t, 640, 128)

    def body(h_ref, g_ref, u_ref, ut_ref):
        u = _rms(h_ref[...], g_ref[...])
        u_ref[...] = u.astype(BF16)
        ut_ref[...] = u.T.astype(BF16)

    return pl.pallas_call(
        body, grid=(t // ts,),
        in_specs=[pl.BlockSpec((ts, D_MODEL), lambda i: (i, 0)), pl.BlockSpec((1, D_MODEL), lambda i: (0, 0))],
        out_specs=(pl.BlockSpec((ts, D_MODEL), lambda i: (i, 0)), pl.BlockSpec((D_MODEL, ts), lambda i: (0, i))),
        out_shape=(jax.ShapeDtypeStruct((t, D_MODEL), BF16), jax.ShapeDtypeStruct((D_MODEL, t), BF16)),
        compiler_params=_params("parallel"), name=name,
    )(h, gain)


def rms_bwd(h, gain, du, dres, name):
    t = h.shape[0]
    ts = _tile(t, 640, 64)

    def body(h_ref, g_ref, du_ref, dres_ref, dh_ref, dg_ref):
        i = pl.program_id(0)
        _, vjp = jax.vjp(_rms, h_ref[...], g_ref[...])
        dx, dg = vjp(du_ref[...])
        row = i * ts + lax.broadcasted_iota(jnp.int32, (ts, 1), 0)
        dh_ref[...] = jnp.where(row >= LEAD, dx + dres_ref[...], 0.0)

        @pl.when(i == 0)
        def _():
            dg_ref[...] = jnp.zeros_like(dg_ref)

        dg_ref[...] += dg

    row_spec = pl.BlockSpec((ts, D_MODEL), lambda i: (i, 0))
    vec_spec = pl.BlockSpec((1, D_MODEL), lambda i: (0, 0))
    return pl.pallas_call(
        body, grid=(t // ts,), in_specs=[row_spec, vec_spec, row_spec, row_spec],
        out_specs=(row_spec, vec_spec),
        out_shape=(jax.ShapeDtypeStruct((t, D_MODEL), F32), jax.ShapeDtypeStruct((1, D_MODEL), F32)),
        compiler_params=_params("arbitrary"), name=name,
    )(h, gain, du, dres)


def loss_head(h, gain, target, name):
    t = h.shape[0]
    ts = _tile(t, 640, 64)

    def body(h_ref, g_ref, t_ref, loss_ref, dh_ref, dg_ref):
        i = pl.program_id(0)
        row = i * ts + lax.broadcasted_iota(jnp.int32, (ts, 1), 0)
        keep = jnp.logical_and(row >= LEAD + N_META, row < t - TAIL)
        tgt = t_ref[...]

        def f(x, g):
            err = jnp.where(keep, _rms(x, g) - tgt, 0.0)
            per_row = jnp.mean(err * err, axis=-1, keepdims=True)
            return 0.5 * jnp.sum(per_row, axis=0, keepdims=True)

        val, vjp = jax.vjp(f, h_ref[...], g_ref[...])
        dx, dg = vjp(jnp.ones((1, 1), F32))
        dh_ref[...] = dx

        @pl.when(i == 0)
        def _():
            dg_ref[...] = jnp.zeros_like(dg_ref)
            loss_ref[...] = jnp.zeros_like(loss_ref)

        dg_ref[...] += dg
        loss_ref[...] += jnp.broadcast_to(val, (1, 128))

    row_spec = pl.BlockSpec((ts, D_MODEL), lambda i: (i, 0))
    vec_spec = pl.BlockSpec((1, D_MODEL), lambda i: (0, 0))
    return pl.pallas_call(
        body, grid=(t // ts,), in_specs=[row_spec, vec_spec, row_spec],
        out_specs=(pl.BlockSpec((1, 128), lambda i: (0, 0)), row_spec, vec_spec),
        out_shape=(jax.ShapeDtypeStruct((1, 128), F32), jax.ShapeDtypeStruct((t, D_MODEL), F32),
                   jax.ShapeDtypeStruct((1, D_MODEL), F32)),
        compiler_params=_params("arbitrary"), name=name,
    )(h, gain, target)


def conv_fwd(x, w, name):
    t, width = x.shape
    k = w.shape[0]
    ts = _tile(t, 640, 64)
    tw = _tile(width, 1536, 128)
    hb = ts // 8

    def body(x_ref, halo_ref, w_ref, o_ref, buf):
        i = pl.program_id(0)
        buf[0:8, :] = jnp.where(i > 0, halo_ref[...], 0.0)
        buf[8:, :] = x_ref[...]
        wv = w_ref[...]
        acc = buf[pl.ds(8 - (k - 1), ts), :] * wv[0:1, :]
        for j in range(1, k):
            acc = acc + buf[pl.ds(8 - (k - 1) + j, ts), :] * wv[j:j + 1, :]
        o_ref[...] = acc

    return pl.pallas_call(
        body, grid=(t // ts, width // tw),
        in_specs=[pl.BlockSpec((ts, tw), lambda i, j: (i, j)),
                  pl.BlockSpec((8, tw), lambda i, j: (jnp.maximum(i * hb - 1, 0), j)),
                  pl.BlockSpec((k, tw), lambda i, j: (0, j))],
        out_specs=pl.BlockSpec((ts, tw), lambda i, j: (i, j)),
        out_shape=jax.ShapeDtypeStruct((t, width), F32),
        scratch_shapes=[pltpu.VMEM((ts + 8, tw), F32)],
        compiler_params=_params("parallel", "parallel"), name=name,
    )(x, x, w)


def conv_bwd(x, dc, w, name):
    t, width = x.shape
    k = w.shape[0]
    ts = _tile(t, 640, 64)
    tw = _tile(width, 1536, 128)
    hb = ts // 8
    nt = t // ts

    def body(x_ref, xh_ref, dc_ref, dch_ref, w_ref, dx_ref, dw_ref, xbuf, dbuf):
        i = pl.program_id(1)
        xbuf[0:8, :] = jnp.where(i > 0, xh_ref[...], 0.0)
        xbuf[8:, :] = x_ref[...]
        d = dc_ref[...]
        dbuf[0:ts, :] = d
        dbuf[ts:, :] = jnp.where(i < nt - 1, dch_ref[...], 0.0)
        wv = w_ref[...]
        acc = dbuf[pl.ds(k - 1, ts), :] * wv[0:1, :]
        for j in range(1, k):
            acc = acc + dbuf[pl.ds(k - 1 - j, ts), :] * wv[j:j + 1, :]
        dx_ref[...] = acc.astype(BF16)

        @pl.when(i == 0)
        def _():
            dw_ref[...] = jnp.zeros_like(dw_ref)

        for j in range(k):
            dw_ref[j:j + 1, :] += jnp.sum(d * xbuf[pl.ds(8 - (k - 1) + j, ts), :], axis=0, keepdims=True)

    return pl.pallas_call(
        body, grid=(width // tw, nt),
        in_specs=[pl.BlockSpec((ts, tw), lambda j, i: (i, j)),
                  pl.BlockSpec((8, tw), lambda j, i: (jnp.maximum(i * hb - 1, 0), j)),
                  pl.BlockSpec((ts, tw), lambda j, i: (i, j)),
                  pl.BlockSpec((8, tw), lambda j, i: (jnp.minimum((i + 1) * hb, t // 8 - 1), j)),
                  pl.BlockSpec((k, tw), lambda j, i: (0, j))],
        out_specs=(pl.BlockSpec((ts, tw), lambda j, i: (i, j)), pl.BlockSpec((8, tw), lambda j, i: (0, j))),
        out_shape=(jax.ShapeDtypeStruct((t, width), BF16), jax.ShapeDtypeStruct((8, width), F32)),
        scratch_shapes=[pltpu.VMEM((ts + 8, tw), F32), pltpu.VMEM((ts + 8, tw), F32)],
        compiler_params=_params("parallel", "arbitrary"), name=name,
    )(x, x, dc, dc, w)


def _pool_count(pos, win):
    return jnp.clip(pos + 1, 1, win).astype(F32)


def poolwin_fwd(p, name):
    t = p.shape[0]
    ts = _tile(t, 640, 64)
    hb = ts // 16

    def body(p_ref, halo_ref, o_ref, buf):
        i = pl.program_id(0)
        buf[0:16, :] = jnp.where(i > 0, halo_ref[...], 0.0)
        buf[16:, :] = p_ref[...]
        pos = i * ts + lax.broadcasted_iota(jnp.int32, (ts, 1), 0) - LEAD
        for gi, win in enumerate(POOL_WINDOWS):
            cols = slice(gi * 128, (gi + 1) * 128)
            own = buf[pl.ds(16, ts), cols]
            acc = own
            for j in range(1, win):
                acc = acc + buf[pl.ds(16 - j, ts), cols]
            o_ref[:, cols] = acc / _pool_count(pos, win) - own

    return pl.pallas_call(
        body, grid=(t // ts,),
        in_specs=[pl.BlockSpec((ts, POOL_WIDTH), lambda i: (i, 0)),
                  pl.BlockSpec((16, POOL_WIDTH), lambda i: (jnp.maximum(i * hb - 1, 0), 0))],
        out_specs=pl.BlockSpec((ts, POOL_WIDTH), lambda i: (i, 0)),
        out_shape=jax.ShapeDtypeStruct((t, POOL_WIDTH), F32),
        scratch_shapes=[pltpu.VMEM((ts + 16, POOL_WIDTH), F32)],
        compiler_params=_params("parallel"), name=name,
    )(p, p)


def poolwin_bwd(dpooled, name):
    t = dpooled.shape[0]
    ts = _tile(t, 640, 64)
    hb = ts // 16
    nt = t // ts

    def body(d_ref, halo_ref, o_ref, buf):
        i = pl.program_id(0)
        buf[0:ts, :] = d_ref[...]
        buf[ts:, :] = jnp.where(i < nt - 1, halo_ref[...], 0.0)
        pos = i * ts + lax.broadcasted_iota(jnp.int32, (ts, 1), 0) - LEAD
        for gi, win in enumerate(POOL_WINDOWS):
            cols = slice(gi * 128, (gi + 1) * 128)
            own = buf[pl.ds(0, ts), cols]
            acc = own / _pool_count(pos, win)
            for j in range(1, win):
                acc = acc + buf[pl.ds(j, ts), cols] / _pool_count(pos + j, win)
            o_ref[:, cols] = (acc - own).astype(BF16)

    return pl.pallas_call(
        body, grid=(nt,),
        in_specs=[pl.BlockSpec((ts, POOL_WIDTH), lambda i: (i, 0)),
                  pl.BlockSpec((16, POOL_WIDTH), lambda i: (jnp.minimum((i + 1) * hb, t // 16 - 1), 0))],
        out_specs=pl.BlockSpec((ts, POOL_WIDTH), lambda i: (i, 0)),
        out_shape=jax.ShapeDtypeStruct((t, POOL_WIDTH), BF16),
        scratch_shapes=[pltpu.VMEM((ts + 16, POOL_WIDTH), F32)],
        compiler_params=_params("parallel"), name=name,
    )(dpooled, dpooled)


def _mix(y_a, gpre, pooled, w_pool, scale):
    parts = [_nn(pooled[:, g * 128:(g + 1) * 128], w_pool[g]) for g in range(4)]
    y_b = jnp.concatenate(parts, axis=1) * scale
    return jax.nn.sigmoid(gpre[:, :D_MODEL]) * y_a + jax.nn.sigmoid(gpre[:, D_MODEL:]) * y_b


def _mix_specs(ts):
    return [pl.BlockSpec((ts, D_MODEL), lambda i: (i, 0)), pl.BlockSpec((ts, 2 * D_MODEL), lambda i: (i, 0)),
            pl.BlockSpec((ts, POOL_WIDTH), lambda i: (i, 0)), pl.BlockSpec((4, 128, 256), lambda i: (0, 0, 0)),
            pl.BlockSpec((1, D_MODEL), lambda i: (0, 0))]


def mix_fwd(y_a, gpre, pooled, w_pool, scale, name):
    t = y_a.shape[0]
    ts = _tile(t, 640, 128)

    def body(ya_ref, g_ref, p_ref, w_ref, s_ref, o_ref, ot_ref):
        y = _mix(ya_ref[...], g_ref[...], p_ref[...], w_ref[...], s_ref[...])
        o_ref[...] = y.astype(BF16)
        ot_ref[...] = y.T.astype(BF16)

    return pl.pallas_call(
        body, grid=(t // ts,), in_specs=_mix_specs(ts),
        out_specs=(pl.BlockSpec((ts, D_MODEL), lambda i: (i, 0)), pl.BlockSpec((D_MODEL, ts), lambda i: (0, i))),
        out_shape=(jax.ShapeDtypeStruct((t, D_MODEL), BF16), jax.ShapeDtypeStruct((D_MODEL, t), BF16)),
        compiler_params=_params("parallel"), name=name,
    )(y_a, gpre, pooled, w_pool, scale)


def mix_bwd(y_a, gpre, pooled, w_pool, scale, dy, name):
    t = y_a.shape[0]
    ts = _tile(t, 320, 64)

    def body(ya_ref, g_ref, p_ref, w_ref, s_ref, dy_ref, dya_ref, dg_ref, dp_ref, dw_ref, ds_ref):
        i = pl.program_id(0)
        _, vjp = jax.vjp(_mix, ya_ref[...], g_ref[...], p_ref[...], w_ref[...], s_ref[...])
        dya, dg, dp, dw, ds = vjp(dy_ref[...])
        dya_ref[...] = dya
        dg_ref[...] = dg.astype(BF16)
        dp_ref[...] = dp

        @pl.when(i == 0)
        def _():
            dw_ref[...] = jnp.zeros_like(dw_ref)
            ds_ref[...] = jnp.zeros_like(ds_ref)

        dw_ref[...] += dw
        ds_ref[...] += ds

    specs = _mix_specs(ts)
    return pl.pallas_call(
        body, grid=(t // ts,), in_specs=specs + [specs[0]],
        out_specs=(specs[0], specs[1], specs[2], specs[3], specs[4]),
        out_shape=(jax.ShapeDtypeStruct((t, D_MODEL), F32), jax.ShapeDtypeStruct((t, 2 * D_MODEL), BF16),
                   jax.ShapeDtypeStruct((t, POOL_WIDTH), F32), jax.ShapeDtypeStruct((4, 128, 256), F32),
                   jax.ShapeDtypeStruct((1, D_MODEL), F32)),
        compiler_params=_params("arbitrary"), name=name,
    )(y_a, gpre, pooled, w_pool, scale, dy)


def _ffn_act(cg, cv):
    return _silu(cg) * cv


def ffnact_fwd(cg, cv, name):
    t, width = cg.shape
    ts = _tile(t, 640, 128)
    tw = _tile(width, 1536, 128)
    spec = pl.BlockSpec((ts, tw), lambda i, j: (i, j))

    def body(g_ref, v_ref, o_ref, ot_ref):
        act = _ffn_act(g_ref[...], v_ref[...])
        o_ref[...] = act.astype(BF16)
        ot_ref[...] = act.T.astype(BF16)

    return pl.pallas_call(
        body, grid=(t // ts, width // tw), in_specs=[spec, spec],
        out_specs=(spec, pl.BlockSpec((tw, ts), lambda i, j: (j, i))),
        out_shape=(jax.ShapeDtypeStruct((t, width), BF16), jax.ShapeDtypeStruct((width, t), BF16)),
        compiler_params=_params("parallel", "parallel"), name=name,
    )(cg, cv)


def ffnact_bwd(cg, cv, dact, name):
    t, width = cg.shape
    ts = _tile(t, 640, 64)
    tw = _tile(width, 1536, 128)
    spec = pl.BlockSpec((ts, tw), lambda i, j: (i, j))

    def body(g_ref, v_ref, d_ref, dg_ref, dv_ref):
        _, vjp = jax.vjp(_ffn_act, g_ref[...], v_ref[...])
        dg_ref[...], dv_ref[...] = vjp(d_ref[...])

    return pl.pallas_call(
        body, grid=(t // ts, width // tw), in_specs=[spec, spec, spec], out_specs=(spec, spec),
        out_shape=(jax.ShapeDtypeStruct((t, width), F32), jax.ShapeDtypeStruct((t, width), F32)),
        compiler_params=_params("parallel", "parallel"), name=name,
    )(cg, cv, dact)


def _gdn_chunk(c, z, ba, pa, pdt, hn, s, *, valid):
    r = lax.broadcasted_iota(jnp.int32, (CHUNK, CHUNK), 0)
    q_ = lax.broadcasted_iota(jnp.int32, (CHUNK, CHUNK), 1)
    causal = r >= q_
    strict = r > q_
    tril = causal.astype(F32)
    triu = (r <= q_).astype(F32)
    eye = (r == q_).astype(F32)
    lane = lax.broadcasted_iota(jnp.int32, (CHUNK, 128), 1)

    decay_log = -jnp.exp(pa) * _softplus(ba + pdt)
    bg = jnp.where(lane < HEADS, jax.nn.sigmoid(ba), jnp.where(lane < 2 * HEADS, decay_log, 0.0))
    bg = jnp.where(valid, bg, 0.0)
    gc = _hnn(tril, bg)
    gct = _hnn(bg.T, triu)
    eg = jnp.exp(gc)
    glast = gc[CHUNK - 1:CHUNK, :]
    ekd = jnp.exp(glast - gc)
    gtot = jnp.exp(glast)

    hd = range(HEADS)
    hs = [slice(h * HEAD_DIM, (h + 1) * HEAD_DIM) for h in hd]
    gl = [slice(HEADS + h, HEADS + h + 1) for h in hd]
    q = [_silu(c[:, hs[h]]) for h in hd]
    k = [_silu(c[:, D_MODEL + h * HEAD_DIM:D_MODEL + (h + 1) * HEAD_DIM]) for h in hd]
    v = [_silu(c[:, 2 * D_MODEL + h * HEAD_DIM:2 * D_MODEL + (h + 1) * HEAD_DIM]) for h in hd]
    q = [q[h] * lax.rsqrt(jnp.sum(q[h] * q[h], axis=-1, keepdims=True) + NORM_EPS) * (HEAD_DIM ** -0.5) for h in hd]
    k = [k[h] * lax.rsqrt(jnp.sum(k[h] * k[h], axis=-1, keepdims=True) + NORM_EPS) for h in hd]
    beta = [bg[:, h:h + 1] for h in hd]
    decay = [jnp.exp(jnp.where(causal, gc[:, gl[h]] - gct[gl[h], :], -1e30)) for h in hd]
    kb = [k[h] * beta[h] for h in hd]
    a = [jnp.where(strict, _nt(kb[h], k[h]) * decay[h], 0.0) for h in hd]
    qk = [jnp.where(causal, _nt(q[h], k[h]) * decay[h], 0.0) for h in hd]
    p = [_nn3(a[h], a[h]) for h in hd]
    x = [(eye - a[h]) + p[h] - _nn(a[h], p[h]) for h in hd]
    for _ in range(4):
        p = [_nn(p[h], p[h]) for h in hd]
        x = [x[h] + p[h] + _nn(x[h] - eye, p[h]) for h in hd]
    u = [_nn(x[h], v[h] * beta[h]) for h in hd]
    w = [_nn(x[h], kb[h] * eg[:, gl[h]]) for h in hd]
    v_new = [u[h] - _nn(w[h], s[h]) for h in hd]
    o = [_nn(q[h] * eg[:, gl[h]], s[h]) + _nn(qk[h], v_new[h]) for h in hd]
    states = [s[h] * gtot[:, gl[h]] + _tn(k[h] * ekd[:, gl[h]], v_new[h]) for h in hd]
    o = [o[h] * lax.rsqrt(jnp.mean(o[h] * o[h], axis=-1, keepdims=True) + NORM_EPS) * hn * _silu(z[:, hs[h]])
         for h in hd]
    return jnp.concatenate(o, axis=1), tuple(states)


def _chunk_valid(n, t):
    row = n * CHUNK + lax.broadcasted_iota(jnp.int32, (CHUNK, 1), 0)
    return jnp.logical_and(row >= LEAD, row < t - TAIL)


def gdn_fwd(c, z, ba, pa, pdt, hn, name):
    t = c.shape[0]
    n_chunks = t // CHUNK

    def body(c_ref, z_ref, ba_ref, pa_ref, pdt_ref, hn_ref, y_ref, ss_ref, state):
        n = pl.program_id(0)

        @pl.when(n == 0)
        def _():
            state[...] = jnp.zeros_like(state)

        s0 = tuple(state[h] for h in range(HEADS))
        for h in range(HEADS):
            ss_ref[0, h] = s0[h]
        y, s1 = _gdn_chunk(c_ref[...], z_ref[...], ba_ref[...], pa_ref[...], pdt_ref[...], hn_ref[...], s0,
                           valid=_chunk_valid(n, t))
        y_ref[...] = y
        for h in range(HEADS):
            state[h] = s1[h]

    vec = pl.BlockSpec((1, 128), lambda n: (0, 0))
    return pl.pallas_call(
        body, grid=(n_chunks,),
        in_specs=[pl.BlockSpec((CHUNK, QKV_DIM), lambda n: (n, 0)), pl.BlockSpec((CHUNK, D_MODEL), lambda n: (n, 0)),
                  pl.BlockSpec((CHUNK, 128), lambda n: (n, 0)), vec, vec, vec],
        out_specs=(pl.BlockSpec((CHUNK, D_MODEL), lambda n: (n, 0)),
                   pl.BlockSpec((1, HEADS, HEAD_DIM, HEAD_DIM), lambda n: (n, 0, 0, 0))),
        out_shape=(jax.ShapeDtypeStruct((t, D_MODEL), F32),
                   jax.ShapeDtypeStruct((n_chunks, HEADS, HEAD_DIM, HEAD_DIM), F32)),
        scratch_shapes=[pltpu.VMEM((HEADS, HEAD_DIM, HEAD_DIM), F32)],
        compiler_params=_params("arbitrary"), name=name,
    )(c, z, ba, pa, pdt, hn)


def gdn_bwd(c, z, ba, pa, pdt, hn, starts, dy, name):
    t = c.shape[0]
    n_chunks = t // CHUNK

    def body(c_ref, z_ref, ba_ref, pa_ref, pdt_ref, hn_ref, ss_ref, dy_ref,
             dc_ref, dz_ref, dba_ref, dpa_ref, dpdt_ref, dhn_ref, dstate):
        step = pl.program_id(0)
        n = n_chunks - 1 - step

        @pl.when(step == 0)
        def _():
            dstate[...] = jnp.zeros_like(dstate)
            dpa_ref[...] = jnp.zeros_like(dpa_ref)
            dpdt_ref[...] = jnp.zeros_like(dpdt_ref)
            dhn_ref[...] = jnp.zeros_like(dhn_ref)

        f = functools.partial(_gdn_chunk, valid=_chunk_valid(n, t))
        _, vjp = jax.vjp(f, c_ref[...], z_ref[...], ba_ref[...], pa_ref[...], pdt_ref[...], hn_ref[...],
                         tuple(ss_ref[0, h] for h in range(HEADS)))
        dc, dz, dba, dpa, dpdt, dhn, ds = vjp((dy_ref[...], tuple(dstate[h] for h in range(HEADS))))
        dc_ref[...] = dc
        dz_ref[...] = dz.astype(BF16)
        dba_ref[...] = dba.astype(BF16)
        dpa_ref[...] += dpa
        dpdt_ref[...] += dpdt
        dhn_ref[...] += dhn
        for h in range(HEADS):
            dstate[h] = ds[h]

    def rev(width):
        return pl.BlockSpec((CHUNK, width), lambda s: (n_chunks - 1 - s, 0))

    vec = pl.BlockSpec((1, 128), lambda s: (0, 0))
    vec_shape = jax.ShapeDtypeStruct((1, 128), F32)
    return pl.pallas_call(
        body, grid=(n_chunks,),
        in_specs=[rev(QKV_DIM), rev(D_MODEL), rev(128), vec, vec, vec,
                  pl.BlockSpec((1, HEADS, HEAD_DIM, HEAD_DIM), lambda s: (n_chunks - 1 - s, 0, 0, 0)), rev(D_MODEL)],
        out_specs=(rev(QKV_DIM), rev(D_MODEL), rev(128), vec, vec, vec),
        out_shape=(jax.ShapeDtypeStruct((t, QKV_DIM), F32), jax.ShapeDtypeStruct((t, D_MODEL), BF16),
                   jax.ShapeDtypeStruct((t, 128), BF16), vec_shape, vec_shape, vec_shape),
        scratch_shapes=[pltpu.VMEM((HEADS, HEAD_DIM, HEAD_DIM), F32)],
        compiler_params=_params("arbitrary"), name=name,
    )(c, z, ba, pa, pdt, hn, starts, dy)


def _layer_fwd(h, w, tag):
    u, ut = rms_fwd(h, w["norm_mix"], f"{tag}_rms_mix")
    pq = mm(u, w["wqkv"], name=f"{tag}_mm_qkv")
    pz = mm(u, w["wz"], name=f"{tag}_mm_z")
    pg = mm(u, w["wg"], name=f"{tag}_mm_gate")
    pp = mm(u, w["wpl"], name=f"{tag}_mm_pool")
    pba = mm(u, w["wba"], name=f"{tag}_mm_ba")
    cq = conv_fwd(pq, w["conv_qkv"], f"{tag}_conv_qkv")
    ya, starts = gdn_fwd(cq, pz, pba, w["pa"], w["pdt"], w["head_norm"], f"{tag}_gdn")
    pooled = poolwin_fwd(pp, f"{tag}_poolwin")
    y, yt = mix_fwd(ya, pg, pooled, w["w_pool"], w["pool_scale"], f"{tag}_mix")
    h1 = mm(y, w["wout"], add=h, name=f"{tag}_mm_out")
    u2, u2t = rms_fwd(h1, w["norm_ffn"], f"{tag}_rms_ffn")
    hg = mm(u2, w["wupg"], name=f"{tag}_mm_upg")
    hv = mm(u2, w["wupv"], name=f"{tag}_mm_upv")
    cg = conv_fwd(hg, w["conv_g"], f"{tag}_conv_g")
    cv = conv_fwd(hv, w["conv_v"], f"{tag}_conv_v")
    act, actt = ffnact_fwd(cg, cv, f"{tag}_act")
    h2 = mm(act, w["wdown"], add=h1, name=f"{tag}_mm_down")
    saved = dict(h=h, ut=ut, pq=pq, pz=pz, pg=pg, pba=pba, cq=cq, ya=ya, starts=starts, pooled=pooled, yt=yt, h1=h1,
                 u2t=u2t, hg=hg, hv=hv, cg=cg, cv=cv, actt=actt)
    return h2, saved


def _layer_bwd(dh2, w, s, tag):
    g = {}
    dact = mm(dh2, w["wdown"], tb=True, name=f"{tag}_bmm_down_x")
    g["wdown"] = mm(s["actt"], dh2, name=f"{tag}_bmm_down_w")
    dcg, dcv = ffnact_bwd(s["cg"], s["cv"], dact, f"{tag}_act_b")
    dhg, g["conv_g"] = conv_bwd(s["hg"], dcg, w["conv_g"], f"{tag}_conv_g_b")
    dhv, g["conv_v"] = conv_bwd(s["hv"], dcv, w["conv_v"], f"{tag}_conv_v_b")
    du2 = mm(dhg, w["wupg"], tb=True, name=f"{tag}_bmm_upg_x")
    du2 = mm(dhv, w["wupv"], tb=True, add=du2, name=f"{tag}_bmm_upv_x")
    g["wupg"] = mm(s["u2t"], dhg, name=f"{tag}_bmm_upg_w")
    g["wupv"] = mm(s["u2t"], dhv, name=f"{tag}_bmm_upv_w")
    dh1, g["norm_ffn"] = rms_bwd(s["h1"], w["norm_ffn"], du2, dh2, f"{tag}_rms_ffn_b")
    dy = mm(dh1, w["wout"], tb=True, name=f"{tag}_bmm_out_x")
    g["wout"] = mm(s["yt"], dh1, name=f"{tag}_bmm_out_w")
    dya, dpg, dpooled, g["w_pool"], g["pool_scale"] = mix_bwd(
        s["ya"], s["pg"], s["pooled"], w["w_pool"], w["pool_scale"], dy, f"{tag}_mix_b")
    dpp = poolwin_bwd(dpooled, f"{tag}_poolwin_b")
    dcq, dpz, dpba, g["pa"], g["pdt"], g["head_norm"] = gdn_bwd(
        s["cq"], s["pz"], s["pba"], w["pa"], w["pdt"], w["head_norm"], s["starts"], dya, f"{tag}_gdn_b")
    dpq, g["conv_qkv"] = conv_bwd(s["pq"], dcq, w["conv_qkv"], f"{tag}_conv_qkv_b")
    du = mm(dpq, w["wqkv"], tb=True, name=f"{tag}_bmm_qkv_x")
    du = mm(dpz, w["wz"], tb=True, add=du, name=f"{tag}_bmm_z_x")
    du = mm(dpg, w["wg"], tb=True, add=du, name=f"{tag}_bmm_gate_x")
    du = mm(dpp, w["wpl"], tb=True, add=du, name=f"{tag}_bmm_pool_x")
    du = mm(dpba, w["wba"], tb=True, add=du, name=f"{tag}_bmm_ba_x")
    g["wqkv"] = mm(s["ut"], dpq, name=f"{tag}_bmm_qkv_w")
    g["wz"] = mm(s["ut"], dpz, name=f"{tag}_bmm_z_w")
    g["wg"] = mm(s["ut"], dpg, name=f"{tag}_bmm_gate_w")
    g["wpl"] = mm(s["ut"], dpp, name=f"{tag}_bmm_pool_w")
    g["wba"] = mm(s["ut"], dpba, name=f"{tag}_bmm_ba_w")
    dh, g["norm_mix"] = rms_bwd(s["h"], w["norm_mix"], du, dh1, f"{tag}_rms_mix_b")
    return dh, g


def local_step(h0, target, layers, norm_final):
    h = h0
    saved = []
    for li, w in enumerate(layers):
        h, s = _layer_fwd(h, w, f"l{li}")
        saved.append(s)
    loss, dh, dnf = loss_head(h, norm_final, target, "loss_head")
    grads = [None] * len(layers)
    for li in reversed(range(len(layers))):
        dh, grads[li] = _layer_bwd(dh, layers[li], saved[li], f"l{li}")
    return loss, dh, grads, dnf


_Z0, _B0, _P0, _G0, _IN_DIM = 3072, 4096, 4112, 4624, 6672


def _lanes_8_to_15(v):
    return jnp.pad(v.reshape(1, HEADS).astype(F32), ((0, 0), (HEADS, 128 - 2 * HEADS)))


def prep_layer(p):
    w_in = p["w_in"].astype(BF16)
    w_up = p["w_up"].astype(BF16)
    row = lambda v: v.reshape(1, -1).astype(F32)
    return dict(
        wqkv=w_in[:, :_Z0], wz=w_in[:, _Z0:_B0], wba=jnp.pad(w_in[:, _B0:_P0], ((0, 0), (0, 128 - 2 * HEADS))),
        wpl=w_in[:, _P0:_G0], wg=w_in[:, _G0:], wout=p["w_out"].astype(BF16),
        wupg=w_up[:, :D_FF], wupv=w_up[:, D_FF:], wdown=p["w_down"].astype(BF16),
        conv_qkv=p["conv_qkv"].astype(F32), conv_g=p["conv_ffn"][:, :D_FF].astype(F32),
        conv_v=p["conv_ffn"][:, D_FF:].astype(F32), w_pool=p["w_pool"].astype(F32),
        pool_scale=row(p["pool_scale"]), head_norm=row(p["head_norm"]), norm_mix=row(p["norm_mix"]),
        norm_ffn=row(p["norm_ffn"]), pa=_lanes_8_to_15(p["a_log"]), pdt=_lanes_8_to_15(p["dt_bias"]))


def layer_grads(g):
    return dict(
        w_in=jnp.concatenate([g["wqkv"], g["wz"], g["wba"][:, :2 * HEADS], g["wpl"], g["wg"]], axis=1),
        conv_qkv=g["conv_qkv"][:4], a_log=g["pa"][0, HEADS:2 * HEADS], dt_bias=g["pdt"][0, HEADS:2 * HEADS],
        head_norm=g["head_norm"][0], w_pool=g["w_pool"], pool_scale=g["pool_scale"][0], w_out=g["wout"],
        norm_mix=g["norm_mix"][0], norm_ffn=g["norm_ffn"][0],
        w_up=jnp.concatenate([g["wupg"], g["wupv"]], axis=1),
        conv_ffn=jnp.concatenate([g["conv_g"][:3], g["conv_v"][:3]], axis=1), w_down=g["wdown"])


LAYER_PARAMS = ("norm_mix", "w_in", "conv_qkv", "a_log", "dt_bias", "head_norm", "w_pool", "pool_scale", "w_out",
                "norm_ffn", "w_up", "conv_ffn", "w_down")


def pad_rows(meta, x):
    return jnp.concatenate([jnp.zeros((LEAD, D_MODEL), F32), meta.astype(F32), x.astype(F32),
                            jnp.zeros((TAIL, D_MODEL), F32)], axis=0)


MESH = pl.DeviceIdType.MESH
ANY = pl.BlockSpec(memory_space=pl.ANY)


def _place():
    x, y, c = lax.axis_index("x"), lax.axis_index("y"), lax.axis_index("c")
    return x, y, c, [(1 - x, y), (x, 1 - y), (1 - x, 1 - y)]


def gather_shards(pack):
    _, rows, lanes = pack.shape

    def body(p_ref, o_ref, send_sems, recv_sems, local_sem):
        x, y, c, chips = _place()
        me = 2 * x + y

        def copy(k, chip, half, to, src=None):
            dst = o_ref.at[2 * chip[0] + chip[1], half]
            return pltpu.make_async_remote_copy(src_ref=dst if src is None else src, dst_ref=dst,
                                                send_sem=send_sems.at[k], recv_sem=recv_sems.at[k],
                                                device_id=to, device_id_type=MESH)

        own = pltpu.make_async_copy(p_ref, o_ref.at[me], local_sem)
        own.start()
        first = [copy(j, (x, y), c, (*chip, c), src=p_ref.at[c]) for j, chip in enumerate(chips)]
        for cp in first:
            cp.start()
        passed = [copy(3 + j, chip, c, (x, y, 1 - c)) for j, chip in enumerate(chips)]
        for j, chip in enumerate(chips):
            copy(j, chip, c, (x, y, c)).wait_recv()
            passed[j].start()
        for j, chip in enumerate(chips):
            copy(3 + j, chip, 1 - c, (x, y, c)).wait_recv()
        for cp in first + passed:
            cp.wait_send()
        own.wait()

    return pl.pallas_call(
        body, in_specs=[ANY], out_specs=ANY,
        out_shape=jax.ShapeDtypeStruct((4, 2, rows, lanes), pack.dtype),
        scratch_shapes=[pltpu.SemaphoreType.DMA((6,)), pltpu.SemaphoreType.DMA((6,)), pltpu.SemaphoreType.DMA],
        name="gather_shards",
    )(pack)


def swap_other_halves(p):
    _, _, rows, lanes = p.shape

    def body(p_ref, o_ref, send_sems, recv_sems):
        x, y, c, _ = _place()
        copies = [pltpu.make_async_remote_copy(src_ref=p_ref.at[s, 1 - c], dst_ref=o_ref.at[s],
                                               send_sem=send_sems.at[s], recv_sem=recv_sems.at[s],
                                               device_id=(x, y, 1 - c), device_id_type=MESH) for s in range(4)]
        for cp in copies:
            cp.start()
        for cp in copies:
            cp.wait()

    return pl.pallas_call(
        body, in_specs=[ANY], out_specs=ANY, out_shape=jax.ShapeDtypeStruct((4, rows, lanes), p.dtype),
        scratch_shapes=[pltpu.SemaphoreType.DMA((4,)), pltpu.SemaphoreType.DMA((4,))], name="swap_other_halves",
    )(p)


def scatter_to_chips(q):
    _, rows, lanes = q.shape

    def body(q_ref, o_ref, send_sems, recv_sems, local_sem):
        x, y, c, chips = _place()
        me = 2 * x + y
        own = pltpu.make_async_copy(q_ref.at[me], o_ref.at[me], local_sem)
        own.start()
        copies = [pltpu.make_async_remote_copy(src_ref=q_ref.at[2 * chip[0] + chip[1]], dst_ref=o_ref.at[me],
                                               send_sem=send_sems.at[j], recv_sem=recv_sems.at[j],
                                               device_id=(*chip, c), device_id_type=MESH)
                  for j, chip in enumerate(chips)]
        for cp in copies:
            cp.start()
        for j, chip in enumerate(chips):
            slot = o_ref.at[2 * chip[0] + chip[1]]
            pltpu.make_async_remote_copy(src_ref=slot, dst_ref=slot, send_sem=send_sems.at[j], recv_sem=recv_sems.at[j],
                                         device_id=(x, y, c), device_id_type=MESH).wait_recv()
        for cp in copies:
            cp.wait_send()
        own.wait()

    return pl.pallas_call(
        body, in_specs=[ANY], out_specs=ANY, out_shape=jax.ShapeDtypeStruct((4, rows, lanes), q.dtype),
        scratch_shapes=[pltpu.SemaphoreType.DMA((3,)), pltpu.SemaphoreType.DMA((3,)), pltpu.SemaphoreType.DMA],
        name="scatter_to_chips",
    )(q)


def join_halves(g):
    rows, lanes = g.shape

    def body(g_ref, o_ref, send_sem, recv_sem, local_sem):
        x, y, c, _ = _place()
        own = pltpu.make_async_copy(g_ref, o_ref.at[c], local_sem)
        own.start()
        cp = pltpu.make_async_remote_copy(src_ref=g_ref, dst_ref=o_ref.at[c], send_sem=send_sem, recv_sem=recv_sem,
                                          device_id=(x, y, 1 - c), device_id_type=MESH)
        cp.start()
        other = o_ref.at[1 - c]
        pltpu.make_async_remote_copy(src_ref=other, dst_ref=other, send_sem=send_sem, recv_sem=recv_sem,
                                     device_id=(x, y, c), device_id_type=MESH).wait_recv()
        cp.wait_send()
        own.wait()

    return pl.pallas_call(
        body, in_specs=[ANY], out_specs=ANY, out_shape=jax.ShapeDtypeStruct((2, rows, lanes), g.dtype),
        scratch_shapes=[pltpu.SemaphoreType.DMA, pltpu.SemaphoreType.DMA, pltpu.SemaphoreType.DMA], name="join_halves",
    )(g)


def add_own_half(p, other, c):
    _, _, rows, lanes = p.shape
    tr = _tile(rows, 4096, 8)

    def body(c_ref, p_ref, o_ref, out_ref):
        out_ref[...] = p_ref[...] + o_ref[...]

    return pl.pallas_call(
        body,
        grid_spec=pltpu.PrefetchScalarGridSpec(
            num_scalar_prefetch=1, grid=(4, rows // tr),
            in_specs=[pl.BlockSpec((None, None, tr, lanes), lambda s, i, c_ref: (s, c_ref[0], i, 0)),
                      pl.BlockSpec((None, tr, lanes), lambda s, i, c_ref: (s, i, 0))],
            out_specs=pl.BlockSpec((None, tr, lanes), lambda s, i, c_ref: (s, i, 0))),
        out_shape=jax.ShapeDtypeStruct((4, rows, lanes), F32),
        compiler_params=_params("parallel", "parallel"), name="add_own_half",
    )(c, p, other)


def sum_chips(b):
    _, rows, lanes = b.shape
    tr = _tile(rows, 4096, 8)

    def body(b_ref, out_ref):
        out_ref[...] = ((b_ref[0] + b_ref[1]) + b_ref[2]) + b_ref[3]

    return pl.pallas_call(
        body, grid=(rows // tr,), in_specs=[pl.BlockSpec((4, tr, lanes), lambda i: (0, i, 0))],
        out_specs=pl.BlockSpec((tr, lanes), lambda i: (i, 0)), out_shape=jax.ShapeDtypeStruct((rows, lanes), F32),
        compiler_params=_params("parallel"), name="sum_chips",
    )(b)


def all_reduce_to_shards(p, c):
    q = add_own_half(p, swap_other_halves(p), c)
    g = sum_chips(scatter_to_chips(q))
    both = join_halves(g)
    return both.reshape(2 * both.shape[1], both.shape[2])


def adamw(w, g, m, v, name):
    shape = w.shape
    cols = shape[-1]
    w2, g2, m2, v2 = (a.reshape(-1, cols) for a in (w, g, m, v))
    rows = w2.shape[0]
    tr = _tile(rows, max(8, 262144 // cols), 8) if rows % 8 == 0 else rows
    c1 = 1.0 - ADAM_B1 ** ADAM_STEP
    c2 = 1.0 - ADAM_B2 ** ADAM_STEP

    def body(w_ref, g_ref, m_ref, v_ref, d_ref, mo_ref, vo_ref):
        gv = g_ref[...]
        mn = ADAM_B1 * m_ref[...] + (1.0 - ADAM_B1) * gv
        vn = ADAM_B2 * v_ref[...] + (1.0 - ADAM_B2) * jnp.square(gv)
        d_ref[...] = -ADAM_LR * ((mn / c1) / (jnp.sqrt(vn / c2) + ADAM_EPS) + ADAM_WD * w_ref[...])
        mo_ref[...] = mn
        vo_ref[...] = vn

    spec = pl.BlockSpec((tr, cols), lambda i: (i, 0))
    out = jax.ShapeDtypeStruct((rows, cols), F32)
    d, mn, vn = pl.pallas_call(
        body, grid=(rows // tr,), in_specs=[spec] * 4, out_specs=(spec,) * 3, out_shape=(out,) * 3,
        compiler_params=_params("parallel"), name=name,
    )(w2, g2, m2, v2)
    return d.reshape(shape), mn.reshape(shape), vn.reshape(shape)


SHARDED = ("w_in", "w_up", "w_out", "w_down", "w_pool", "conv_qkv", "conv_ffn", "meta_tokens")
MATMUL_WEIGHTS = ("w_in", "w_up", "w_out", "w_down", "w_pool")
REPLICATED = ("norm_mix", "a_log", "dt_bias", "head_norm", "pool_scale", "norm_ffn", "norm_final")
SHARD_AXIS = {"w_in": 2, "w_up": 2, "w_out": 1, "w_down": 1, "w_pool": 3, "conv_qkv": 2, "conv_ffn": 2, "meta_tokens": 1}


def _rows_of(a):
    return a.reshape(-1, 128)


SEGMENT_ROWS = 16


def _segment(n_rows):
    return -(-n_rows // SEGMENT_ROWS) * SEGMENT_ROWS


def _pad_segment(a):
    pad = [(0, 0)] * a.ndim
    pad[-2] = (0, _segment(a.shape[-2]) - a.shape[-2])
    return jnp.pad(a, pad)


def _unshard(stacked, axis):
    full = jnp.moveaxis(stacked, 0, axis)
    shape = list(full.shape)
    shape[axis:axis + 2] = [shape[axis] * shape[axis + 1]]
    return full.reshape(shape)


def _shard_stack(full, axis):
    shape = list(full.shape)
    shape[axis:axis + 1] = [4, shape[axis] // 4]
    return jnp.moveaxis(full.reshape(shape), axis, 0)


def pack_weights(shards):
    parts = [_rows_of(shards[k].astype(WIRE)) for k in MATMUL_WEIGHTS]
    parts += [lax.bitcast_convert_type(_rows_of(shards[k].astype(F32)), WIRE).reshape(-1, 128)
              for k in SHARDED if k not in MATMUL_WEIGHTS]
    parts = [_pad_segment(p) for p in parts]
    rows = sum(p.shape[0] for p in parts)
    if rows % (2 * SEGMENT_ROWS):
        parts.append(jnp.zeros((SEGMENT_ROWS, 128), WIRE))
        rows += SEGMENT_ROWS
    return jnp.concatenate(parts, axis=0).reshape(2, rows // 2, 128)


def unpack_weights(gathered, shard_shapes):
    flat = gathered.reshape(4, -1, 128)
    out, at = {}, 0
    for k in SHARDED:
        shp = shard_shapes[k]
        n = 1
        for e in shp:
            n *= e
        if k in MATMUL_WEIGHTS:
            r = n // 128
            stacked = flat[:, at:at + r].reshape((4,) + tuple(shp))
        else:
            r = 2 * n // 128
            stacked = lax.bitcast_convert_type(flat[:, at:at + r].reshape(4, n // 128, 128, 2), F32).reshape((4,) + tuple(shp))
        out[k] = _unshard(stacked, SHARD_AXIS[k])
        at += _segment(r)
    return out


def pack_grads(full, repl):
    parts = [_shard_stack(full[k], SHARD_AXIS[k]).reshape(4, -1, 128) for k in SHARDED]
    r = jnp.concatenate([repl[k].reshape(-1) for k in REPLICATED])
    r = jnp.pad(r, (0, -r.shape[0] % 128)).reshape(1, -1, 128)
    parts.append(jnp.broadcast_to(r, (4,) + r.shape[1:]))
    parts = [_pad_segment(p) for p in parts]
    rows = sum(p.shape[1] for p in parts)
    if rows % (2 * SEGMENT_ROWS):
        parts.append(jnp.zeros((4, SEGMENT_ROWS, 128), F32))
        rows += SEGMENT_ROWS
    return jnp.concatenate(parts, axis=1).reshape(4, 2, rows // 2, 128)


def unpack_grads(flat, shard_shapes, repl_shapes):
    out, at = {}, 0
    for k in SHARDED:
        n = 1
        for e in shard_shapes[k]:
            n *= e
        out[k] = flat[at:at + n // 128].reshape(shard_shapes[k])
        at += _segment(n // 128)
    r = flat[at:].reshape(-1)
    at = 0
    for k in REPLICATED:
        n = 1
        for e in repl_shapes[k]:
            n *= e
        out[k] = r[at:at + n].reshape(repl_shapes[k])
        at += n
    return out


WEIGHT_ORDER = ("meta_tokens", "norm_mix", "w_in", "conv_qkv", "a_log", "dt_bias", "head_norm", "w_pool", "pool_scale",
                "w_out", "norm_ffn", "w_up", "conv_ffn", "w_down", "norm_final")


def kernel(x, meta_tokens, norm_mix, w_in, conv_qkv, a_log, dt_bias, head_norm, w_pool, pool_scale, w_out, norm_ffn, w_up, conv_ffn, w_down, norm_final, loss_target, m_meta_tokens, m_norm_mix, m_w_in, m_conv_qkv, m_a_log, m_dt_bias, m_head_norm, m_w_pool, m_pool_scale, m_w_out, m_norm_ffn, m_w_up, m_conv_ffn, m_w_down, m_norm_final, v_meta_tokens, v_norm_mix, v_w_in, v_conv_qkv, v_a_log, v_dt_bias, v_head_norm, v_w_pool, v_pool_scale, v_w_out, v_norm_ffn, v_w_up, v_conv_ffn, v_w_down, v_norm_final):
    weights = dict(meta_tokens=meta_tokens, norm_mix=norm_mix, w_in=w_in, conv_qkv=conv_qkv, a_log=a_log,
                   dt_bias=dt_bias, head_norm=head_norm, w_pool=w_pool, pool_scale=pool_scale, w_out=w_out,
                   norm_ffn=norm_ffn, w_up=w_up, conv_ffn=conv_ffn, w_down=w_down, norm_final=norm_final)
    m_in = dict(zip(WEIGHT_ORDER, (m_meta_tokens, m_norm_mix, m_w_in, m_conv_qkv, m_a_log, m_dt_bias, m_head_norm,
                                   m_w_pool, m_pool_scale, m_w_out, m_norm_ffn, m_w_up, m_conv_ffn, m_w_down, m_norm_final)))
    v_in = dict(zip(WEIGHT_ORDER, (v_meta_tokens, v_norm_mix, v_w_in, v_conv_qkv, v_a_log, v_dt_bias, v_head_norm,
                                   v_w_pool, v_pool_scale, v_w_out, v_norm_ffn, v_w_up, v_conv_ffn, v_w_down, v_norm_final)))
    shard_shapes = {k: weights[k].shape for k in SHARDED}
    repl_shapes = {k: weights[k].shape for k in REPLICATED}
    core = lax.axis_index("c").astype(jnp.int32).reshape(1)

    full = unpack_weights(gather_shards(pack_weights({k: weights[k] for k in SHARDED})), shard_shapes)
    layers = []
    for li in range(DEPTH):
        p = {k: (full[k][li] if k in full else weights[k][li]) for k in LAYER_PARAMS}
        layers.append(prep_layer(p))

    h0 = pad_rows(full["meta_tokens"], x[0])
    target = pad_rows(jnp.zeros((N_META, D_MODEL), F32), loss_target[0])
    loss, dh0, grads, d_norm_final = local_step(h0, target, layers, norm_final.reshape(1, D_MODEL))
    seq = x.shape[1]
    grad_x = dh0[LEAD + N_META:LEAD + N_META + seq][None]

    per_layer = [layer_grads(g) for g in grads]
    g_all = {k: jnp.stack([pl_[k] for pl_ in per_layer]) for k in LAYER_PARAMS}
    g_all["meta_tokens"] = dh0[LEAD:LEAD + N_META]
    g_all["norm_final"] = d_norm_final[0]
    packed = pack_grads({k: g_all[k] for k in SHARDED}, {k: g_all[k] for k in REPLICATED})
    g_mine = unpack_grads(all_reduce_to_shards(packed, core), shard_shapes, repl_shapes)

    loss_sum = lax.psum(loss[0, 0], ("x", "y", "c"))
    deltas, new_m, new_v = {}, {}, {}
    for k in WEIGHT_ORDER:
        deltas[k], new_m[k], new_v[k] = adamw(weights[k], g_mine[k], m_in[k], v_in[k], f"adamw_{k}")
    return (loss_sum, grad_x, *[g_mine[k] for k in WEIGHT_ORDER], *[deltas[k] for k in WEIGHT_ORDER],
            *[new_m[k] for k in WEIGHT_ORDER], *[new_v[k] for k in WEIGHT_ORDER])
```

```python
import functools

import jax
import jax.numpy as jnp
from jax import lax
from jax.experimental import pallas as pl
from jax.experimental.pallas import tpu as pltpu

F32 = jnp.float32
BF16 = jnp.bfloat16
WIRE = jnp.bfloat16

D_MODEL = 1024
HEADS = 8
HEAD_DIM = 128
CHUNK = 64
N_META = 16
LEAD = 48
TAIL = 64
QKV_DIM = 3072
D_FF = 2816
POOL_WIDTH = 512
POOL_WINDOWS = (2, 4, 8, 16)
DEPTH = 2
NORM_EPS = 1e-6
ADAM_LR, ADAM_B1, ADAM_B2, ADAM_EPS, ADAM_WD, ADAM_STEP = 0.001, 0.9, 0.999, 1e-08, 0.01, 10
VMEM_LIMIT_BYTES = 48 * 1024 * 1024


def _params(*sem):
    return pltpu.CompilerParams(dimension_semantics=sem if sem else None, vmem_limit_bytes=VMEM_LIMIT_BYTES)


def _tile(n, cap, mult):
    best = None
    for t in range(mult, min(n, cap) + 1, mult):
        if n % t == 0:
            best = t
    assert best is not None, (n, cap, mult)
    return best


def _silu(x):
    return x * jax.nn.sigmoid(x)


def _softplus(x):
    return jnp.maximum(x, 0.0) + jnp.log(1.0 + jnp.exp(-jnp.abs(x)))


def _split_bf16(a):
    hi = a.astype(BF16)
    return hi, (a - hi.astype(F32)).astype(BF16)


def _dg(a, b, ca, cb, hi):
    dims = (((ca,), (cb,)), ((), ()))
    if hi is True:
        return lax.dot_general(a, b, dims, precision=lax.Precision.HIGHEST, preferred_element_type=F32)
    if hi == 3:
        (ah, al), (bh, bl) = _split_bf16(a), _split_bf16(b)
        dot = lambda x, y: lax.dot_general(x, y, dims, preferred_element_type=F32)
        return dot(ah, bh) + (dot(ah, bl) + dot(al, bh))
    return lax.dot_general(a.astype(BF16), b.astype(BF16), dims, preferred_element_type=F32)


def _make_dots(hi):
    @jax.custom_vjp
    def nn(a, b):
        return _dg(a, b, 1, 0, hi)

    @jax.custom_vjp
    def nt(a, b):
        return _dg(a, b, 1, 1, hi)

    @jax.custom_vjp
    def tn(a, b):
        return _dg(a, b, 0, 0, hi)

    nn.defvjp(lambda a, b: (nn(a, b), (a, b)), lambda r, g: (nt(g, r[1]), tn(r[0], g)))
    nt.defvjp(lambda a, b: (nt(a, b), (a, b)), lambda r, g: (nn(g, r[1]), tn(g, r[0])))
    tn.defvjp(lambda a, b: (tn(a, b), (a, b)), lambda r, g: (nt(r[1], g), nn(r[0], g)))
    return nn, nt, tn


_nn, _nt, _tn = _make_dots(False)
_hnn, _hnt, _htn = _make_dots(True)
_nn3, _nt3, _tn3 = _make_dots(3)


def mm(a, b, *, tb=False, add=None, out_dtype=F32, name):
    m, kdim = a.shape
    (n, kb) = b.shape if tb else b.shape[::-1]
    assert kdim == kb, (a.shape, b.shape, tb)
    tm = _tile(m, 1408, 128) if m % 128 == 0 and m <= 4096 else _tile(m, 640, 64)
    tn = _tile(n, 1536, 128)
    tk = kdim if kdim <= 3072 else _tile(kdim, 1664 if a.dtype == b.dtype == BF16 else 640, 128)
    nk = kdim // tk
    dims = (((1,), (1 if tb else 0,)), ((), ()))

    def body(*refs):
        if add is not None:
            a_ref, b_ref, add_ref, o_ref, acc = refs
        else:
            a_ref, b_ref, o_ref, acc = refs
        k = pl.program_id(2)
        part = lax.dot_general(a_ref[...].astype(BF16), b_ref[...].astype(BF16), dims, preferred_element_type=F32)

        def finish(r):
            if add is not None:
                r = r + add_ref[...]
            o_ref[...] = r.astype(out_dtype)

        if nk == 1:
            finish(part)
        else:
            @pl.when(k == 0)
            def _():
                acc[...] = part

            @pl.when(jnp.logical_and(k > 0, k < nk - 1))
            def _():
                acc[...] += part

            @pl.when(k == nk - 1)
            def _():
                finish(acc[...] + part)

    a_spec = pl.BlockSpec((tm, tk), lambda j, i, k: (i, k))
    b_spec = pl.BlockSpec((tn, tk), lambda j, i, k: (j, k)) if tb else pl.BlockSpec((tk, tn), lambda j, i, k: (k, j))
    in_specs = [a_spec, b_spec]
    args = [a, b]
    if add is not None:
        in_specs.append(pl.BlockSpec((tm, tn), lambda j, i, k: (i, j)))
        args.append(add)
    return pl.pallas_call(
        body, grid=(n // tn, m // tm, nk), in_specs=in_specs,
        out_specs=pl.BlockSpec((tm, tn), lambda j, i, k: (i, j)),
        out_shape=jax.ShapeDtypeStruct((m, n), out_dtype),
        scratch_shapes=[pltpu.VMEM((tm, tn) if nk > 1 else (8, 128), F32)],
        compiler_params=_params("parallel", "parallel", "arbitrary"), name=name,
    )(*args)


def _rms(x, gain):
    return x * lax.rsqrt(jnp.mean(x * x, axis=-1, keepdims=True) + NORM_EPS) * gain


def rms_fwd(h, gain, name):
    t = h.shape[0]
    ts = _tile(t, 640, 128)

    def body(h_ref, g_ref, u_ref, ut_ref):
        u = _rms(h_ref[...], g_ref[...])
        u_ref[...] = u.astype(BF16)
        ut_ref[...] = u.T.astype(BF16)

    return pl.pallas_call(
        body, grid=(t // ts,),
        in_specs=[pl.BlockSpec((ts, D_MODEL), lambda i: (i, 0)), pl.BlockSpec((1, D_MODEL), lambda i: (0, 0))],
        out_specs=(pl.BlockSpec((ts, D_MODEL), lambda i: (i, 0)), pl.BlockSpec((D_MODEL, ts), lambda i: (0, i))),
        out_shape=(jax.ShapeDtypeStruct((t, D_MODEL), BF16), jax.ShapeDtypeStruct((D_MODEL, t), BF16)),
        compiler_params=_params("parallel"), name=name,
    )(h, gain)


def rms_bwd(h, gain, du, dres, name):
    t = h.shape[0]
    ts = _tile(t, 640, 64)

    def body(h_ref, g_ref, du_ref, dres_ref, dh_ref, dg_ref):
        i = pl.program_id(0)
        _, vjp = jax.vjp(_rms, h_ref[...], g_ref[...])
        dx, dg = vjp(du_ref[...])
        row = i * ts + lax.broadcasted_iota(jnp.int32, (ts, 1), 0)
        dh_ref[...] = jnp.where(row >= LEAD, dx + dres_ref[...], 0.0)

        @pl.when(i == 0)
        def _():
            dg_ref[...] = jnp.zeros_like(dg_ref)

        dg_ref[...] += dg

    row_spec = pl.BlockSpec((ts, D_MODEL), lambda i: (i, 0))
    vec_spec = pl.BlockSpec((1, D_MODEL), lambda i: (0, 0))
    return pl.pallas_call(
        body, grid=(t // ts,), in_specs=[row_spec, vec_spec, row_spec, row_spec],
        out_specs=(row_spec, vec_spec),
        out_shape=(jax.ShapeDtypeStruct((t, D_MODEL), F32), jax.ShapeDtypeStruct((1, D_MODEL), F32)),
        compiler_params=_params("arbitrary"), name=name,
    )(h, gain, du, dres)


def loss_head(h, gain, target, name):
    t = h.shape[0]
    ts = _tile(t, 640, 64)

    def body(h_ref, g_ref, t_ref, loss_ref, dh_ref, dg_ref):
        i = pl.program_id(0)
        row = i * ts + lax.broadcasted_iota(jnp.int32, (ts, 1), 0)
        keep = jnp.logical_and(row >= LEAD + N_META, row < t - TAIL)
        tgt = t_ref[...]

        def f(x, g):
            err = jnp.where(keep, _rms(x, g) - tgt, 0.0)
            per_row = jnp.mean(err * err, axis=-1, keepdims=True)
            return 0.5 * jnp.sum(per_row, axis=0, keepdims=True)

        val, vjp = jax.vjp(f, h_ref[...], g_ref[...])
        dx, dg = vjp(jnp.ones((1, 1), F32))
        dh_ref[...] = dx

        @pl.when(i == 0)
        def _():
            dg_ref[...] = jnp.zeros_like(dg_ref)
            loss_ref[...] = jnp.zeros_like(loss_ref)

        dg_ref[...] += dg
        loss_ref[...] += jnp.broadcast_to(val, (1, 128))

    row_spec = pl.BlockSpec((ts, D_MODEL), lambda i: (i, 0))
    vec_spec = pl.BlockSpec((1, D_MODEL), lambda i: (0, 0))
    return pl.pallas_call(
        body, grid=(t // ts,), in_specs=[row_spec, vec_spec, row_spec],
        out_specs=(pl.BlockSpec((1, 128), lambda i: (0, 0)), row_spec, vec_spec),
        out_shape=(jax.ShapeDtypeStruct((1, 128), F32), jax.ShapeDtypeStruct((t, D_MODEL), F32),
                   jax.ShapeDtypeStruct((1, D_MODEL), F32)),
        compiler_params=_params("arbitrary"), name=name,
    )(h, gain, target)


CONV_ROWS = 16
CONV_LANES = 512


def conv_fwd(x, w, name):
    t, width = x.shape
    k = w.shape[0]
    ts = _tile(t, 640, 64)
    tw = _tile(width, CONV_LANES, 128)
    hb = ts // 8

    def body(x_ref, halo_ref, w_ref, o_ref, buf):
        i = pl.program_id(0)
        buf[0:8, :] = jnp.where(i > 0, halo_ref[...], 0.0)
        buf[8:, :] = x_ref[...]
        taps = [w_ref[j:j + 1, :] for j in range(k)]

        def sub_block(r, carry):
            r0 = pl.multiple_of(r * CONV_ROWS, CONV_ROWS)
            blk = buf[pl.ds(r0, CONV_ROWS + 8), :]
            acc = blk[8 - (k - 1):8 - (k - 1) + CONV_ROWS] * taps[0]
            for j in range(1, k):
                acc = acc + blk[8 - (k - 1) + j:8 - (k - 1) + j + CONV_ROWS] * taps[j]
            o_ref[pl.ds(r0, CONV_ROWS), :] = acc
            return carry

        lax.fori_loop(0, ts // CONV_ROWS, sub_block, 0)

    return pl.pallas_call(
        body, grid=(t // ts, width // tw),
        in_specs=[pl.BlockSpec((ts, tw), lambda i, j: (i, j)),
                  pl.BlockSpec((8, tw), lambda i, j: (jnp.maximum(i * hb - 1, 0), j)),
                  pl.BlockSpec((k, tw), lambda i, j: (0, j))],
        out_specs=pl.BlockSpec((ts, tw), lambda i, j: (i, j)),
        out_shape=jax.ShapeDtypeStruct((t, width), F32),
        scratch_shapes=[pltpu.VMEM((ts + 8, tw), F32)],
        compiler_params=_params("parallel", "parallel"), name=name,
    )(x, x, w)


def conv_bwd(x, dc, w, name):
    t, width = x.shape
    k = w.shape[0]
    ts = _tile(t, 640, 64)
    tw = _tile(width, CONV_LANES, 128)
    hb = ts // 8
    nt = t // ts

    def body(x_ref, xh_ref, dc_ref, dch_ref, w_ref, dx_ref, dw_ref, xbuf, dbuf, wacc):
        i = pl.program_id(1)
        xbuf[0:8, :] = jnp.where(i > 0, xh_ref[...], 0.0)
        xbuf[8:, :] = x_ref[...]
        dbuf[0:ts, :] = dc_ref[...]
        dbuf[ts:, :] = jnp.where(i < nt - 1, dch_ref[...], 0.0)
        taps = [w_ref[j:j + 1, :] for j in range(k)]

        @pl.when(i == 0)
        def _():
            wacc[...] = jnp.zeros_like(wacc)

        def sub_block(r, carry):
            r0 = pl.multiple_of(r * CONV_ROWS, CONV_ROWS)
            xb = xbuf[pl.ds(r0, CONV_ROWS + 8), :]
            db = dbuf[pl.ds(r0, CONV_ROWS + 8), :]
            acc = db[k - 1:k - 1 + CONV_ROWS] * taps[0]
            for j in range(1, k):
                acc = acc + db[k - 1 - j:k - 1 - j + CONV_ROWS] * taps[j]
            dx_ref[pl.ds(r0, CONV_ROWS), :] = acc.astype(BF16)
            d = db[0:CONV_ROWS]
            for j in range(k):
                prod = d * xb[8 - (k - 1) + j:8 - (k - 1) + j + CONV_ROWS]
                part = prod[0:8]
                for s in range(8, CONV_ROWS, 8):
                    part = part + prod[s:s + 8]
                wacc[j] += part
            return carry

        lax.fori_loop(0, ts // CONV_ROWS, sub_block, 0)

        @pl.when(i == nt - 1)
        def _():
            rows = [jnp.sum(wacc[j], axis=0, keepdims=True) for j in range(k)]
            dw_ref[...] = jnp.concatenate(rows + [jnp.zeros((8 - k, tw), F32)], axis=0)

    return pl.pallas_call(
        body, grid=(width // tw, nt),
        in_specs=[pl.BlockSpec((ts, tw), lambda j, i: (i, j)),
                  pl.BlockSpec((8, tw), lambda j, i: (jnp.maximum(i * hb - 1, 0), j)),
                  pl.BlockSpec((ts, tw), lambda j, i: (i, j)),
                  pl.BlockSpec((8, tw), lambda j, i: (jnp.minimum((i + 1) * hb, t // 8 - 1), j)),
                  pl.BlockSpec((k, tw), lambda j, i: (0, j))],
        out_specs=(pl.BlockSpec((ts, tw), lambda j, i: (i, j)), pl.BlockSpec((8, tw), lambda j, i: (0, j))),
        out_shape=(jax.ShapeDtypeStruct((t, width), BF16), jax.ShapeDtypeStruct((8, width), F32)),
        scratch_shapes=[pltpu.VMEM((ts + 8, tw), F32), pltpu.VMEM((ts + 8, tw), F32), pltpu.VMEM((k, 8, tw), F32)],
        compiler_params=_params("parallel", "arbitrary"), name=name,
    )(x, x, dc, dc, w)


def _pool_count(pos, win):
    return jnp.clip(pos + 1, 1, win).astype(F32)


def poolwin_fwd(p, name):
    t = p.shape[0]
    ts = _tile(t, 640, 64)
    hb = ts // 16

    def body(p_ref, halo_ref, o_ref, buf):
        i = pl.program_id(0)
        buf[0:16, :] = jnp.where(i > 0, halo_ref[...], 0.0)
        buf[16:, :] = p_ref[...]
        pos = i * ts + lax.broadcasted_iota(jnp.int32, (ts, 1), 0) - LEAD
        for gi, win in enumerate(POOL_WINDOWS):
            cols = slice(gi * 128, (gi + 1) * 128)
            own = buf[pl.ds(16, ts), cols]
            acc = own
            for j in range(1, win):
                acc = acc + buf[pl.ds(16 - j, ts), cols]
            o_ref[:, cols] = acc / _pool_count(pos, win) - own

    return pl.pallas_call(
        body, grid=(t // ts,),
        in_specs=[pl.BlockSpec((ts, POOL_WIDTH), lambda i: (i, 0)),
                  pl.BlockSpec((16, POOL_WIDTH), lambda i: (jnp.maximum(i * hb - 1, 0), 0))],
        out_specs=pl.BlockSpec((ts, POOL_WIDTH), lambda i: (i, 0)),
        out_shape=jax.ShapeDtypeStruct((t, POOL_WIDTH), F32),
        scratch_shapes=[pltpu.VMEM((ts + 16, POOL_WIDTH), F32)],
        compiler_params=_params("parallel"), name=name,
    )(p, p)


def poolwin_bwd(dpooled, name):
    t = dpooled.shape[0]
    ts = _tile(t, 640, 64)
    hb = ts // 16
    nt = t // ts

    def body(d_ref, halo_ref, o_ref, buf):
        i = pl.program_id(0)
        buf[0:ts, :] = d_ref[...]
        buf[ts:, :] = jnp.where(i < nt - 1, halo_ref[...], 0.0)
        pos = i * ts + lax.broadcasted_iota(jnp.int32, (ts, 1), 0) - LEAD
        for gi, win in enumerate(POOL_WINDOWS):
            cols = slice(gi * 128, (gi + 1) * 128)
            own = buf[pl.ds(0, ts), cols]
            acc = own / _pool_count(pos, win)
            for j in range(1, win):
                acc = acc + buf[pl.ds(j, ts), cols] / _pool_count(pos + j, win)
            o_ref[:, cols] = (acc - own).astype(BF16)

    return pl.pallas_call(
        body, grid=(nt,),
        in_specs=[pl.BlockSpec((ts, POOL_WIDTH), lambda i: (i, 0)),
                  pl.BlockSpec((16, POOL_WIDTH), lambda i: (jnp.minimum((i + 1) * hb, t // 16 - 1), 0))],
        out_specs=pl.BlockSpec((ts, POOL_WIDTH), lambda i: (i, 0)),
        out_shape=jax.ShapeDtypeStruct((t, POOL_WIDTH), BF16),
        scratch_shapes=[pltpu.VMEM((ts + 16, POOL_WIDTH), F32)],
        compiler_params=_params("parallel"), name=name,
    )(dpooled, dpooled)


def _mix(y_a, gpre, pooled, w_pool, scale):
    parts = [_nn(pooled[:, g * 128:(g + 1) * 128], w_pool[g]) for g in range(4)]
    y_b = jnp.concatenate(parts, axis=1) * scale
    return jax.nn.sigmoid(gpre[:, :D_MODEL]) * y_a + jax.nn.sigmoid(gpre[:, D_MODEL:]) * y_b


def _mix_specs(ts):
    return [pl.BlockSpec((ts, D_MODEL), lambda i: (i, 0)), pl.BlockSpec((ts, 2 * D_MODEL), lambda i: (i, 0)),
            pl.BlockSpec((ts, POOL_WIDTH), lambda i: (i, 0)), pl.BlockSpec((4, 128, 256), lambda i: (0, 0, 0)),
            pl.BlockSpec((1, D_MODEL), lambda i: (0, 0))]


def mix_fwd(y_a, gpre, pooled, w_pool, scale, name):
    t = y_a.shape[0]
    ts = _tile(t, 640, 128)

    def body(ya_ref, g_ref, p_ref, w_ref, s_ref, o_ref, ot_ref):
        y = _mix(ya_ref[...], g_ref[...], p_ref[...], w_ref[...], s_ref[...])
        o_ref[...] = y.astype(BF16)
        ot_ref[...] = y.T.astype(BF16)

    return pl.pallas_call(
        body, grid=(t // ts,), in_specs=_mix_specs(ts),
        out_specs=(pl.BlockSpec((ts, D_MODEL), lambda i: (i, 0)), pl.BlockSpec((D_MODEL, ts), lambda i: (0, i))),
        out_shape=(jax.ShapeDtypeStruct((t, D_MODEL), BF16), jax.ShapeDtypeStruct((D_MODEL, t), BF16)),
        compiler_params=_params("parallel"), name=name,
    )(y_a, gpre, pooled, w_pool, scale)


def mix_bwd(y_a, gpre, pooled, w_pool, scale, dy, name):
    t = y_a.shape[0]
    ts = _tile(t, 320, 64)

    def body(ya_ref, g_ref, p_ref, w_ref, s_ref, dy_ref, dya_ref, dg_ref, dp_ref, dw_ref, ds_ref):
        i = pl.program_id(0)
        _, vjp = jax.vjp(_mix, ya_ref[...], g_ref[...], p_ref[...], w_ref[...], s_ref[...])
        dya, dg, dp, dw, ds = vjp(dy_ref[...])
        dya_ref[...] = dya
        dg_ref[...] = dg.astype(BF16)
        dp_ref[...] = dp

        @pl.when(i == 0)
        def _():
            dw_ref[...] = jnp.zeros_like(dw_ref)
            ds_ref[...] = jnp.zeros_like(ds_ref)

        dw_ref[...] += dw
        ds_ref[...] += ds

    specs = _mix_specs(ts)
    return pl.pallas_call(
        body, grid=(t // ts,), in_specs=specs + [specs[0]],
        out_specs=(specs[0], specs[1], specs[2], specs[3], specs[4]),
        out_shape=(jax.ShapeDtypeStruct((t, D_MODEL), F32), jax.ShapeDtypeStruct((t, 2 * D_MODEL), BF16),
                   jax.ShapeDtypeStruct((t, POOL_WIDTH), F32), jax.ShapeDtypeStruct((4, 128, 256), F32),
                   jax.ShapeDtypeStruct((1, D_MODEL), F32)),
        compiler_params=_params("arbitrary"), name=name,
    )(y_a, gpre, pooled, w_pool, scale, dy)


def _ffn_act(cg, cv):
    return _silu(cg) * cv


def ffnact_fwd(cg, cv, name):
    t, width = cg.shape
    ts = _tile(t, 640, 128)
    tw = _tile(width, 1536, 128)
    spec = pl.BlockSpec((ts, tw), lambda i, j: (i, j))

    def body(g_ref, v_ref, o_ref, ot_ref):
        act = _ffn_act(g_ref[...], v_ref[...])
        o_ref[...] = act.astype(BF16)
        ot_ref[...] = act.T.astype(BF16)

    return pl.pallas_call(
        body, grid=(t // ts, width // tw), in_specs=[spec, spec],
        out_specs=(spec, pl.BlockSpec((tw, ts), lambda i, j: (j, i))),
        out_shape=(jax.ShapeDtypeStruct((t, width), BF16), jax.ShapeDtypeStruct((width, t), BF16)),
        compiler_params=_params("parallel", "parallel"), name=name,
    )(cg, cv)


def ffnact_bwd(cg, cv, dact, name):
    t, width = cg.shape
    ts = _tile(t, 640, 64)
    tw = _tile(width, 1536, 128)
    spec = pl.BlockSpec((ts, tw), lambda i, j: (i, j))

    def body(g_ref, v_ref, d_ref, dg_ref, dv_ref):
        _, vjp = jax.vjp(_ffn_act, g_ref[...], v_ref[...])
        dg_ref[...], dv_ref[...] = vjp(d_ref[...])

    return pl.pallas_call(
        body, grid=(t // ts, width // tw), in_specs=[spec, spec, spec], out_specs=(spec, spec),
        out_shape=(jax.ShapeDtypeStruct((t, width), F32), jax.ShapeDtypeStruct((t, width), F32)),
        compiler_params=_params("parallel", "parallel"), name=name,
    )(cg, cv, dact)


def _gdn_chunk(c, z, ba, pa, pdt, hn, s, *, valid):
    r = lax.broadcasted_iota(jnp.int32, (CHUNK, CHUNK), 0)
    q_ = lax.broadcasted_iota(jnp.int32, (CHUNK, CHUNK), 1)
    causal = r >= q_
    strict = r > q_
    tril = causal.astype(F32)
    triu = (r <= q_).astype(F32)
    eye = (r == q_).astype(F32)
    lane = lax.broadcasted_iota(jnp.int32, (CHUNK, 128), 1)

    decay_log = -jnp.exp(pa) * _softplus(ba + pdt)
    bg = jnp.where(lane < HEADS, jax.nn.sigmoid(ba), jnp.where(lane < 2 * HEADS, decay_log, 0.0))
    bg = jnp.where(valid, bg, 0.0)
    gc = _hnn(tril, bg)
    gct = _hnn(bg.T, triu)
    eg = jnp.exp(gc)
    glast = gc[CHUNK - 1:CHUNK, :]
    ekd = jnp.exp(glast - gc)
    gtot = jnp.exp(glast)

    hd = range(HEADS)
    hs = [slice(h * HEAD_DIM, (h + 1) * HEAD_DIM) for h in hd]
    gl = [slice(HEADS + h, HEADS + h + 1) for h in hd]
    q = [_silu(c[:, hs[h]]) for h in hd]
    k = [_silu(c[:, D_MODEL + h * HEAD_DIM:D_MODEL + (h + 1) * HEAD_DIM]) for h in hd]
    v = [_silu(c[:, 2 * D_MODEL + h * HEAD_DIM:2 * D_MODEL + (h + 1) * HEAD_DIM]) for h in hd]
    q = [q[h] * lax.rsqrt(jnp.sum(q[h] * q[h], axis=-1, keepdims=True) + NORM_EPS) * (HEAD_DIM ** -0.5) for h in hd]
    k = [k[h] * lax.rsqrt(jnp.sum(k[h] * k[h], axis=-1, keepdims=True) + NORM_EPS) for h in hd]
    beta = [bg[:, h:h + 1] for h in hd]
    decay = [jnp.exp(jnp.where(causal, gc[:, gl[h]] - gct[gl[h], :], -1e30)) for h in hd]
    kb = [k[h] * beta[h] for h in hd]
    a = [jnp.where(strict, _nt(kb[h], k[h]) * decay[h], 0.0) for h in hd]
    qk = [jnp.where(causal, _nt(q[h], k[h]) * decay[h], 0.0) for h in hd]
    p = [_nn3(a[h], a[h]) for h in hd]
    x = [(eye - a[h]) + p[h] - _nn(a[h], p[h]) for h in hd]
    for _ in range(4):
        p = [_nn(p[h], p[h]) for h in hd]
        x = [x[h] + p[h] + _nn(x[h] - eye, p[h]) for h in hd]
    u = [_nn(x[h], v[h] * beta[h]) for h in hd]
    w = [_nn(x[h], kb[h] * eg[:, gl[h]]) for h in hd]
    v_new = [u[h] - _nn(w[h], s[h]) for h in hd]
    o = [_nn(q[h] * eg[:, gl[h]], s[h]) + _nn(qk[h], v_new[h]) for h in hd]
    states = [s[h] * gtot[:, gl[h]] + _tn(k[h] * ekd[:, gl[h]], v_new[h]) for h in hd]
    o = [o[h] * lax.rsqrt(jnp.mean(o[h] * o[h], axis=-1, keepdims=True) + NORM_EPS) * hn * _silu(z[:, hs[h]])
         for h in hd]
    return jnp.concatenate(o, axis=1), tuple(states)


def _chunk_valid(n, t):
    row = n * CHUNK + lax.broadcasted_iota(jnp.int32, (CHUNK, 1), 0)
    return jnp.logical_and(row >= LEAD, row < t - TAIL)


def gdn_fwd(c, z, ba, pa, pdt, hn, name):
    t = c.shape[0]
    n_chunks = t // CHUNK

    def body(c_ref, z_ref, ba_ref, pa_ref, pdt_ref, hn_ref, y_ref, ss_ref, state):
        n = pl.program_id(0)

        @pl.when(n == 0)
        def _():
            state[...] = jnp.zeros_like(state)

        s0 = tuple(state[h] for h in range(HEADS))
        for h in range(HEADS):
            ss_ref[0, h] = s0[h]
        y, s1 = _gdn_chunk(c_ref[...], z_ref[...], ba_ref[...], pa_ref[...], pdt_ref[...], hn_ref[...], s0,
                           valid=_chunk_valid(n, t))
        y_ref[...] = y
        for h in range(HEADS):
            state[h] = s1[h]

    vec = pl.BlockSpec((1, 128), lambda n: (0, 0))
    return pl.pallas_call(
        body, grid=(n_chunks,),
        in_specs=[pl.BlockSpec((CHUNK, QKV_DIM), lambda n: (n, 0)), pl.BlockSpec((CHUNK, D_MODEL), lambda n: (n, 0)),
                  pl.BlockSpec((CHUNK, 128), lambda n: (n, 0)), vec, vec, vec],
        out_specs=(pl.BlockSpec((CHUNK, D_MODEL), lambda n: (n, 0)),
                   pl.BlockSpec((1, HEADS, HEAD_DIM, HEAD_DIM), lambda n: (n, 0, 0, 0))),
        out_shape=(jax.ShapeDtypeStruct((t, D_MODEL), F32),
                   jax.ShapeDtypeStruct((n_chunks, HEADS, HEAD_DIM, HEAD_DIM), F32)),
        scratch_shapes=[pltpu.VMEM((HEADS, HEAD_DIM, HEAD_DIM), F32)],
        compiler_params=_params("arbitrary"), name=name,
    )(c, z, ba, pa, pdt, hn)


def gdn_bwd(c, z, ba, pa, pdt, hn, starts, dy, name):
    t = c.shape[0]
    n_chunks = t // CHUNK

    def body(c_ref, z_ref, ba_ref, pa_ref, pdt_ref, hn_ref, ss_ref, dy_ref,
             dc_ref, dz_ref, dba_ref, dpa_ref, dpdt_ref, dhn_ref, dstate):
        step = pl.program_id(0)
        n = n_chunks - 1 - step

        @pl.when(step == 0)
        def _():
            dstate[...] = jnp.zeros_like(dstate)
            dpa_ref[...] = jnp.zeros_like(dpa_ref)
            dpdt_ref[...] = jnp.zeros_like(dpdt_ref)
            dhn_ref[...] = jnp.zeros_like(dhn_ref)

        f = functools.partial(_gdn_chunk, valid=_chunk_valid(n, t))
        _, vjp = jax.vjp(f, c_ref[...], z_ref[...], ba_ref[...], pa_ref[...], pdt_ref[...], hn_ref[...],
                         tuple(ss_ref[0, h] for h in range(HEADS)))
        dc, dz, dba, dpa, dpdt, dhn, ds = vjp((dy_ref[...], tuple(dstate[h] for h in range(HEADS))))
        dc_ref[...] = dc
        dz_ref[...] = dz.astype(BF16)
        dba_ref[...] = dba.astype(BF16)
        dpa_ref[...] += dpa
        dpdt_ref[...] += dpdt
        dhn_ref[...] += dhn
        for h in range(HEADS):
            dstate[h] = ds[h]

    def rev(width):
        return pl.BlockSpec((CHUNK, width), lambda s: (n_chunks - 1 - s, 0))

    vec = pl.BlockSpec((1, 128), lambda s: (0, 0))
    vec_shape = jax.ShapeDtypeStruct((1, 128), F32)
    return pl.pallas_call(
        body, grid=(n_chunks,),
        in_specs=[rev(QKV_DIM), rev(D_MODEL), rev(128), vec, vec, vec,
                  pl.BlockSpec((1, HEADS, HEAD_DIM, HEAD_DIM), lambda s: (n_chunks - 1 - s, 0, 0, 0)), rev(D_MODEL)],
        out_specs=(rev(QKV_DIM), rev(D_MODEL), rev(128), vec, vec, vec),
        out_shape=(jax.ShapeDtypeStruct((t, QKV_DIM), F32), jax.ShapeDtypeStruct((t, D_MODEL), BF16),
                   jax.ShapeDtypeStruct((t, 128), BF16), vec_shape, vec_shape, vec_shape),
        scratch_shapes=[pltpu.VMEM((HEADS, HEAD_DIM, HEAD_DIM), F32)],
        compiler_params=_params("arbitrary"), name=name,
    )(c, z, ba, pa, pdt, hn, starts, dy)


def _layer_fwd(h, w, tag):
    u, ut = rms_fwd(h, w["norm_mix"], f"{tag}_rms_mix")
    pq = mm(u, w["wqkv"], name=f"{tag}_mm_qkv")
    pz = mm(u, w["wz"], name=f"{tag}_mm_z")
    pg = mm(u, w["wg"], name=f"{tag}_mm_gate")
    pp = mm(u, w["wpl"], name=f"{tag}_mm_pool")
    pba = mm(u, w["wba"], name=f"{tag}_mm_ba")
    cq = conv_fwd(pq, w["conv_qkv"], f"{tag}_conv_qkv")
    ya, starts = gdn_fwd(cq, pz, pba, w["pa"], w["pdt"], w["head_norm"], f"{tag}_gdn")
    pooled = poolwin_fwd(pp, f"{tag}_poolwin")
    y, yt = mix_fwd(ya, pg, pooled, w["w_pool"], w["pool_scale"], f"{tag}_mix")
    h1 = mm(y, w["wout"], add=h, name=f"{tag}_mm_out")
    u2, u2t = rms_fwd(h1, w["norm_ffn"], f"{tag}_rms_ffn")
    hg = mm(u2, w["wupg"], name=f"{tag}_mm_upg")
    hv = mm(u2, w["wupv"], name=f"{tag}_mm_upv")
    cg = conv_fwd(hg, w["conv_g"], f"{tag}_conv_g")
    cv = conv_fwd(hv, w["conv_v"], f"{tag}_conv_v")
    act, actt = ffnact_fwd(cg, cv, f"{tag}_act")
    h2 = mm(act, w["wdown"], add=h1, name=f"{tag}_mm_down")
    saved = dict(h=h, ut=ut, pq=pq, pz=pz, pg=pg, pba=pba, cq=cq, ya=ya, starts=starts, pooled=pooled, yt=yt, h1=h1,
                 u2t=u2t, hg=hg, hv=hv, cg=cg, cv=cv, actt=actt)
    return h2, saved


def _layer_bwd(dh2, w, s, tag):
    g = {}
    dact = mm(dh2, w["wdown"], tb=True, name=f"{tag}_bmm_down_x")
    g["wdown"] = mm(s["actt"], dh2, name=f"{tag}_bmm_down_w")
    dcg, dcv = ffnact_bwd(s["cg"], s["cv"], dact, f"{tag}_act_b")
    dhg, g["conv_g"] = conv_bwd(s["hg"], dcg, w["conv_g"], f"{tag}_conv_g_b")
    dhv, g["conv_v"] = conv_bwd(s["hv"], dcv, w["conv_v"], f"{tag}_conv_v_b")
    du2 = mm(dhg, w["wupg"], tb=True, name=f"{tag}_bmm_upg_x")
    du2 = mm(dhv, w["wupv"], tb=True, add=du2, name=f"{tag}_bmm_upv_x")
    g["wupg"] = mm(s["u2t"], dhg, name=f"{tag}_bmm_upg_w")
    g["wupv"] = mm(s["u2t"], dhv, name=f"{tag}_bmm_upv_w")
    dh1, g["norm_ffn"] = rms_bwd(s["h1"], w["norm_ffn"], du2, dh2, f"{tag}_rms_ffn_b")
    dy = mm(dh1, w["wout"], tb=True, name=f"{tag}_bmm_out_x")
    g["wout"] = mm(s["yt"], dh1, name=f"{tag}_bmm_out_w")
    dya, dpg, dpooled, g["w_pool"], g["pool_scale"] = mix_bwd(
        s["ya"], s["pg"], s["pooled"], w["w_pool"], w["pool_scale"], dy, f"{tag}_mix_b")
    dpp = poolwin_bwd(dpooled, f"{tag}_poolwin_b")
    dcq, dpz, dpba, g["pa"], g["pdt"], g["head_norm"] = gdn_bwd(
        s["cq"], s["pz"], s["pba"], w["pa"], w["pdt"], w["head_norm"], s["starts"], dya, f"{tag}_gdn_b")
    dpq, g["conv_qkv"] = conv_bwd(s["pq"], dcq, w["conv_qkv"], f"{tag}_conv_qkv_b")
    du = mm(dpq, w["wqkv"], tb=True, name=f"{tag}_bmm_qkv_x")
    du = mm(dpz, w["wz"], tb=True, add=du, name=f"{tag}_bmm_z_x")
    du = mm(dpg, w["wg"], tb=True, add=du, name=f"{tag}_bmm_gate_x")
    du = mm(dpp, w["wpl"], tb=True, add=du, name=f"{tag}_bmm_pool_x")
    du = mm(dpba, w["wba"], tb=True, add=du, name=f"{tag}_bmm_ba_x")
    g["wqkv"] = mm(s["ut"], dpq, name=f"{tag}_bmm_qkv_w")
    g["wz"] = mm(s["ut"], dpz, name=f"{tag}_bmm_z_w")
    g["wg"] = mm(s["ut"], dpg, name=f"{tag}_bmm_gate_w")
    g["wpl"] = mm(s["ut"], dpp, name=f"{tag}_bmm_pool_w")
    g["wba"] = mm(s["ut"], dpba, name=f"{tag}_bmm_ba_w")
    dh, g["norm_mix"] = rms_bwd(s["h"], w["norm_mix"], du, dh1, f"{tag}_rms_mix_b")
    return dh, g


def local_step(h0, target, layers, norm_final):
    h = h0
    saved = []
    for li, w in enumerate(layers):
        h, s = _layer_fwd(h, w, f"l{li}")
        saved.append(s)
    loss, dh, dnf = loss_head(h, norm_final, target, "loss_head")
    grads = [None] * len(layers)
    for li in reversed(range(len(layers))):
        dh, grads[li] = _layer_bwd(dh, layers[li], saved[li], f"l{li}")
    return loss, dh, grads, dnf


_Z0, _B0, _P0, _G0, _IN_DIM = 3072, 4096, 4112, 4624, 6672


def _lanes_8_to_15(v):
    return jnp.pad(v.reshape(1, HEADS).astype(F32), ((0, 0), (HEADS, 128 - 2 * HEADS)))


def prep_layer(p):
    w_in = p["w_in"].astype(BF16)
    w_up = p["w_up"].astype(BF16)
    row = lambda v: v.reshape(1, -1).astype(F32)
    return dict(
        wqkv=w_in[:, :_Z0], wz=w_in[:, _Z0:_B0], wba=jnp.pad(w_in[:, _B0:_P0], ((0, 0), (0, 128 - 2 * HEADS))),
        wpl=w_in[:, _P0:_G0], wg=w_in[:, _G0:], wout=p["w_out"].astype(BF16),
        wupg=w_up[:, :D_FF], wupv=w_up[:, D_FF:], wdown=p["w_down"].astype(BF16),
        conv_qkv=p["conv_qkv"].astype(F32), conv_g=p["conv_ffn"][:, :D_FF].astype(F32),
        conv_v=p["conv_ffn"][:, D_FF:].astype(F32), w_pool=p["w_pool"].astype(F32),
        pool_scale=row(p["pool_scale"]), head_norm=row(p["head_norm"]), norm_mix=row(p["norm_mix"]),
        norm_ffn=row(p["norm_ffn"]), pa=_lanes_8_to_15(p["a_log"]), pdt=_lanes_8_to_15(p["dt_bias"]))


def layer_grads(g):
    return dict(
        w_in=jnp.concatenate([g["wqkv"], g["wz"], g["wba"][:, :2 * HEADS], g["wpl"], g["wg"]], axis=1),
        conv_qkv=g["conv_qkv"][:4], a_log=g["pa"][0, HEADS:2 * HEADS], dt_bias=g["pdt"][0, HEADS:2 * HEADS],
        head_norm=g["head_norm"][0], w_pool=g["w_pool"], pool_scale=g["pool_scale"][0], w_out=g["wout"],
        norm_mix=g["norm_mix"][0], norm_ffn=g["norm_ffn"][0],
        w_up=jnp.concatenate([g["wupg"], g["wupv"]], axis=1),
        conv_ffn=jnp.concatenate([g["conv_g"][:3], g["conv_v"][:3]], axis=1), w_down=g["wdown"])


LAYER_PARAMS = ("norm_mix", "w_in", "conv_qkv", "a_log", "dt_bias", "head_norm", "w_pool", "pool_scale", "w_out",
                "norm_ffn", "w_up", "conv_ffn", "w_down")


def pad_rows(meta, x):
    return jnp.concatenate([jnp.zeros((LEAD, D_MODEL), F32), meta.astype(F32), x.astype(F32),
                            jnp.zeros((TAIL, D_MODEL), F32)], axis=0)


MESH = pl.DeviceIdType.MESH
ANY = pl.BlockSpec(memory_space=pl.ANY)


def _place():
    x, y, c = lax.axis_index("x"), lax.axis_index("y"), lax.axis_index("c")
    return x, y, c, [(1 - x, y), (x, 1 - y), (1 - x, 1 - y)]


def _my_chip():
    return 2 * lax.axis_index("x") + lax.axis_index("y")


def gather_shards(pack):
    _, rows, lanes = pack.shape

    def body(p_ref, o_ref, send_sems, recv_sems):
        x, y, c, chips = _place()

        def copy(k, chip, half, to, src=None):
            dst = o_ref.at[2 * chip[0] + chip[1], half]
            return pltpu.make_async_remote_copy(src_ref=dst if src is None else src, dst_ref=dst,
                                                send_sem=send_sems.at[k], recv_sem=recv_sems.at[k],
                                                device_id=to, device_id_type=MESH)

        first = [copy(j, (x, y), c, (*chip, c), src=p_ref.at[c]) for j, chip in enumerate(chips)]
        for cp in first:
            cp.start()
        passed = [copy(3 + j, chip, c, (x, y, 1 - c)) for j, chip in enumerate(chips)]
        for j, chip in enumerate(chips):
            copy(j, chip, c, (x, y, c)).wait_recv()
            passed[j].start()
        for j, chip in enumerate(chips):
            copy(3 + j, chip, 1 - c, (x, y, c)).wait_recv()
        for cp in first + passed:
            cp.wait_send()

    gathered = pl.pallas_call(
        body, in_specs=[ANY], out_specs=ANY,
        out_shape=jax.ShapeDtypeStruct((4, 2, rows, lanes), pack.dtype),
        scratch_shapes=[pltpu.SemaphoreType.DMA((6,)), pltpu.SemaphoreType.DMA((6,))],
        name="gather_shards",
    )(pack)
    return lax.dynamic_update_slice(gathered, pack[None], (_my_chip(), 0, 0, 0))


def swap_other_halves(p):
    _, _, rows, lanes = p.shape

    def body(p_ref, o_ref, send_sems, recv_sems):
        x, y, c, _ = _place()
        copies = [pltpu.make_async_remote_copy(src_ref=p_ref.at[s, 1 - c], dst_ref=o_ref.at[s],
                                               send_sem=send_sems.at[s], recv_sem=recv_sems.at[s],
                                               device_id=(x, y, 1 - c), device_id_type=MESH) for s in range(4)]
        for cp in copies:
            cp.start()
        for cp in copies:
            cp.wait()

    return pl.pallas_call(
        body, in_specs=[ANY], out_specs=ANY, out_shape=jax.ShapeDtypeStruct((4, rows, lanes), p.dtype),
        scratch_shapes=[pltpu.SemaphoreType.DMA((4,)), pltpu.SemaphoreType.DMA((4,))], name="swap_other_halves",
    )(p)


def scatter_to_chips(q):
    _, rows, lanes = q.shape

    def body(q_ref, o_ref, send_sems, recv_sems):
        x, y, c, chips = _place()
        me = 2 * x + y
        copies = [pltpu.make_async_remote_copy(src_ref=q_ref.at[2 * chip[0] + chip[1]], dst_ref=o_ref.at[me],
                                               send_sem=send_sems.at[j], recv_sem=recv_sems.at[j],
                                               device_id=(*chip, c), device_id_type=MESH)
                  for j, chip in enumerate(chips)]
        for cp in copies:
            cp.start()
        for j, chip in enumerate(chips):
            slot = o_ref.at[2 * chip[0] + chip[1]]
            pltpu.make_async_remote_copy(src_ref=slot, dst_ref=slot, send_sem=send_sems.at[j], recv_sem=recv_sems.at[j],
                                         device_id=(x, y, c), device_id_type=MESH).wait_recv()
        for cp in copies:
            cp.wait_send()

    received = pl.pallas_call(
        body, in_specs=[ANY], out_specs=ANY, out_shape=jax.ShapeDtypeStruct((4, rows, lanes), q.dtype),
        scratch_shapes=[pltpu.SemaphoreType.DMA((3,)), pltpu.SemaphoreType.DMA((3,))],
        name="scatter_to_chips",
    )(q)
    me = _my_chip()
    return lax.dynamic_update_slice(received, lax.dynamic_slice_in_dim(q, me, 1, axis=0), (me, 0, 0))


def join_halves(both):
    def body(_, o_ref, send_sem, recv_sem):
        x, y, c, _unused = _place()
        cp = pltpu.make_async_remote_copy(src_ref=o_ref.at[c], dst_ref=o_ref.at[c], send_sem=send_sem,
                                          recv_sem=recv_sem, device_id=(x, y, 1 - c), device_id_type=MESH)
        cp.start()
        other = o_ref.at[1 - c]
        pltpu.make_async_remote_copy(src_ref=other, dst_ref=other, send_sem=send_sem, recv_sem=recv_sem,
                                     device_id=(x, y, c), device_id_type=MESH).wait_recv()
        cp.wait_send()

    return pl.pallas_call(
        body, in_specs=[ANY], out_specs=ANY, out_shape=jax.ShapeDtypeStruct(both.shape, both.dtype),
        input_output_aliases={0: 0},
        scratch_shapes=[pltpu.SemaphoreType.DMA, pltpu.SemaphoreType.DMA], name="join_halves",
    )(both)


def add_own_half(p, other, c):
    _, _, rows, lanes = p.shape
    tr = _tile(rows, 4096, 8)

    def body(c_ref, p_ref, o_ref, out_ref):
        out_ref[...] = p_ref[...] + o_ref[...]

    return pl.pallas_call(
        body,
        grid_spec=pltpu.PrefetchScalarGridSpec(
            num_scalar_prefetch=1, grid=(4, rows // tr),
            in_specs=[pl.BlockSpec((None, None, tr, lanes), lambda s, i, c_ref: (s, c_ref[0], i, 0)),
                      pl.BlockSpec((None, tr, lanes), lambda s, i, c_ref: (s, i, 0))],
            out_specs=pl.BlockSpec((None, tr, lanes), lambda s, i, c_ref: (s, i, 0))),
        out_shape=jax.ShapeDtypeStruct((4, rows, lanes), F32),
        compiler_params=_params("parallel", "parallel"), name="add_own_half",
    )(c, p, other)


def sum_chips(b, c):
    _, rows, lanes = b.shape
    tr = _tile(rows, 4096, 8)

    def body(c_ref, b_ref, out_ref):
        out_ref[...] = ((b_ref[0] + b_ref[1]) + b_ref[2]) + b_ref[3]

    return pl.pallas_call(
        body,
        grid_spec=pltpu.PrefetchScalarGridSpec(
            num_scalar_prefetch=1, grid=(rows // tr,),
            in_specs=[pl.BlockSpec((4, tr, lanes), lambda i, c_ref: (0, i, 0))],
            out_specs=pl.BlockSpec((None, tr, lanes), lambda i, c_ref: (c_ref[0], i, 0))),
        out_shape=jax.ShapeDtypeStruct((2, rows, lanes), F32),
        compiler_params=_params("parallel"), name="sum_chips",
    )(c, b)


def all_reduce_to_shards(p, c):
    q = add_own_half(p, swap_other_halves(p), c)
    both = join_halves(sum_chips(scatter_to_chips(q), c))
    return both.reshape(2 * both.shape[1], both.shape[2])


def adamw(w, g, m, v, name):
    shape = w.shape
    cols = shape[-1]
    w2, g2, m2, v2 = (a.reshape(-1, cols) for a in (w, g, m, v))
    rows = w2.shape[0]
    tr = _tile(rows, max(8, 262144 // cols), 8) if rows % 8 == 0 else rows
    c1 = 1.0 - ADAM_B1 ** ADAM_STEP
    c2 = 1.0 - ADAM_B2 ** ADAM_STEP

    def body(w_ref, g_ref, m_ref, v_ref, d_ref, mo_ref, vo_ref):
        gv = g_ref[...]
        mn = ADAM_B1 * m_ref[...] + (1.0 - ADAM_B1) * gv
        vn = ADAM_B2 * v_ref[...] + (1.0 - ADAM_B2) * jnp.square(gv)
        d_ref[...] = -ADAM_LR * ((mn / c1) / (jnp.sqrt(vn / c2) + ADAM_EPS) + ADAM_WD * w_ref[...])
        mo_ref[...] = mn
        vo_ref[...] = vn

    spec = pl.BlockSpec((tr, cols), lambda i: (i, 0))
    out = jax.ShapeDtypeStruct((rows, cols), F32)
    d, mn, vn = pl.pallas_call(
        body, grid=(rows // tr,), in_specs=[spec] * 4, out_specs=(spec,) * 3, out_shape=(out,) * 3,
        compiler_params=_params("parallel"), name=name,
    )(w2, g2, m2, v2)
    return d.reshape(shape), mn.reshape(shape), vn.reshape(shape)


SHARDED = ("w_in", "w_up", "w_out", "w_down", "w_pool", "conv_qkv", "conv_ffn", "meta_tokens")
MATMUL_WEIGHTS = ("w_in", "w_up", "w_out", "w_down", "w_pool")
REPLICATED = ("norm_mix", "a_log", "dt_bias", "head_norm", "pool_scale", "norm_ffn", "norm_final")
SHARD_AXIS = {"w_in": 2, "w_up": 2, "w_out": 1, "w_down": 1, "w_pool": 3, "conv_qkv": 2, "conv_ffn": 2, "meta_tokens": 1}


def _rows_of(a):
    return a.reshape(-1, 128)


SEGMENT_ROWS = 16


def _segment(n_rows):
    return -(-n_rows // SEGMENT_ROWS) * SEGMENT_ROWS


def _pad_segment(a):
    pad = [(0, 0)] * a.ndim
    pad[-2] = (0, _segment(a.shape[-2]) - a.shape[-2])
    return jnp.pad(a, pad)


def _unshard(stacked, axis):
    full = jnp.moveaxis(stacked, 0, axis)
    shape = list(full.shape)
    shape[axis:axis + 2] = [shape[axis] * shape[axis + 1]]
    return full.reshape(shape)


def _shard_stack(full, axis):
    shape = list(full.shape)
    shape[axis:axis + 1] = [4, shape[axis] // 4]
    return jnp.moveaxis(full.reshape(shape), axis, 0)


def pack_weights(shards):
    parts = [_rows_of(shards[k].astype(WIRE)) for k in MATMUL_WEIGHTS]
    parts += [lax.bitcast_convert_type(_rows_of(shards[k].astype(F32)), WIRE).reshape(-1, 128)
              for k in SHARDED if k not in MATMUL_WEIGHTS]
    parts = [_pad_segment(p) for p in parts]
    rows = sum(p.shape[0] for p in parts)
    if rows % (2 * SEGMENT_ROWS):
        parts.append(jnp.zeros((SEGMENT_ROWS, 128), WIRE))
        rows += SEGMENT_ROWS
    return jnp.concatenate(parts, axis=0).reshape(2, rows // 2, 128)


def unpack_weights(gathered, shard_shapes):
    flat = gathered.reshape(4, -1, 128)
    out, at = {}, 0
    for k in SHARDED:
        shp = shard_shapes[k]
        n = 1
        for e in shp:
            n *= e
        if k in MATMUL_WEIGHTS:
            r = n // 128
            stacked = flat[:, at:at + r].reshape((4,) + tuple(shp))
        else:
            r = 2 * n // 128
            stacked = lax.bitcast_convert_type(flat[:, at:at + r].reshape(4, n // 128, 128, 2), F32).reshape((4,) + tuple(shp))
        out[k] = _unshard(stacked, SHARD_AXIS[k])
        at += _segment(r)
    return out


def pack_grads(full, repl):
    parts = [_shard_stack(full[k], SHARD_AXIS[k]).reshape(4, -1, 128) for k in SHARDED]
    r = jnp.concatenate([repl[k].reshape(-1) for k in REPLICATED])
    r = jnp.pad(r, (0, -r.shape[0] % 128)).reshape(1, -1, 128)
    parts.append(jnp.broadcast_to(r, (4,) + r.shape[1:]))
    parts = [_pad_segment(p) for p in parts]
    rows = sum(p.shape[1] for p in parts)
    if rows % (2 * SEGMENT_ROWS):
        parts.append(jnp.zeros((4, SEGMENT_ROWS, 128), F32))
        rows += SEGMENT_ROWS
    return jnp.concatenate(parts, axis=1).reshape(4, 2, rows // 2, 128)


def unpack_grads(flat, shard_shapes, repl_shapes):
    out, at = {}, 0
    for k in SHARDED:
        n = 1
        for e in shard_shapes[k]:
            n *= e
        out[k] = flat[at:at + n // 128].reshape(shard_shapes[k])
        at += _segment(n // 128)
    r = flat[at:].reshape(-1)
    at = 0
    for k in REPLICATED:
        n = 1
        for e in repl_shapes[k]:
            n *= e
        out[k] = r[at:at + n].reshape(repl_shapes[k])
        at += n
    return out


WEIGHT_ORDER = ("meta_tokens", "norm_mix", "w_in", "conv_qkv", "a_log", "dt_bias", "head_norm", "w_pool", "pool_scale",
                "w_out", "norm_ffn", "w_up", "conv_ffn", "w_down", "norm_final")


def kernel(x, meta_tokens, norm_mix, w_in, conv_qkv, a_log, dt_bias, head_norm, w_pool, pool_scale, w_out, norm_ffn, w_up, conv_ffn, w_down, norm_final, loss_target, m_meta_tokens, m_norm_mix, m_w_in, m_conv_qkv, m_a_log, m_dt_bias, m_head_norm, m_w_pool, m_pool_scale, m_w_out, m_norm_ffn, m_w_up, m_conv_ffn, m_w_down, m_norm_final, v_meta_tokens, v_norm_mix, v_w_in, v_conv_qkv, v_a_log, v_dt_bias, v_head_norm, v_w_pool, v_pool_scale, v_w_out, v_norm_ffn, v_w_up, v_conv_ffn, v_w_down, v_norm_final):
    weights = dict(meta_tokens=meta_tokens, norm_mix=norm_mix, w_in=w_in, conv_qkv=conv_qkv, a_log=a_log,
                   dt_bias=dt_bias, head_norm=head_norm, w_pool=w_pool, pool_scale=pool_scale, w_out=w_out,
                   norm_ffn=norm_ffn, w_up=w_up, conv_ffn=conv_ffn, w_down=w_down, norm_final=norm_final)
    m_in = dict(zip(WEIGHT_ORDER, (m_meta_tokens, m_norm_mix, m_w_in, m_conv_qkv, m_a_log, m_dt_bias, m_head_norm,
                                   m_w_pool, m_pool_scale, m_w_out, m_norm_ffn, m_w_up, m_conv_ffn, m_w_down, m_norm_final)))
    v_in = dict(zip(WEIGHT_ORDER, (v_meta_tokens, v_norm_mix, v_w_in, v_conv_qkv, v_a_log, v_dt_bias, v_head_norm,
                                   v_w_pool, v_pool_scale, v_w_out, v_norm_ffn, v_w_up, v_conv_ffn, v_w_down, v_norm_final)))
    shard_shapes = {k: weights[k].shape for k in SHARDED}
    repl_shapes = {k: weights[k].shape for k in REPLICATED}
    core = lax.axis_index("c").astype(jnp.int32).reshape(1)

    full = unpack_weights(gather_shards(pack_weights({k: weights[k] for k in SHARDED})), shard_shapes)
    layers = []
    for li in range(DEPTH):
        p = {k: (full[k][li] if k in full else weights[k][li]) for k in LAYER_PARAMS}
        layers.append(prep_layer(p))

    h0 = pad_rows(full["meta_tokens"], x[0])
    target = pad_rows(jnp.zeros((N_META, D_MODEL), F32), loss_target[0])
    loss, dh0, grads, d_norm_final = local_step(h0, target, layers, norm_final.reshape(1, D_MODEL))
    seq = x.shape[1]
    grad_x = dh0[LEAD + N_META:LEAD + N_META + seq][None]

    per_layer = [layer_grads(g) for g in grads]
    g_all = {k: jnp.stack([pl_[k] for pl_ in per_layer]) for k in LAYER_PARAMS}
    g_all["meta_tokens"] = dh0[LEAD:LEAD + N_META]
    g_all["norm_final"] = d_norm_final[0]
    packed = pack_grads({k: g_all[k] for k in SHARDED}, {k: g_all[k] for k in REPLICATED})
    g_mine = unpack_grads(all_reduce_to_shards(packed, core), shard_shapes, repl_shapes)

    loss_sum = lax.psum(loss[0, 0], ("x", "y", "c"))
    deltas, new_m, new_v = {}, {}, {}
    for k in WEIGHT_ORDER:
        deltas[k], new_m[k], new_v[k] = adamw(weights[k], g_mine[k], m_in[k], v_in[k], f"adamw_{k}")
    return (loss_sum, grad_x, *[g_mine[k] for k in WEIGHT_ORDER], *[deltas[k] for k in WEIGHT_ORDER],
            *[new_m[k] for k in WEIGHT_ORDER], *[new_v[k] for k in WEIGHT_ORDER])
```

```python
import functools

import jax
import jax.numpy as jnp
from jax import lax
from jax.experimental import pallas as pl
from jax.experimental.pallas import tpu as pltpu

F32 = jnp.float32
BF16 = jnp.bfloat16
WIRE = jnp.bfloat16
GRAD_WIRE = jnp.bfloat16

D_MODEL = 1024
HEADS = 8
HEAD_DIM = 128
CHUNK = 64
N_META = 16
LEAD = 48
TAIL = 64
QKV_DIM = 3072
D_FF = 2816
POOL_WIDTH = 512
POOL_WINDOWS = (2, 4, 8, 16)
DEPTH = 2
NORM_EPS = 1e-6
ADAM_LR, ADAM_B1, ADAM_B2, ADAM_EPS, ADAM_WD, ADAM_STEP = 0.001, 0.9, 0.999, 1e-08, 0.01, 10
VMEM_LIMIT_BYTES = 48 * 1024 * 1024


def _params(*sem):
    return pltpu.CompilerParams(dimension_semantics=sem if sem else None, vmem_limit_bytes=VMEM_LIMIT_BYTES)


def _tile(n, cap, mult):
    best = None
    for t in range(mult, min(n, cap) + 1, mult):
        if n % t == 0:
            best = t
    assert best is not None, (n, cap, mult)
    return best


def _silu(x):
    return x * jax.nn.sigmoid(x)


def _softplus(x):
    return jnp.maximum(x, 0.0) + jnp.log(1.0 + jnp.exp(-jnp.abs(x)))


def _split_bf16(a):
    hi = a.astype(BF16)
    return hi, (a - hi.astype(F32)).astype(BF16)


def _dg(a, b, ca, cb, hi):
    dims = (((ca,), (cb,)), ((), ()))
    if hi is True:
        return lax.dot_general(a, b, dims, precision=lax.Precision.HIGHEST, preferred_element_type=F32)
    if hi == 3:
        (ah, al), (bh, bl) = _split_bf16(a), _split_bf16(b)
        dot = lambda x, y: lax.dot_general(x, y, dims, preferred_element_type=F32)
        return dot(ah, bh) + (dot(ah, bl) + dot(al, bh))
    return lax.dot_general(a.astype(BF16), b.astype(BF16), dims, preferred_element_type=F32)


def _make_dots(hi):
    @jax.custom_vjp
    def nn(a, b):
        return _dg(a, b, 1, 0, hi)

    @jax.custom_vjp
    def nt(a, b):
        return _dg(a, b, 1, 1, hi)

    @jax.custom_vjp
    def tn(a, b):
        return _dg(a, b, 0, 0, hi)

    nn.defvjp(lambda a, b: (nn(a, b), (a, b)), lambda r, g: (nt(g, r[1]), tn(r[0], g)))
    nt.defvjp(lambda a, b: (nt(a, b), (a, b)), lambda r, g: (nn(g, r[1]), tn(g, r[0])))
    tn.defvjp(lambda a, b: (tn(a, b), (a, b)), lambda r, g: (nt(r[1], g), nn(r[0], g)))
    return nn, nt, tn


_nn, _nt, _tn = _make_dots(False)
_hnn, _hnt, _htn = _make_dots(True)
_nn3, _nt3, _tn3 = _make_dots(3)


def mm(a, b, *, tb=False, add=None, out_dtype=F32, name):
    m, kdim = a.shape
    (n, kb) = b.shape if tb else b.shape[::-1]
    assert kdim == kb, (a.shape, b.shape, tb)
    tm = _tile(m, 1408, 128) if m % 128 == 0 and m <= 4096 else _tile(m, 640, 64)
    tn = _tile(n, 1536, 128)
    tk = kdim if kdim <= 3072 else _tile(kdim, 1664 if a.dtype == b.dtype == BF16 else 640, 128)
    nk = kdim // tk
    dims = (((1,), (1 if tb else 0,)), ((), ()))

    def body(*refs):
        if add is not None:
            a_ref, b_ref, add_ref, o_ref, acc = refs
        else:
            a_ref, b_ref, o_ref, acc = refs
        k = pl.program_id(2)
        part = lax.dot_general(a_ref[...].astype(BF16), b_ref[...].astype(BF16), dims, preferred_element_type=F32)

        def finish(r):
            if add is not None:
                r = r + add_ref[...]
            o_ref[...] = r.astype(out_dtype)

        if nk == 1:
            finish(part)
        else:
            @pl.when(k == 0)
            def _():
                acc[...] = part

            @pl.when(jnp.logical_and(k > 0, k < nk - 1))
            def _():
                acc[...] += part

            @pl.when(k == nk - 1)
            def _():
                finish(acc[...] + part)

    a_spec = pl.BlockSpec((tm, tk), lambda j, i, k: (i, k))
    b_spec = pl.BlockSpec((tn, tk), lambda j, i, k: (j, k)) if tb else pl.BlockSpec((tk, tn), lambda j, i, k: (k, j))
    in_specs = [a_spec, b_spec]
    args = [a, b]
    if add is not None:
        in_specs.append(pl.BlockSpec((tm, tn), lambda j, i, k: (i, j)))
        args.append(add)
    return pl.pallas_call(
        body, grid=(n // tn, m // tm, nk), in_specs=in_specs,
        out_specs=pl.BlockSpec((tm, tn), lambda j, i, k: (i, j)),
        out_shape=jax.ShapeDtypeStruct((m, n), out_dtype),
        scratch_shapes=[pltpu.VMEM((tm, tn) if nk > 1 else (8, 128), F32)],
        compiler_params=_params("parallel", "parallel", "arbitrary"), name=name,
    )(*args)


def _rms(x, gain):
    return x * lax.rsqrt(jnp.mean(x * x, axis=-1, keepdims=True) + NORM_EPS) * gain


def rms_fwd(h, gain, name):
    t = h.shape[0]
    ts = _tile(t, 640, 128)

    def body(h_ref, g_ref, u_ref, ut_ref):
        u = _rms(h_ref[...], g_ref[...])
        u_ref[...] = u.astype(BF16)
        ut_ref[...] = u.T.astype(BF16)

    return pl.pallas_call(
        body, grid=(t // ts,),
        in_specs=[pl.BlockSpec((ts, D_MODEL), lambda i: (i, 0)), pl.BlockSpec((1, D_MODEL), lambda i: (0, 0))],
        out_specs=(pl.BlockSpec((ts, D_MODEL), lambda i: (i, 0)), pl.BlockSpec((D_MODEL, ts), lambda i: (0, i))),
        out_shape=(jax.ShapeDtypeStruct((t, D_MODEL), BF16), jax.ShapeDtypeStruct((D_MODEL, t), BF16)),
        compiler_params=_params("parallel"), name=name,
    )(h, gain)


def rms_bwd(h, gain, du, dres, name):
    t = h.shape[0]
    ts = _tile(t, 640, 64)

    def body(h_ref, g_ref, du_ref, dres_ref, dh_ref, dg_ref):
        i = pl.program_id(0)
        _, vjp = jax.vjp(_rms, h_ref[...], g_ref[...])
        dx, dg = vjp(du_ref[...])
        row = i * ts + lax.broadcasted_iota(jnp.int32, (ts, 1), 0)
        dh_ref[...] = jnp.where(row >= LEAD, dx + dres_ref[...], 0.0)

        @pl.when(i == 0)
        def _():
            dg_ref[...] = jnp.zeros_like(dg_ref)

        dg_ref[...] += dg

    row_spec = pl.BlockSpec((ts, D_MODEL), lambda i: (i, 0))
    vec_spec = pl.BlockSpec((1, D_MODEL), lambda i: (0, 0))
    return pl.pallas_call(
        body, grid=(t // ts,), in_specs=[row_spec, vec_spec, row_spec, row_spec],
        out_specs=(row_spec, vec_spec),
        out_shape=(jax.ShapeDtypeStruct((t, D_MODEL), F32), jax.ShapeDtypeStruct((1, D_MODEL), F32)),
        compiler_params=_params("arbitrary"), name=name,
    )(h, gain, du, dres)


def loss_head(h, gain, target, name):
    t = h.shape[0]
    ts = _tile(t, 640, 64)

    def body(h_ref, g_ref, t_ref, loss_ref, dh_ref, dg_ref):
        i = pl.program_id(0)
        row = i * ts + lax.broadcasted_iota(jnp.int32, (ts, 1), 0)
        keep = jnp.logical_and(row >= LEAD + N_META, row < t - TAIL)
        tgt = t_ref[...]

        def f(x, g):
            err = jnp.where(keep, _rms(x, g) - tgt, 0.0)
            per_row = jnp.mean(err * err, axis=-1, keepdims=True)
            return 0.5 * jnp.sum(per_row, axis=0, keepdims=True)

        val, vjp = jax.vjp(f, h_ref[...], g_ref[...])
        dx, dg = vjp(jnp.ones((1, 1), F32))
        dh_ref[...] = dx

        @pl.when(i == 0)
        def _():
            dg_ref[...] = jnp.zeros_like(dg_ref)
            loss_ref[...] = jnp.zeros_like(loss_ref)

        dg_ref[...] += dg
        loss_ref[...] += jnp.broadcast_to(val, (1, 128))

    row_spec = pl.BlockSpec((ts, D_MODEL), lambda i: (i, 0))
    vec_spec = pl.BlockSpec((1, D_MODEL), lambda i: (0, 0))
    return pl.pallas_call(
        body, grid=(t // ts,), in_specs=[row_spec, vec_spec, row_spec],
        out_specs=(pl.BlockSpec((1, 128), lambda i: (0, 0)), row_spec, vec_spec),
        out_shape=(jax.ShapeDtypeStruct((1, 128), F32), jax.ShapeDtypeStruct((t, D_MODEL), F32),
                   jax.ShapeDtypeStruct((1, D_MODEL), F32)),
        compiler_params=_params("arbitrary"), name=name,
    )(h, gain, target)


def conv_fwd(x, w, name):
    t, width = x.shape
    k = w.shape[0]
    ts = _tile(t, 640, 64)
    tw = _tile(width, 1536, 128)
    hb = ts // 8

    def body(x_ref, halo_ref, w_ref, o_ref, buf):
        i = pl.program_id(0)
        buf[0:8, :] = jnp.where(i > 0, halo_ref[...], 0.0)
        buf[8:, :] = x_ref[...]
        wv = w_ref[...]
        acc = buf[pl.ds(8 - (k - 1), ts), :] * wv[0:1, :]
        for j in range(1, k):
            acc = acc + buf[pl.ds(8 - (k - 1) + j, ts), :] * wv[j:j + 1, :]
        o_ref[...] = acc

    return pl.pallas_call(
        body, grid=(t // ts, width // tw),
        in_specs=[pl.BlockSpec((ts, tw), lambda i, j: (i, j)),
                  pl.BlockSpec((8, tw), lambda i, j: (jnp.maximum(i * hb - 1, 0), j)),
                  pl.BlockSpec((k, tw), lambda i, j: (0, j))],
        out_specs=pl.BlockSpec((ts, tw), lambda i, j: (i, j)),
        out_shape=jax.ShapeDtypeStruct((t, width), F32),
        scratch_shapes=[pltpu.VMEM((ts + 8, tw), F32)],
        compiler_params=_params("parallel", "parallel"), name=name,
    )(x, x, w)


def conv_bwd(x, dc, w, name):
    t, width = x.shape
    k = w.shape[0]
    ts = _tile(t, 640, 64)
    tw = _tile(width, 1536, 128)
    hb = ts // 8
    nt = t // ts

    def body(x_ref, xh_ref, dc_ref, dch_ref, w_ref, dx_ref, dw_ref, xbuf, dbuf):
        i = pl.program_id(1)
        xbuf[0:8, :] = jnp.where(i > 0, xh_ref[...], 0.0)
        xbuf[8:, :] = x_ref[...]
        d = dc_ref[...]
        dbuf[0:ts, :] = d
        dbuf[ts:, :] = jnp.where(i < nt - 1, dch_ref[...], 0.0)
        wv = w_ref[...]
        acc = dbuf[pl.ds(k - 1, ts), :] * wv[0:1, :]
        for j in range(1, k):
            acc = acc + dbuf[pl.ds(k - 1 - j, ts), :] * wv[j:j + 1, :]
        dx_ref[...] = acc.astype(BF16)

        @pl.when(i == 0)
        def _():
            dw_ref[...] = jnp.zeros_like(dw_ref)

        for j in range(k):
            dw_ref[j:j + 1, :] += jnp.sum(d * xbuf[pl.ds(8 - (k - 1) + j, ts), :], axis=0, keepdims=True)

    return pl.pallas_call(
        body, grid=(width // tw, nt),
        in_specs=[pl.BlockSpec((ts, tw), lambda j, i: (i, j)),
                  pl.BlockSpec((8, tw), lambda j, i: (jnp.maximum(i * hb - 1, 0), j)),
                  pl.BlockSpec((ts, tw), lambda j, i: (i, j)),
                  pl.BlockSpec((8, tw), lambda j, i: (jnp.minimum((i + 1) * hb, t // 8 - 1), j)),
                  pl.BlockSpec((k, tw), lambda j, i: (0, j))],
        out_specs=(pl.BlockSpec((ts, tw), lambda j, i: (i, j)), pl.BlockSpec((8, tw), lambda j, i: (0, j))),
        out_shape=(jax.ShapeDtypeStruct((t, width), BF16), jax.ShapeDtypeStruct((8, width), F32)),
        scratch_shapes=[pltpu.VMEM((ts + 8, tw), F32), pltpu.VMEM((ts + 8, tw), F32)],
        compiler_params=_params("parallel", "arbitrary"), name=name,
    )(x, x, dc, dc, w)


def _pool_count(pos, win):
    return jnp.clip(pos + 1, 1, win).astype(F32)


def poolwin_fwd(p, name):
    t = p.shape[0]
    ts = _tile(t, 640, 64)
    hb = ts // 16

    def body(p_ref, halo_ref, o_ref, buf):
        i = pl.program_id(0)
        buf[0:16, :] = jnp.where(i > 0, halo_ref[...], 0.0)
        buf[16:, :] = p_ref[...]
        pos = i * ts + lax.broadcasted_iota(jnp.int32, (ts, 1), 0) - LEAD
        for gi, win in enumerate(POOL_WINDOWS):
            cols = slice(gi * 128, (gi + 1) * 128)
            own = buf[pl.ds(16, ts), cols]
            acc = own
            for j in range(1, win):
                acc = acc + buf[pl.ds(16 - j, ts), cols]
            o_ref[:, cols] = acc / _pool_count(pos, win) - own

    return pl.pallas_call(
        body, grid=(t // ts,),
        in_specs=[pl.BlockSpec((ts, POOL_WIDTH), lambda i: (i, 0)),
                  pl.BlockSpec((16, POOL_WIDTH), lambda i: (jnp.maximum(i * hb - 1, 0), 0))],
        out_specs=pl.BlockSpec((ts, POOL_WIDTH), lambda i: (i, 0)),
        out_shape=jax.ShapeDtypeStruct((t, POOL_WIDTH), F32),
        scratch_shapes=[pltpu.VMEM((ts + 16, POOL_WIDTH), F32)],
        compiler_params=_params("parallel"), name=name,
    )(p, p)


def poolwin_bwd(dpooled, name):
    t = dpooled.shape[0]
    ts = _tile(t, 640, 64)
    hb = ts // 16
    nt = t // ts

    def body(d_ref, halo_ref, o_ref, buf):
        i = pl.program_id(0)
        buf[0:ts, :] = d_ref[...]
        buf[ts:, :] = jnp.where(i < nt - 1, halo_ref[...], 0.0)
        pos = i * ts + lax.broadcasted_iota(jnp.int32, (ts, 1), 0) - LEAD
        for gi, win in enumerate(POOL_WINDOWS):
            cols = slice(gi * 128, (gi + 1) * 128)
            own = buf[pl.ds(0, ts), cols]
            acc = own / _pool_count(pos, win)
            for j in range(1, win):
                acc = acc + buf[pl.ds(j, ts), cols] / _pool_count(pos + j, win)
            o_ref[:, cols] = (acc - own).astype(BF16)

    return pl.pallas_call(
        body, grid=(nt,),
        in_specs=[pl.BlockSpec((ts, POOL_WIDTH), lambda i: (i, 0)),
                  pl.BlockSpec((16, POOL_WIDTH), lambda i: (jnp.minimum((i + 1) * hb, t // 16 - 1), 0))],
        out_specs=pl.BlockSpec((ts, POOL_WIDTH), lambda i: (i, 0)),
        out_shape=jax.ShapeDtypeStruct((t, POOL_WIDTH), BF16),
        scratch_shapes=[pltpu.VMEM((ts + 16, POOL_WIDTH), F32)],
        compiler_params=_params("parallel"), name=name,
    )(dpooled, dpooled)


def _mix(y_a, gpre, pooled, w_pool, scale):
    parts = [_nn(pooled[:, g * 128:(g + 1) * 128], w_pool[g]) for g in range(4)]
    y_b = jnp.concatenate(parts, axis=1) * scale
    return jax.nn.sigmoid(gpre[:, :D_MODEL]) * y_a + jax.nn.sigmoid(gpre[:, D_MODEL:]) * y_b


def _mix_specs(ts):
    return [pl.BlockSpec((ts, D_MODEL), lambda i: (i, 0)), pl.BlockSpec((ts, 2 * D_MODEL), lambda i: (i, 0)),
            pl.BlockSpec((ts, POOL_WIDTH), lambda i: (i, 0)), pl.BlockSpec((4, 128, 256), lambda i: (0, 0, 0)),
            pl.BlockSpec((1, D_MODEL), lambda i: (0, 0))]


def mix_fwd(y_a, gpre, pooled, w_pool, scale, name):
    t = y_a.shape[0]
    ts = _tile(t, 640, 128)

    def body(ya_ref, g_ref, p_ref, w_ref, s_ref, o_ref, ot_ref):
        y = _mix(ya_ref[...], g_ref[...], p_ref[...], w_ref[...], s_ref[...])
        o_ref[...] = y.astype(BF16)
        ot_ref[...] = y.T.astype(BF16)

    return pl.pallas_call(
        body, grid=(t // ts,), in_specs=_mix_specs(ts),
        out_specs=(pl.BlockSpec((ts, D_MODEL), lambda i: (i, 0)), pl.BlockSpec((D_MODEL, ts), lambda i: (0, i))),
        out_shape=(jax.ShapeDtypeStruct((t, D_MODEL), BF16), jax.ShapeDtypeStruct((D_MODEL, t), BF16)),
        compiler_params=_params("parallel"), name=name,
    )(y_a, gpre, pooled, w_pool, scale)


def mix_bwd(y_a, gpre, pooled, w_pool, scale, dy, name):
    t = y_a.shape[0]
    ts = _tile(t, 320, 64)

    def body(ya_ref, g_ref, p_ref, w_ref, s_ref, dy_ref, dya_ref, dg_ref, dp_ref, dw_ref, ds_ref):
        i = pl.program_id(0)
        _, vjp = jax.vjp(_mix, ya_ref[...], g_ref[...], p_ref[...], w_ref[...], s_ref[...])
        dya, dg, dp, dw, ds = vjp(dy_ref[...])
        dya_ref[...] = dya
        dg_ref[...] = dg.astype(BF16)
        dp_ref[...] = dp

        @pl.when(i == 0)
        def _():
            dw_ref[...] = jnp.zeros_like(dw_ref)
            ds_ref[...] = jnp.zeros_like(ds_ref)

        dw_ref[...] += dw
        ds_ref[...] += ds

    specs = _mix_specs(ts)
    return pl.pallas_call(
        body, grid=(t // ts,), in_specs=specs + [specs[0]],
        out_specs=(specs[0], specs[1], specs[2], specs[3], specs[4]),
        out_shape=(jax.ShapeDtypeStruct((t, D_MODEL), F32), jax.ShapeDtypeStruct((t, 2 * D_MODEL), BF16),
                   jax.ShapeDtypeStruct((t, POOL_WIDTH), F32), jax.ShapeDtypeStruct((4, 128, 256), F32),
                   jax.ShapeDtypeStruct((1, D_MODEL), F32)),
        compiler_params=_params("arbitrary"), name=name,
    )(y_a, gpre, pooled, w_pool, scale, dy)


def _ffn_act(cg, cv):
    return _silu(cg) * cv


def ffnact_fwd(cg, cv, name):
    t, width = cg.shape
    ts = _tile(t, 640, 128)
    tw = _tile(width, 1536, 128)
    spec = pl.BlockSpec((ts, tw), lambda i, j: (i, j))

    def body(g_ref, v_ref, o_ref, ot_ref):
        act = _ffn_act(g_ref[...], v_ref[...])
        o_ref[...] = act.astype(BF16)
        ot_ref[...] = act.T.astype(BF16)

    return pl.pallas_call(
        body, grid=(t // ts, width // tw), in_specs=[spec, spec],
        out_specs=(spec, pl.BlockSpec((tw, ts), lambda i, j: (j, i))),
        out_shape=(jax.ShapeDtypeStruct((t, width), BF16), jax.ShapeDtypeStruct((width, t), BF16)),
        compiler_params=_params("parallel", "parallel"), name=name,
    )(cg, cv)


def ffnact_bwd(cg, cv, dact, name):
    t, width = cg.shape
    ts = _tile(t, 640, 64)
    tw = _tile(width, 1536, 128)
    spec = pl.BlockSpec((ts, tw), lambda i, j: (i, j))

    def body(g_ref, v_ref, d_ref, dg_ref, dv_ref):
        _, vjp = jax.vjp(_ffn_act, g_ref[...], v_ref[...])
        dg_ref[...], dv_ref[...] = vjp(d_ref[...])

    return pl.pallas_call(
        body, grid=(t // ts, width // tw), in_specs=[spec, spec, spec], out_specs=(spec, spec),
        out_shape=(jax.ShapeDtypeStruct((t, width), F32), jax.ShapeDtypeStruct((t, width), F32)),
        compiler_params=_params("parallel", "parallel"), name=name,
    )(cg, cv, dact)


def _gdn_chunk(c, z, ba, pa, pdt, hn, s, *, valid):
    r = lax.broadcasted_iota(jnp.int32, (CHUNK, CHUNK), 0)
    q_ = lax.broadcasted_iota(jnp.int32, (CHUNK, CHUNK), 1)
    causal = r >= q_
    strict = r > q_
    tril = causal.astype(F32)
    triu = (r <= q_).astype(F32)
    eye = (r == q_).astype(F32)
    lane = lax.broadcasted_iota(jnp.int32, (CHUNK, 128), 1)

    decay_log = -jnp.exp(pa) * _softplus(ba + pdt)
    bg = jnp.where(lane < HEADS, jax.nn.sigmoid(ba), jnp.where(lane < 2 * HEADS, decay_log, 0.0))
    bg = jnp.where(valid, bg, 0.0)
    gc = _hnn(tril, bg)
    gct = _hnn(bg.T, triu)
    eg = jnp.exp(gc)
    glast = gc[CHUNK - 1:CHUNK, :]
    ekd = jnp.exp(glast - gc)
    gtot = jnp.exp(glast)

    hd = range(HEADS)
    hs = [slice(h * HEAD_DIM, (h + 1) * HEAD_DIM) for h in hd]
    gl = [slice(HEADS + h, HEADS + h + 1) for h in hd]
    q = [_silu(c[:, hs[h]]) for h in hd]
    k = [_silu(c[:, D_MODEL + h * HEAD_DIM:D_MODEL + (h + 1) * HEAD_DIM]) for h in hd]
    v = [_silu(c[:, 2 * D_MODEL + h * HEAD_DIM:2 * D_MODEL + (h + 1) * HEAD_DIM]) for h in hd]
    q = [q[h] * lax.rsqrt(jnp.sum(q[h] * q[h], axis=-1, keepdims=True) + NORM_EPS) * (HEAD_DIM ** -0.5) for h in hd]
    k = [k[h] * lax.rsqrt(jnp.sum(k[h] * k[h], axis=-1, keepdims=True) + NORM_EPS) for h in hd]
    beta = [bg[:, h:h + 1] for h in hd]
    decay = [jnp.exp(jnp.where(causal, gc[:, gl[h]] - gct[gl[h], :], -1e30)) for h in hd]
    kb = [k[h] * beta[h] for h in hd]
    a = [jnp.where(strict, _nt(kb[h], k[h]) * decay[h], 0.0) for h in hd]
    qk = [jnp.where(causal, _nt(q[h], k[h]) * decay[h], 0.0) for h in hd]
    p = [_nn3(a[h], a[h]) for h in hd]
    x = [(eye - a[h]) + p[h] - _nn(a[h], p[h]) for h in hd]
    for _ in range(4):
        p = [_nn(p[h], p[h]) for h in hd]
        x = [x[h] + p[h] + _nn(x[h] - eye, p[h]) for h in hd]
    u = [_nn(x[h], v[h] * beta[h]) for h in hd]
    w = [_nn(x[h], kb[h] * eg[:, gl[h]]) for h in hd]
    v_new = [u[h] - _nn(w[h], s[h]) for h in hd]
    o = [_nn(q[h] * eg[:, gl[h]], s[h]) + _nn(qk[h], v_new[h]) for h in hd]
    states = [s[h] * gtot[:, gl[h]] + _tn(k[h] * ekd[:, gl[h]], v_new[h]) for h in hd]
    o = [o[h] * lax.rsqrt(jnp.mean(o[h] * o[h], axis=-1, keepdims=True) + NORM_EPS) * hn * _silu(z[:, hs[h]])
         for h in hd]
    return jnp.concatenate(o, axis=1), tuple(states)


def _chunk_valid(n, t):
    row = n * CHUNK + lax.broadcasted_iota(jnp.int32, (CHUNK, 1), 0)
    return jnp.logical_and(row >= LEAD, row < t - TAIL)


def gdn_fwd(c, z, ba, pa, pdt, hn, name):
    t = c.shape[0]
    n_chunks = t // CHUNK

    def body(c_ref, z_ref, ba_ref, pa_ref, pdt_ref, hn_ref, y_ref, ss_ref, state):
        n = pl.program_id(0)

        @pl.when(n == 0)
        def _():
            state[...] = jnp.zeros_like(state)

        s0 = tuple(state[h] for h in range(HEADS))
        for h in range(HEADS):
            ss_ref[0, h] = s0[h]
        y, s1 = _gdn_chunk(c_ref[...], z_ref[...], ba_ref[...], pa_ref[...], pdt_ref[...], hn_ref[...], s0,
                           valid=_chunk_valid(n, t))
        y_ref[...] = y
        for h in range(HEADS):
            state[h] = s1[h]

    vec = pl.BlockSpec((1, 128), lambda n: (0, 0))
    return pl.pallas_call(
        body, grid=(n_chunks,),
        in_specs=[pl.BlockSpec((CHUNK, QKV_DIM), lambda n: (n, 0)), pl.BlockSpec((CHUNK, D_MODEL), lambda n: (n, 0)),
                  pl.BlockSpec((CHUNK, 128), lambda n: (n, 0)), vec, vec, vec],
        out_specs=(pl.BlockSpec((CHUNK, D_MODEL), lambda n: (n, 0)),
                   pl.BlockSpec((1, HEADS, HEAD_DIM, HEAD_DIM), lambda n: (n, 0, 0, 0))),
        out_shape=(jax.ShapeDtypeStruct((t, D_MODEL), F32),
                   jax.ShapeDtypeStruct((n_chunks, HEADS, HEAD_DIM, HEAD_DIM), F32)),
        scratch_shapes=[pltpu.VMEM((HEADS, HEAD_DIM, HEAD_DIM), F32)],
        compiler_params=_params("arbitrary"), name=name,
    )(c, z, ba, pa, pdt, hn)


def gdn_bwd(c, z, ba, pa, pdt, hn, starts, dy, name):
    t = c.shape[0]
    n_chunks = t // CHUNK

    def body(c_ref, z_ref, ba_ref, pa_ref, pdt_ref, hn_ref, ss_ref, dy_ref,
             dc_ref, dz_ref, dba_ref, dpa_ref, dpdt_ref, dhn_ref, dstate):
        step = pl.program_id(0)
        n = n_chunks - 1 - step

        @pl.when(step == 0)
        def _():
            dstate[...] = jnp.zeros_like(dstate)
            dpa_ref[...] = jnp.zeros_like(dpa_ref)
            dpdt_ref[...] = jnp.zeros_like(dpdt_ref)
            dhn_ref[...] = jnp.zeros_like(dhn_ref)

        f = functools.partial(_gdn_chunk, valid=_chunk_valid(n, t))
        _, vjp = jax.vjp(f, c_ref[...], z_ref[...], ba_ref[...], pa_ref[...], pdt_ref[...], hn_ref[...],
                         tuple(ss_ref[0, h] for h in range(HEADS)))
        dc, dz, dba, dpa, dpdt, dhn, ds = vjp((dy_ref[...], tuple(dstate[h] for h in range(HEADS))))
        dc_ref[...] = dc
        dz_ref[...] = dz.astype(BF16)
        dba_ref[...] = dba.astype(BF16)
        dpa_ref[...] += dpa
        dpdt_ref[...] += dpdt
        dhn_ref[...] += dhn
        for h in range(HEADS):
            dstate[h] = ds[h]

    def rev(width):
        return pl.BlockSpec((CHUNK, width), lambda s: (n_chunks - 1 - s, 0))

    vec = pl.BlockSpec((1, 128), lambda s: (0, 0))
    vec_shape = jax.ShapeDtypeStruct((1, 128), F32)
    return pl.pallas_call(
        body, grid=(n_chunks,),
        in_specs=[rev(QKV_DIM), rev(D_MODEL), rev(128), vec, vec, vec,
                  pl.BlockSpec((1, HEADS, HEAD_DIM, HEAD_DIM), lambda s: (n_chunks - 1 - s, 0, 0, 0)), rev(D_MODEL)],
        out_specs=(rev(QKV_DIM), rev(D_MODEL), rev(128), vec, vec, vec),
        out_shape=(jax.ShapeDtypeStruct((t, QKV_DIM), F32), jax.ShapeDtypeStruct((t, D_MODEL), BF16),
                   jax.ShapeDtypeStruct((t, 128), BF16), vec_shape, vec_shape, vec_shape),
        scratch_shapes=[pltpu.VMEM((HEADS, HEAD_DIM, HEAD_DIM), F32)],
        compiler_params=_params("arbitrary"), name=name,
    )(c, z, ba, pa, pdt, hn, starts, dy)


def _layer_fwd(h, w, tag):
    u, ut = rms_fwd(h, w["norm_mix"], f"{tag}_rms_mix")
    pq = mm(u, w["wqkv"], name=f"{tag}_mm_qkv")
    pz = mm(u, w["wz"], name=f"{tag}_mm_z")
    pg = mm(u, w["wg"], name=f"{tag}_mm_gate")
    pp = mm(u, w["wpl"], name=f"{tag}_mm_pool")
    pba = mm(u, w["wba"], name=f"{tag}_mm_ba")
    cq = conv_fwd(pq, w["conv_qkv"], f"{tag}_conv_qkv")
    ya, starts = gdn_fwd(cq, pz, pba, w["pa"], w["pdt"], w["head_norm"], f"{tag}_gdn")
    pooled = poolwin_fwd(pp, f"{tag}_poolwin")
    y, yt = mix_fwd(ya, pg, pooled, w["w_pool"], w["pool_scale"], f"{tag}_mix")
    h1 = mm(y, w["wout"], add=h, name=f"{tag}_mm_out")
    u2, u2t = rms_fwd(h1, w["norm_ffn"], f"{tag}_rms_ffn")
    hg = mm(u2, w["wupg"], name=f"{tag}_mm_upg")
    hv = mm(u2, w["wupv"], name=f"{tag}_mm_upv")
    cg = conv_fwd(hg, w["conv_g"], f"{tag}_conv_g")
    cv = conv_fwd(hv, w["conv_v"], f"{tag}_conv_v")
    act, actt = ffnact_fwd(cg, cv, f"{tag}_act")
    h2 = mm(act, w["wdown"], add=h1, name=f"{tag}_mm_down")
    saved = dict(h=h, ut=ut, pq=pq, pz=pz, pg=pg, pba=pba, cq=cq, ya=ya, starts=starts, pooled=pooled, yt=yt, h1=h1,
                 u2t=u2t, hg=hg, hv=hv, cg=cg, cv=cv, actt=actt)
    return h2, saved


def _layer_bwd(dh2, w, s, tag):
    g = {}
    dact = mm(dh2, w["wdown"], tb=True, name=f"{tag}_bmm_down_x")
    g["wdown"] = mm(s["actt"], dh2, name=f"{tag}_bmm_down_w")
    dcg, dcv = ffnact_bwd(s["cg"], s["cv"], dact, f"{tag}_act_b")
    dhg, g["conv_g"] = conv_bwd(s["hg"], dcg, w["conv_g"], f"{tag}_conv_g_b")
    dhv, g["conv_v"] = conv_bwd(s["hv"], dcv, w["conv_v"], f"{tag}_conv_v_b")
    du2 = mm(dhg, w["wupg"], tb=True, name=f"{tag}_bmm_upg_x")
    du2 = mm(dhv, w["wupv"], tb=True, add=du2, name=f"{tag}_bmm_upv_x")
    g["wupg"] = mm(s["u2t"], dhg, name=f"{tag}_bmm_upg_w")
    g["wupv"] = mm(s["u2t"], dhv, name=f"{tag}_bmm_upv_w")
    dh1, g["norm_ffn"] = rms_bwd(s["h1"], w["norm_ffn"], du2, dh2, f"{tag}_rms_ffn_b")
    dy = mm(dh1, w["wout"], tb=True, name=f"{tag}_bmm_out_x")
    g["wout"] = mm(s["yt"], dh1, name=f"{tag}_bmm_out_w")
    dya, dpg, dpooled, g["w_pool"], g["pool_scale"] = mix_bwd(
        s["ya"], s["pg"], s["pooled"], w["w_pool"], w["pool_scale"], dy, f"{tag}_mix_b")
    dpp = poolwin_bwd(dpooled, f"{tag}_poolwin_b")
    dcq, dpz, dpba, g["pa"], g["pdt"], g["head_norm"] = gdn_bwd(
        s["cq"], s["pz"], s["pba"], w["pa"], w["pdt"], w["head_norm"], s["starts"], dya, f"{tag}_gdn_b")
    dpq, g["conv_qkv"] = conv_bwd(s["pq"], dcq, w["conv_qkv"], f"{tag}_conv_qkv_b")
    du = mm(dpq, w["wqkv"], tb=True, name=f"{tag}_bmm_qkv_x")
    du = mm(dpz, w["wz"], tb=True, add=du, name=f"{tag}_bmm_z_x")
    du = mm(dpg, w["wg"], tb=True, add=du, name=f"{tag}_bmm_gate_x")
    du = mm(dpp, w["wpl"], tb=True, add=du, name=f"{tag}_bmm_pool_x")
    du = mm(dpba, w["wba"], tb=True, add=du, name=f"{tag}_bmm_ba_x")
    g["wqkv"] = mm(s["ut"], dpq, name=f"{tag}_bmm_qkv_w")
    g["wz"] = mm(s["ut"], dpz, name=f"{tag}_bmm_z_w")
    g["wg"] = mm(s["ut"], dpg, name=f"{tag}_bmm_gate_w")
    g["wpl"] = mm(s["ut"], dpp, name=f"{tag}_bmm_pool_w")
    g["wba"] = mm(s["ut"], dpba, name=f"{tag}_bmm_ba_w")
    dh, g["norm_mix"] = rms_bwd(s["h"], w["norm_mix"], du, dh1, f"{tag}_rms_mix_b")
    return dh, g


def local_step(h0, target, layers, norm_final):
    h = h0
    saved = []
    for li, w in enumerate(layers):
        h, s = _layer_fwd(h, w, f"l{li}")
        saved.append(s)
    loss, dh, dnf = loss_head(h, norm_final, target, "loss_head")
    grads = [None] * len(layers)
    for li in reversed(range(len(layers))):
        dh, grads[li] = _layer_bwd(dh, layers[li], saved[li], f"l{li}")
    return loss, dh, grads, dnf


_Z0, _B0, _P0, _G0, _IN_DIM = 3072, 4096, 4112, 4624, 6672


def _lanes_8_to_15(v):
    return jnp.pad(v.reshape(1, HEADS).astype(F32), ((0, 0), (HEADS, 128 - 2 * HEADS)))


def prep_layer(p):
    w_in = p["w_in"].astype(BF16)
    w_up = p["w_up"].astype(BF16)
    row = lambda v: v.reshape(1, -1).astype(F32)
    return dict(
        wqkv=w_in[:, :_Z0], wz=w_in[:, _Z0:_B0], wba=jnp.pad(w_in[:, _B0:_P0], ((0, 0), (0, 128 - 2 * HEADS))),
        wpl=w_in[:, _P0:_G0], wg=w_in[:, _G0:], wout=p["w_out"].astype(BF16),
        wupg=w_up[:, :D_FF], wupv=w_up[:, D_FF:], wdown=p["w_down"].astype(BF16),
        conv_qkv=p["conv_qkv"].astype(F32), conv_g=p["conv_ffn"][:, :D_FF].astype(F32),
        conv_v=p["conv_ffn"][:, D_FF:].astype(F32), w_pool=p["w_pool"].astype(F32),
        pool_scale=row(p["pool_scale"]), head_norm=row(p["head_norm"]), norm_mix=row(p["norm_mix"]),
        norm_ffn=row(p["norm_ffn"]), pa=_lanes_8_to_15(p["a_log"]), pdt=_lanes_8_to_15(p["dt_bias"]))


def layer_grads(g):
    return dict(
        w_in=jnp.concatenate([g["wqkv"], g["wz"], g["wba"][:, :2 * HEADS], g["wpl"], g["wg"]], axis=1),
        conv_qkv=g["conv_qkv"][:4], a_log=g["pa"][0, HEADS:2 * HEADS], dt_bias=g["pdt"][0, HEADS:2 * HEADS],
        head_norm=g["head_norm"][0], w_pool=g["w_pool"], pool_scale=g["pool_scale"][0], w_out=g["wout"],
        norm_mix=g["norm_mix"][0], norm_ffn=g["norm_ffn"][0],
        w_up=jnp.concatenate([g["wupg"], g["wupv"]], axis=1),
        conv_ffn=jnp.concatenate([g["conv_g"][:3], g["conv_v"][:3]], axis=1), w_down=g["wdown"])


LAYER_PARAMS = ("norm_mix", "w_in", "conv_qkv", "a_log", "dt_bias", "head_norm", "w_pool", "pool_scale", "w_out",
                "norm_ffn", "w_up", "conv_ffn", "w_down")


def pad_rows(meta, x):
    return jnp.concatenate([jnp.zeros((LEAD, D_MODEL), F32), meta.astype(F32), x.astype(F32),
                            jnp.zeros((TAIL, D_MODEL), F32)], axis=0)


MESH = pl.DeviceIdType.MESH
ANY = pl.BlockSpec(memory_space=pl.ANY)


def _place():
    x, y, c = lax.axis_index("x"), lax.axis_index("y"), lax.axis_index("c")
    return x, y, c, [(1 - x, y), (x, 1 - y), (1 - x, 1 - y)]


def _my_chip():
    return 2 * lax.axis_index("x") + lax.axis_index("y")


def gather_shards(pack):
    _, rows, lanes = pack.shape

    def body(p_ref, o_ref, send_sems, recv_sems):
        x, y, c, chips = _place()

        def copy(k, chip, half, to, src=None):
            dst = o_ref.at[2 * chip[0] + chip[1], half]
            return pltpu.make_async_remote_copy(src_ref=dst if src is None else src, dst_ref=dst,
                                                send_sem=send_sems.at[k], recv_sem=recv_sems.at[k],
                                                device_id=to, device_id_type=MESH)

        first = [copy(j, (x, y), c, (*chip, c), src=p_ref.at[c]) for j, chip in enumerate(chips)]
        for cp in first:
            cp.start()
        passed = [copy(3 + j, chip, c, (x, y, 1 - c)) for j, chip in enumerate(chips)]
        for j, chip in enumerate(chips):
            copy(j, chip, c, (x, y, c)).wait_recv()
            passed[j].start()
        for j, chip in enumerate(chips):
            copy(3 + j, chip, 1 - c, (x, y, c)).wait_recv()
        for cp in first + passed:
            cp.wait_send()

    gathered = pl.pallas_call(
        body, in_specs=[ANY], out_specs=ANY,
        out_shape=jax.ShapeDtypeStruct((4, 2, rows, lanes), pack.dtype),
        scratch_shapes=[pltpu.SemaphoreType.DMA((6,)), pltpu.SemaphoreType.DMA((6,))],
        name="gather_shards",
    )(pack)
    return lax.dynamic_update_slice(gathered, pack[None], (_my_chip(), 0, 0, 0))


def swap_other_halves(ps):
    n = len(ps)

    def body(*refs):
        p_refs, o_refs, (send_sems, recv_sems) = refs[:n], refs[n:2 * n], refs[2 * n:]
        x, y, c, _ = _place()
        copies = [pltpu.make_async_remote_copy(src_ref=p_refs[a].at[s, 1 - c], dst_ref=o_refs[a].at[s],
                                               send_sem=send_sems.at[4 * a + s], recv_sem=recv_sems.at[4 * a + s],
                                               device_id=(x, y, 1 - c), device_id_type=MESH)
                  for a in range(n) for s in range(4)]
        for cp in copies:
            cp.start()
        for cp in copies:
            cp.wait()

    return pl.pallas_call(
        body, in_specs=[ANY] * n, out_specs=[ANY] * n,
        out_shape=[jax.ShapeDtypeStruct((4,) + p.shape[2:], p.dtype) for p in ps],
        scratch_shapes=[pltpu.SemaphoreType.DMA((4 * n,)), pltpu.SemaphoreType.DMA((4 * n,))],
        name="swap_other_halves",
    )(*ps)


def scatter_to_chips(qs):
    n = len(qs)

    def body(*refs):
        q_refs, o_refs, (send_sems, recv_sems) = refs[:n], refs[n:2 * n], refs[2 * n:]
        x, y, c, chips = _place()
        me = 2 * x + y
        copies = [pltpu.make_async_remote_copy(src_ref=q_refs[a].at[2 * chip[0] + chip[1]], dst_ref=o_refs[a].at[me],
                                               send_sem=send_sems.at[3 * a + j], recv_sem=recv_sems.at[3 * a + j],
                                               device_id=(*chip, c), device_id_type=MESH)
                  for a in range(n) for j, chip in enumerate(chips)]
        for cp in copies:
            cp.start()
        for a in range(n):
            for j, chip in enumerate(chips):
                slot = o_refs[a].at[2 * chip[0] + chip[1]]
                pltpu.make_async_remote_copy(src_ref=slot, dst_ref=slot, send_sem=send_sems.at[3 * a + j],
                                             recv_sem=recv_sems.at[3 * a + j],
                                             device_id=(x, y, c), device_id_type=MESH).wait_recv()
        for cp in copies:
            cp.wait_send()

    received = pl.pallas_call(
        body, in_specs=[ANY] * n, out_specs=[ANY] * n,
        out_shape=[jax.ShapeDtypeStruct(q.shape, q.dtype) for q in qs],
        scratch_shapes=[pltpu.SemaphoreType.DMA((3 * n,)), pltpu.SemaphoreType.DMA((3 * n,))],
        name="scatter_to_chips",
    )(*qs)
    me = _my_chip()
    return [lax.dynamic_update_slice(r, lax.dynamic_slice_in_dim(q, me, 1, axis=0), (me, 0, 0))
            for r, q in zip(received, qs)]


def join_halves(boths):
    n = len(boths)

    def body(*refs):
        o_refs, (send_sems, recv_sems) = refs[n:2 * n], refs[2 * n:]
        x, y, c, _ = _place()
        copies = [pltpu.make_async_remote_copy(src_ref=o_refs[a].at[c], dst_ref=o_refs[a].at[c],
                                               send_sem=send_sems.at[a], recv_sem=recv_sems.at[a],
                                               device_id=(x, y, 1 - c), device_id_type=MESH) for a in range(n)]
        for cp in copies:
            cp.start()
        for a in range(n):
            other = o_refs[a].at[1 - c]
            pltpu.make_async_remote_copy(src_ref=other, dst_ref=other, send_sem=send_sems.at[a],
                                         recv_sem=recv_sems.at[a], device_id=(x, y, c), device_id_type=MESH).wait_recv()
        for cp in copies:
            cp.wait_send()

    return pl.pallas_call(
        body, in_specs=[ANY] * n, out_specs=[ANY] * n,
        out_shape=[jax.ShapeDtypeStruct(b.shape, b.dtype) for b in boths],
        input_output_aliases={a: a for a in range(n)},
        scratch_shapes=[pltpu.SemaphoreType.DMA((n,)), pltpu.SemaphoreType.DMA((n,))], name="join_halves",
    )(*boths)


def add_own_half(p, other, c, out_dtype, name):
    _, _, rows, lanes = p.shape
    tr = _tile(rows, 4096, 16)

    def body(c_ref, p_ref, o_ref, out_ref):
        out_ref[...] = (p_ref[...] + o_ref[...]).astype(out_dtype)

    return pl.pallas_call(
        body,
        grid_spec=pltpu.PrefetchScalarGridSpec(
            num_scalar_prefetch=1, grid=(4, rows // tr),
            in_specs=[pl.BlockSpec((None, None, tr, lanes), lambda s, i, c_ref: (s, c_ref[0], i, 0)),
                      pl.BlockSpec((None, tr, lanes), lambda s, i, c_ref: (s, i, 0))],
            out_specs=pl.BlockSpec((None, tr, lanes), lambda s, i, c_ref: (s, i, 0))),
        out_shape=jax.ShapeDtypeStruct((4, rows, lanes), out_dtype),
        compiler_params=_params("parallel", "parallel"), name=name,
    )(c, p, other)


def sum_chips(b, c, name):
    _, rows, lanes = b.shape
    tr = _tile(rows, 4096, 16)

    def body(c_ref, b_ref, out_ref):
        b0, b1, b2, b3 = (b_ref[k].astype(F32) for k in range(4))
        out_ref[...] = ((b0 + b1) + b2) + b3

    return pl.pallas_call(
        body,
        grid_spec=pltpu.PrefetchScalarGridSpec(
            num_scalar_prefetch=1, grid=(rows // tr,),
            in_specs=[pl.BlockSpec((4, tr, lanes), lambda i, c_ref: (0, i, 0))],
            out_specs=pl.BlockSpec((None, tr, lanes), lambda i, c_ref: (c_ref[0], i, 0))),
        out_shape=jax.ShapeDtypeStruct((2, rows, lanes), F32),
        compiler_params=_params("parallel"), name=name,
    )(c, b)


def all_reduce_to_shards(main, side, c):
    packs, wires, tags = (main, side), (GRAD_WIRE, F32), ("main", "side")
    others = swap_other_halves(packs)
    qs = [add_own_half(p, o, c, wire, f"add_own_half_{tag}") for p, o, wire, tag in zip(packs, others, wires, tags)]
    boths = join_halves([sum_chips(r, c, f"sum_chips_{tag}") for r, tag in zip(scatter_to_chips(qs), tags)])
    return [b.reshape(2 * b.shape[1], b.shape[2]) for b in boths]


def adamw(w, g, m, v, name):
    shape = w.shape
    cols = shape[-1]
    w2, g2, m2, v2 = (a.reshape(-1, cols) for a in (w, g, m, v))
    rows = w2.shape[0]
    tr = _tile(rows, max(8, 262144 // cols), 8) if rows % 8 == 0 else rows
    c1 = 1.0 - ADAM_B1 ** ADAM_STEP
    c2 = 1.0 - ADAM_B2 ** ADAM_STEP

    def body(w_ref, g_ref, m_ref, v_ref, d_ref, mo_ref, vo_ref):
        gv = g_ref[...]
        mn = ADAM_B1 * m_ref[...] + (1.0 - ADAM_B1) * gv
        vn = ADAM_B2 * v_ref[...] + (1.0 - ADAM_B2) * jnp.square(gv)
        d_ref[...] = -ADAM_LR * ((mn / c1) / (jnp.sqrt(vn / c2) + ADAM_EPS) + ADAM_WD * w_ref[...])
        mo_ref[...] = mn
        vo_ref[...] = vn

    spec = pl.BlockSpec((tr, cols), lambda i: (i, 0))
    out = jax.ShapeDtypeStruct((rows, cols), F32)
    d, mn, vn = pl.pallas_call(
        body, grid=(rows // tr,), in_specs=[spec] * 4, out_specs=(spec,) * 3, out_shape=(out,) * 3,
        compiler_params=_params("parallel"), name=name,
    )(w2, g2, m2, v2)
    return d.reshape(shape), mn.reshape(shape), vn.reshape(shape)


SHARDED = ("w_in", "w_up", "w_out", "w_down", "w_pool", "conv_qkv", "conv_ffn", "meta_tokens")
MATMUL_WEIGHTS = ("w_in", "w_up", "w_out", "w_down", "w_pool")
REPLICATED = ("norm_mix", "a_log", "dt_bias", "head_norm", "pool_scale", "norm_ffn", "norm_final")
SHARD_AXIS = {"w_in": 2, "w_up": 2, "w_out": 1, "w_down": 1, "w_pool": 3, "conv_qkv": 2, "conv_ffn": 2, "meta_tokens": 1}


def _rows_of(a):
    return a.reshape(-1, 128)


SEGMENT_ROWS = 16


def _segment(n_rows):
    return -(-n_rows // SEGMENT_ROWS) * SEGMENT_ROWS


def _pad_segment(a):
    pad = [(0, 0)] * a.ndim
    pad[-2] = (0, _segment(a.shape[-2]) - a.shape[-2])
    return jnp.pad(a, pad)


def _unshard(stacked, axis):
    full = jnp.moveaxis(stacked, 0, axis)
    shape = list(full.shape)
    shape[axis:axis + 2] = [shape[axis] * shape[axis + 1]]
    return full.reshape(shape)


def _shard_stack(full, axis):
    shape = list(full.shape)
    shape[axis:axis + 1] = [4, shape[axis] // 4]
    return jnp.moveaxis(full.reshape(shape), axis, 0)


def pack_weights(shards):
    parts = [_rows_of(shards[k].astype(WIRE)) for k in MATMUL_WEIGHTS]
    parts += [lax.bitcast_convert_type(_rows_of(shards[k].astype(F32)), WIRE).reshape(-1, 128)
              for k in SHARDED if k not in MATMUL_WEIGHTS]
    parts = [_pad_segment(p) for p in parts]
    rows = sum(p.shape[0] for p in parts)
    if rows % (2 * SEGMENT_ROWS):
        parts.append(jnp.zeros((SEGMENT_ROWS, 128), WIRE))
        rows += SEGMENT_ROWS
    return jnp.concatenate(parts, axis=0).reshape(2, rows // 2, 128)


def unpack_weights(gathered, shard_shapes):
    flat = gathered.reshape(4, -1, 128)
    out, at = {}, 0
    for k in SHARDED:
        shp = shard_shapes[k]
        n = 1
        for e in shp:
            n *= e
        if k in MATMUL_WEIGHTS:
            r = n // 128
            stacked = flat[:, at:at + r].reshape((4,) + tuple(shp))
        else:
            r = 2 * n // 128
            stacked = lax.bitcast_convert_type(flat[:, at:at + r].reshape(4, n // 128, 128, 2), F32).reshape((4,) + tuple(shp))
        out[k] = _unshard(stacked, SHARD_AXIS[k])
        at += _segment(r)
    return out


SIDE = tuple(k for k in SHARDED if k not in MATMUL_WEIGHTS)


def _join_pack(parts):
    parts = [_pad_segment(p) for p in parts]
    rows = sum(p.shape[1] for p in parts)
    if rows % (2 * SEGMENT_ROWS):
        parts.append(jnp.zeros((4, SEGMENT_ROWS, 128), F32))
        rows += SEGMENT_ROWS
    return jnp.concatenate(parts, axis=1).reshape(4, 2, rows // 2, 128)


def pack_grads(full, repl):
    cut = lambda k: _shard_stack(full[k], SHARD_AXIS[k]).reshape(4, -1, 128)
    r = jnp.concatenate([repl[k].reshape(-1) for k in REPLICATED])
    r = jnp.pad(r, (0, -r.shape[0] % 128)).reshape(1, -1, 128)
    side = [cut(k) for k in SIDE] + [jnp.broadcast_to(r, (4,) + r.shape[1:])]
    return _join_pack([cut(k) for k in MATMUL_WEIGHTS]), _join_pack(side)


def unpack_grads(main, side, shard_shapes, repl_shapes):
    out = {}
    for flat, names in ((main, MATMUL_WEIGHTS), (side, SIDE)):
        at = 0
        for k in names:
            n = 1
            for e in shard_shapes[k]:
                n *= e
            out[k] = flat[at:at + n // 128].reshape(shard_shapes[k])
            at += _segment(n // 128)
    r = side[at:].reshape(-1)
    at = 0
    for k in REPLICATED:
        n = 1
        for e in repl_shapes[k]:
            n *= e
        out[k] = r[at:at + n].reshape(repl_shapes[k])
        at += n
    return out


WEIGHT_ORDER = ("meta_tokens", "norm_mix", "w_in", "conv_qkv", "a_log", "dt_bias", "head_norm", "w_pool", "pool_scale",
                "w_out", "norm_ffn", "w_up", "conv_ffn", "w_down", "norm_final")


def kernel(x, meta_tokens, norm_mix, w_in, conv_qkv, a_log, dt_bias, head_norm, w_pool, pool_scale, w_out, norm_ffn, w_up, conv_ffn, w_down, norm_final, loss_target, m_meta_tokens, m_norm_mix, m_w_in, m_conv_qkv, m_a_log, m_dt_bias, m_head_norm, m_w_pool, m_pool_scale, m_w_out, m_norm_ffn, m_w_up, m_conv_ffn, m_w_down, m_norm_final, v_meta_tokens, v_norm_mix, v_w_in, v_conv_qkv, v_a_log, v_dt_bias, v_head_norm, v_w_pool, v_pool_scale, v_w_out, v_norm_ffn, v_w_up, v_conv_ffn, v_w_down, v_norm_final):
    weights = dict(meta_tokens=meta_tokens, norm_mix=norm_mix, w_in=w_in, conv_qkv=conv_qkv, a_log=a_log,
                   dt_bias=dt_bias, head_norm=head_norm, w_pool=w_pool, pool_scale=pool_scale, w_out=w_out,
                   norm_ffn=norm_ffn, w_up=w_up, conv_ffn=conv_ffn, w_down=w_down, norm_final=norm_final)
    m_in = dict(zip(WEIGHT_ORDER, (m_meta_tokens, m_norm_mix, m_w_in, m_conv_qkv, m_a_log, m_dt_bias, m_head_norm,
                                   m_w_pool, m_pool_scale, m_w_out, m_norm_ffn, m_w_up, m_conv_ffn, m_w_down, m_norm_final)))
    v_in = dict(zip(WEIGHT_ORDER, (v_meta_tokens, v_norm_mix, v_w_in, v_conv_qkv, v_a_log, v_dt_bias, v_head_norm,
                                   v_w_pool, v_pool_scale, v_w_out, v_norm_ffn, v_w_up, v_conv_ffn, v_w_down, v_norm_final)))
    shard_shapes = {k: weights[k].shape for k in SHARDED}
    repl_shapes = {k: weights[k].shape for k in REPLICATED}
    core = lax.axis_index("c").astype(jnp.int32).reshape(1)

    full = unpack_weights(gather_shards(pack_weights({k: weights[k] for k in SHARDED})), shard_shapes)
    layers = []
    for li in range(DEPTH):
        p = {k: (full[k][li] if k in full else weights[k][li]) for k in LAYER_PARAMS}
        layers.append(prep_layer(p))

    h0 = pad_rows(full["meta_tokens"], x[0])
    target = pad_rows(jnp.zeros((N_META, D_MODEL), F32), loss_target[0])
    loss, dh0, grads, d_norm_final = local_step(h0, target, layers, norm_final.reshape(1, D_MODEL))
    seq = x.shape[1]
    grad_x = dh0[LEAD + N_META:LEAD + N_META + seq][None]

    per_layer = [layer_grads(g) for g in grads]
    g_all = {k: jnp.stack([pl_[k] for pl_ in per_layer]) for k in LAYER_PARAMS}
    g_all["meta_tokens"] = dh0[LEAD:LEAD + N_META]
    g_all["norm_final"] = d_norm_final[0]
    main, side = pack_grads({k: g_all[k] for k in SHARDED}, {k: g_all[k] for k in REPLICATED})
    g_mine = unpack_grads(*all_reduce_to_shards(main, side, core), shard_shapes, repl_shapes)

    loss_sum = lax.psum(loss[0, 0], ("x", "y", "c"))
    deltas, new_m, new_v = {}, {}, {}
    for k in WEIGHT_ORDER:
        deltas[k], new_m[k], new_v[k] = adamw(weights[k], g_mine[k], m_in[k], v_in[k], f"adamw_{k}")
    return (loss_sum, grad_x, *[g_mine[k] for k in WEIGHT_ORDER], *[deltas[k] for k in WEIGHT_ORDER],
            *[new_m[k] for k in WEIGHT_ORDER], *[new_v[k] for k in WEIGHT_ORDER])
```

```python
import functools

import jax
import jax.numpy as jnp
from jax import lax
from jax.experimental import pallas as pl
from jax.experimental.pallas import tpu as pltpu

F32 = jnp.float32
BF16 = jnp.bfloat16
WIRE = jnp.bfloat16
GRAD_WIRE = jnp.bfloat16

D_MODEL = 1024
HEADS = 8
HEAD_DIM = 128
CHUNK = 64
N_META = 16
LEAD = 48
TAIL = 64
QKV_DIM = 3072
D_FF = 2816
POOL_WIDTH = 512
POOL_WINDOWS = (2, 4, 8, 16)
DEPTH = 2
NORM_EPS = 1e-6
ADAM_LR, ADAM_B1, ADAM_B2, ADAM_EPS, ADAM_WD, ADAM_STEP = 0.001, 0.9, 0.999, 1e-08, 0.01, 10
VMEM_LIMIT_BYTES = 48 * 1024 * 1024


def _params(*sem):
    return pltpu.CompilerParams(dimension_semantics=sem if sem else None, vmem_limit_bytes=VMEM_LIMIT_BYTES)


def _tile(n, cap, mult):
    best = None
    for t in range(mult, min(n, cap) + 1, mult):
        if n % t == 0:
            best = t
    assert best is not None, (n, cap, mult)
    return best


def _silu(x):
    return x * jax.nn.sigmoid(x)


def _softplus(x):
    return jnp.maximum(x, 0.0) + jnp.log(1.0 + jnp.exp(-jnp.abs(x)))


def _split_bf16(a):
    hi = a.astype(BF16)
    return hi, (a - hi.astype(F32)).astype(BF16)


def _dg(a, b, ca, cb, hi):
    dims = (((ca,), (cb,)), ((), ()))
    if hi is True:
        return lax.dot_general(a, b, dims, precision=lax.Precision.HIGHEST, preferred_element_type=F32)
    if hi == 3:
        (ah, al), (bh, bl) = _split_bf16(a), _split_bf16(b)
        dot = lambda x, y: lax.dot_general(x, y, dims, preferred_element_type=F32)
        return dot(ah, bh) + (dot(ah, bl) + dot(al, bh))
    return lax.dot_general(a.astype(BF16), b.astype(BF16), dims, preferred_element_type=F32)


def _make_dots(hi):
    @jax.custom_vjp
    def nn(a, b):
        return _dg(a, b, 1, 0, hi)

    @jax.custom_vjp
    def nt(a, b):
        return _dg(a, b, 1, 1, hi)

    @jax.custom_vjp
    def tn(a, b):
        return _dg(a, b, 0, 0, hi)

    nn.defvjp(lambda a, b: (nn(a, b), (a, b)), lambda r, g: (nt(g, r[1]), tn(r[0], g)))
    nt.defvjp(lambda a, b: (nt(a, b), (a, b)), lambda r, g: (nn(g, r[1]), tn(g, r[0])))
    tn.defvjp(lambda a, b: (tn(a, b), (a, b)), lambda r, g: (nt(r[1], g), nn(r[0], g)))
    return nn, nt, tn


_nn, _nt, _tn = _make_dots(False)
_hnn, _hnt, _htn = _make_dots(True)
_nn3, _nt3, _tn3 = _make_dots(3)


def mm(a, b, *, tb=False, add=None, out_dtype=F32, name):
    m, kdim = a.shape
    (n, kb) = b.shape if tb else b.shape[::-1]
    assert kdim == kb, (a.shape, b.shape, tb)
    tm = _tile(m, 1408, 128) if m % 128 == 0 and m <= 4096 else _tile(m, 640, 64)
    tn = _tile(n, 1536, 128)
    tk = kdim if kdim <= 3072 else _tile(kdim, 1664 if a.dtype == b.dtype == BF16 else 640, 128)
    nk = kdim // tk
    dims = (((1,), (1 if tb else 0,)), ((), ()))

    def body(*refs):
        if add is not None:
            a_ref, b_ref, add_ref, o_ref, acc = refs
        else:
            a_ref, b_ref, o_ref, acc = refs
        k = pl.program_id(2)
        part = lax.dot_general(a_ref[...].astype(BF16), b_ref[...].astype(BF16), dims, preferred_element_type=F32)

        def finish(r):
            if add is not None:
                r = r + add_ref[...]
            o_ref[...] = r.astype(out_dtype)

        if nk == 1:
            finish(part)
        else:
            @pl.when(k == 0)
            def _():
                acc[...] = part

            @pl.when(jnp.logical_and(k > 0, k < nk - 1))
            def _():
                acc[...] += part

            @pl.when(k == nk - 1)
            def _():
                finish(acc[...] + part)

    a_spec = pl.BlockSpec((tm, tk), lambda j, i, k: (i, k))
    b_spec = pl.BlockSpec((tn, tk), lambda j, i, k: (j, k)) if tb else pl.BlockSpec((tk, tn), lambda j, i, k: (k, j))
    in_specs = [a_spec, b_spec]
    args = [a, b]
    if add is not None:
        in_specs.append(pl.BlockSpec((tm, tn), lambda j, i, k: (i, j)))
        args.append(add)
    return pl.pallas_call(
        body, grid=(n // tn, m // tm, nk), in_specs=in_specs,
        out_specs=pl.BlockSpec((tm, tn), lambda j, i, k: (i, j)),
        out_shape=jax.ShapeDtypeStruct((m, n), out_dtype),
        scratch_shapes=[pltpu.VMEM((tm, tn) if nk > 1 else (8, 128), F32)],
        compiler_params=_params("parallel", "parallel", "arbitrary"), name=name,
    )(*args)


def _rms(x, gain):
    return x * lax.rsqrt(jnp.mean(x * x, axis=-1, keepdims=True) + NORM_EPS) * gain


def rms_fwd(h, gain, name):
    t = h.shape[0]
    ts = _tile(t, 640, 128)

    def body(h_ref, g_ref, u_ref, ut_ref):
        u = _rms(h_ref[...], g_ref[...])
        u_ref[...] = u.astype(BF16)
        ut_ref[...] = u.T.astype(BF16)

    return pl.pallas_call(
        body, grid=(t // ts,),
        in_specs=[pl.BlockSpec((ts, D_MODEL), lambda i: (i, 0)), pl.BlockSpec((1, D_MODEL), lambda i: (0, 0))],
        out_specs=(pl.BlockSpec((ts, D_MODEL), lambda i: (i, 0)), pl.BlockSpec((D_MODEL, ts), lambda i: (0, i))),
        out_shape=(jax.ShapeDtypeStruct((t, D_MODEL), BF16), jax.ShapeDtypeStruct((D_MODEL, t), BF16)),
        compiler_params=_params("parallel"), name=name,
    )(h, gain)


def rms_bwd(h, gain, du, dres, name):
    t = h.shape[0]
    ts = _tile(t, 640, 64)

    def body(h_ref, g_ref, du_ref, dres_ref, dh_ref, dg_ref):
        i = pl.program_id(0)
        _, vjp = jax.vjp(_rms, h_ref[...], g_ref[...])
        dx, dg = vjp(du_ref[...])
        row = i * ts + lax.broadcasted_iota(jnp.int32, (ts, 1), 0)
        dh_ref[...] = jnp.where(row >= LEAD, dx + dres_ref[...], 0.0)

        @pl.when(i == 0)
        def _():
            dg_ref[...] = jnp.zeros_like(dg_ref)

        dg_ref[...] += dg

    row_spec = pl.BlockSpec((ts, D_MODEL), lambda i: (i, 0))
    vec_spec = pl.BlockSpec((1, D_MODEL), lambda i: (0, 0))
    return pl.pallas_call(
        body, grid=(t // ts,), in_specs=[row_spec, vec_spec, row_spec, row_spec],
        out_specs=(row_spec, vec_spec),
        out_shape=(jax.ShapeDtypeStruct((t, D_MODEL), F32), jax.ShapeDtypeStruct((1, D_MODEL), F32)),
        compiler_params=_params("arbitrary"), name=name,
    )(h, gain, du, dres)


def loss_head(h, gain, target, name):
    t = h.shape[0]
    ts = _tile(t, 640, 64)

    def body(h_ref, g_ref, t_ref, loss_ref, dh_ref, dg_ref):
        i = pl.program_id(0)
        row = i * ts + lax.broadcasted_iota(jnp.int32, (ts, 1), 0)
        keep = jnp.logical_and(row >= LEAD + N_META, row < t - TAIL)
        tgt = t_ref[...]

        def f(x, g):
            err = jnp.where(keep, _rms(x, g) - tgt, 0.0)
            per_row = jnp.mean(err * err, axis=-1, keepdims=True)
            return 0.5 * jnp.sum(per_row, axis=0, keepdims=True)

        val, vjp = jax.vjp(f, h_ref[...], g_ref[...])
        dx, dg = vjp(jnp.ones((1, 1), F32))
        dh_ref[...] = dx

        @pl.when(i == 0)
        def _():
            dg_ref[...] = jnp.zeros_like(dg_ref)
            loss_ref[...] = jnp.zeros_like(loss_ref)

        dg_ref[...] += dg
        loss_ref[...] += jnp.broadcast_to(val, (1, 128))

    row_spec = pl.BlockSpec((ts, D_MODEL), lambda i: (i, 0))
    vec_spec = pl.BlockSpec((1, D_MODEL), lambda i: (0, 0))
    return pl.pallas_call(
        body, grid=(t // ts,), in_specs=[row_spec, vec_spec, row_spec],
        out_specs=(pl.BlockSpec((1, 128), lambda i: (0, 0)), row_spec, vec_spec),
        out_shape=(jax.ShapeDtypeStruct((1, 128), F32), jax.ShapeDtypeStruct((t, D_MODEL), F32),
                   jax.ShapeDtypeStruct((1, D_MODEL), F32)),
        compiler_params=_params("arbitrary"), name=name,
    )(h, gain, target)


def conv_fwd(x, w, name):
    t, width = x.shape
    k = w.shape[0]
    ts = _tile(t, 640, 64)
    tw = _tile(width, 1536, 128)
    hb = ts // 8

    def body(x_ref, halo_ref, w_ref, o_ref, buf):
        i = pl.program_id(0)
        buf[0:8, :] = jnp.where(i > 0, halo_ref[...], 0.0)
        buf[8:, :] = x_ref[...]
        wv = w_ref[...]
        acc = buf[pl.ds(8 - (k - 1), ts), :] * wv[0:1, :]
        for j in range(1, k):
            acc = acc + buf[pl.ds(8 - (k - 1) + j, ts), :] * wv[j:j + 1, :]
        o_ref[...] = acc

    return pl.pallas_call(
        body, grid=(t // ts, width // tw),
        in_specs=[pl.BlockSpec((ts, tw), lambda i, j: (i, j)),
                  pl.BlockSpec((8, tw), lambda i, j: (jnp.maximum(i * hb - 1, 0), j)),
                  pl.BlockSpec((k, tw), lambda i, j: (0, j))],
        out_specs=pl.BlockSpec((ts, tw), lambda i, j: (i, j)),
        out_shape=jax.ShapeDtypeStruct((t, width), F32),
        scratch_shapes=[pltpu.VMEM((ts + 8, tw), F32)],
        compiler_params=_params("parallel", "parallel"), name=name,
    )(x, x, w)


def conv_bwd(x, dc, w, name):
    t, width = x.shape
    k = w.shape[0]
    ts = _tile(t, 640, 64)
    tw = _tile(width, 1536, 128)
    hb = ts // 8
    nt = t // ts

    def body(x_ref, xh_ref, dc_ref, dch_ref, w_ref, dx_ref, dw_ref, xbuf, dbuf):
        i = pl.program_id(1)
        xbuf[0:8, :] = jnp.where(i > 0, xh_ref[...], 0.0)
        xbuf[8:, :] = x_ref[...]
        d = dc_ref[...]
        dbuf[0:ts, :] = d
        dbuf[ts:, :] = jnp.where(i < nt - 1, dch_ref[...], 0.0)
        wv = w_ref[...]
        acc = dbuf[pl.ds(k - 1, ts), :] * wv[0:1, :]
        for j in range(1, k):
            acc = acc + dbuf[pl.ds(k - 1 - j, ts), :] * wv[j:j + 1, :]
        dx_ref[...] = acc.astype(BF16)

        @pl.when(i == 0)
        def _():
            dw_ref[...] = jnp.zeros_like(dw_ref)

        for j in range(k):
            dw_ref[j:j + 1, :] += jnp.sum(d * xbuf[pl.ds(8 - (k - 1) + j, ts), :], axis=0, keepdims=True)

    return pl.pallas_call(
        body, grid=(width // tw, nt),
        in_specs=[pl.BlockSpec((ts, tw), lambda j, i: (i, j)),
                  pl.BlockSpec((8, tw), lambda j, i: (jnp.maximum(i * hb - 1, 0), j)),
                  pl.BlockSpec((ts, tw), lambda j, i: (i, j)),
                  pl.BlockSpec((8, tw), lambda j, i: (jnp.minimum((i + 1) * hb, t // 8 - 1), j)),
                  pl.BlockSpec((k, tw), lambda j, i: (0, j))],
        out_specs=(pl.BlockSpec((ts, tw), lambda j, i: (i, j)), pl.BlockSpec((8, tw), lambda j, i: (0, j))),
        out_shape=(jax.ShapeDtypeStruct((t, width), BF16), jax.ShapeDtypeStruct((8, width), F32)),
        scratch_shapes=[pltpu.VMEM((ts + 8, tw), F32), pltpu.VMEM((ts + 8, tw), F32)],
        compiler_params=_params("parallel", "arbitrary"), name=name,
    )(x, x, dc, dc, w)


def _pool_count(pos, win):
    return jnp.clip(pos + 1, 1, win).astype(F32)


def poolwin_fwd(p, name):
    t = p.shape[0]
    ts = _tile(t, 640, 64)
    hb = ts // 16

    def body(p_ref, halo_ref, o_ref, buf):
        i = pl.program_id(0)
        buf[0:16, :] = jnp.where(i > 0, halo_ref[...], 0.0)
        buf[16:, :] = p_ref[...]
        pos = i * ts + lax.broadcasted_iota(jnp.int32, (ts, 1), 0) - LEAD
        for gi, win in enumerate(POOL_WINDOWS):
            cols = slice(gi * 128, (gi + 1) * 128)
            own = buf[pl.ds(16, ts), cols]
            acc = own
            for j in range(1, win):
                acc = acc + buf[pl.ds(16 - j, ts), cols]
            o_ref[:, cols] = acc / _pool_count(pos, win) - own

    return pl.pallas_call(
        body, grid=(t // ts,),
        in_specs=[pl.BlockSpec((ts, POOL_WIDTH), lambda i: (i, 0)),
                  pl.BlockSpec((16, POOL_WIDTH), lambda i: (jnp.maximum(i * hb - 1, 0), 0))],
        out_specs=pl.BlockSpec((ts, POOL_WIDTH), lambda i: (i, 0)),
        out_shape=jax.ShapeDtypeStruct((t, POOL_WIDTH), F32),
        scratch_shapes=[pltpu.VMEM((ts + 16, POOL_WIDTH), F32)],
        compiler_params=_params("parallel"), name=name,
    )(p, p)


def poolwin_bwd(dpooled, name):
    t = dpooled.shape[0]
    ts = _tile(t, 640, 64)
    hb = ts // 16
    nt = t // ts

    def body(d_ref, halo_ref, o_ref, buf):
        i = pl.program_id(0)
        buf[0:ts, :] = d_ref[...]
        buf[ts:, :] = jnp.where(i < nt - 1, halo_ref[...], 0.0)
        pos = i * ts + lax.broadcasted_iota(jnp.int32, (ts, 1), 0) - LEAD
        for gi, win in enumerate(POOL_WINDOWS):
            cols = slice(gi * 128, (gi + 1) * 128)
            own = buf[pl.ds(0, ts), cols]
            acc = own / _pool_count(pos, win)
            for j in range(1, win):
                acc = acc + buf[pl.ds(j, ts), cols] / _pool_count(pos + j, win)
            o_ref[:, cols] = (acc - own).astype(BF16)

    return pl.pallas_call(
        body, grid=(nt,),
        in_specs=[pl.BlockSpec((ts, POOL_WIDTH), lambda i: (i, 0)),
                  pl.BlockSpec((16, POOL_WIDTH), lambda i: (jnp.minimum((i + 1) * hb, t // 16 - 1), 0))],
        out_specs=pl.BlockSpec((ts, POOL_WIDTH), lambda i: (i, 0)),
        out_shape=jax.ShapeDtypeStruct((t, POOL_WIDTH), BF16),
        scratch_shapes=[pltpu.VMEM((ts + 16, POOL_WIDTH), F32)],
        compiler_params=_params("parallel"), name=name,
    )(dpooled, dpooled)


def _mix(y_a, gpre, pooled, w_pool, scale):
    parts = [_nn(pooled[:, g * 128:(g + 1) * 128], w_pool[g]) for g in range(4)]
    y_b = jnp.concatenate(parts, axis=1) * scale
    return jax.nn.sigmoid(gpre[:, :D_MODEL]) * y_a + jax.nn.sigmoid(gpre[:, D_MODEL:]) * y_b


def _mix_specs(ts):
    return [pl.BlockSpec((ts, D_MODEL), lambda i: (i, 0)), pl.BlockSpec((ts, 2 * D_MODEL), lambda i: (i, 0)),
            pl.BlockSpec((ts, POOL_WIDTH), lambda i: (i, 0)), pl.BlockSpec((4, 128, 256), lambda i: (0, 0, 0)),
            pl.BlockSpec((1, D_MODEL), lambda i: (0, 0))]


def mix_fwd(y_a, gpre, pooled, w_pool, scale, name):
    t = y_a.shape[0]
    ts = _tile(t, 640, 128)

    def body(ya_ref, g_ref, p_ref, w_ref, s_ref, o_ref, ot_ref):
        y = _mix(ya_ref[...], g_ref[...], p_ref[...], w_ref[...], s_ref[...])
        o_ref[...] = y.astype(BF16)
        ot_ref[...] = y.T.astype(BF16)

    return pl.pallas_call(
        body, grid=(t // ts,), in_specs=_mix_specs(ts),
        out_specs=(pl.BlockSpec((ts, D_MODEL), lambda i: (i, 0)), pl.BlockSpec((D_MODEL, ts), lambda i: (0, i))),
        out_shape=(jax.ShapeDtypeStruct((t, D_MODEL), BF16), jax.ShapeDtypeStruct((D_MODEL, t), BF16)),
        compiler_params=_params("parallel"), name=name,
    )(y_a, gpre, pooled, w_pool, scale)


def mix_bwd(y_a, gpre, pooled, w_pool, scale, dy, name):
    t = y_a.shape[0]
    ts = _tile(t, 320, 64)

    def body(ya_ref, g_ref, p_ref, w_ref, s_ref, dy_ref, dya_ref, dg_ref, dp_ref, dw_ref, ds_ref):
        i = pl.program_id(0)
        _, vjp = jax.vjp(_mix, ya_ref[...], g_ref[...], p_ref[...], w_ref[...], s_ref[...])
        dya, dg, dp, dw, ds = vjp(dy_ref[...])
        dya_ref[...] = dya
        dg_ref[...] = dg.astype(BF16)
        dp_ref[...] = dp

        @pl.when(i == 0)
        def _():
            dw_ref[...] = jnp.zeros_like(dw_ref)
            ds_ref[...] = jnp.zeros_like(ds_ref)

        dw_ref[...] += dw
        ds_ref[...] += ds

    specs = _mix_specs(ts)
    return pl.pallas_call(
        body, grid=(t // ts,), in_specs=specs + [specs[0]],
        out_specs=(specs[0], specs[1], specs[2], specs[3], specs[4]),
        out_shape=(jax.ShapeDtypeStruct((t, D_MODEL), F32), jax.ShapeDtypeStruct((t, 2 * D_MODEL), BF16),
                   jax.ShapeDtypeStruct((t, POOL_WIDTH), F32), jax.ShapeDtypeStruct((4, 128, 256), F32),
                   jax.ShapeDtypeStruct((1, D_MODEL), F32)),
        compiler_params=_params("arbitrary"), name=name,
    )(y_a, gpre, pooled, w_pool, scale, dy)


def _ffn_act(cg, cv):
    return _silu(cg) * cv


def ffnact_fwd(cg, cv, name):
    t, width = cg.shape
    ts = _tile(t, 640, 128)
    tw = _tile(width, 1536, 128)
    spec = pl.BlockSpec((ts, tw), lambda i, j: (i, j))

    def body(g_ref, v_ref, o_ref, ot_ref):
        act = _ffn_act(g_ref[...], v_ref[...])
        o_ref[...] = act.astype(BF16)
        ot_ref[...] = act.T.astype(BF16)

    return pl.pallas_call(
        body, grid=(t // ts, width // tw), in_specs=[spec, spec],
        out_specs=(spec, pl.BlockSpec((tw, ts), lambda i, j: (j, i))),
        out_shape=(jax.ShapeDtypeStruct((t, width), BF16), jax.ShapeDtypeStruct((width, t), BF16)),
        compiler_params=_params("parallel", "parallel"), name=name,
    )(cg, cv)


def ffnact_bwd(cg, cv, dact, name):
    t, width = cg.shape
    ts = _tile(t, 640, 64)
    tw = _tile(width, 1536, 128)
    spec = pl.BlockSpec((ts, tw), lambda i, j: (i, j))

    def body(g_ref, v_ref, d_ref, dg_ref, dv_ref):
        _, vjp = jax.vjp(_ffn_act, g_ref[...], v_ref[...])
        dg_ref[...], dv_ref[...] = vjp(d_ref[...])

    return pl.pallas_call(
        body, grid=(t // ts, width // tw), in_specs=[spec, spec, spec], out_specs=(spec, spec),
        out_shape=(jax.ShapeDtypeStruct((t, width), F32), jax.ShapeDtypeStruct((t, width), F32)),
        compiler_params=_params("parallel", "parallel"), name=name,
    )(cg, cv, dact)


def _gdn_chunk(c, z, ba, pa, pdt, hn, s, *, valid):
    r = lax.broadcasted_iota(jnp.int32, (CHUNK, CHUNK), 0)
    q_ = lax.broadcasted_iota(jnp.int32, (CHUNK, CHUNK), 1)
    causal = r >= q_
    strict = r > q_
    tril = causal.astype(F32)
    triu = (r <= q_).astype(F32)
    eye = (r == q_).astype(F32)
    lane = lax.broadcasted_iota(jnp.int32, (CHUNK, 128), 1)

    decay_log = -jnp.exp(pa) * _softplus(ba + pdt)
    bg = jnp.where(lane < HEADS, jax.nn.sigmoid(ba), jnp.where(lane < 2 * HEADS, decay_log, 0.0))
    bg = jnp.where(valid, bg, 0.0)
    gc = _hnn(tril, bg)
    gct = _hnn(bg.T, triu)
    eg = jnp.exp(gc)
    glast = gc[CHUNK - 1:CHUNK, :]
    ekd = jnp.exp(glast - gc)
    gtot = jnp.exp(glast)

    hd = range(HEADS)
    hs = [slice(h * HEAD_DIM, (h + 1) * HEAD_DIM) for h in hd]
    gl = [slice(HEADS + h, HEADS + h + 1) for h in hd]
    q = [_silu(c[:, hs[h]]) for h in hd]
    k = [_silu(c[:, D_MODEL + h * HEAD_DIM:D_MODEL + (h + 1) * HEAD_DIM]) for h in hd]
    v = [_silu(c[:, 2 * D_MODEL + h * HEAD_DIM:2 * D_MODEL + (h + 1) * HEAD_DIM]) for h in hd]
    q = [q[h] * lax.rsqrt(jnp.sum(q[h] * q[h], axis=-1, keepdims=True) + NORM_EPS) * (HEAD_DIM ** -0.5) for h in hd]
    k = [k[h] * lax.rsqrt(jnp.sum(k[h] * k[h], axis=-1, keepdims=True) + NORM_EPS) for h in hd]
    beta = [bg[:, h:h + 1] for h in hd]
    decay = [jnp.exp(jnp.where(causal, gc[:, gl[h]] - gct[gl[h], :], -1e30)) for h in hd]
    kb = [k[h] * beta[h] for h in hd]
    a = [jnp.where(strict, _nt(kb[h], k[h]) * decay[h], 0.0) for h in hd]
    qk = [jnp.where(causal, _nt(q[h], k[h]) * decay[h], 0.0) for h in hd]
    p = [_nn3(a[h], a[h]) for h in hd]
    x = [(eye - a[h]) + p[h] - _nn(a[h], p[h]) for h in hd]
    for _ in range(4):
        p = [_nn(p[h], p[h]) for h in hd]
        x = [x[h] + p[h] + _nn(x[h] - eye, p[h]) for h in hd]
    u = [_nn(x[h], v[h] * beta[h]) for h in hd]
    w = [_nn(x[h], kb[h] * eg[:, gl[h]]) for h in hd]
    v_new = [u[h] - _nn(w[h], s[h]) for h in hd]
    o = [_nn(q[h] * eg[:, gl[h]], s[h]) + _nn(qk[h], v_new[h]) for h in hd]
    states = [s[h] * gtot[:, gl[h]] + _tn(k[h] * ekd[:, gl[h]], v_new[h]) for h in hd]
    o = [o[h] * lax.rsqrt(jnp.mean(o[h] * o[h], axis=-1, keepdims=True) + NORM_EPS) * hn * _silu(z[:, hs[h]])
         for h in hd]
    return jnp.concatenate(o, axis=1), tuple(states)


def _chunk_valid(n, t):
    row = n * CHUNK + lax.broadcasted_iota(jnp.int32, (CHUNK, 1), 0)
    return jnp.logical_and(row >= LEAD, row < t - TAIL)


def gdn_fwd(c, z, ba, pa, pdt, hn, name):
    t = c.shape[0]
    n_chunks = t // CHUNK

    def body(c_ref, z_ref, ba_ref, pa_ref, pdt_ref, hn_ref, y_ref, ss_ref, state):
        n = pl.program_id(0)

        @pl.when(n == 0)
        def _():
            state[...] = jnp.zeros_like(state)

        s0 = tuple(state[h] for h in range(HEADS))
        for h in range(HEADS):
            ss_ref[0, h] = s0[h]
        y, s1 = _gdn_chunk(c_ref[...], z_ref[...], ba_ref[...], pa_ref[...], pdt_ref[...], hn_ref[...], s0,
                           valid=_chunk_valid(n, t))
        y_ref[...] = y
        for h in range(HEADS):
            state[h] = s1[h]

    vec = pl.BlockSpec((1, 128), lambda n: (0, 0))
    return pl.pallas_call(
        body, grid=(n_chunks,),
        in_specs=[pl.BlockSpec((CHUNK, QKV_DIM), lambda n: (n, 0)), pl.BlockSpec((CHUNK, D_MODEL), lambda n: (n, 0)),
                  pl.BlockSpec((CHUNK, 128), lambda n: (n, 0)), vec, vec, vec],
        out_specs=(pl.BlockSpec((CHUNK, D_MODEL), lambda n: (n, 0)),
                   pl.BlockSpec((1, HEADS, HEAD_DIM, HEAD_DIM), lambda n: (n, 0, 0, 0))),
        out_shape=(jax.ShapeDtypeStruct((t, D_MODEL), F32),
                   jax.ShapeDtypeStruct((n_chunks, HEADS, HEAD_DIM, HEAD_DIM), F32)),
        scratch_shapes=[pltpu.VMEM((HEADS, HEAD_DIM, HEAD_DIM), F32)],
        compiler_params=_params("arbitrary"), name=name,
    )(c, z, ba, pa, pdt, hn)


def gdn_bwd(c, z, ba, pa, pdt, hn, starts, dy, name):
    t = c.shape[0]
    n_chunks = t // CHUNK

    def body(c_ref, z_ref, ba_ref, pa_ref, pdt_ref, hn_ref, ss_ref, dy_ref,
             dc_ref, dz_ref, dba_ref, dpa_ref, dpdt_ref, dhn_ref, dstate):
        step = pl.program_id(0)
        n = n_chunks - 1 - step

        @pl.when(step == 0)
        def _():
            dstate[...] = jnp.zeros_like(dstate)
            dpa_ref[...] = jnp.zeros_like(dpa_ref)
            dpdt_ref[...] = jnp.zeros_like(dpdt_ref)
            dhn_ref[...] = jnp.zeros_like(dhn_ref)

        f = functools.partial(_gdn_chunk, valid=_chunk_valid(n, t))
        _, vjp = jax.vjp(f, c_ref[...], z_ref[...], ba_ref[...], pa_ref[...], pdt_ref[...], hn_ref[...],
                         tuple(ss_ref[0, h] for h in range(HEADS)))
        dc, dz, dba, dpa, dpdt, dhn, ds = vjp((dy_ref[...], tuple(dstate[h] for h in range(HEADS))))
        dc_ref[...] = dc
        dz_ref[...] = dz.astype(BF16)
        dba_ref[...] = dba.astype(BF16)
        dpa_ref[...] += dpa
        dpdt_ref[...] += dpdt
        dhn_ref[...] += dhn
        for h in range(HEADS):
            dstate[h] = ds[h]

    def rev(width):
        return pl.BlockSpec((CHUNK, width), lambda s: (n_chunks - 1 - s, 0))

    vec = pl.BlockSpec((1, 128), lambda s: (0, 0))
    vec_shape = jax.ShapeDtypeStruct((1, 128), F32)
    return pl.pallas_call(
        body, grid=(n_chunks,),
        in_specs=[rev(QKV_DIM), rev(D_MODEL), rev(128), vec, vec, vec,
                  pl.BlockSpec((1, HEADS, HEAD_DIM, HEAD_DIM), lambda s: (n_chunks - 1 - s, 0, 0, 0)), rev(D_MODEL)],
        out_specs=(rev(QKV_DIM), rev(D_MODEL), rev(128), vec, vec, vec),
        out_shape=(jax.ShapeDtypeStruct((t, QKV_DIM), F32), jax.ShapeDtypeStruct((t, D_MODEL), BF16),
                   jax.ShapeDtypeStruct((t, 128), BF16), vec_shape, vec_shape, vec_shape),
        scratch_shapes=[pltpu.VMEM((HEADS, HEAD_DIM, HEAD_DIM), F32)],
        compiler_params=_params("arbitrary"), name=name,
    )(c, z, ba, pa, pdt, hn, starts, dy)


def _layer_fwd(h, w, tag):
    u, ut = rms_fwd(h, w["norm_mix"], f"{tag}_rms_mix")
    pq = mm(u, w["wqkv"], name=f"{tag}_mm_qkv")
    pz = mm(u, w["wz"], name=f"{tag}_mm_z")
    pg = mm(u, w["wg"], name=f"{tag}_mm_gate")
    pp = mm(u, w["wpl"], name=f"{tag}_mm_pool")
    pba = mm(u, w["wba"], name=f"{tag}_mm_ba")
    cq = conv_fwd(pq, w["conv_qkv"], f"{tag}_conv_qkv")
    ya, starts = gdn_fwd(cq, pz, pba, w["pa"], w["pdt"], w["head_norm"], f"{tag}_gdn")
    pooled = poolwin_fwd(pp, f"{tag}_poolwin")
    y, yt = mix_fwd(ya, pg, pooled, w["w_pool"], w["pool_scale"], f"{tag}_mix")
    h1 = mm(y, w["wout"], add=h, name=f"{tag}_mm_out")
    u2, u2t = rms_fwd(h1, w["norm_ffn"], f"{tag}_rms_ffn")
    hg = mm(u2, w["wupg"], name=f"{tag}_mm_upg")
    hv = mm(u2, w["wupv"], name=f"{tag}_mm_upv")
    cg = conv_fwd(hg, w["conv_g"], f"{tag}_conv_g")
    cv = conv_fwd(hv, w["conv_v"], f"{tag}_conv_v")
    act, actt = ffnact_fwd(cg, cv, f"{tag}_act")
    h2 = mm(act, w["wdown"], add=h1, name=f"{tag}_mm_down")
    saved = dict(h=h, ut=ut, pq=pq, pz=pz, pg=pg, pba=pba, cq=cq, ya=ya, starts=starts, pooled=pooled, yt=yt, h1=h1,
                 u2t=u2t, hg=hg, hv=hv, cg=cg, cv=cv, actt=actt)
    return h2, saved


def _layer_bwd(dh2, w, s, tag):
    g = {}
    dact = mm(dh2, w["wdown"], tb=True, name=f"{tag}_bmm_down_x")
    g["wdown"] = mm(s["actt"], dh2, name=f"{tag}_bmm_down_w")
    dcg, dcv = ffnact_bwd(s["cg"], s["cv"], dact, f"{tag}_act_b")
    dhg, g["conv_g"] = conv_bwd(s["hg"], dcg, w["conv_g"], f"{tag}_conv_g_b")
    dhv, g["conv_v"] = conv_bwd(s["hv"], dcv, w["conv_v"], f"{tag}_conv_v_b")
    du2 = mm(dhg, w["wupg"], tb=True, name=f"{tag}_bmm_upg_x")
    du2 = mm(dhv, w["wupv"], tb=True, add=du2, name=f"{tag}_bmm_upv_x")
    g["wupg"] = mm(s["u2t"], dhg, name=f"{tag}_bmm_upg_w")
    g["wupv"] = mm(s["u2t"], dhv, name=f"{tag}_bmm_upv_w")
    dh1, g["norm_ffn"] = rms_bwd(s["h1"], w["norm_ffn"], du2, dh2, f"{tag}_rms_ffn_b")
    dy = mm(dh1, w["wout"], tb=True, name=f"{tag}_bmm_out_x")
    g["wout"] = mm(s["yt"], dh1, name=f"{tag}_bmm_out_w")
    dya, dpg, dpooled, g["w_pool"], g["pool_scale"] = mix_bwd(
        s["ya"], s["pg"], s["pooled"], w["w_pool"], w["pool_scale"], dy, f"{tag}_mix_b")
    dpp = poolwin_bwd(dpooled, f"{tag}_poolwin_b")
    dcq, dpz, dpba, g["pa"], g["pdt"], g["head_norm"] = gdn_bwd(
        s["cq"], s["pz"], s["pba"], w["pa"], w["pdt"], w["head_norm"], s["starts"], dya, f"{tag}_gdn_b")
    dpq, g["conv_qkv"] = conv_bwd(s["pq"], dcq, w["conv_qkv"], f"{tag}_conv_qkv_b")
    du = mm(dpq, w["wqkv"], tb=True, name=f"{tag}_bmm_qkv_x")
    du = mm(dpz, w["wz"], tb=True, add=du, name=f"{tag}_bmm_z_x")
    du = mm(dpg, w["wg"], tb=True, add=du, name=f"{tag}_bmm_gate_x")
    du = mm(dpp, w["wpl"], tb=True, add=du, name=f"{tag}_bmm_pool_x")
    du = mm(dpba, w["wba"], tb=True, add=du, name=f"{tag}_bmm_ba_x")
    g["wqkv"] = mm(s["ut"], dpq, name=f"{tag}_bmm_qkv_w")
    g["wz"] = mm(s["ut"], dpz, name=f"{tag}_bmm_z_w")
    g["wg"] = mm(s["ut"], dpg, name=f"{tag}_bmm_gate_w")
    g["wpl"] = mm(s["ut"], dpp, name=f"{tag}_bmm_pool_w")
    g["wba"] = mm(s["ut"], dpba, name=f"{tag}_bmm_ba_w")
    dh, g["norm_mix"] = rms_bwd(s["h"], w["norm_mix"], du, dh1, f"{tag}_rms_mix_b")
    return dh, g


def local_step(h0, target, layers, norm_final):
    h = h0
    saved = []
    for li, w in enumerate(layers):
        h, s = _layer_fwd(h, w, f"l{li}")
        saved.append(s)
    loss, dh, dnf = loss_head(h, norm_final, target, "loss_head")
    grads = [None] * len(layers)
    for li in reversed(range(len(layers))):
        dh, grads[li] = _layer_bwd(dh, layers[li], saved[li], f"l{li}")
    return loss, dh, grads, dnf


_Z0, _B0, _P0, _G0, _IN_DIM = 3072, 4096, 4112, 4624, 6672


def _lanes_8_to_15(v):
    return jnp.pad(v.reshape(1, HEADS).astype(F32), ((0, 0), (HEADS, 128 - 2 * HEADS)))


def prep_layer(p):
    w_in = p["w_in"].astype(BF16)
    w_up = p["w_up"].astype(BF16)
    row = lambda v: v.reshape(1, -1).astype(F32)
    return dict(
        wqkv=w_in[:, :_Z0], wz=w_in[:, _Z0:_B0], wba=jnp.pad(w_in[:, _B0:_P0], ((0, 0), (0, 128 - 2 * HEADS))),
        wpl=w_in[:, _P0:_G0], wg=w_in[:, _G0:], wout=p["w_out"].astype(BF16),
        wupg=w_up[:, :D_FF], wupv=w_up[:, D_FF:], wdown=p["w_down"].astype(BF16),
        conv_qkv=p["conv_qkv"].astype(F32), conv_g=p["conv_ffn"][:, :D_FF].astype(F32),
        conv_v=p["conv_ffn"][:, D_FF:].astype(F32), w_pool=p["w_pool"].astype(F32),
        pool_scale=row(p["pool_scale"]), head_norm=row(p["head_norm"]), norm_mix=row(p["norm_mix"]),
        norm_ffn=row(p["norm_ffn"]), pa=_lanes_8_to_15(p["a_log"]), pdt=_lanes_8_to_15(p["dt_bias"]))


def layer_grads(g):
    return dict(
        w_in=jnp.concatenate([g["wqkv"], g["wz"], g["wba"][:, :2 * HEADS], g["wpl"], g["wg"]], axis=1),
        conv_qkv=g["conv_qkv"][:4], a_log=g["pa"][0, HEADS:2 * HEADS], dt_bias=g["pdt"][0, HEADS:2 * HEADS],
        head_norm=g["head_norm"][0], w_pool=g["w_pool"], pool_scale=g["pool_scale"][0], w_out=g["wout"],
        norm_mix=g["norm_mix"][0], norm_ffn=g["norm_ffn"][0],
        w_up=jnp.concatenate([g["wupg"], g["wupv"]], axis=1),
        conv_ffn=jnp.concatenate([g["conv_g"][:3], g["conv_v"][:3]], axis=1), w_down=g["wdown"])


LAYER_PARAMS = ("norm_mix", "w_in", "conv_qkv", "a_log", "dt_bias", "head_norm", "w_pool", "pool_scale", "w_out",
                "norm_ffn", "w_up", "conv_ffn", "w_down")


def pad_rows(meta, x):
    return jnp.concatenate([jnp.zeros((LEAD, D_MODEL), F32), meta.astype(F32), x.astype(F32),
                            jnp.zeros((TAIL, D_MODEL), F32)], axis=0)


MESH = pl.DeviceIdType.MESH
ANY = pl.BlockSpec(memory_space=pl.ANY)


def _place():
    x, y, c = lax.axis_index("x"), lax.axis_index("y"), lax.axis_index("c")
    return x, y, c, [(1 - x, y), (x, 1 - y), (1 - x, 1 - y)]


def _my_chip():
    return 2 * lax.axis_index("x") + lax.axis_index("y")


def gather_shards(packs):
    n = len(packs)

    def body(*refs):
        p_refs, o_refs, (send_sems, recv_sems) = refs[:n], refs[n:2 * n], refs[2 * n:]
        x, y, c, chips = _place()

        def copy(a, k, chip, half, to, src=None):
            dst = o_refs[a].at[2 * chip[0] + chip[1], half]
            return pltpu.make_async_remote_copy(src_ref=dst if src is None else src, dst_ref=dst,
                                                send_sem=send_sems.at[6 * a + k], recv_sem=recv_sems.at[6 * a + k],
                                                device_id=to, device_id_type=MESH)

        first = [copy(a, j, (x, y), c, (*chip, c), src=p_refs[a].at[c]) for a in range(n) for j, chip in enumerate(chips)]
        for cp in first:
            cp.start()
        passed = []
        for a in range(n):
            for j, chip in enumerate(chips):
                copy(a, j, chip, c, (x, y, c)).wait_recv()
                passed.append(copy(a, 3 + j, chip, c, (x, y, 1 - c)))
                passed[-1].start()
        for a in range(n):
            for j, chip in enumerate(chips):
                copy(a, 3 + j, chip, 1 - c, (x, y, c)).wait_recv()
        for cp in first + passed:
            cp.wait_send()

    gathered = pl.pallas_call(
        body, in_specs=[ANY] * n, out_specs=[ANY] * n,
        out_shape=[jax.ShapeDtypeStruct((4,) + p.shape, p.dtype) for p in packs],
        scratch_shapes=[pltpu.SemaphoreType.DMA((6 * n,)), pltpu.SemaphoreType.DMA((6 * n,))],
        name="gather_shards",
    )(*packs)
    me = _my_chip()
    return [lax.dynamic_update_slice(g, p[None], (me,) + (0,) * p.ndim) for g, p in zip(gathered, packs)]


def swap_other_halves(ps):
    n = len(ps)

    def body(*refs):
        p_refs, o_refs, (send_sems, recv_sems) = refs[:n], refs[n:2 * n], refs[2 * n:]
        x, y, c, _ = _place()
        copies = [pltpu.make_async_remote_copy(src_ref=p_refs[a].at[s, 1 - c], dst_ref=o_refs[a].at[s],
                                               send_sem=send_sems.at[4 * a + s], recv_sem=recv_sems.at[4 * a + s],
                                               device_id=(x, y, 1 - c), device_id_type=MESH)
                  for a in range(n) for s in range(4)]
        for cp in copies:
            cp.start()
        for cp in copies:
            cp.wait()

    return pl.pallas_call(
        body, in_specs=[ANY] * n, out_specs=[ANY] * n,
        out_shape=[jax.ShapeDtypeStruct((4,) + p.shape[2:], p.dtype) for p in ps],
        scratch_shapes=[pltpu.SemaphoreType.DMA((4 * n,)), pltpu.SemaphoreType.DMA((4 * n,))],
        name="swap_other_halves",
    )(*ps)


def scatter_to_chips(qs):
    n = len(qs)

    def body(*refs):
        q_refs, o_refs, (send_sems, recv_sems) = refs[:n], refs[n:2 * n], refs[2 * n:]
        x, y, c, chips = _place()
        me = 2 * x + y
        copies = [pltpu.make_async_remote_copy(src_ref=q_refs[a].at[2 * chip[0] + chip[1]], dst_ref=o_refs[a].at[me],
                                               send_sem=send_sems.at[3 * a + j], recv_sem=recv_sems.at[3 * a + j],
                                               device_id=(*chip, c), device_id_type=MESH)
                  for a in range(n) for j, chip in enumerate(chips)]
        for cp in copies:
            cp.start()
        for a in range(n):
            for j, chip in enumerate(chips):
                slot = o_refs[a].at[2 * chip[0] + chip[1]]
                pltpu.make_async_remote_copy(src_ref=slot, dst_ref=slot, send_sem=send_sems.at[3 * a + j],
                                             recv_sem=recv_sems.at[3 * a + j],
                                             device_id=(x, y, c), device_id_type=MESH).wait_recv()
        for cp in copies:
            cp.wait_send()

    received = pl.pallas_call(
        body, in_specs=[ANY] * n, out_specs=[ANY] * n,
        out_shape=[jax.ShapeDtypeStruct(q.shape, q.dtype) for q in qs],
        scratch_shapes=[pltpu.SemaphoreType.DMA((3 * n,)), pltpu.SemaphoreType.DMA((3 * n,))],
        name="scatter_to_chips",
    )(*qs)
    me = _my_chip()
    return [lax.dynamic_update_slice(r, lax.dynamic_slice_in_dim(q, me, 1, axis=0), (me, 0, 0))
            for r, q in zip(received, qs)]


def join_halves(boths):
    n = len(boths)

    def body(*refs):
        o_refs, (send_sems, recv_sems) = refs[n:2 * n], refs[2 * n:]
        x, y, c, _ = _place()
        copies = [pltpu.make_async_remote_copy(src_ref=o_refs[a].at[c], dst_ref=o_refs[a].at[c],
                                               send_sem=send_sems.at[a], recv_sem=recv_sems.at[a],
                                               device_id=(x, y, 1 - c), device_id_type=MESH) for a in range(n)]
        for cp in copies:
            cp.start()
        for a in range(n):
            other = o_refs[a].at[1 - c]
            pltpu.make_async_remote_copy(src_ref=other, dst_ref=other, send_sem=send_sems.at[a],
                                         recv_sem=recv_sems.at[a], device_id=(x, y, c), device_id_type=MESH).wait_recv()
        for cp in copies:
            cp.wait_send()

    return pl.pallas_call(
        body, in_specs=[ANY] * n, out_specs=[ANY] * n,
        out_shape=[jax.ShapeDtypeStruct(b.shape, b.dtype) for b in boths],
        input_output_aliases={a: a for a in range(n)},
        scratch_shapes=[pltpu.SemaphoreType.DMA((n,)), pltpu.SemaphoreType.DMA((n,))], name="join_halves",
    )(*boths)


def add_own_half(p, other, c, out_dtype, name):
    _, _, rows, lanes = p.shape
    tr = _tile(rows, max(16, 524288 // lanes), 16)

    def body(c_ref, p_ref, o_ref, out_ref):
        out_ref[...] = (p_ref[...] + o_ref[...]).astype(out_dtype)

    return pl.pallas_call(
        body,
        grid_spec=pltpu.PrefetchScalarGridSpec(
            num_scalar_prefetch=1, grid=(4, rows // tr),
            in_specs=[pl.BlockSpec((None, None, tr, lanes), lambda s, i, c_ref: (s, c_ref[0], i, 0)),
                      pl.BlockSpec((None, tr, lanes), lambda s, i, c_ref: (s, i, 0))],
            out_specs=pl.BlockSpec((None, tr, lanes), lambda s, i, c_ref: (s, i, 0))),
        out_shape=jax.ShapeDtypeStruct((4, rows, lanes), out_dtype),
        compiler_params=_params("parallel", "parallel"), name=name,
    )(c, p, other)


def sum_chips(b, c, name):
    _, rows, lanes = b.shape
    tr = _tile(rows, max(16, 524288 // lanes), 16)

    def body(c_ref, b_ref, out_ref):
        b0, b1, b2, b3 = (b_ref[k].astype(F32) for k in range(4))
        out_ref[...] = ((b0 + b1) + b2) + b3

    return pl.pallas_call(
        body,
        grid_spec=pltpu.PrefetchScalarGridSpec(
            num_scalar_prefetch=1, grid=(rows // tr,),
            in_specs=[pl.BlockSpec((4, tr, lanes), lambda i, c_ref: (0, i, 0))],
            out_specs=pl.BlockSpec((None, tr, lanes), lambda i, c_ref: (c_ref[0], i, 0))),
        out_shape=jax.ShapeDtypeStruct((2, rows, lanes), F32),
        compiler_params=_params("parallel"), name=name,
    )(c, b)


def all_reduce_to_shards(packs, wires, tags, c):
    others = swap_other_halves(packs)
    qs = [add_own_half(p, o, c, wire, f"add_own_half_{tag}") for p, o, wire, tag in zip(packs, others, wires, tags)]
    return join_halves([sum_chips(r, c, f"sum_chips_{tag}") for r, tag in zip(scatter_to_chips(qs), tags)])


def adamw(w, g, m, v, name):
    shape = w.shape
    cols = shape[-1]
    w2, g2, m2, v2 = (a.reshape(-1, cols) for a in (w, g, m, v))
    rows = w2.shape[0]
    tr = _tile(rows, max(8, 262144 // cols), 8) if rows % 8 == 0 else rows
    c1 = 1.0 - ADAM_B1 ** ADAM_STEP
    c2 = 1.0 - ADAM_B2 ** ADAM_STEP

    def body(w_ref, g_ref, m_ref, v_ref, d_ref, mo_ref, vo_ref):
        gv = g_ref[...]
        mn = ADAM_B1 * m_ref[...] + (1.0 - ADAM_B1) * gv
        vn = ADAM_B2 * v_ref[...] + (1.0 - ADAM_B2) * jnp.square(gv)
        d_ref[...] = -ADAM_LR * ((mn / c1) / (jnp.sqrt(vn / c2) + ADAM_EPS) + ADAM_WD * w_ref[...])
        mo_ref[...] = mn
        vo_ref[...] = vn

    spec = pl.BlockSpec((tr, cols), lambda i: (i, 0))
    out = jax.ShapeDtypeStruct((rows, cols), F32)
    d, mn, vn = pl.pallas_call(
        body, grid=(rows // tr,), in_specs=[spec] * 4, out_specs=(spec,) * 3, out_shape=(out,) * 3,
        compiler_params=_params("parallel"), name=name,
    )(w2, g2, m2, v2)
    return d.reshape(shape), mn.reshape(shape), vn.reshape(shape)


BIG = ("w_in", "w_up", "w_down", "w_out")
SMALL = ("w_pool", "conv_qkv", "conv_ffn", "meta_tokens")
SHARDED = BIG + SMALL
MATMUL_WEIGHTS = BIG + ("w_pool",)
REPLICATED = ("norm_mix", "a_log", "dt_bias", "head_norm", "pool_scale", "norm_ffn", "norm_final")
SHARD_AXIS = {"w_in": 2, "w_up": 2, "w_out": 1, "w_down": 1, "w_pool": 3, "conv_qkv": 2, "conv_ffn": 2, "meta_tokens": 1}


def _rows_of(a):
    return a.reshape(-1, 128)


SEGMENT_ROWS = 16


def _segment(n_rows):
    return -(-n_rows // SEGMENT_ROWS) * SEGMENT_ROWS


def _pad_segment(a):
    pad = [(0, 0)] * a.ndim
    pad[-2] = (0, _segment(a.shape[-2]) - a.shape[-2])
    return jnp.pad(a, pad)


def _unshard(stacked, axis):
    full = jnp.moveaxis(stacked, 0, axis)
    shape = list(full.shape)
    shape[axis:axis + 2] = [shape[axis] * shape[axis + 1]]
    return full.reshape(shape)


def _shard_stack(full, axis):
    shape = list(full.shape)
    shape[axis:axis + 1] = [4, shape[axis] // 4]
    return jnp.moveaxis(full.reshape(shape), axis, 0)


def pack_weights(shards):
    parts = [_rows_of(shards[k].astype(WIRE)) if k in MATMUL_WEIGHTS else
             lax.bitcast_convert_type(_rows_of(shards[k].astype(F32)), WIRE).reshape(-1, 128) for k in SMALL]
    parts = [_pad_segment(p) for p in parts]
    rows = sum(p.shape[0] for p in parts)
    if rows % (2 * SEGMENT_ROWS):
        parts.append(jnp.zeros((SEGMENT_ROWS, 128), WIRE))
        rows += SEGMENT_ROWS
    return [shards[k].astype(WIRE) for k in BIG] + [jnp.concatenate(parts, axis=0).reshape(2, rows // 2, 128)]


def unpack_weights(gathered, shard_shapes):
    out = {k: _unshard(g, SHARD_AXIS[k]) for k, g in zip(BIG, gathered)}
    flat = gathered[-1].reshape(4, -1, 128)
    at = 0
    for k in SMALL:
        shp = shard_shapes[k]
        n = 1
        for e in shp:
            n *= e
        if k in MATMUL_WEIGHTS:
            r = n // 128
            stacked = flat[:, at:at + r].reshape((4,) + tuple(shp))
        else:
            r = 2 * n // 128
            stacked = lax.bitcast_convert_type(flat[:, at:at + r].reshape(4, n // 128, 128, 2), F32).reshape((4,) + tuple(shp))
        out[k] = _unshard(stacked, SHARD_AXIS[k])
        at += _segment(r)
    return out


def pack_grads(full, repl):
    r = jnp.concatenate([repl[k].reshape(-1) for k in REPLICATED])
    r = jnp.pad(r, (0, -r.shape[0] % 128)).reshape(1, -1, 128)
    parts = [_shard_stack(full[k], SHARD_AXIS[k]).reshape(4, -1, 128) for k in SMALL]
    parts = [_pad_segment(p) for p in parts + [jnp.broadcast_to(r, (4,) + r.shape[1:])]]
    rows = sum(p.shape[1] for p in parts)
    if rows % (2 * SEGMENT_ROWS):
        parts.append(jnp.zeros((4, SEGMENT_ROWS, 128), F32))
        rows += SEGMENT_ROWS
    side = jnp.concatenate(parts, axis=1).reshape(4, 2, rows // 2, 128)
    return [_shard_stack(full[k], SHARD_AXIS[k]) for k in BIG] + [side]


def unpack_grads(reduced, shard_shapes, repl_shapes):
    out = dict(zip(BIG, reduced))
    side = reduced[-1].reshape(-1, 128)
    at = 0
    for k in SMALL:
        n = 1
        for e in shard_shapes[k]:
            n *= e
        out[k] = side[at:at + n // 128].reshape(shard_shapes[k])
        at += _segment(n // 128)
    r = side[at:].reshape(-1)
    at = 0
    for k in REPLICATED:
        n = 1
        for e in repl_shapes[k]:
            n *= e
        out[k] = r[at:at + n].reshape(repl_shapes[k])
        at += n
    return out


WEIGHT_ORDER = ("meta_tokens", "norm_mix", "w_in", "conv_qkv", "a_log", "dt_bias", "head_norm", "w_pool", "pool_scale",
                "w_out", "norm_ffn", "w_up", "conv_ffn", "w_down", "norm_final")


def kernel(x, meta_tokens, norm_mix, w_in, conv_qkv, a_log, dt_bias, head_norm, w_pool, pool_scale, w_out, norm_ffn, w_up, conv_ffn, w_down, norm_final, loss_target, m_meta_tokens, m_norm_mix, m_w_in, m_conv_qkv, m_a_log, m_dt_bias, m_head_norm, m_w_pool, m_pool_scale, m_w_out, m_norm_ffn, m_w_up, m_conv_ffn, m_w_down, m_norm_final, v_meta_tokens, v_norm_mix, v_w_in, v_conv_qkv, v_a_log, v_dt_bias, v_head_norm, v_w_pool, v_pool_scale, v_w_out, v_norm_ffn, v_w_up, v_conv_ffn, v_w_down, v_norm_final):
    weights = dict(meta_tokens=meta_tokens, norm_mix=norm_mix, w_in=w_in, conv_qkv=conv_qkv, a_log=a_log,
                   dt_bias=dt_bias, head_norm=head_norm, w_pool=w_pool, pool_scale=pool_scale, w_out=w_out,
                   norm_ffn=norm_ffn, w_up=w_up, conv_ffn=conv_ffn, w_down=w_down, norm_final=norm_final)
    m_in = dict(zip(WEIGHT_ORDER, (m_meta_tokens, m_norm_mix, m_w_in, m_conv_qkv, m_a_log, m_dt_bias, m_head_norm,
                                   m_w_pool, m_pool_scale, m_w_out, m_norm_ffn, m_w_up, m_conv_ffn, m_w_down, m_norm_final)))
    v_in = dict(zip(WEIGHT_ORDER, (v_meta_tokens, v_norm_mix, v_w_in, v_conv_qkv, v_a_log, v_dt_bias, v_head_norm,
                                   v_w_pool, v_pool_scale, v_w_out, v_norm_ffn, v_w_up, v_conv_ffn, v_w_down, v_norm_final)))
    shard_shapes = {k: weights[k].shape for k in SHARDED}
    repl_shapes = {k: weights[k].shape for k in REPLICATED}
    core = lax.axis_index("c").astype(jnp.int32).reshape(1)

    full = unpack_weights(gather_shards(pack_weights({k: weights[k] for k in SHARDED})), shard_shapes)
    layers = []
    for li in range(DEPTH):
        p = {k: (full[k][li] if k in full else weights[k][li]) for k in LAYER_PARAMS}
        layers.append(prep_layer(p))

    h0 = pad_rows(full["meta_tokens"], x[0])
    target = pad_rows(jnp.zeros((N_META, D_MODEL), F32), loss_target[0])
    loss, dh0, grads, d_norm_final = local_step(h0, target, layers, norm_final.reshape(1, D_MODEL))
    seq = x.shape[1]
    grad_x = dh0[LEAD + N_META:LEAD + N_META + seq][None]

    per_layer = [layer_grads(g) for g in grads]
    g_all = {k: jnp.stack([pl_[k] for pl_ in per_layer]) for k in LAYER_PARAMS}
    g_all["meta_tokens"] = dh0[LEAD:LEAD + N_META]
    g_all["norm_final"] = d_norm_final[0]
    packs = pack_grads({k: g_all[k] for k in SHARDED}, {k: g_all[k] for k in REPLICATED})
    reduced = all_reduce_to_shards(packs, [GRAD_WIRE] * len(BIG) + [F32], BIG + ("side",), core)
    g_mine = unpack_grads(reduced, shard_shapes, repl_shapes)

    loss_sum = lax.psum(loss[0, 0], ("x", "y", "c"))
    deltas, new_m, new_v = {}, {}, {}
    for k in WEIGHT_ORDER:
        deltas[k], new_m[k], new_v[k] = adamw(weights[k], g_mine[k], m_in[k], v_in[k], f"adamw_{k}")
    return (loss_sum, grad_x, *[g_mine[k] for k in WEIGHT_ORDER], *[deltas[k] for k in WEIGHT_ORDER],
            *[new_m[k] for k in WEIGHT_ORDER], *[new_v[k] for k in WEIGHT_ORDER])
```

```python
import functools

import jax
import jax.numpy as jnp
from jax import lax
from jax.experimental import pallas as pl
from jax.experimental.pallas import tpu as pltpu

F32 = jnp.float32
BF16 = jnp.bfloat16
WIRE = jnp.bfloat16
GRAD_WIRE = jnp.bfloat16

D_MODEL = 1024
HEADS = 8
HEAD_DIM = 128
CHUNK = 64
N_META = 16
LEAD = 48
TAIL = 64
QKV_DIM = 3072
D_FF = 2816
POOL_WIDTH = 512
POOL_WINDOWS = (2, 4, 8, 16)
DEPTH = 2
NORM_EPS = 1e-6
ADAM_LR, ADAM_B1, ADAM_B2, ADAM_EPS, ADAM_WD, ADAM_STEP = 0.001, 0.9, 0.999, 1e-08, 0.01, 10
VMEM_LIMIT_BYTES = 48 * 1024 * 1024


def _params(*sem):
    return pltpu.CompilerParams(dimension_semantics=sem if sem else None, vmem_limit_bytes=VMEM_LIMIT_BYTES)


def _tile(n, cap, mult):
    best = None
    for t in range(mult, min(n, cap) + 1, mult):
        if n % t == 0:
            best = t
    assert best is not None, (n, cap, mult)
    return best


def _silu(x):
    return x * jax.nn.sigmoid(x)


def _softplus(x):
    return jnp.maximum(x, 0.0) + jnp.log(1.0 + jnp.exp(-jnp.abs(x)))


def _split_bf16(a):
    hi = a.astype(BF16)
    return hi, (a - hi.astype(F32)).astype(BF16)


def _dg(a, b, ca, cb, hi):
    dims = (((ca,), (cb,)), ((), ()))
    if hi is True:
        return lax.dot_general(a, b, dims, precision=lax.Precision.HIGHEST, preferred_element_type=F32)
    if hi == 3:
        (ah, al), (bh, bl) = _split_bf16(a), _split_bf16(b)
        dot = lambda x, y: lax.dot_general(x, y, dims, preferred_element_type=F32)
        return dot(ah, bh) + (dot(ah, bl) + dot(al, bh))
    return lax.dot_general(a.astype(BF16), b.astype(BF16), dims, preferred_element_type=F32)


def _make_dots(hi):
    @jax.custom_vjp
    def nn(a, b):
        return _dg(a, b, 1, 0, hi)

    @jax.custom_vjp
    def nt(a, b):
        return _dg(a, b, 1, 1, hi)

    @jax.custom_vjp
    def tn(a, b):
        return _dg(a, b, 0, 0, hi)

    nn.defvjp(lambda a, b: (nn(a, b), (a, b)), lambda r, g: (nt(g, r[1]), tn(r[0], g)))
    nt.defvjp(lambda a, b: (nt(a, b), (a, b)), lambda r, g: (nn(g, r[1]), tn(g, r[0])))
    tn.defvjp(lambda a, b: (tn(a, b), (a, b)), lambda r, g: (nt(r[1], g), nn(r[0], g)))
    return nn, nt, tn


_nn, _nt, _tn = _make_dots(False)
_hnn, _hnt, _htn = _make_dots(True)
_nn3, _nt3, _tn3 = _make_dots(3)


def mm(a, b, *, tb=False, add=None, out_dtype=F32, name):
    m, kdim = a.shape
    (n, kb) = b.shape if tb else b.shape[::-1]
    assert kdim == kb, (a.shape, b.shape, tb)
    tm = _tile(m, 1408, 128) if m % 128 == 0 and m <= 4096 else _tile(m, 640, 64)
    tn = _tile(n, 1536, 128)
    tk = kdim if kdim <= 3072 else _tile(kdim, 1664 if a.dtype == b.dtype == BF16 else 640, 128)
    nk = kdim // tk
    dims = (((1,), (1 if tb else 0,)), ((), ()))

    def body(*refs):
        if add is not None:
            a_ref, b_ref, add_ref, o_ref, acc = refs
        else:
            a_ref, b_ref, o_ref, acc = refs
        k = pl.program_id(2)
        part = lax.dot_general(a_ref[...].astype(BF16), b_ref[...].astype(BF16), dims, preferred_element_type=F32)

        def finish(r):
            if add is not None:
                r = r + add_ref[...]
            o_ref[...] = r.astype(out_dtype)

        if nk == 1:
            finish(part)
        else:
            @pl.when(k == 0)
            def _():
                acc[...] = part

            @pl.when(jnp.logical_and(k > 0, k < nk - 1))
            def _():
                acc[...] += part

            @pl.when(k == nk - 1)
            def _():
                finish(acc[...] + part)

    a_spec = pl.BlockSpec((tm, tk), lambda j, i, k: (i, k))
    b_spec = pl.BlockSpec((tn, tk), lambda j, i, k: (j, k)) if tb else pl.BlockSpec((tk, tn), lambda j, i, k: (k, j))
    in_specs = [a_spec, b_spec]
    args = [a, b]
    if add is not None:
        in_specs.append(pl.BlockSpec((tm, tn), lambda j, i, k: (i, j)))
        args.append(add)
    return pl.pallas_call(
        body, grid=(n // tn, m // tm, nk), in_specs=in_specs,
        out_specs=pl.BlockSpec((tm, tn), lambda j, i, k: (i, j)),
        out_shape=jax.ShapeDtypeStruct((m, n), out_dtype),
        scratch_shapes=[pltpu.VMEM((tm, tn) if nk > 1 else (8, 128), F32)],
        compiler_params=_params("parallel", "parallel", "arbitrary"), name=name,
    )(*args)


def _rms(x, gain):
    return x * lax.rsqrt(jnp.mean(x * x, axis=-1, keepdims=True) + NORM_EPS) * gain


def rms_fwd(h, gain, name):
    t = h.shape[0]
    ts = _tile(t, 640, 128)

    def body(h_ref, g_ref, u_ref, ut_ref):
        u = _rms(h_ref[...], g_ref[...])
        u_ref[...] = u.astype(BF16)
        ut_ref[...] = u.T.astype(BF16)

    return pl.pallas_call(
        body, grid=(t // ts,),
        in_specs=[pl.BlockSpec((ts, D_MODEL), lambda i: (i, 0)), pl.BlockSpec((1, D_MODEL), lambda i: (0, 0))],
        out_specs=(pl.BlockSpec((ts, D_MODEL), lambda i: (i, 0)), pl.BlockSpec((D_MODEL, ts), lambda i: (0, i))),
        out_shape=(jax.ShapeDtypeStruct((t, D_MODEL), BF16), jax.ShapeDtypeStruct((D_MODEL, t), BF16)),
        compiler_params=_params("parallel"), name=name,
    )(h, gain)


def rms_bwd(h, gain, du, dres, name):
    t = h.shape[0]
    ts = _tile(t, 640, 64)

    def body(h_ref, g_ref, du_ref, dres_ref, dh_ref, dg_ref):
        i = pl.program_id(0)
        _, vjp = jax.vjp(_rms, h_ref[...], g_ref[...])
        dx, dg = vjp(du_ref[...])
        row = i * ts + lax.broadcasted_iota(jnp.int32, (ts, 1), 0)
        dh_ref[...] = jnp.where(row >= LEAD, dx + dres_ref[...], 0.0)

        @pl.when(i == 0)
        def _():
            dg_ref[...] = jnp.zeros_like(dg_ref)

        dg_ref[...] += dg

    row_spec = pl.BlockSpec((ts, D_MODEL), lambda i: (i, 0))
    vec_spec = pl.BlockSpec((1, D_MODEL), lambda i: (0, 0))
    return pl.pallas_call(
        body, grid=(t // ts,), in_specs=[row_spec, vec_spec, row_spec, row_spec],
        out_specs=(row_spec, vec_spec),
        out_shape=(jax.ShapeDtypeStruct((t, D_MODEL), F32), jax.ShapeDtypeStruct((1, D_MODEL), F32)),
        compiler_params=_params("arbitrary"), name=name,
    )(h, gain, du, dres)


def loss_head(h, gain, target, name):
    t = h.shape[0]
    ts = _tile(t, 640, 64)

    def body(h_ref, g_ref, t_ref, loss_ref, dh_ref, dg_ref):
        i = pl.program_id(0)
        row = i * ts + lax.broadcasted_iota(jnp.int32, (ts, 1), 0)
        keep = jnp.logical_and(row >= LEAD + N_META, row < t - TAIL)
        tgt = t_ref[...]

        def f(x, g):
            err = jnp.where(keep, _rms(x, g) - tgt, 0.0)
            per_row = jnp.mean(err * err, axis=-1, keepdims=True)
            return 0.5 * jnp.sum(per_row, axis=0, keepdims=True)

        val, vjp = jax.vjp(f, h_ref[...], g_ref[...])
        dx, dg = vjp(jnp.ones((1, 1), F32))
        dh_ref[...] = dx

        @pl.when(i == 0)
        def _():
            dg_ref[...] = jnp.zeros_like(dg_ref)
            loss_ref[...] = jnp.zeros_like(loss_ref)

        dg_ref[...] += dg
        loss_ref[...] += jnp.broadcast_to(val, (1, 128))

    row_spec = pl.BlockSpec((ts, D_MODEL), lambda i: (i, 0))
    vec_spec = pl.BlockSpec((1, D_MODEL), lambda i: (0, 0))
    return pl.pallas_call(
        body, grid=(t // ts,), in_specs=[row_spec, vec_spec, row_spec],
        out_specs=(pl.BlockSpec((1, 128), lambda i: (0, 0)), row_spec, vec_spec),
        out_shape=(jax.ShapeDtypeStruct((1, 128), F32), jax.ShapeDtypeStruct((t, D_MODEL), F32),
                   jax.ShapeDtypeStruct((1, D_MODEL), F32)),
        compiler_params=_params("arbitrary"), name=name,
    )(h, gain, target)


def conv_fwd(x, w, name):
    t, width = x.shape
    k = w.shape[0]
    ts = _tile(t, 640, 64)
    tw = _tile(width, 1536, 128)
    hb = ts // 8

    def body(x_ref, halo_ref, w_ref, o_ref, buf):
        i = pl.program_id(0)
        buf[0:8, :] = jnp.where(i > 0, halo_ref[...], 0.0)
        buf[8:, :] = x_ref[...]
        wv = w_ref[...]
        acc = buf[pl.ds(8 - (k - 1), ts), :] * wv[0:1, :]
        for j in range(1, k):
            acc = acc + buf[pl.ds(8 - (k - 1) + j, ts), :] * wv[j:j + 1, :]
        o_ref[...] = acc

    return pl.pallas_call(
        body, grid=(t // ts, width // tw),
        in_specs=[pl.BlockSpec((ts, tw), lambda i, j: (i, j)),
                  pl.BlockSpec((8, tw), lambda i, j: (jnp.maximum(i * hb - 1, 0), j)),
                  pl.BlockSpec((k, tw), lambda i, j: (0, j))],
        out_specs=pl.BlockSpec((ts, tw), lambda i, j: (i, j)),
        out_shape=jax.ShapeDtypeStruct((t, width), F32),
        scratch_shapes=[pltpu.VMEM((ts + 8, tw), F32)],
        compiler_params=_params("parallel", "parallel"), name=name,
    )(x, x, w)


def conv_bwd(x, dc, w, name):
    t, width = x.shape
    k = w.shape[0]
    ts = _tile(t, 640, 64)
    tw = _tile(width, 1536, 128)
    hb = ts // 8
    nt = t // ts

    def body(x_ref, xh_ref, dc_ref, dch_ref, w_ref, dx_ref, dw_ref, xbuf, dbuf):
        i = pl.program_id(1)
        xbuf[0:8, :] = jnp.where(i > 0, xh_ref[...], 0.0)
        xbuf[8:, :] = x_ref[...]
        d = dc_ref[...]
        dbuf[0:ts, :] = d
        dbuf[ts:, :] = jnp.where(i < nt - 1, dch_ref[...], 0.0)
        wv = w_ref[...]
        acc = dbuf[pl.ds(k - 1, ts), :] * wv[0:1, :]
        for j in range(1, k):
            acc = acc + dbuf[pl.ds(k - 1 - j, ts), :] * wv[j:j + 1, :]
        dx_ref[...] = acc.astype(BF16)

        @pl.when(i == 0)
        def _():
            dw_ref[...] = jnp.zeros_like(dw_ref)

        for j in range(k):
            dw_ref[j:j + 1, :] += jnp.sum(d * xbuf[pl.ds(8 - (k - 1) + j, ts), :], axis=0, keepdims=True)

    return pl.pallas_call(
        body, grid=(width // tw, nt),
        in_specs=[pl.BlockSpec((ts, tw), lambda j, i: (i, j)),
                  pl.BlockSpec((8, tw), lambda j, i: (jnp.maximum(i * hb - 1, 0), j)),
                  pl.BlockSpec((ts, tw), lambda j, i: (i, j)),
                  pl.BlockSpec((8, tw), lambda j, i: (jnp.minimum((i + 1) * hb, t // 8 - 1), j)),
                  pl.BlockSpec((k, tw), lambda j, i: (0, j))],
        out_specs=(pl.BlockSpec((ts, tw), lambda j, i: (i, j)), pl.BlockSpec((8, tw), lambda j, i: (0, j))),
        out_shape=(jax.ShapeDtypeStruct((t, width), BF16), jax.ShapeDtypeStruct((8, width), F32)),
        scratch_shapes=[pltpu.VMEM((ts + 8, tw), F32), pltpu.VMEM((ts + 8, tw), F32)],
        compiler_params=_params("parallel", "arbitrary"), name=name,
    )(x, x, dc, dc, w)


def _pool_count(pos, win):
    return jnp.clip(pos + 1, 1, win).astype(F32)


def poolwin_fwd(p, name):
    t = p.shape[0]
    ts = _tile(t, 640, 64)
    hb = ts // 16

    def body(p_ref, halo_ref, o_ref, buf):
        i = pl.program_id(0)
        buf[0:16, :] = jnp.where(i > 0, halo_ref[...], 0.0)
        buf[16:, :] = p_ref[...]
        pos = i * ts + lax.broadcasted_iota(jnp.int32, (ts, 1), 0) - LEAD
        for gi, win in enumerate(POOL_WINDOWS):
            cols = slice(gi * 128, (gi + 1) * 128)
            own = buf[pl.ds(16, ts), cols]
            acc = own
            for j in range(1, win):
                acc = acc + buf[pl.ds(16 - j, ts), cols]
            o_ref[:, cols] = acc / _pool_count(pos, win) - own

    return pl.pallas_call(
        body, grid=(t // ts,),
        in_specs=[pl.BlockSpec((ts, POOL_WIDTH), lambda i: (i, 0)),
                  pl.BlockSpec((16, POOL_WIDTH), lambda i: (jnp.maximum(i * hb - 1, 0), 0))],
        out_specs=pl.BlockSpec((ts, POOL_WIDTH), lambda i: (i, 0)),
        out_shape=jax.ShapeDtypeStruct((t, POOL_WIDTH), F32),
        scratch_shapes=[pltpu.VMEM((ts + 16, POOL_WIDTH), F32)],
        compiler_params=_params("parallel"), name=name,
    )(p, p)


def poolwin_bwd(dpooled, name):
    t = dpooled.shape[0]
    ts = _tile(t, 640, 64)
    hb = ts // 16
    nt = t // ts

    def body(d_ref, halo_ref, o_ref, buf):
        i = pl.program_id(0)
        buf[0:ts, :] = d_ref[...]
        buf[ts:, :] = jnp.where(i < nt - 1, halo_ref[...], 0.0)
        pos = i * ts + lax.broadcasted_iota(jnp.int32, (ts, 1), 0) - LEAD
        for gi, win in enumerate(POOL_WINDOWS):
            cols = slice(gi * 128, (gi + 1) * 128)
            own = buf[pl.ds(0, ts), cols]
            acc = own / _pool_count(pos, win)
            for j in range(1, win):
                acc = acc + buf[pl.ds(j, ts), cols] / _pool_count(pos + j, win)
            o_ref[:, cols] = (acc - own).astype(BF16)

    return pl.pallas_call(
        body, grid=(nt,),
        in_specs=[pl.BlockSpec((ts, POOL_WIDTH), lambda i: (i, 0)),
                  pl.BlockSpec((16, POOL_WIDTH), lambda i: (jnp.minimum((i + 1) * hb, t // 16 - 1), 0))],
        out_specs=pl.BlockSpec((ts, POOL_WIDTH), lambda i: (i, 0)),
        out_shape=jax.ShapeDtypeStruct((t, POOL_WIDTH), BF16),
        scratch_shapes=[pltpu.VMEM((ts + 16, POOL_WIDTH), F32)],
        compiler_params=_params("parallel"), name=name,
    )(dpooled, dpooled)


def _mix(y_a, gpre, pooled, w_pool, scale):
    parts = [_nn(pooled[:, g * 128:(g + 1) * 128], w_pool[g]) for g in range(4)]
    y_b = jnp.concatenate(parts, axis=1) * scale
    return jax.nn.sigmoid(gpre[:, :D_MODEL]) * y_a + jax.nn.sigmoid(gpre[:, D_MODEL:]) * y_b


def _mix_specs(ts):
    return [pl.BlockSpec((ts, D_MODEL), lambda i: (i, 0)), pl.BlockSpec((ts, 2 * D_MODEL), lambda i: (i, 0)),
            pl.BlockSpec((ts, POOL_WIDTH), lambda i: (i, 0)), pl.BlockSpec((4, 128, 256), lambda i: (0, 0, 0)),
            pl.BlockSpec((1, D_MODEL), lambda i: (0, 0))]


def mix_fwd(y_a, gpre, pooled, w_pool, scale, name):
    t = y_a.shape[0]
    ts = _tile(t, 640, 128)

    def body(ya_ref, g_ref, p_ref, w_ref, s_ref, o_ref, ot_ref):
        y = _mix(ya_ref[...], g_ref[...], p_ref[...], w_ref[...], s_ref[...])
        o_ref[...] = y.astype(BF16)
        ot_ref[...] = y.T.astype(BF16)

    return pl.pallas_call(
        body, grid=(t // ts,), in_specs=_mix_specs(ts),
        out_specs=(pl.BlockSpec((ts, D_MODEL), lambda i: (i, 0)), pl.BlockSpec((D_MODEL, ts), lambda i: (0, i))),
        out_shape=(jax.ShapeDtypeStruct((t, D_MODEL), BF16), jax.ShapeDtypeStruct((D_MODEL, t), BF16)),
        compiler_params=_params("parallel"), name=name,
    )(y_a, gpre, pooled, w_pool, scale)


def mix_bwd(y_a, gpre, pooled, w_pool, scale, dy, name):
    t = y_a.shape[0]
    ts = _tile(t, 320, 64)

    def body(ya_ref, g_ref, p_ref, w_ref, s_ref, dy_ref, dya_ref, dg_ref, dp_ref, dw_ref, ds_ref):
        i = pl.program_id(0)
        _, vjp = jax.vjp(_mix, ya_ref[...], g_ref[...], p_ref[...], w_ref[...], s_ref[...])
        dya, dg, dp, dw, ds = vjp(dy_ref[...])
        dya_ref[...] = dya
        dg_ref[...] = dg.astype(BF16)
        dp_ref[...] = dp

        @pl.when(i == 0)
        def _():
            dw_ref[...] = jnp.zeros_like(dw_ref)
            ds_ref[...] = jnp.zeros_like(ds_ref)

        dw_ref[...] += dw
        ds_ref[...] += ds

    specs = _mix_specs(ts)
    return pl.pallas_call(
        body, grid=(t // ts,), in_specs=specs + [specs[0]],
        out_specs=(specs[0], specs[1], specs[2], specs[3], specs[4]),
        out_shape=(jax.ShapeDtypeStruct((t, D_MODEL), F32), jax.ShapeDtypeStruct((t, 2 * D_MODEL), BF16),
                   jax.ShapeDtypeStruct((t, POOL_WIDTH), F32), jax.ShapeDtypeStruct((4, 128, 256), F32),
                   jax.ShapeDtypeStruct((1, D_MODEL), F32)),
        compiler_params=_params("arbitrary"), name=name,
    )(y_a, gpre, pooled, w_pool, scale, dy)


def _ffn_act(cg, cv):
    return _silu(cg) * cv


def ffnact_fwd(cg, cv, name):
    t, width = cg.shape
    ts = _tile(t, 640, 128)
    tw = _tile(width, 1536, 128)
    spec = pl.BlockSpec((ts, tw), lambda i, j: (i, j))

    def body(g_ref, v_ref, o_ref, ot_ref):
        act = _ffn_act(g_ref[...], v_ref[...])
        o_ref[...] = act.astype(BF16)
        ot_ref[...] = act.T.astype(BF16)

    return pl.pallas_call(
        body, grid=(t // ts, width // tw), in_specs=[spec, spec],
        out_specs=(spec, pl.BlockSpec((tw, ts), lambda i, j: (j, i))),
        out_shape=(jax.ShapeDtypeStruct((t, width), BF16), jax.ShapeDtypeStruct((width, t), BF16)),
        compiler_params=_params("parallel", "parallel"), name=name,
    )(cg, cv)


def ffnact_bwd(cg, cv, dact, name):
    t, width = cg.shape
    ts = _tile(t, 640, 64)
    tw = _tile(width, 1536, 128)
    spec = pl.BlockSpec((ts, tw), lambda i, j: (i, j))

    def body(g_ref, v_ref, d_ref, dg_ref, dv_ref):
        _, vjp = jax.vjp(_ffn_act, g_ref[...], v_ref[...])
        dg_ref[...], dv_ref[...] = vjp(d_ref[...])

    return pl.pallas_call(
        body, grid=(t // ts, width // tw), in_specs=[spec, spec, spec], out_specs=(spec, spec),
        out_shape=(jax.ShapeDtypeStruct((t, width), F32), jax.ShapeDtypeStruct((t, width), F32)),
        compiler_params=_params("parallel", "parallel"), name=name,
    )(cg, cv, dact)


def _gdn_chunk(c, z, ba, pa, pdt, hn, s, *, valid):
    r = lax.broadcasted_iota(jnp.int32, (CHUNK, CHUNK), 0)
    q_ = lax.broadcasted_iota(jnp.int32, (CHUNK, CHUNK), 1)
    causal = r >= q_
    strict = r > q_
    tril = causal.astype(F32)
    triu = (r <= q_).astype(F32)
    eye = (r == q_).astype(F32)
    lane = lax.broadcasted_iota(jnp.int32, (CHUNK, 128), 1)

    decay_log = -jnp.exp(pa) * _softplus(ba + pdt)
    bg = jnp.where(lane < HEADS, jax.nn.sigmoid(ba), jnp.where(lane < 2 * HEADS, decay_log, 0.0))
    bg = jnp.where(valid, bg, 0.0)
    gc = _hnn(tril, bg)
    gct = _hnn(bg.T, triu)
    eg = jnp.exp(gc)
    glast = gc[CHUNK - 1:CHUNK, :]
    ekd = jnp.exp(glast - gc)
    gtot = jnp.exp(glast)

    hd = range(HEADS)
    hs = [slice(h * HEAD_DIM, (h + 1) * HEAD_DIM) for h in hd]
    gl = [slice(HEADS + h, HEADS + h + 1) for h in hd]
    q = [_silu(c[:, hs[h]]) for h in hd]
    k = [_silu(c[:, D_MODEL + h * HEAD_DIM:D_MODEL + (h + 1) * HEAD_DIM]) for h in hd]
    v = [_silu(c[:, 2 * D_MODEL + h * HEAD_DIM:2 * D_MODEL + (h + 1) * HEAD_DIM]) for h in hd]
    q = [q[h] * lax.rsqrt(jnp.sum(q[h] * q[h], axis=-1, keepdims=True) + NORM_EPS) * (HEAD_DIM ** -0.5) for h in hd]
    k = [k[h] * lax.rsqrt(jnp.sum(k[h] * k[h], axis=-1, keepdims=True) + NORM_EPS) for h in hd]
    beta = [bg[:, h:h + 1] for h in hd]
    decay = [jnp.exp(jnp.where(causal, gc[:, gl[h]] - gct[gl[h], :], -1e30)) for h in hd]
    kb = [k[h] * beta[h] for h in hd]
    a = [jnp.where(strict, _nt(kb[h], k[h]) * decay[h], 0.0) for h in hd]
    qk = [jnp.where(causal, _nt(q[h], k[h]) * decay[h], 0.0) for h in hd]
    p = [_nn3(a[h], a[h]) for h in hd]
    x = [(eye - a[h]) + p[h] - _nn(a[h], p[h]) for h in hd]
    for _ in range(4):
        p = [_nn(p[h], p[h]) for h in hd]
        x = [x[h] + p[h] + _nn(x[h] - eye, p[h]) for h in hd]
    u = [_nn(x[h], v[h] * beta[h]) for h in hd]
    w = [_nn(x[h], kb[h] * eg[:, gl[h]]) for h in hd]
    v_new = [u[h] - _nn(w[h], s[h]) for h in hd]
    o = [_nn(q[h] * eg[:, gl[h]], s[h]) + _nn(qk[h], v_new[h]) for h in hd]
    states = [s[h] * gtot[:, gl[h]] + _tn(k[h] * ekd[:, gl[h]], v_new[h]) for h in hd]
    o = [o[h] * lax.rsqrt(jnp.mean(o[h] * o[h], axis=-1, keepdims=True) + NORM_EPS) * hn * _silu(z[:, hs[h]])
         for h in hd]
    return jnp.concatenate(o, axis=1), tuple(states)


GDN_FWD_CHUNKS = 5
GDN_BWD_CHUNKS = 2


def _chunk_valid(n, t):
    row = n * CHUNK + lax.broadcasted_iota(jnp.int32, (CHUNK, 1), 0)
    return jnp.logical_and(row >= LEAD, row < t - TAIL)


def gdn_fwd(c, z, ba, pa, pdt, hn, name):
    t = c.shape[0]
    n_chunks = t // CHUNK
    per_step = GDN_FWD_CHUNKS if n_chunks % GDN_FWD_CHUNKS == 0 else 1
    rows_per_step = per_step * CHUNK

    def body(c_ref, z_ref, ba_ref, pa_ref, pdt_ref, hn_ref, y_ref, ss_ref, state):
        step = pl.program_id(0)

        @pl.when(step == 0)
        def _():
            state[...] = jnp.zeros_like(state)

        s = tuple(state[h] for h in range(HEADS))
        for j in range(per_step):
            rows = pl.ds(j * CHUNK, CHUNK)
            for h in range(HEADS):
                ss_ref[j, h] = s[h]
            y, s = _gdn_chunk(c_ref[rows, :], z_ref[rows, :], ba_ref[rows, :], pa_ref[...], pdt_ref[...], hn_ref[...], s,
                              valid=_chunk_valid(step * per_step + j, t))
            y_ref[rows, :] = y
        for h in range(HEADS):
            state[h] = s[h]

    vec = pl.BlockSpec((1, 128), lambda n: (0, 0))
    return pl.pallas_call(
        body, grid=(n_chunks // per_step,),
        in_specs=[pl.BlockSpec((rows_per_step, QKV_DIM), lambda n: (n, 0)),
                  pl.BlockSpec((rows_per_step, D_MODEL), lambda n: (n, 0)),
                  pl.BlockSpec((rows_per_step, 128), lambda n: (n, 0)), vec, vec, vec],
        out_specs=(pl.BlockSpec((rows_per_step, D_MODEL), lambda n: (n, 0)),
                   pl.BlockSpec((per_step, HEADS, HEAD_DIM, HEAD_DIM), lambda n: (n, 0, 0, 0))),
        out_shape=(jax.ShapeDtypeStruct((t, D_MODEL), F32),
                   jax.ShapeDtypeStruct((n_chunks, HEADS, HEAD_DIM, HEAD_DIM), F32)),
        scratch_shapes=[pltpu.VMEM((HEADS, HEAD_DIM, HEAD_DIM), F32)],
        compiler_params=_params("arbitrary"), name=name,
    )(c, z, ba, pa, pdt, hn)


def gdn_bwd(c, z, ba, pa, pdt, hn, starts, dy, name):
    t = c.shape[0]
    per_step = GDN_BWD_CHUNKS if (t // CHUNK) % GDN_BWD_CHUNKS == 0 else 1
    n_steps = t // CHUNK // per_step
    rows_per_step = per_step * CHUNK

    def body(c_ref, z_ref, ba_ref, pa_ref, pdt_ref, hn_ref, ss_ref, dy_ref,
             dc_ref, dz_ref, dba_ref, dpa_ref, dpdt_ref, dhn_ref, dstate):
        step = pl.program_id(0)

        @pl.when(step == 0)
        def _():
            dstate[...] = jnp.zeros_like(dstate)
            dpa_ref[...] = jnp.zeros_like(dpa_ref)
            dpdt_ref[...] = jnp.zeros_like(dpdt_ref)
            dhn_ref[...] = jnp.zeros_like(dhn_ref)

        ds = tuple(dstate[h] for h in range(HEADS))
        for j in reversed(range(per_step)):
            rows = pl.ds(j * CHUNK, CHUNK)
            f = functools.partial(_gdn_chunk, valid=_chunk_valid((n_steps - 1 - step) * per_step + j, t))
            _, vjp = jax.vjp(f, c_ref[rows, :], z_ref[rows, :], ba_ref[rows, :], pa_ref[...], pdt_ref[...], hn_ref[...],
                             tuple(ss_ref[j, h] for h in range(HEADS)))
            dc, dz, dba, dpa, dpdt, dhn, ds = vjp((dy_ref[rows, :], ds))
            dc_ref[rows, :] = dc
            dz_ref[rows, :] = dz.astype(BF16)
            dba_ref[rows, :] = dba.astype(BF16)
            dpa_ref[...] += dpa
            dpdt_ref[...] += dpdt
            dhn_ref[...] += dhn
        for h in range(HEADS):
            dstate[h] = ds[h]

    def rev(width):
        return pl.BlockSpec((rows_per_step, width), lambda s: (n_steps - 1 - s, 0))

    vec = pl.BlockSpec((1, 128), lambda s: (0, 0))
    vec_shape = jax.ShapeDtypeStruct((1, 128), F32)
    return pl.pallas_call(
        body, grid=(n_steps,),
        in_specs=[rev(QKV_DIM), rev(D_MODEL), rev(128), vec, vec, vec,
                  pl.BlockSpec((per_step, HEADS, HEAD_DIM, HEAD_DIM), lambda s: (n_steps - 1 - s, 0, 0, 0)),
                  rev(D_MODEL)],
        out_specs=(rev(QKV_DIM), rev(D_MODEL), rev(128), vec, vec, vec),
        out_shape=(jax.ShapeDtypeStruct((t, QKV_DIM), F32), jax.ShapeDtypeStruct((t, D_MODEL), BF16),
                   jax.ShapeDtypeStruct((t, 128), BF16), vec_shape, vec_shape, vec_shape),
        scratch_shapes=[pltpu.VMEM((HEADS, HEAD_DIM, HEAD_DIM), F32)],
        compiler_params=_params("arbitrary"), name=name,
    )(c, z, ba, pa, pdt, hn, starts, dy)


def _layer_fwd(h, w, tag):
    u, ut = rms_fwd(h, w["norm_mix"], f"{tag}_rms_mix")
    pq = mm(u, w["wqkv"], name=f"{tag}_mm_qkv")
    pz = mm(u, w["wz"], name=f"{tag}_mm_z")
    pg = mm(u, w["wg"], name=f"{tag}_mm_gate")
    pp = mm(u, w["wpl"], name=f"{tag}_mm_pool")
    pba = mm(u, w["wba"], name=f"{tag}_mm_ba")
    cq = conv_fwd(pq, w["conv_qkv"], f"{tag}_conv_qkv")
    ya, starts = gdn_fwd(cq, pz, pba, w["pa"], w["pdt"], w["head_norm"], f"{tag}_gdn")
    pooled = poolwin_fwd(pp, f"{tag}_poolwin")
    y, yt = mix_fwd(ya, pg, pooled, w["w_pool"], w["pool_scale"], f"{tag}_mix")
    h1 = mm(y, w["wout"], add=h, name=f"{tag}_mm_out")
    u2, u2t = rms_fwd(h1, w["norm_ffn"], f"{tag}_rms_ffn")
    hg = mm(u2, w["wupg"], name=f"{tag}_mm_upg")
    hv = mm(u2, w["wupv"], name=f"{tag}_mm_upv")
    cg = conv_fwd(hg, w["conv_g"], f"{tag}_conv_g")
    cv = conv_fwd(hv, w["conv_v"], f"{tag}_conv_v")
    act, actt = ffnact_fwd(cg, cv, f"{tag}_act")
    h2 = mm(act, w["wdown"], add=h1, name=f"{tag}_mm_down")
    saved = dict(h=h, ut=ut, pq=pq, pz=pz, pg=pg, pba=pba, cq=cq, ya=ya, starts=starts, pooled=pooled, yt=yt, h1=h1,
                 u2t=u2t, hg=hg, hv=hv, cg=cg, cv=cv, actt=actt)
    return h2, saved


def _layer_bwd(dh2, w, s, tag):
    g = {}
    dact = mm(dh2, w["wdown"], tb=True, name=f"{tag}_bmm_down_x")
    g["wdown"] = mm(s["actt"], dh2, name=f"{tag}_bmm_down_w")
    dcg, dcv = ffnact_bwd(s["cg"], s["cv"], dact, f"{tag}_act_b")
    dhg, g["conv_g"] = conv_bwd(s["hg"], dcg, w["conv_g"], f"{tag}_conv_g_b")
    dhv, g["conv_v"] = conv_bwd(s["hv"], dcv, w["conv_v"], f"{tag}_conv_v_b")
    du2 = mm(dhg, w["wupg"], tb=True, name=f"{tag}_bmm_upg_x")
    du2 = mm(dhv, w["wupv"], tb=True, add=du2, name=f"{tag}_bmm_upv_x")
    g["wupg"] = mm(s["u2t"], dhg, name=f"{tag}_bmm_upg_w")
    g["wupv"] = mm(s["u2t"], dhv, name=f"{tag}_bmm_upv_w")
    dh1, g["norm_ffn"] = rms_bwd(s["h1"], w["norm_ffn"], du2, dh2, f"{tag}_rms_ffn_b")
    dy = mm(dh1, w["wout"], tb=True, name=f"{tag}_bmm_out_x")
    g["wout"] = mm(s["yt"], dh1, name=f"{tag}_bmm_out_w")
    dya, dpg, dpooled, g["w_pool"], g["pool_scale"] = mix_bwd(
        s["ya"], s["pg"], s["pooled"], w["w_pool"], w["pool_scale"], dy, f"{tag}_mix_b")
    dpp = poolwin_bwd(dpooled, f"{tag}_poolwin_b")
    dcq, dpz, dpba, g["pa"], g["pdt"], g["head_norm"] = gdn_bwd(
        s["cq"], s["pz"], s["pba"], w["pa"], w["pdt"], w["head_norm"], s["starts"], dya, f"{tag}_gdn_b")
    dpq, g["conv_qkv"] = conv_bwd(s["pq"], dcq, w["conv_qkv"], f"{tag}_conv_qkv_b")
    du = mm(dpq, w["wqkv"], tb=True, name=f"{tag}_bmm_qkv_x")
    du = mm(dpz, w["wz"], tb=True, add=du, name=f"{tag}_bmm_z_x")
    du = mm(dpg, w["wg"], tb=True, add=du, name=f"{tag}_bmm_gate_x")
    du = mm(dpp, w["wpl"], tb=True, add=du, name=f"{tag}_bmm_pool_x")
    du = mm(dpba, w["wba"], tb=True, add=du, name=f"{tag}_bmm_ba_x")
    g["wqkv"] = mm(s["ut"], dpq, name=f"{tag}_bmm_qkv_w")
    g["wz"] = mm(s["ut"], dpz, name=f"{tag}_bmm_z_w")
    g["wg"] = mm(s["ut"], dpg, name=f"{tag}_bmm_gate_w")
    g["wpl"] = mm(s["ut"], dpp, name=f"{tag}_bmm_pool_w")
    g["wba"] = mm(s["ut"], dpba, name=f"{tag}_bmm_ba_w")
    dh, g["norm_mix"] = rms_bwd(s["h"], w["norm_mix"], du, dh1, f"{tag}_rms_mix_b")
    return dh, g


def local_step(h0, target, layers, norm_final):
    h = h0
    saved = []
    for li, w in enumerate(layers):
        h, s = _layer_fwd(h, w, f"l{li}")
        saved.append(s)
    loss, dh, dnf = loss_head(h, norm_final, target, "loss_head")
    grads = [None] * len(layers)
    for li in reversed(range(len(layers))):
        dh, grads[li] = _layer_bwd(dh, layers[li], saved[li], f"l{li}")
    return loss, dh, grads, dnf


_Z0, _B0, _P0, _G0, _IN_DIM = 3072, 4096, 4112, 4624, 6672


def _lanes_8_to_15(v):
    return jnp.pad(v.reshape(1, HEADS).astype(F32), ((0, 0), (HEADS, 128 - 2 * HEADS)))


IN_PIECES = (("wqkv", 0, _Z0), ("wz", _Z0, _B0), ("wba", _B0, _P0), ("wpl", _P0, _G0), ("wg", _G0, _IN_DIM))
IN_SHARD = _IN_DIM // 4


def _overlaps(a, b, spans):
    return [(name, max(a, lo) - lo, min(b, hi) - max(a, lo)) for name, lo, hi in spans if max(a, lo) < min(b, hi)]


def _cat(parts):
    return parts[0] if len(parts) == 1 else jnp.concatenate(parts, axis=1)


def prep_layer(p):
    row = lambda v: v.reshape(1, -1).astype(F32)
    w_in, w_up = p["w_in"], p["w_up"]
    if not isinstance(w_in, (list, tuple)):
        w_in = [w_in[:, s * IN_SHARD:(s + 1) * IN_SHARD] for s in range(4)]
        w_up = [w_up[:, s * (D_FF // 2):(s + 1) * (D_FF // 2)] for s in range(4)]
    shards = [(s, s * IN_SHARD, (s + 1) * IN_SHARD) for s in range(4)]
    piece = {name: _cat([w_in[s][:, off:off + width].astype(BF16) for s, off, width in _overlaps(lo, hi, shards)])
             for name, lo, hi in IN_PIECES}
    return dict(
        wqkv=piece["wqkv"], wz=piece["wz"], wba=jnp.pad(piece["wba"], ((0, 0), (0, 128 - 2 * HEADS))),
        wpl=piece["wpl"], wg=piece["wg"], wout=p["w_out"].astype(BF16),
        wupg=_cat([w_up[0].astype(BF16), w_up[1].astype(BF16)]), wupv=_cat([w_up[2].astype(BF16), w_up[3].astype(BF16)]),
        wdown=p["w_down"].astype(BF16),
        conv_qkv=p["conv_qkv"].astype(F32), conv_g=p["conv_ffn"][:, :D_FF].astype(F32),
        conv_v=p["conv_ffn"][:, D_FF:].astype(F32), w_pool=p["w_pool"].astype(F32),
        pool_scale=row(p["pool_scale"]), head_norm=row(p["head_norm"]), norm_mix=row(p["norm_mix"]),
        norm_ffn=row(p["norm_ffn"]), pa=_lanes_8_to_15(p["a_log"]), pdt=_lanes_8_to_15(p["dt_bias"]))


def layer_grads(g):
    return dict(
        w_in=jnp.concatenate([g["wqkv"], g["wz"], g["wba"][:, :2 * HEADS], g["wpl"], g["wg"]], axis=1),
        conv_qkv=g["conv_qkv"][:4], a_log=g["pa"][0, HEADS:2 * HEADS], dt_bias=g["pdt"][0, HEADS:2 * HEADS],
        head_norm=g["head_norm"][0], w_pool=g["w_pool"], pool_scale=g["pool_scale"][0], w_out=g["wout"],
        norm_mix=g["norm_mix"][0], norm_ffn=g["norm_ffn"][0],
        w_up=jnp.concatenate([g["wupg"], g["wupv"]], axis=1),
        conv_ffn=jnp.concatenate([g["conv_g"][:3], g["conv_v"][:3]], axis=1), w_down=g["wdown"])


def big_grad_shards(g):
    in_shards = [_cat([g[name][:, off:off + width] for name, off, width in
                       _overlaps(s * IN_SHARD, (s + 1) * IN_SHARD, IN_PIECES)]) for s in range(4)]
    half = D_FF // 2
    up_shards = [g["wupg"][:, :half], g["wupg"][:, half:], g["wupv"][:, :half], g["wupv"][:, half:]]
    return dict(w_in=jnp.stack(in_shards), w_up=jnp.stack(up_shards),
                w_down=g["wdown"].reshape(4, D_FF // 4, D_MODEL), w_out=g["wout"].reshape(4, D_MODEL // 4, D_MODEL))


LAYER_PARAMS = ("norm_mix", "w_in", "conv_qkv", "a_log", "dt_bias", "head_norm", "w_pool", "pool_scale", "w_out",
                "norm_ffn", "w_up", "conv_ffn", "w_down")


def pad_rows(meta, x):
    return jnp.concatenate([jnp.zeros((LEAD, D_MODEL), F32), meta.astype(F32), x.astype(F32),
                            jnp.zeros((TAIL, D_MODEL), F32)], axis=0)


MESH = pl.DeviceIdType.MESH
ANY = pl.BlockSpec(memory_space=pl.ANY)


def _place():
    x, y, c = lax.axis_index("x"), lax.axis_index("y"), lax.axis_index("c")
    return x, y, c, [(1 - x, y), (x, 1 - y), (1 - x, 1 - y)]


def _my_chip():
    return 2 * lax.axis_index("x") + lax.axis_index("y")


def gather_shards(packs):
    n = len(packs)

    def body(*refs):
        p_refs, o_refs, (send_sems, recv_sems) = refs[:n], refs[n:2 * n], refs[2 * n:]
        x, y, c, chips = _place()

        def copy(a, k, chip, half, to, src=None):
            dst = o_refs[a].at[2 * chip[0] + chip[1], half]
            return pltpu.make_async_remote_copy(src_ref=dst if src is None else src, dst_ref=dst,
                                                send_sem=send_sems.at[6 * a + k], recv_sem=recv_sems.at[6 * a + k],
                                                device_id=to, device_id_type=MESH)

        first = [copy(a, j, (x, y), c, (*chip, c), src=p_refs[a].at[c]) for a in range(n) for j, chip in enumerate(chips)]
        for cp in first:
            cp.start()
        passed = []
        for a in range(n):
            for j, chip in enumerate(chips):
                copy(a, j, chip, c, (x, y, c)).wait_recv()
                passed.append(copy(a, 3 + j, chip, c, (x, y, 1 - c)))
                passed[-1].start()
        for a in range(n):
            for j, chip in enumerate(chips):
                copy(a, 3 + j, chip, 1 - c, (x, y, c)).wait_recv()
        for cp in first + passed:
            cp.wait_send()

    gathered = pl.pallas_call(
        body, in_specs=[ANY] * n, out_specs=[ANY] * n,
        out_shape=[jax.ShapeDtypeStruct((4,) + p.shape, p.dtype) for p in packs],
        scratch_shapes=[pltpu.SemaphoreType.DMA((6 * n,)), pltpu.SemaphoreType.DMA((6 * n,))],
        name="gather_shards",
    )(*packs)
    me = _my_chip()
    return [lax.dynamic_update_slice(g, p[None], (me,) + (0,) * p.ndim) for g, p in zip(gathered, packs)]


def swap_other_halves(ps):
    n = len(ps)

    def body(*refs):
        p_refs, o_refs, (send_sems, recv_sems) = refs[:n], refs[n:2 * n], refs[2 * n:]
        x, y, c, _ = _place()
        copies = [pltpu.make_async_remote_copy(src_ref=p_refs[a].at[s, 1 - c], dst_ref=o_refs[a].at[s],
                                               send_sem=send_sems.at[4 * a + s], recv_sem=recv_sems.at[4 * a + s],
                                               device_id=(x, y, 1 - c), device_id_type=MESH)
                  for a in range(n) for s in range(4)]
        for cp in copies:
            cp.start()
        for cp in copies:
            cp.wait()

    return pl.pallas_call(
        body, in_specs=[ANY] * n, out_specs=[ANY] * n,
        out_shape=[jax.ShapeDtypeStruct((4,) + p.shape[2:], p.dtype) for p in ps],
        scratch_shapes=[pltpu.SemaphoreType.DMA((4 * n,)), pltpu.SemaphoreType.DMA((4 * n,))],
        name="swap_other_halves",
    )(*ps)


def scatter_to_chips(qs):
    n = len(qs)

    def body(*refs):
        q_refs, o_refs, (send_sems, recv_sems) = refs[:n], refs[n:2 * n], refs[2 * n:]
        x, y, c, chips = _place()
        me = 2 * x + y
        copies = [pltpu.make_async_remote_copy(src_ref=q_refs[a].at[2 * chip[0] + chip[1]], dst_ref=o_refs[a].at[me],
                                               send_sem=send_sems.at[3 * a + j], recv_sem=recv_sems.at[3 * a + j],
                                               device_id=(*chip, c), device_id_type=MESH)
                  for a in range(n) for j, chip in enumerate(chips)]
        for cp in copies:
            cp.start()
        for a in range(n):
            for j, chip in enumerate(chips):
                slot = o_refs[a].at[2 * chip[0] + chip[1]]
                pltpu.make_async_remote_copy(src_ref=slot, dst_ref=slot, send_sem=send_sems.at[3 * a + j],
                                             recv_sem=recv_sems.at[3 * a + j],
                                             device_id=(x, y, c), device_id_type=MESH).wait_recv()
        for cp in copies:
            cp.wait_send()

    received = pl.pallas_call(
        body, in_specs=[ANY] * n, out_specs=[ANY] * n,
        out_shape=[jax.ShapeDtypeStruct(q.shape, q.dtype) for q in qs],
        scratch_shapes=[pltpu.SemaphoreType.DMA((3 * n,)), pltpu.SemaphoreType.DMA((3 * n,))],
        name="scatter_to_chips",
    )(*qs)
    me = _my_chip()
    return [lax.dynamic_update_slice(r, lax.dynamic_slice_in_dim(q, me, 1, axis=0), (me, 0, 0))
            for r, q in zip(received, qs)]


def join_halves(boths):
    n = len(boths)

    def body(*refs):
        o_refs, (send_sems, recv_sems) = refs[n:2 * n], refs[2 * n:]
        x, y, c, _ = _place()
        copies = [pltpu.make_async_remote_copy(src_ref=o_refs[a].at[c], dst_ref=o_refs[a].at[c],
                                               send_sem=send_sems.at[a], recv_sem=recv_sems.at[a],
                                               device_id=(x, y, 1 - c), device_id_type=MESH) for a in range(n)]
        for cp in copies:
            cp.start()
        for a in range(n):
            other = o_refs[a].at[1 - c]
            pltpu.make_async_remote_copy(src_ref=other, dst_ref=other, send_sem=send_sems.at[a],
                                         recv_sem=recv_sems.at[a], device_id=(x, y, c), device_id_type=MESH).wait_recv()
        for cp in copies:
            cp.wait_send()

    return pl.pallas_call(
        body, in_specs=[ANY] * n, out_specs=[ANY] * n,
        out_shape=[jax.ShapeDtypeStruct(b.shape, b.dtype) for b in boths],
        input_output_aliases={a: a for a in range(n)},
        scratch_shapes=[pltpu.SemaphoreType.DMA((n,)), pltpu.SemaphoreType.DMA((n,))], name="join_halves",
    )(*boths)


def add_own_half(p, other, c, out_dtype, name):
    _, _, rows, lanes = p.shape
    tr = _tile(rows, max(16, 524288 // lanes), 16)

    def body(c_ref, p_ref, o_ref, out_ref):
        out_ref[...] = (p_ref[...] + o_ref[...]).astype(out_dtype)

    return pl.pallas_call(
        body,
        grid_spec=pltpu.PrefetchScalarGridSpec(
            num_scalar_prefetch=1, grid=(4, rows // tr),
            in_specs=[pl.BlockSpec((None, None, tr, lanes), lambda s, i, c_ref: (s, c_ref[0], i, 0)),
                      pl.BlockSpec((None, tr, lanes), lambda s, i, c_ref: (s, i, 0))],
            out_specs=pl.BlockSpec((None, tr, lanes), lambda s, i, c_ref: (s, i, 0))),
        out_shape=jax.ShapeDtypeStruct((4, rows, lanes), out_dtype),
        compiler_params=_params("parallel", "parallel"), name=name,
    )(c, p, other)


def sum_chips(b, c, name):
    _, rows, lanes = b.shape
    tr = _tile(rows, max(16, 524288 // lanes), 16)

    def body(c_ref, b_ref, out_ref):
        b0, b1, b2, b3 = (b_ref[k].astype(F32) for k in range(4))
        out_ref[...] = ((b0 + b1) + b2) + b3

    return pl.pallas_call(
        body,
        grid_spec=pltpu.PrefetchScalarGridSpec(
            num_scalar_prefetch=1, grid=(rows // tr,),
            in_specs=[pl.BlockSpec((4, tr, lanes), lambda i, c_ref: (0, i, 0))],
            out_specs=pl.BlockSpec((None, tr, lanes), lambda i, c_ref: (c_ref[0], i, 0))),
        out_shape=jax.ShapeDtypeStruct((2, rows, lanes), F32),
        compiler_params=_params("parallel"), name=name,
    )(c, b)


def all_reduce_to_shards(packs, wires, tags, c):
    others = swap_other_halves(packs)
    qs = [add_own_half(p, o, c, wire, f"add_own_half_{tag}") for p, o, wire, tag in zip(packs, others, wires, tags)]
    return join_halves([sum_chips(r, c, f"sum_chips_{tag}") for r, tag in zip(scatter_to_chips(qs), tags)])


def adamw(w, g, m, v, name):
    shape = w.shape
    cols = shape[-1]
    w2, g2, m2, v2 = (a.reshape(-1, cols) for a in (w, g, m, v))
    rows = w2.shape[0]
    tr = _tile(rows, max(8, 262144 // cols), 8) if rows % 8 == 0 else rows
    c1 = 1.0 - ADAM_B1 ** ADAM_STEP
    c2 = 1.0 - ADAM_B2 ** ADAM_STEP

    def body(w_ref, g_ref, m_ref, v_ref, d_ref, mo_ref, vo_ref):
        gv = g_ref[...]
        mn = ADAM_B1 * m_ref[...] + (1.0 - ADAM_B1) * gv
        vn = ADAM_B2 * v_ref[...] + (1.0 - ADAM_B2) * jnp.square(gv)
        d_ref[...] = -ADAM_LR * ((mn / c1) / (jnp.sqrt(vn / c2) + ADAM_EPS) + ADAM_WD * w_ref[...])
        mo_ref[...] = mn
        vo_ref[...] = vn

    spec = pl.BlockSpec((tr, cols), lambda i: (i, 0))
    out = jax.ShapeDtypeStruct((rows, cols), F32)
    d, mn, vn = pl.pallas_call(
        body, grid=(rows // tr,), in_specs=[spec] * 4, out_specs=(spec,) * 3, out_shape=(out,) * 3,
        compiler_params=_params("parallel"), name=name,
    )(w2, g2, m2, v2)
    return d.reshape(shape), mn.reshape(shape), vn.reshape(shape)


BIG = ("w_in", "w_up", "w_down", "w_out")
SMALL = ("w_pool", "conv_qkv", "conv_ffn", "meta_tokens")
SHARDED = BIG + SMALL
MATMUL_WEIGHTS = BIG + ("w_pool",)
REPLICATED = ("norm_mix", "a_log", "dt_bias", "head_norm", "pool_scale", "norm_ffn", "norm_final")
SHARD_AXIS = {"w_in": 2, "w_up": 2, "w_out": 1, "w_down": 1, "w_pool": 3, "conv_qkv": 2, "conv_ffn": 2, "meta_tokens": 1}


def _rows_of(a):
    return a.reshape(-1, 128)


SEGMENT_ROWS = 16


def _segment(n_rows):
    return -(-n_rows // SEGMENT_ROWS) * SEGMENT_ROWS


def _pad_segment(a):
    pad = [(0, 0)] * a.ndim
    pad[-2] = (0, _segment(a.shape[-2]) - a.shape[-2])
    return jnp.pad(a, pad)


def _unshard(stacked, axis):
    full = jnp.moveaxis(stacked, 0, axis)
    shape = list(full.shape)
    shape[axis:axis + 2] = [shape[axis] * shape[axis + 1]]
    return full.reshape(shape)


def _shard_stack(full, axis):
    shape = list(full.shape)
    shape[axis:axis + 1] = [4, shape[axis] // 4]
    return jnp.moveaxis(full.reshape(shape), axis, 0)


def pack_weights(shards):
    parts = [_rows_of(shards[k].astype(WIRE)) if k in MATMUL_WEIGHTS else
             lax.bitcast_convert_type(_rows_of(shards[k].astype(F32)), WIRE).reshape(-1, 128) for k in SMALL]
    parts = [_pad_segment(p) for p in parts]
    rows = sum(p.shape[0] for p in parts)
    if rows % (2 * SEGMENT_ROWS):
        parts.append(jnp.zeros((SEGMENT_ROWS, 128), WIRE))
        rows += SEGMENT_ROWS
    return [shards[k].astype(WIRE) for k in BIG] + [jnp.concatenate(parts, axis=0).reshape(2, rows // 2, 128)]


def unpack_weights(gathered, shard_shapes):
    out = {k: _unshard(g, SHARD_AXIS[k]) for k, g in zip(BIG, gathered)}
    flat = gathered[-1].reshape(4, -1, 128)
    at = 0
    for k in SMALL:
        shp = shard_shapes[k]
        n = 1
        for e in shp:
            n *= e
        if k in MATMUL_WEIGHTS:
            r = n // 128
            stacked = flat[:, at:at + r].reshape((4,) + tuple(shp))
        else:
            r = 2 * n // 128
            stacked = lax.bitcast_convert_type(flat[:, at:at + r].reshape(4, n // 128, 128, 2), F32).reshape((4,) + tuple(shp))
        out[k] = _unshard(stacked, SHARD_AXIS[k])
        at += _segment(r)
    return out


def pack_grads(big, full, repl):
    r = jnp.concatenate([repl[k].reshape(-1) for k in REPLICATED])
    r = jnp.pad(r, (0, -r.shape[0] % 128)).reshape(1, -1, 128)
    parts = [_shard_stack(full[k], SHARD_AXIS[k]).reshape(4, -1, 128) for k in SMALL]
    parts = [_pad_segment(p) for p in parts + [jnp.broadcast_to(r, (4,) + r.shape[1:])]]
    rows = sum(p.shape[1] for p in parts)
    if rows % (2 * SEGMENT_ROWS):
        parts.append(jnp.zeros((4, SEGMENT_ROWS, 128), F32))
        rows += SEGMENT_ROWS
    side = jnp.concatenate(parts, axis=1).reshape(4, 2, rows // 2, 128)
    return list(big) + [side]


def unpack_grads(reduced, shard_shapes, repl_shapes):
    out = dict(zip(BIG, reduced))
    side = reduced[-1].reshape(-1, 128)
    at = 0
    for k in SMALL:
        n = 1
        for e in shard_shapes[k]:
            n *= e
        out[k] = side[at:at + n // 128].reshape(shard_shapes[k])
        at += _segment(n // 128)
    r = side[at:].reshape(-1)
    at = 0
    for k in REPLICATED:
        n = 1
        for e in repl_shapes[k]:
            n *= e
        out[k] = r[at:at + n].reshape(repl_shapes[k])
        at += n
    return out


WEIGHT_ORDER = ("meta_tokens", "norm_mix", "w_in", "conv_qkv", "a_log", "dt_bias", "head_norm", "w_pool", "pool_scale",
                "w_out", "norm_ffn", "w_up", "conv_ffn", "w_down", "norm_final")


def kernel(x, meta_tokens, norm_mix, w_in, conv_qkv, a_log, dt_bias, head_norm, w_pool, pool_scale, w_out, norm_ffn, w_up, conv_ffn, w_down, norm_final, loss_target, m_meta_tokens, m_norm_mix, m_w_in, m_conv_qkv, m_a_log, m_dt_bias, m_head_norm, m_w_pool, m_pool_scale, m_w_out, m_norm_ffn, m_w_up, m_conv_ffn, m_w_down, m_norm_final, v_meta_tokens, v_norm_mix, v_w_in, v_conv_qkv, v_a_log, v_dt_bias, v_head_norm, v_w_pool, v_pool_scale, v_w_out, v_norm_ffn, v_w_up, v_conv_ffn, v_w_down, v_norm_final):
    weights = dict(meta_tokens=meta_tokens, norm_mix=norm_mix, w_in=w_in, conv_qkv=conv_qkv, a_log=a_log,
                   dt_bias=dt_bias, head_norm=head_norm, w_pool=w_pool, pool_scale=pool_scale, w_out=w_out,
                   norm_ffn=norm_ffn, w_up=w_up, conv_ffn=conv_ffn, w_down=w_down, norm_final=norm_final)
    m_in = dict(zip(WEIGHT_ORDER, (m_meta_tokens, m_norm_mix, m_w_in, m_conv_qkv, m_a_log, m_dt_bias, m_head_norm,
                                   m_w_pool, m_pool_scale, m_w_out, m_norm_ffn, m_w_up, m_conv_ffn, m_w_down, m_norm_final)))
    v_in = dict(zip(WEIGHT_ORDER, (v_meta_tokens, v_norm_mix, v_w_in, v_conv_qkv, v_a_log, v_dt_bias, v_head_norm,
                                   v_w_pool, v_pool_scale, v_w_out, v_norm_ffn, v_w_up, v_conv_ffn, v_w_down, v_norm_final)))
    shard_shapes = {k: weights[k].shape for k in SHARDED}
    repl_shapes = {k: weights[k].shape for k in REPLICATED}
    core = lax.axis_index("c").astype(jnp.int32).reshape(1)

    gathered = gather_shards(pack_weights({k: weights[k] for k in SHARDED}))
    full = unpack_weights(gathered, shard_shapes)
    shards = dict(zip(BIG, gathered))
    layers = []
    for li in range(DEPTH):
        p = {k: (full[k][li] if k in SMALL else weights[k][li]) for k in LAYER_PARAMS if k not in BIG}
        p.update(w_in=[shards["w_in"][s, li] for s in range(4)], w_up=[shards["w_up"][s, li] for s in range(4)],
                 w_down=shards["w_down"][:, li].reshape(D_FF, D_MODEL),
                 w_out=shards["w_out"][:, li].reshape(D_MODEL, D_MODEL))
        layers.append(prep_layer(p))

    h0 = pad_rows(full["meta_tokens"], x[0])
    target = pad_rows(jnp.zeros((N_META, D_MODEL), F32), loss_target[0])
    loss, dh0, grads, d_norm_final = local_step(h0, target, layers, norm_final.reshape(1, D_MODEL))
    seq = x.shape[1]
    grad_x = dh0[LEAD + N_META:LEAD + N_META + seq][None]

    per_layer = [layer_grads(g) for g in grads]
    g_all = {k: jnp.stack([pl_[k] for pl_ in per_layer]) for k in LAYER_PARAMS if k not in BIG}
    g_all["meta_tokens"] = dh0[LEAD:LEAD + N_META]
    g_all["norm_final"] = d_norm_final[0]
    big = [big_grad_shards(g) for g in grads]
    packs = pack_grads([jnp.stack([b[k] for b in big], axis=1) for k in BIG],
                       {k: g_all[k] for k in SMALL}, {k: g_all[k] for k in REPLICATED})
    reduced = all_reduce_to_shards(packs, [GRAD_WIRE] * len(BIG) + [F32], BIG + ("side",), core)
    g_mine = unpack_grads(reduced, shard_shapes, repl_shapes)

    loss_sum = lax.psum(loss[0, 0], ("x", "y", "c"))
    deltas, new_m, new_v = {}, {}, {}
    for k in WEIGHT_ORDER:
        deltas[k], new_m[k], new_v[k] = adamw(weights[k], g_mine[k], m_in[k], v_in[k], f"adamw_{k}")
    return (loss_sum, grad_x, *[g_mine[k] for k in WEIGHT_ORDER], *[deltas[k] for k in WEIGHT_ORDER],
            *[new_m[k] for k in WEIGHT_ORDER], *[new_v[k] for k in WEIGHT_ORDER])
```

```python
import functools

import jax
import jax.numpy as jnp
from jax import lax
from jax.experimental import pallas as pl
from jax.experimental.pallas import tpu as pltpu

F32 = jnp.float32
BF16 = jnp.bfloat16
WIRE = jnp.bfloat16
GRAD_WIRE = jnp.bfloat16

D_MODEL = 1024
HEADS = 8
HEAD_DIM = 128
CHUNK = 64
N_META = 16
LEAD = 48
TAIL = 64
QKV_DIM = 3072
D_FF = 2816
POOL_WIDTH = 512
POOL_WINDOWS = (2, 4, 8, 16)
DEPTH = 2
NORM_EPS = 1e-6
ADAM_LR, ADAM_B1, ADAM_B2, ADAM_EPS, ADAM_WD, ADAM_STEP = 0.001, 0.9, 0.999, 1e-08, 0.01, 10
VMEM_LIMIT_BYTES = 48 * 1024 * 1024


def _params(*sem):
    return pltpu.CompilerParams(dimension_semantics=sem if sem else None, vmem_limit_bytes=VMEM_LIMIT_BYTES)


def _tile(n, cap, mult):
    best = None
    for t in range(mult, min(n, cap) + 1, mult):
        if n % t == 0:
            best = t
    assert best is not None, (n, cap, mult)
    return best


def _silu(x):
    return x * jax.nn.sigmoid(x)


def _softplus(x):
    return jnp.maximum(x, 0.0) + jnp.log(1.0 + jnp.exp(-jnp.abs(x)))


def _split_bf16(a):
    hi = a.astype(BF16)
    return hi, (a - hi.astype(F32)).astype(BF16)


def _dg(a, b, ca, cb, hi):
    dims = (((ca,), (cb,)), ((), ()))
    if hi is True:
        return lax.dot_general(a, b, dims, precision=lax.Precision.HIGHEST, preferred_element_type=F32)
    if hi == 3:
        (ah, al), (bh, bl) = _split_bf16(a), _split_bf16(b)
        dot = lambda x, y: lax.dot_general(x, y, dims, preferred_element_type=F32)
        return dot(ah, bh) + (dot(ah, bl) + dot(al, bh))
    return lax.dot_general(a.astype(BF16), b.astype(BF16), dims, preferred_element_type=F32)


def _make_dots(hi):
    @jax.custom_vjp
    def nn(a, b):
        return _dg(a, b, 1, 0, hi)

    @jax.custom_vjp
    def nt(a, b):
        return _dg(a, b, 1, 1, hi)

    @jax.custom_vjp
    def tn(a, b):
        return _dg(a, b, 0, 0, hi)

    nn.defvjp(lambda a, b: (nn(a, b), (a, b)), lambda r, g: (nt(g, r[1]), tn(r[0], g)))
    nt.defvjp(lambda a, b: (nt(a, b), (a, b)), lambda r, g: (nn(g, r[1]), tn(g, r[0])))
    tn.defvjp(lambda a, b: (tn(a, b), (a, b)), lambda r, g: (nt(r[1], g), nn(r[0], g)))
    return nn, nt, tn


_nn, _nt, _tn = _make_dots(False)
_hnn, _hnt, _htn = _make_dots(True)


def _neumann(a):
    n = a[0].shape[0]
    eye = (lax.broadcasted_iota(jnp.int32, (n, n), 0) == lax.broadcasted_iota(jnp.int32, (n, n), 1)).astype(F32)
    hd = range(len(a))
    p = [_dg(a[h], a[h], 1, 0, 3) for h in hd]
    x = [(eye - a[h]) + p[h] - _dg(a[h], p[h], 1, 0, False) for h in hd]
    for _ in range(4):
        p = [_dg(p[h], p[h], 1, 0, False) for h in hd]
        x = [x[h] + p[h] + _dg(x[h] - eye, p[h], 1, 0, False) for h in hd]
    return tuple(x)


@jax.custom_vjp
def _inv_unit_lower(a):
    return _neumann(a)


def _inv_unit_lower_bwd(x, g):
    t = [_dg(x[h], g[h], 0, 0, 3) for h in range(len(x))]
    return (tuple(-_dg(t[h], x[h], 1, 1, 3) for h in range(len(x))),)


_inv_unit_lower.defvjp(lambda a: (_neumann(a),) * 2, _inv_unit_lower_bwd)


def mm(a, b, *, tb=False, add=None, out_dtype=F32, name):
    m, kdim = a.shape
    (n, kb) = b.shape if tb else b.shape[::-1]
    assert kdim == kb, (a.shape, b.shape, tb)
    tm = _tile(m, 1408, 128) if m % 128 == 0 and m <= 4096 else _tile(m, 640, 64)
    tn = _tile(n, 1536, 128)
    tk = kdim if kdim <= 3072 else _tile(kdim, 1664 if a.dtype == b.dtype == BF16 else 640, 128)
    nk = kdim // tk
    dims = (((1,), (1 if tb else 0,)), ((), ()))

    def body(*refs):
        if add is not None:
            a_ref, b_ref, add_ref, o_ref, acc = refs
        else:
            a_ref, b_ref, o_ref, acc = refs
        k = pl.program_id(2)
        part = lax.dot_general(a_ref[...].astype(BF16), b_ref[...].astype(BF16), dims, preferred_element_type=F32)

        def finish(r):
            if add is not None:
                r = r + add_ref[...]
            o_ref[...] = r.astype(out_dtype)

        if nk == 1:
            finish(part)
        else:
            @pl.when(k == 0)
            def _():
                acc[...] = part

            @pl.when(jnp.logical_and(k > 0, k < nk - 1))
            def _():
                acc[...] += part

            @pl.when(k == nk - 1)
            def _():
                finish(acc[...] + part)

    a_spec = pl.BlockSpec((tm, tk), lambda j, i, k: (i, k))
    b_spec = pl.BlockSpec((tn, tk), lambda j, i, k: (j, k)) if tb else pl.BlockSpec((tk, tn), lambda j, i, k: (k, j))
    in_specs = [a_spec, b_spec]
    args = [a, b]
    if add is not None:
        in_specs.append(pl.BlockSpec((tm, tn), lambda j, i, k: (i, j)))
        args.append(add)
    return pl.pallas_call(
        body, grid=(n // tn, m // tm, nk), in_specs=in_specs,
        out_specs=pl.BlockSpec((tm, tn), lambda j, i, k: (i, j)),
        out_shape=jax.ShapeDtypeStruct((m, n), out_dtype),
        scratch_shapes=[pltpu.VMEM((tm, tn) if nk > 1 else (8, 128), F32)],
        compiler_params=_params("parallel", "parallel", "arbitrary"), name=name,
    )(*args)


def _rms(x, gain):
    return x * lax.rsqrt(jnp.mean(x * x, axis=-1, keepdims=True) + NORM_EPS) * gain


def rms_fwd(h, gain, name):
    t = h.shape[0]
    ts = _tile(t, 640, 128)

    def body(h_ref, g_ref, u_ref, ut_ref):
        u = _rms(h_ref[...], g_ref[...])
        u_ref[...] = u.astype(BF16)
        ut_ref[...] = u.T.astype(BF16)

    return pl.pallas_call(
        body, grid=(t // ts,),
        in_specs=[pl.BlockSpec((ts, D_MODEL), lambda i: (i, 0)), pl.BlockSpec((1, D_MODEL), lambda i: (0, 0))],
        out_specs=(pl.BlockSpec((ts, D_MODEL), lambda i: (i, 0)), pl.BlockSpec((D_MODEL, ts), lambda i: (0, i))),
        out_shape=(jax.ShapeDtypeStruct((t, D_MODEL), BF16), jax.ShapeDtypeStruct((D_MODEL, t), BF16)),
        compiler_params=_params("parallel"), name=name,
    )(h, gain)


def rms_bwd(h, gain, du, dres, name):
    t = h.shape[0]
    ts = _tile(t, 640, 64)

    def body(h_ref, g_ref, du_ref, dres_ref, dh_ref, dg_ref):
        i = pl.program_id(0)
        _, vjp = jax.vjp(_rms, h_ref[...], g_ref[...])
        dx, dg = vjp(du_ref[...])
        row = i * ts + lax.broadcasted_iota(jnp.int32, (ts, 1), 0)
        dh_ref[...] = jnp.where(row >= LEAD, dx + dres_ref[...], 0.0)

        @pl.when(i == 0)
        def _():
            dg_ref[...] = jnp.zeros_like(dg_ref)

        dg_ref[...] += dg

    row_spec = pl.BlockSpec((ts, D_MODEL), lambda i: (i, 0))
    vec_spec = pl.BlockSpec((1, D_MODEL), lambda i: (0, 0))
    return pl.pallas_call(
        body, grid=(t // ts,), in_specs=[row_spec, vec_spec, row_spec, row_spec],
        out_specs=(row_spec, vec_spec),
        out_shape=(jax.ShapeDtypeStruct((t, D_MODEL), F32), jax.ShapeDtypeStruct((1, D_MODEL), F32)),
        compiler_params=_params("arbitrary"), name=name,
    )(h, gain, du, dres)


def loss_head(h, gain, target, name):
    t = h.shape[0]
    ts = _tile(t, 640, 64)

    def body(h_ref, g_ref, t_ref, loss_ref, dh_ref, dg_ref):
        i = pl.program_id(0)
        row = i * ts + lax.broadcasted_iota(jnp.int32, (ts, 1), 0)
        keep = jnp.logical_and(row >= LEAD + N_META, row < t - TAIL)
        tgt = t_ref[...]

        def f(x, g):
            err = jnp.where(keep, _rms(x, g) - tgt, 0.0)
            per_row = jnp.mean(err * err, axis=-1, keepdims=True)
            return 0.5 * jnp.sum(per_row, axis=0, keepdims=True)

        val, vjp = jax.vjp(f, h_ref[...], g_ref[...])
        dx, dg = vjp(jnp.ones((1, 1), F32))
        dh_ref[...] = dx

        @pl.when(i == 0)
        def _():
            dg_ref[...] = jnp.zeros_like(dg_ref)
            loss_ref[...] = jnp.zeros_like(loss_ref)

        dg_ref[...] += dg
        loss_ref[...] += jnp.broadcast_to(val, (1, 128))

    row_spec = pl.BlockSpec((ts, D_MODEL), lambda i: (i, 0))
    vec_spec = pl.BlockSpec((1, D_MODEL), lambda i: (0, 0))
    return pl.pallas_call(
        body, grid=(t // ts,), in_specs=[row_spec, vec_spec, row_spec],
        out_specs=(pl.BlockSpec((1, 128), lambda i: (0, 0)), row_spec, vec_spec),
        out_shape=(jax.ShapeDtypeStruct((1, 128), F32), jax.ShapeDtypeStruct((t, D_MODEL), F32),
                   jax.ShapeDtypeStruct((1, D_MODEL), F32)),
        compiler_params=_params("arbitrary"), name=name,
    )(h, gain, target)


def conv_fwd(x, w, name):
    t, width = x.shape
    k = w.shape[0]
    ts = _tile(t, 640, 64)
    tw = _tile(width, 1536, 128)
    hb = ts // 8

    def body(x_ref, halo_ref, w_ref, o_ref, buf):
        i = pl.program_id(0)
        buf[0:8, :] = jnp.where(i > 0, halo_ref[...], 0.0)
        buf[8:, :] = x_ref[...]
        wv = w_ref[...]
        acc = buf[pl.ds(8 - (k - 1), ts), :] * wv[0:1, :]
        for j in range(1, k):
            acc = acc + buf[pl.ds(8 - (k - 1) + j, ts), :] * wv[j:j + 1, :]
        o_ref[...] = acc

    return pl.pallas_call(
        body, grid=(t // ts, width // tw),
        in_specs=[pl.BlockSpec((ts, tw), lambda i, j: (i, j)),
                  pl.BlockSpec((8, tw), lambda i, j: (jnp.maximum(i * hb - 1, 0), j)),
                  pl.BlockSpec((k, tw), lambda i, j: (0, j))],
        out_specs=pl.BlockSpec((ts, tw), lambda i, j: (i, j)),
        out_shape=jax.ShapeDtypeStruct((t, width), F32),
        scratch_shapes=[pltpu.VMEM((ts + 8, tw), F32)],
        compiler_params=_params("parallel", "parallel"), name=name,
    )(x, x, w)


def conv_bwd(x, dc, w, name):
    t, width = x.shape
    k = w.shape[0]
    ts = _tile(t, 640, 64)
    tw = _tile(width, 1536, 128)
    hb = ts // 8
    nt = t // ts

    def body(x_ref, xh_ref, dc_ref, dch_ref, w_ref, dx_ref, dw_ref, xbuf, dbuf):
        i = pl.program_id(1)
        xbuf[0:8, :] = jnp.where(i > 0, xh_ref[...], 0.0)
        xbuf[8:, :] = x_ref[...]
        d = dc_ref[...]
        dbuf[0:ts, :] = d
        dbuf[ts:, :] = jnp.where(i < nt - 1, dch_ref[...], 0.0)
        wv = w_ref[...]
        acc = dbuf[pl.ds(k - 1, ts), :] * wv[0:1, :]
        for j in range(1, k):
            acc = acc + dbuf[pl.ds(k - 1 - j, ts), :] * wv[j:j + 1, :]
        dx_ref[...] = acc.astype(BF16)

        @pl.when(i == 0)
        def _():
            dw_ref[...] = jnp.zeros_like(dw_ref)

        for j in range(k):
            dw_ref[j:j + 1, :] += jnp.sum(d * xbuf[pl.ds(8 - (k - 1) + j, ts), :], axis=0, keepdims=True)

    return pl.pallas_call(
        body, grid=(width // tw, nt),
        in_specs=[pl.BlockSpec((ts, tw), lambda j, i: (i, j)),
                  pl.BlockSpec((8, tw), lambda j, i: (jnp.maximum(i * hb - 1, 0), j)),
                  pl.BlockSpec((ts, tw), lambda j, i: (i, j)),
                  pl.BlockSpec((8, tw), lambda j, i: (jnp.minimum((i + 1) * hb, t // 8 - 1), j)),
                  pl.BlockSpec((k, tw), lambda j, i: (0, j))],
        out_specs=(pl.BlockSpec((ts, tw), lambda j, i: (i, j)), pl.BlockSpec((8, tw), lambda j, i: (0, j))),
        out_shape=(jax.ShapeDtypeStruct((t, width), BF16), jax.ShapeDtypeStruct((8, width), F32)),
        scratch_shapes=[pltpu.VMEM((ts + 8, tw), F32), pltpu.VMEM((ts + 8, tw), F32)],
        compiler_params=_params("parallel", "arbitrary"), name=name,
    )(x, x, dc, dc, w)


def _pool_count(pos, win):
    return jnp.clip(pos + 1, 1, win).astype(F32)


def poolwin_fwd(p, name):
    t = p.shape[0]
    ts = _tile(t, 640, 64)
    hb = ts // 16

    def body(p_ref, halo_ref, o_ref, buf):
        i = pl.program_id(0)
        buf[0:16, :] = jnp.where(i > 0, halo_ref[...], 0.0)
        buf[16:, :] = p_ref[...]
        pos = i * ts + lax.broadcasted_iota(jnp.int32, (ts, 1), 0) - LEAD
        for gi, win in enumerate(POOL_WINDOWS):
            cols = slice(gi * 128, (gi + 1) * 128)
            own = buf[pl.ds(16, ts), cols]
            acc = own
            for j in range(1, win):
                acc = acc + buf[pl.ds(16 - j, ts), cols]
            o_ref[:, cols] = acc / _pool_count(pos, win) - own

    return pl.pallas_call(
        body, grid=(t // ts,),
        in_specs=[pl.BlockSpec((ts, POOL_WIDTH), lambda i: (i, 0)),
                  pl.BlockSpec((16, POOL_WIDTH), lambda i: (jnp.maximum(i * hb - 1, 0), 0))],
        out_specs=pl.BlockSpec((ts, POOL_WIDTH), lambda i: (i, 0)),
        out_shape=jax.ShapeDtypeStruct((t, POOL_WIDTH), F32),
        scratch_shapes=[pltpu.VMEM((ts + 16, POOL_WIDTH), F32)],
        compiler_params=_params("parallel"), name=name,
    )(p, p)


def poolwin_bwd(dpooled, name):
    t = dpooled.shape[0]
    ts = _tile(t, 640, 64)
    hb = ts // 16
    nt = t // ts

    def body(d_ref, halo_ref, o_ref, buf):
        i = pl.program_id(0)
        buf[0:ts, :] = d_ref[...]
        buf[ts:, :] = jnp.where(i < nt - 1, halo_ref[...], 0.0)
        pos = i * ts + lax.broadcasted_iota(jnp.int32, (ts, 1), 0) - LEAD
        for gi, win in enumerate(POOL_WINDOWS):
            cols = slice(gi * 128, (gi + 1) * 128)
            own = buf[pl.ds(0, ts), cols]
            acc = own / _pool_count(pos, win)
            for j in range(1, win):
                acc = acc + buf[pl.ds(j, ts), cols] / _pool_count(pos + j, win)
            o_ref[:, cols] = (acc - own).astype(BF16)

    return pl.pallas_call(
        body, grid=(nt,),
        in_specs=[pl.BlockSpec((ts, POOL_WIDTH), lambda i: (i, 0)),
                  pl.BlockSpec((16, POOL_WIDTH), lambda i: (jnp.minimum((i + 1) * hb, t // 16 - 1), 0))],
        out_specs=pl.BlockSpec((ts, POOL_WIDTH), lambda i: (i, 0)),
        out_shape=jax.ShapeDtypeStruct((t, POOL_WIDTH), BF16),
        scratch_shapes=[pltpu.VMEM((ts + 16, POOL_WIDTH), F32)],
        compiler_params=_params("parallel"), name=name,
    )(dpooled, dpooled)


def _mix(y_a, gpre, pooled, w_pool, scale):
    parts = [_nn(pooled[:, g * 128:(g + 1) * 128], w_pool[g]) for g in range(4)]
    y_b = jnp.concatenate(parts, axis=1) * scale
    return jax.nn.sigmoid(gpre[:, :D_MODEL]) * y_a + jax.nn.sigmoid(gpre[:, D_MODEL:]) * y_b


def _mix_specs(ts):
    return [pl.BlockSpec((ts, D_MODEL), lambda i: (i, 0)), pl.BlockSpec((ts, 2 * D_MODEL), lambda i: (i, 0)),
            pl.BlockSpec((ts, POOL_WIDTH), lambda i: (i, 0)), pl.BlockSpec((4, 128, 256), lambda i: (0, 0, 0)),
            pl.BlockSpec((1, D_MODEL), lambda i: (0, 0))]


def mix_fwd(y_a, gpre, pooled, w_pool, scale, name):
    t = y_a.shape[0]
    ts = _tile(t, 640, 128)

    def body(ya_ref, g_ref, p_ref, w_ref, s_ref, o_ref, ot_ref):
        y = _mix(ya_ref[...], g_ref[...], p_ref[...], w_ref[...], s_ref[...])
        o_ref[...] = y.astype(BF16)
        ot_ref[...] = y.T.astype(BF16)

    return pl.pallas_call(
        body, grid=(t // ts,), in_specs=_mix_specs(ts),
        out_specs=(pl.BlockSpec((ts, D_MODEL), lambda i: (i, 0)), pl.BlockSpec((D_MODEL, ts), lambda i: (0, i))),
        out_shape=(jax.ShapeDtypeStruct((t, D_MODEL), BF16), jax.ShapeDtypeStruct((D_MODEL, t), BF16)),
        compiler_params=_params("parallel"), name=name,
    )(y_a, gpre, pooled, w_pool, scale)


def mix_bwd(y_a, gpre, pooled, w_pool, scale, dy, name):
    t = y_a.shape[0]
    ts = _tile(t, 320, 64)

    def body(ya_ref, g_ref, p_ref, w_ref, s_ref, dy_ref, dya_ref, dg_ref, dp_ref, dw_ref, ds_ref):
        i = pl.program_id(0)
        _, vjp = jax.vjp(_mix, ya_ref[...], g_ref[...], p_ref[...], w_ref[...], s_ref[...])
        dya, dg, dp, dw, ds = vjp(dy_ref[...])
        dya_ref[...] = dya
        dg_ref[...] = dg.astype(BF16)
        dp_ref[...] = dp

        @pl.when(i == 0)
        def _():
            dw_ref[...] = jnp.zeros_like(dw_ref)
            ds_ref[...] = jnp.zeros_like(ds_ref)

        dw_ref[...] += dw
        ds_ref[...] += ds

    specs = _mix_specs(ts)
    return pl.pallas_call(
        body, grid=(t // ts,), in_specs=specs + [specs[0]],
        out_specs=(specs[0], specs[1], specs[2], specs[3], specs[4]),
        out_shape=(jax.ShapeDtypeStruct((t, D_MODEL), F32), jax.ShapeDtypeStruct((t, 2 * D_MODEL), BF16),
                   jax.ShapeDtypeStruct((t, POOL_WIDTH), F32), jax.ShapeDtypeStruct((4, 128, 256), F32),
                   jax.ShapeDtypeStruct((1, D_MODEL), F32)),
        compiler_params=_params("arbitrary"), name=name,
    )(y_a, gpre, pooled, w_pool, scale, dy)


def _ffn_act(cg, cv):
    return _silu(cg) * cv


def ffnact_fwd(cg, cv, name):
    t, width = cg.shape
    ts = _tile(t, 640, 128)
    tw = _tile(width, 1536, 128)
    spec = pl.BlockSpec((ts, tw), lambda i, j: (i, j))

    def body(g_ref, v_ref, o_ref, ot_ref):
        act = _ffn_act(g_ref[...], v_ref[...])
        o_ref[...] = act.astype(BF16)
        ot_ref[...] = act.T.astype(BF16)

    return pl.pallas_call(
        body, grid=(t // ts, width // tw), in_specs=[spec, spec],
        out_specs=(spec, pl.BlockSpec((tw, ts), lambda i, j: (j, i))),
        out_shape=(jax.ShapeDtypeStruct((t, width), BF16), jax.ShapeDtypeStruct((width, t), BF16)),
        compiler_params=_params("parallel", "parallel"), name=name,
    )(cg, cv)


def ffnact_bwd(cg, cv, dact, name):
    t, width = cg.shape
    ts = _tile(t, 640, 64)
    tw = _tile(width, 1536, 128)
    spec = pl.BlockSpec((ts, tw), lambda i, j: (i, j))

    def body(g_ref, v_ref, d_ref, dg_ref, dv_ref):
        _, vjp = jax.vjp(_ffn_act, g_ref[...], v_ref[...])
        dg_ref[...], dv_ref[...] = vjp(d_ref[...])

    return pl.pallas_call(
        body, grid=(t // ts, width // tw), in_specs=[spec, spec, spec], out_specs=(spec, spec),
        out_shape=(jax.ShapeDtypeStruct((t, width), F32), jax.ShapeDtypeStruct((t, width), F32)),
        compiler_params=_params("parallel", "parallel"), name=name,
    )(cg, cv, dact)


def _gdn_chunk(c, z, ba, pa, pdt, hn, s, *, valid):
    r = lax.broadcasted_iota(jnp.int32, (CHUNK, CHUNK), 0)
    q_ = lax.broadcasted_iota(jnp.int32, (CHUNK, CHUNK), 1)
    causal = r >= q_
    strict = r > q_
    tril = causal.astype(F32)
    triu = (r <= q_).astype(F32)
    lane = lax.broadcasted_iota(jnp.int32, (CHUNK, 128), 1)

    decay_log = -jnp.exp(pa) * _softplus(ba + pdt)
    bg = jnp.where(lane < HEADS, jax.nn.sigmoid(ba), jnp.where(lane < 2 * HEADS, decay_log, 0.0))
    bg = jnp.where(valid, bg, 0.0)
    gc = _hnn(tril, bg)
    gct = _hnn(bg.T, triu)
    eg = jnp.exp(gc)
    glast = gc[CHUNK - 1:CHUNK, :]
    ekd = jnp.exp(glast - gc)
    gtot = jnp.exp(glast)

    hd = range(HEADS)
    hs = [slice(h * HEAD_DIM, (h + 1) * HEAD_DIM) for h in hd]
    gl = [slice(HEADS + h, HEADS + h + 1) for h in hd]
    q = [_silu(c[:, hs[h]]) for h in hd]
    k = [_silu(c[:, D_MODEL + h * HEAD_DIM:D_MODEL + (h + 1) * HEAD_DIM]) for h in hd]
    v = [_silu(c[:, 2 * D_MODEL + h * HEAD_DIM:2 * D_MODEL + (h + 1) * HEAD_DIM]) for h in hd]
    q = [q[h] * lax.rsqrt(jnp.sum(q[h] * q[h], axis=-1, keepdims=True) + NORM_EPS) * (HEAD_DIM ** -0.5) for h in hd]
    k = [k[h] * lax.rsqrt(jnp.sum(k[h] * k[h], axis=-1, keepdims=True) + NORM_EPS) for h in hd]
    beta = [bg[:, h:h + 1] for h in hd]
    decay = [jnp.exp(jnp.where(causal, gc[:, gl[h]] - gct[gl[h], :], -1e30)) for h in hd]
    kb = [k[h] * beta[h] for h in hd]
    a = [jnp.where(strict, _nt(kb[h], k[h]) * decay[h], 0.0) for h in hd]
    qk = [jnp.where(causal, _nt(q[h], k[h]) * decay[h], 0.0) for h in hd]
    x = _inv_unit_lower(tuple(a))
    u = [_nn(x[h], v[h] * beta[h]) for h in hd]
    w = [_nn(x[h], kb[h] * eg[:, gl[h]]) for h in hd]
    v_new = [u[h] - _nn(w[h], s[h]) for h in hd]
    o = [_nn(q[h] * eg[:, gl[h]], s[h]) + _nn(qk[h], v_new[h]) for h in hd]
    states = [s[h] * gtot[:, gl[h]] + _tn(k[h] * ekd[:, gl[h]], v_new[h]) for h in hd]
    o = [o[h] * lax.rsqrt(jnp.mean(o[h] * o[h], axis=-1, keepdims=True) + NORM_EPS) * hn * _silu(z[:, hs[h]])
         for h in hd]
    return jnp.concatenate(o, axis=1), tuple(states)


GDN_FWD_CHUNKS = 5
GDN_BWD_CHUNKS = 2


def _chunk_valid(n, t):
    row = n * CHUNK + lax.broadcasted_iota(jnp.int32, (CHUNK, 1), 0)
    return jnp.logical_and(row >= LEAD, row < t - TAIL)


def gdn_fwd(c, z, ba, pa, pdt, hn, name):
    t = c.shape[0]
    n_chunks = t // CHUNK
    per_step = GDN_FWD_CHUNKS if n_chunks % GDN_FWD_CHUNKS == 0 else 1
    rows_per_step = per_step * CHUNK

    def body(c_ref, z_ref, ba_ref, pa_ref, pdt_ref, hn_ref, y_ref, ss_ref, state):
        step = pl.program_id(0)

        @pl.when(step == 0)
        def _():
            state[...] = jnp.zeros_like(state)

        s = tuple(state[h] for h in range(HEADS))
        for j in range(per_step):
            rows = pl.ds(j * CHUNK, CHUNK)
            for h in range(HEADS):
                ss_ref[j, h] = s[h]
            y, s = _gdn_chunk(c_ref[rows, :], z_ref[rows, :], ba_ref[rows, :], pa_ref[...], pdt_ref[...], hn_ref[...], s,
                              valid=_chunk_valid(step * per_step + j, t))
            y_ref[rows, :] = y
        for h in range(HEADS):
            state[h] = s[h]

    vec = pl.BlockSpec((1, 128), lambda n: (0, 0))
    return pl.pallas_call(
        body, grid=(n_chunks // per_step,),
        in_specs=[pl.BlockSpec((rows_per_step, QKV_DIM), lambda n: (n, 0)),
                  pl.BlockSpec((rows_per_step, D_MODEL), lambda n: (n, 0)),
                  pl.BlockSpec((rows_per_step, 128), lambda n: (n, 0)), vec, vec, vec],
        out_specs=(pl.BlockSpec((rows_per_step, D_MODEL), lambda n: (n, 0)),
                   pl.BlockSpec((per_step, HEADS, HEAD_DIM, HEAD_DIM), lambda n: (n, 0, 0, 0))),
        out_shape=(jax.ShapeDtypeStruct((t, D_MODEL), F32),
                   jax.ShapeDtypeStruct((n_chunks, HEADS, HEAD_DIM, HEAD_DIM), F32)),
        scratch_shapes=[pltpu.VMEM((HEADS, HEAD_DIM, HEAD_DIM), F32)],
        compiler_params=_params("arbitrary"), name=name,
    )(c, z, ba, pa, pdt, hn)


def gdn_bwd(c, z, ba, pa, pdt, hn, starts, dy, name):
    t = c.shape[0]
    per_step = GDN_BWD_CHUNKS if (t // CHUNK) % GDN_BWD_CHUNKS == 0 else 1
    n_steps = t // CHUNK // per_step
    rows_per_step = per_step * CHUNK

    def body(c_ref, z_ref, ba_ref, pa_ref, pdt_ref, hn_ref, ss_ref, dy_ref,
             dc_ref, dz_ref, dba_ref, dpa_ref, dpdt_ref, dhn_ref, dstate):
        step = pl.program_id(0)

        @pl.when(step == 0)
        def _():
            dstate[...] = jnp.zeros_like(dstate)
            dpa_ref[...] = jnp.zeros_like(dpa_ref)
            dpdt_ref[...] = jnp.zeros_like(dpdt_ref)
            dhn_ref[...] = jnp.zeros_like(dhn_ref)

        ds = tuple(dstate[h] for h in range(HEADS))
        for j in reversed(range(per_step)):
            rows = pl.ds(j * CHUNK, CHUNK)
            f = functools.partial(_gdn_chunk, valid=_chunk_valid((n_steps - 1 - step) * per_step + j, t))
            _, vjp = jax.vjp(f, c_ref[rows, :], z_ref[rows, :], ba_ref[rows, :], pa_ref[...], pdt_ref[...], hn_ref[...],
                             tuple(ss_ref[j, h] for h in range(HEADS)))
            dc, dz, dba, dpa, dpdt, dhn, ds = vjp((dy_ref[rows, :], ds))
            dc_ref[rows, :] = dc
            dz_ref[rows, :] = dz.astype(BF16)
            dba_ref[rows, :] = dba.astype(BF16)
            dpa_ref[...] += dpa
            dpdt_ref[...] += dpdt
            dhn_ref[...] += dhn
        for h in range(HEADS):
            dstate[h] = ds[h]

    def rev(width):
        return pl.BlockSpec((rows_per_step, width), lambda s: (n_steps - 1 - s, 0))

    vec = pl.BlockSpec((1, 128), lambda s: (0, 0))
    vec_shape = jax.ShapeDtypeStruct((1, 128), F32)
    return pl.pallas_call(
        body, grid=(n_steps,),
        in_specs=[rev(QKV_DIM), rev(D_MODEL), rev(128), vec, vec, vec,
                  pl.BlockSpec((per_step, HEADS, HEAD_DIM, HEAD_DIM), lambda s: (n_steps - 1 - s, 0, 0, 0)),
                  rev(D_MODEL)],
        out_specs=(rev(QKV_DIM), rev(D_MODEL), rev(128), vec, vec, vec),
        out_shape=(jax.ShapeDtypeStruct((t, QKV_DIM), F32), jax.ShapeDtypeStruct((t, D_MODEL), BF16),
                   jax.ShapeDtypeStruct((t, 128), BF16), vec_shape, vec_shape, vec_shape),
        scratch_shapes=[pltpu.VMEM((HEADS, HEAD_DIM, HEAD_DIM), F32)],
        compiler_params=_params("arbitrary"), name=name,
    )(c, z, ba, pa, pdt, hn, starts, dy)


def _layer_fwd(h, w, tag):
    u, ut = rms_fwd(h, w["norm_mix"], f"{tag}_rms_mix")
    pq = mm(u, w["wqkv"], name=f"{tag}_mm_qkv")
    pz = mm(u, w["wz"], name=f"{tag}_mm_z")
    pg = mm(u, w["wg"], name=f"{tag}_mm_gate")
    pp = mm(u, w["wpl"], name=f"{tag}_mm_pool")
    pba = mm(u, w["wba"], name=f"{tag}_mm_ba")
    cq = conv_fwd(pq, w["conv_qkv"], f"{tag}_conv_qkv")
    ya, starts = gdn_fwd(cq, pz, pba, w["pa"], w["pdt"], w["head_norm"], f"{tag}_gdn")
    pooled = poolwin_fwd(pp, f"{tag}_poolwin")
    y, yt = mix_fwd(ya, pg, pooled, w["w_pool"], w["pool_scale"], f"{tag}_mix")
    h1 = mm(y, w["wout"], add=h, name=f"{tag}_mm_out")
    u2, u2t = rms_fwd(h1, w["norm_ffn"], f"{tag}_rms_ffn")
    hg = mm(u2, w["wupg"], name=f"{tag}_mm_upg")
    hv = mm(u2, w["wupv"], name=f"{tag}_mm_upv")
    cg = conv_fwd(hg, w["conv_g"], f"{tag}_conv_g")
    cv = conv_fwd(hv, w["conv_v"], f"{tag}_conv_v")
    act, actt = ffnact_fwd(cg, cv, f"{tag}_act")
    h2 = mm(act, w["wdown"], add=h1, name=f"{tag}_mm_down")
    saved = dict(h=h, ut=ut, pq=pq, pz=pz, pg=pg, pba=pba, cq=cq, ya=ya, starts=starts, pooled=pooled, yt=yt, h1=h1,
                 u2t=u2t, hg=hg, hv=hv, cg=cg, cv=cv, actt=actt)
    return h2, saved


def _layer_bwd(dh2, w, s, tag):
    g = {}
    dact = mm(dh2, w["wdown"], tb=True, name=f"{tag}_bmm_down_x")
    g["wdown"] = mm(s["actt"], dh2, name=f"{tag}_bmm_down_w")
    dcg, dcv = ffnact_bwd(s["cg"], s["cv"], dact, f"{tag}_act_b")
    dhg, g["conv_g"] = conv_bwd(s["hg"], dcg, w["conv_g"], f"{tag}_conv_g_b")
    dhv, g["conv_v"] = conv_bwd(s["hv"], dcv, w["conv_v"], f"{tag}_conv_v_b")
    du2 = mm(dhg, w["wupg"], tb=True, name=f"{tag}_bmm_upg_x")
    du2 = mm(dhv, w["wupv"], tb=True, add=du2, name=f"{tag}_bmm_upv_x")
    g["wupg"] = mm(s["u2t"], dhg, name=f"{tag}_bmm_upg_w")
    g["wupv"] = mm(s["u2t"], dhv, name=f"{tag}_bmm_upv_w")
    dh1, g["norm_ffn"] = rms_bwd(s["h1"], w["norm_ffn"], du2, dh2, f"{tag}_rms_ffn_b")
    dy = mm(dh1, w["wout"], tb=True, name=f"{tag}_bmm_out_x")
    g["wout"] = mm(s["yt"], dh1, name=f"{tag}_bmm_out_w")
    dya, dpg, dpooled, g["w_pool"], g["pool_scale"] = mix_bwd(
        s["ya"], s["pg"], s["pooled"], w["w_pool"], w["pool_scale"], dy, f"{tag}_mix_b")
    dpp = poolwin_bwd(dpooled, f"{tag}_poolwin_b")
    dcq, dpz, dpba, g["pa"], g["pdt"], g["head_norm"] = gdn_bwd(
        s["cq"], s["pz"], s["pba"], w["pa"], w["pdt"], w["head_norm"], s["starts"], dya, f"{tag}_gdn_b")
    dpq, g["conv_qkv"] = conv_bwd(s["pq"], dcq, w["conv_qkv"], f"{tag}_conv_qkv_b")
    du = mm(dpq, w["wqkv"], tb=True, name=f"{tag}_bmm_qkv_x")
    du = mm(dpz, w["wz"], tb=True, add=du, name=f"{tag}_bmm_z_x")
    du = mm(dpg, w["wg"], tb=True, add=du, name=f"{tag}_bmm_gate_x")
    du = mm(dpp, w["wpl"], tb=True, add=du, name=f"{tag}_bmm_pool_x")
    du = mm(dpba, w["wba"], tb=True, add=du, name=f"{tag}_bmm_ba_x")
    g["wqkv"] = mm(s["ut"], dpq, name=f"{tag}_bmm_qkv_w")
    g["wz"] = mm(s["ut"], dpz, name=f"{tag}_bmm_z_w")
    g["wg"] = mm(s["ut"], dpg, name=f"{tag}_bmm_gate_w")
    g["wpl"] = mm(s["ut"], dpp, name=f"{tag}_bmm_pool_w")
    g["wba"] = mm(s["ut"], dpba, name=f"{tag}_bmm_ba_w")
    dh, g["norm_mix"] = rms_bwd(s["h"], w["norm_mix"], du, dh1, f"{tag}_rms_mix_b")
    return dh, g


def local_step(h0, target, layers, norm_final):
    h = h0
    saved = []
    for li, w in enumerate(layers):
        h, s = _layer_fwd(h, w, f"l{li}")
        saved.append(s)
    loss, dh, dnf = loss_head(h, norm_final, target, "loss_head")
    grads = [None] * len(layers)
    for li in reversed(range(len(layers))):
        dh, grads[li] = _layer_bwd(dh, layers[li], saved[li], f"l{li}")
    return loss, dh, grads, dnf


_Z0, _B0, _P0, _G0, _IN_DIM = 3072, 4096, 4112, 4624, 6672


def _lanes_8_to_15(v):
    return jnp.pad(v.reshape(1, HEADS).astype(F32), ((0, 0), (HEADS, 128 - 2 * HEADS)))


IN_PIECES = (("wqkv", 0, _Z0), ("wz", _Z0, _B0), ("wba", _B0, _P0), ("wpl", _P0, _G0), ("wg", _G0, _IN_DIM))
IN_SHARD = _IN_DIM // 4


def _overlaps(a, b, spans):
    return [(name, max(a, lo) - lo, min(b, hi) - max(a, lo)) for name, lo, hi in spans if max(a, lo) < min(b, hi)]


def _cat(parts):
    return parts[0] if len(parts) == 1 else jnp.concatenate(parts, axis=1)


def prep_layer(p):
    row = lambda v: v.reshape(1, -1).astype(F32)
    w_in, w_up = p["w_in"], p["w_up"]
    if not isinstance(w_in, (list, tuple)):
        w_in = [w_in[:, s * IN_SHARD:(s + 1) * IN_SHARD] for s in range(4)]
        w_up = [w_up[:, s * (D_FF // 2):(s + 1) * (D_FF // 2)] for s in range(4)]
    shards = [(s, s * IN_SHARD, (s + 1) * IN_SHARD) for s in range(4)]
    piece = {name: _cat([w_in[s][:, off:off + width].astype(BF16) for s, off, width in _overlaps(lo, hi, shards)])
             for name, lo, hi in IN_PIECES}
    return dict(
        wqkv=piece["wqkv"], wz=piece["wz"], wba=jnp.pad(piece["wba"], ((0, 0), (0, 128 - 2 * HEADS))),
        wpl=piece["wpl"], wg=piece["wg"], wout=p["w_out"].astype(BF16),
        wupg=_cat([w_up[0].astype(BF16), w_up[1].astype(BF16)]), wupv=_cat([w_up[2].astype(BF16), w_up[3].astype(BF16)]),
        wdown=p["w_down"].astype(BF16),
        conv_qkv=p["conv_qkv"].astype(F32), conv_g=p["conv_ffn"][:, :D_FF].astype(F32),
        conv_v=p["conv_ffn"][:, D_FF:].astype(F32), w_pool=p["w_pool"].astype(F32),
        pool_scale=row(p["pool_scale"]), head_norm=row(p["head_norm"]), norm_mix=row(p["norm_mix"]),
        norm_ffn=row(p["norm_ffn"]), pa=_lanes_8_to_15(p["a_log"]), pdt=_lanes_8_to_15(p["dt_bias"]))


def layer_grads(g):
    return dict(
        w_in=jnp.concatenate([g["wqkv"], g["wz"], g["wba"][:, :2 * HEADS], g["wpl"], g["wg"]], axis=1),
        conv_qkv=g["conv_qkv"][:4], a_log=g["pa"][0, HEADS:2 * HEADS], dt_bias=g["pdt"][0, HEADS:2 * HEADS],
        head_norm=g["head_norm"][0], w_pool=g["w_pool"], pool_scale=g["pool_scale"][0], w_out=g["wout"],
        norm_mix=g["norm_mix"][0], norm_ffn=g["norm_ffn"][0],
        w_up=jnp.concatenate([g["wupg"], g["wupv"]], axis=1),
        conv_ffn=jnp.concatenate([g["conv_g"][:3], g["conv_v"][:3]], axis=1), w_down=g["wdown"])


def big_grad_shards(g):
    in_shards = [_cat([g[name][:, off:off + width] for name, off, width in
                       _overlaps(s * IN_SHARD, (s + 1) * IN_SHARD, IN_PIECES)]) for s in range(4)]
    half = D_FF // 2
    up_shards = [g["wupg"][:, :half], g["wupg"][:, half:], g["wupv"][:, :half], g["wupv"][:, half:]]
    return dict(w_in=jnp.stack(in_shards), w_up=jnp.stack(up_shards),
                w_down=g["wdown"].reshape(4, D_FF // 4, D_MODEL), w_out=g["wout"].reshape(4, D_MODEL // 4, D_MODEL))


LAYER_PARAMS = ("norm_mix", "w_in", "conv_qkv", "a_log", "dt_bias", "head_norm", "w_pool", "pool_scale", "w_out",
                "norm_ffn", "w_up", "conv_ffn", "w_down")


def pad_rows(meta, x):
    return jnp.concatenate([jnp.zeros((LEAD, D_MODEL), F32), meta.astype(F32), x.astype(F32),
                            jnp.zeros((TAIL, D_MODEL), F32)], axis=0)


MESH = pl.DeviceIdType.MESH
ANY = pl.BlockSpec(memory_space=pl.ANY)


def _place():
    x, y, c = lax.axis_index("x"), lax.axis_index("y"), lax.axis_index("c")
    return x, y, c, [(1 - x, y), (x, 1 - y), (1 - x, 1 - y)]


def _my_chip():
    return 2 * lax.axis_index("x") + lax.axis_index("y")


def gather_shards(packs):
    n = len(packs)

    def body(*refs):
        p_refs, o_refs, (send_sems, recv_sems) = refs[:n], refs[n:2 * n], refs[2 * n:]
        x, y, c, chips = _place()

        def copy(a, k, chip, half, to, src=None):
            dst = o_refs[a].at[2 * chip[0] + chip[1], half]
            return pltpu.make_async_remote_copy(src_ref=dst if src is None else src, dst_ref=dst,
                                                send_sem=send_sems.at[6 * a + k], recv_sem=recv_sems.at[6 * a + k],
                                                device_id=to, device_id_type=MESH)

        first = [copy(a, j, (x, y), c, (*chip, c), src=p_refs[a].at[c]) for a in range(n) for j, chip in enumerate(chips)]
        for cp in first:
            cp.start()
        passed = []
        for a in range(n):
            for j, chip in enumerate(chips):
                copy(a, j, chip, c, (x, y, c)).wait_recv()
                passed.append(copy(a, 3 + j, chip, c, (x, y, 1 - c)))
                passed[-1].start()
        for a in range(n):
            for j, chip in enumerate(chips):
                copy(a, 3 + j, chip, 1 - c, (x, y, c)).wait_recv()
        for cp in first + passed:
            cp.wait_send()

    gathered = pl.pallas_call(
        body, in_specs=[ANY] * n, out_specs=[ANY] * n,
        out_shape=[jax.ShapeDtypeStruct((4,) + p.shape, p.dtype) for p in packs],
        scratch_shapes=[pltpu.SemaphoreType.DMA((6 * n,)), pltpu.SemaphoreType.DMA((6 * n,))],
        name="gather_shards",
    )(*packs)
    me = _my_chip()
    return [lax.dynamic_update_slice(g, p[None], (me,) + (0,) * p.ndim) for g, p in zip(gathered, packs)]


def swap_other_halves(ps):
    n = len(ps)

    def body(*refs):
        p_refs, o_refs, (send_sems, recv_sems) = refs[:n], refs[n:2 * n], refs[2 * n:]
        x, y, c, _ = _place()
        copies = [pltpu.make_async_remote_copy(src_ref=p_refs[a].at[s, 1 - c], dst_ref=o_refs[a].at[s],
                                               send_sem=send_sems.at[4 * a + s], recv_sem=recv_sems.at[4 * a + s],
                                               device_id=(x, y, 1 - c), device_id_type=MESH)
                  for a in range(n) for s in range(4)]
        for cp in copies:
            cp.start()
        for cp in copies:
            cp.wait()

    return pl.pallas_call(
        body, in_specs=[ANY] * n, out_specs=[ANY] * n,
        out_shape=[jax.ShapeDtypeStruct((4,) + p.shape[2:], p.dtype) for p in ps],
        scratch_shapes=[pltpu.SemaphoreType.DMA((4 * n,)), pltpu.SemaphoreType.DMA((4 * n,))],
        name="swap_other_halves",
    )(*ps)


def scatter_to_chips(qs):
    n = len(qs)

    def body(*refs):
        q_refs, o_refs, (send_sems, recv_sems) = refs[:n], refs[n:2 * n], refs[2 * n:]
        x, y, c, chips = _place()
        me = 2 * x + y
        copies = [pltpu.make_async_remote_copy(src_ref=q_refs[a].at[2 * chip[0] + chip[1]], dst_ref=o_refs[a].at[me],
                                               send_sem=send_sems.at[3 * a + j], recv_sem=recv_sems.at[3 * a + j],
                                               device_id=(*chip, c), device_id_type=MESH)
                  for a in range(n) for j, chip in enumerate(chips)]
        for cp in copies:
            cp.start()
        for a in range(n):
            for j, chip in enumerate(chips):
                slot = o_refs[a].at[2 * chip[0] + chip[1]]
                pltpu.make_async_remote_copy(src_ref=slot, dst_ref=slot, send_sem=send_sems.at[3 * a + j],
                                             recv_sem=recv_sems.at[3 * a + j],
                                             device_id=(x, y, c), device_id_type=MESH).wait_recv()
        for cp in copies:
            cp.wait_send()

    received = pl.pallas_call(
        body, in_specs=[ANY] * n, out_specs=[ANY] * n,
        out_shape=[jax.ShapeDtypeStruct(q.shape, q.dtype) for q in qs],
        scratch_shapes=[pltpu.SemaphoreType.DMA((3 * n,)), pltpu.SemaphoreType.DMA((3 * n,))],
        name="scatter_to_chips",
    )(*qs)
    me = _my_chip()
    return [lax.dynamic_update_slice(r, lax.dynamic_slice_in_dim(q, me, 1, axis=0), (me, 0, 0))
            for r, q in zip(received, qs)]


def join_halves(boths):
    n = len(boths)

    def body(*refs):
        o_refs, (send_sems, recv_sems) = refs[n:2 * n], refs[2 * n:]
        x, y, c, _ = _place()
        copies = [pltpu.make_async_remote_copy(src_ref=o_refs[a].at[c], dst_ref=o_refs[a].at[c],
                                               send_sem=send_sems.at[a], recv_sem=recv_sems.at[a],
                                               device_id=(x, y, 1 - c), device_id_type=MESH) for a in range(n)]
        for cp in copies:
            cp.start()
        for a in range(n):
            other = o_refs[a].at[1 - c]
            pltpu.make_async_remote_copy(src_ref=other, dst_ref=other, send_sem=send_sems.at[a],
                                         recv_sem=recv_sems.at[a], device_id=(x, y, c), device_id_type=MESH).wait_recv()
        for cp in copies:
            cp.wait_send()

    return pl.pallas_call(
        body, in_specs=[ANY] * n, out_specs=[ANY] * n,
        out_shape=[jax.ShapeDtypeStruct(b.shape, b.dtype) for b in boths],
        input_output_aliases={a: a for a in range(n)},
        scratch_shapes=[pltpu.SemaphoreType.DMA((n,)), pltpu.SemaphoreType.DMA((n,))], name="join_halves",
    )(*boths)


def add_own_half(p, other, c, out_dtype, name):
    _, _, rows, lanes = p.shape
    tr = _tile(rows, max(16, 524288 // lanes), 16)

    def body(c_ref, p_ref, o_ref, out_ref):
        out_ref[...] = (p_ref[...] + o_ref[...]).astype(out_dtype)

    return pl.pallas_call(
        body,
        grid_spec=pltpu.PrefetchScalarGridSpec(
            num_scalar_prefetch=1, grid=(4, rows // tr),
            in_specs=[pl.BlockSpec((None, None, tr, lanes), lambda s, i, c_ref: (s, c_ref[0], i, 0)),
                      pl.BlockSpec((None, tr, lanes), lambda s, i, c_ref: (s, i, 0))],
            out_specs=pl.BlockSpec((None, tr, lanes), lambda s, i, c_ref: (s, i, 0))),
        out_shape=jax.ShapeDtypeStruct((4, rows, lanes), out_dtype),
        compiler_params=_params("parallel", "parallel"), name=name,
    )(c, p, other)


def sum_chips(b, c, name):
    _, rows, lanes = b.shape
    tr = _tile(rows, max(16, 524288 // lanes), 16)

    def body(c_ref, b_ref, out_ref):
        b0, b1, b2, b3 = (b_ref[k].astype(F32) for k in range(4))
        out_ref[...] = ((b0 + b1) + b2) + b3

    return pl.pallas_call(
        body,
        grid_spec=pltpu.PrefetchScalarGridSpec(
            num_scalar_prefetch=1, grid=(rows // tr,),
            in_specs=[pl.BlockSpec((4, tr, lanes), lambda i, c_ref: (0, i, 0))],
            out_specs=pl.BlockSpec((None, tr, lanes), lambda i, c_ref: (c_ref[0], i, 0))),
        out_shape=jax.ShapeDtypeStruct((2, rows, lanes), F32),
        compiler_params=_params("parallel"), name=name,
    )(c, b)


def all_reduce_to_shards(packs, wires, tags, c):
    others = swap_other_halves(packs)
    qs = [add_own_half(p, o, c, wire, f"add_own_half_{tag}") for p, o, wire, tag in zip(packs, others, wires, tags)]
    return join_halves([sum_chips(r, c, f"sum_chips_{tag}") for r, tag in zip(scatter_to_chips(qs), tags)])


def adamw(w, g, m, v, name):
    shape = w.shape
    cols = shape[-1]
    w2, g2, m2, v2 = (a.reshape(-1, cols) for a in (w, g, m, v))
    rows = w2.shape[0]
    tr = _tile(rows, max(8, 262144 // cols), 8) if rows % 8 == 0 else rows
    c1 = 1.0 - ADAM_B1 ** ADAM_STEP
    c2 = 1.0 - ADAM_B2 ** ADAM_STEP

    def body(w_ref, g_ref, m_ref, v_ref, d_ref, mo_ref, vo_ref):
        gv = g_ref[...]
        mn = ADAM_B1 * m_ref[...] + (1.0 - ADAM_B1) * gv
        vn = ADAM_B2 * v_ref[...] + (1.0 - ADAM_B2) * jnp.square(gv)
        d_ref[...] = -ADAM_LR * ((mn / c1) / (jnp.sqrt(vn / c2) + ADAM_EPS) + ADAM_WD * w_ref[...])
        mo_ref[...] = mn
        vo_ref[...] = vn

    spec = pl.BlockSpec((tr, cols), lambda i: (i, 0))
    out = jax.ShapeDtypeStruct((rows, cols), F32)
    d, mn, vn = pl.pallas_call(
        body, grid=(rows // tr,), in_specs=[spec] * 4, out_specs=(spec,) * 3, out_shape=(out,) * 3,
        compiler_params=_params("parallel"), name=name,
    )(w2, g2, m2, v2)
    return d.reshape(shape), mn.reshape(shape), vn.reshape(shape)


BIG = ("w_in", "w_up", "w_down", "w_out")
SMALL = ("w_pool", "conv_qkv", "conv_ffn", "meta_tokens")
SHARDED = BIG + SMALL
MATMUL_WEIGHTS = BIG + ("w_pool",)
REPLICATED = ("norm_mix", "a_log", "dt_bias", "head_norm", "pool_scale", "norm_ffn", "norm_final")
SHARD_AXIS = {"w_in": 2, "w_up": 2, "w_out": 1, "w_down": 1, "w_pool": 3, "conv_qkv": 2, "conv_ffn": 2, "meta_tokens": 1}


def _rows_of(a):
    return a.reshape(-1, 128)


SEGMENT_ROWS = 16


def _segment(n_rows):
    return -(-n_rows // SEGMENT_ROWS) * SEGMENT_ROWS


def _pad_segment(a):
    pad = [(0, 0)] * a.ndim
    pad[-2] = (0, _segment(a.shape[-2]) - a.shape[-2])
    return jnp.pad(a, pad)


def _unshard(stacked, axis):
    full = jnp.moveaxis(stacked, 0, axis)
    shape = list(full.shape)
    shape[axis:axis + 2] = [shape[axis] * shape[axis + 1]]
    return full.reshape(shape)


def _shard_stack(full, axis):
    shape = list(full.shape)
    shape[axis:axis + 1] = [4, shape[axis] // 4]
    return jnp.moveaxis(full.reshape(shape), axis, 0)


def pack_weights(shards):
    parts = [_rows_of(shards[k].astype(WIRE)) if k in MATMUL_WEIGHTS else
             lax.bitcast_convert_type(_rows_of(shards[k].astype(F32)), WIRE).reshape(-1, 128) for k in SMALL]
    parts = [_pad_segment(p) for p in parts]
    rows = sum(p.shape[0] for p in parts)
    if rows % (2 * SEGMENT_ROWS):
        parts.append(jnp.zeros((SEGMENT_ROWS, 128), WIRE))
        rows += SEGMENT_ROWS
    return [shards[k].astype(WIRE) for k in BIG] + [jnp.concatenate(parts, axis=0).reshape(2, rows // 2, 128)]


def unpack_weights(gathered, shard_shapes):
    out = {k: _unshard(g, SHARD_AXIS[k]) for k, g in zip(BIG, gathered)}
    flat = gathered[-1].reshape(4, -1, 128)
    at = 0
    for k in SMALL:
        shp = shard_shapes[k]
        n = 1
        for e in shp:
            n *= e
        if k in MATMUL_WEIGHTS:
            r = n // 128
            stacked = flat[:, at:at + r].reshape((4,) + tuple(shp))
        else:
            r = 2 * n // 128
            stacked = lax.bitcast_convert_type(flat[:, at:at + r].reshape(4, n // 128, 128, 2), F32).reshape((4,) + tuple(shp))
        out[k] = _unshard(stacked, SHARD_AXIS[k])
        at += _segment(r)
    return out


def pack_grads(big, full, repl):
    r = jnp.concatenate([repl[k].reshape(-1) for k in REPLICATED])
    r = jnp.pad(r, (0, -r.shape[0] % 128)).reshape(1, -1, 128)
    parts = [_shard_stack(full[k], SHARD_AXIS[k]).reshape(4, -1, 128) for k in SMALL]
    parts = [_pad_segment(p) for p in parts + [jnp.broadcast_to(r, (4,) + r.shape[1:])]]
    rows = sum(p.shape[1] for p in parts)
    if rows % (2 * SEGMENT_ROWS):
        parts.append(jnp.zeros((4, SEGMENT_ROWS, 128), F32))
        rows += SEGMENT_ROWS
    side = jnp.concatenate(parts, axis=1).reshape(4, 2, rows // 2, 128)
    return list(big) + [side]


def unpack_grads(reduced, shard_shapes, repl_shapes):
    out = dict(zip(BIG, reduced))
    side = reduced[-1].reshape(-1, 128)
    at = 0
    for k in SMALL:
        n = 1
        for e in shard_shapes[k]:
            n *= e
        out[k] = side[at:at + n // 128].reshape(shard_shapes[k])
        at += _segment(n // 128)
    r = side[at:].reshape(-1)
    at = 0
    for k in REPLICATED:
        n = 1
        for e in repl_shapes[k]:
            n *= e
        out[k] = r[at:at + n].reshape(repl_shapes[k])
        at += n
    return out


WEIGHT_ORDER = ("meta_tokens", "norm_mix", "w_in", "conv_qkv", "a_log", "dt_bias", "head_norm", "w_pool", "pool_scale",
                "w_out", "norm_ffn", "w_up", "conv_ffn", "w_down", "norm_final")


def kernel(x, meta_tokens, norm_mix, w_in, conv_qkv, a_log, dt_bias, head_norm, w_pool, pool_scale, w_out, norm_ffn, w_up, conv_ffn, w_down, norm_final, loss_target, m_meta_tokens, m_norm_mix, m_w_in, m_conv_qkv, m_a_log, m_dt_bias, m_head_norm, m_w_pool, m_pool_scale, m_w_out, m_norm_ffn, m_w_up, m_conv_ffn, m_w_down, m_norm_final, v_meta_tokens, v_norm_mix, v_w_in, v_conv_qkv, v_a_log, v_dt_bias, v_head_norm, v_w_pool, v_pool_scale, v_w_out, v_norm_ffn, v_w_up, v_conv_ffn, v_w_down, v_norm_final):
    weights = dict(meta_tokens=meta_tokens, norm_mix=norm_mix, w_in=w_in, conv_qkv=conv_qkv, a_log=a_log,
                   dt_bias=dt_bias, head_norm=head_norm, w_pool=w_pool, pool_scale=pool_scale, w_out=w_out,
                   norm_ffn=norm_ffn, w_up=w_up, conv_ffn=conv_ffn, w_down=w_down, norm_final=norm_final)
    m_in = dict(zip(WEIGHT_ORDER, (m_meta_tokens, m_norm_mix, m_w_in, m_conv_qkv, m_a_log, m_dt_bias, m_head_norm,
                                   m_w_pool, m_pool_scale, m_w_out, m_norm_ffn, m_w_up, m_conv_ffn, m_w_down, m_norm_final)))
    v_in = dict(zip(WEIGHT_ORDER, (v_meta_tokens, v_norm_mix, v_w_in, v_conv_qkv, v_a_log, v_dt_bias, v_head_norm,
                                   v_w_pool, v_pool_scale, v_w_out, v_norm_ffn, v_w_up, v_conv_ffn, v_w_down, v_norm_final)))
    shard_shapes = {k: weights[k].shape for k in SHARDED}
    repl_shapes = {k: weights[k].shape for k in REPLICATED}
    core = lax.axis_index("c").astype(jnp.int32).reshape(1)

    gathered = gather_shards(pack_weights({k: weights[k] for k in SHARDED}))
    full = unpack_weights(gathered, shard_shapes)
    shards = dict(zip(BIG, gathered))
    layers = []
    for li in range(DEPTH):
        p = {k: (full[k][li] if k in SMALL else weights[k][li]) for k in LAYER_PARAMS if k not in BIG}
        p.update(w_in=[shards["w_in"][s, li] for s in range(4)], w_up=[shards["w_up"][s, li] for s in range(4)],
                 w_down=shards["w_down"][:, li].reshape(D_FF, D_MODEL),
                 w_out=shards["w_out"][:, li].reshape(D_MODEL, D_MODEL))
        layers.append(prep_layer(p))

    h0 = pad_rows(full["meta_tokens"], x[0])
    target = pad_rows(jnp.zeros((N_META, D_MODEL), F32), loss_target[0])
    loss, dh0, grads, d_norm_final = local_step(h0, target, layers, norm_final.reshape(1, D_MODEL))
    seq = x.shape[1]
    grad_x = dh0[LEAD + N_META:LEAD + N_META + seq][None]

    per_layer = [layer_grads(g) for g in grads]
    g_all = {k: jnp.stack([pl_[k] for pl_ in per_layer]) for k in LAYER_PARAMS if k not in BIG}
    g_all["meta_tokens"] = dh0[LEAD:LEAD + N_META]
    g_all["norm_final"] = d_norm_final[0]
    big = [big_grad_shards(g) for g in grads]
    packs = pack_grads([jnp.stack([b[k] for b in big], axis=1) for k in BIG],
                       {k: g_all[k] for k in SMALL}, {k: g_all[k] for k in REPLICATED})
    reduced = all_reduce_to_shards(packs, [GRAD_WIRE] * len(BIG) + [F32], BIG + ("side",), core)
    g_mine = unpack_grads(reduced, shard_shapes, repl_shapes)

    loss_sum = lax.psum(loss[0, 0], ("x", "y", "c"))
    deltas, new_m, new_v = {}, {}, {}
    for k in WEIGHT_ORDER:
        deltas[k], new_m[k], new_v[k] = adamw(weights[k], g_mine[k], m_in[k], v_in[k], f"adamw_{k}")
    return (loss_sum, grad_x, *[g_mine[k] for k in WEIGHT_ORDER], *[deltas[k] for k in WEIGHT_ORDER],
            *[new_m[k] for k in WEIGHT_ORDER], *[new_v[k] for k in WEIGHT_ORDER])
```

```python
import functools

import jax
import jax.numpy as jnp
from jax import lax
from jax.experimental import pallas as pl
from jax.experimental.pallas import tpu as pltpu

F32 = jnp.float32
BF16 = jnp.bfloat16
WIRE = jnp.bfloat16
GRAD_WIRE = jnp.bfloat16

D_MODEL = 1024
HEADS = 8
HEAD_DIM = 128
CHUNK = 64
N_META = 16
LEAD = 48
TAIL = 64
QKV_DIM = 3072
D_FF = 2816
POOL_WIDTH = 512
POOL_WINDOWS = (2, 4, 8, 16)
DEPTH = 2
NORM_EPS = 1e-6
ADAM_LR, ADAM_B1, ADAM_B2, ADAM_EPS, ADAM_WD, ADAM_STEP = 0.001, 0.9, 0.999, 1e-08, 0.01, 10
VMEM_LIMIT_BYTES = 48 * 1024 * 1024


def _params(*sem):
    return pltpu.CompilerParams(dimension_semantics=sem if sem else None, vmem_limit_bytes=VMEM_LIMIT_BYTES)


def _tile(n, cap, mult):
    best = None
    for t in range(mult, min(n, cap) + 1, mult):
        if n % t == 0:
            best = t
    assert best is not None, (n, cap, mult)
    return best


def _silu(x):
    return x * jax.nn.sigmoid(x)


def _softplus(x):
    return jnp.maximum(x, 0.0) + jnp.log(1.0 + jnp.exp(-jnp.abs(x)))


def _split_bf16(a):
    hi = a.astype(BF16)
    return hi, (a - hi.astype(F32)).astype(BF16)


def _dg(a, b, ca, cb, hi):
    dims = (((ca,), (cb,)), ((), ()))
    if hi is True:
        return lax.dot_general(a, b, dims, precision=lax.Precision.HIGHEST, preferred_element_type=F32)
    if hi == 3:
        (ah, al), (bh, bl) = _split_bf16(a), _split_bf16(b)
        dot = lambda x, y: lax.dot_general(x, y, dims, preferred_element_type=F32)
        return dot(ah, bh) + (dot(ah, bl) + dot(al, bh))
    return lax.dot_general(a.astype(BF16), b.astype(BF16), dims, preferred_element_type=F32)


def _make_dots(hi):
    @jax.custom_vjp
    def nn(a, b):
        return _dg(a, b, 1, 0, hi)

    @jax.custom_vjp
    def nt(a, b):
        return _dg(a, b, 1, 1, hi)

    @jax.custom_vjp
    def tn(a, b):
        return _dg(a, b, 0, 0, hi)

    nn.defvjp(lambda a, b: (nn(a, b), (a, b)), lambda r, g: (nt(g, r[1]), tn(r[0], g)))
    nt.defvjp(lambda a, b: (nt(a, b), (a, b)), lambda r, g: (nn(g, r[1]), tn(g, r[0])))
    tn.defvjp(lambda a, b: (tn(a, b), (a, b)), lambda r, g: (nt(r[1], g), nn(r[0], g)))
    return nn, nt, tn


_nn, _nt, _tn = _make_dots(False)
_hnn, _hnt, _htn = _make_dots(True)


def _neumann(a):
    n = a[0].shape[0]
    eye = (lax.broadcasted_iota(jnp.int32, (n, n), 0) == lax.broadcasted_iota(jnp.int32, (n, n), 1)).astype(F32)
    hd = range(len(a))
    p = [_dg(a[h], a[h], 1, 0, 3) for h in hd]
    x = [(eye - a[h]) + p[h] - _dg(a[h], p[h], 1, 0, False) for h in hd]
    for _ in range(4):
        p = [_dg(p[h], p[h], 1, 0, False) for h in hd]
        x = [x[h] + p[h] + _dg(x[h] - eye, p[h], 1, 0, False) for h in hd]
    return tuple(x)


@jax.custom_vjp
def _inv_unit_lower(a):
    return _neumann(a)


def _inv_unit_lower_bwd(x, g):
    t = [_dg(x[h], g[h], 0, 0, 3) for h in range(len(x))]
    return (tuple(-_dg(t[h], x[h], 1, 1, 3) for h in range(len(x))),)


_inv_unit_lower.defvjp(lambda a: (_neumann(a),) * 2, _inv_unit_lower_bwd)


@jax.custom_vjp
def _kept_inverse(a, x):
    return x


_kept_inverse.defvjp(lambda a, x: (x, x),
                     lambda x, g: _inv_unit_lower_bwd(x, g) + (tuple(jnp.zeros_like(e) for e in x),))


def mm(a, b, *, tb=False, add=None, out_dtype=F32, name):
    m, kdim = a.shape
    (n, kb) = b.shape if tb else b.shape[::-1]
    assert kdim == kb, (a.shape, b.shape, tb)
    tm = _tile(m, 1408, 128) if m % 128 == 0 and m <= 4096 else _tile(m, 640, 64)
    tn = _tile(n, 1536, 128)
    tk = kdim if kdim <= 3072 else _tile(kdim, 1664 if a.dtype == b.dtype == BF16 else 640, 128)
    nk = kdim // tk
    dims = (((1,), (1 if tb else 0,)), ((), ()))

    def body(*refs):
        if add is not None:
            a_ref, b_ref, add_ref, o_ref, acc = refs
        else:
            a_ref, b_ref, o_ref, acc = refs
        k = pl.program_id(2)
        part = lax.dot_general(a_ref[...].astype(BF16), b_ref[...].astype(BF16), dims, preferred_element_type=F32)

        def finish(r):
            if add is not None:
                r = r + add_ref[...]
            o_ref[...] = r.astype(out_dtype)

        if nk == 1:
            finish(part)
        else:
            @pl.when(k == 0)
            def _():
                acc[...] = part

            @pl.when(jnp.logical_and(k > 0, k < nk - 1))
            def _():
                acc[...] += part

            @pl.when(k == nk - 1)
            def _():
                finish(acc[...] + part)

    a_spec = pl.BlockSpec((tm, tk), lambda j, i, k: (i, k))
    b_spec = pl.BlockSpec((tn, tk), lambda j, i, k: (j, k)) if tb else pl.BlockSpec((tk, tn), lambda j, i, k: (k, j))
    in_specs = [a_spec, b_spec]
    args = [a, b]
    if add is not None:
        in_specs.append(pl.BlockSpec((tm, tn), lambda j, i, k: (i, j)))
        args.append(add)
    return pl.pallas_call(
        body, grid=(n // tn, m // tm, nk), in_specs=in_specs,
        out_specs=pl.BlockSpec((tm, tn), lambda j, i, k: (i, j)),
        out_shape=jax.ShapeDtypeStruct((m, n), out_dtype),
        scratch_shapes=[pltpu.VMEM((tm, tn) if nk > 1 else (8, 128), F32)],
        compiler_params=_params("parallel", "parallel", "arbitrary"), name=name,
    )(*args)


def _rms(x, gain):
    return x * lax.rsqrt(jnp.mean(x * x, axis=-1, keepdims=True) + NORM_EPS) * gain


def rms_fwd(h, gain, name):
    t = h.shape[0]
    ts = _tile(t, 640, 128)

    def body(h_ref, g_ref, u_ref, ut_ref):
        u = _rms(h_ref[...], g_ref[...])
        u_ref[...] = u.astype(BF16)
        ut_ref[...] = u.T.astype(BF16)

    return pl.pallas_call(
        body, grid=(t // ts,),
        in_specs=[pl.BlockSpec((ts, D_MODEL), lambda i: (i, 0)), pl.BlockSpec((1, D_MODEL), lambda i: (0, 0))],
        out_specs=(pl.BlockSpec((ts, D_MODEL), lambda i: (i, 0)), pl.BlockSpec((D_MODEL, ts), lambda i: (0, i))),
        out_shape=(jax.ShapeDtypeStruct((t, D_MODEL), BF16), jax.ShapeDtypeStruct((D_MODEL, t), BF16)),
        compiler_params=_params("parallel"), name=name,
    )(h, gain)


def rms_bwd(h, gain, du, dres, name):
    t = h.shape[0]
    ts = _tile(t, 640, 64)

    def body(h_ref, g_ref, du_ref, dres_ref, dh_ref, dg_ref):
        i = pl.program_id(0)
        _, vjp = jax.vjp(_rms, h_ref[...], g_ref[...])
        dx, dg = vjp(du_ref[...])
        row = i * ts + lax.broadcasted_iota(jnp.int32, (ts, 1), 0)
        dh_ref[...] = jnp.where(row >= LEAD, dx + dres_ref[...], 0.0)

        @pl.when(i == 0)
        def _():
            dg_ref[...] = jnp.zeros_like(dg_ref)

        dg_ref[...] += dg

    row_spec = pl.BlockSpec((ts, D_MODEL), lambda i: (i, 0))
    vec_spec = pl.BlockSpec((1, D_MODEL), lambda i: (0, 0))
    return pl.pallas_call(
        body, grid=(t // ts,), in_specs=[row_spec, vec_spec, row_spec, row_spec],
        out_specs=(row_spec, vec_spec),
        out_shape=(jax.ShapeDtypeStruct((t, D_MODEL), F32), jax.ShapeDtypeStruct((1, D_MODEL), F32)),
        compiler_params=_params("arbitrary"), name=name,
    )(h, gain, du, dres)


def loss_head(h, gain, target, name):
    t = h.shape[0]
    ts = _tile(t, 640, 64)

    def body(h_ref, g_ref, t_ref, loss_ref, dh_ref, dg_ref):
        i = pl.program_id(0)
        row = i * ts + lax.broadcasted_iota(jnp.int32, (ts, 1), 0)
        keep = jnp.logical_and(row >= LEAD + N_META, row < t - TAIL)
        tgt = t_ref[...]

        def f(x, g):
            err = jnp.where(keep, _rms(x, g) - tgt, 0.0)
            per_row = jnp.mean(err * err, axis=-1, keepdims=True)
            return 0.5 * jnp.sum(per_row, axis=0, keepdims=True)

        val, vjp = jax.vjp(f, h_ref[...], g_ref[...])
        dx, dg = vjp(jnp.ones((1, 1), F32))
        dh_ref[...] = dx

        @pl.when(i == 0)
        def _():
            dg_ref[...] = jnp.zeros_like(dg_ref)
            loss_ref[...] = jnp.zeros_like(loss_ref)

        dg_ref[...] += dg
        loss_ref[...] += jnp.broadcast_to(val, (1, 128))

    row_spec = pl.BlockSpec((ts, D_MODEL), lambda i: (i, 0))
    vec_spec = pl.BlockSpec((1, D_MODEL), lambda i: (0, 0))
    return pl.pallas_call(
        body, grid=(t // ts,), in_specs=[row_spec, vec_spec, row_spec],
        out_specs=(pl.BlockSpec((1, 128), lambda i: (0, 0)), row_spec, vec_spec),
        out_shape=(jax.ShapeDtypeStruct((1, 128), F32), jax.ShapeDtypeStruct((t, D_MODEL), F32),
                   jax.ShapeDtypeStruct((1, D_MODEL), F32)),
        compiler_params=_params("arbitrary"), name=name,
    )(h, gain, target)


def conv_fwd(x, w, name):
    t, width = x.shape
    k = w.shape[0]
    ts = _tile(t, 640, 64)
    tw = _tile(width, 1536, 128)
    hb = ts // 8

    def body(x_ref, halo_ref, w_ref, o_ref, buf):
        i = pl.program_id(0)
        buf[0:8, :] = jnp.where(i > 0, halo_ref[...], 0.0)
        buf[8:, :] = x_ref[...]
        wv = w_ref[...]
        acc = buf[pl.ds(8 - (k - 1), ts), :] * wv[0:1, :]
        for j in range(1, k):
            acc = acc + buf[pl.ds(8 - (k - 1) + j, ts), :] * wv[j:j + 1, :]
        o_ref[...] = acc

    return pl.pallas_call(
        body, grid=(t // ts, width // tw),
        in_specs=[pl.BlockSpec((ts, tw), lambda i, j: (i, j)),
                  pl.BlockSpec((8, tw), lambda i, j: (jnp.maximum(i * hb - 1, 0), j)),
                  pl.BlockSpec((k, tw), lambda i, j: (0, j))],
        out_specs=pl.BlockSpec((ts, tw), lambda i, j: (i, j)),
        out_shape=jax.ShapeDtypeStruct((t, width), F32),
        scratch_shapes=[pltpu.VMEM((ts + 8, tw), F32)],
        compiler_params=_params("parallel", "parallel"), name=name,
    )(x, x, w)


def conv_bwd(x, dc, w, name):
    t, width = x.shape
    k = w.shape[0]
    ts = _tile(t, 640, 64)
    tw = _tile(width, 1536, 128)
    hb = ts // 8
    nt = t // ts

    def body(x_ref, xh_ref, dc_ref, dch_ref, w_ref, dx_ref, dw_ref, xbuf, dbuf):
        i = pl.program_id(1)
        xbuf[0:8, :] = jnp.where(i > 0, xh_ref[...], 0.0)
        xbuf[8:, :] = x_ref[...]
        d = dc_ref[...]
        dbuf[0:ts, :] = d
        dbuf[ts:, :] = jnp.where(i < nt - 1, dch_ref[...], 0.0)
        wv = w_ref[...]
        acc = dbuf[pl.ds(k - 1, ts), :] * wv[0:1, :]
        for j in range(1, k):
            acc = acc + dbuf[pl.ds(k - 1 - j, ts), :] * wv[j:j + 1, :]
        dx_ref[...] = acc.astype(BF16)

        @pl.when(i == 0)
        def _():
            dw_ref[...] = jnp.zeros_like(dw_ref)

        for j in range(k):
            dw_ref[j:j + 1, :] += jnp.sum(d * xbuf[pl.ds(8 - (k - 1) + j, ts), :], axis=0, keepdims=True)

    return pl.pallas_call(
        body, grid=(width // tw, nt),
        in_specs=[pl.BlockSpec((ts, tw), lambda j, i: (i, j)),
                  pl.BlockSpec((8, tw), lambda j, i: (jnp.maximum(i * hb - 1, 0), j)),
                  pl.BlockSpec((ts, tw), lambda j, i: (i, j)),
                  pl.BlockSpec((8, tw), lambda j, i: (jnp.minimum((i + 1) * hb, t // 8 - 1), j)),
                  pl.BlockSpec((k, tw), lambda j, i: (0, j))],
        out_specs=(pl.BlockSpec((ts, tw), lambda j, i: (i, j)), pl.BlockSpec((8, tw), lambda j, i: (0, j))),
        out_shape=(jax.ShapeDtypeStruct((t, width), BF16), jax.ShapeDtypeStruct((8, width), F32)),
        scratch_shapes=[pltpu.VMEM((ts + 8, tw), F32), pltpu.VMEM((ts + 8, tw), F32)],
        compiler_params=_params("parallel", "arbitrary"), name=name,
    )(x, x, dc, dc, w)


def _pool_count(pos, win):
    return jnp.clip(pos + 1, 1, win).astype(F32)


def poolwin_fwd(p, name):
    t = p.shape[0]
    ts = _tile(t, 640, 64)
    hb = ts // 16

    def body(p_ref, halo_ref, o_ref, buf):
        i = pl.program_id(0)
        buf[0:16, :] = jnp.where(i > 0, halo_ref[...], 0.0)
        buf[16:, :] = p_ref[...]
        pos = i * ts + lax.broadcasted_iota(jnp.int32, (ts, 1), 0) - LEAD
        for gi, win in enumerate(POOL_WINDOWS):
            cols = slice(gi * 128, (gi + 1) * 128)
            own = buf[pl.ds(16, ts), cols]
            acc = own
            for j in range(1, win):
                acc = acc + buf[pl.ds(16 - j, ts), cols]
            o_ref[:, cols] = acc / _pool_count(pos, win) - own

    return pl.pallas_call(
        body, grid=(t // ts,),
        in_specs=[pl.BlockSpec((ts, POOL_WIDTH), lambda i: (i, 0)),
                  pl.BlockSpec((16, POOL_WIDTH), lambda i: (jnp.maximum(i * hb - 1, 0), 0))],
        out_specs=pl.BlockSpec((ts, POOL_WIDTH), lambda i: (i, 0)),
        out_shape=jax.ShapeDtypeStruct((t, POOL_WIDTH), F32),
        scratch_shapes=[pltpu.VMEM((ts + 16, POOL_WIDTH), F32)],
        compiler_params=_params("parallel"), name=name,
    )(p, p)


def poolwin_bwd(dpooled, name):
    t = dpooled.shape[0]
    ts = _tile(t, 640, 64)
    hb = ts // 16
    nt = t // ts

    def body(d_ref, halo_ref, o_ref, buf):
        i = pl.program_id(0)
        buf[0:ts, :] = d_ref[...]
        buf[ts:, :] = jnp.where(i < nt - 1, halo_ref[...], 0.0)
        pos = i * ts + lax.broadcasted_iota(jnp.int32, (ts, 1), 0) - LEAD
        for gi, win in enumerate(POOL_WINDOWS):
            cols = slice(gi * 128, (gi + 1) * 128)
            own = buf[pl.ds(0, ts), cols]
            acc = own / _pool_count(pos, win)
            for j in range(1, win):
                acc = acc + buf[pl.ds(j, ts), cols] / _pool_count(pos + j, win)
            o_ref[:, cols] = (acc - own).astype(BF16)

    return pl.pallas_call(
        body, grid=(nt,),
        in_specs=[pl.BlockSpec((ts, POOL_WIDTH), lambda i: (i, 0)),
                  pl.BlockSpec((16, POOL_WIDTH), lambda i: (jnp.minimum((i + 1) * hb, t // 16 - 1), 0))],
        out_specs=pl.BlockSpec((ts, POOL_WIDTH), lambda i: (i, 0)),
        out_shape=jax.ShapeDtypeStruct((t, POOL_WIDTH), BF16),
        scratch_shapes=[pltpu.VMEM((ts + 16, POOL_WIDTH), F32)],
        compiler_params=_params("parallel"), name=name,
    )(dpooled, dpooled)


def _mix(y_a, gpre, pooled, w_pool, scale):
    parts = [_nn(pooled[:, g * 128:(g + 1) * 128], w_pool[g]) for g in range(4)]
    y_b = jnp.concatenate(parts, axis=1) * scale
    return jax.nn.sigmoid(gpre[:, :D_MODEL]) * y_a + jax.nn.sigmoid(gpre[:, D_MODEL:]) * y_b


def _mix_specs(ts):
    return [pl.BlockSpec((ts, D_MODEL), lambda i: (i, 0)), pl.BlockSpec((ts, 2 * D_MODEL), lambda i: (i, 0)),
            pl.BlockSpec((ts, POOL_WIDTH), lambda i: (i, 0)), pl.BlockSpec((4, 128, 256), lambda i: (0, 0, 0)),
            pl.BlockSpec((1, D_MODEL), lambda i: (0, 0))]


def mix_fwd(y_a, gpre, pooled, w_pool, scale, name):
    t = y_a.shape[0]
    ts = _tile(t, 640, 128)

    def body(ya_ref, g_ref, p_ref, w_ref, s_ref, o_ref, ot_ref):
        y = _mix(ya_ref[...], g_ref[...], p_ref[...], w_ref[...], s_ref[...])
        o_ref[...] = y.astype(BF16)
        ot_ref[...] = y.T.astype(BF16)

    return pl.pallas_call(
        body, grid=(t // ts,), in_specs=_mix_specs(ts),
        out_specs=(pl.BlockSpec((ts, D_MODEL), lambda i: (i, 0)), pl.BlockSpec((D_MODEL, ts), lambda i: (0, i))),
        out_shape=(jax.ShapeDtypeStruct((t, D_MODEL), BF16), jax.ShapeDtypeStruct((D_MODEL, t), BF16)),
        compiler_params=_params("parallel"), name=name,
    )(y_a, gpre, pooled, w_pool, scale)


def mix_bwd(y_a, gpre, pooled, w_pool, scale, dy, name):
    t = y_a.shape[0]
    ts = _tile(t, 320, 64)

    def body(ya_ref, g_ref, p_ref, w_ref, s_ref, dy_ref, dya_ref, dg_ref, dp_ref, dw_ref, ds_ref):
        i = pl.program_id(0)
        _, vjp = jax.vjp(_mix, ya_ref[...], g_ref[...], p_ref[...], w_ref[...], s_ref[...])
        dya, dg, dp, dw, ds = vjp(dy_ref[...])
        dya_ref[...] = dya
        dg_ref[...] = dg.astype(BF16)
        dp_ref[...] = dp

        @pl.when(i == 0)
        def _():
            dw_ref[...] = jnp.zeros_like(dw_ref)
            ds_ref[...] = jnp.zeros_like(ds_ref)

        dw_ref[...] += dw
        ds_ref[...] += ds

    specs = _mix_specs(ts)
    return pl.pallas_call(
        body, grid=(t // ts,), in_specs=specs + [specs[0]],
        out_specs=(specs[0], specs[1], specs[2], specs[3], specs[4]),
        out_shape=(jax.ShapeDtypeStruct((t, D_MODEL), F32), jax.ShapeDtypeStruct((t, 2 * D_MODEL), BF16),
                   jax.ShapeDtypeStruct((t, POOL_WIDTH), F32), jax.ShapeDtypeStruct((4, 128, 256), F32),
                   jax.ShapeDtypeStruct((1, D_MODEL), F32)),
        compiler_params=_params("arbitrary"), name=name,
    )(y_a, gpre, pooled, w_pool, scale, dy)


def _ffn_act(cg, cv):
    return _silu(cg) * cv


def ffnact_fwd(cg, cv, name):
    t, width = cg.shape
    ts = _tile(t, 640, 128)
    tw = _tile(width, 1536, 128)
    spec = pl.BlockSpec((ts, tw), lambda i, j: (i, j))

    def body(g_ref, v_ref, o_ref, ot_ref):
        act = _ffn_act(g_ref[...], v_ref[...])
        o_ref[...] = act.astype(BF16)
        ot_ref[...] = act.T.astype(BF16)

    return pl.pallas_call(
        body, grid=(t // ts, width // tw), in_specs=[spec, spec],
        out_specs=(spec, pl.BlockSpec((tw, ts), lambda i, j: (j, i))),
        out_shape=(jax.ShapeDtypeStruct((t, width), BF16), jax.ShapeDtypeStruct((width, t), BF16)),
        compiler_params=_params("parallel", "parallel"), name=name,
    )(cg, cv)


def ffnact_bwd(cg, cv, dact, name):
    t, width = cg.shape
    ts = _tile(t, 640, 64)
    tw = _tile(width, 1536, 128)
    spec = pl.BlockSpec((ts, tw), lambda i, j: (i, j))

    def body(g_ref, v_ref, d_ref, dg_ref, dv_ref):
        _, vjp = jax.vjp(_ffn_act, g_ref[...], v_ref[...])
        dg_ref[...], dv_ref[...] = vjp(d_ref[...])

    return pl.pallas_call(
        body, grid=(t // ts, width // tw), in_specs=[spec, spec, spec], out_specs=(spec, spec),
        out_shape=(jax.ShapeDtypeStruct((t, width), F32), jax.ShapeDtypeStruct((t, width), F32)),
        compiler_params=_params("parallel", "parallel"), name=name,
    )(cg, cv, dact)


def _gdn_chunk(c, z, ba, pa, pdt, hn, s, *, valid, inverse=None, with_inverse=False):
    r = lax.broadcasted_iota(jnp.int32, (CHUNK, CHUNK), 0)
    q_ = lax.broadcasted_iota(jnp.int32, (CHUNK, CHUNK), 1)
    causal = r >= q_
    strict = r > q_
    tril = causal.astype(F32)
    triu = (r <= q_).astype(F32)
    lane = lax.broadcasted_iota(jnp.int32, (CHUNK, 128), 1)

    decay_log = -jnp.exp(pa) * _softplus(ba + pdt)
    bg = jnp.where(lane < HEADS, jax.nn.sigmoid(ba), jnp.where(lane < 2 * HEADS, decay_log, 0.0))
    bg = jnp.where(valid, bg, 0.0)
    gc = _hnn(tril, bg)
    gct = _hnn(bg.T, triu)
    eg = jnp.exp(gc)
    glast = gc[CHUNK - 1:CHUNK, :]
    ekd = jnp.exp(glast - gc)
    gtot = jnp.exp(glast)

    hd = range(HEADS)
    hs = [slice(h * HEAD_DIM, (h + 1) * HEAD_DIM) for h in hd]
    gl = [slice(HEADS + h, HEADS + h + 1) for h in hd]
    q = [_silu(c[:, hs[h]]) for h in hd]
    k = [_silu(c[:, D_MODEL + h * HEAD_DIM:D_MODEL + (h + 1) * HEAD_DIM]) for h in hd]
    v = [_silu(c[:, 2 * D_MODEL + h * HEAD_DIM:2 * D_MODEL + (h + 1) * HEAD_DIM]) for h in hd]
    q = [q[h] * lax.rsqrt(jnp.sum(q[h] * q[h], axis=-1, keepdims=True) + NORM_EPS) * (HEAD_DIM ** -0.5) for h in hd]
    k = [k[h] * lax.rsqrt(jnp.sum(k[h] * k[h], axis=-1, keepdims=True) + NORM_EPS) for h in hd]
    beta = [bg[:, h:h + 1] for h in hd]
    decay = [jnp.exp(jnp.where(causal, gc[:, gl[h]] - gct[gl[h], :], -1e30)) for h in hd]
    kb = [k[h] * beta[h] for h in hd]
    a = [jnp.where(strict, _nt(kb[h], k[h]) * decay[h], 0.0) for h in hd]
    qk = [jnp.where(causal, _nt(q[h], k[h]) * decay[h], 0.0) for h in hd]
    x = _inv_unit_lower(tuple(a)) if inverse is None else _kept_inverse(tuple(a), tuple(inverse))
    u = [_nn(x[h], v[h] * beta[h]) for h in hd]
    w = [_nn(x[h], kb[h] * eg[:, gl[h]]) for h in hd]
    v_new = [u[h] - _nn(w[h], s[h]) for h in hd]
    o = [_nn(q[h] * eg[:, gl[h]], s[h]) + _nn(qk[h], v_new[h]) for h in hd]
    states = [s[h] * gtot[:, gl[h]] + _tn(k[h] * ekd[:, gl[h]], v_new[h]) for h in hd]
    o = [o[h] * lax.rsqrt(jnp.mean(o[h] * o[h], axis=-1, keepdims=True) + NORM_EPS) * hn * _silu(z[:, hs[h]])
         for h in hd]
    if with_inverse:
        return jnp.concatenate(o, axis=1), tuple(states), x
    return jnp.concatenate(o, axis=1), tuple(states)


GDN_FWD_CHUNKS = 5
GDN_BWD_CHUNKS = 2


def _chunk_valid(n, t):
    row = n * CHUNK + lax.broadcasted_iota(jnp.int32, (CHUNK, 1), 0)
    return jnp.logical_and(row >= LEAD, row < t - TAIL)


def gdn_fwd(c, z, ba, pa, pdt, hn, name):
    t = c.shape[0]
    n_chunks = t // CHUNK
    per_step = GDN_FWD_CHUNKS if n_chunks % GDN_FWD_CHUNKS == 0 else 1
    rows_per_step = per_step * CHUNK

    def body(c_ref, z_ref, ba_ref, pa_ref, pdt_ref, hn_ref, y_ref, ss_ref, inv_ref, state):
        step = pl.program_id(0)

        @pl.when(step == 0)
        def _():
            state[...] = jnp.zeros_like(state)

        s = tuple(state[h] for h in range(HEADS))
        for j in range(per_step):
            rows = pl.ds(j * CHUNK, CHUNK)
            for h in range(HEADS):
                ss_ref[j, h] = s[h]
            y, s, inv = _gdn_chunk(c_ref[rows, :], z_ref[rows, :], ba_ref[rows, :], pa_ref[...], pdt_ref[...],
                                   hn_ref[...], s, valid=_chunk_valid(step * per_step + j, t), with_inverse=True)
            y_ref[rows, :] = y
            for h in range(HEADS):
                inv_ref[j, h] = inv[h]
        for h in range(HEADS):
            state[h] = s[h]

    vec = pl.BlockSpec((1, 128), lambda n: (0, 0))
    return pl.pallas_call(
        body, grid=(n_chunks // per_step,),
        in_specs=[pl.BlockSpec((rows_per_step, QKV_DIM), lambda n: (n, 0)),
                  pl.BlockSpec((rows_per_step, D_MODEL), lambda n: (n, 0)),
                  pl.BlockSpec((rows_per_step, 128), lambda n: (n, 0)), vec, vec, vec],
        out_specs=(pl.BlockSpec((rows_per_step, D_MODEL), lambda n: (n, 0)),
                   pl.BlockSpec((per_step, HEADS, HEAD_DIM, HEAD_DIM), lambda n: (n, 0, 0, 0)),
                   pl.BlockSpec((per_step, HEADS, CHUNK, CHUNK), lambda n: (n, 0, 0, 0))),
        out_shape=(jax.ShapeDtypeStruct((t, D_MODEL), F32),
                   jax.ShapeDtypeStruct((n_chunks, HEADS, HEAD_DIM, HEAD_DIM), F32),
                   jax.ShapeDtypeStruct((n_chunks, HEADS, CHUNK, CHUNK), F32)),
        scratch_shapes=[pltpu.VMEM((HEADS, HEAD_DIM, HEAD_DIM), F32)],
        compiler_params=_params("arbitrary"), name=name,
    )(c, z, ba, pa, pdt, hn)


def gdn_bwd(c, z, ba, pa, pdt, hn, starts, inverses, dy, name):
    t = c.shape[0]
    per_step = GDN_BWD_CHUNKS if (t // CHUNK) % GDN_BWD_CHUNKS == 0 else 1
    n_steps = t // CHUNK // per_step
    rows_per_step = per_step * CHUNK

    def body(c_ref, z_ref, ba_ref, pa_ref, pdt_ref, hn_ref, ss_ref, inv_ref, dy_ref,
             dc_ref, dz_ref, dba_ref, dpa_ref, dpdt_ref, dhn_ref, dstate):
        step = pl.program_id(0)

        @pl.when(step == 0)
        def _():
            dstate[...] = jnp.zeros_like(dstate)
            dpa_ref[...] = jnp.zeros_like(dpa_ref)
            dpdt_ref[...] = jnp.zeros_like(dpdt_ref)
            dhn_ref[...] = jnp.zeros_like(dhn_ref)

        ds = tuple(dstate[h] for h in range(HEADS))
        for j in reversed(range(per_step)):
            rows = pl.ds(j * CHUNK, CHUNK)
            f = functools.partial(_gdn_chunk, valid=_chunk_valid((n_steps - 1 - step) * per_step + j, t),
                                  inverse=tuple(inv_ref[j, h] for h in range(HEADS)))
            _, vjp = jax.vjp(f, c_ref[rows, :], z_ref[rows, :], ba_ref[rows, :], pa_ref[...], pdt_ref[...], hn_ref[...],
                             tuple(ss_ref[j, h] for h in range(HEADS)))
            dc, dz, dba, dpa, dpdt, dhn, ds = vjp((dy_ref[rows, :], ds))
            dc_ref[rows, :] = dc
            dz_ref[rows, :] = dz.astype(BF16)
            dba_ref[rows, :] = dba.astype(BF16)
            dpa_ref[...] += dpa
            dpdt_ref[...] += dpdt
            dhn_ref[...] += dhn
        for h in range(HEADS):
            dstate[h] = ds[h]

    def rev(width):
        return pl.BlockSpec((rows_per_step, width), lambda s: (n_steps - 1 - s, 0))

    vec = pl.BlockSpec((1, 128), lambda s: (0, 0))
    vec_shape = jax.ShapeDtypeStruct((1, 128), F32)
    return pl.pallas_call(
        body, grid=(n_steps,),
        in_specs=[rev(QKV_DIM), rev(D_MODEL), rev(128), vec, vec, vec,
                  pl.BlockSpec((per_step, HEADS, HEAD_DIM, HEAD_DIM), lambda s: (n_steps - 1 - s, 0, 0, 0)),
                  pl.BlockSpec((per_step, HEADS, CHUNK, CHUNK), lambda s: (n_steps - 1 - s, 0, 0, 0)),
                  rev(D_MODEL)],
        out_specs=(rev(QKV_DIM), rev(D_MODEL), rev(128), vec, vec, vec),
        out_shape=(jax.ShapeDtypeStruct((t, QKV_DIM), F32), jax.ShapeDtypeStruct((t, D_MODEL), BF16),
                   jax.ShapeDtypeStruct((t, 128), BF16), vec_shape, vec_shape, vec_shape),
        scratch_shapes=[pltpu.VMEM((HEADS, HEAD_DIM, HEAD_DIM), F32)],
        compiler_params=_params("arbitrary"), name=name,
    )(c, z, ba, pa, pdt, hn, starts, inverses, dy)


def _layer_fwd(h, w, tag):
    u, ut = rms_fwd(h, w["norm_mix"], f"{tag}_rms_mix")
    pq = mm(u, w["wqkv"], name=f"{tag}_mm_qkv")
    pz = mm(u, w["wz"], name=f"{tag}_mm_z")
    pg = mm(u, w["wg"], name=f"{tag}_mm_gate")
    pp = mm(u, w["wpl"], name=f"{tag}_mm_pool")
    pba = mm(u, w["wba"], name=f"{tag}_mm_ba")
    cq = conv_fwd(pq, w["conv_qkv"], f"{tag}_conv_qkv")
    ya, starts, inverses = gdn_fwd(cq, pz, pba, w["pa"], w["pdt"], w["head_norm"], f"{tag}_gdn")
    pooled = poolwin_fwd(pp, f"{tag}_poolwin")
    y, yt = mix_fwd(ya, pg, pooled, w["w_pool"], w["pool_scale"], f"{tag}_mix")
    h1 = mm(y, w["wout"], add=h, name=f"{tag}_mm_out")
    u2, u2t = rms_fwd(h1, w["norm_ffn"], f"{tag}_rms_ffn")
    hg = mm(u2, w["wupg"], name=f"{tag}_mm_upg")
    hv = mm(u2, w["wupv"], name=f"{tag}_mm_upv")
    cg = conv_fwd(hg, w["conv_g"], f"{tag}_conv_g")
    cv = conv_fwd(hv, w["conv_v"], f"{tag}_conv_v")
    act, actt = ffnact_fwd(cg, cv, f"{tag}_act")
    h2 = mm(act, w["wdown"], add=h1, name=f"{tag}_mm_down")
    saved = dict(h=h, ut=ut, pq=pq, pz=pz, pg=pg, pba=pba, cq=cq, ya=ya, starts=starts, inverses=inverses, pooled=pooled, yt=yt, h1=h1,
                 u2t=u2t, hg=hg, hv=hv, cg=cg, cv=cv, actt=actt)
    return h2, saved


def _layer_bwd(dh2, w, s, tag):
    g = {}
    dact = mm(dh2, w["wdown"], tb=True, name=f"{tag}_bmm_down_x")
    g["wdown"] = mm(s["actt"], dh2, name=f"{tag}_bmm_down_w")
    dcg, dcv = ffnact_bwd(s["cg"], s["cv"], dact, f"{tag}_act_b")
    dhg, g["conv_g"] = conv_bwd(s["hg"], dcg, w["conv_g"], f"{tag}_conv_g_b")
    dhv, g["conv_v"] = conv_bwd(s["hv"], dcv, w["conv_v"], f"{tag}_conv_v_b")
    du2 = mm(dhg, w["wupg"], tb=True, name=f"{tag}_bmm_upg_x")
    du2 = mm(dhv, w["wupv"], tb=True, add=du2, name=f"{tag}_bmm_upv_x")
    g["wupg"] = mm(s["u2t"], dhg, name=f"{tag}_bmm_upg_w")
    g["wupv"] = mm(s["u2t"], dhv, name=f"{tag}_bmm_upv_w")
    dh1, g["norm_ffn"] = rms_bwd(s["h1"], w["norm_ffn"], du2, dh2, f"{tag}_rms_ffn_b")
    dy = mm(dh1, w["wout"], tb=True, name=f"{tag}_bmm_out_x")
    g["wout"] = mm(s["yt"], dh1, name=f"{tag}_bmm_out_w")
    dya, dpg, dpooled, g["w_pool"], g["pool_scale"] = mix_bwd(
        s["ya"], s["pg"], s["pooled"], w["w_pool"], w["pool_scale"], dy, f"{tag}_mix_b")
    dpp = poolwin_bwd(dpooled, f"{tag}_poolwin_b")
    dcq, dpz, dpba, g["pa"], g["pdt"], g["head_norm"] = gdn_bwd(
        s["cq"], s["pz"], s["pba"], w["pa"], w["pdt"], w["head_norm"], s["starts"], s["inverses"], dya, f"{tag}_gdn_b")
    dpq, g["conv_qkv"] = conv_bwd(s["pq"], dcq, w["conv_qkv"], f"{tag}_conv_qkv_b")
    du = mm(dpq, w["wqkv"], tb=True, name=f"{tag}_bmm_qkv_x")
    du = mm(dpz, w["wz"], tb=True, add=du, name=f"{tag}_bmm_z_x")
    du = mm(dpg, w["wg"], tb=True, add=du, name=f"{tag}_bmm_gate_x")
    du = mm(dpp, w["wpl"], tb=True, add=du, name=f"{tag}_bmm_pool_x")
    du = mm(dpba, w["wba"], tb=True, add=du, name=f"{tag}_bmm_ba_x")
    g["wqkv"] = mm(s["ut"], dpq, name=f"{tag}_bmm_qkv_w")
    g["wz"] = mm(s["ut"], dpz, name=f"{tag}_bmm_z_w")
    g["wg"] = mm(s["ut"], dpg, name=f"{tag}_bmm_gate_w")
    g["wpl"] = mm(s["ut"], dpp, name=f"{tag}_bmm_pool_w")
    g["wba"] = mm(s["ut"], dpba, name=f"{tag}_bmm_ba_w")
    dh, g["norm_mix"] = rms_bwd(s["h"], w["norm_mix"], du, dh1, f"{tag}_rms_mix_b")
    return dh, g


def local_step(h0, target, layers, norm_final):
    h = h0
    saved = []
    for li, w in enumerate(layers):
        h, s = _layer_fwd(h, w, f"l{li}")
        saved.append(s)
    loss, dh, dnf = loss_head(h, norm_final, target, "loss_head")
    grads = [None] * len(layers)
    for li in reversed(range(len(layers))):
        dh, grads[li] = _layer_bwd(dh, layers[li], saved[li], f"l{li}")
    return loss, dh, grads, dnf


_Z0, _B0, _P0, _G0, _IN_DIM = 3072, 4096, 4112, 4624, 6672


def _lanes_8_to_15(v):
    return jnp.pad(v.reshape(1, HEADS).astype(F32), ((0, 0), (HEADS, 128 - 2 * HEADS)))


IN_PIECES = (("wqkv", 0, _Z0), ("wz", _Z0, _B0), ("wba", _B0, _P0), ("wpl", _P0, _G0), ("wg", _G0, _IN_DIM))
IN_SHARD = _IN_DIM // 4


def _overlaps(a, b, spans):
    return [(name, max(a, lo) - lo, min(b, hi) - max(a, lo)) for name, lo, hi in spans if max(a, lo) < min(b, hi)]


def _cat(parts):
    return parts[0] if len(parts) == 1 else jnp.concatenate(parts, axis=1)


def prep_layer(p):
    row = lambda v: v.reshape(1, -1).astype(F32)
    w_in, w_up = p["w_in"], p["w_up"]
    if not isinstance(w_in, (list, tuple)):
        w_in = [w_in[:, s * IN_SHARD:(s + 1) * IN_SHARD] for s in range(4)]
        w_up = [w_up[:, s * (D_FF // 2):(s + 1) * (D_FF // 2)] for s in range(4)]
    shards = [(s, s * IN_SHARD, (s + 1) * IN_SHARD) for s in range(4)]
    piece = {name: _cat([w_in[s][:, off:off + width].astype(BF16) for s, off, width in _overlaps(lo, hi, shards)])
             for name, lo, hi in IN_PIECES}
    return dict(
        wqkv=piece["wqkv"], wz=piece["wz"], wba=jnp.pad(piece["wba"], ((0, 0), (0, 128 - 2 * HEADS))),
        wpl=piece["wpl"], wg=piece["wg"], wout=p["w_out"].astype(BF16),
        wupg=_cat([w_up[0].astype(BF16), w_up[1].astype(BF16)]), wupv=_cat([w_up[2].astype(BF16), w_up[3].astype(BF16)]),
        wdown=p["w_down"].astype(BF16),
        conv_qkv=p["conv_qkv"].astype(F32), conv_g=p["conv_ffn"][:, :D_FF].astype(F32),
        conv_v=p["conv_ffn"][:, D_FF:].astype(F32), w_pool=p["w_pool"].astype(F32),
        pool_scale=row(p["pool_scale"]), head_norm=row(p["head_norm"]), norm_mix=row(p["norm_mix"]),
        norm_ffn=row(p["norm_ffn"]), pa=_lanes_8_to_15(p["a_log"]), pdt=_lanes_8_to_15(p["dt_bias"]))


def layer_grads(g):
    return dict(
        w_in=jnp.concatenate([g["wqkv"], g["wz"], g["wba"][:, :2 * HEADS], g["wpl"], g["wg"]], axis=1),
        conv_qkv=g["conv_qkv"][:4], a_log=g["pa"][0, HEADS:2 * HEADS], dt_bias=g["pdt"][0, HEADS:2 * HEADS],
        head_norm=g["head_norm"][0], w_pool=g["w_pool"], pool_scale=g["pool_scale"][0], w_out=g["wout"],
        norm_mix=g["norm_mix"][0], norm_ffn=g["norm_ffn"][0],
        w_up=jnp.concatenate([g["wupg"], g["wupv"]], axis=1),
        conv_ffn=jnp.concatenate([g["conv_g"][:3], g["conv_v"][:3]], axis=1), w_down=g["wdown"])


def big_grad_shards(g):
    in_shards = [_cat([g[name][:, off:off + width] for name, off, width in
                       _overlaps(s * IN_SHARD, (s + 1) * IN_SHARD, IN_PIECES)]) for s in range(4)]
    half = D_FF // 2
    up_shards = [g["wupg"][:, :half], g["wupg"][:, half:], g["wupv"][:, :half], g["wupv"][:, half:]]
    return dict(w_in=jnp.stack(in_shards), w_up=jnp.stack(up_shards),
                w_down=g["wdown"].reshape(4, D_FF // 4, D_MODEL), w_out=g["wout"].reshape(4, D_MODEL // 4, D_MODEL))


LAYER_PARAMS = ("norm_mix", "w_in", "conv_qkv", "a_log", "dt_bias", "head_norm", "w_pool", "pool_scale", "w_out",
                "norm_ffn", "w_up", "conv_ffn", "w_down")


def pad_rows(meta, x):
    return jnp.concatenate([jnp.zeros((LEAD, D_MODEL), F32), meta.astype(F32), x.astype(F32),
                            jnp.zeros((TAIL, D_MODEL), F32)], axis=0)


MESH = pl.DeviceIdType.MESH
ANY = pl.BlockSpec(memory_space=pl.ANY)


def _place():
    x, y, c = lax.axis_index("x"), lax.axis_index("y"), lax.axis_index("c")
    return x, y, c, [(1 - x, y), (x, 1 - y), (1 - x, 1 - y)]


def _my_chip():
    return 2 * lax.axis_index("x") + lax.axis_index("y")


def gather_shards(packs):
    n = len(packs)

    def body(*refs):
        p_refs, o_refs, (send_sems, recv_sems) = refs[:n], refs[n:2 * n], refs[2 * n:]
        x, y, c, chips = _place()

        def copy(a, k, chip, half, to, src=None):
            dst = o_refs[a].at[2 * chip[0] + chip[1], half]
            return pltpu.make_async_remote_copy(src_ref=dst if src is None else src, dst_ref=dst,
                                                send_sem=send_sems.at[6 * a + k], recv_sem=recv_sems.at[6 * a + k],
                                                device_id=to, device_id_type=MESH)

        first = [copy(a, j, (x, y), c, (*chip, c), src=p_refs[a].at[c]) for a in range(n) for j, chip in enumerate(chips)]
        for cp in first:
            cp.start()
        passed = []
        for a in range(n):
            for j, chip in enumerate(chips):
                copy(a, j, chip, c, (x, y, c)).wait_recv()
                passed.append(copy(a, 3 + j, chip, c, (x, y, 1 - c)))
                passed[-1].start()
        for a in range(n):
            for j, chip in enumerate(chips):
                copy(a, 3 + j, chip, 1 - c, (x, y, c)).wait_recv()
        for cp in first + passed:
            cp.wait_send()

    gathered = pl.pallas_call(
        body, in_specs=[ANY] * n, out_specs=[ANY] * n,
        out_shape=[jax.ShapeDtypeStruct((4,) + p.shape, p.dtype) for p in packs],
        scratch_shapes=[pltpu.SemaphoreType.DMA((6 * n,)), pltpu.SemaphoreType.DMA((6 * n,))],
        name="gather_shards",
    )(*packs)
    me = _my_chip()
    return [lax.dynamic_update_slice(g, p[None], (me,) + (0,) * p.ndim) for g, p in zip(gathered, packs)]


def swap_other_halves(ps):
    n = len(ps)

    def body(*refs):
        p_refs, o_refs, (send_sems, recv_sems) = refs[:n], refs[n:2 * n], refs[2 * n:]
        x, y, c, _ = _place()
        copies = [pltpu.make_async_remote_copy(src_ref=p_refs[a].at[s, 1 - c], dst_ref=o_refs[a].at[s],
                                               send_sem=send_sems.at[4 * a + s], recv_sem=recv_sems.at[4 * a + s],
                                               device_id=(x, y, 1 - c), device_id_type=MESH)
                  for a in range(n) for s in range(4)]
        for cp in copies:
            cp.start()
        for cp in copies:
            cp.wait()

    return pl.pallas_call(
        body, in_specs=[ANY] * n, out_specs=[ANY] * n,
        out_shape=[jax.ShapeDtypeStruct((4,) + p.shape[2:], p.dtype) for p in ps],
        scratch_shapes=[pltpu.SemaphoreType.DMA((4 * n,)), pltpu.SemaphoreType.DMA((4 * n,))],
        name="swap_other_halves",
    )(*ps)


def scatter_to_chips(qs):
    n = len(qs)

    def body(*refs):
        q_refs, o_refs, (send_sems, recv_sems) = refs[:n], refs[n:2 * n], refs[2 * n:]
        x, y, c, chips = _place()
        me = 2 * x + y
        copies = [pltpu.make_async_remote_copy(src_ref=q_refs[a].at[2 * chip[0] + chip[1]], dst_ref=o_refs[a].at[me],
                                               send_sem=send_sems.at[3 * a + j], recv_sem=recv_sems.at[3 * a + j],
                                               device_id=(*chip, c), device_id_type=MESH)
                  for a in range(n) for j, chip in enumerate(chips)]
        for cp in copies:
            cp.start()
        for a in range(n):
            for j, chip in enumerate(chips):
                slot = o_refs[a].at[2 * chip[0] + chip[1]]
                pltpu.make_async_remote_copy(src_ref=slot, dst_ref=slot, send_sem=send_sems.at[3 * a + j],
                                             recv_sem=recv_sems.at[3 * a + j],
                                             device_id=(x, y, c), device_id_type=MESH).wait_recv()
        for cp in copies:
            cp.wait_send()

    received = pl.pallas_call(
        body, in_specs=[ANY] * n, out_specs=[ANY] * n,
        out_shape=[jax.ShapeDtypeStruct(q.shape, q.dtype) for q in qs],
        scratch_shapes=[pltpu.SemaphoreType.DMA((3 * n,)), pltpu.SemaphoreType.DMA((3 * n,))],
        name="scatter_to_chips",
    )(*qs)
    me = _my_chip()
    return [lax.dynamic_update_slice(r, lax.dynamic_slice_in_dim(q, me, 1, axis=0), (me, 0, 0))
            for r, q in zip(received, qs)]


def join_halves(boths):
    n = len(boths)

    def body(*refs):
        o_refs, (send_sems, recv_sems) = refs[n:2 * n], refs[2 * n:]
        x, y, c, _ = _place()
        copies = [pltpu.make_async_remote_copy(src_ref=o_refs[a].at[c], dst_ref=o_refs[a].at[c],
                                               send_sem=send_sems.at[a], recv_sem=recv_sems.at[a],
                                               device_id=(x, y, 1 - c), device_id_type=MESH) for a in range(n)]
        for cp in copies:
            cp.start()
        for a in range(n):
            other = o_refs[a].at[1 - c]
            pltpu.make_async_remote_copy(src_ref=other, dst_ref=other, send_sem=send_sems.at[a],
                                         recv_sem=recv_sems.at[a], device_id=(x, y, c), device_id_type=MESH).wait_recv()
        for cp in copies:
            cp.wait_send()

    return pl.pallas_call(
        body, in_specs=[ANY] * n, out_specs=[ANY] * n,
        out_shape=[jax.ShapeDtypeStruct(b.shape, b.dtype) for b in boths],
        input_output_aliases={a: a for a in range(n)},
        scratch_shapes=[pltpu.SemaphoreType.DMA((n,)), pltpu.SemaphoreType.DMA((n,))], name="join_halves",
    )(*boths)


def add_own_half(p, other, c, out_dtype, name):
    _, _, rows, lanes = p.shape
    tr = _tile(rows, max(16, 524288 // lanes), 16)

    def body(c_ref, p_ref, o_ref, out_ref):
        out_ref[...] = (p_ref[...] + o_ref[...]).astype(out_dtype)

    return pl.pallas_call(
        body,
        grid_spec=pltpu.PrefetchScalarGridSpec(
            num_scalar_prefetch=1, grid=(4, rows // tr),
            in_specs=[pl.BlockSpec((None, None, tr, lanes), lambda s, i, c_ref: (s, c_ref[0], i, 0)),
                      pl.BlockSpec((None, tr, lanes), lambda s, i, c_ref: (s, i, 0))],
            out_specs=pl.BlockSpec((None, tr, lanes), lambda s, i, c_ref: (s, i, 0))),
        out_shape=jax.ShapeDtypeStruct((4, rows, lanes), out_dtype),
        compiler_params=_params("parallel", "parallel"), name=name,
    )(c, p, other)


def sum_chips(b, c, name):
    _, rows, lanes = b.shape
    tr = _tile(rows, max(16, 524288 // lanes), 16)

    def body(c_ref, b_ref, out_ref):
        b0, b1, b2, b3 = (b_ref[k].astype(F32) for k in range(4))
        out_ref[...] = ((b0 + b1) + b2) + b3

    return pl.pallas_call(
        body,
        grid_spec=pltpu.PrefetchScalarGridSpec(
            num_scalar_prefetch=1, grid=(rows // tr,),
            in_specs=[pl.BlockSpec((4, tr, lanes), lambda i, c_ref: (0, i, 0))],
            out_specs=pl.BlockSpec((None, tr, lanes), lambda i, c_ref: (c_ref[0], i, 0))),
        out_shape=jax.ShapeDtypeStruct((2, rows, lanes), F32),
        compiler_params=_params("parallel"), name=name,
    )(c, b)


def all_reduce_to_shards(packs, wires, tags, c):
    others = swap_other_halves(packs)
    qs = [add_own_half(p, o, c, wire, f"add_own_half_{tag}") for p, o, wire, tag in zip(packs, others, wires, tags)]
    return join_halves([sum_chips(r, c, f"sum_chips_{tag}") for r, tag in zip(scatter_to_chips(qs), tags)])


def adamw(w, g, m, v, name):
    shape = w.shape
    cols = shape[-1]
    w2, g2, m2, v2 = (a.reshape(-1, cols) for a in (w, g, m, v))
    rows = w2.shape[0]
    tr = _tile(rows, max(8, 262144 // cols), 8) if rows % 8 == 0 else rows
    c1 = 1.0 - ADAM_B1 ** ADAM_STEP
    c2 = 1.0 - ADAM_B2 ** ADAM_STEP

    def body(w_ref, g_ref, m_ref, v_ref, d_ref, mo_ref, vo_ref):
        gv = g_ref[...]
        mn = ADAM_B1 * m_ref[...] + (1.0 - ADAM_B1) * gv
        vn = ADAM_B2 * v_ref[...] + (1.0 - ADAM_B2) * jnp.square(gv)
        d_ref[...] = -ADAM_LR * ((mn / c1) / (jnp.sqrt(vn / c2) + ADAM_EPS) + ADAM_WD * w_ref[...])
        mo_ref[...] = mn
        vo_ref[...] = vn

    spec = pl.BlockSpec((tr, cols), lambda i: (i, 0))
    out = jax.ShapeDtypeStruct((rows, cols), F32)
    d, mn, vn = pl.pallas_call(
        body, grid=(rows // tr,), in_specs=[spec] * 4, out_specs=(spec,) * 3, out_shape=(out,) * 3,
        compiler_params=_params("parallel"), name=name,
    )(w2, g2, m2, v2)
    return d.reshape(shape), mn.reshape(shape), vn.reshape(shape)


BIG = ("w_in", "w_up", "w_down", "w_out")
SMALL = ("w_pool", "conv_qkv", "conv_ffn", "meta_tokens")
SHARDED = BIG + SMALL
MATMUL_WEIGHTS = BIG + ("w_pool",)
REPLICATED = ("norm_mix", "a_log", "dt_bias", "head_norm", "pool_scale", "norm_ffn", "norm_final")
SHARD_AXIS = {"w_in": 2, "w_up": 2, "w_out": 1, "w_down": 1, "w_pool": 3, "conv_qkv": 2, "conv_ffn": 2, "meta_tokens": 1}


def _rows_of(a):
    return a.reshape(-1, 128)


SEGMENT_ROWS = 16


def _segment(n_rows):
    return -(-n_rows // SEGMENT_ROWS) * SEGMENT_ROWS


def _pad_segment(a):
    pad = [(0, 0)] * a.ndim
    pad[-2] = (0, _segment(a.shape[-2]) - a.shape[-2])
    return jnp.pad(a, pad)


def _unshard(stacked, axis):
    full = jnp.moveaxis(stacked, 0, axis)
    shape = list(full.shape)
    shape[axis:axis + 2] = [shape[axis] * shape[axis + 1]]
    return full.reshape(shape)


def _shard_stack(full, axis):
    shape = list(full.shape)
    shape[axis:axis + 1] = [4, shape[axis] // 4]
    return jnp.moveaxis(full.reshape(shape), axis, 0)


def pack_weights(shards):
    parts = [_rows_of(shards[k].astype(WIRE)) if k in MATMUL_WEIGHTS else
             lax.bitcast_convert_type(_rows_of(shards[k].astype(F32)), WIRE).reshape(-1, 128) for k in SMALL]
    parts = [_pad_segment(p) for p in parts]
    rows = sum(p.shape[0] for p in parts)
    if rows % (2 * SEGMENT_ROWS):
        parts.append(jnp.zeros((SEGMENT_ROWS, 128), WIRE))
        rows += SEGMENT_ROWS
    return [shards[k].astype(WIRE) for k in BIG] + [jnp.concatenate(parts, axis=0).reshape(2, rows // 2, 128)]


def unpack_weights(gathered, shard_shapes):
    out = {k: _unshard(g, SHARD_AXIS[k]) for k, g in zip(BIG, gathered)}
    flat = gathered[-1].reshape(4, -1, 128)
    at = 0
    for k in SMALL:
        shp = shard_shapes[k]
        n = 1
        for e in shp:
            n *= e
        if k in MATMUL_WEIGHTS:
            r = n // 128
            stacked = flat[:, at:at + r].reshape((4,) + tuple(shp))
        else:
            r = 2 * n // 128
            stacked = lax.bitcast_convert_type(flat[:, at:at + r].reshape(4, n // 128, 128, 2), F32).reshape((4,) + tuple(shp))
        out[k] = _unshard(stacked, SHARD_AXIS[k])
        at += _segment(r)
    return out


def pack_grads(big, full, repl):
    r = jnp.concatenate([repl[k].reshape(-1) for k in REPLICATED])
    r = jnp.pad(r, (0, -r.shape[0] % 128)).reshape(1, -1, 128)
    parts = [_shard_stack(full[k], SHARD_AXIS[k]).reshape(4, -1, 128) for k in SMALL]
    parts = [_pad_segment(p) for p in parts + [jnp.broadcast_to(r, (4,) + r.shape[1:])]]
    rows = sum(p.shape[1] for p in parts)
    if rows % (2 * SEGMENT_ROWS):
        parts.append(jnp.zeros((4, SEGMENT_ROWS, 128), F32))
        rows += SEGMENT_ROWS
    side = jnp.concatenate(parts, axis=1).reshape(4, 2, rows // 2, 128)
    return list(big) + [side]


def unpack_grads(reduced, shard_shapes, repl_shapes):
    out = dict(zip(BIG, reduced))
    side = reduced[-1].reshape(-1, 128)
    at = 0
    for k in SMALL:
        n = 1
        for e in shard_shapes[k]:
            n *= e
        out[k] = side[at:at + n // 128].reshape(shard_shapes[k])
        at += _segment(n // 128)
    r = side[at:].reshape(-1)
    at = 0
    for k in REPLICATED:
        n = 1
        for e in repl_shapes[k]:
            n *= e
        out[k] = r[at:at + n].reshape(repl_shapes[k])
        at += n
    return out


WEIGHT_ORDER = ("meta_tokens", "norm_mix", "w_in", "conv_qkv", "a_log", "dt_bias", "head_norm", "w_pool", "pool_scale",
                "w_out", "norm_ffn", "w_up", "conv_ffn", "w_down", "norm_final")


def kernel(x, meta_tokens, norm_mix, w_in, conv_qkv, a_log, dt_bias, head_norm, w_pool, pool_scale, w_out, norm_ffn, w_up, conv_ffn, w_down, norm_final, loss_target, m_meta_tokens, m_norm_mix, m_w_in, m_conv_qkv, m_a_log, m_dt_bias, m_head_norm, m_w_pool, m_pool_scale, m_w_out, m_norm_ffn, m_w_up, m_conv_ffn, m_w_down, m_norm_final, v_meta_tokens, v_norm_mix, v_w_in, v_conv_qkv, v_a_log, v_dt_bias, v_head_norm, v_w_pool, v_pool_scale, v_w_out, v_norm_ffn, v_w_up, v_conv_ffn, v_w_down, v_norm_final):
    weights = dict(meta_tokens=meta_tokens, norm_mix=norm_mix, w_in=w_in, conv_qkv=conv_qkv, a_log=a_log,
                   dt_bias=dt_bias, head_norm=head_norm, w_pool=w_pool, pool_scale=pool_scale, w_out=w_out,
                   norm_ffn=norm_ffn, w_up=w_up, conv_ffn=conv_ffn, w_down=w_down, norm_final=norm_final)
    m_in = dict(zip(WEIGHT_ORDER, (m_meta_tokens, m_norm_mix, m_w_in, m_conv_qkv, m_a_log, m_dt_bias, m_head_norm,
                                   m_w_pool, m_pool_scale, m_w_out, m_norm_ffn, m_w_up, m_conv_ffn, m_w_down, m_norm_final)))
    v_in = dict(zip(WEIGHT_ORDER, (v_meta_tokens, v_norm_mix, v_w_in, v_conv_qkv, v_a_log, v_dt_bias, v_head_norm,
                                   v_w_pool, v_pool_scale, v_w_out, v_norm_ffn, v_w_up, v_conv_ffn, v_w_down, v_norm_final)))
    shard_shapes = {k: weights[k].shape for k in SHARDED}
    repl_shapes = {k: weights[k].shape for k in REPLICATED}
    core = lax.axis_index("c").astype(jnp.int32).reshape(1)

    gathered = gather_shards(pack_weights({k: weights[k] for k in SHARDED}))
    full = unpack_weights(gathered, shard_shapes)
    shards = dict(zip(BIG, gathered))
    layers = []
    for li in range(DEPTH):
        p = {k: (full[k][li] if k in SMALL else weights[k][li]) for k in LAYER_PARAMS if k not in BIG}
        p.update(w_in=[shards["w_in"][s, li] for s in range(4)], w_up=[shards["w_up"][s, li] for s in range(4)],
                 w_down=shards["w_down"][:, li].reshape(D_FF, D_MODEL),
                 w_out=shards["w_out"][:, li].reshape(D_MODEL, D_MODEL))
        layers.append(prep_layer(p))

    h0 = pad_rows(full["meta_tokens"], x[0])
    target = pad_rows(jnp.zeros((N_META, D_MODEL), F32), loss_target[0])
    loss, dh0, grads, d_norm_final = local_step(h0, target, layers, norm_final.reshape(1, D_MODEL))
    seq = x.shape[1]
    grad_x = dh0[LEAD + N_META:LEAD + N_META + seq][None]

    per_layer = [layer_grads(g) for g in grads]
    g_all = {k: jnp.stack([pl_[k] for pl_ in per_layer]) for k in LAYER_PARAMS if k not in BIG}
    g_all["meta_tokens"] = dh0[LEAD:LEAD + N_META]
    g_all["norm_final"] = d_norm_final[0]
    big = [big_grad_shards(g) for g in grads]
    packs = pack_grads([jnp.stack([b[k] for b in big], axis=1) for k in BIG],
                       {k: g_all[k] for k in SMALL}, {k: g_all[k] for k in REPLICATED})
    reduced = all_reduce_to_shards(packs, [GRAD_WIRE] * len(BIG) + [F32], BIG + ("side",), core)
    g_mine = unpack_grads(reduced, shard_shapes, repl_shapes)

    loss_sum = lax.psum(loss[0, 0], ("x", "y", "c"))
    deltas, new_m, new_v = {}, {}, {}
    for k in WEIGHT_ORDER:
        deltas[k], new_m[k], new_v[k] = adamw(weights[k], g_mine[k], m_in[k], v_in[k], f"adamw_{k}")
    return (loss_sum, grad_x, *[g_mine[k] for k in WEIGHT_ORDER], *[deltas[k] for k in WEIGHT_ORDER],
            *[new_m[k] for k in WEIGHT_ORDER], *[new_v[k] for k in WEIGHT_ORDER])
```

```python
import functools

import jax
import jax.numpy as jnp
from jax import lax
from jax.experimental import pallas as pl
from jax.experimental.pallas import tpu as pltpu

F32 = jnp.float32
BF16 = jnp.bfloat16
WIRE = jnp.bfloat16
GRAD_WIRE = jnp.bfloat16

D_MODEL = 1024
HEADS = 8
HEAD_DIM = 128
CHUNK = 64
N_META = 16
LEAD = 48
TAIL = 64
QKV_DIM = 3072
D_FF = 2816
POOL_WIDTH = 512
POOL_WINDOWS = (2, 4, 8, 16)
BA_BLOCK = POOL_WIDTH // 128
DEPTH = 2
NORM_EPS = 1e-6
ADAM_LR, ADAM_B1, ADAM_B2, ADAM_EPS, ADAM_WD, ADAM_STEP = 0.001, 0.9, 0.999, 1e-08, 0.01, 10
VMEM_LIMIT_BYTES = 48 * 1024 * 1024


def _params(*sem):
    return pltpu.CompilerParams(dimension_semantics=sem if sem else None, vmem_limit_bytes=VMEM_LIMIT_BYTES)


def _tile(n, cap, mult):
    best = None
    for t in range(mult, min(n, cap) + 1, mult):
        if n % t == 0:
            best = t
    assert best is not None, (n, cap, mult)
    return best


def _silu(x):
    return x * jax.nn.sigmoid(x)


def _softplus(x):
    return jnp.maximum(x, 0.0) + jnp.log(1.0 + jnp.exp(-jnp.abs(x)))


def _split_bf16(a):
    hi = a.astype(BF16)
    return hi, (a - hi.astype(F32)).astype(BF16)


def _dg(a, b, ca, cb, hi):
    dims = (((ca,), (cb,)), ((), ()))
    if hi is True:
        return lax.dot_general(a, b, dims, precision=lax.Precision.HIGHEST, preferred_element_type=F32)
    if hi == 3:
        (ah, al), (bh, bl) = _split_bf16(a), _split_bf16(b)
        dot = lambda x, y: lax.dot_general(x, y, dims, preferred_element_type=F32)
        return dot(ah, bh) + (dot(ah, bl) + dot(al, bh))
    return lax.dot_general(a.astype(BF16), b.astype(BF16), dims, preferred_element_type=F32)


def _make_dots(hi):
    @jax.custom_vjp
    def nn(a, b):
        return _dg(a, b, 1, 0, hi)

    @jax.custom_vjp
    def nt(a, b):
        return _dg(a, b, 1, 1, hi)

    @jax.custom_vjp
    def tn(a, b):
        return _dg(a, b, 0, 0, hi)

    nn.defvjp(lambda a, b: (nn(a, b), (a, b)), lambda r, g: (nt(g, r[1]), tn(r[0], g)))
    nt.defvjp(lambda a, b: (nt(a, b), (a, b)), lambda r, g: (nn(g, r[1]), tn(g, r[0])))
    tn.defvjp(lambda a, b: (tn(a, b), (a, b)), lambda r, g: (nt(r[1], g), nn(r[0], g)))
    return nn, nt, tn


_nn, _nt, _tn = _make_dots(False)
_hnn, _hnt, _htn = _make_dots(True)


def _neumann(a):
    n = a[0].shape[0]
    eye = (lax.broadcasted_iota(jnp.int32, (n, n), 0) == lax.broadcasted_iota(jnp.int32, (n, n), 1)).astype(F32)
    hd = range(len(a))
    p = [_dg(a[h], a[h], 1, 0, 3) for h in hd]
    x = [(eye - a[h]) + p[h] - _dg(a[h], p[h], 1, 0, False) for h in hd]
    for _ in range(4):
        p = [_dg(p[h], p[h], 1, 0, False) for h in hd]
        x = [x[h] + p[h] + _dg(x[h] - eye, p[h], 1, 0, False) for h in hd]
    return tuple(x)


@jax.custom_vjp
def _inv_unit_lower(a):
    return _neumann(a)


def _inv_unit_lower_bwd(x, g):
    t = [_dg(x[h], g[h], 0, 0, 3) for h in range(len(x))]
    return (tuple(-_dg(t[h], x[h], 1, 1, 3) for h in range(len(x))),)


_inv_unit_lower.defvjp(lambda a: (_neumann(a),) * 2, _inv_unit_lower_bwd)


@jax.custom_vjp
def _kept_inverse(a, x):
    return x


_kept_inverse.defvjp(lambda a, x: (x, x),
                     lambda x, g: _inv_unit_lower_bwd(x, g) + (tuple(jnp.zeros_like(e) for e in x),))


def mm(a, b, *, tb=False, add=None, out_dtype=F32, name):
    m, kdim = a.shape
    (n, kb) = b.shape if tb else b.shape[::-1]
    assert kdim == kb, (a.shape, b.shape, tb)
    tm = _tile(m, 1408, 128) if m % 128 == 0 and m <= 4096 else _tile(m, 640, 64)
    tn = _tile(n, 3072 if (kdim <= 1024 and not tb and add is None) else 1536, 128)
    tk = kdim if kdim <= 3072 else _tile(kdim, 1664 if a.dtype == b.dtype == BF16 else 640, 128)
    nk = kdim // tk
    dims = (((1,), (1 if tb else 0,)), ((), ()))

    def body(*refs):
        if add is not None:
            a_ref, b_ref, add_ref, o_ref, acc = refs
        else:
            a_ref, b_ref, o_ref, acc = refs
        k = pl.program_id(2)
        part = lax.dot_general(a_ref[...].astype(BF16), b_ref[...].astype(BF16), dims, preferred_element_type=F32)

        def finish(r):
            if add is not None:
                r = r + add_ref[...]
            o_ref[...] = r.astype(out_dtype)

        if nk == 1:
            finish(part)
        else:
            @pl.when(k == 0)
            def _():
                acc[...] = part

            @pl.when(jnp.logical_and(k > 0, k < nk - 1))
            def _():
                acc[...] += part

            @pl.when(k == nk - 1)
            def _():
                finish(acc[...] + part)

    a_spec = pl.BlockSpec((tm, tk), lambda j, i, k: (i, k))
    b_spec = pl.BlockSpec((tn, tk), lambda j, i, k: (j, k)) if tb else pl.BlockSpec((tk, tn), lambda j, i, k: (k, j))
    in_specs = [a_spec, b_spec]
    args = [a, b]
    if add is not None:
        in_specs.append(pl.BlockSpec((tm, tn), lambda j, i, k: (i, j)))
        args.append(add)
    return pl.pallas_call(
        body, grid=(n // tn, m // tm, nk), in_specs=in_specs,
        out_specs=pl.BlockSpec((tm, tn), lambda j, i, k: (i, j)),
        out_shape=jax.ShapeDtypeStruct((m, n), out_dtype),
        scratch_shapes=[pltpu.VMEM((tm, tn) if nk > 1 else (8, 128), F32)],
        compiler_params=_params("parallel", "parallel", "arbitrary"), name=name,
    )(*args)


def _rms(x, gain):
    return x * lax.rsqrt(jnp.mean(x * x, axis=-1, keepdims=True) + NORM_EPS) * gain


def rms_fwd(h, gain, name):
    t = h.shape[0]
    ts = _tile(t, 640, 128)

    def body(h_ref, g_ref, u_ref, ut_ref):
        u = _rms(h_ref[...], g_ref[...])
        u_ref[...] = u.astype(BF16)
        ut_ref[...] = u.T.astype(BF16)

    return pl.pallas_call(
        body, grid=(t // ts,),
        in_specs=[pl.BlockSpec((ts, D_MODEL), lambda i: (i, 0)), pl.BlockSpec((1, D_MODEL), lambda i: (0, 0))],
        out_specs=(pl.BlockSpec((ts, D_MODEL), lambda i: (i, 0)), pl.BlockSpec((D_MODEL, ts), lambda i: (0, i))),
        out_shape=(jax.ShapeDtypeStruct((t, D_MODEL), BF16), jax.ShapeDtypeStruct((D_MODEL, t), BF16)),
        compiler_params=_params("parallel"), name=name,
    )(h, gain)


def rms_bwd(h, gain, du, dres, name):
    t = h.shape[0]
    ts = _tile(t, 640, 64)

    def body(h_ref, g_ref, du_ref, dres_ref, dh_ref, dg_ref):
        i = pl.program_id(0)
        _, vjp = jax.vjp(_rms, h_ref[...], g_ref[...])
        dx, dg = vjp(du_ref[...])
        row = i * ts + lax.broadcasted_iota(jnp.int32, (ts, 1), 0)
        dh_ref[...] = jnp.where(row >= LEAD, dx + dres_ref[...], 0.0)

        @pl.when(i == 0)
        def _():
            dg_ref[...] = jnp.zeros_like(dg_ref)

        dg_ref[...] += dg

    row_spec = pl.BlockSpec((ts, D_MODEL), lambda i: (i, 0))
    vec_spec = pl.BlockSpec((1, D_MODEL), lambda i: (0, 0))
    return pl.pallas_call(
        body, grid=(t // ts,), in_specs=[row_spec, vec_spec, row_spec, row_spec],
        out_specs=(row_spec, vec_spec),
        out_shape=(jax.ShapeDtypeStruct((t, D_MODEL), F32), jax.ShapeDtypeStruct((1, D_MODEL), F32)),
        compiler_params=_params("arbitrary"), name=name,
    )(h, gain, du, dres)


def loss_head(h, gain, target, name):
    t = h.shape[0]
    ts = _tile(t, 640, 64)

    def body(h_ref, g_ref, t_ref, loss_ref, dh_ref, dg_ref):
        i = pl.program_id(0)
        row = i * ts + lax.broadcasted_iota(jnp.int32, (ts, 1), 0)
        keep = jnp.logical_and(row >= LEAD + N_META, row < t - TAIL)
        tgt = t_ref[...]

        def f(x, g):
            err = jnp.where(keep, _rms(x, g) - tgt, 0.0)
            per_row = jnp.mean(err * err, axis=-1, keepdims=True)
            return 0.5 * jnp.sum(per_row, axis=0, keepdims=True)

        val, vjp = jax.vjp(f, h_ref[...], g_ref[...])
        dx, dg = vjp(jnp.ones((1, 1), F32))
        dh_ref[...] = dx

        @pl.when(i == 0)
        def _():
            dg_ref[...] = jnp.zeros_like(dg_ref)
            loss_ref[...] = jnp.zeros_like(loss_ref)

        dg_ref[...] += dg
        loss_ref[...] += jnp.broadcast_to(val, (1, 128))

    row_spec = pl.BlockSpec((ts, D_MODEL), lambda i: (i, 0))
    vec_spec = pl.BlockSpec((1, D_MODEL), lambda i: (0, 0))
    return pl.pallas_call(
        body, grid=(t // ts,), in_specs=[row_spec, vec_spec, row_spec],
        out_specs=(pl.BlockSpec((1, 128), lambda i: (0, 0)), row_spec, vec_spec),
        out_shape=(jax.ShapeDtypeStruct((1, 128), F32), jax.ShapeDtypeStruct((t, D_MODEL), F32),
                   jax.ShapeDtypeStruct((1, D_MODEL), F32)),
        compiler_params=_params("arbitrary"), name=name,
    )(h, gain, target)


def conv_fwd(x, w, name):
    t, width = x.shape
    k = w.shape[0]
    ts = _tile(t, 640, 64)
    tw = _tile(width, 1536, 128)
    hb = ts // 8

    def body(x_ref, halo_ref, w_ref, o_ref, buf):
        i = pl.program_id(0)
        buf[0:8, :] = jnp.where(i > 0, halo_ref[...], 0.0)
        buf[8:, :] = x_ref[...]
        wv = w_ref[...]
        acc = buf[pl.ds(8 - (k - 1), ts), :] * wv[0:1, :]
        for j in range(1, k):
            acc = acc + buf[pl.ds(8 - (k - 1) + j, ts), :] * wv[j:j + 1, :]
        o_ref[...] = acc

    return pl.pallas_call(
        body, grid=(t // ts, width // tw),
        in_specs=[pl.BlockSpec((ts, tw), lambda i, j: (i, j)),
                  pl.BlockSpec((8, tw), lambda i, j: (jnp.maximum(i * hb - 1, 0), j)),
                  pl.BlockSpec((k, tw), lambda i, j: (0, j))],
        out_specs=pl.BlockSpec((ts, tw), lambda i, j: (i, j)),
        out_shape=jax.ShapeDtypeStruct((t, width), F32),
        scratch_shapes=[pltpu.VMEM((ts + 8, tw), F32)],
        compiler_params=_params("parallel", "parallel"), name=name,
    )(x, x, w)


def conv_bwd(x, dc, w, name):
    t, width = x.shape
    k = w.shape[0]
    ts = _tile(t, 640, 64)
    tw = _tile(width, 1536, 128)
    hb = ts // 8
    nt = t // ts

    def body(x_ref, xh_ref, dc_ref, dch_ref, w_ref, dx_ref, dw_ref, xbuf, dbuf):
        i = pl.program_id(1)
        xbuf[0:8, :] = jnp.where(i > 0, xh_ref[...], 0.0)
        xbuf[8:, :] = x_ref[...]
        d = dc_ref[...]
        dbuf[0:ts, :] = d
        dbuf[ts:, :] = jnp.where(i < nt - 1, dch_ref[...], 0.0)
        wv = w_ref[...]
        acc = dbuf[pl.ds(k - 1, ts), :] * wv[0:1, :]
        for j in range(1, k):
            acc = acc + dbuf[pl.ds(k - 1 - j, ts), :] * wv[j:j + 1, :]
        dx_ref[...] = acc.astype(BF16)

        @pl.when(i == 0)
        def _():
            dw_ref[...] = jnp.zeros_like(dw_ref)

        for j in range(k):
            dw_ref[j:j + 1, :] += jnp.sum(d * xbuf[pl.ds(8 - (k - 1) + j, ts), :], axis=0, keepdims=True)

    return pl.pallas_call(
        body, grid=(width // tw, nt),
        in_specs=[pl.BlockSpec((ts, tw), lambda j, i: (i, j)),
                  pl.BlockSpec((8, tw), lambda j, i: (jnp.maximum(i * hb - 1, 0), j)),
                  pl.BlockSpec((ts, tw), lambda j, i: (i, j)),
                  pl.BlockSpec((8, tw), lambda j, i: (jnp.minimum((i + 1) * hb, t // 8 - 1), j)),
                  pl.BlockSpec((k, tw), lambda j, i: (0, j))],
        out_specs=(pl.BlockSpec((ts, tw), lambda j, i: (i, j)), pl.BlockSpec((8, tw), lambda j, i: (0, j))),
        out_shape=(jax.ShapeDtypeStruct((t, width), BF16), jax.ShapeDtypeStruct((8, width), F32)),
        scratch_shapes=[pltpu.VMEM((ts + 8, tw), F32), pltpu.VMEM((ts + 8, tw), F32)],
        compiler_params=_params("parallel", "arbitrary"), name=name,
    )(x, x, dc, dc, w)


def _pool_count(pos, win):
    return jnp.clip(pos + 1, 1, win).astype(F32)


def poolwin_fwd(p, name):
    t = p.shape[0]
    ts = _tile(t, 640, 64)
    hb = ts // 16

    def body(p_ref, halo_ref, o_ref, buf):
        i = pl.program_id(0)
        buf[0:16, :] = jnp.where(i > 0, halo_ref[...], 0.0)
        buf[16:, :] = p_ref[...]
        pos = i * ts + lax.broadcasted_iota(jnp.int32, (ts, 1), 0) - LEAD
        for gi, win in enumerate(POOL_WINDOWS):
            cols = slice(gi * 128, (gi + 1) * 128)
            own = buf[pl.ds(16, ts), cols]
            acc = own
            for j in range(1, win):
                acc = acc + buf[pl.ds(16 - j, ts), cols]
            o_ref[:, cols] = acc / _pool_count(pos, win) - own

    return pl.pallas_call(
        body, grid=(t // ts,),
        in_specs=[pl.BlockSpec((ts, POOL_WIDTH), lambda i: (i, 0)),
                  pl.BlockSpec((16, POOL_WIDTH), lambda i: (jnp.maximum(i * hb - 1, 0), 0))],
        out_specs=pl.BlockSpec((ts, POOL_WIDTH), lambda i: (i, 0)),
        out_shape=jax.ShapeDtypeStruct((t, POOL_WIDTH), F32),
        scratch_shapes=[pltpu.VMEM((ts + 16, POOL_WIDTH), F32)],
        compiler_params=_params("parallel"), name=name,
    )(p, p)


def poolwin_bwd(dpooled, name):
    t = dpooled.shape[0]
    ts = _tile(t, 640, 64)
    hb = ts // 16
    nt = t // ts

    def body(d_ref, halo_ref, o_ref, buf):
        i = pl.program_id(0)
        buf[0:ts, :] = d_ref[...]
        buf[ts:, :] = jnp.where(i < nt - 1, halo_ref[...], 0.0)
        pos = i * ts + lax.broadcasted_iota(jnp.int32, (ts, 1), 0) - LEAD
        for gi, win in enumerate(POOL_WINDOWS):
            cols = slice(gi * 128, (gi + 1) * 128)
            own = buf[pl.ds(0, ts), cols]
            acc = own / _pool_count(pos, win)
            for j in range(1, win):
                acc = acc + buf[pl.ds(j, ts), cols] / _pool_count(pos + j, win)
            o_ref[:, cols] = (acc - own).astype(BF16)

    return pl.pallas_call(
        body, grid=(nt,),
        in_specs=[pl.BlockSpec((ts, POOL_WIDTH), lambda i: (i, 0)),
                  pl.BlockSpec((16, POOL_WIDTH), lambda i: (jnp.minimum((i + 1) * hb, t // 16 - 1), 0))],
        out_specs=pl.BlockSpec((ts, POOL_WIDTH), lambda i: (i, 0)),
        out_shape=jax.ShapeDtypeStruct((t, POOL_WIDTH), BF16),
        scratch_shapes=[pltpu.VMEM((ts + 16, POOL_WIDTH), F32)],
        compiler_params=_params("parallel"), name=name,
    )(dpooled, dpooled)


def _mix(y_a, gpre, pooled, w_pool, scale):
    parts = [_nn(pooled[:, g * 128:(g + 1) * 128], w_pool[g]) for g in range(4)]
    y_b = jnp.concatenate(parts, axis=1) * scale
    return jax.nn.sigmoid(gpre[:, :D_MODEL]) * y_a + jax.nn.sigmoid(gpre[:, D_MODEL:]) * y_b


def _mix_specs(ts):
    return [pl.BlockSpec((ts, D_MODEL), lambda i: (i, 0)), pl.BlockSpec((ts, 2 * D_MODEL), lambda i: (i, 0)),
            pl.BlockSpec((ts, POOL_WIDTH), lambda i: (i, 0)), pl.BlockSpec((4, 128, 256), lambda i: (0, 0, 0)),
            pl.BlockSpec((1, D_MODEL), lambda i: (0, 0))]


def mix_fwd(y_a, gpre, pooled, w_pool, scale, name):
    t = y_a.shape[0]
    ts = _tile(t, 640, 128)

    def body(ya_ref, g_ref, p_ref, w_ref, s_ref, o_ref, ot_ref):
        y = _mix(ya_ref[...], g_ref[...], p_ref[...], w_ref[...], s_ref[...])
        o_ref[...] = y.astype(BF16)
        ot_ref[...] = y.T.astype(BF16)

    return pl.pallas_call(
        body, grid=(t // ts,), in_specs=_mix_specs(ts),
        out_specs=(pl.BlockSpec((ts, D_MODEL), lambda i: (i, 0)), pl.BlockSpec((D_MODEL, ts), lambda i: (0, i))),
        out_shape=(jax.ShapeDtypeStruct((t, D_MODEL), BF16), jax.ShapeDtypeStruct((D_MODEL, t), BF16)),
        compiler_params=_params("parallel"), name=name,
    )(y_a, gpre, pooled, w_pool, scale)


def mix_bwd(y_a, gpre, pooled, w_pool, scale, dy, name):
    t = y_a.shape[0]
    ts = _tile(t, 320, 64)

    def body(ya_ref, g_ref, p_ref, w_ref, s_ref, dy_ref, dya_ref, dg_ref, dp_ref, dw_ref, ds_ref):
        i = pl.program_id(0)
        _, vjp = jax.vjp(_mix, ya_ref[...], g_ref[...], p_ref[...], w_ref[...], s_ref[...])
        dya, dg, dp, dw, ds = vjp(dy_ref[...])
        dya_ref[...] = dya
        dg_ref[...] = dg.astype(BF16)
        dp_ref[...] = dp

        @pl.when(i == 0)
        def _():
            dw_ref[...] = jnp.zeros_like(dw_ref)
            ds_ref[...] = jnp.zeros_like(ds_ref)

        dw_ref[...] += dw
        ds_ref[...] += ds

    specs = _mix_specs(ts)
    return pl.pallas_call(
        body, grid=(t // ts,), in_specs=specs + [specs[0]],
        out_specs=(specs[0], specs[1], specs[2], specs[3], specs[4]),
        out_shape=(jax.ShapeDtypeStruct((t, D_MODEL), F32), jax.ShapeDtypeStruct((t, 2 * D_MODEL), BF16),
                   jax.ShapeDtypeStruct((t, POOL_WIDTH), F32), jax.ShapeDtypeStruct((4, 128, 256), F32),
                   jax.ShapeDtypeStruct((1, D_MODEL), F32)),
        compiler_params=_params("arbitrary"), name=name,
    )(y_a, gpre, pooled, w_pool, scale, dy)


def _ffn_act(cg, cv):
    return _silu(cg) * cv


def ffnact_fwd(cg, cv, name):
    t, width = cg.shape
    ts = _tile(t, 640, 128)
    tw = _tile(width, 1536, 128)
    spec = pl.BlockSpec((ts, tw), lambda i, j: (i, j))

    def body(g_ref, v_ref, o_ref, ot_ref):
        act = _ffn_act(g_ref[...], v_ref[...])
        o_ref[...] = act.astype(BF16)
        ot_ref[...] = act.T.astype(BF16)

    return pl.pallas_call(
        body, grid=(t // ts, width // tw), in_specs=[spec, spec],
        out_specs=(spec, pl.BlockSpec((tw, ts), lambda i, j: (j, i))),
        out_shape=(jax.ShapeDtypeStruct((t, width), BF16), jax.ShapeDtypeStruct((width, t), BF16)),
        compiler_params=_params("parallel", "parallel"), name=name,
    )(cg, cv)


def ffnact_bwd(cg, cv, dact, name):
    t, width = cg.shape
    ts = _tile(t, 640, 64)
    tw = _tile(width, 1536, 128)
    spec = pl.BlockSpec((ts, tw), lambda i, j: (i, j))

    def body(g_ref, v_ref, d_ref, dg_ref, dv_ref):
        _, vjp = jax.vjp(_ffn_act, g_ref[...], v_ref[...])
        dg_ref[...], dv_ref[...] = vjp(d_ref[...])

    return pl.pallas_call(
        body, grid=(t // ts, width // tw), in_specs=[spec, spec, spec], out_specs=(spec, spec),
        out_shape=(jax.ShapeDtypeStruct((t, width), F32), jax.ShapeDtypeStruct((t, width), F32)),
        compiler_params=_params("parallel", "parallel"), name=name,
    )(cg, cv, dact)


def _gdn_chunk(c, z, ba, pa, pdt, hn, s, *, valid, inverse=None, with_inverse=False):
    r = lax.broadcasted_iota(jnp.int32, (CHUNK, CHUNK), 0)
    q_ = lax.broadcasted_iota(jnp.int32, (CHUNK, CHUNK), 1)
    causal = r >= q_
    strict = r > q_
    tril = causal.astype(F32)
    triu = (r <= q_).astype(F32)
    lane = lax.broadcasted_iota(jnp.int32, (CHUNK, 128), 1)

    decay_log = -jnp.exp(pa) * _softplus(ba + pdt)
    bg = jnp.where(lane < HEADS, jax.nn.sigmoid(ba), jnp.where(lane < 2 * HEADS, decay_log, 0.0))
    bg = jnp.where(valid, bg, 0.0)
    gc = _hnn(tril, bg)
    gct = _hnn(bg.T, triu)
    eg = jnp.exp(gc)
    glast = gc[CHUNK - 1:CHUNK, :]
    ekd = jnp.exp(glast - gc)
    gtot = jnp.exp(glast)

    hd = range(HEADS)
    hs = [slice(h * HEAD_DIM, (h + 1) * HEAD_DIM) for h in hd]
    gl = [slice(HEADS + h, HEADS + h + 1) for h in hd]
    q = [_silu(c[:, hs[h]]) for h in hd]
    k = [_silu(c[:, D_MODEL + h * HEAD_DIM:D_MODEL + (h + 1) * HEAD_DIM]) for h in hd]
    v = [_silu(c[:, 2 * D_MODEL + h * HEAD_DIM:2 * D_MODEL + (h + 1) * HEAD_DIM]) for h in hd]
    q = [q[h] * lax.rsqrt(jnp.sum(q[h] * q[h], axis=-1, keepdims=True) + NORM_EPS) * (HEAD_DIM ** -0.5) for h in hd]
    k = [k[h] * lax.rsqrt(jnp.sum(k[h] * k[h], axis=-1, keepdims=True) + NORM_EPS) for h in hd]
    beta = [bg[:, h:h + 1] for h in hd]
    decay = [jnp.exp(jnp.where(causal, gc[:, gl[h]] - gct[gl[h], :], -1e30)) for h in hd]
    kb = [k[h] * beta[h] for h in hd]
    a = [jnp.where(strict, _nt(kb[h], k[h]) * decay[h], 0.0) for h in hd]
    qk = [jnp.where(causal, _nt(q[h], k[h]) * decay[h], 0.0) for h in hd]
    x = _inv_unit_lower(tuple(a)) if inverse is None else _kept_inverse(tuple(a), tuple(inverse))
    u = [_nn(x[h], v[h] * beta[h]) for h in hd]
    w = [_nn(x[h], kb[h] * eg[:, gl[h]]) for h in hd]
    v_new = [u[h] - _nn(w[h], s[h]) for h in hd]
    o = [_nn(q[h] * eg[:, gl[h]], s[h]) + _nn(qk[h], v_new[h]) for h in hd]
    states = [s[h] * gtot[:, gl[h]] + _tn(k[h] * ekd[:, gl[h]], v_new[h]) for h in hd]
    o = [o[h] * lax.rsqrt(jnp.mean(o[h] * o[h], axis=-1, keepdims=True) + NORM_EPS) * hn * _silu(z[:, hs[h]])
         for h in hd]
    if with_inverse:
        return jnp.concatenate(o, axis=1), tuple(states), x
    return jnp.concatenate(o, axis=1), tuple(states)


GDN_FWD_CHUNKS = 5
GDN_BWD_CHUNKS = 2


def _chunk_valid(n, t):
    row = n * CHUNK + lax.broadcasted_iota(jnp.int32, (CHUNK, 1), 0)
    return jnp.logical_and(row >= LEAD, row < t - TAIL)


def gdn_fwd(c, z, ba, pa, pdt, hn, name, ba_block=0):
    t = c.shape[0]
    n_chunks = t // CHUNK
    per_step = GDN_FWD_CHUNKS if n_chunks % GDN_FWD_CHUNKS == 0 else 1
    rows_per_step = per_step * CHUNK

    def body(c_ref, z_ref, ba_ref, pa_ref, pdt_ref, hn_ref, y_ref, ss_ref, inv_ref, state):
        step = pl.program_id(0)

        @pl.when(step == 0)
        def _():
            state[...] = jnp.zeros_like(state)

        s = tuple(state[h] for h in range(HEADS))
        for j in range(per_step):
            rows = pl.ds(j * CHUNK, CHUNK)
            for h in range(HEADS):
                ss_ref[j, h] = s[h]
            y, s, inv = _gdn_chunk(c_ref[rows, :], z_ref[rows, :], ba_ref[rows, :], pa_ref[...], pdt_ref[...],
                                   hn_ref[...], s, valid=_chunk_valid(step * per_step + j, t), with_inverse=True)
            y_ref[rows, :] = y
            for h in range(HEADS):
                inv_ref[j, h] = inv[h]
        for h in range(HEADS):
            state[h] = s[h]

    vec = pl.BlockSpec((1, 128), lambda n: (0, 0))
    return pl.pallas_call(
        body, grid=(n_chunks // per_step,),
        in_specs=[pl.BlockSpec((rows_per_step, QKV_DIM), lambda n: (n, 0)),
                  pl.BlockSpec((rows_per_step, D_MODEL), lambda n: (n, 0)),
                  pl.BlockSpec((rows_per_step, 128), lambda n: (n, ba_block)), vec, vec, vec],
        out_specs=(pl.BlockSpec((rows_per_step, D_MODEL), lambda n: (n, 0)),
                   pl.BlockSpec((per_step, HEADS, HEAD_DIM, HEAD_DIM), lambda n: (n, 0, 0, 0)),
                   pl.BlockSpec((per_step, HEADS, CHUNK, CHUNK), lambda n: (n, 0, 0, 0))),
        out_shape=(jax.ShapeDtypeStruct((t, D_MODEL), F32),
                   jax.ShapeDtypeStruct((n_chunks, HEADS, HEAD_DIM, HEAD_DIM), F32),
                   jax.ShapeDtypeStruct((n_chunks, HEADS, CHUNK, CHUNK), F32)),
        scratch_shapes=[pltpu.VMEM((HEADS, HEAD_DIM, HEAD_DIM), F32)],
        compiler_params=_params("arbitrary"), name=name,
    )(c, z, ba, pa, pdt, hn)


def gdn_bwd(c, z, ba, pa, pdt, hn, starts, inverses, dy, name, ba_block=0):
    t = c.shape[0]
    per_step = GDN_BWD_CHUNKS if (t // CHUNK) % GDN_BWD_CHUNKS == 0 else 1
    n_steps = t // CHUNK // per_step
    rows_per_step = per_step * CHUNK

    def body(c_ref, z_ref, ba_ref, pa_ref, pdt_ref, hn_ref, ss_ref, inv_ref, dy_ref,
             dc_ref, dz_ref, dba_ref, dpa_ref, dpdt_ref, dhn_ref, dstate):
        step = pl.program_id(0)

        @pl.when(step == 0)
        def _():
            dstate[...] = jnp.zeros_like(dstate)
            dpa_ref[...] = jnp.zeros_like(dpa_ref)
            dpdt_ref[...] = jnp.zeros_like(dpdt_ref)
            dhn_ref[...] = jnp.zeros_like(dhn_ref)

        ds = tuple(dstate[h] for h in range(HEADS))
        for j in reversed(range(per_step)):
            rows = pl.ds(j * CHUNK, CHUNK)
            f = functools.partial(_gdn_chunk, valid=_chunk_valid((n_steps - 1 - step) * per_step + j, t),
                                  inverse=tuple(inv_ref[j, h] for h in range(HEADS)))
            _, vjp = jax.vjp(f, c_ref[rows, :], z_ref[rows, :], ba_ref[rows, :], pa_ref[...], pdt_ref[...], hn_ref[...],
                             tuple(ss_ref[j, h] for h in range(HEADS)))
            dc, dz, dba, dpa, dpdt, dhn, ds = vjp((dy_ref[rows, :], ds))
            dc_ref[rows, :] = dc
            dz_ref[rows, :] = dz.astype(BF16)
            dba_ref[rows, :] = dba.astype(BF16)
            dpa_ref[...] += dpa
            dpdt_ref[...] += dpdt
            dhn_ref[...] += dhn
        for h in range(HEADS):
            dstate[h] = ds[h]

    def rev(width, block=0):
        return pl.BlockSpec((rows_per_step, width), lambda s: (n_steps - 1 - s, block))

    vec = pl.BlockSpec((1, 128), lambda s: (0, 0))
    vec_shape = jax.ShapeDtypeStruct((1, 128), F32)
    return pl.pallas_call(
        body, grid=(n_steps,),
        in_specs=[rev(QKV_DIM), rev(D_MODEL), rev(128, ba_block), vec, vec, vec,
                  pl.BlockSpec((per_step, HEADS, HEAD_DIM, HEAD_DIM), lambda s: (n_steps - 1 - s, 0, 0, 0)),
                  pl.BlockSpec((per_step, HEADS, CHUNK, CHUNK), lambda s: (n_steps - 1 - s, 0, 0, 0)),
                  rev(D_MODEL)],
        out_specs=(rev(QKV_DIM), rev(D_MODEL), rev(128), vec, vec, vec),
        out_shape=(jax.ShapeDtypeStruct((t, QKV_DIM), F32), jax.ShapeDtypeStruct((t, D_MODEL), BF16),
                   jax.ShapeDtypeStruct((t, 128), BF16), vec_shape, vec_shape, vec_shape),
        scratch_shapes=[pltpu.VMEM((HEADS, HEAD_DIM, HEAD_DIM), F32)],
        compiler_params=_params("arbitrary"), name=name,
    )(c, z, ba, pa, pdt, hn, starts, inverses, dy)


def _layer_fwd(h, w, tag):
    u, ut = rms_fwd(h, w["norm_mix"], f"{tag}_rms_mix")
    pq = mm(u, w["wqkv"], name=f"{tag}_mm_qkv")
    pz = mm(u, w["wz"], name=f"{tag}_mm_z")
    pg = mm(u, w["wg"], name=f"{tag}_mm_gate")
    pba = mm(u, w["wpb"], name=f"{tag}_mm_pool_ba")
    cq = conv_fwd(pq, w["conv_qkv"], f"{tag}_conv_qkv")
    ya, starts, inverses = gdn_fwd(cq, pz, pba, w["pa"], w["pdt"], w["head_norm"], f"{tag}_gdn", ba_block=BA_BLOCK)
    pooled = poolwin_fwd(pba, f"{tag}_poolwin")
    y, yt = mix_fwd(ya, pg, pooled, w["w_pool"], w["pool_scale"], f"{tag}_mix")
    h1 = mm(y, w["wout"], add=h, name=f"{tag}_mm_out")
    u2, u2t = rms_fwd(h1, w["norm_ffn"], f"{tag}_rms_ffn")
    hg = mm(u2, w["wupg"], name=f"{tag}_mm_upg")
    hv = mm(u2, w["wupv"], name=f"{tag}_mm_upv")
    cg = conv_fwd(hg, w["conv_g"], f"{tag}_conv_g")
    cv = conv_fwd(hv, w["conv_v"], f"{tag}_conv_v")
    act, actt = ffnact_fwd(cg, cv, f"{tag}_act")
    h2 = mm(act, w["wdown"], add=h1, name=f"{tag}_mm_down")
    saved = dict(h=h, ut=ut, pq=pq, pz=pz, pg=pg, pba=pba, cq=cq, ya=ya, starts=starts, inverses=inverses, pooled=pooled, yt=yt, h1=h1,
                 u2t=u2t, hg=hg, hv=hv, cg=cg, cv=cv, actt=actt)
    return h2, saved


def _layer_bwd(dh2, w, s, tag):
    g = {}
    dact = mm(dh2, w["wdown"], tb=True, name=f"{tag}_bmm_down_x")
    g["wdown"] = mm(s["actt"], dh2, name=f"{tag}_bmm_down_w")
    dcg, dcv = ffnact_bwd(s["cg"], s["cv"], dact, f"{tag}_act_b")
    dhg, g["conv_g"] = conv_bwd(s["hg"], dcg, w["conv_g"], f"{tag}_conv_g_b")
    dhv, g["conv_v"] = conv_bwd(s["hv"], dcv, w["conv_v"], f"{tag}_conv_v_b")
    du2 = mm(dhg, w["wupg"], tb=True, name=f"{tag}_bmm_upg_x")
    du2 = mm(dhv, w["wupv"], tb=True, add=du2, name=f"{tag}_bmm_upv_x")
    g["wupg"] = mm(s["u2t"], dhg, name=f"{tag}_bmm_upg_w")
    g["wupv"] = mm(s["u2t"], dhv, name=f"{tag}_bmm_upv_w")
    dh1, g["norm_ffn"] = rms_bwd(s["h1"], w["norm_ffn"], du2, dh2, f"{tag}_rms_ffn_b")
    dy = mm(dh1, w["wout"], tb=True, name=f"{tag}_bmm_out_x")
    g["wout"] = mm(s["yt"], dh1, name=f"{tag}_bmm_out_w")
    dya, dpg, dpooled, g["w_pool"], g["pool_scale"] = mix_bwd(
        s["ya"], s["pg"], s["pooled"], w["w_pool"], w["pool_scale"], dy, f"{tag}_mix_b")
    dpp = poolwin_bwd(dpooled, f"{tag}_poolwin_b")
    dcq, dpz, dpba, g["pa"], g["pdt"], g["head_norm"] = gdn_bwd(
        s["cq"], s["pz"], s["pba"], w["pa"], w["pdt"], w["head_norm"], s["starts"], s["inverses"], dya, f"{tag}_gdn_b",
        ba_block=BA_BLOCK)
    dpb = jnp.concatenate([dpp, dpba], axis=1)
    dpq, g["conv_qkv"] = conv_bwd(s["pq"], dcq, w["conv_qkv"], f"{tag}_conv_qkv_b")
    du = mm(dpq, w["wqkv"], tb=True, name=f"{tag}_bmm_qkv_x")
    du = mm(dpz, w["wz"], tb=True, add=du, name=f"{tag}_bmm_z_x")
    du = mm(dpg, w["wg"], tb=True, add=du, name=f"{tag}_bmm_gate_x")
    du = mm(dpb, w["wpb"], tb=True, add=du, name=f"{tag}_bmm_pool_ba_x")
    g["wqkv"] = mm(s["ut"], dpq, name=f"{tag}_bmm_qkv_w")
    g["wz"] = mm(s["ut"], dpz, name=f"{tag}_bmm_z_w")
    g["wg"] = mm(s["ut"], dpg, name=f"{tag}_bmm_gate_w")
    dwpb = mm(s["ut"], dpb, name=f"{tag}_bmm_pool_ba_w")
    g["wpl"], g["wba"] = dwpb[:, :POOL_WIDTH], dwpb[:, POOL_WIDTH:]
    dh, g["norm_mix"] = rms_bwd(s["h"], w["norm_mix"], du, dh1, f"{tag}_rms_mix_b")
    return dh, g


def local_step(h0, target, layers, norm_final):
    h = h0
    saved = []
    for li, w in enumerate(layers):
        h, s = _layer_fwd(h, w, f"l{li}")
        saved.append(s)
    loss, dh, dnf = loss_head(h, norm_final, target, "loss_head")
    grads = [None] * len(layers)
    for li in reversed(range(len(layers))):
        dh, grads[li] = _layer_bwd(dh, layers[li], saved[li], f"l{li}")
    return loss, dh, grads, dnf


_Z0, _B0, _P0, _G0, _IN_DIM = 3072, 4096, 4112, 4624, 6672


def _lanes_8_to_15(v):
    return jnp.pad(v.reshape(1, HEADS).astype(F32), ((0, 0), (HEADS, 128 - 2 * HEADS)))


IN_PIECES = (("wqkv", 0, _Z0), ("wz", _Z0, _B0), ("wba", _B0, _P0), ("wpl", _P0, _G0), ("wg", _G0, _IN_DIM))
IN_SHARD = _IN_DIM // 4


def _overlaps(a, b, spans):
    return [(name, max(a, lo) - lo, min(b, hi) - max(a, lo)) for name, lo, hi in spans if max(a, lo) < min(b, hi)]


def _cat(parts):
    return parts[0] if len(parts) == 1 else jnp.concatenate(parts, axis=1)


def prep_layer(p):
    row = lambda v: v.reshape(1, -1).astype(F32)
    w_in, w_up = p["w_in"], p["w_up"]
    if not isinstance(w_in, (list, tuple)):
        w_in = [w_in[:, s * IN_SHARD:(s + 1) * IN_SHARD] for s in range(4)]
        w_up = [w_up[:, s * (D_FF // 2):(s + 1) * (D_FF // 2)] for s in range(4)]
    shards = [(s, s * IN_SHARD, (s + 1) * IN_SHARD) for s in range(4)]
    piece = {name: _cat([w_in[s][:, off:off + width].astype(BF16) for s, off, width in _overlaps(lo, hi, shards)])
             for name, lo, hi in IN_PIECES}
    return dict(
        wqkv=piece["wqkv"], wz=piece["wz"],
        wpb=jnp.concatenate([piece["wpl"], jnp.pad(piece["wba"], ((0, 0), (0, 128 - 2 * HEADS)))], axis=1),
        wg=piece["wg"], wout=p["w_out"].astype(BF16),
        wupg=_cat([w_up[0].astype(BF16), w_up[1].astype(BF16)]), wupv=_cat([w_up[2].astype(BF16), w_up[3].astype(BF16)]),
        wdown=p["w_down"].astype(BF16),
        conv_qkv=p["conv_qkv"].astype(F32), conv_g=p["conv_ffn"][:, :D_FF].astype(F32),
        conv_v=p["conv_ffn"][:, D_FF:].astype(F32), w_pool=p["w_pool"].astype(F32),
        pool_scale=row(p["pool_scale"]), head_norm=row(p["head_norm"]), norm_mix=row(p["norm_mix"]),
        norm_ffn=row(p["norm_ffn"]), pa=_lanes_8_to_15(p["a_log"]), pdt=_lanes_8_to_15(p["dt_bias"]))


def layer_grads(g):
    return dict(
        w_in=jnp.concatenate([g["wqkv"], g["wz"], g["wba"][:, :2 * HEADS], g["wpl"], g["wg"]], axis=1),
        conv_qkv=g["conv_qkv"][:4], a_log=g["pa"][0, HEADS:2 * HEADS], dt_bias=g["pdt"][0, HEADS:2 * HEADS],
        head_norm=g["head_norm"][0], w_pool=g["w_pool"], pool_scale=g["pool_scale"][0], w_out=g["wout"],
        norm_mix=g["norm_mix"][0], norm_ffn=g["norm_ffn"][0],
        w_up=jnp.concatenate([g["wupg"], g["wupv"]], axis=1),
        conv_ffn=jnp.concatenate([g["conv_g"][:3], g["conv_v"][:3]], axis=1), w_down=g["wdown"])


def big_grad_shards(g):
    in_shards = [_cat([g[name][:, off:off + width] for name, off, width in
                       _overlaps(s * IN_SHARD, (s + 1) * IN_SHARD, IN_PIECES)]) for s in range(4)]
    half = D_FF // 2
    up_shards = [g["wupg"][:, :half], g["wupg"][:, half:], g["wupv"][:, :half], g["wupv"][:, half:]]
    return dict(w_in=jnp.stack(in_shards), w_up=jnp.stack(up_shards),
                w_down=g["wdown"].reshape(4, D_FF // 4, D_MODEL), w_out=g["wout"].reshape(4, D_MODEL // 4, D_MODEL))


LAYER_PARAMS = ("norm_mix", "w_in", "conv_qkv", "a_log", "dt_bias", "head_norm", "w_pool", "pool_scale", "w_out",
                "norm_ffn", "w_up", "conv_ffn", "w_down")


def pad_rows(meta, x):
    return jnp.concatenate([jnp.zeros((LEAD, D_MODEL), F32), meta.astype(F32), x.astype(F32),
                            jnp.zeros((TAIL, D_MODEL), F32)], axis=0)


MESH = pl.DeviceIdType.MESH
ANY = pl.BlockSpec(memory_space=pl.ANY)


def _place():
    x, y, c = lax.axis_index("x"), lax.axis_index("y"), lax.axis_index("c")
    return x, y, c, [(1 - x, y), (x, 1 - y), (1 - x, 1 - y)]


def _my_chip():
    return 2 * lax.axis_index("x") + lax.axis_index("y")


def gather_shards(packs):
    n = len(packs)

    def body(*refs):
        p_refs, o_refs, (send_sems, recv_sems) = refs[:n], refs[n:2 * n], refs[2 * n:]
        x, y, c, chips = _place()

        def copy(a, k, chip, half, to, src=None):
            dst = o_refs[a].at[2 * chip[0] + chip[1], half]
            return pltpu.make_async_remote_copy(src_ref=dst if src is None else src, dst_ref=dst,
                                                send_sem=send_sems.at[6 * a + k], recv_sem=recv_sems.at[6 * a + k],
                                                device_id=to, device_id_type=MESH)

        first = [copy(a, j, (x, y), c, (*chip, c), src=p_refs[a].at[c]) for a in range(n) for j, chip in enumerate(chips)]
        for cp in first:
            cp.start()
        passed = []
        for a in range(n):
            for j, chip in enumerate(chips):
                copy(a, j, chip, c, (x, y, c)).wait_recv()
                passed.append(copy(a, 3 + j, chip, c, (x, y, 1 - c)))
                passed[-1].start()
        for a in range(n):
            for j, chip in enumerate(chips):
                copy(a, 3 + j, chip, 1 - c, (x, y, c)).wait_recv()
        for cp in first + passed:
            cp.wait_send()

    gathered = pl.pallas_call(
        body, in_specs=[ANY] * n, out_specs=[ANY] * n,
        out_shape=[jax.ShapeDtypeStruct((4,) + p.shape, p.dtype) for p in packs],
        scratch_shapes=[pltpu.SemaphoreType.DMA((6 * n,)), pltpu.SemaphoreType.DMA((6 * n,))],
        name="gather_shards",
    )(*packs)
    me = _my_chip()
    return [lax.dynamic_update_slice(g, p[None], (me,) + (0,) * p.ndim) for g, p in zip(gathered, packs)]


def swap_other_halves(ps):
    n = len(ps)

    def body(*refs):
        p_refs, o_refs, (send_sems, recv_sems) = refs[:n], refs[n:2 * n], refs[2 * n:]
        x, y, c, _ = _place()
        copies = [pltpu.make_async_remote_copy(src_ref=p_refs[a].at[s, 1 - c], dst_ref=o_refs[a].at[s],
                                               send_sem=send_sems.at[4 * a + s], recv_sem=recv_sems.at[4 * a + s],
                                               device_id=(x, y, 1 - c), device_id_type=MESH)
                  for a in range(n) for s in range(4)]
        for cp in copies:
            cp.start()
        for cp in copies:
            cp.wait()

    return pl.pallas_call(
        body, in_specs=[ANY] * n, out_specs=[ANY] * n,
        out_shape=[jax.ShapeDtypeStruct((4,) + p.shape[2:], p.dtype) for p in ps],
        scratch_shapes=[pltpu.SemaphoreType.DMA((4 * n,)), pltpu.SemaphoreType.DMA((4 * n,))],
        name="swap_other_halves",
    )(*ps)


def scatter_to_chips(qs):
    n = len(qs)

    def body(*refs):
        q_refs, o_refs, (send_sems, recv_sems) = refs[:n], refs[n:2 * n], refs[2 * n:]
        x, y, c, chips = _place()
        me = 2 * x + y
        copies = [pltpu.make_async_remote_copy(src_ref=q_refs[a].at[2 * chip[0] + chip[1]], dst_ref=o_refs[a].at[me],
                                               send_sem=send_sems.at[3 * a + j], recv_sem=recv_sems.at[3 * a + j],
                                               device_id=(*chip, c), device_id_type=MESH)
                  for a in range(n) for j, chip in enumerate(chips)]
        for cp in copies:
            cp.start()
        for a in range(n):
            for j, chip in enumerate(chips):
                slot = o_refs[a].at[2 * chip[0] + chip[1]]
                pltpu.make_async_remote_copy(src_ref=slot, dst_ref=slot, send_sem=send_sems.at[3 * a + j],
                                             recv_sem=recv_sems.at[3 * a + j],
                                             device_id=(x, y, c), device_id_type=MESH).wait_recv()
        for cp in copies:
            cp.wait_send()

    received = pl.pallas_call(
        body, in_specs=[ANY] * n, out_specs=[ANY] * n,
        out_shape=[jax.ShapeDtypeStruct(q.shape, q.dtype) for q in qs],
        scratch_shapes=[pltpu.SemaphoreType.DMA((3 * n,)), pltpu.SemaphoreType.DMA((3 * n,))],
        name="scatter_to_chips",
    )(*qs)
    me = _my_chip()
    return [lax.dynamic_update_slice(r, lax.dynamic_slice_in_dim(q, me, 1, axis=0), (me, 0, 0))
            for r, q in zip(received, qs)]


def join_halves(boths):
    n = len(boths)

    def body(*refs):
        o_refs, (send_sems, recv_sems) = refs[n:2 * n], refs[2 * n:]
        x, y, c, _ = _place()
        copies = [pltpu.make_async_remote_copy(src_ref=o_refs[a].at[c], dst_ref=o_refs[a].at[c],
                                               send_sem=send_sems.at[a], recv_sem=recv_sems.at[a],
                                               device_id=(x, y, 1 - c), device_id_type=MESH) for a in range(n)]
        for cp in copies:
            cp.start()
        for a in range(n):
            other = o_refs[a].at[1 - c]
            pltpu.make_async_remote_copy(src_ref=other, dst_ref=other, send_sem=send_sems.at[a],
                                         recv_sem=recv_sems.at[a], device_id=(x, y, c), device_id_type=MESH).wait_recv()
        for cp in copies:
            cp.wait_send()

    return pl.pallas_call(
        body, in_specs=[ANY] * n, out_specs=[ANY] * n,
        out_shape=[jax.ShapeDtypeStruct(b.shape, b.dtype) for b in boths],
        input_output_aliases={a: a for a in range(n)},
        scratch_shapes=[pltpu.SemaphoreType.DMA((n,)), pltpu.SemaphoreType.DMA((n,))], name="join_halves",
    )(*boths)


def add_own_half(p, other, c, out_dtype, name):
    _, _, rows, lanes = p.shape
    tr = _tile(rows, max(16, 524288 // lanes), 16)

    def body(c_ref, p_ref, o_ref, out_ref):
        out_ref[...] = (p_ref[...] + o_ref[...]).astype(out_dtype)

    return pl.pallas_call(
        body,
        grid_spec=pltpu.PrefetchScalarGridSpec(
            num_scalar_prefetch=1, grid=(4, rows // tr),
            in_specs=[pl.BlockSpec((None, None, tr, lanes), lambda s, i, c_ref: (s, c_ref[0], i, 0)),
                      pl.BlockSpec((None, tr, lanes), lambda s, i, c_ref: (s, i, 0))],
            out_specs=pl.BlockSpec((None, tr, lanes), lambda s, i, c_ref: (s, i, 0))),
        out_shape=jax.ShapeDtypeStruct((4, rows, lanes), out_dtype),
        compiler_params=_params("parallel", "parallel"), name=name,
    )(c, p, other)


def sum_chips(b, c, name):
    _, rows, lanes = b.shape
    tr = _tile(rows, max(16, 524288 // lanes), 16)

    def body(c_ref, b_ref, out_ref):
        b0, b1, b2, b3 = (b_ref[k].astype(F32) for k in range(4))
        out_ref[...] = ((b0 + b1) + b2) + b3

    return pl.pallas_call(
        body,
        grid_spec=pltpu.PrefetchScalarGridSpec(
            num_scalar_prefetch=1, grid=(rows // tr,),
            in_specs=[pl.BlockSpec((4, tr, lanes), lambda i, c_ref: (0, i, 0))],
            out_specs=pl.BlockSpec((None, tr, lanes), lambda i, c_ref: (c_ref[0], i, 0))),
        out_shape=jax.ShapeDtypeStruct((2, rows, lanes), F32),
        compiler_params=_params("parallel"), name=name,
    )(c, b)


def all_reduce_to_shards(packs, wires, tags, c):
    others = swap_other_halves(packs)
    qs = [add_own_half(p, o, c, wire, f"add_own_half_{tag}") for p, o, wire, tag in zip(packs, others, wires, tags)]
    return join_halves([sum_chips(r, c, f"sum_chips_{tag}") for r, tag in zip(scatter_to_chips(qs), tags)])


def adamw(w, g, m, v, name):
    shape = w.shape
    cols = shape[-1]
    w2, g2, m2, v2 = (a.reshape(-1, cols) for a in (w, g, m, v))
    rows = w2.shape[0]
    tr = _tile(rows, max(8, 262144 // cols), 8) if rows % 8 == 0 else rows
    c1 = 1.0 - ADAM_B1 ** ADAM_STEP
    c2 = 1.0 - ADAM_B2 ** ADAM_STEP

    def body(w_ref, g_ref, m_ref, v_ref, d_ref, mo_ref, vo_ref):
        gv = g_ref[...]
        mn = ADAM_B1 * m_ref[...] + (1.0 - ADAM_B1) * gv
        vn = ADAM_B2 * v_ref[...] + (1.0 - ADAM_B2) * jnp.square(gv)
        d_ref[...] = -ADAM_LR * ((mn / c1) / (jnp.sqrt(vn / c2) + ADAM_EPS) + ADAM_WD * w_ref[...])
        mo_ref[...] = mn
        vo_ref[...] = vn

    spec = pl.BlockSpec((tr, cols), lambda i: (i, 0))
    out = jax.ShapeDtypeStruct((rows, cols), F32)
    d, mn, vn = pl.pallas_call(
        body, grid=(rows // tr,), in_specs=[spec] * 4, out_specs=(spec,) * 3, out_shape=(out,) * 3,
        compiler_params=_params("parallel"), name=name,
    )(w2, g2, m2, v2)
    return d.reshape(shape), mn.reshape(shape), vn.reshape(shape)


BIG = ("w_in", "w_up", "w_down", "w_out")
SMALL = ("w_pool", "conv_qkv", "conv_ffn", "meta_tokens")
SHARDED = BIG + SMALL
MATMUL_WEIGHTS = BIG + ("w_pool",)
REPLICATED = ("norm_mix", "a_log", "dt_bias", "head_norm", "pool_scale", "norm_ffn", "norm_final")
SHARD_AXIS = {"w_in": 2, "w_up": 2, "w_out": 1, "w_down": 1, "w_pool": 3, "conv_qkv": 2, "conv_ffn": 2, "meta_tokens": 1}


def _rows_of(a):
    return a.reshape(-1, 128)


SEGMENT_ROWS = 16


def _segment(n_rows):
    return -(-n_rows // SEGMENT_ROWS) * SEGMENT_ROWS


def _pad_segment(a):
    pad = [(0, 0)] * a.ndim
    pad[-2] = (0, _segment(a.shape[-2]) - a.shape[-2])
    return jnp.pad(a, pad)


def _unshard(stacked, axis):
    full = jnp.moveaxis(stacked, 0, axis)
    shape = list(full.shape)
    shape[axis:axis + 2] = [shape[axis] * shape[axis + 1]]
    return full.reshape(shape)


def _shard_stack(full, axis):
    shape = list(full.shape)
    shape[axis:axis + 1] = [4, shape[axis] // 4]
    return jnp.moveaxis(full.reshape(shape), axis, 0)


def pack_weights(shards):
    parts = [_rows_of(shards[k].astype(WIRE)) if k in MATMUL_WEIGHTS else
             lax.bitcast_convert_type(_rows_of(shards[k].astype(F32)), WIRE).reshape(-1, 128) for k in SMALL]
    parts = [_pad_segment(p) for p in parts]
    rows = sum(p.shape[0] for p in parts)
    if rows % (2 * SEGMENT_ROWS):
        parts.append(jnp.zeros((SEGMENT_ROWS, 128), WIRE))
        rows += SEGMENT_ROWS
    return [shards[k].astype(WIRE) for k in BIG] + [jnp.concatenate(parts, axis=0).reshape(2, rows // 2, 128)]


def unpack_weights(gathered, shard_shapes):
    out = {k: _unshard(g, SHARD_AXIS[k]) for k, g in zip(BIG, gathered)}
    flat = gathered[-1].reshape(4, -1, 128)
    at = 0
    for k in SMALL:
        shp = shard_shapes[k]
        n = 1
        for e in shp:
            n *= e
        if k in MATMUL_WEIGHTS:
            r = n // 128
            stacked = flat[:, at:at + r].reshape((4,) + tuple(shp))
        else:
            r = 2 * n // 128
            stacked = lax.bitcast_convert_type(flat[:, at:at + r].reshape(4, n // 128, 128, 2), F32).reshape((4,) + tuple(shp))
        out[k] = _unshard(stacked, SHARD_AXIS[k])
        at += _segment(r)
    return out


def pack_grads(big, full, repl):
    r = jnp.concatenate([repl[k].reshape(-1) for k in REPLICATED])
    r = jnp.pad(r, (0, -r.shape[0] % 128)).reshape(1, -1, 128)
    parts = [_shard_stack(full[k], SHARD_AXIS[k]).reshape(4, -1, 128) for k in SMALL]
    parts = [_pad_segment(p) for p in parts + [jnp.broadcast_to(r, (4,) + r.shape[1:])]]
    rows = sum(p.shape[1] for p in parts)
    if rows % (2 * SEGMENT_ROWS):
        parts.append(jnp.zeros((4, SEGMENT_ROWS, 128), F32))
        rows += SEGMENT_ROWS
    side = jnp.concatenate(parts, axis=1).reshape(4, 2, rows // 2, 128)
    return list(big) + [side]


def unpack_grads(reduced, shard_shapes, repl_shapes):
    out = dict(zip(BIG, reduced))
    side = reduced[-1].reshape(-1, 128)
    at = 0
    for k in SMALL:
        n = 1
        for e in shard_shapes[k]:
            n *= e
        out[k] = side[at:at + n // 128].reshape(shard_shapes[k])
        at += _segment(n // 128)
    r = side[at:].reshape(-1)
    at = 0
    for k in REPLICATED:
        n = 1
        for e in repl_shapes[k]:
            n *= e
        out[k] = r[at:at + n].reshape(repl_shapes[k])
        at += n
    return out


WEIGHT_ORDER = ("meta_tokens", "norm_mix", "w_in", "conv_qkv", "a_log", "dt_bias", "head_norm", "w_pool", "pool_scale",
                "w_out", "norm_ffn", "w_up", "conv_ffn", "w_down", "norm_final")


def kernel(x, meta_tokens, norm_mix, w_in, conv_qkv, a_log, dt_bias, head_norm, w_pool, pool_scale, w_out, norm_ffn, w_up, conv_ffn, w_down, norm_final, loss_target, m_meta_tokens, m_norm_mix, m_w_in, m_conv_qkv, m_a_log, m_dt_bias, m_head_norm, m_w_pool, m_pool_scale, m_w_out, m_norm_ffn, m_w_up, m_conv_ffn, m_w_down, m_norm_final, v_meta_tokens, v_norm_mix, v_w_in, v_conv_qkv, v_a_log, v_dt_bias, v_head_norm, v_w_pool, v_pool_scale, v_w_out, v_norm_ffn, v_w_up, v_conv_ffn, v_w_down, v_norm_final):
    weights = dict(meta_tokens=meta_tokens, norm_mix=norm_mix, w_in=w_in, conv_qkv=conv_qkv, a_log=a_log,
                   dt_bias=dt_bias, head_norm=head_norm, w_pool=w_pool, pool_scale=pool_scale, w_out=w_out,
                   norm_ffn=norm_ffn, w_up=w_up, conv_ffn=conv_ffn, w_down=w_down, norm_final=norm_final)
    m_in = dict(zip(WEIGHT_ORDER, (m_meta_tokens, m_norm_mix, m_w_in, m_conv_qkv, m_a_log, m_dt_bias, m_head_norm,
                                   m_w_pool, m_pool_scale, m_w_out, m_norm_ffn, m_w_up, m_conv_ffn, m_w_down, m_norm_final)))
    v_in = dict(zip(WEIGHT_ORDER, (v_meta_tokens, v_norm_mix, v_w_in, v_conv_qkv, v_a_log, v_dt_bias, v_head_norm,
                                   v_w_pool, v_pool_scale, v_w_out, v_norm_ffn, v_w_up, v_conv_ffn, v_w_down, v_norm_final)))
    shard_shapes = {k: weights[k].shape for k in SHARDED}
    repl_shapes = {k: weights[k].shape for k in REPLICATED}
    core = lax.axis_index("c").astype(jnp.int32).reshape(1)

    gathered = gather_shards(pack_weights({k: weights[k] for k in SHARDED}))
    full = unpack_weights(gathered, shard_shapes)
    shards = dict(zip(BIG, gathered))
    layers = []
    for li in range(DEPTH):
        p = {k: (full[k][li] if k in SMALL else weights[k][li]) for k in LAYER_PARAMS if k not in BIG}
        p.update(w_in=[shards["w_in"][s, li] for s in range(4)], w_up=[shards["w_up"][s, li] for s in range(4)],
                 w_down=shards["w_down"][:, li].reshape(D_FF, D_MODEL),
                 w_out=shards["w_out"][:, li].reshape(D_MODEL, D_MODEL))
        layers.append(prep_layer(p))

    h0 = pad_rows(full["meta_tokens"], x[0])
    target = pad_rows(jnp.zeros((N_META, D_MODEL), F32), loss_target[0])
    loss, dh0, grads, d_norm_final = local_step(h0, target, layers, norm_final.reshape(1, D_MODEL))
    seq = x.shape[1]
    grad_x = dh0[LEAD + N_META:LEAD + N_META + seq][None]

    per_layer = [layer_grads(g) for g in grads]
    g_all = {k: jnp.stack([pl_[k] for pl_ in per_layer]) for k in LAYER_PARAMS if k not in BIG}
    g_all["meta_tokens"] = dh0[LEAD:LEAD + N_META]
    g_all["norm_final"] = d_norm_final[0]
    big = [big_grad_shards(g) for g in grads]
    packs = pack_grads([jnp.stack([b[k] for b in big], axis=1) for k in BIG],
                       {k: g_all[k] for k in SMALL}, {k: g_all[k] for k in REPLICATED})
    reduced = all_reduce_to_shards(packs, [GRAD_WIRE] * len(BIG) + [F32], BIG + ("side",), core)
    g_mine = unpack_grads(reduced, shard_shapes, repl_shapes)

    loss_sum = lax.psum(loss[0, 0], ("x", "y", "c"))
    deltas, new_m, new_v = {}, {}, {}
    for k in WEIGHT_ORDER:
        deltas[k], new_m[k], new_v[k] = adamw(weights[k], g_mine[k], m_in[k], v_in[k], f"adamw_{k}")
    return (loss_sum, grad_x, *[g_mine[k] for k in WEIGHT_ORDER], *[deltas[k] for k in WEIGHT_ORDER],
            *[new_m[k] for k in WEIGHT_ORDER], *[new_v[k] for k in WEIGHT_ORDER])
```

```python
import functools

import jax
import jax.numpy as jnp
from jax import lax
from jax.experimental import pallas as pl
from jax.experimental.pallas import tpu as pltpu

F32 = jnp.float32
BF16 = jnp.bfloat16
WIRE = jnp.bfloat16
GRAD_WIRE = jnp.bfloat16

D_MODEL = 1024
HEADS = 8
HEAD_DIM = 128
CHUNK = 64
N_META = 16
LEAD = 48
TAIL = 64
QKV_DIM = 3072
D_FF = 2816
POOL_WIDTH = 512
POOL_WINDOWS = (2, 4, 8, 16)
BA_BLOCK = POOL_WIDTH // 128
DEPTH = 2
NORM_EPS = 1e-6
ADAM_LR, ADAM_B1, ADAM_B2, ADAM_EPS, ADAM_WD, ADAM_STEP = 0.001, 0.9, 0.999, 1e-08, 0.01, 10
VMEM_LIMIT_BYTES = 48 * 1024 * 1024


def _params(*sem):
    return pltpu.CompilerParams(dimension_semantics=sem if sem else None, vmem_limit_bytes=VMEM_LIMIT_BYTES)


def _tile(n, cap, mult):
    best = None
    for t in range(mult, min(n, cap) + 1, mult):
        if n % t == 0:
            best = t
    assert best is not None, (n, cap, mult)
    return best


def _silu(x):
    return x * jax.nn.sigmoid(x)


def _softplus(x):
    return jnp.maximum(x, 0.0) + jnp.log(1.0 + jnp.exp(-jnp.abs(x)))


def _split_bf16(a):
    hi = a.astype(BF16)
    return hi, (a - hi.astype(F32)).astype(BF16)


def _dg(a, b, ca, cb, hi):
    dims = (((ca,), (cb,)), ((), ()))
    if hi is True:
        return lax.dot_general(a, b, dims, precision=lax.Precision.HIGHEST, preferred_element_type=F32)
    if hi == 3:
        (ah, al), (bh, bl) = _split_bf16(a), _split_bf16(b)
        dot = lambda x, y: lax.dot_general(x, y, dims, preferred_element_type=F32)
        return dot(ah, bh) + (dot(ah, bl) + dot(al, bh))
    return lax.dot_general(a.astype(BF16), b.astype(BF16), dims, preferred_element_type=F32)


def _make_dots(hi):
    @jax.custom_vjp
    def nn(a, b):
        return _dg(a, b, 1, 0, hi)

    @jax.custom_vjp
    def nt(a, b):
        return _dg(a, b, 1, 1, hi)

    @jax.custom_vjp
    def tn(a, b):
        return _dg(a, b, 0, 0, hi)

    nn.defvjp(lambda a, b: (nn(a, b), (a, b)), lambda r, g: (nt(g, r[1]), tn(r[0], g)))
    nt.defvjp(lambda a, b: (nt(a, b), (a, b)), lambda r, g: (nn(g, r[1]), tn(g, r[0])))
    tn.defvjp(lambda a, b: (tn(a, b), (a, b)), lambda r, g: (nt(r[1], g), nn(r[0], g)))
    return nn, nt, tn


_nn, _nt, _tn = _make_dots(False)
_hnn, _hnt, _htn = _make_dots(True)


def _neumann(a):
    n = a[0].shape[0]
    eye = (lax.broadcasted_iota(jnp.int32, (n, n), 0) == lax.broadcasted_iota(jnp.int32, (n, n), 1)).astype(F32)
    hd = range(len(a))
    p = [_dg(a[h], a[h], 1, 0, 3) for h in hd]
    x = [(eye - a[h]) + p[h] - _dg(a[h], p[h], 1, 0, False) for h in hd]
    for _ in range(4):
        p = [_dg(p[h], p[h], 1, 0, False) for h in hd]
        x = [x[h] + p[h] + _dg(x[h] - eye, p[h], 1, 0, False) for h in hd]
    return tuple(x)


@jax.custom_vjp
def _inv_unit_lower(a):
    return _neumann(a)


def _inv_unit_lower_bwd(x, g):
    t = [_dg(x[h], g[h], 0, 0, 3) for h in range(len(x))]
    return (tuple(-_dg(t[h], x[h], 1, 1, 3) for h in range(len(x))),)


_inv_unit_lower.defvjp(lambda a: (_neumann(a),) * 2, _inv_unit_lower_bwd)


@jax.custom_vjp
def _kept_inverse(a, x):
    return x


_kept_inverse.defvjp(lambda a, x: (x, x),
                     lambda x, g: _inv_unit_lower_bwd(x, g) + (tuple(jnp.zeros_like(e) for e in x),))


def mm(a, b, *, tb=False, add=None, out_dtype=F32, name):
    m, kdim = a.shape
    (n, kb) = b.shape if tb else b.shape[::-1]
    assert kdim == kb, (a.shape, b.shape, tb)
    tm = _tile(m, 1408, 128) if m % 128 == 0 and m <= 4096 else _tile(m, 640, 64)
    tn = _tile(n, 3072 if (kdim <= 1024 and not tb and add is None) else 1536, 128)
    tk = kdim if kdim <= 3072 else _tile(kdim, 1664 if a.dtype == b.dtype == BF16 else 640, 128)
    nk = kdim // tk
    dims = (((1,), (1 if tb else 0,)), ((), ()))

    def body(*refs):
        if add is not None:
            a_ref, b_ref, add_ref, o_ref, acc = refs
        else:
            a_ref, b_ref, o_ref, acc = refs
        k = pl.program_id(2)
        part = lax.dot_general(a_ref[...].astype(BF16), b_ref[...].astype(BF16), dims, preferred_element_type=F32)

        def finish(r):
            if add is not None:
                r = r + add_ref[...]
            o_ref[...] = r.astype(out_dtype)

        if nk == 1:
            finish(part)
        else:
            @pl.when(k == 0)
            def _():
                acc[...] = part

            @pl.when(jnp.logical_and(k > 0, k < nk - 1))
            def _():
                acc[...] += part

            @pl.when(k == nk - 1)
            def _():
                finish(acc[...] + part)

    a_spec = pl.BlockSpec((tm, tk), lambda j, i, k: (i, k))
    b_spec = pl.BlockSpec((tn, tk), lambda j, i, k: (j, k)) if tb else pl.BlockSpec((tk, tn), lambda j, i, k: (k, j))
    in_specs = [a_spec, b_spec]
    args = [a, b]
    if add is not None:
        in_specs.append(pl.BlockSpec((tm, tn), lambda j, i, k: (i, j)))
        args.append(add)
    return pl.pallas_call(
        body, grid=(n // tn, m // tm, nk), in_specs=in_specs,
        out_specs=pl.BlockSpec((tm, tn), lambda j, i, k: (i, j)),
        out_shape=jax.ShapeDtypeStruct((m, n), out_dtype),
        scratch_shapes=[pltpu.VMEM((tm, tn) if nk > 1 else (8, 128), F32)],
        compiler_params=_params("parallel", "parallel", "arbitrary"), name=name,
    )(*args)


def _rms(x, gain):
    return x * lax.rsqrt(jnp.mean(x * x, axis=-1, keepdims=True) + NORM_EPS) * gain


def rms_fwd(h, gain, name):
    t = h.shape[0]
    ts = _tile(t, 640, 128)

    def body(h_ref, g_ref, u_ref, ut_ref):
        u = _rms(h_ref[...], g_ref[...])
        u_ref[...] = u.astype(BF16)
        ut_ref[...] = u.T.astype(BF16)

    return pl.pallas_call(
        body, grid=(t // ts,),
        in_specs=[pl.BlockSpec((ts, D_MODEL), lambda i: (i, 0)), pl.BlockSpec((1, D_MODEL), lambda i: (0, 0))],
        out_specs=(pl.BlockSpec((ts, D_MODEL), lambda i: (i, 0)), pl.BlockSpec((D_MODEL, ts), lambda i: (0, i))),
        out_shape=(jax.ShapeDtypeStruct((t, D_MODEL), BF16), jax.ShapeDtypeStruct((D_MODEL, t), BF16)),
        compiler_params=_params("parallel"), name=name,
    )(h, gain)


def rms_bwd(h, gain, du, dres, name):
    t = h.shape[0]
    ts = _tile(t, 640, 64)

    def body(h_ref, g_ref, du_ref, dres_ref, dh_ref, dg_ref):
        i = pl.program_id(0)
        _, vjp = jax.vjp(_rms, h_ref[...], g_ref[...])
        dx, dg = vjp(du_ref[...])
        row = i * ts + lax.broadcasted_iota(jnp.int32, (ts, 1), 0)
        dh_ref[...] = jnp.where(row >= LEAD, dx + dres_ref[...], 0.0)

        @pl.when(i == 0)
        def _():
            dg_ref[...] = jnp.zeros_like(dg_ref)

        dg_ref[...] += dg

    row_spec = pl.BlockSpec((ts, D_MODEL), lambda i: (i, 0))
    vec_spec = pl.BlockSpec((1, D_MODEL), lambda i: (0, 0))
    return pl.pallas_call(
        body, grid=(t // ts,), in_specs=[row_spec, vec_spec, row_spec, row_spec],
        out_specs=(row_spec, vec_spec),
        out_shape=(jax.ShapeDtypeStruct((t, D_MODEL), F32), jax.ShapeDtypeStruct((1, D_MODEL), F32)),
        compiler_params=_params("arbitrary"), name=name,
    )(h, gain, du, dres)


def loss_head(h, gain, target, name):
    t = h.shape[0]
    ts = _tile(t, 640, 64)

    def body(h_ref, g_ref, t_ref, loss_ref, dh_ref, dg_ref):
        i = pl.program_id(0)
        row = i * ts + lax.broadcasted_iota(jnp.int32, (ts, 1), 0)
        keep = jnp.logical_and(row >= LEAD + N_META, row < t - TAIL)
        tgt = t_ref[...]

        def f(x, g):
            err = jnp.where(keep, _rms(x, g) - tgt, 0.0)
            per_row = jnp.mean(err * err, axis=-1, keepdims=True)
            return 0.5 * jnp.sum(per_row, axis=0, keepdims=True)

        val, vjp = jax.vjp(f, h_ref[...], g_ref[...])
        dx, dg = vjp(jnp.ones((1, 1), F32))
        dh_ref[...] = dx

        @pl.when(i == 0)
        def _():
            dg_ref[...] = jnp.zeros_like(dg_ref)
            loss_ref[...] = jnp.zeros_like(loss_ref)

        dg_ref[...] += dg
        loss_ref[...] += jnp.broadcast_to(val, (1, 128))

    row_spec = pl.BlockSpec((ts, D_MODEL), lambda i: (i, 0))
    vec_spec = pl.BlockSpec((1, D_MODEL), lambda i: (0, 0))
    return pl.pallas_call(
        body, grid=(t // ts,), in_specs=[row_spec, vec_spec, row_spec],
        out_specs=(pl.BlockSpec((1, 128), lambda i: (0, 0)), row_spec, vec_spec),
        out_shape=(jax.ShapeDtypeStruct((1, 128), F32), jax.ShapeDtypeStruct((t, D_MODEL), F32),
                   jax.ShapeDtypeStruct((1, D_MODEL), F32)),
        compiler_params=_params("arbitrary"), name=name,
    )(h, gain, target)


def _rows_down(a, s):
    return a if s == 0 else pltpu.roll(a, s, axis=0)


def _rows_up(a, s):
    return a if s == 0 else pltpu.roll(a, a.shape[0] - s, axis=0)


def conv_fwd(x, w, name):
    t, width = x.shape
    k = w.shape[0]
    ts = _tile(t, 640, 64)
    tw = _tile(width, 1536, 128)
    hb = ts // 8

    def body(x_ref, halo_ref, w_ref, o_ref, buf):
        i = pl.program_id(0)
        buf[0:8, :] = jnp.where(i > 0, halo_ref[...], 0.0)
        buf[8:, :] = x_ref[...]
        ext = buf[...]
        wv = w_ref[...]
        acc = _rows_down(ext, k - 1)[8:, :] * wv[0:1, :]
        for j in range(1, k):
            acc = acc + _rows_down(ext, k - 1 - j)[8:, :] * wv[j:j + 1, :]
        o_ref[...] = acc

    return pl.pallas_call(
        body, grid=(t // ts, width // tw),
        in_specs=[pl.BlockSpec((ts, tw), lambda i, j: (i, j)),
                  pl.BlockSpec((8, tw), lambda i, j: (jnp.maximum(i * hb - 1, 0), j)),
                  pl.BlockSpec((k, tw), lambda i, j: (0, j))],
        out_specs=pl.BlockSpec((ts, tw), lambda i, j: (i, j)),
        out_shape=jax.ShapeDtypeStruct((t, width), F32),
        scratch_shapes=[pltpu.VMEM((ts + 8, tw), F32)],
        compiler_params=_params("parallel", "parallel"), name=name,
    )(x, x, w)


def conv_bwd(x, dc, w, name):
    t, width = x.shape
    k = w.shape[0]
    ts = _tile(t, 640, 64)
    tw = _tile(width, 1536, 128)
    hb = ts // 8
    nt = t // ts

    def body(x_ref, xh_ref, dc_ref, dch_ref, w_ref, dx_ref, dw_ref, xbuf, dbuf):
        i = pl.program_id(1)
        xbuf[0:8, :] = jnp.where(i > 0, xh_ref[...], 0.0)
        xbuf[8:, :] = x_ref[...]
        d = dc_ref[...]
        dbuf[0:ts, :] = d
        dbuf[ts:, :] = jnp.where(i < nt - 1, dch_ref[...], 0.0)
        wv = w_ref[...]
        ext_x, ext_d = xbuf[...], dbuf[...]
        acc = _rows_up(ext_d, k - 1)[0:ts, :] * wv[0:1, :]
        for j in range(1, k):
            acc = acc + _rows_up(ext_d, k - 1 - j)[0:ts, :] * wv[j:j + 1, :]
        dx_ref[...] = acc.astype(BF16)

        @pl.when(i == 0)
        def _():
            dw_ref[...] = jnp.zeros_like(dw_ref)

        for j in range(k):
            dw_ref[j:j + 1, :] += jnp.sum(d * _rows_down(ext_x, k - 1 - j)[8:, :], axis=0, keepdims=True)

    return pl.pallas_call(
        body, grid=(width // tw, nt),
        in_specs=[pl.BlockSpec((ts, tw), lambda j, i: (i, j)),
                  pl.BlockSpec((8, tw), lambda j, i: (jnp.maximum(i * hb - 1, 0), j)),
                  pl.BlockSpec((ts, tw), lambda j, i: (i, j)),
                  pl.BlockSpec((8, tw), lambda j, i: (jnp.minimum((i + 1) * hb, t // 8 - 1), j)),
                  pl.BlockSpec((k, tw), lambda j, i: (0, j))],
        out_specs=(pl.BlockSpec((ts, tw), lambda j, i: (i, j)), pl.BlockSpec((8, tw), lambda j, i: (0, j))),
        out_shape=(jax.ShapeDtypeStruct((t, width), BF16), jax.ShapeDtypeStruct((8, width), F32)),
        scratch_shapes=[pltpu.VMEM((ts + 8, tw), F32), pltpu.VMEM((ts + 8, tw), F32)],
        compiler_params=_params("parallel", "arbitrary"), name=name,
    )(x, x, dc, dc, w)


def _pool_count(pos, win):
    return jnp.clip(pos + 1, 1, win).astype(F32)


def poolwin_fwd(p, name):
    t = p.shape[0]
    ts = _tile(t, 640, 64)
    hb = ts // 16

    def body(p_ref, halo_ref, o_ref, buf):
        i = pl.program_id(0)
        buf[0:16, :] = jnp.where(i > 0, halo_ref[...], 0.0)
        buf[16:, :] = p_ref[...]
        pos = i * ts + lax.broadcasted_iota(jnp.int32, (ts, 1), 0) - LEAD
        ext = buf[...]
        own = ext[16:, :]
        sums, span = ext, 1
        for gi, win in enumerate(POOL_WINDOWS):
            while span < win:
                sums = sums + _rows_down(sums, span)
                span *= 2
            cols = slice(gi * 128, (gi + 1) * 128)
            o_ref[:, cols] = sums[16:, cols] / _pool_count(pos, win) - own[:, cols]

    return pl.pallas_call(
        body, grid=(t // ts,),
        in_specs=[pl.BlockSpec((ts, POOL_WIDTH), lambda i: (i, 0)),
                  pl.BlockSpec((16, POOL_WIDTH), lambda i: (jnp.maximum(i * hb - 1, 0), 0))],
        out_specs=pl.BlockSpec((ts, POOL_WIDTH), lambda i: (i, 0)),
        out_shape=jax.ShapeDtypeStruct((t, POOL_WIDTH), F32),
        scratch_shapes=[pltpu.VMEM((ts + 16, POOL_WIDTH), F32)],
        compiler_params=_params("parallel"), name=name,
    )(p, p)


def poolwin_bwd(dpooled, name):
    t = dpooled.shape[0]
    ts = _tile(t, 640, 64)
    hb = ts // 16
    nt = t // ts

    def body(d_ref, halo_ref, o_ref, buf):
        i = pl.program_id(0)
        buf[0:ts, :] = d_ref[...]
        buf[ts:, :] = jnp.where(i < nt - 1, halo_ref[...], 0.0)
        pos = i * ts + lax.broadcasted_iota(jnp.int32, (ts + 16, 1), 0) - LEAD
        ext = buf[...]
        for gi, win in enumerate(POOL_WINDOWS):
            cols = slice(gi * 128, (gi + 1) * 128)
            sums, span = ext[:, cols] / _pool_count(pos, win), 1
            while span < win:
                sums = sums + _rows_up(sums, span)
                span *= 2
            o_ref[:, cols] = (sums[0:ts, :] - ext[0:ts, cols]).astype(BF16)

    return pl.pallas_call(
        body, grid=(nt,),
        in_specs=[pl.BlockSpec((ts, POOL_WIDTH), lambda i: (i, 0)),
                  pl.BlockSpec((16, POOL_WIDTH), lambda i: (jnp.minimum((i + 1) * hb, t // 16 - 1), 0))],
        out_specs=pl.BlockSpec((ts, POOL_WIDTH), lambda i: (i, 0)),
        out_shape=jax.ShapeDtypeStruct((t, POOL_WIDTH), BF16),
        scratch_shapes=[pltpu.VMEM((ts + 16, POOL_WIDTH), F32)],
        compiler_params=_params("parallel"), name=name,
    )(dpooled, dpooled)


def _mix(y_a, gpre, pooled, w_pool, scale):
    parts = [_nn(pooled[:, g * 128:(g + 1) * 128], w_pool[g]) for g in range(4)]
    y_b = jnp.concatenate(parts, axis=1) * scale
    return jax.nn.sigmoid(gpre[:, :D_MODEL]) * y_a + jax.nn.sigmoid(gpre[:, D_MODEL:]) * y_b


def _mix_specs(ts):
    return [pl.BlockSpec((ts, D_MODEL), lambda i: (i, 0)), pl.BlockSpec((ts, 2 * D_MODEL), lambda i: (i, 0)),
            pl.BlockSpec((ts, POOL_WIDTH), lambda i: (i, 0)), pl.BlockSpec((4, 128, 256), lambda i: (0, 0, 0)),
            pl.BlockSpec((1, D_MODEL), lambda i: (0, 0))]


def mix_fwd(y_a, gpre, pooled, w_pool, scale, name):
    t = y_a.shape[0]
    ts = _tile(t, 640, 128)

    def body(ya_ref, g_ref, p_ref, w_ref, s_ref, o_ref, ot_ref):
        y = _mix(ya_ref[...], g_ref[...], p_ref[...], w_ref[...], s_ref[...])
        o_ref[...] = y.astype(BF16)
        ot_ref[...] = y.T.astype(BF16)

    return pl.pallas_call(
        body, grid=(t // ts,), in_specs=_mix_specs(ts),
        out_specs=(pl.BlockSpec((ts, D_MODEL), lambda i: (i, 0)), pl.BlockSpec((D_MODEL, ts), lambda i: (0, i))),
        out_shape=(jax.ShapeDtypeStruct((t, D_MODEL), BF16), jax.ShapeDtypeStruct((D_MODEL, t), BF16)),
        compiler_params=_params("parallel"), name=name,
    )(y_a, gpre, pooled, w_pool, scale)


def mix_bwd(y_a, gpre, pooled, w_pool, scale, dy, name):
    t = y_a.shape[0]
    ts = _tile(t, 320, 64)

    def body(ya_ref, g_ref, p_ref, w_ref, s_ref, dy_ref, dya_ref, dg_ref, dp_ref, dw_ref, ds_ref):
        i = pl.program_id(0)
        _, vjp = jax.vjp(_mix, ya_ref[...], g_ref[...], p_ref[...], w_ref[...], s_ref[...])
        dya, dg, dp, dw, ds = vjp(dy_ref[...])
        dya_ref[...] = dya
        dg_ref[...] = dg.astype(BF16)
        dp_ref[...] = dp

        @pl.when(i == 0)
        def _():
            dw_ref[...] = jnp.zeros_like(dw_ref)
            ds_ref[...] = jnp.zeros_like(ds_ref)

        dw_ref[...] += dw
        ds_ref[...] += ds

    specs = _mix_specs(ts)
    return pl.pallas_call(
        body, grid=(t // ts,), in_specs=specs + [specs[0]],
        out_specs=(specs[0], specs[1], specs[2], specs[3], specs[4]),
        out_shape=(jax.ShapeDtypeStruct((t, D_MODEL), F32), jax.ShapeDtypeStruct((t, 2 * D_MODEL), BF16),
                   jax.ShapeDtypeStruct((t, POOL_WIDTH), F32), jax.ShapeDtypeStruct((4, 128, 256), F32),
                   jax.ShapeDtypeStruct((1, D_MODEL), F32)),
        compiler_params=_params("arbitrary"), name=name,
    )(y_a, gpre, pooled, w_pool, scale, dy)


def _ffn_act(cg, cv):
    return _silu(cg) * cv


def ffnact_fwd(cg, cv, name):
    t, width = cg.shape
    ts = _tile(t, 640, 128)
    tw = _tile(width, 1536, 128)
    spec = pl.BlockSpec((ts, tw), lambda i, j: (i, j))

    def body(g_ref, v_ref, o_ref, ot_ref):
        act = _ffn_act(g_ref[...], v_ref[...])
        o_ref[...] = act.astype(BF16)
        ot_ref[...] = act.T.astype(BF16)

    return pl.pallas_call(
        body, grid=(t // ts, width // tw), in_specs=[spec, spec],
        out_specs=(spec, pl.BlockSpec((tw, ts), lambda i, j: (j, i))),
        out_shape=(jax.ShapeDtypeStruct((t, width), BF16), jax.ShapeDtypeStruct((width, t), BF16)),
        compiler_params=_params("parallel", "parallel"), name=name,
    )(cg, cv)


def ffnact_bwd(cg, cv, dact, name):
    t, width = cg.shape
    ts = _tile(t, 640, 64)
    tw = _tile(width, 1536, 128)
    spec = pl.BlockSpec((ts, tw), lambda i, j: (i, j))

    def body(g_ref, v_ref, d_ref, dg_ref, dv_ref):
        _, vjp = jax.vjp(_ffn_act, g_ref[...], v_ref[...])
        dg_ref[...], dv_ref[...] = vjp(d_ref[...])

    return pl.pallas_call(
        body, grid=(t // ts, width // tw), in_specs=[spec, spec, spec], out_specs=(spec, spec),
        out_shape=(jax.ShapeDtypeStruct((t, width), F32), jax.ShapeDtypeStruct((t, width), F32)),
        compiler_params=_params("parallel", "parallel"), name=name,
    )(cg, cv, dact)


def _gdn_chunk(c, z, ba, pa, pdt, hn, s, *, valid, inverse=None, with_inverse=False):
    r = lax.broadcasted_iota(jnp.int32, (CHUNK, CHUNK), 0)
    q_ = lax.broadcasted_iota(jnp.int32, (CHUNK, CHUNK), 1)
    causal = r >= q_
    strict = r > q_
    tril = causal.astype(F32)
    triu = (r <= q_).astype(F32)
    lane = lax.broadcasted_iota(jnp.int32, (CHUNK, 128), 1)

    decay_log = -jnp.exp(pa) * _softplus(ba + pdt)
    bg = jnp.where(lane < HEADS, jax.nn.sigmoid(ba), jnp.where(lane < 2 * HEADS, decay_log, 0.0))
    bg = jnp.where(valid, bg, 0.0)
    gc = _hnn(tril, bg)
    gct = _hnn(bg.T, triu)
    eg = jnp.exp(gc)
    glast = gc[CHUNK - 1:CHUNK, :]
    ekd = jnp.exp(glast - gc)
    gtot = jnp.exp(glast)

    hd = range(HEADS)
    hs = [slice(h * HEAD_DIM, (h + 1) * HEAD_DIM) for h in hd]
    gl = [slice(HEADS + h, HEADS + h + 1) for h in hd]
    q = [_silu(c[:, hs[h]]) for h in hd]
    k = [_silu(c[:, D_MODEL + h * HEAD_DIM:D_MODEL + (h + 1) * HEAD_DIM]) for h in hd]
    v = [_silu(c[:, 2 * D_MODEL + h * HEAD_DIM:2 * D_MODEL + (h + 1) * HEAD_DIM]) for h in hd]
    q = [q[h] * lax.rsqrt(jnp.sum(q[h] * q[h], axis=-1, keepdims=True) + NORM_EPS) * (HEAD_DIM ** -0.5) for h in hd]
    k = [k[h] * lax.rsqrt(jnp.sum(k[h] * k[h], axis=-1, keepdims=True) + NORM_EPS) for h in hd]
    beta = [bg[:, h:h + 1] for h in hd]
    decay = [jnp.exp(jnp.where(causal, gc[:, gl[h]] - gct[gl[h], :], -1e30)) for h in hd]
    kb = [k[h] * beta[h] for h in hd]
    a = [jnp.where(strict, _nt(kb[h], k[h]) * decay[h], 0.0) for h in hd]
    qk = [jnp.where(causal, _nt(q[h], k[h]) * decay[h], 0.0) for h in hd]
    x = _inv_unit_lower(tuple(a)) if inverse is None else _kept_inverse(tuple(a), tuple(inverse))
    u = [_nn(x[h], v[h] * beta[h]) for h in hd]
    w = [_nn(x[h], kb[h] * eg[:, gl[h]]) for h in hd]
    v_new = [u[h] - _nn(w[h], s[h]) for h in hd]
    o = [_nn(q[h] * eg[:, gl[h]], s[h]) + _nn(qk[h], v_new[h]) for h in hd]
    states = [s[h] * gtot[:, gl[h]] + _tn(k[h] * ekd[:, gl[h]], v_new[h]) for h in hd]
    o = [o[h] * lax.rsqrt(jnp.mean(o[h] * o[h], axis=-1, keepdims=True) + NORM_EPS) * hn * _silu(z[:, hs[h]])
         for h in hd]
    if with_inverse:
        return jnp.concatenate(o, axis=1), tuple(states), x
    return jnp.concatenate(o, axis=1), tuple(states)


GDN_FWD_CHUNKS = 5
GDN_BWD_CHUNKS = 2


def _chunk_valid(n, t):
    row = n * CHUNK + lax.broadcasted_iota(jnp.int32, (CHUNK, 1), 0)
    return jnp.logical_and(row >= LEAD, row < t - TAIL)


def gdn_fwd(c, z, ba, pa, pdt, hn, name, ba_block=0):
    t = c.shape[0]
    n_chunks = t // CHUNK
    per_step = GDN_FWD_CHUNKS if n_chunks % GDN_FWD_CHUNKS == 0 else 1
    rows_per_step = per_step * CHUNK

    def body(c_ref, z_ref, ba_ref, pa_ref, pdt_ref, hn_ref, y_ref, ss_ref, inv_ref, state):
        step = pl.program_id(0)

        @pl.when(step == 0)
        def _():
            state[...] = jnp.zeros_like(state)

        s = tuple(state[h] for h in range(HEADS))
        for j in range(per_step):
            rows = pl.ds(j * CHUNK, CHUNK)
            for h in range(HEADS):
                ss_ref[j, h] = s[h]
            y, s, inv = _gdn_chunk(c_ref[rows, :], z_ref[rows, :], ba_ref[rows, :], pa_ref[...], pdt_ref[...],
                                   hn_ref[...], s, valid=_chunk_valid(step * per_step + j, t), with_inverse=True)
            y_ref[rows, :] = y
            for h in range(HEADS):
                inv_ref[j, h] = inv[h]
        for h in range(HEADS):
            state[h] = s[h]

    vec = pl.BlockSpec((1, 128), lambda n: (0, 0))
    return pl.pallas_call(
        body, grid=(n_chunks // per_step,),
        in_specs=[pl.BlockSpec((rows_per_step, QKV_DIM), lambda n: (n, 0)),
                  pl.BlockSpec((rows_per_step, D_MODEL), lambda n: (n, 0)),
                  pl.BlockSpec((rows_per_step, 128), lambda n: (n, ba_block)), vec, vec, vec],
        out_specs=(pl.BlockSpec((rows_per_step, D_MODEL), lambda n: (n, 0)),
                   pl.BlockSpec((per_step, HEADS, HEAD_DIM, HEAD_DIM), lambda n: (n, 0, 0, 0)),
                   pl.BlockSpec((per_step, HEADS, CHUNK, CHUNK), lambda n: (n, 0, 0, 0))),
        out_shape=(jax.ShapeDtypeStruct((t, D_MODEL), F32),
                   jax.ShapeDtypeStruct((n_chunks, HEADS, HEAD_DIM, HEAD_DIM), F32),
                   jax.ShapeDtypeStruct((n_chunks, HEADS, CHUNK, CHUNK), F32)),
        scratch_shapes=[pltpu.VMEM((HEADS, HEAD_DIM, HEAD_DIM), F32)],
        compiler_params=_params("arbitrary"), name=name,
    )(c, z, ba, pa, pdt, hn)


def gdn_bwd(c, z, ba, pa, pdt, hn, starts, inverses, dy, name, ba_block=0):
    t = c.shape[0]
    per_step = GDN_BWD_CHUNKS if (t // CHUNK) % GDN_BWD_CHUNKS == 0 else 1
    n_steps = t // CHUNK // per_step
    rows_per_step = per_step * CHUNK

    def body(c_ref, z_ref, ba_ref, pa_ref, pdt_ref, hn_ref, ss_ref, inv_ref, dy_ref,
             dc_ref, dz_ref, dba_ref, dpa_ref, dpdt_ref, dhn_ref, dstate):
        step = pl.program_id(0)

        @pl.when(step == 0)
        def _():
            dstate[...] = jnp.zeros_like(dstate)
            dpa_ref[...] = jnp.zeros_like(dpa_ref)
            dpdt_ref[...] = jnp.zeros_like(dpdt_ref)
            dhn_ref[...] = jnp.zeros_like(dhn_ref)

        ds = tuple(dstate[h] for h in range(HEADS))
        for j in reversed(range(per_step)):
            rows = pl.ds(j * CHUNK, CHUNK)
            f = functools.partial(_gdn_chunk, valid=_chunk_valid((n_steps - 1 - step) * per_step + j, t),
                                  inverse=tuple(inv_ref[j, h] for h in range(HEADS)))
            _, vjp = jax.vjp(f, c_ref[rows, :], z_ref[rows, :], ba_ref[rows, :], pa_ref[...], pdt_ref[...], hn_ref[...],
                             tuple(ss_ref[j, h] for h in range(HEADS)))
            dc, dz, dba, dpa, dpdt, dhn, ds = vjp((dy_ref[rows, :], ds))
            dc_ref[rows, :] = dc
            dz_ref[rows, :] = dz.astype(BF16)
            dba_ref[rows, :] = dba.astype(BF16)
            dpa_ref[...] += dpa
            dpdt_ref[...] += dpdt
            dhn_ref[...] += dhn
        for h in range(HEADS):
            dstate[h] = ds[h]

    def rev(width, block=0):
        return pl.BlockSpec((rows_per_step, width), lambda s: (n_steps - 1 - s, block))

    vec = pl.BlockSpec((1, 128), lambda s: (0, 0))
    vec_shape = jax.ShapeDtypeStruct((1, 128), F32)
    return pl.pallas_call(
        body, grid=(n_steps,),
        in_specs=[rev(QKV_DIM), rev(D_MODEL), rev(128, ba_block), vec, vec, vec,
                  pl.BlockSpec((per_step, HEADS, HEAD_DIM, HEAD_DIM), lambda s: (n_steps - 1 - s, 0, 0, 0)),
                  pl.BlockSpec((per_step, HEADS, CHUNK, CHUNK), lambda s: (n_steps - 1 - s, 0, 0, 0)),
                  rev(D_MODEL)],
        out_specs=(rev(QKV_DIM), rev(D_MODEL), rev(128), vec, vec, vec),
        out_shape=(jax.ShapeDtypeStruct((t, QKV_DIM), F32), jax.ShapeDtypeStruct((t, D_MODEL), BF16),
                   jax.ShapeDtypeStruct((t, 128), BF16), vec_shape, vec_shape, vec_shape),
        scratch_shapes=[pltpu.VMEM((HEADS, HEAD_DIM, HEAD_DIM), F32)],
        compiler_params=_params("arbitrary"), name=name,
    )(c, z, ba, pa, pdt, hn, starts, inverses, dy)


def _layer_fwd(h, w, tag):
    u, ut = rms_fwd(h, w["norm_mix"], f"{tag}_rms_mix")
    pq = mm(u, w["wqkv"], name=f"{tag}_mm_qkv")
    pz = mm(u, w["wz"], name=f"{tag}_mm_z")
    pg = mm(u, w["wg"], name=f"{tag}_mm_gate")
    pba = mm(u, w["wpb"], name=f"{tag}_mm_pool_ba")
    cq = conv_fwd(pq, w["conv_qkv"], f"{tag}_conv_qkv")
    ya, starts, inverses = gdn_fwd(cq, pz, pba, w["pa"], w["pdt"], w["head_norm"], f"{tag}_gdn", ba_block=BA_BLOCK)
    pooled = poolwin_fwd(pba, f"{tag}_poolwin")
    y, yt = mix_fwd(ya, pg, pooled, w["w_pool"], w["pool_scale"], f"{tag}_mix")
    h1 = mm(y, w["wout"], add=h, name=f"{tag}_mm_out")
    u2, u2t = rms_fwd(h1, w["norm_ffn"], f"{tag}_rms_ffn")
    hg = mm(u2, w["wupg"], name=f"{tag}_mm_upg")
    hv = mm(u2, w["wupv"], name=f"{tag}_mm_upv")
    cg = conv_fwd(hg, w["conv_g"], f"{tag}_conv_g")
    cv = conv_fwd(hv, w["conv_v"], f"{tag}_conv_v")
    act, actt = ffnact_fwd(cg, cv, f"{tag}_act")
    h2 = mm(act, w["wdown"], add=h1, name=f"{tag}_mm_down")
    saved = dict(h=h, ut=ut, pq=pq, pz=pz, pg=pg, pba=pba, cq=cq, ya=ya, starts=starts, inverses=inverses, pooled=pooled, yt=yt, h1=h1,
                 u2t=u2t, hg=hg, hv=hv, cg=cg, cv=cv, actt=actt)
    return h2, saved


def _layer_bwd(dh2, w, s, tag):
    g = {}
    dact = mm(dh2, w["wdown"], tb=True, name=f"{tag}_bmm_down_x")
    g["wdown"] = mm(s["actt"], dh2, name=f"{tag}_bmm_down_w")
    dcg, dcv = ffnact_bwd(s["cg"], s["cv"], dact, f"{tag}_act_b")
    dhg, g["conv_g"] = conv_bwd(s["hg"], dcg, w["conv_g"], f"{tag}_conv_g_b")
    dhv, g["conv_v"] = conv_bwd(s["hv"], dcv, w["conv_v"], f"{tag}_conv_v_b")
    du2 = mm(dhg, w["wupg"], tb=True, name=f"{tag}_bmm_upg_x")
    du2 = mm(dhv, w["wupv"], tb=True, add=du2, name=f"{tag}_bmm_upv_x")
    g["wupg"] = mm(s["u2t"], dhg, name=f"{tag}_bmm_upg_w")
    g["wupv"] = mm(s["u2t"], dhv, name=f"{tag}_bmm_upv_w")
    dh1, g["norm_ffn"] = rms_bwd(s["h1"], w["norm_ffn"], du2, dh2, f"{tag}_rms_ffn_b")
    dy = mm(dh1, w["wout"], tb=True, name=f"{tag}_bmm_out_x")
    g["wout"] = mm(s["yt"], dh1, name=f"{tag}_bmm_out_w")
    dya, dpg, dpooled, g["w_pool"], g["pool_scale"] = mix_bwd(
        s["ya"], s["pg"], s["pooled"], w["w_pool"], w["pool_scale"], dy, f"{tag}_mix_b")
    dpp = poolwin_bwd(dpooled, f"{tag}_poolwin_b")
    dcq, dpz, dpba, g["pa"], g["pdt"], g["head_norm"] = gdn_bwd(
        s["cq"], s["pz"], s["pba"], w["pa"], w["pdt"], w["head_norm"], s["starts"], s["inverses"], dya, f"{tag}_gdn_b",
        ba_block=BA_BLOCK)
    dpb = jnp.concatenate([dpp, dpba], axis=1)
    dpq, g["conv_qkv"] = conv_bwd(s["pq"], dcq, w["conv_qkv"], f"{tag}_conv_qkv_b")
    du = mm(dpq, w["wqkv"], tb=True, name=f"{tag}_bmm_qkv_x")
    du = mm(dpz, w["wz"], tb=True, add=du, name=f"{tag}_bmm_z_x")
    du = mm(dpg, w["wg"], tb=True, add=du, name=f"{tag}_bmm_gate_x")
    du = mm(dpb, w["wpb"], tb=True, add=du, name=f"{tag}_bmm_pool_ba_x")
    g["wqkv"] = mm(s["ut"], dpq, name=f"{tag}_bmm_qkv_w")
    g["wz"] = mm(s["ut"], dpz, name=f"{tag}_bmm_z_w")
    g["wg"] = mm(s["ut"], dpg, name=f"{tag}_bmm_gate_w")
    dwpb = mm(s["ut"], dpb, name=f"{tag}_bmm_pool_ba_w")
    g["wpl"], g["wba"] = dwpb[:, :POOL_WIDTH], dwpb[:, POOL_WIDTH:]
    dh, g["norm_mix"] = rms_bwd(s["h"], w["norm_mix"], du, dh1, f"{tag}_rms_mix_b")
    return dh, g


def local_step(h0, target, layers, norm_final):
    h = h0
    saved = []
    for li, w in enumerate(layers):
        h, s = _layer_fwd(h, w, f"l{li}")
        saved.append(s)
    loss, dh, dnf = loss_head(h, norm_final, target, "loss_head")
    grads = [None] * len(layers)
    for li in reversed(range(len(layers))):
        dh, grads[li] = _layer_bwd(dh, layers[li], saved[li], f"l{li}")
    return loss, dh, grads, dnf


_Z0, _B0, _P0, _G0, _IN_DIM = 3072, 4096, 4112, 4624, 6672


def _lanes_8_to_15(v):
    return jnp.pad(v.reshape(1, HEADS).astype(F32), ((0, 0), (HEADS, 128 - 2 * HEADS)))


IN_PIECES = (("wqkv", 0, _Z0), ("wz", _Z0, _B0), ("wba", _B0, _P0), ("wpl", _P0, _G0), ("wg", _G0, _IN_DIM))
IN_SHARD = _IN_DIM // 4


def _overlaps(a, b, spans):
    return [(name, max(a, lo) - lo, min(b, hi) - max(a, lo)) for name, lo, hi in spans if max(a, lo) < min(b, hi)]


def _cat(parts):
    return parts[0] if len(parts) == 1 else jnp.concatenate(parts, axis=1)


def prep_layer(p):
    row = lambda v: v.reshape(1, -1).astype(F32)
    w_in, w_up = p["w_in"], p["w_up"]
    if not isinstance(w_in, (list, tuple)):
        w_in = [w_in[:, s * IN_SHARD:(s + 1) * IN_SHARD] for s in range(4)]
        w_up = [w_up[:, s * (D_FF // 2):(s + 1) * (D_FF // 2)] for s in range(4)]
    shards = [(s, s * IN_SHARD, (s + 1) * IN_SHARD) for s in range(4)]
    piece = {name: _cat([w_in[s][:, off:off + width].astype(BF16) for s, off, width in _overlaps(lo, hi, shards)])
             for name, lo, hi in IN_PIECES}
    return dict(
        wqkv=piece["wqkv"], wz=piece["wz"],
        wpb=jnp.concatenate([piece["wpl"], jnp.pad(piece["wba"], ((0, 0), (0, 128 - 2 * HEADS)))], axis=1),
        wg=piece["wg"], wout=p["w_out"].astype(BF16),
        wupg=_cat([w_up[0].astype(BF16), w_up[1].astype(BF16)]), wupv=_cat([w_up[2].astype(BF16), w_up[3].astype(BF16)]),
        wdown=p["w_down"].astype(BF16),
        conv_qkv=p["conv_qkv"].astype(F32), conv_g=p["conv_ffn"][:, :D_FF].astype(F32),
        conv_v=p["conv_ffn"][:, D_FF:].astype(F32), w_pool=p["w_pool"].astype(F32),
        pool_scale=row(p["pool_scale"]), head_norm=row(p["head_norm"]), norm_mix=row(p["norm_mix"]),
        norm_ffn=row(p["norm_ffn"]), pa=_lanes_8_to_15(p["a_log"]), pdt=_lanes_8_to_15(p["dt_bias"]))


def layer_grads(g):
    return dict(
        w_in=jnp.concatenate([g["wqkv"], g["wz"], g["wba"][:, :2 * HEADS], g["wpl"], g["wg"]], axis=1),
        conv_qkv=g["conv_qkv"][:4], a_log=g["pa"][0, HEADS:2 * HEADS], dt_bias=g["pdt"][0, HEADS:2 * HEADS],
        head_norm=g["head_norm"][0], w_pool=g["w_pool"], pool_scale=g["pool_scale"][0], w_out=g["wout"],
        norm_mix=g["norm_mix"][0], norm_ffn=g["norm_ffn"][0],
        w_up=jnp.concatenate([g["wupg"], g["wupv"]], axis=1),
        conv_ffn=jnp.concatenate([g["conv_g"][:3], g["conv_v"][:3]], axis=1), w_down=g["wdown"])


def big_grad_shards(g):
    in_shards = [_cat([g[name][:, off:off + width] for name, off, width in
                       _overlaps(s * IN_SHARD, (s + 1) * IN_SHARD, IN_PIECES)]) for s in range(4)]
    half = D_FF // 2
    up_shards = [g["wupg"][:, :half], g["wupg"][:, half:], g["wupv"][:, :half], g["wupv"][:, half:]]
    return dict(w_in=jnp.stack(in_shards), w_up=jnp.stack(up_shards),
                w_down=g["wdown"].reshape(4, D_FF // 4, D_MODEL), w_out=g["wout"].reshape(4, D_MODEL // 4, D_MODEL))


LAYER_PARAMS = ("norm_mix", "w_in", "conv_qkv", "a_log", "dt_bias", "head_norm", "w_pool", "pool_scale", "w_out",
                "norm_ffn", "w_up", "conv_ffn", "w_down")


def pad_rows(meta, x):
    return jnp.concatenate([jnp.zeros((LEAD, D_MODEL), F32), meta.astype(F32), x.astype(F32),
                            jnp.zeros((TAIL, D_MODEL), F32)], axis=0)


MESH = pl.DeviceIdType.MESH
ANY = pl.BlockSpec(memory_space=pl.ANY)


def _place():
    x, y, c = lax.axis_index("x"), lax.axis_index("y"), lax.axis_index("c")
    return x, y, c, [(1 - x, y), (x, 1 - y), (1 - x, 1 - y)]


def _my_chip():
    return 2 * lax.axis_index("x") + lax.axis_index("y")


def gather_shards(packs):
    n = len(packs)

    def body(*refs):
        p_refs, o_refs, (send_sems, recv_sems) = refs[:n], refs[n:2 * n], refs[2 * n:]
        x, y, c, chips = _place()

        def copy(a, k, chip, half, to, src=None):
            dst = o_refs[a].at[2 * chip[0] + chip[1], half]
            return pltpu.make_async_remote_copy(src_ref=dst if src is None else src, dst_ref=dst,
                                                send_sem=send_sems.at[6 * a + k], recv_sem=recv_sems.at[6 * a + k],
                                                device_id=to, device_id_type=MESH)

        first = [copy(a, j, (x, y), c, (*chip, c), src=p_refs[a].at[c]) for a in range(n) for j, chip in enumerate(chips)]
        for cp in first:
            cp.start()
        passed = []
        for a in range(n):
            for j, chip in enumerate(chips):
                copy(a, j, chip, c, (x, y, c)).wait_recv()
                passed.append(copy(a, 3 + j, chip, c, (x, y, 1 - c)))
                passed[-1].start()
        for a in range(n):
            for j, chip in enumerate(chips):
                copy(a, 3 + j, chip, 1 - c, (x, y, c)).wait_recv()
        for cp in first + passed:
            cp.wait_send()

    gathered = pl.pallas_call(
        body, in_specs=[ANY] * n, out_specs=[ANY] * n,
        out_shape=[jax.ShapeDtypeStruct((4,) + p.shape, p.dtype) for p in packs],
        scratch_shapes=[pltpu.SemaphoreType.DMA((6 * n,)), pltpu.SemaphoreType.DMA((6 * n,))],
        name="gather_shards",
    )(*packs)
    me = _my_chip()
    return [lax.dynamic_update_slice(g, p[None], (me,) + (0,) * p.ndim) for g, p in zip(gathered, packs)]


def swap_other_halves(ps):
    n = len(ps)

    def body(*refs):
        p_refs, o_refs, (send_sems, recv_sems) = refs[:n], refs[n:2 * n], refs[2 * n:]
        x, y, c, _ = _place()
        copies = [pltpu.make_async_remote_copy(src_ref=p_refs[a].at[s, 1 - c], dst_ref=o_refs[a].at[s],
                                               send_sem=send_sems.at[4 * a + s], recv_sem=recv_sems.at[4 * a + s],
                                               device_id=(x, y, 1 - c), device_id_type=MESH)
                  for a in range(n) for s in range(4)]
        for cp in copies:
            cp.start()
        for cp in copies:
            cp.wait()

    return pl.pallas_call(
        body, in_specs=[ANY] * n, out_specs=[ANY] * n,
        out_shape=[jax.ShapeDtypeStruct((4,) + p.shape[2:], p.dtype) for p in ps],
        scratch_shapes=[pltpu.SemaphoreType.DMA((4 * n,)), pltpu.SemaphoreType.DMA((4 * n,))],
        name="swap_other_halves",
    )(*ps)


def scatter_to_chips(qs):
    n = len(qs)

    def body(*refs):
        q_refs, o_refs, (send_sems, recv_sems) = refs[:n], refs[n:2 * n], refs[2 * n:]
        x, y, c, chips = _place()
        me = 2 * x + y
        copies = [pltpu.make_async_remote_copy(src_ref=q_refs[a].at[2 * chip[0] + chip[1]], dst_ref=o_refs[a].at[me],
                                               send_sem=send_sems.at[3 * a + j], recv_sem=recv_sems.at[3 * a + j],
                                               device_id=(*chip, c), device_id_type=MESH)
                  for a in range(n) for j, chip in enumerate(chips)]
        for cp in copies:
            cp.start()
        for a in range(n):
            for j, chip in enumerate(chips):
                slot = o_refs[a].at[2 * chip[0] + chip[1]]
                pltpu.make_async_remote_copy(src_ref=slot, dst_ref=slot, send_sem=send_sems.at[3 * a + j],
                                             recv_sem=recv_sems.at[3 * a + j],
                                             device_id=(x, y, c), device_id_type=MESH).wait_recv()
        for cp in copies:
            cp.wait_send()

    received = pl.pallas_call(
        body, in_specs=[ANY] * n, out_specs=[ANY] * n,
        out_shape=[jax.ShapeDtypeStruct(q.shape, q.dtype) for q in qs],
        scratch_shapes=[pltpu.SemaphoreType.DMA((3 * n,)), pltpu.SemaphoreType.DMA((3 * n,))],
        name="scatter_to_chips",
    )(*qs)
    me = _my_chip()
    return [lax.dynamic_update_slice(r, lax.dynamic_slice_in_dim(q, me, 1, axis=0), (me, 0, 0))
            for r, q in zip(received, qs)]


def join_halves(boths):
    n = len(boths)

    def body(*refs):
        o_refs, (send_sems, recv_sems) = refs[n:2 * n], refs[2 * n:]
        x, y, c, _ = _place()
        copies = [pltpu.make_async_remote_copy(src_ref=o_refs[a].at[c], dst_ref=o_refs[a].at[c],
                                               send_sem=send_sems.at[a], recv_sem=recv_sems.at[a],
                                               device_id=(x, y, 1 - c), device_id_type=MESH) for a in range(n)]
        for cp in copies:
            cp.start()
        for a in range(n):
            other = o_refs[a].at[1 - c]
            pltpu.make_async_remote_copy(src_ref=other, dst_ref=other, send_sem=send_sems.at[a],
                                         recv_sem=recv_sems.at[a], device_id=(x, y, c), device_id_type=MESH).wait_recv()
        for cp in copies:
            cp.wait_send()

    return pl.pallas_call(
        body, in_specs=[ANY] * n, out_specs=[ANY] * n,
        out_shape=[jax.ShapeDtypeStruct(b.shape, b.dtype) for b in boths],
        input_output_aliases={a: a for a in range(n)},
        scratch_shapes=[pltpu.SemaphoreType.DMA((n,)), pltpu.SemaphoreType.DMA((n,))], name="join_halves",
    )(*boths)


def add_own_half(p, other, c, out_dtype, name):
    _, _, rows, lanes = p.shape
    tr = _tile(rows, max(16, 524288 // lanes), 16)

    def body(c_ref, p_ref, o_ref, out_ref):
        out_ref[...] = (p_ref[...] + o_ref[...]).astype(out_dtype)

    return pl.pallas_call(
        body,
        grid_spec=pltpu.PrefetchScalarGridSpec(
            num_scalar_prefetch=1, grid=(4, rows // tr),
            in_specs=[pl.BlockSpec((None, None, tr, lanes), lambda s, i, c_ref: (s, c_ref[0], i, 0)),
                      pl.BlockSpec((None, tr, lanes), lambda s, i, c_ref: (s, i, 0))],
            out_specs=pl.BlockSpec((None, tr, lanes), lambda s, i, c_ref: (s, i, 0))),
        out_shape=jax.ShapeDtypeStruct((4, rows, lanes), out_dtype),
        compiler_params=_params("parallel", "parallel"), name=name,
    )(c, p, other)


def sum_chips(b, c, name):
    _, rows, lanes = b.shape
    tr = _tile(rows, max(16, 524288 // lanes), 16)

    def body(c_ref, b_ref, out_ref):
        b0, b1, b2, b3 = (b_ref[k].astype(F32) for k in range(4))
        out_ref[...] = ((b0 + b1) + b2) + b3

    return pl.pallas_call(
        body,
        grid_spec=pltpu.PrefetchScalarGridSpec(
            num_scalar_prefetch=1, grid=(rows // tr,),
            in_specs=[pl.BlockSpec((4, tr, lanes), lambda i, c_ref: (0, i, 0))],
            out_specs=pl.BlockSpec((None, tr, lanes), lambda i, c_ref: (c_ref[0], i, 0))),
        out_shape=jax.ShapeDtypeStruct((2, rows, lanes), F32),
        compiler_params=_params("parallel"), name=name,
    )(c, b)


def all_reduce_to_shards(packs, wires, tags, c):
    others = swap_other_halves(packs)
    qs = [add_own_half(p, o, c, wire, f"add_own_half_{tag}") for p, o, wire, tag in zip(packs, others, wires, tags)]
    return join_halves([sum_chips(r, c, f"sum_chips_{tag}") for r, tag in zip(scatter_to_chips(qs), tags)])


def adamw(w, g, m, v, name):
    shape = w.shape
    cols = shape[-1]
    w2, g2, m2, v2 = (a.reshape(-1, cols) for a in (w, g, m, v))
    rows = w2.shape[0]
    tr = _tile(rows, max(8, 262144 // cols), 8) if rows % 8 == 0 else rows
    c1 = 1.0 - ADAM_B1 ** ADAM_STEP
    c2 = 1.0 - ADAM_B2 ** ADAM_STEP

    def body(w_ref, g_ref, m_ref, v_ref, d_ref, mo_ref, vo_ref):
        gv = g_ref[...]
        mn = ADAM_B1 * m_ref[...] + (1.0 - ADAM_B1) * gv
        vn = ADAM_B2 * v_ref[...] + (1.0 - ADAM_B2) * jnp.square(gv)
        d_ref[...] = -ADAM_LR * ((mn / c1) / (jnp.sqrt(vn / c2) + ADAM_EPS) + ADAM_WD * w_ref[...])
        mo_ref[...] = mn
        vo_ref[...] = vn

    spec = pl.BlockSpec((tr, cols), lambda i: (i, 0))
    out = jax.ShapeDtypeStruct((rows, cols), F32)
    d, mn, vn = pl.pallas_call(
        body, grid=(rows // tr,), in_specs=[spec] * 4, out_specs=(spec,) * 3, out_shape=(out,) * 3,
        compiler_params=_params("parallel"), name=name,
    )(w2, g2, m2, v2)
    return d.reshape(shape), mn.reshape(shape), vn.reshape(shape)


BIG = ("w_in", "w_up", "w_down", "w_out")
SMALL = ("w_pool", "conv_qkv", "conv_ffn", "meta_tokens")
SHARDED = BIG + SMALL
MATMUL_WEIGHTS = BIG + ("w_pool",)
REPLICATED = ("norm_mix", "a_log", "dt_bias", "head_norm", "pool_scale", "norm_ffn", "norm_final")
SHARD_AXIS = {"w_in": 2, "w_up": 2, "w_out": 1, "w_down": 1, "w_pool": 3, "conv_qkv": 2, "conv_ffn": 2, "meta_tokens": 1}


def _rows_of(a):
    return a.reshape(-1, 128)


SEGMENT_ROWS = 16


def _segment(n_rows):
    return -(-n_rows // SEGMENT_ROWS) * SEGMENT_ROWS


def _pad_segment(a):
    pad = [(0, 0)] * a.ndim
    pad[-2] = (0, _segment(a.shape[-2]) - a.shape[-2])
    return jnp.pad(a, pad)


def _unshard(stacked, axis):
    full = jnp.moveaxis(stacked, 0, axis)
    shape = list(full.shape)
    shape[axis:axis + 2] = [shape[axis] * shape[axis + 1]]
    return full.reshape(shape)


def _shard_stack(full, axis):
    shape = list(full.shape)
    shape[axis:axis + 1] = [4, shape[axis] // 4]
    return jnp.moveaxis(full.reshape(shape), axis, 0)


def pack_weights(shards):
    parts = [_rows_of(shards[k].astype(WIRE)) if k in MATMUL_WEIGHTS else
             lax.bitcast_convert_type(_rows_of(shards[k].astype(F32)), WIRE).reshape(-1, 128) for k in SMALL]
    parts = [_pad_segment(p) for p in parts]
    rows = sum(p.shape[0] for p in parts)
    if rows % (2 * SEGMENT_ROWS):
        parts.append(jnp.zeros((SEGMENT_ROWS, 128), WIRE))
        rows += SEGMENT_ROWS
    return [shards[k].astype(WIRE) for k in BIG] + [jnp.concatenate(parts, axis=0).reshape(2, rows // 2, 128)]


def unpack_weights(gathered, shard_shapes):
    out = {k: _unshard(g, SHARD_AXIS[k]) for k, g in zip(BIG, gathered)}
    flat = gathered[-1].reshape(4, -1, 128)
    at = 0
    for k in SMALL:
        shp = shard_shapes[k]
        n = 1
        for e in shp:
            n *= e
        if k in MATMUL_WEIGHTS:
            r = n // 128
            stacked = flat[:, at:at + r].reshape((4,) + tuple(shp))
        else:
            r = 2 * n // 128
            stacked = lax.bitcast_convert_type(flat[:, at:at + r].reshape(4, n // 128, 128, 2), F32).reshape((4,) + tuple(shp))
        out[k] = _unshard(stacked, SHARD_AXIS[k])
        at += _segment(r)
    return out


def pack_grads(big, full, repl):
    r = jnp.concatenate([repl[k].reshape(-1) for k in REPLICATED])
    r = jnp.pad(r, (0, -r.shape[0] % 128)).reshape(1, -1, 128)
    parts = [_shard_stack(full[k], SHARD_AXIS[k]).reshape(4, -1, 128) for k in SMALL]
    parts = [_pad_segment(p) for p in parts + [jnp.broadcast_to(r, (4,) + r.shape[1:])]]
    rows = sum(p.shape[1] for p in parts)
    if rows % (2 * SEGMENT_ROWS):
        parts.append(jnp.zeros((4, SEGMENT_ROWS, 128), F32))
        rows += SEGMENT_ROWS
    side = jnp.concatenate(parts, axis=1).reshape(4, 2, rows // 2, 128)
    return list(big) + [side]


def unpack_grads(reduced, shard_shapes, repl_shapes):
    out = dict(zip(BIG, reduced))
    side = reduced[-1].reshape(-1, 128)
    at = 0
    for k in SMALL:
        n = 1
        for e in shard_shapes[k]:
            n *= e
        out[k] = side[at:at + n // 128].reshape(shard_shapes[k])
        at += _segment(n // 128)
    r = side[at:].reshape(-1)
    at = 0
    for k in REPLICATED:
        n = 1
        for e in repl_shapes[k]:
            n *= e
        out[k] = r[at:at + n].reshape(repl_shapes[k])
        at += n
    return out


WEIGHT_ORDER = ("meta_tokens", "norm_mix", "w_in", "conv_qkv", "a_log", "dt_bias", "head_norm", "w_pool", "pool_scale",
                "w_out", "norm_ffn", "w_up", "conv_ffn", "w_down", "norm_final")


def kernel(x, meta_tokens, norm_mix, w_in, conv_qkv, a_log, dt_bias, head_norm, w_pool, pool_scale, w_out, norm_ffn, w_up, conv_ffn, w_down, norm_final, loss_target, m_meta_tokens, m_norm_mix, m_w_in, m_conv_qkv, m_a_log, m_dt_bias, m_head_norm, m_w_pool, m_pool_scale, m_w_out, m_norm_ffn, m_w_up, m_conv_ffn, m_w_down, m_norm_final, v_meta_tokens, v_norm_mix, v_w_in, v_conv_qkv, v_a_log, v_dt_bias, v_head_norm, v_w_pool, v_pool_scale, v_w_out, v_norm_ffn, v_w_up, v_conv_ffn, v_w_down, v_norm_final):
    weights = dict(meta_tokens=meta_tokens, norm_mix=norm_mix, w_in=w_in, conv_qkv=conv_qkv, a_log=a_log,
                   dt_bias=dt_bias, head_norm=head_norm, w_pool=w_pool, pool_scale=pool_scale, w_out=w_out,
                   norm_ffn=norm_ffn, w_up=w_up, conv_ffn=conv_ffn, w_down=w_down, norm_final=norm_final)
    m_in = dict(zip(WEIGHT_ORDER, (m_meta_tokens, m_norm_mix, m_w_in, m_conv_qkv, m_a_log, m_dt_bias, m_head_norm,
                                   m_w_pool, m_pool_scale, m_w_out, m_norm_ffn, m_w_up, m_conv_ffn, m_w_down, m_norm_final)))
    v_in = dict(zip(WEIGHT_ORDER, (v_meta_tokens, v_norm_mix, v_w_in, v_conv_qkv, v_a_log, v_dt_bias, v_head_norm,
                                   v_w_pool, v_pool_scale, v_w_out, v_norm_ffn, v_w_up, v_conv_ffn, v_w_down, v_norm_final)))
    shard_shapes = {k: weights[k].shape for k in SHARDED}
    repl_shapes = {k: weights[k].shape for k in REPLICATED}
    core = lax.axis_index("c").astype(jnp.int32).reshape(1)

    gathered = gather_shards(pack_weights({k: weights[k] for k in SHARDED}))
    full = unpack_weights(gathered, shard_shapes)
    shards = dict(zip(BIG, gathered))
    layers = []
    for li in range(DEPTH):
        p = {k: (full[k][li] if k in SMALL else weights[k][li]) for k in LAYER_PARAMS if k not in BIG}
        p.update(w_in=[shards["w_in"][s, li] for s in range(4)], w_up=[shards["w_up"][s, li] for s in range(4)],
                 w_down=shards["w_down"][:, li].reshape(D_FF, D_MODEL),
                 w_out=shards["w_out"][:, li].reshape(D_MODEL, D_MODEL))
        layers.append(prep_layer(p))

    h0 = pad_rows(full["meta_tokens"], x[0])
    target = pad_rows(jnp.zeros((N_META, D_MODEL), F32), loss_target[0])
    loss, dh0, grads, d_norm_final = local_step(h0, target, layers, norm_final.reshape(1, D_MODEL))
    seq = x.shape[1]
    grad_x = dh0[LEAD + N_META:LEAD + N_META + seq][None]

    per_layer = [layer_grads(g) for g in grads]
    g_all = {k: jnp.stack([pl_[k] for pl_ in per_layer]) for k in LAYER_PARAMS if k not in BIG}
    g_all["meta_tokens"] = dh0[LEAD:LEAD + N_META]
    g_all["norm_final"] = d_norm_final[0]
    big = [big_grad_shards(g) for g in grads]
    packs = pack_grads([jnp.stack([b[k] for b in big], axis=1) for k in BIG],
                       {k: g_all[k] for k in SMALL}, {k: g_all[k] for k in REPLICATED})
    reduced = all_reduce_to_shards(packs, [GRAD_WIRE] * len(BIG) + [F32], BIG + ("side",), core)
    g_mine = unpack_grads(reduced, shard_shapes, repl_shapes)

    loss_sum = lax.psum(loss[0, 0], ("x", "y", "c"))
    deltas, new_m, new_v = {}, {}, {}
    for k in WEIGHT_ORDER:
        deltas[k], new_m[k], new_v[k] = adamw(weights[k], g_mine[k], m_in[k], v_in[k], f"adamw_{k}")
    return (loss_sum, grad_x, *[g_mine[k] for k in WEIGHT_ORDER], *[deltas[k] for k in WEIGHT_ORDER],
            *[new_m[k] for k in WEIGHT_ORDER], *[new_v[k] for k in WEIGHT_ORDER])
```

```python
import functools

import jax
import jax.numpy as jnp
from jax import lax
from jax.experimental import pallas as pl
from jax.experimental.pallas import tpu as pltpu

F32 = jnp.float32
BF16 = jnp.bfloat16
WIRE = jnp.bfloat16
GRAD_WIRE = jnp.bfloat16

D_MODEL = 1024
HEADS = 8
HEAD_DIM = 128
CHUNK = 64
N_META = 16
LEAD = 48
TAIL = 64
QKV_DIM = 3072
D_FF = 2816
POOL_WIDTH = 512
POOL_WINDOWS = (2, 4, 8, 16)
BA_BLOCK = POOL_WIDTH // 128
DEPTH = 2
NORM_EPS = 1e-6
ADAM_LR, ADAM_B1, ADAM_B2, ADAM_EPS, ADAM_WD, ADAM_STEP = 0.001, 0.9, 0.999, 1e-08, 0.01, 10
VMEM_LIMIT_BYTES = 48 * 1024 * 1024


def _params(*sem):
    return pltpu.CompilerParams(dimension_semantics=sem if sem else None, vmem_limit_bytes=VMEM_LIMIT_BYTES)


def _tile(n, cap, mult):
    best = None
    for t in range(mult, min(n, cap) + 1, mult):
        if n % t == 0:
            best = t
    assert best is not None, (n, cap, mult)
    return best


def _silu(x):
    return x * jax.nn.sigmoid(x)


def _softplus(x):
    return jnp.maximum(x, 0.0) + jnp.log(1.0 + jnp.exp(-jnp.abs(x)))


def _split_bf16(a):
    hi = a.astype(BF16)
    return hi, (a - hi.astype(F32)).astype(BF16)


def _dg(a, b, ca, cb, hi):
    dims = (((ca,), (cb,)), ((), ()))
    if hi is True:
        return lax.dot_general(a, b, dims, precision=lax.Precision.HIGHEST, preferred_element_type=F32)
    if hi == 3:
        (ah, al), (bh, bl) = _split_bf16(a), _split_bf16(b)
        dot = lambda x, y: lax.dot_general(x, y, dims, preferred_element_type=F32)
        return dot(ah, bh) + (dot(ah, bl) + dot(al, bh))
    return lax.dot_general(a.astype(BF16), b.astype(BF16), dims, preferred_element_type=F32)


def _make_dots(hi):
    @jax.custom_vjp
    def nn(a, b):
        return _dg(a, b, 1, 0, hi)

    @jax.custom_vjp
    def nt(a, b):
        return _dg(a, b, 1, 1, hi)

    @jax.custom_vjp
    def tn(a, b):
        return _dg(a, b, 0, 0, hi)

    nn.defvjp(lambda a, b: (nn(a, b), (a, b)), lambda r, g: (nt(g, r[1]), tn(r[0], g)))
    nt.defvjp(lambda a, b: (nt(a, b), (a, b)), lambda r, g: (nn(g, r[1]), tn(g, r[0])))
    tn.defvjp(lambda a, b: (tn(a, b), (a, b)), lambda r, g: (nt(r[1], g), nn(r[0], g)))
    return nn, nt, tn


_nn, _nt, _tn = _make_dots(False)
_hnn, _hnt, _htn = _make_dots(True)


def _neumann(a):
    n = a[0].shape[0]
    eye = (lax.broadcasted_iota(jnp.int32, (n, n), 0) == lax.broadcasted_iota(jnp.int32, (n, n), 1)).astype(F32)
    hd = range(len(a))
    p = [_dg(a[h], a[h], 1, 0, 3) for h in hd]
    x = [(eye - a[h]) + p[h] - _dg(a[h], p[h], 1, 0, False) for h in hd]
    for _ in range(4):
        p = [_dg(p[h], p[h], 1, 0, False) for h in hd]
        x = [x[h] + p[h] + _dg(x[h] - eye, p[h], 1, 0, False) for h in hd]
    return tuple(x)


@jax.custom_vjp
def _inv_unit_lower(a):
    return _neumann(a)


def _inv_unit_lower_bwd(x, g):
    t = [_dg(x[h], g[h], 0, 0, 3) for h in range(len(x))]
    return (tuple(-_dg(t[h], x[h], 1, 1, 3) for h in range(len(x))),)


_inv_unit_lower.defvjp(lambda a: (_neumann(a),) * 2, _inv_unit_lower_bwd)


@jax.custom_vjp
def _kept_inverse(a, x):
    return x


_kept_inverse.defvjp(lambda a, x: (x, x),
                     lambda x, g: _inv_unit_lower_bwd(x, g) + (tuple(jnp.zeros_like(e) for e in x),))


def mm(a, b, *, tb=False, add=None, out_dtype=F32, name):
    m, kdim = a.shape
    (n, kb) = b.shape if tb else b.shape[::-1]
    assert kdim == kb, (a.shape, b.shape, tb)
    tm = _tile(m, 1408, 128) if m % 128 == 0 and m <= 4096 else _tile(m, 640, 64)
    tn = _tile(n, 3072 if (kdim <= 1024 and not tb and add is None) else 1536, 128)
    tk = kdim if kdim <= 3072 else _tile(kdim, 1664 if a.dtype == b.dtype == BF16 else 640, 128)
    nk = kdim // tk
    dims = (((1,), (1 if tb else 0,)), ((), ()))

    def body(*refs):
        if add is not None:
            a_ref, b_ref, add_ref, o_ref, acc = refs
        else:
            a_ref, b_ref, o_ref, acc = refs
        k = pl.program_id(2)
        part = lax.dot_general(a_ref[...].astype(BF16), b_ref[...].astype(BF16), dims, preferred_element_type=F32)

        def finish(r):
            if add is not None:
                r = r + add_ref[...]
            o_ref[...] = r.astype(out_dtype)

        if nk == 1:
            finish(part)
        else:
            @pl.when(k == 0)
            def _():
                acc[...] = part

            @pl.when(jnp.logical_and(k > 0, k < nk - 1))
            def _():
                acc[...] += part

            @pl.when(k == nk - 1)
            def _():
                finish(acc[...] + part)

    a_spec = pl.BlockSpec((tm, tk), lambda j, i, k: (i, k))
    b_spec = pl.BlockSpec((tn, tk), lambda j, i, k: (j, k)) if tb else pl.BlockSpec((tk, tn), lambda j, i, k: (k, j))
    in_specs = [a_spec, b_spec]
    args = [a, b]
    if add is not None:
        in_specs.append(pl.BlockSpec((tm, tn), lambda j, i, k: (i, j)))
        args.append(add)
    return pl.pallas_call(
        body, grid=(n // tn, m // tm, nk), in_specs=in_specs,
        out_specs=pl.BlockSpec((tm, tn), lambda j, i, k: (i, j)),
        out_shape=jax.ShapeDtypeStruct((m, n), out_dtype),
        scratch_shapes=[pltpu.VMEM((tm, tn) if nk > 1 else (8, 128), F32)],
        compiler_params=_params("parallel", "parallel", "arbitrary"), name=name,
    )(*args)


def _rms(x, gain):
    return x * lax.rsqrt(jnp.mean(x * x, axis=-1, keepdims=True) + NORM_EPS) * gain


def rms_fwd(h, gain, name):
    t = h.shape[0]
    ts = _tile(t, 640, 128)

    def body(h_ref, g_ref, u_ref, ut_ref):
        u = _rms(h_ref[...], g_ref[...])
        u_ref[...] = u.astype(BF16)
        ut_ref[...] = u.T.astype(BF16)

    return pl.pallas_call(
        body, grid=(t // ts,),
        in_specs=[pl.BlockSpec((ts, D_MODEL), lambda i: (i, 0)), pl.BlockSpec((1, D_MODEL), lambda i: (0, 0))],
        out_specs=(pl.BlockSpec((ts, D_MODEL), lambda i: (i, 0)), pl.BlockSpec((D_MODEL, ts), lambda i: (0, i))),
        out_shape=(jax.ShapeDtypeStruct((t, D_MODEL), BF16), jax.ShapeDtypeStruct((D_MODEL, t), BF16)),
        compiler_params=_params("parallel"), name=name,
    )(h, gain)


def rms_bwd(h, gain, du, dres, name):
    t = h.shape[0]
    ts = _tile(t, 640, 64)

    def body(h_ref, g_ref, du_ref, dres_ref, dh_ref, dg_ref):
        i = pl.program_id(0)
        _, vjp = jax.vjp(_rms, h_ref[...], g_ref[...])
        dx, dg = vjp(du_ref[...])
        row = i * ts + lax.broadcasted_iota(jnp.int32, (ts, 1), 0)
        dh_ref[...] = jnp.where(row >= LEAD, dx + dres_ref[...], 0.0)

        @pl.when(i == 0)
        def _():
            dg_ref[...] = jnp.zeros_like(dg_ref)

        dg_ref[...] += dg

    row_spec = pl.BlockSpec((ts, D_MODEL), lambda i: (i, 0))
    vec_spec = pl.BlockSpec((1, D_MODEL), lambda i: (0, 0))
    return pl.pallas_call(
        body, grid=(t // ts,), in_specs=[row_spec, vec_spec, row_spec, row_spec],
        out_specs=(row_spec, vec_spec),
        out_shape=(jax.ShapeDtypeStruct((t, D_MODEL), F32), jax.ShapeDtypeStruct((1, D_MODEL), F32)),
        compiler_params=_params("arbitrary"), name=name,
    )(h, gain, du, dres)


def loss_head(h, gain, target, name):
    t = h.shape[0]
    ts = _tile(t, 640, 64)

    def body(h_ref, g_ref, t_ref, loss_ref, dh_ref, dg_ref):
        i = pl.program_id(0)
        row = i * ts + lax.broadcasted_iota(jnp.int32, (ts, 1), 0)
        keep = jnp.logical_and(row >= LEAD + N_META, row < t - TAIL)
        tgt = t_ref[...]

        def f(x, g):
            err = jnp.where(keep, _rms(x, g) - tgt, 0.0)
            per_row = jnp.mean(err * err, axis=-1, keepdims=True)
            return 0.5 * jnp.sum(per_row, axis=0, keepdims=True)

        val, vjp = jax.vjp(f, h_ref[...], g_ref[...])
        dx, dg = vjp(jnp.ones((1, 1), F32))
        dh_ref[...] = dx

        @pl.when(i == 0)
        def _():
            dg_ref[...] = jnp.zeros_like(dg_ref)
            loss_ref[...] = jnp.zeros_like(loss_ref)

        dg_ref[...] += dg
        loss_ref[...] += jnp.broadcast_to(val, (1, 128))

    row_spec = pl.BlockSpec((ts, D_MODEL), lambda i: (i, 0))
    vec_spec = pl.BlockSpec((1, D_MODEL), lambda i: (0, 0))
    return pl.pallas_call(
        body, grid=(t // ts,), in_specs=[row_spec, vec_spec, row_spec],
        out_specs=(pl.BlockSpec((1, 128), lambda i: (0, 0)), row_spec, vec_spec),
        out_shape=(jax.ShapeDtypeStruct((1, 128), F32), jax.ShapeDtypeStruct((t, D_MODEL), F32),
                   jax.ShapeDtypeStruct((1, D_MODEL), F32)),
        compiler_params=_params("arbitrary"), name=name,
    )(h, gain, target)


def _rows_down(a, s):
    return a if s == 0 else pltpu.roll(a, s, axis=0)


def _rows_up(a, s):
    return a if s == 0 else pltpu.roll(a, a.shape[0] - s, axis=0)


def conv_fwd(x, w, name):
    t, width = x.shape
    k = w.shape[0]
    ts = _tile(t, 640, 64)
    tw = _tile(width, 1536, 128)
    hb = ts // 8

    def body(x_ref, halo_ref, w_ref, o_ref, buf):
        i = pl.program_id(0)
        buf[0:8, :] = jnp.where(i > 0, halo_ref[...], 0.0)
        buf[8:, :] = x_ref[...]
        ext = buf[...]
        wv = w_ref[...]
        acc = _rows_down(ext, k - 1)[8:, :] * wv[0:1, :]
        for j in range(1, k):
            acc = acc + _rows_down(ext, k - 1 - j)[8:, :] * wv[j:j + 1, :]
        o_ref[...] = acc

    return pl.pallas_call(
        body, grid=(t // ts, width // tw),
        in_specs=[pl.BlockSpec((ts, tw), lambda i, j: (i, j)),
                  pl.BlockSpec((8, tw), lambda i, j: (jnp.maximum(i * hb - 1, 0), j)),
                  pl.BlockSpec((k, tw), lambda i, j: (0, j))],
        out_specs=pl.BlockSpec((ts, tw), lambda i, j: (i, j)),
        out_shape=jax.ShapeDtypeStruct((t, width), F32),
        scratch_shapes=[pltpu.VMEM((ts + 8, tw), F32)],
        compiler_params=_params("parallel", "parallel"), name=name,
    )(x, x, w)


def conv_bwd(x, dc, w, name):
    t, width = x.shape
    k = w.shape[0]
    ts = _tile(t, 640, 64)
    tw = _tile(width, 1536, 128)
    hb = ts // 8
    nt = t // ts

    def body(x_ref, xh_ref, dc_ref, dch_ref, w_ref, dx_ref, dw_ref, xbuf, dbuf):
        i = pl.program_id(1)
        xbuf[0:8, :] = jnp.where(i > 0, xh_ref[...], 0.0)
        xbuf[8:, :] = x_ref[...]
        d = dc_ref[...]
        dbuf[0:ts, :] = d
        dbuf[ts:, :] = jnp.where(i < nt - 1, dch_ref[...], 0.0)
        wv = w_ref[...]
        ext_x, ext_d = xbuf[...], dbuf[...]
        acc = _rows_up(ext_d, k - 1)[0:ts, :] * wv[0:1, :]
        for j in range(1, k):
            acc = acc + _rows_up(ext_d, k - 1 - j)[0:ts, :] * wv[j:j + 1, :]
        dx_ref[...] = acc.astype(BF16)

        @pl.when(i == 0)
        def _():
            dw_ref[...] = jnp.zeros_like(dw_ref)

        for j in range(k):
            dw_ref[j:j + 1, :] += jnp.sum(d * _rows_down(ext_x, k - 1 - j)[8:, :], axis=0, keepdims=True)

    return pl.pallas_call(
        body, grid=(width // tw, nt),
        in_specs=[pl.BlockSpec((ts, tw), lambda j, i: (i, j)),
                  pl.BlockSpec((8, tw), lambda j, i: (jnp.maximum(i * hb - 1, 0), j)),
                  pl.BlockSpec((ts, tw), lambda j, i: (i, j)),
                  pl.BlockSpec((8, tw), lambda j, i: (jnp.minimum((i + 1) * hb, t // 8 - 1), j)),
                  pl.BlockSpec((k, tw), lambda j, i: (0, j))],
        out_specs=(pl.BlockSpec((ts, tw), lambda j, i: (i, j)), pl.BlockSpec((8, tw), lambda j, i: (0, j))),
        out_shape=(jax.ShapeDtypeStruct((t, width), BF16), jax.ShapeDtypeStruct((8, width), F32)),
        scratch_shapes=[pltpu.VMEM((ts + 8, tw), F32), pltpu.VMEM((ts + 8, tw), F32)],
        compiler_params=_params("parallel", "arbitrary"), name=name,
    )(x, x, dc, dc, w)


def _pool_count(pos, win):
    return jnp.clip(pos + 1, 1, win).astype(F32)


def poolwin_fwd(p, name):
    t = p.shape[0]
    ts = _tile(t, 640, 64)
    hb = ts // 16

    def body(p_ref, halo_ref, o_ref, buf):
        i = pl.program_id(0)
        buf[0:16, :] = jnp.where(i > 0, halo_ref[...], 0.0)
        buf[16:, :] = p_ref[...]
        pos = i * ts + lax.broadcasted_iota(jnp.int32, (ts, 1), 0) - LEAD
        ext = buf[...]
        own = ext[16:, :]
        sums, span = ext, 1
        for gi, win in enumerate(POOL_WINDOWS):
            while span < win:
                sums = sums + _rows_down(sums, span)
                span *= 2
            cols = slice(gi * 128, (gi + 1) * 128)
            o_ref[:, cols] = sums[16:, cols] / _pool_count(pos, win) - own[:, cols]

    return pl.pallas_call(
        body, grid=(t // ts,),
        in_specs=[pl.BlockSpec((ts, POOL_WIDTH), lambda i: (i, 0)),
                  pl.BlockSpec((16, POOL_WIDTH), lambda i: (jnp.maximum(i * hb - 1, 0), 0))],
        out_specs=pl.BlockSpec((ts, POOL_WIDTH), lambda i: (i, 0)),
        out_shape=jax.ShapeDtypeStruct((t, POOL_WIDTH), F32),
        scratch_shapes=[pltpu.VMEM((ts + 16, POOL_WIDTH), F32)],
        compiler_params=_params("parallel"), name=name,
    )(p, p)


def poolwin_bwd(dpooled, name):
    t = dpooled.shape[0]
    ts = _tile(t, 640, 64)
    hb = ts // 16
    nt = t // ts

    def body(d_ref, halo_ref, o_ref, buf):
        i = pl.program_id(0)
        buf[0:ts, :] = d_ref[...]
        buf[ts:, :] = jnp.where(i < nt - 1, halo_ref[...], 0.0)
        pos = i * ts + lax.broadcasted_iota(jnp.int32, (ts + 16, 1), 0) - LEAD
        ext = buf[...]
        for gi, win in enumerate(POOL_WINDOWS):
            cols = slice(gi * 128, (gi + 1) * 128)
            sums, span = ext[:, cols] / _pool_count(pos, win), 1
            while span < win:
                sums = sums + _rows_up(sums, span)
                span *= 2
            o_ref[:, cols] = (sums[0:ts, :] - ext[0:ts, cols]).astype(BF16)

    return pl.pallas_call(
        body, grid=(nt,),
        in_specs=[pl.BlockSpec((ts, POOL_WIDTH), lambda i: (i, 0)),
                  pl.BlockSpec((16, POOL_WIDTH), lambda i: (jnp.minimum((i + 1) * hb, t // 16 - 1), 0))],
        out_specs=pl.BlockSpec((ts, POOL_WIDTH), lambda i: (i, 0)),
        out_shape=jax.ShapeDtypeStruct((t, POOL_WIDTH), BF16),
        scratch_shapes=[pltpu.VMEM((ts + 16, POOL_WIDTH), F32)],
        compiler_params=_params("parallel"), name=name,
    )(dpooled, dpooled)


def _mix(y_a, gpre, pooled, w_pool, scale):
    parts = [_nn(pooled[:, g * 128:(g + 1) * 128], w_pool[g]) for g in range(4)]
    y_b = jnp.concatenate(parts, axis=1) * scale
    return jax.nn.sigmoid(gpre[:, :D_MODEL]) * y_a + jax.nn.sigmoid(gpre[:, D_MODEL:]) * y_b


def _mix_specs(ts):
    return [pl.BlockSpec((ts, D_MODEL), lambda i: (i, 0)), pl.BlockSpec((ts, 2 * D_MODEL), lambda i: (i, 0)),
            pl.BlockSpec((ts, POOL_WIDTH), lambda i: (i, 0)), pl.BlockSpec((4, 128, 256), lambda i: (0, 0, 0)),
            pl.BlockSpec((1, D_MODEL), lambda i: (0, 0))]


def mix_fwd(y_a, gpre, pooled, w_pool, scale, name):
    t = y_a.shape[0]
    ts = _tile(t, 640, 128)

    def body(ya_ref, g_ref, p_ref, w_ref, s_ref, o_ref, ot_ref):
        y = _mix(ya_ref[...], g_ref[...], p_ref[...], w_ref[...], s_ref[...])
        o_ref[...] = y.astype(BF16)
        ot_ref[...] = y.T.astype(BF16)

    return pl.pallas_call(
        body, grid=(t // ts,), in_specs=_mix_specs(ts),
        out_specs=(pl.BlockSpec((ts, D_MODEL), lambda i: (i, 0)), pl.BlockSpec((D_MODEL, ts), lambda i: (0, i))),
        out_shape=(jax.ShapeDtypeStruct((t, D_MODEL), BF16), jax.ShapeDtypeStruct((D_MODEL, t), BF16)),
        compiler_params=_params("parallel"), name=name,
    )(y_a, gpre, pooled, w_pool, scale)


def mix_bwd(y_a, gpre, pooled, w_pool, scale, dy, name):
    t = y_a.shape[0]
    ts = _tile(t, 320, 64)

    def body(ya_ref, g_ref, p_ref, w_ref, s_ref, dy_ref, dya_ref, dg_ref, dp_ref, dw_ref, ds_ref):
        i = pl.program_id(0)
        _, vjp = jax.vjp(_mix, ya_ref[...], g_ref[...], p_ref[...], w_ref[...], s_ref[...])
        dya, dg, dp, dw, ds = vjp(dy_ref[...])
        dya_ref[...] = dya
        dg_ref[...] = dg.astype(BF16)
        dp_ref[...] = dp

        @pl.when(i == 0)
        def _():
            dw_ref[...] = jnp.zeros_like(dw_ref)
            ds_ref[...] = jnp.zeros_like(ds_ref)

        dw_ref[...] += dw
        ds_ref[...] += ds

    specs = _mix_specs(ts)
    return pl.pallas_call(
        body, grid=(t // ts,), in_specs=specs + [specs[0]],
        out_specs=(specs[0], specs[1], specs[2], specs[3], specs[4]),
        out_shape=(jax.ShapeDtypeStruct((t, D_MODEL), F32), jax.ShapeDtypeStruct((t, 2 * D_MODEL), BF16),
                   jax.ShapeDtypeStruct((t, POOL_WIDTH), F32), jax.ShapeDtypeStruct((4, 128, 256), F32),
                   jax.ShapeDtypeStruct((1, D_MODEL), F32)),
        compiler_params=_params("arbitrary"), name=name,
    )(y_a, gpre, pooled, w_pool, scale, dy)


def _ffn_act(cg, cv):
    return _silu(cg) * cv


def ffnact_fwd(cg, cv, name):
    t, width = cg.shape
    ts = _tile(t, 640, 128)
    tw = _tile(width, 1536, 128)
    spec = pl.BlockSpec((ts, tw), lambda i, j: (i, j))

    def body(g_ref, v_ref, o_ref, ot_ref):
        act = _ffn_act(g_ref[...], v_ref[...])
        o_ref[...] = act.astype(BF16)
        ot_ref[...] = act.T.astype(BF16)

    return pl.pallas_call(
        body, grid=(t // ts, width // tw), in_specs=[spec, spec],
        out_specs=(spec, pl.BlockSpec((tw, ts), lambda i, j: (j, i))),
        out_shape=(jax.ShapeDtypeStruct((t, width), BF16), jax.ShapeDtypeStruct((width, t), BF16)),
        compiler_params=_params("parallel", "parallel"), name=name,
    )(cg, cv)


def ffn_mid_bwd(hg, hv, cg, cv, dact, wg, wv, name):
    t, width = hg.shape
    k = wg.shape[0]
    ts = _tile(t, 320, 64)
    tw = _tile(width, 1536, 128)
    hb = ts // 8
    nt = t // ts

    def body(hg_ref, hgp_ref, hv_ref, hvp_ref, cg_ref, cgn_ref, cv_ref, cvn_ref, da_ref, dan_ref, wg_ref, wv_ref,
             dhg_ref, dhv_ref, dwg_ref, dwv_ref, xg, xv, dg, dv):
        i = pl.program_id(1)
        behind = i < nt - 1

        def d_conv(c_g, c_v, d_a):
            _, vjp = jax.vjp(_ffn_act, c_g, c_v)
            return vjp(d_a)

        dcg, dcv = d_conv(cg_ref[...], cv_ref[...], da_ref[...])
        dcg_n, dcv_n = d_conv(cgn_ref[...], cvn_ref[...], jnp.where(behind, dan_ref[...], 0.0))

        @pl.when(i == 0)
        def _():
            dwg_ref[...] = jnp.zeros_like(dwg_ref)
            dwv_ref[...] = jnp.zeros_like(dwv_ref)

        for x_ref, xp_ref, xbuf, dbuf, d, d_n, w_ref, dx_ref, dw_ref in (
                (hg_ref, hgp_ref, xg, dg, dcg, dcg_n, wg_ref, dhg_ref, dwg_ref),
                (hv_ref, hvp_ref, xv, dv, dcv, dcv_n, wv_ref, dhv_ref, dwv_ref)):
            xbuf[0:8, :] = jnp.where(i > 0, xp_ref[...], 0.0)
            xbuf[8:, :] = x_ref[...]
            dbuf[0:ts, :] = d
            dbuf[ts:, :] = jnp.where(behind, d_n, 0.0)
            wt = w_ref[...]
            ext_x, ext_d = xbuf[...], dbuf[...]
            acc = _rows_up(ext_d, k - 1)[0:ts, :] * wt[0:1, :]
            for j in range(1, k):
                acc = acc + _rows_up(ext_d, k - 1 - j)[0:ts, :] * wt[j:j + 1, :]
            dx_ref[...] = acc.astype(BF16)
            for j in range(k):
                dw_ref[j:j + 1, :] += jnp.sum(d * _rows_down(ext_x, k - 1 - j)[8:, :], axis=0, keepdims=True)

    tile = pl.BlockSpec((ts, tw), lambda j, i: (i, j))
    prev = pl.BlockSpec((8, tw), lambda j, i: (jnp.maximum(i * hb - 1, 0), j))
    nxt = pl.BlockSpec((8, tw), lambda j, i: (jnp.minimum((i + 1) * hb, t // 8 - 1), j))
    taps = pl.BlockSpec((k, tw), lambda j, i: (0, j))
    dw_spec = pl.BlockSpec((8, tw), lambda j, i: (0, j))
    return pl.pallas_call(
        body, grid=(width // tw, nt),
        in_specs=[tile, prev, tile, prev, tile, nxt, tile, nxt, tile, nxt, taps, taps],
        out_specs=(tile, tile, dw_spec, dw_spec),
        out_shape=(jax.ShapeDtypeStruct((t, width), BF16), jax.ShapeDtypeStruct((t, width), BF16),
                   jax.ShapeDtypeStruct((8, width), F32), jax.ShapeDtypeStruct((8, width), F32)),
        scratch_shapes=[pltpu.VMEM((ts + 8, tw), F32)] * 4,
        compiler_params=_params("parallel", "arbitrary"), name=name,
    )(hg, hg, hv, hv, cg, cg, cv, cv, dact, dact, wg, wv)


def _gdn_chunk(c, z, ba, pa, pdt, hn, s, *, valid, inverse=None, with_inverse=False):
    r = lax.broadcasted_iota(jnp.int32, (CHUNK, CHUNK), 0)
    q_ = lax.broadcasted_iota(jnp.int32, (CHUNK, CHUNK), 1)
    causal = r >= q_
    strict = r > q_
    tril = causal.astype(F32)
    triu = (r <= q_).astype(F32)
    lane = lax.broadcasted_iota(jnp.int32, (CHUNK, 128), 1)

    decay_log = -jnp.exp(pa) * _softplus(ba + pdt)
    bg = jnp.where(lane < HEADS, jax.nn.sigmoid(ba), jnp.where(lane < 2 * HEADS, decay_log, 0.0))
    bg = jnp.where(valid, bg, 0.0)
    gc = _hnn(tril, bg)
    gct = _hnn(bg.T, triu)
    eg = jnp.exp(gc)
    glast = gc[CHUNK - 1:CHUNK, :]
    ekd = jnp.exp(glast - gc)
    gtot = jnp.exp(glast)

    hd = range(HEADS)
    hs = [slice(h * HEAD_DIM, (h + 1) * HEAD_DIM) for h in hd]
    gl = [slice(HEADS + h, HEADS + h + 1) for h in hd]
    q = [_silu(c[:, hs[h]]) for h in hd]
    k = [_silu(c[:, D_MODEL + h * HEAD_DIM:D_MODEL + (h + 1) * HEAD_DIM]) for h in hd]
    v = [_silu(c[:, 2 * D_MODEL + h * HEAD_DIM:2 * D_MODEL + (h + 1) * HEAD_DIM]) for h in hd]
    q = [q[h] * lax.rsqrt(jnp.sum(q[h] * q[h], axis=-1, keepdims=True) + NORM_EPS) * (HEAD_DIM ** -0.5) for h in hd]
    k = [k[h] * lax.rsqrt(jnp.sum(k[h] * k[h], axis=-1, keepdims=True) + NORM_EPS) for h in hd]
    beta = [bg[:, h:h + 1] for h in hd]
    decay = [jnp.exp(jnp.where(causal, gc[:, gl[h]] - gct[gl[h], :], -1e30)) for h in hd]
    kb = [k[h] * beta[h] for h in hd]
    a = [jnp.where(strict, _nt(kb[h], k[h]) * decay[h], 0.0) for h in hd]
    qk = [jnp.where(causal, _nt(q[h], k[h]) * decay[h], 0.0) for h in hd]
    x = _inv_unit_lower(tuple(a)) if inverse is None else _kept_inverse(tuple(a), tuple(inverse))
    u = [_nn(x[h], v[h] * beta[h]) for h in hd]
    w = [_nn(x[h], kb[h] * eg[:, gl[h]]) for h in hd]
    v_new = [u[h] - _nn(w[h], s[h]) for h in hd]
    o = [_nn(q[h] * eg[:, gl[h]], s[h]) + _nn(qk[h], v_new[h]) for h in hd]
    states = [s[h] * gtot[:, gl[h]] + _tn(k[h] * ekd[:, gl[h]], v_new[h]) for h in hd]
    o = [o[h] * lax.rsqrt(jnp.mean(o[h] * o[h], axis=-1, keepdims=True) + NORM_EPS) * hn * _silu(z[:, hs[h]])
         for h in hd]
    if with_inverse:
        return jnp.concatenate(o, axis=1), tuple(states), x
    return jnp.concatenate(o, axis=1), tuple(states)


GDN_FWD_CHUNKS = 5
GDN_BWD_CHUNKS = 2


def _chunk_valid(n, t):
    row = n * CHUNK + lax.broadcasted_iota(jnp.int32, (CHUNK, 1), 0)
    return jnp.logical_and(row >= LEAD, row < t - TAIL)


def gdn_fwd(c, z, ba, pa, pdt, hn, name, ba_block=0):
    t = c.shape[0]
    n_chunks = t // CHUNK
    per_step = GDN_FWD_CHUNKS if n_chunks % GDN_FWD_CHUNKS == 0 else 1
    rows_per_step = per_step * CHUNK

    def body(c_ref, z_ref, ba_ref, pa_ref, pdt_ref, hn_ref, y_ref, ss_ref, inv_ref, state):
        step = pl.program_id(0)

        @pl.when(step == 0)
        def _():
            state[...] = jnp.zeros_like(state)

        s = tuple(state[h] for h in range(HEADS))
        for j in range(per_step):
            rows = pl.ds(j * CHUNK, CHUNK)
            for h in range(HEADS):
                ss_ref[j, h] = s[h]
            y, s, inv = _gdn_chunk(c_ref[rows, :], z_ref[rows, :], ba_ref[rows, :], pa_ref[...], pdt_ref[...],
                                   hn_ref[...], s, valid=_chunk_valid(step * per_step + j, t), with_inverse=True)
            y_ref[rows, :] = y
            for h in range(HEADS):
                inv_ref[j, h] = inv[h]
        for h in range(HEADS):
            state[h] = s[h]

    vec = pl.BlockSpec((1, 128), lambda n: (0, 0))
    return pl.pallas_call(
        body, grid=(n_chunks // per_step,),
        in_specs=[pl.BlockSpec((rows_per_step, QKV_DIM), lambda n: (n, 0)),
                  pl.BlockSpec((rows_per_step, D_MODEL), lambda n: (n, 0)),
                  pl.BlockSpec((rows_per_step, 128), lambda n: (n, ba_block)), vec, vec, vec],
        out_specs=(pl.BlockSpec((rows_per_step, D_MODEL), lambda n: (n, 0)),
                   pl.BlockSpec((per_step, HEADS, HEAD_DIM, HEAD_DIM), lambda n: (n, 0, 0, 0)),
                   pl.BlockSpec((per_step, HEADS, CHUNK, CHUNK), lambda n: (n, 0, 0, 0))),
        out_shape=(jax.ShapeDtypeStruct((t, D_MODEL), F32),
                   jax.ShapeDtypeStruct((n_chunks, HEADS, HEAD_DIM, HEAD_DIM), F32),
                   jax.ShapeDtypeStruct((n_chunks, HEADS, CHUNK, CHUNK), F32)),
        scratch_shapes=[pltpu.VMEM((HEADS, HEAD_DIM, HEAD_DIM), F32)],
        compiler_params=_params("arbitrary"), name=name,
    )(c, z, ba, pa, pdt, hn)


def gdn_bwd(c, z, ba, pa, pdt, hn, starts, inverses, dy, name, ba_block=0):
    t = c.shape[0]
    per_step = GDN_BWD_CHUNKS if (t // CHUNK) % GDN_BWD_CHUNKS == 0 else 1
    n_steps = t // CHUNK // per_step
    rows_per_step = per_step * CHUNK

    def body(c_ref, z_ref, ba_ref, pa_ref, pdt_ref, hn_ref, ss_ref, inv_ref, dy_ref,
             dc_ref, dz_ref, dba_ref, dpa_ref, dpdt_ref, dhn_ref, dstate):
        step = pl.program_id(0)

        @pl.when(step == 0)
        def _():
            dstate[...] = jnp.zeros_like(dstate)
            dpa_ref[...] = jnp.zeros_like(dpa_ref)
            dpdt_ref[...] = jnp.zeros_like(dpdt_ref)
            dhn_ref[...] = jnp.zeros_like(dhn_ref)

        ds = tuple(dstate[h] for h in range(HEADS))
        for j in reversed(range(per_step)):
            rows = pl.ds(j * CHUNK, CHUNK)
            f = functools.partial(_gdn_chunk, valid=_chunk_valid((n_steps - 1 - step) * per_step + j, t),
                                  inverse=tuple(inv_ref[j, h] for h in range(HEADS)))
            _, vjp = jax.vjp(f, c_ref[rows, :], z_ref[rows, :], ba_ref[rows, :], pa_ref[...], pdt_ref[...], hn_ref[...],
                             tuple(ss_ref[j, h] for h in range(HEADS)))
            dc, dz, dba, dpa, dpdt, dhn, ds = vjp((dy_ref[rows, :], ds))
            dc_ref[rows, :] = dc
            dz_ref[rows, :] = dz.astype(BF16)
            dba_ref[rows, :] = dba.astype(BF16)
            dpa_ref[...] += dpa
            dpdt_ref[...] += dpdt
            dhn_ref[...] += dhn
        for h in range(HEADS):
            dstate[h] = ds[h]

    def rev(width, block=0):
        return pl.BlockSpec((rows_per_step, width), lambda s: (n_steps - 1 - s, block))

    vec = pl.BlockSpec((1, 128), lambda s: (0, 0))
    vec_shape = jax.ShapeDtypeStruct((1, 128), F32)
    return pl.pallas_call(
        body, grid=(n_steps,),
        in_specs=[rev(QKV_DIM), rev(D_MODEL), rev(128, ba_block), vec, vec, vec,
                  pl.BlockSpec((per_step, HEADS, HEAD_DIM, HEAD_DIM), lambda s: (n_steps - 1 - s, 0, 0, 0)),
                  pl.BlockSpec((per_step, HEADS, CHUNK, CHUNK), lambda s: (n_steps - 1 - s, 0, 0, 0)),
                  rev(D_MODEL)],
        out_specs=(rev(QKV_DIM), rev(D_MODEL), rev(128), vec, vec, vec),
        out_shape=(jax.ShapeDtypeStruct((t, QKV_DIM), F32), jax.ShapeDtypeStruct((t, D_MODEL), BF16),
                   jax.ShapeDtypeStruct((t, 128), BF16), vec_shape, vec_shape, vec_shape),
        scratch_shapes=[pltpu.VMEM((HEADS, HEAD_DIM, HEAD_DIM), F32)],
        compiler_params=_params("arbitrary"), name=name,
    )(c, z, ba, pa, pdt, hn, starts, inverses, dy)


def _layer_fwd(h, w, tag):
    u, ut = rms_fwd(h, w["norm_mix"], f"{tag}_rms_mix")
    pq = mm(u, w["wqkv"], name=f"{tag}_mm_qkv")
    pz = mm(u, w["wz"], name=f"{tag}_mm_z")
    pg = mm(u, w["wg"], name=f"{tag}_mm_gate")
    pba = mm(u, w["wpb"], name=f"{tag}_mm_pool_ba")
    cq = conv_fwd(pq, w["conv_qkv"], f"{tag}_conv_qkv")
    ya, starts, inverses = gdn_fwd(cq, pz, pba, w["pa"], w["pdt"], w["head_norm"], f"{tag}_gdn", ba_block=BA_BLOCK)
    pooled = poolwin_fwd(pba, f"{tag}_poolwin")
    y, yt = mix_fwd(ya, pg, pooled, w["w_pool"], w["pool_scale"], f"{tag}_mix")
    h1 = mm(y, w["wout"], add=h, name=f"{tag}_mm_out")
    u2, u2t = rms_fwd(h1, w["norm_ffn"], f"{tag}_rms_ffn")
    hg = mm(u2, w["wupg"], name=f"{tag}_mm_upg")
    hv = mm(u2, w["wupv"], name=f"{tag}_mm_upv")
    cg = conv_fwd(hg, w["conv_g"], f"{tag}_conv_g")
    cv = conv_fwd(hv, w["conv_v"], f"{tag}_conv_v")
    act, actt = ffnact_fwd(cg, cv, f"{tag}_act")
    h2 = mm(act, w["wdown"], add=h1, name=f"{tag}_mm_down")
    saved = dict(h=h, ut=ut, pq=pq, pz=pz, pg=pg, pba=pba, cq=cq, ya=ya, starts=starts, inverses=inverses, pooled=pooled, yt=yt, h1=h1,
                 u2t=u2t, hg=hg, hv=hv, cg=cg, cv=cv, actt=actt)
    return h2, saved


def _layer_bwd(dh2, w, s, tag):
    g = {}
    dact = mm(dh2, w["wdown"], tb=True, name=f"{tag}_bmm_down_x")
    g["wdown"] = mm(s["actt"], dh2, name=f"{tag}_bmm_down_w")
    dhg, dhv, g["conv_g"], g["conv_v"] = ffn_mid_bwd(s["hg"], s["hv"], s["cg"], s["cv"], dact, w["conv_g"], w["conv_v"],
                                                     f"{tag}_ffn_mid_b")
    du2 = mm(dhg, w["wupg"], tb=True, name=f"{tag}_bmm_upg_x")
    du2 = mm(dhv, w["wupv"], tb=True, add=du2, name=f"{tag}_bmm_upv_x")
    g["wupg"] = mm(s["u2t"], dhg, name=f"{tag}_bmm_upg_w")
    g["wupv"] = mm(s["u2t"], dhv, name=f"{tag}_bmm_upv_w")
    dh1, g["norm_ffn"] = rms_bwd(s["h1"], w["norm_ffn"], du2, dh2, f"{tag}_rms_ffn_b")
    dy = mm(dh1, w["wout"], tb=True, name=f"{tag}_bmm_out_x")
    g["wout"] = mm(s["yt"], dh1, name=f"{tag}_bmm_out_w")
    dya, dpg, dpooled, g["w_pool"], g["pool_scale"] = mix_bwd(
        s["ya"], s["pg"], s["pooled"], w["w_pool"], w["pool_scale"], dy, f"{tag}_mix_b")
    dpp = poolwin_bwd(dpooled, f"{tag}_poolwin_b")
    dcq, dpz, dpba, g["pa"], g["pdt"], g["head_norm"] = gdn_bwd(
        s["cq"], s["pz"], s["pba"], w["pa"], w["pdt"], w["head_norm"], s["starts"], s["inverses"], dya, f"{tag}_gdn_b",
        ba_block=BA_BLOCK)
    dpb = jnp.concatenate([dpp, dpba], axis=1)
    dpq, g["conv_qkv"] = conv_bwd(s["pq"], dcq, w["conv_qkv"], f"{tag}_conv_qkv_b")
    du = mm(dpq, w["wqkv"], tb=True, name=f"{tag}_bmm_qkv_x")
    du = mm(dpz, w["wz"], tb=True, add=du, name=f"{tag}_bmm_z_x")
    du = mm(dpg, w["wg"], tb=True, add=du, name=f"{tag}_bmm_gate_x")
    du = mm(dpb, w["wpb"], tb=True, add=du, name=f"{tag}_bmm_pool_ba_x")
    g["wqkv"] = mm(s["ut"], dpq, name=f"{tag}_bmm_qkv_w")
    g["wz"] = mm(s["ut"], dpz, name=f"{tag}_bmm_z_w")
    g["wg"] = mm(s["ut"], dpg, name=f"{tag}_bmm_gate_w")
    dwpb = mm(s["ut"], dpb, name=f"{tag}_bmm_pool_ba_w")
    g["wpl"], g["wba"] = dwpb[:, :POOL_WIDTH], dwpb[:, POOL_WIDTH:]
    dh, g["norm_mix"] = rms_bwd(s["h"], w["norm_mix"], du, dh1, f"{tag}_rms_mix_b")
    return dh, g


def local_step(h0, target, layers, norm_final):
    h = h0
    saved = []
    for li, w in enumerate(layers):
        h, s = _layer_fwd(h, w, f"l{li}")
        saved.append(s)
    loss, dh, dnf = loss_head(h, norm_final, target, "loss_head")
    grads = [None] * len(layers)
    for li in reversed(range(len(layers))):
        dh, grads[li] = _layer_bwd(dh, layers[li], saved[li], f"l{li}")
    return loss, dh, grads, dnf


_Z0, _B0, _P0, _G0, _IN_DIM = 3072, 4096, 4112, 4624, 6672


def _lanes_8_to_15(v):
    return jnp.pad(v.reshape(1, HEADS).astype(F32), ((0, 0), (HEADS, 128 - 2 * HEADS)))


IN_PIECES = (("wqkv", 0, _Z0), ("wz", _Z0, _B0), ("wba", _B0, _P0), ("wpl", _P0, _G0), ("wg", _G0, _IN_DIM))
IN_SHARD = _IN_DIM // 4


def _overlaps(a, b, spans):
    return [(name, max(a, lo) - lo, min(b, hi) - max(a, lo)) for name, lo, hi in spans if max(a, lo) < min(b, hi)]


def _cat(parts):
    return parts[0] if len(parts) == 1 else jnp.concatenate(parts, axis=1)


def prep_layer(p):
    row = lambda v: v.reshape(1, -1).astype(F32)
    w_in, w_up = p["w_in"], p["w_up"]
    if not isinstance(w_in, (list, tuple)):
        w_in = [w_in[:, s * IN_SHARD:(s + 1) * IN_SHARD] for s in range(4)]
        w_up = [w_up[:, s * (D_FF // 2):(s + 1) * (D_FF // 2)] for s in range(4)]
    shards = [(s, s * IN_SHARD, (s + 1) * IN_SHARD) for s in range(4)]
    piece = {name: _cat([w_in[s][:, off:off + width].astype(BF16) for s, off, width in _overlaps(lo, hi, shards)])
             for name, lo, hi in IN_PIECES}
    return dict(
        wqkv=piece["wqkv"], wz=piece["wz"],
        wpb=jnp.concatenate([piece["wpl"], jnp.pad(piece["wba"], ((0, 0), (0, 128 - 2 * HEADS)))], axis=1),
        wg=piece["wg"], wout=p["w_out"].astype(BF16),
        wupg=_cat([w_up[0].astype(BF16), w_up[1].astype(BF16)]), wupv=_cat([w_up[2].astype(BF16), w_up[3].astype(BF16)]),
        wdown=p["w_down"].astype(BF16),
        conv_qkv=p["conv_qkv"].astype(F32), conv_g=p["conv_ffn"][:, :D_FF].astype(F32),
        conv_v=p["conv_ffn"][:, D_FF:].astype(F32), w_pool=p["w_pool"].astype(F32),
        pool_scale=row(p["pool_scale"]), head_norm=row(p["head_norm"]), norm_mix=row(p["norm_mix"]),
        norm_ffn=row(p["norm_ffn"]), pa=_lanes_8_to_15(p["a_log"]), pdt=_lanes_8_to_15(p["dt_bias"]))


def layer_grads(g):
    return dict(
        w_in=jnp.concatenate([g["wqkv"], g["wz"], g["wba"][:, :2 * HEADS], g["wpl"], g["wg"]], axis=1),
        conv_qkv=g["conv_qkv"][:4], a_log=g["pa"][0, HEADS:2 * HEADS], dt_bias=g["pdt"][0, HEADS:2 * HEADS],
        head_norm=g["head_norm"][0], w_pool=g["w_pool"], pool_scale=g["pool_scale"][0], w_out=g["wout"],
        norm_mix=g["norm_mix"][0], norm_ffn=g["norm_ffn"][0],
        w_up=jnp.concatenate([g["wupg"], g["wupv"]], axis=1),
        conv_ffn=jnp.concatenate([g["conv_g"][:3], g["conv_v"][:3]], axis=1), w_down=g["wdown"])


def big_grad_shards(g):
    in_shards = [_cat([g[name][:, off:off + width] for name, off, width in
                       _overlaps(s * IN_SHARD, (s + 1) * IN_SHARD, IN_PIECES)]) for s in range(4)]
    half = D_FF // 2
    up_shards = [g["wupg"][:, :half], g["wupg"][:, half:], g["wupv"][:, :half], g["wupv"][:, half:]]
    return dict(w_in=jnp.stack(in_shards), w_up=jnp.stack(up_shards),
                w_down=g["wdown"].reshape(4, D_FF // 4, D_MODEL), w_out=g["wout"].reshape(4, D_MODEL // 4, D_MODEL))


LAYER_PARAMS = ("norm_mix", "w_in", "conv_qkv", "a_log", "dt_bias", "head_norm", "w_pool", "pool_scale", "w_out",
                "norm_ffn", "w_up", "conv_ffn", "w_down")


def pad_rows(meta, x):
    return jnp.concatenate([jnp.zeros((LEAD, D_MODEL), F32), meta.astype(F32), x.astype(F32),
                            jnp.zeros((TAIL, D_MODEL), F32)], axis=0)


MESH = pl.DeviceIdType.MESH
ANY = pl.BlockSpec(memory_space=pl.ANY)


def _place():
    x, y, c = lax.axis_index("x"), lax.axis_index("y"), lax.axis_index("c")
    return x, y, c, [(1 - x, y), (x, 1 - y), (1 - x, 1 - y)]


def _my_chip():
    return 2 * lax.axis_index("x") + lax.axis_index("y")


def gather_shards(packs):
    n = len(packs)

    def body(*refs):
        p_refs, o_refs, (send_sems, recv_sems) = refs[:n], refs[n:2 * n], refs[2 * n:]
        x, y, c, chips = _place()

        def copy(a, k, chip, half, to, src=None):
            dst = o_refs[a].at[2 * chip[0] + chip[1], half]
            return pltpu.make_async_remote_copy(src_ref=dst if src is None else src, dst_ref=dst,
                                                send_sem=send_sems.at[6 * a + k], recv_sem=recv_sems.at[6 * a + k],
                                                device_id=to, device_id_type=MESH)

        first = [copy(a, j, (x, y), c, (*chip, c), src=p_refs[a].at[c]) for a in range(n) for j, chip in enumerate(chips)]
        for cp in first:
            cp.start()
        passed = []
        for a in range(n):
            for j, chip in enumerate(chips):
                copy(a, j, chip, c, (x, y, c)).wait_recv()
                passed.append(copy(a, 3 + j, chip, c, (x, y, 1 - c)))
                passed[-1].start()
        for a in range(n):
            for j, chip in enumerate(chips):
                copy(a, 3 + j, chip, 1 - c, (x, y, c)).wait_recv()
        for cp in first + passed:
            cp.wait_send()

    gathered = pl.pallas_call(
        body, in_specs=[ANY] * n, out_specs=[ANY] * n,
        out_shape=[jax.ShapeDtypeStruct((4,) + p.shape, p.dtype) for p in packs],
        scratch_shapes=[pltpu.SemaphoreType.DMA((6 * n,)), pltpu.SemaphoreType.DMA((6 * n,))],
        name="gather_shards",
    )(*packs)
    me = _my_chip()
    return [lax.dynamic_update_slice(g, p[None], (me,) + (0,) * p.ndim) for g, p in zip(gathered, packs)]


def swap_other_halves(ps):
    n = len(ps)

    def body(*refs):
        p_refs, o_refs, (send_sems, recv_sems) = refs[:n], refs[n:2 * n], refs[2 * n:]
        x, y, c, _ = _place()
        copies = [pltpu.make_async_remote_copy(src_ref=p_refs[a].at[s, 1 - c], dst_ref=o_refs[a].at[s],
                                               send_sem=send_sems.at[4 * a + s], recv_sem=recv_sems.at[4 * a + s],
                                               device_id=(x, y, 1 - c), device_id_type=MESH)
                  for a in range(n) for s in range(4)]
        for cp in copies:
            cp.start()
        for cp in copies:
            cp.wait()

    return pl.pallas_call(
        body, in_specs=[ANY] * n, out_specs=[ANY] * n,
        out_shape=[jax.ShapeDtypeStruct((4,) + p.shape[2:], p.dtype) for p in ps],
        scratch_shapes=[pltpu.SemaphoreType.DMA((4 * n,)), pltpu.SemaphoreType.DMA((4 * n,))],
        name="swap_other_halves",
    )(*ps)


def scatter_to_chips(qs):
    n = len(qs)

    def body(*refs):
        q_refs, o_refs, (send_sems, recv_sems) = refs[:n], refs[n:2 * n], refs[2 * n:]
        x, y, c, chips = _place()
        me = 2 * x + y
        copies = [pltpu.make_async_remote_copy(src_ref=q_refs[a].at[2 * chip[0] + chip[1]], dst_ref=o_refs[a].at[me],
                                               send_sem=send_sems.at[3 * a + j], recv_sem=recv_sems.at[3 * a + j],
                                               device_id=(*chip, c), device_id_type=MESH)
                  for a in range(n) for j, chip in enumerate(chips)]
        for cp in copies:
            cp.start()
        for a in range(n):
            for j, chip in enumerate(chips):
                slot = o_refs[a].at[2 * chip[0] + chip[1]]
                pltpu.make_async_remote_copy(src_ref=slot, dst_ref=slot, send_sem=send_sems.at[3 * a + j],
                                             recv_sem=recv_sems.at[3 * a + j],
                                             device_id=(x, y, c), device_id_type=MESH).wait_recv()
        for cp in copies:
            cp.wait_send()

    received = pl.pallas_call(
        body, in_specs=[ANY] * n, out_specs=[ANY] * n,
        out_shape=[jax.ShapeDtypeStruct(q.shape, q.dtype) for q in qs],
        scratch_shapes=[pltpu.SemaphoreType.DMA((3 * n,)), pltpu.SemaphoreType.DMA((3 * n,))],
        name="scatter_to_chips",
    )(*qs)
    me = _my_chip()
    return [lax.dynamic_update_slice(r, lax.dynamic_slice_in_dim(q, me, 1, axis=0), (me, 0, 0))
            for r, q in zip(received, qs)]


def join_halves(boths):
    n = len(boths)

    def body(*refs):
        o_refs, (send_sems, recv_sems) = refs[n:2 * n], refs[2 * n:]
        x, y, c, _ = _place()
        copies = [pltpu.make_async_remote_copy(src_ref=o_refs[a].at[c], dst_ref=o_refs[a].at[c],
                                               send_sem=send_sems.at[a], recv_sem=recv_sems.at[a],
                                               device_id=(x, y, 1 - c), device_id_type=MESH) for a in range(n)]
        for cp in copies:
            cp.start()
        for a in range(n):
            other = o_refs[a].at[1 - c]
            pltpu.make_async_remote_copy(src_ref=other, dst_ref=other, send_sem=send_sems.at[a],
                                         recv_sem=recv_sems.at[a], device_id=(x, y, c), device_id_type=MESH).wait_recv()
        for cp in copies:
            cp.wait_send()

    return pl.pallas_call(
        body, in_specs=[ANY] * n, out_specs=[ANY] * n,
        out_shape=[jax.ShapeDtypeStruct(b.shape, b.dtype) for b in boths],
        input_output_aliases={a: a for a in range(n)},
        scratch_shapes=[pltpu.SemaphoreType.DMA((n,)), pltpu.SemaphoreType.DMA((n,))], name="join_halves",
    )(*boths)


def add_own_half(p, other, c, out_dtype, name):
    _, _, rows, lanes = p.shape
    tr = _tile(rows, max(16, 524288 // lanes), 16)

    def body(c_ref, p_ref, o_ref, out_ref):
        out_ref[...] = (p_ref[...] + o_ref[...]).astype(out_dtype)

    return pl.pallas_call(
        body,
        grid_spec=pltpu.PrefetchScalarGridSpec(
            num_scalar_prefetch=1, grid=(4, rows // tr),
            in_specs=[pl.BlockSpec((None, None, tr, lanes), lambda s, i, c_ref: (s, c_ref[0], i, 0)),
                      pl.BlockSpec((None, tr, lanes), lambda s, i, c_ref: (s, i, 0))],
            out_specs=pl.BlockSpec((None, tr, lanes), lambda s, i, c_ref: (s, i, 0))),
        out_shape=jax.ShapeDtypeStruct((4, rows, lanes), out_dtype),
        compiler_params=_params("parallel", "parallel"), name=name,
    )(c, p, other)


def sum_chips(b, c, name):
    _, rows, lanes = b.shape
    tr = _tile(rows, max(16, 524288 // lanes), 16)

    def body(c_ref, b_ref, out_ref):
        b0, b1, b2, b3 = (b_ref[k].astype(F32) for k in range(4))
        out_ref[...] = ((b0 + b1) + b2) + b3

    return pl.pallas_call(
        body,
        grid_spec=pltpu.PrefetchScalarGridSpec(
            num_scalar_prefetch=1, grid=(rows // tr,),
            in_specs=[pl.BlockSpec((4, tr, lanes), lambda i, c_ref: (0, i, 0))],
            out_specs=pl.BlockSpec((None, tr, lanes), lambda i, c_ref: (c_ref[0], i, 0))),
        out_shape=jax.ShapeDtypeStruct((2, rows, lanes), F32),
        compiler_params=_params("parallel"), name=name,
    )(c, b)


def all_reduce_to_shards(packs, wires, tags, c):
    others = swap_other_halves(packs)
    qs = [add_own_half(p, o, c, wire, f"add_own_half_{tag}") for p, o, wire, tag in zip(packs, others, wires, tags)]
    return join_halves([sum_chips(r, c, f"sum_chips_{tag}") for r, tag in zip(scatter_to_chips(qs), tags)])


def adamw(w, g, m, v, name):
    shape = w.shape
    cols = shape[-1]
    w2, g2, m2, v2 = (a.reshape(-1, cols) for a in (w, g, m, v))
    rows = w2.shape[0]
    tr = _tile(rows, max(8, 262144 // cols), 8) if rows % 8 == 0 else rows
    c1 = 1.0 - ADAM_B1 ** ADAM_STEP
    c2 = 1.0 - ADAM_B2 ** ADAM_STEP

    def body(w_ref, g_ref, m_ref, v_ref, d_ref, mo_ref, vo_ref):
        gv = g_ref[...]
        mn = ADAM_B1 * m_ref[...] + (1.0 - ADAM_B1) * gv
        vn = ADAM_B2 * v_ref[...] + (1.0 - ADAM_B2) * jnp.square(gv)
        d_ref[...] = -ADAM_LR * ((mn / c1) / (jnp.sqrt(vn / c2) + ADAM_EPS) + ADAM_WD * w_ref[...])
        mo_ref[...] = mn
        vo_ref[...] = vn

    spec = pl.BlockSpec((tr, cols), lambda i: (i, 0))
    out = jax.ShapeDtypeStruct((rows, cols), F32)
    d, mn, vn = pl.pallas_call(
        body, grid=(rows // tr,), in_specs=[spec] * 4, out_specs=(spec,) * 3, out_shape=(out,) * 3,
        compiler_params=_params("parallel"), name=name,
    )(w2, g2, m2, v2)
    return d.reshape(shape), mn.reshape(shape), vn.reshape(shape)


BIG = ("w_in", "w_up", "w_down", "w_out")
SMALL = ("w_pool", "conv_qkv", "conv_ffn", "meta_tokens")
SHARDED = BIG + SMALL
MATMUL_WEIGHTS = BIG + ("w_pool",)
REPLICATED = ("norm_mix", "a_log", "dt_bias", "head_norm", "pool_scale", "norm_ffn", "norm_final")
SHARD_AXIS = {"w_in": 2, "w_up": 2, "w_out": 1, "w_down": 1, "w_pool": 3, "conv_qkv": 2, "conv_ffn": 2, "meta_tokens": 1}


def _rows_of(a):
    return a.reshape(-1, 128)


SEGMENT_ROWS = 16


def _segment(n_rows):
    return -(-n_rows // SEGMENT_ROWS) * SEGMENT_ROWS


def _pad_segment(a):
    pad = [(0, 0)] * a.ndim
    pad[-2] = (0, _segment(a.shape[-2]) - a.shape[-2])
    return jnp.pad(a, pad)


def _unshard(stacked, axis):
    full = jnp.moveaxis(stacked, 0, axis)
    shape = list(full.shape)
    shape[axis:axis + 2] = [shape[axis] * shape[axis + 1]]
    return full.reshape(shape)


def _shard_stack(full, axis):
    shape = list(full.shape)
    shape[axis:axis + 1] = [4, shape[axis] // 4]
    return jnp.moveaxis(full.reshape(shape), axis, 0)


def pack_weights(shards):
    parts = [_rows_of(shards[k].astype(WIRE)) if k in MATMUL_WEIGHTS else
             lax.bitcast_convert_type(_rows_of(shards[k].astype(F32)), WIRE).reshape(-1, 128) for k in SMALL]
    parts = [_pad_segment(p) for p in parts]
    rows = sum(p.shape[0] for p in parts)
    if rows % (2 * SEGMENT_ROWS):
        parts.append(jnp.zeros((SEGMENT_ROWS, 128), WIRE))
        rows += SEGMENT_ROWS
    return [shards[k].astype(WIRE) for k in BIG] + [jnp.concatenate(parts, axis=0).reshape(2, rows // 2, 128)]


def unpack_weights(gathered, shard_shapes):
    out = {k: _unshard(g, SHARD_AXIS[k]) for k, g in zip(BIG, gathered)}
    flat = gathered[-1].reshape(4, -1, 128)
    at = 0
    for k in SMALL:
        shp = shard_shapes[k]
        n = 1
        for e in shp:
            n *= e
        if k in MATMUL_WEIGHTS:
            r = n // 128
            stacked = flat[:, at:at + r].reshape((4,) + tuple(shp))
        else:
            r = 2 * n // 128
            stacked = lax.bitcast_convert_type(flat[:, at:at + r].reshape(4, n // 128, 128, 2), F32).reshape((4,) + tuple(shp))
        out[k] = _unshard(stacked, SHARD_AXIS[k])
        at += _segment(r)
    return out


def pack_grads(big, full, repl):
    r = jnp.concatenate([repl[k].reshape(-1) for k in REPLICATED])
    r = jnp.pad(r, (0, -r.shape[0] % 128)).reshape(1, -1, 128)
    parts = [_shard_stack(full[k], SHARD_AXIS[k]).reshape(4, -1, 128) for k in SMALL]
    parts = [_pad_segment(p) for p in parts + [jnp.broadcast_to(r, (4,) + r.shape[1:])]]
    rows = sum(p.shape[1] for p in parts)
    if rows % (2 * SEGMENT_ROWS):
        parts.append(jnp.zeros((4, SEGMENT_ROWS, 128), F32))
        rows += SEGMENT_ROWS
    side = jnp.concatenate(parts, axis=1).reshape(4, 2, rows // 2, 128)
    return list(big) + [side]


def unpack_grads(reduced, shard_shapes, repl_shapes):
    out = dict(zip(BIG, reduced))
    side = reduced[-1].reshape(-1, 128)
    at = 0
    for k in SMALL:
        n = 1
        for e in shard_shapes[k]:
            n *= e
        out[k] = side[at:at + n // 128].reshape(shard_shapes[k])
        at += _segment(n // 128)
    r = side[at:].reshape(-1)
    at = 0
    for k in REPLICATED:
        n = 1
        for e in repl_shapes[k]:
            n *= e
        out[k] = r[at:at + n].reshape(repl_shapes[k])
        at += n
    return out


WEIGHT_ORDER = ("meta_tokens", "norm_mix", "w_in", "conv_qkv", "a_log", "dt_bias", "head_norm", "w_pool", "pool_scale",
                "w_out", "norm_ffn", "w_up", "conv_ffn", "w_down", "norm_final")


def kernel(x, meta_tokens, norm_mix, w_in, conv_qkv, a_log, dt_bias, head_norm, w_pool, pool_scale, w_out, norm_ffn, w_up, conv_ffn, w_down, norm_final, loss_target, m_meta_tokens, m_norm_mix, m_w_in, m_conv_qkv, m_a_log, m_dt_bias, m_head_norm, m_w_pool, m_pool_scale, m_w_out, m_norm_ffn, m_w_up, m_conv_ffn, m_w_down, m_norm_final, v_meta_tokens, v_norm_mix, v_w_in, v_conv_qkv, v_a_log, v_dt_bias, v_head_norm, v_w_pool, v_pool_scale, v_w_out, v_norm_ffn, v_w_up, v_conv_ffn, v_w_down, v_norm_final):
    weights = dict(meta_tokens=meta_tokens, norm_mix=norm_mix, w_in=w_in, conv_qkv=conv_qkv, a_log=a_log,
                   dt_bias=dt_bias, head_norm=head_norm, w_pool=w_pool, pool_scale=pool_scale, w_out=w_out,
                   norm_ffn=norm_ffn, w_up=w_up, conv_ffn=conv_ffn, w_down=w_down, norm_final=norm_final)
    m_in = dict(zip(WEIGHT_ORDER, (m_meta_tokens, m_norm_mix, m_w_in, m_conv_qkv, m_a_log, m_dt_bias, m_head_norm,
                                   m_w_pool, m_pool_scale, m_w_out, m_norm_ffn, m_w_up, m_conv_ffn, m_w_down, m_norm_final)))
    v_in = dict(zip(WEIGHT_ORDER, (v_meta_tokens, v_norm_mix, v_w_in, v_conv_qkv, v_a_log, v_dt_bias, v_head_norm,
                                   v_w_pool, v_pool_scale, v_w_out, v_norm_ffn, v_w_up, v_conv_ffn, v_w_down, v_norm_final)))
    shard_shapes = {k: weights[k].shape for k in SHARDED}
    repl_shapes = {k: weights[k].shape for k in REPLICATED}
    core = lax.axis_index("c").astype(jnp.int32).reshape(1)

    gathered = gather_shards(pack_weights({k: weights[k] for k in SHARDED}))
    full = unpack_weights(gathered, shard_shapes)
    shards = dict(zip(BIG, gathered))
    layers = []
    for li in range(DEPTH):
        p = {k: (full[k][li] if k in SMALL else weights[k][li]) for k in LAYER_PARAMS if k not in BIG}
        p.update(w_in=[shards["w_in"][s, li] for s in range(4)], w_up=[shards["w_up"][s, li] for s in range(4)],
                 w_down=shards["w_down"][:, li].reshape(D_FF, D_MODEL),
                 w_out=shards["w_out"][:, li].reshape(D_MODEL, D_MODEL))
        layers.append(prep_layer(p))

    h0 = pad_rows(full["meta_tokens"], x[0])
    target = pad_rows(jnp.zeros((N_META, D_MODEL), F32), loss_target[0])
    loss, dh0, grads, d_norm_final = local_step(h0, target, layers, norm_final.reshape(1, D_MODEL))
    seq = x.shape[1]
    grad_x = dh0[LEAD + N_META:LEAD + N_META + seq][None]

    per_layer = [layer_grads(g) for g in grads]
    g_all = {k: jnp.stack([pl_[k] for pl_ in per_layer]) for k in LAYER_PARAMS if k not in BIG}
    g_all["meta_tokens"] = dh0[LEAD:LEAD + N_META]
    g_all["norm_final"] = d_norm_final[0]
    big = [big_grad_shards(g) for g in grads]
    packs = pack_grads([jnp.stack([b[k] for b in big], axis=1) for k in BIG],
                       {k: g_all[k] for k in SMALL}, {k: g_all[k] for k in REPLICATED})
    reduced = all_reduce_to_shards(packs, [GRAD_WIRE] * len(BIG) + [F32], BIG + ("side",), core)
    g_mine = unpack_grads(reduced, shard_shapes, repl_shapes)

    loss_sum = lax.psum(loss[0, 0], ("x", "y", "c"))
    deltas, new_m, new_v = {}, {}, {}
    for k in WEIGHT_ORDER:
        deltas[k], new_m[k], new_v[k] = adamw(weights[k], g_mine[k], m_in[k], v_in[k], f"adamw_{k}")
    return (loss_sum, grad_x, *[g_mine[k] for k in WEIGHT_ORDER], *[deltas[k] for k in WEIGHT_ORDER],
            *[new_m[k] for k in WEIGHT_ORDER], *[new_v[k] for k in WEIGHT_ORDER])
```

```python
import functools

import jax
import jax.numpy as jnp
from jax import lax
from jax.experimental import pallas as pl
from jax.experimental.pallas import tpu as pltpu

F32 = jnp.float32
BF16 = jnp.bfloat16
WIRE = jnp.bfloat16
GRAD_WIRE = jnp.bfloat16

D_MODEL = 1024
HEADS = 8
HEAD_DIM = 128
CHUNK = 64
N_META = 16
LEAD = 48
TAIL = 64
QKV_DIM = 3072
D_FF = 2816
POOL_WIDTH = 512
POOL_WINDOWS = (2, 4, 8, 16)
BA_BLOCK = POOL_WIDTH // 128
DEPTH = 2
NORM_EPS = 1e-6
ADAM_LR, ADAM_B1, ADAM_B2, ADAM_EPS, ADAM_WD, ADAM_STEP = 0.001, 0.9, 0.999, 1e-08, 0.01, 10
VMEM_LIMIT_BYTES = 48 * 1024 * 1024


def _params(*sem):
    return pltpu.CompilerParams(dimension_semantics=sem if sem else None, vmem_limit_bytes=VMEM_LIMIT_BYTES)


def _tile(n, cap, mult):
    best = None
    for t in range(mult, min(n, cap) + 1, mult):
        if n % t == 0:
            best = t
    assert best is not None, (n, cap, mult)
    return best


def _silu(x):
    return x * jax.nn.sigmoid(x)


def _softplus(x):
    return jnp.maximum(x, 0.0) + jnp.log(1.0 + jnp.exp(-jnp.abs(x)))


def _split_bf16(a):
    hi = a.astype(BF16)
    return hi, (a - hi.astype(F32)).astype(BF16)


def _dg(a, b, ca, cb, hi):
    dims = (((ca,), (cb,)), ((), ()))
    if hi is True:
        return lax.dot_general(a, b, dims, precision=lax.Precision.HIGHEST, preferred_element_type=F32)
    if hi == 3:
        (ah, al), (bh, bl) = _split_bf16(a), _split_bf16(b)
        dot = lambda x, y: lax.dot_general(x, y, dims, preferred_element_type=F32)
        return dot(ah, bh) + (dot(ah, bl) + dot(al, bh))
    return lax.dot_general(a.astype(BF16), b.astype(BF16), dims, preferred_element_type=F32)


def _make_dots(hi):
    @jax.custom_vjp
    def nn(a, b):
        return _dg(a, b, 1, 0, hi)

    @jax.custom_vjp
    def nt(a, b):
        return _dg(a, b, 1, 1, hi)

    @jax.custom_vjp
    def tn(a, b):
        return _dg(a, b, 0, 0, hi)

    nn.defvjp(lambda a, b: (nn(a, b), (a, b)), lambda r, g: (nt(g, r[1]), tn(r[0], g)))
    nt.defvjp(lambda a, b: (nt(a, b), (a, b)), lambda r, g: (nn(g, r[1]), tn(g, r[0])))
    tn.defvjp(lambda a, b: (tn(a, b), (a, b)), lambda r, g: (nt(r[1], g), nn(r[0], g)))
    return nn, nt, tn


_nn, _nt, _tn = _make_dots(False)
_hnn, _hnt, _htn = _make_dots(True)


def _neumann(a):
    n = a[0].shape[0]
    eye = (lax.broadcasted_iota(jnp.int32, (n, n), 0) == lax.broadcasted_iota(jnp.int32, (n, n), 1)).astype(F32)
    hd = range(len(a))
    p = [_dg(a[h], a[h], 1, 0, 3) for h in hd]
    x = [(eye - a[h]) + p[h] - _dg(a[h], p[h], 1, 0, False) for h in hd]
    for _ in range(4):
        p = [_dg(p[h], p[h], 1, 0, False) for h in hd]
        x = [x[h] + p[h] + _dg(x[h] - eye, p[h], 1, 0, False) for h in hd]
    return tuple(x)


@jax.custom_vjp
def _inv_unit_lower(a):
    return _neumann(a)


def _inv_unit_lower_bwd(x, g):
    t = [_dg(x[h], g[h], 0, 0, 3) for h in range(len(x))]
    return (tuple(-_dg(t[h], x[h], 1, 1, 3) for h in range(len(x))),)


_inv_unit_lower.defvjp(lambda a: (_neumann(a),) * 2, _inv_unit_lower_bwd)


@jax.custom_vjp
def _kept_inverse(a, x):
    return x


_kept_inverse.defvjp(lambda a, x: (x, x),
                     lambda x, g: _inv_unit_lower_bwd(x, g) + (tuple(jnp.zeros_like(e) for e in x),))


def mm(a, b, *, tb=False, add=None, out_dtype=F32, name):
    m, kdim = a.shape
    (n, kb) = b.shape if tb else b.shape[::-1]
    assert kdim == kb, (a.shape, b.shape, tb)
    tm = _tile(m, 1408, 128) if m % 128 == 0 and m <= 4096 else _tile(m, 640, 64)
    tn = _tile(n, 3072 if (kdim <= 1024 and not tb and add is None) else 1536, 128)
    tk = kdim if kdim <= 3072 else _tile(kdim, 1664 if a.dtype == b.dtype == BF16 else 640, 128)
    nk = kdim // tk
    dims = (((1,), (1 if tb else 0,)), ((), ()))

    def body(*refs):
        if add is not None:
            a_ref, b_ref, add_ref, o_ref, acc = refs
        else:
            a_ref, b_ref, o_ref, acc = refs
        k = pl.program_id(2)
        part = lax.dot_general(a_ref[...].astype(BF16), b_ref[...].astype(BF16), dims, preferred_element_type=F32)

        def finish(r):
            if add is not None:
                r = r + add_ref[...]
            o_ref[...] = r.astype(out_dtype)

        if nk == 1:
            finish(part)
        else:
            @pl.when(k == 0)
            def _():
                acc[...] = part

            @pl.when(jnp.logical_and(k > 0, k < nk - 1))
            def _():
                acc[...] += part

            @pl.when(k == nk - 1)
            def _():
                finish(acc[...] + part)

    a_spec = pl.BlockSpec((tm, tk), lambda j, i, k: (i, k))
    b_spec = pl.BlockSpec((tn, tk), lambda j, i, k: (j, k)) if tb else pl.BlockSpec((tk, tn), lambda j, i, k: (k, j))
    in_specs = [a_spec, b_spec]
    args = [a, b]
    if add is not None:
        in_specs.append(pl.BlockSpec((tm, tn), lambda j, i, k: (i, j)))
        args.append(add)
    return pl.pallas_call(
        body, grid=(n // tn, m // tm, nk), in_specs=in_specs,
        out_specs=pl.BlockSpec((tm, tn), lambda j, i, k: (i, j)),
        out_shape=jax.ShapeDtypeStruct((m, n), out_dtype),
        scratch_shapes=[pltpu.VMEM((tm, tn) if nk > 1 else (8, 128), F32)],
        compiler_params=_params("parallel", "parallel", "arbitrary"), name=name,
    )(*args)


def _rms(x, gain):
    return x * lax.rsqrt(jnp.mean(x * x, axis=-1, keepdims=True) + NORM_EPS) * gain


def rms_fwd(h, gain, name):
    t = h.shape[0]
    ts = _tile(t, 640, 128)

    def body(h_ref, g_ref, u_ref, ut_ref):
        u = _rms(h_ref[...], g_ref[...])
        u_ref[...] = u.astype(BF16)
        ut_ref[...] = u.T.astype(BF16)

    return pl.pallas_call(
        body, grid=(t // ts,),
        in_specs=[pl.BlockSpec((ts, D_MODEL), lambda i: (i, 0)), pl.BlockSpec((1, D_MODEL), lambda i: (0, 0))],
        out_specs=(pl.BlockSpec((ts, D_MODEL), lambda i: (i, 0)), pl.BlockSpec((D_MODEL, ts), lambda i: (0, i))),
        out_shape=(jax.ShapeDtypeStruct((t, D_MODEL), BF16), jax.ShapeDtypeStruct((D_MODEL, t), BF16)),
        compiler_params=_params("parallel"), name=name,
    )(h, gain)


def rms_bwd(h, gain, du, dres, name):
    t = h.shape[0]
    ts = _tile(t, 640, 64)

    def body(h_ref, g_ref, du_ref, dres_ref, dh_ref, dg_ref):
        i = pl.program_id(0)
        _, vjp = jax.vjp(_rms, h_ref[...], g_ref[...])
        dx, dg = vjp(du_ref[...])
        row = i * ts + lax.broadcasted_iota(jnp.int32, (ts, 1), 0)
        dh_ref[...] = jnp.where(row >= LEAD, dx + dres_ref[...], 0.0)

        @pl.when(i == 0)
        def _():
            dg_ref[...] = jnp.zeros_like(dg_ref)

        dg_ref[...] += dg

    row_spec = pl.BlockSpec((ts, D_MODEL), lambda i: (i, 0))
    vec_spec = pl.BlockSpec((1, D_MODEL), lambda i: (0, 0))
    return pl.pallas_call(
        body, grid=(t // ts,), in_specs=[row_spec, vec_spec, row_spec, row_spec],
        out_specs=(row_spec, vec_spec),
        out_shape=(jax.ShapeDtypeStruct((t, D_MODEL), F32), jax.ShapeDtypeStruct((1, D_MODEL), F32)),
        compiler_params=_params("arbitrary"), name=name,
    )(h, gain, du, dres)


def loss_head(h, gain, target, name):
    t = h.shape[0]
    ts = _tile(t, 640, 64)

    def body(h_ref, g_ref, t_ref, loss_ref, dh_ref, dg_ref):
        i = pl.program_id(0)
        row = i * ts + lax.broadcasted_iota(jnp.int32, (ts, 1), 0)
        keep = jnp.logical_and(row >= LEAD + N_META, row < t - TAIL)
        tgt = t_ref[...]

        def f(x, g):
            err = jnp.where(keep, _rms(x, g) - tgt, 0.0)
            per_row = jnp.mean(err * err, axis=-1, keepdims=True)
            return 0.5 * jnp.sum(per_row, axis=0, keepdims=True)

        val, vjp = jax.vjp(f, h_ref[...], g_ref[...])
        dx, dg = vjp(jnp.ones((1, 1), F32))
        dh_ref[...] = dx

        @pl.when(i == 0)
        def _():
            dg_ref[...] = jnp.zeros_like(dg_ref)
            loss_ref[...] = jnp.zeros_like(loss_ref)

        dg_ref[...] += dg
        loss_ref[...] += jnp.broadcast_to(val, (1, 128))

    row_spec = pl.BlockSpec((ts, D_MODEL), lambda i: (i, 0))
    vec_spec = pl.BlockSpec((1, D_MODEL), lambda i: (0, 0))
    return pl.pallas_call(
        body, grid=(t // ts,), in_specs=[row_spec, vec_spec, row_spec],
        out_specs=(pl.BlockSpec((1, 128), lambda i: (0, 0)), row_spec, vec_spec),
        out_shape=(jax.ShapeDtypeStruct((1, 128), F32), jax.ShapeDtypeStruct((t, D_MODEL), F32),
                   jax.ShapeDtypeStruct((1, D_MODEL), F32)),
        compiler_params=_params("arbitrary"), name=name,
    )(h, gain, target)


def _rows_down(a, s):
    return a if s == 0 else pltpu.roll(a, s, axis=0)


def _rows_up(a, s):
    return a if s == 0 else pltpu.roll(a, a.shape[0] - s, axis=0)


def conv_fwd(x, w, name):
    t, width = x.shape
    k = w.shape[0]
    ts = _tile(t, 640, 64)
    tw = _tile(width, 1536, 128)
    hb = ts // 8

    def body(x_ref, halo_ref, w_ref, o_ref, buf):
        i = pl.program_id(0)
        buf[0:8, :] = jnp.where(i > 0, halo_ref[...], 0.0)
        buf[8:, :] = x_ref[...]
        ext = buf[...]
        wv = w_ref[...]
        acc = _rows_down(ext, k - 1)[8:, :] * wv[0:1, :]
        for j in range(1, k):
            acc = acc + _rows_down(ext, k - 1 - j)[8:, :] * wv[j:j + 1, :]
        o_ref[...] = acc

    return pl.pallas_call(
        body, grid=(t // ts, width // tw),
        in_specs=[pl.BlockSpec((ts, tw), lambda i, j: (i, j)),
                  pl.BlockSpec((8, tw), lambda i, j: (jnp.maximum(i * hb - 1, 0), j)),
                  pl.BlockSpec((k, tw), lambda i, j: (0, j))],
        out_specs=pl.BlockSpec((ts, tw), lambda i, j: (i, j)),
        out_shape=jax.ShapeDtypeStruct((t, width), F32),
        scratch_shapes=[pltpu.VMEM((ts + 8, tw), F32)],
        compiler_params=_params("parallel", "parallel"), name=name,
    )(x, x, w)


def conv_bwd(x, dc, w, name):
    t, width = x.shape
    k = w.shape[0]
    ts = _tile(t, 640, 64)
    tw = _tile(width, 1536, 128)
    hb = ts // 8
    nt = t // ts

    def body(x_ref, xh_ref, dc_ref, dch_ref, w_ref, dx_ref, dw_ref, xbuf, dbuf):
        i = pl.program_id(1)
        xbuf[0:8, :] = jnp.where(i > 0, xh_ref[...], 0.0)
        xbuf[8:, :] = x_ref[...]
        d = dc_ref[...]
        dbuf[0:ts, :] = d
        dbuf[ts:, :] = jnp.where(i < nt - 1, dch_ref[...], 0.0)
        wv = w_ref[...]
        ext_x, ext_d = xbuf[...], dbuf[...]
        acc = _rows_up(ext_d, k - 1)[0:ts, :] * wv[0:1, :]
        for j in range(1, k):
            acc = acc + _rows_up(ext_d, k - 1 - j)[0:ts, :] * wv[j:j + 1, :]
        dx_ref[...] = acc.astype(BF16)

        @pl.when(i == 0)
        def _():
            dw_ref[...] = jnp.zeros_like(dw_ref)

        for j in range(k):
            dw_ref[j:j + 1, :] += jnp.sum(d * _rows_down(ext_x, k - 1 - j)[8:, :], axis=0, keepdims=True)

    return pl.pallas_call(
        body, grid=(width // tw, nt),
        in_specs=[pl.BlockSpec((ts, tw), lambda j, i: (i, j)),
                  pl.BlockSpec((8, tw), lambda j, i: (jnp.maximum(i * hb - 1, 0), j)),
                  pl.BlockSpec((ts, tw), lambda j, i: (i, j)),
                  pl.BlockSpec((8, tw), lambda j, i: (jnp.minimum((i + 1) * hb, t // 8 - 1), j)),
                  pl.BlockSpec((k, tw), lambda j, i: (0, j))],
        out_specs=(pl.BlockSpec((ts, tw), lambda j, i: (i, j)), pl.BlockSpec((8, tw), lambda j, i: (0, j))),
        out_shape=(jax.ShapeDtypeStruct((t, width), BF16), jax.ShapeDtypeStruct((8, width), F32)),
        scratch_shapes=[pltpu.VMEM((ts + 8, tw), F32), pltpu.VMEM((ts + 8, tw), F32)],
        compiler_params=_params("parallel", "arbitrary"), name=name,
    )(x, x, dc, dc, w)


def _pool_count(pos, win):
    return jnp.clip(pos + 1, 1, win).astype(F32)


def poolwin_fwd(p, name):
    t = p.shape[0]
    ts = _tile(t, 640, 64)
    hb = ts // 16

    def body(p_ref, halo_ref, o_ref, buf):
        i = pl.program_id(0)
        buf[0:16, :] = jnp.where(i > 0, halo_ref[...], 0.0)
        buf[16:, :] = p_ref[...]
        pos = i * ts + lax.broadcasted_iota(jnp.int32, (ts, 1), 0) - LEAD
        ext = buf[...]
        own = ext[16:, :]
        sums, span = ext, 1
        for gi, win in enumerate(POOL_WINDOWS):
            while span < win:
                sums = sums + _rows_down(sums, span)
                span *= 2
            cols = slice(gi * 128, (gi + 1) * 128)
            o_ref[:, cols] = sums[16:, cols] / _pool_count(pos, win) - own[:, cols]

    return pl.pallas_call(
        body, grid=(t // ts,),
        in_specs=[pl.BlockSpec((ts, POOL_WIDTH), lambda i: (i, 0)),
                  pl.BlockSpec((16, POOL_WIDTH), lambda i: (jnp.maximum(i * hb - 1, 0), 0))],
        out_specs=pl.BlockSpec((ts, POOL_WIDTH), lambda i: (i, 0)),
        out_shape=jax.ShapeDtypeStruct((t, POOL_WIDTH), F32),
        scratch_shapes=[pltpu.VMEM((ts + 16, POOL_WIDTH), F32)],
        compiler_params=_params("parallel"), name=name,
    )(p, p)


def poolwin_bwd(dpooled, name):
    t = dpooled.shape[0]
    ts = _tile(t, 640, 64)
    hb = ts // 16
    nt = t // ts

    def body(d_ref, halo_ref, o_ref, buf):
        i = pl.program_id(0)
        buf[0:ts, :] = d_ref[...]
        buf[ts:, :] = jnp.where(i < nt - 1, halo_ref[...], 0.0)
        pos = i * ts + lax.broadcasted_iota(jnp.int32, (ts + 16, 1), 0) - LEAD
        ext = buf[...]
        for gi, win in enumerate(POOL_WINDOWS):
            cols = slice(gi * 128, (gi + 1) * 128)
            sums, span = ext[:, cols] / _pool_count(pos, win), 1
            while span < win:
                sums = sums + _rows_up(sums, span)
                span *= 2
            o_ref[:, cols] = (sums[0:ts, :] - ext[0:ts, cols]).astype(BF16)

    return pl.pallas_call(
        body, grid=(nt,),
        in_specs=[pl.BlockSpec((ts, POOL_WIDTH), lambda i: (i, 0)),
                  pl.BlockSpec((16, POOL_WIDTH), lambda i: (jnp.minimum((i + 1) * hb, t // 16 - 1), 0))],
        out_specs=pl.BlockSpec((ts, POOL_WIDTH), lambda i: (i, 0)),
        out_shape=jax.ShapeDtypeStruct((t, POOL_WIDTH), BF16),
        scratch_shapes=[pltpu.VMEM((ts + 16, POOL_WIDTH), F32)],
        compiler_params=_params("parallel"), name=name,
    )(dpooled, dpooled)


def _mix(y_a, gpre, pooled, w_pool, scale):
    parts = [_nn(pooled[:, g * 128:(g + 1) * 128], w_pool[g]) for g in range(4)]
    y_b = jnp.concatenate(parts, axis=1) * scale
    return jax.nn.sigmoid(gpre[:, :D_MODEL]) * y_a + jax.nn.sigmoid(gpre[:, D_MODEL:]) * y_b


def _mix_specs(ts):
    return [pl.BlockSpec((ts, D_MODEL), lambda i: (i, 0)), pl.BlockSpec((ts, 2 * D_MODEL), lambda i: (i, 0)),
            pl.BlockSpec((ts, POOL_WIDTH), lambda i: (i, 0)), pl.BlockSpec((4, 128, 256), lambda i: (0, 0, 0)),
            pl.BlockSpec((1, D_MODEL), lambda i: (0, 0))]


def mix_fwd(y_a, gpre, pooled, w_pool, scale, name):
    t = y_a.shape[0]
    ts = _tile(t, 640, 128)

    def body(ya_ref, g_ref, p_ref, w_ref, s_ref, o_ref, ot_ref):
        y = _mix(ya_ref[...], g_ref[...], p_ref[...], w_ref[...], s_ref[...])
        o_ref[...] = y.astype(BF16)
        ot_ref[...] = y.T.astype(BF16)

    return pl.pallas_call(
        body, grid=(t // ts,), in_specs=_mix_specs(ts),
        out_specs=(pl.BlockSpec((ts, D_MODEL), lambda i: (i, 0)), pl.BlockSpec((D_MODEL, ts), lambda i: (0, i))),
        out_shape=(jax.ShapeDtypeStruct((t, D_MODEL), BF16), jax.ShapeDtypeStruct((D_MODEL, t), BF16)),
        compiler_params=_params("parallel"), name=name,
    )(y_a, gpre, pooled, w_pool, scale)


def mix_bwd(y_a, gpre, pooled, w_pool, scale, dy, name):
    t = y_a.shape[0]
    ts = _tile(t, 320, 64)

    def body(ya_ref, g_ref, p_ref, w_ref, s_ref, dy_ref, dya_ref, dg_ref, dp_ref, dw_ref, ds_ref):
        i = pl.program_id(0)
        _, vjp = jax.vjp(_mix, ya_ref[...], g_ref[...], p_ref[...], w_ref[...], s_ref[...])
        dya, dg, dp, dw, ds = vjp(dy_ref[...])
        dya_ref[...] = dya
        dg_ref[...] = dg.astype(BF16)
        dp_ref[...] = dp

        @pl.when(i == 0)
        def _():
            dw_ref[...] = jnp.zeros_like(dw_ref)
            ds_ref[...] = jnp.zeros_like(ds_ref)

        dw_ref[...] += dw
        ds_ref[...] += ds

    specs = _mix_specs(ts)
    return pl.pallas_call(
        body, grid=(t // ts,), in_specs=specs + [specs[0]],
        out_specs=(specs[0], specs[1], specs[2], specs[3], specs[4]),
        out_shape=(jax.ShapeDtypeStruct((t, D_MODEL), F32), jax.ShapeDtypeStruct((t, 2 * D_MODEL), BF16),
                   jax.ShapeDtypeStruct((t, POOL_WIDTH), F32), jax.ShapeDtypeStruct((4, 128, 256), F32),
                   jax.ShapeDtypeStruct((1, D_MODEL), F32)),
        compiler_params=_params("arbitrary"), name=name,
    )(y_a, gpre, pooled, w_pool, scale, dy)


def _ffn_act(cg, cv):
    return _silu(cg) * cv


FFN_MID_VMEM_BYTES = 58 * 1024 * 1024


def ffn_mid_fwd(hg, hv, wg, wv, name):
    t, width = hg.shape
    k = wg.shape[0]
    ts = _tile(t, 640, 128)
    tw = _tile(width, 1536, 128)
    hb = ts // 8

    def body(hg_ref, hgp_ref, hv_ref, hvp_ref, wg_ref, wv_ref, cg_ref, cv_ref, act_ref, actt_ref, xg, xv):
        i = pl.program_id(0)
        convs = []
        for x_ref, xp_ref, buf, w_ref, c_ref in ((hg_ref, hgp_ref, xg, wg_ref, cg_ref),
                                                 (hv_ref, hvp_ref, xv, wv_ref, cv_ref)):
            buf[0:8, :] = jnp.where(i > 0, xp_ref[...], 0.0)
            buf[8:, :] = x_ref[...]
            ext, wt = buf[...], w_ref[...]
            acc = _rows_down(ext, k - 1)[8:, :] * wt[0:1, :]
            for j in range(1, k):
                acc = acc + _rows_down(ext, k - 1 - j)[8:, :] * wt[j:j + 1, :]
            c_ref[...] = acc
            convs.append(acc)
        act = _ffn_act(*convs)
        act_ref[...] = act.astype(BF16)
        actt_ref[...] = act.T.astype(BF16)

    tile = pl.BlockSpec((ts, tw), lambda i, j: (i, j))
    prev = pl.BlockSpec((8, tw), lambda i, j: (jnp.maximum(i * hb - 1, 0), j))
    taps = pl.BlockSpec((k, tw), lambda i, j: (0, j))
    return pl.pallas_call(
        body, grid=(t // ts, width // tw), in_specs=[tile, prev, tile, prev, taps, taps],
        out_specs=(tile, tile, tile, pl.BlockSpec((tw, ts), lambda i, j: (j, i))),
        out_shape=(jax.ShapeDtypeStruct((t, width), F32), jax.ShapeDtypeStruct((t, width), F32),
                   jax.ShapeDtypeStruct((t, width), BF16), jax.ShapeDtypeStruct((width, t), BF16)),
        scratch_shapes=[pltpu.VMEM((ts + 8, tw), F32)] * 2,
        compiler_params=pltpu.CompilerParams(dimension_semantics=("parallel", "parallel"),
                                             vmem_limit_bytes=FFN_MID_VMEM_BYTES), name=name,
    )(hg, hg, hv, hv, wg, wv)


def ffn_mid_bwd(hg, hv, cg, cv, dact, wg, wv, name):
    t, width = hg.shape
    k = wg.shape[0]
    ts = _tile(t, 320, 64)
    tw = _tile(width, 1536, 128)
    hb = ts // 8
    nt = t // ts

    def body(hg_ref, hgp_ref, hv_ref, hvp_ref, cg_ref, cgn_ref, cv_ref, cvn_ref, da_ref, dan_ref, wg_ref, wv_ref,
             dhg_ref, dhv_ref, dwg_ref, dwv_ref, xg, xv, dg, dv):
        i = pl.program_id(1)
        behind = i < nt - 1

        def d_conv(c_g, c_v, d_a):
            _, vjp = jax.vjp(_ffn_act, c_g, c_v)
            return vjp(d_a)

        dcg, dcv = d_conv(cg_ref[...], cv_ref[...], da_ref[...])
        dcg_n, dcv_n = d_conv(cgn_ref[...], cvn_ref[...], jnp.where(behind, dan_ref[...], 0.0))

        @pl.when(i == 0)
        def _():
            dwg_ref[...] = jnp.zeros_like(dwg_ref)
            dwv_ref[...] = jnp.zeros_like(dwv_ref)

        for x_ref, xp_ref, xbuf, dbuf, d, d_n, w_ref, dx_ref, dw_ref in (
                (hg_ref, hgp_ref, xg, dg, dcg, dcg_n, wg_ref, dhg_ref, dwg_ref),
                (hv_ref, hvp_ref, xv, dv, dcv, dcv_n, wv_ref, dhv_ref, dwv_ref)):
            xbuf[0:8, :] = jnp.where(i > 0, xp_ref[...], 0.0)
            xbuf[8:, :] = x_ref[...]
            dbuf[0:ts, :] = d
            dbuf[ts:, :] = jnp.where(behind, d_n, 0.0)
            wt = w_ref[...]
            ext_x, ext_d = xbuf[...], dbuf[...]
            acc = _rows_up(ext_d, k - 1)[0:ts, :] * wt[0:1, :]
            for j in range(1, k):
                acc = acc + _rows_up(ext_d, k - 1 - j)[0:ts, :] * wt[j:j + 1, :]
            dx_ref[...] = acc.astype(BF16)
            for j in range(k):
                dw_ref[j:j + 1, :] += jnp.sum(d * _rows_down(ext_x, k - 1 - j)[8:, :], axis=0, keepdims=True)

    tile = pl.BlockSpec((ts, tw), lambda j, i: (i, j))
    prev = pl.BlockSpec((8, tw), lambda j, i: (jnp.maximum(i * hb - 1, 0), j))
    nxt = pl.BlockSpec((8, tw), lambda j, i: (jnp.minimum((i + 1) * hb, t // 8 - 1), j))
    taps = pl.BlockSpec((k, tw), lambda j, i: (0, j))
    dw_spec = pl.BlockSpec((8, tw), lambda j, i: (0, j))
    return pl.pallas_call(
        body, grid=(width // tw, nt),
        in_specs=[tile, prev, tile, prev, tile, nxt, tile, nxt, tile, nxt, taps, taps],
        out_specs=(tile, tile, dw_spec, dw_spec),
        out_shape=(jax.ShapeDtypeStruct((t, width), BF16), jax.ShapeDtypeStruct((t, width), BF16),
                   jax.ShapeDtypeStruct((8, width), F32), jax.ShapeDtypeStruct((8, width), F32)),
        scratch_shapes=[pltpu.VMEM((ts + 8, tw), F32)] * 4,
        compiler_params=_params("parallel", "arbitrary"), name=name,
    )(hg, hg, hv, hv, cg, cg, cv, cv, dact, dact, wg, wv)


def _gdn_chunk(c, z, ba, pa, pdt, hn, s, *, valid, inverse=None, with_inverse=False):
    r = lax.broadcasted_iota(jnp.int32, (CHUNK, CHUNK), 0)
    q_ = lax.broadcasted_iota(jnp.int32, (CHUNK, CHUNK), 1)
    causal = r >= q_
    strict = r > q_
    tril = causal.astype(F32)
    triu = (r <= q_).astype(F32)
    lane = lax.broadcasted_iota(jnp.int32, (CHUNK, 128), 1)

    decay_log = -jnp.exp(pa) * _softplus(ba + pdt)
    bg = jnp.where(lane < HEADS, jax.nn.sigmoid(ba), jnp.where(lane < 2 * HEADS, decay_log, 0.0))
    bg = jnp.where(valid, bg, 0.0)
    gc = _hnn(tril, bg)
    gct = _hnn(bg.T, triu)
    eg = jnp.exp(gc)
    glast = gc[CHUNK - 1:CHUNK, :]
    ekd = jnp.exp(glast - gc)
    gtot = jnp.exp(glast)

    hd = range(HEADS)
    hs = [slice(h * HEAD_DIM, (h + 1) * HEAD_DIM) for h in hd]
    gl = [slice(HEADS + h, HEADS + h + 1) for h in hd]
    q = [_silu(c[:, hs[h]]) for h in hd]
    k = [_silu(c[:, D_MODEL + h * HEAD_DIM:D_MODEL + (h + 1) * HEAD_DIM]) for h in hd]
    v = [_silu(c[:, 2 * D_MODEL + h * HEAD_DIM:2 * D_MODEL + (h + 1) * HEAD_DIM]) for h in hd]
    q = [q[h] * lax.rsqrt(jnp.sum(q[h] * q[h], axis=-1, keepdims=True) + NORM_EPS) * (HEAD_DIM ** -0.5) for h in hd]
    k = [k[h] * lax.rsqrt(jnp.sum(k[h] * k[h], axis=-1, keepdims=True) + NORM_EPS) for h in hd]
    beta = [bg[:, h:h + 1] for h in hd]
    decay = [jnp.exp(jnp.where(causal, gc[:, gl[h]] - gct[gl[h], :], -1e30)) for h in hd]
    kb = [k[h] * beta[h] for h in hd]
    a = [jnp.where(strict, _nt(kb[h], k[h]) * decay[h], 0.0) for h in hd]
    qk = [jnp.where(causal, _nt(q[h], k[h]) * decay[h], 0.0) for h in hd]
    x = _inv_unit_lower(tuple(a)) if inverse is None else _kept_inverse(tuple(a), tuple(inverse))
    u = [_nn(x[h], v[h] * beta[h]) for h in hd]
    w = [_nn(x[h], kb[h] * eg[:, gl[h]]) for h in hd]
    v_new = [u[h] - _nn(w[h], s[h]) for h in hd]
    o = [_nn(q[h] * eg[:, gl[h]], s[h]) + _nn(qk[h], v_new[h]) for h in hd]
    states = [s[h] * gtot[:, gl[h]] + _tn(k[h] * ekd[:, gl[h]], v_new[h]) for h in hd]
    o = [o[h] * lax.rsqrt(jnp.mean(o[h] * o[h], axis=-1, keepdims=True) + NORM_EPS) * hn * _silu(z[:, hs[h]])
         for h in hd]
    if with_inverse:
        return jnp.concatenate(o, axis=1), tuple(states), x
    return jnp.concatenate(o, axis=1), tuple(states)


GDN_FWD_CHUNKS = 5
GDN_BWD_CHUNKS = 2


def _chunk_valid(n, t):
    row = n * CHUNK + lax.broadcasted_iota(jnp.int32, (CHUNK, 1), 0)
    return jnp.logical_and(row >= LEAD, row < t - TAIL)


def gdn_fwd(c, z, ba, pa, pdt, hn, name, ba_block=0):
    t = c.shape[0]
    n_chunks = t // CHUNK
    per_step = GDN_FWD_CHUNKS if n_chunks % GDN_FWD_CHUNKS == 0 else 1
    rows_per_step = per_step * CHUNK

    def body(c_ref, z_ref, ba_ref, pa_ref, pdt_ref, hn_ref, y_ref, ss_ref, inv_ref, state):
        step = pl.program_id(0)

        @pl.when(step == 0)
        def _():
            state[...] = jnp.zeros_like(state)

        s = tuple(state[h] for h in range(HEADS))
        for j in range(per_step):
            rows = pl.ds(j * CHUNK, CHUNK)
            for h in range(HEADS):
                ss_ref[j, h] = s[h]
            y, s, inv = _gdn_chunk(c_ref[rows, :], z_ref[rows, :], ba_ref[rows, :], pa_ref[...], pdt_ref[...],
                                   hn_ref[...], s, valid=_chunk_valid(step * per_step + j, t), with_inverse=True)
            y_ref[rows, :] = y
            for h in range(HEADS):
                inv_ref[j, h] = inv[h]
        for h in range(HEADS):
            state[h] = s[h]

    vec = pl.BlockSpec((1, 128), lambda n: (0, 0))
    return pl.pallas_call(
        body, grid=(n_chunks // per_step,),
        in_specs=[pl.BlockSpec((rows_per_step, QKV_DIM), lambda n: (n, 0)),
                  pl.BlockSpec((rows_per_step, D_MODEL), lambda n: (n, 0)),
                  pl.BlockSpec((rows_per_step, 128), lambda n: (n, ba_block)), vec, vec, vec],
        out_specs=(pl.BlockSpec((rows_per_step, D_MODEL), lambda n: (n, 0)),
                   pl.BlockSpec((per_step, HEADS, HEAD_DIM, HEAD_DIM), lambda n: (n, 0, 0, 0)),
                   pl.BlockSpec((per_step, HEADS, CHUNK, CHUNK), lambda n: (n, 0, 0, 0))),
        out_shape=(jax.ShapeDtypeStruct((t, D_MODEL), F32),
                   jax.ShapeDtypeStruct((n_chunks, HEADS, HEAD_DIM, HEAD_DIM), F32),
                   jax.ShapeDtypeStruct((n_chunks, HEADS, CHUNK, CHUNK), F32)),
        scratch_shapes=[pltpu.VMEM((HEADS, HEAD_DIM, HEAD_DIM), F32)],
        compiler_params=_params("arbitrary"), name=name,
    )(c, z, ba, pa, pdt, hn)


def gdn_bwd(c, z, ba, pa, pdt, hn, starts, inverses, dy, name, ba_block=0):
    t = c.shape[0]
    per_step = GDN_BWD_CHUNKS if (t // CHUNK) % GDN_BWD_CHUNKS == 0 else 1
    n_steps = t // CHUNK // per_step
    rows_per_step = per_step * CHUNK

    def body(c_ref, z_ref, ba_ref, pa_ref, pdt_ref, hn_ref, ss_ref, inv_ref, dy_ref,
             dc_ref, dz_ref, dba_ref, dpa_ref, dpdt_ref, dhn_ref, dstate):
        step = pl.program_id(0)

        @pl.when(step == 0)
        def _():
            dstate[...] = jnp.zeros_like(dstate)
            dpa_ref[...] = jnp.zeros_like(dpa_ref)
            dpdt_ref[...] = jnp.zeros_like(dpdt_ref)
            dhn_ref[...] = jnp.zeros_like(dhn_ref)

        ds = tuple(dstate[h] for h in range(HEADS))
        for j in reversed(range(per_step)):
            rows = pl.ds(j * CHUNK, CHUNK)
            f = functools.partial(_gdn_chunk, valid=_chunk_valid((n_steps - 1 - step) * per_step + j, t),
                                  inverse=tuple(inv_ref[j, h] for h in range(HEADS)))
            _, vjp = jax.vjp(f, c_ref[rows, :], z_ref[rows, :], ba_ref[rows, :], pa_ref[...], pdt_ref[...], hn_ref[...],
                             tuple(ss_ref[j, h] for h in range(HEADS)))
            dc, dz, dba, dpa, dpdt, dhn, ds = vjp((dy_ref[rows, :], ds))
            dc_ref[rows, :] = dc
            dz_ref[rows, :] = dz.astype(BF16)
            dba_ref[rows, :] = dba.astype(BF16)
            dpa_ref[...] += dpa
            dpdt_ref[...] += dpdt
            dhn_ref[...] += dhn
        for h in range(HEADS):
            dstate[h] = ds[h]

    def rev(width, block=0):
        return pl.BlockSpec((rows_per_step, width), lambda s: (n_steps - 1 - s, block))

    vec = pl.BlockSpec((1, 128), lambda s: (0, 0))
    vec_shape = jax.ShapeDtypeStruct((1, 128), F32)
    return pl.pallas_call(
        body, grid=(n_steps,),
        in_specs=[rev(QKV_DIM), rev(D_MODEL), rev(128, ba_block), vec, vec, vec,
                  pl.BlockSpec((per_step, HEADS, HEAD_DIM, HEAD_DIM), lambda s: (n_steps - 1 - s, 0, 0, 0)),
                  pl.BlockSpec((per_step, HEADS, CHUNK, CHUNK), lambda s: (n_steps - 1 - s, 0, 0, 0)),
                  rev(D_MODEL)],
        out_specs=(rev(QKV_DIM), rev(D_MODEL), rev(128), vec, vec, vec),
        out_shape=(jax.ShapeDtypeStruct((t, QKV_DIM), F32), jax.ShapeDtypeStruct((t, D_MODEL), BF16),
                   jax.ShapeDtypeStruct((t, 128), BF16), vec_shape, vec_shape, vec_shape),
        scratch_shapes=[pltpu.VMEM((HEADS, HEAD_DIM, HEAD_DIM), F32)],
        compiler_params=_params("arbitrary"), name=name,
    )(c, z, ba, pa, pdt, hn, starts, inverses, dy)


def _layer_fwd(h, w, tag):
    u, ut = rms_fwd(h, w["norm_mix"], f"{tag}_rms_mix")
    pq = mm(u, w["wqkv"], name=f"{tag}_mm_qkv")
    pz = mm(u, w["wz"], name=f"{tag}_mm_z")
    pg = mm(u, w["wg"], name=f"{tag}_mm_gate")
    pba = mm(u, w["wpb"], name=f"{tag}_mm_pool_ba")
    cq = conv_fwd(pq, w["conv_qkv"], f"{tag}_conv_qkv")
    ya, starts, inverses = gdn_fwd(cq, pz, pba, w["pa"], w["pdt"], w["head_norm"], f"{tag}_gdn", ba_block=BA_BLOCK)
    pooled = poolwin_fwd(pba, f"{tag}_poolwin")
    y, yt = mix_fwd(ya, pg, pooled, w["w_pool"], w["pool_scale"], f"{tag}_mix")
    h1 = mm(y, w["wout"], add=h, name=f"{tag}_mm_out")
    u2, u2t = rms_fwd(h1, w["norm_ffn"], f"{tag}_rms_ffn")
    hg = mm(u2, w["wupg"], name=f"{tag}_mm_upg")
    hv = mm(u2, w["wupv"], name=f"{tag}_mm_upv")
    cg, cv, act, actt = ffn_mid_fwd(hg, hv, w["conv_g"], w["conv_v"], f"{tag}_ffn_mid")
    h2 = mm(act, w["wdown"], add=h1, name=f"{tag}_mm_down")
    saved = dict(h=h, ut=ut, pq=pq, pz=pz, pg=pg, pba=pba, cq=cq, ya=ya, starts=starts, inverses=inverses, pooled=pooled, yt=yt, h1=h1,
                 u2t=u2t, hg=hg, hv=hv, cg=cg, cv=cv, actt=actt)
    return h2, saved


def _layer_bwd(dh2, w, s, tag):
    g = {}
    dact = mm(dh2, w["wdown"], tb=True, name=f"{tag}_bmm_down_x")
    g["wdown"] = mm(s["actt"], dh2, name=f"{tag}_bmm_down_w")
    dhg, dhv, g["conv_g"], g["conv_v"] = ffn_mid_bwd(s["hg"], s["hv"], s["cg"], s["cv"], dact, w["conv_g"], w["conv_v"],
                                                     f"{tag}_ffn_mid_b")
    du2 = mm(dhg, w["wupg"], tb=True, name=f"{tag}_bmm_upg_x")
    du2 = mm(dhv, w["wupv"], tb=True, add=du2, name=f"{tag}_bmm_upv_x")
    g["wupg"] = mm(s["u2t"], dhg, name=f"{tag}_bmm_upg_w")
    g["wupv"] = mm(s["u2t"], dhv, name=f"{tag}_bmm_upv_w")
    dh1, g["norm_ffn"] = rms_bwd(s["h1"], w["norm_ffn"], du2, dh2, f"{tag}_rms_ffn_b")
    dy = mm(dh1, w["wout"], tb=True, name=f"{tag}_bmm_out_x")
    g["wout"] = mm(s["yt"], dh1, name=f"{tag}_bmm_out_w")
    dya, dpg, dpooled, g["w_pool"], g["pool_scale"] = mix_bwd(
        s["ya"], s["pg"], s["pooled"], w["w_pool"], w["pool_scale"], dy, f"{tag}_mix_b")
    dpp = poolwin_bwd(dpooled, f"{tag}_poolwin_b")
    dcq, dpz, dpba, g["pa"], g["pdt"], g["head_norm"] = gdn_bwd(
        s["cq"], s["pz"], s["pba"], w["pa"], w["pdt"], w["head_norm"], s["starts"], s["inverses"], dya, f"{tag}_gdn_b",
        ba_block=BA_BLOCK)
    dpb = jnp.concatenate([dpp, dpba], axis=1)
    dpq, g["conv_qkv"] = conv_bwd(s["pq"], dcq, w["conv_qkv"], f"{tag}_conv_qkv_b")
    du = mm(dpq, w["wqkv"], tb=True, name=f"{tag}_bmm_qkv_x")
    du = mm(dpz, w["wz"], tb=True, add=du, name=f"{tag}_bmm_z_x")
    du = mm(dpg, w["wg"], tb=True, add=du, name=f"{tag}_bmm_gate_x")
    du = mm(dpb, w["wpb"], tb=True, add=du, name=f"{tag}_bmm_pool_ba_x")
    g["wqkv"] = mm(s["ut"], dpq, name=f"{tag}_bmm_qkv_w")
    g["wz"] = mm(s["ut"], dpz, name=f"{tag}_bmm_z_w")
    g["wg"] = mm(s["ut"], dpg, name=f"{tag}_bmm_gate_w")
    dwpb = mm(s["ut"], dpb, name=f"{tag}_bmm_pool_ba_w")
    g["wpl"], g["wba"] = dwpb[:, :POOL_WIDTH], dwpb[:, POOL_WIDTH:]
    dh, g["norm_mix"] = rms_bwd(s["h"], w["norm_mix"], du, dh1, f"{tag}_rms_mix_b")
    return dh, g


def local_step(h0, target, layers, norm_final):
    h = h0
    saved = []
    for li, w in enumerate(layers):
        h, s = _layer_fwd(h, w, f"l{li}")
        saved.append(s)
    loss, dh, dnf = loss_head(h, norm_final, target, "loss_head")
    grads = [None] * len(layers)
    for li in reversed(range(len(layers))):
        dh, grads[li] = _layer_bwd(dh, layers[li], saved[li], f"l{li}")
    return loss, dh, grads, dnf


_Z0, _B0, _P0, _G0, _IN_DIM = 3072, 4096, 4112, 4624, 6672


def _lanes_8_to_15(v):
    return jnp.pad(v.reshape(1, HEADS).astype(F32), ((0, 0), (HEADS, 128 - 2 * HEADS)))


IN_PIECES = (("wqkv", 0, _Z0), ("wz", _Z0, _B0), ("wba", _B0, _P0), ("wpl", _P0, _G0), ("wg", _G0, _IN_DIM))
IN_SHARD = _IN_DIM // 4


def _overlaps(a, b, spans):
    return [(name, max(a, lo) - lo, min(b, hi) - max(a, lo)) for name, lo, hi in spans if max(a, lo) < min(b, hi)]


def _cat(parts):
    return parts[0] if len(parts) == 1 else jnp.concatenate(parts, axis=1)


def prep_layer(p):
    row = lambda v: v.reshape(1, -1).astype(F32)
    w_in, w_up = p["w_in"], p["w_up"]
    if not isinstance(w_in, (list, tuple)):
        w_in = [w_in[:, s * IN_SHARD:(s + 1) * IN_SHARD] for s in range(4)]
        w_up = [w_up[:, s * (D_FF // 2):(s + 1) * (D_FF // 2)] for s in range(4)]
    shards = [(s, s * IN_SHARD, (s + 1) * IN_SHARD) for s in range(4)]
    piece = {name: _cat([w_in[s][:, off:off + width].astype(BF16) for s, off, width in _overlaps(lo, hi, shards)])
             for name, lo, hi in IN_PIECES}
    return dict(
        wqkv=piece["wqkv"], wz=piece["wz"],
        wpb=jnp.concatenate([piece["wpl"], jnp.pad(piece["wba"], ((0, 0), (0, 128 - 2 * HEADS)))], axis=1),
        wg=piece["wg"], wout=p["w_out"].astype(BF16),
        wupg=_cat([w_up[0].astype(BF16), w_up[1].astype(BF16)]), wupv=_cat([w_up[2].astype(BF16), w_up[3].astype(BF16)]),
        wdown=p["w_down"].astype(BF16),
        conv_qkv=p["conv_qkv"].astype(F32), conv_g=p["conv_ffn"][:, :D_FF].astype(F32),
        conv_v=p["conv_ffn"][:, D_FF:].astype(F32), w_pool=p["w_pool"].astype(F32),
        pool_scale=row(p["pool_scale"]), head_norm=row(p["head_norm"]), norm_mix=row(p["norm_mix"]),
        norm_ffn=row(p["norm_ffn"]), pa=_lanes_8_to_15(p["a_log"]), pdt=_lanes_8_to_15(p["dt_bias"]))


def layer_grads(g):
    return dict(
        w_in=jnp.concatenate([g["wqkv"], g["wz"], g["wba"][:, :2 * HEADS], g["wpl"], g["wg"]], axis=1),
        conv_qkv=g["conv_qkv"][:4], a_log=g["pa"][0, HEADS:2 * HEADS], dt_bias=g["pdt"][0, HEADS:2 * HEADS],
        head_norm=g["head_norm"][0], w_pool=g["w_pool"], pool_scale=g["pool_scale"][0], w_out=g["wout"],
        norm_mix=g["norm_mix"][0], norm_ffn=g["norm_ffn"][0],
        w_up=jnp.concatenate([g["wupg"], g["wupv"]], axis=1),
        conv_ffn=jnp.concatenate([g["conv_g"][:3], g["conv_v"][:3]], axis=1), w_down=g["wdown"])


def big_grad_shards(g):
    in_shards = [_cat([g[name][:, off:off + width] for name, off, width in
                       _overlaps(s * IN_SHARD, (s + 1) * IN_SHARD, IN_PIECES)]) for s in range(4)]
    half = D_FF // 2
    up_shards = [g["wupg"][:, :half], g["wupg"][:, half:], g["wupv"][:, :half], g["wupv"][:, half:]]
    return dict(w_in=jnp.stack(in_shards), w_up=jnp.stack(up_shards),
                w_down=g["wdown"].reshape(4, D_FF // 4, D_MODEL), w_out=g["wout"].reshape(4, D_MODEL // 4, D_MODEL))


LAYER_PARAMS = ("norm_mix", "w_in", "conv_qkv", "a_log", "dt_bias", "head_norm", "w_pool", "pool_scale", "w_out",
                "norm_ffn", "w_up", "conv_ffn", "w_down")


def pad_rows(meta, x):
    return jnp.concatenate([jnp.zeros((LEAD, D_MODEL), F32), meta.astype(F32), x.astype(F32),
                            jnp.zeros((TAIL, D_MODEL), F32)], axis=0)


MESH = pl.DeviceIdType.MESH
ANY = pl.BlockSpec(memory_space=pl.ANY)


def _place():
    x, y, c = lax.axis_index("x"), lax.axis_index("y"), lax.axis_index("c")
    return x, y, c, [(1 - x, y), (x, 1 - y), (1 - x, 1 - y)]


def _my_chip():
    return 2 * lax.axis_index("x") + lax.axis_index("y")


def gather_shards(packs):
    n = len(packs)

    def body(*refs):
        p_refs, o_refs, (send_sems, recv_sems) = refs[:n], refs[n:2 * n], refs[2 * n:]
        x, y, c, chips = _place()

        def copy(a, k, chip, half, to, src=None):
            dst = o_refs[a].at[2 * chip[0] + chip[1], half]
            return pltpu.make_async_remote_copy(src_ref=dst if src is None else src, dst_ref=dst,
                                                send_sem=send_sems.at[6 * a + k], recv_sem=recv_sems.at[6 * a + k],
                                                device_id=to, device_id_type=MESH)

        first = [copy(a, j, (x, y), c, (*chip, c), src=p_refs[a].at[c]) for a in range(n) for j, chip in enumerate(chips)]
        for cp in first:
            cp.start()
        passed = []
        for a in range(n):
            for j, chip in enumerate(chips):
                copy(a, j, chip, c, (x, y, c)).wait_recv()
                passed.append(copy(a, 3 + j, chip, c, (x, y, 1 - c)))
                passed[-1].start()
        for a in range(n):
            for j, chip in enumerate(chips):
                copy(a, 3 + j, chip, 1 - c, (x, y, c)).wait_recv()
        for cp in first + passed:
            cp.wait_send()

    gathered = pl.pallas_call(
        body, in_specs=[ANY] * n, out_specs=[ANY] * n,
        out_shape=[jax.ShapeDtypeStruct((4,) + p.shape, p.dtype) for p in packs],
        scratch_shapes=[pltpu.SemaphoreType.DMA((6 * n,)), pltpu.SemaphoreType.DMA((6 * n,))],
        name="gather_shards",
    )(*packs)
    me = _my_chip()
    return [lax.dynamic_update_slice(g, p[None], (me,) + (0,) * p.ndim) for g, p in zip(gathered, packs)]


def swap_other_halves(ps):
    n = len(ps)

    def body(*refs):
        p_refs, o_refs, (send_sems, recv_sems) = refs[:n], refs[n:2 * n], refs[2 * n:]
        x, y, c, _ = _place()
        copies = [pltpu.make_async_remote_copy(src_ref=p_refs[a].at[s, 1 - c], dst_ref=o_refs[a].at[s],
                                               send_sem=send_sems.at[4 * a + s], recv_sem=recv_sems.at[4 * a + s],
                                               device_id=(x, y, 1 - c), device_id_type=MESH)
                  for a in range(n) for s in range(4)]
        for cp in copies:
            cp.start()
        for cp in copies:
            cp.wait()

    return pl.pallas_call(
        body, in_specs=[ANY] * n, out_specs=[ANY] * n,
        out_shape=[jax.ShapeDtypeStruct((4,) + p.shape[2:], p.dtype) for p in ps],
        scratch_shapes=[pltpu.SemaphoreType.DMA((4 * n,)), pltpu.SemaphoreType.DMA((4 * n,))],
        name="swap_other_halves",
    )(*ps)


def scatter_to_chips(qs):
    n = len(qs)

    def body(*refs):
        q_refs, o_refs, (send_sems, recv_sems) = refs[:n], refs[n:2 * n], refs[2 * n:]
        x, y, c, chips = _place()
        me = 2 * x + y
        copies = [pltpu.make_async_remote_copy(src_ref=q_refs[a].at[2 * chip[0] + chip[1]], dst_ref=o_refs[a].at[me],
                                               send_sem=send_sems.at[3 * a + j], recv_sem=recv_sems.at[3 * a + j],
                                               device_id=(*chip, c), device_id_type=MESH)
                  for a in range(n) for j, chip in enumerate(chips)]
        for cp in copies:
            cp.start()
        for a in range(n):
            for j, chip in enumerate(chips):
                slot = o_refs[a].at[2 * chip[0] + chip[1]]
                pltpu.make_async_remote_copy(src_ref=slot, dst_ref=slot, send_sem=send_sems.at[3 * a + j],
                                             recv_sem=recv_sems.at[3 * a + j],
                                             device_id=(x, y, c), device_id_type=MESH).wait_recv()
        for cp in copies:
            cp.wait_send()

    received = pl.pallas_call(
        body, in_specs=[ANY] * n, out_specs=[ANY] * n,
        out_shape=[jax.ShapeDtypeStruct(q.shape, q.dtype) for q in qs],
        scratch_shapes=[pltpu.SemaphoreType.DMA((3 * n,)), pltpu.SemaphoreType.DMA((3 * n,))],
        name="scatter_to_chips",
    )(*qs)
    me = _my_chip()
    return [lax.dynamic_update_slice(r, lax.dynamic_slice_in_dim(q, me, 1, axis=0), (me, 0, 0))
            for r, q in zip(received, qs)]


def join_halves(boths):
    n = len(boths)

    def body(*refs):
        o_refs, (send_sems, recv_sems) = refs[n:2 * n], refs[2 * n:]
        x, y, c, _ = _place()
        copies = [pltpu.make_async_remote_copy(src_ref=o_refs[a].at[c], dst_ref=o_refs[a].at[c],
                                               send_sem=send_sems.at[a], recv_sem=recv_sems.at[a],
                                               device_id=(x, y, 1 - c), device_id_type=MESH) for a in range(n)]
        for cp in copies:
            cp.start()
        for a in range(n):
            other = o_refs[a].at[1 - c]
            pltpu.make_async_remote_copy(src_ref=other, dst_ref=other, send_sem=send_sems.at[a],
                                         recv_sem=recv_sems.at[a], device_id=(x, y, c), device_id_type=MESH).wait_recv()
        for cp in copies:
            cp.wait_send()

    return pl.pallas_call(
        body, in_specs=[ANY] * n, out_specs=[ANY] * n,
        out_shape=[jax.ShapeDtypeStruct(b.shape, b.dtype) for b in boths],
        input_output_aliases={a: a for a in range(n)},
        scratch_shapes=[pltpu.SemaphoreType.DMA((n,)), pltpu.SemaphoreType.DMA((n,))], name="join_halves",
    )(*boths)


def add_own_half(p, other, c, out_dtype, name):
    _, _, rows, lanes = p.shape
    tr = _tile(rows, max(16, 524288 // lanes), 16)

    def body(c_ref, p_ref, o_ref, out_ref):
        out_ref[...] = (p_ref[...] + o_ref[...]).astype(out_dtype)

    return pl.pallas_call(
        body,
        grid_spec=pltpu.PrefetchScalarGridSpec(
            num_scalar_prefetch=1, grid=(4, rows // tr),
            in_specs=[pl.BlockSpec((None, None, tr, lanes), lambda s, i, c_ref: (s, c_ref[0], i, 0)),
                      pl.BlockSpec((None, tr, lanes), lambda s, i, c_ref: (s, i, 0))],
            out_specs=pl.BlockSpec((None, tr, lanes), lambda s, i, c_ref: (s, i, 0))),
        out_shape=jax.ShapeDtypeStruct((4, rows, lanes), out_dtype),
        compiler_params=_params("parallel", "parallel"), name=name,
    )(c, p, other)


def sum_chips(b, c, name):
    _, rows, lanes = b.shape
    tr = _tile(rows, max(16, 524288 // lanes), 16)

    def body(c_ref, b_ref, out_ref):
        b0, b1, b2, b3 = (b_ref[k].astype(F32) for k in range(4))
        out_ref[...] = ((b0 + b1) + b2) + b3

    return pl.pallas_call(
        body,
        grid_spec=pltpu.PrefetchScalarGridSpec(
            num_scalar_prefetch=1, grid=(rows // tr,),
            in_specs=[pl.BlockSpec((4, tr, lanes), lambda i, c_ref: (0, i, 0))],
            out_specs=pl.BlockSpec((None, tr, lanes), lambda i, c_ref: (c_ref[0], i, 0))),
        out_shape=jax.ShapeDtypeStruct((2, rows, lanes), F32),
        compiler_params=_params("parallel"), name=name,
    )(c, b)


def all_reduce_to_shards(packs, wires, tags, c):
    others = swap_other_halves(packs)
    qs = [add_own_half(p, o, c, wire, f"add_own_half_{tag}") for p, o, wire, tag in zip(packs, others, wires, tags)]
    return join_halves([sum_chips(r, c, f"sum_chips_{tag}") for r, tag in zip(scatter_to_chips(qs), tags)])


def adamw(w, g, m, v, name):
    shape = w.shape
    cols = shape[-1]
    w2, g2, m2, v2 = (a.reshape(-1, cols) for a in (w, g, m, v))
    rows = w2.shape[0]
    tr = _tile(rows, max(8, 262144 // cols), 8) if rows % 8 == 0 else rows
    c1 = 1.0 - ADAM_B1 ** ADAM_STEP
    c2 = 1.0 - ADAM_B2 ** ADAM_STEP

    def body(w_ref, g_ref, m_ref, v_ref, d_ref, mo_ref, vo_ref):
        gv = g_ref[...]
        mn = ADAM_B1 * m_ref[...] + (1.0 - ADAM_B1) * gv
        vn = ADAM_B2 * v_ref[...] + (1.0 - ADAM_B2) * jnp.square(gv)
        d_ref[...] = -ADAM_LR * ((mn / c1) / (jnp.sqrt(vn / c2) + ADAM_EPS) + ADAM_WD * w_ref[...])
        mo_ref[...] = mn
        vo_ref[...] = vn

    spec = pl.BlockSpec((tr, cols), lambda i: (i, 0))
    out = jax.ShapeDtypeStruct((rows, cols), F32)
    d, mn, vn = pl.pallas_call(
        body, grid=(rows // tr,), in_specs=[spec] * 4, out_specs=(spec,) * 3, out_shape=(out,) * 3,
        compiler_params=_params("parallel"), name=name,
    )(w2, g2, m2, v2)
    return d.reshape(shape), mn.reshape(shape), vn.reshape(shape)


BIG = ("w_in", "w_up", "w_down", "w_out")
SMALL = ("w_pool", "conv_qkv", "conv_ffn", "meta_tokens")
SHARDED = BIG + SMALL
MATMUL_WEIGHTS = BIG + ("w_pool",)
REPLICATED = ("norm_mix", "a_log", "dt_bias", "head_norm", "pool_scale", "norm_ffn", "norm_final")
SHARD_AXIS = {"w_in": 2, "w_up": 2, "w_out": 1, "w_down": 1, "w_pool": 3, "conv_qkv": 2, "conv_ffn": 2, "meta_tokens": 1}


def _rows_of(a):
    return a.reshape(-1, 128)


SEGMENT_ROWS = 16


def _segment(n_rows):
    return -(-n_rows // SEGMENT_ROWS) * SEGMENT_ROWS


def _pad_segment(a):
    pad = [(0, 0)] * a.ndim
    pad[-2] = (0, _segment(a.shape[-2]) - a.shape[-2])
    return jnp.pad(a, pad)


def _unshard(stacked, axis):
    full = jnp.moveaxis(stacked, 0, axis)
    shape = list(full.shape)
    shape[axis:axis + 2] = [shape[axis] * shape[axis + 1]]
    return full.reshape(shape)


def _shard_stack(full, axis):
    shape = list(full.shape)
    shape[axis:axis + 1] = [4, shape[axis] // 4]
    return jnp.moveaxis(full.reshape(shape), axis, 0)


def pack_weights(shards):
    parts = [_rows_of(shards[k].astype(WIRE)) if k in MATMUL_WEIGHTS else
             lax.bitcast_convert_type(_rows_of(shards[k].astype(F32)), WIRE).reshape(-1, 128) for k in SMALL]
    parts = [_pad_segment(p) for p in parts]
    rows = sum(p.shape[0] for p in parts)
    if rows % (2 * SEGMENT_ROWS):
        parts.append(jnp.zeros((SEGMENT_ROWS, 128), WIRE))
        rows += SEGMENT_ROWS
    return [shards[k].astype(WIRE) for k in BIG] + [jnp.concatenate(parts, axis=0).reshape(2, rows // 2, 128)]


def unpack_weights(gathered, shard_shapes):
    out = {k: _unshard(g, SHARD_AXIS[k]) for k, g in zip(BIG, gathered)}
    flat = gathered[-1].reshape(4, -1, 128)
    at = 0
    for k in SMALL:
        shp = shard_shapes[k]
        n = 1
        for e in shp:
            n *= e
        if k in MATMUL_WEIGHTS:
            r = n // 128
            stacked = flat[:, at:at + r].reshape((4,) + tuple(shp))
        else:
            r = 2 * n // 128
            stacked = lax.bitcast_convert_type(flat[:, at:at + r].reshape(4, n // 128, 128, 2), F32).reshape((4,) + tuple(shp))
        out[k] = _unshard(stacked, SHARD_AXIS[k])
        at += _segment(r)
    return out


def pack_grads(big, full, repl):
    r = jnp.concatenate([repl[k].reshape(-1) for k in REPLICATED])
    r = jnp.pad(r, (0, -r.shape[0] % 128)).reshape(1, -1, 128)
    parts = [_shard_stack(full[k], SHARD_AXIS[k]).reshape(4, -1, 128) for k in SMALL]
    parts = [_pad_segment(p) for p in parts + [jnp.broadcast_to(r, (4,) + r.shape[1:])]]
    rows = sum(p.shape[1] for p in parts)
    if rows % (2 * SEGMENT_ROWS):
        parts.append(jnp.zeros((4, SEGMENT_ROWS, 128), F32))
        rows += SEGMENT_ROWS
    side = jnp.concatenate(parts, axis=1).reshape(4, 2, rows // 2, 128)
    return list(big) + [side]


def unpack_grads(reduced, shard_shapes, repl_shapes):
    out = dict(zip(BIG, reduced))
    side = reduced[-1].reshape(-1, 128)
    at = 0
    for k in SMALL:
        n = 1
        for e in shard_shapes[k]:
            n *= e
        out[k] = side[at:at + n // 128].reshape(shard_shapes[k])
        at += _segment(n // 128)
    r = side[at:].reshape(-1)
    at = 0
    for k in REPLICATED:
        n = 1
        for e in repl_shapes[k]:
            n *= e
        out[k] = r[at:at + n].reshape(repl_shapes[k])
        at += n
    return out


WEIGHT_ORDER = ("meta_tokens", "norm_mix", "w_in", "conv_qkv", "a_log", "dt_bias", "head_norm", "w_pool", "pool_scale",
                "w_out", "norm_ffn", "w_up", "conv_ffn", "w_down", "norm_final")


def kernel(x, meta_tokens, norm_mix, w_in, conv_qkv, a_log, dt_bias, head_norm, w_pool, pool_scale, w_out, norm_ffn, w_up, conv_ffn, w_down, norm_final, loss_target, m_meta_tokens, m_norm_mix, m_w_in, m_conv_qkv, m_a_log, m_dt_bias, m_head_norm, m_w_pool, m_pool_scale, m_w_out, m_norm_ffn, m_w_up, m_conv_ffn, m_w_down, m_norm_final, v_meta_tokens, v_norm_mix, v_w_in, v_conv_qkv, v_a_log, v_dt_bias, v_head_norm, v_w_pool, v_pool_scale, v_w_out, v_norm_ffn, v_w_up, v_conv_ffn, v_w_down, v_norm_final):
    weights = dict(meta_tokens=meta_tokens, norm_mix=norm_mix, w_in=w_in, conv_qkv=conv_qkv, a_log=a_log,
                   dt_bias=dt_bias, head_norm=head_norm, w_pool=w_pool, pool_scale=pool_scale, w_out=w_out,
                   norm_ffn=norm_ffn, w_up=w_up, conv_ffn=conv_ffn, w_down=w_down, norm_final=norm_final)
    m_in = dict(zip(WEIGHT_ORDER, (m_meta_tokens, m_norm_mix, m_w_in, m_conv_qkv, m_a_log, m_dt_bias, m_head_norm,
                                   m_w_pool, m_pool_scale, m_w_out, m_norm_ffn, m_w_up, m_conv_ffn, m_w_down, m_norm_final)))
    v_in = dict(zip(WEIGHT_ORDER, (v_meta_tokens, v_norm_mix, v_w_in, v_conv_qkv, v_a_log, v_dt_bias, v_head_norm,
                                   v_w_pool, v_pool_scale, v_w_out, v_norm_ffn, v_w_up, v_conv_ffn, v_w_down, v_norm_final)))
    shard_shapes = {k: weights[k].shape for k in SHARDED}
    repl_shapes = {k: weights[k].shape for k in REPLICATED}
    core = lax.axis_index("c").astype(jnp.int32).reshape(1)

    gathered = gather_shards(pack_weights({k: weights[k] for k in SHARDED}))
    full = unpack_weights(gathered, shard_shapes)
    shards = dict(zip(BIG, gathered))
    layers = []
    for li in range(DEPTH):
        p = {k: (full[k][li] if k in SMALL else weights[k][li]) for k in LAYER_PARAMS if k not in BIG}
        p.update(w_in=[shards["w_in"][s, li] for s in range(4)], w_up=[shards["w_up"][s, li] for s in range(4)],
                 w_down=shards["w_down"][:, li].reshape(D_FF, D_MODEL),
                 w_out=shards["w_out"][:, li].reshape(D_MODEL, D_MODEL))
        layers.append(prep_layer(p))

    h0 = pad_rows(full["meta_tokens"], x[0])
    target = pad_rows(jnp.zeros((N_META, D_MODEL), F32), loss_target[0])
    loss, dh0, grads, d_norm_final = local_step(h0, target, layers, norm_final.reshape(1, D_MODEL))
    seq = x.shape[1]
    grad_x = dh0[LEAD + N_META:LEAD + N_META + seq][None]

    per_layer = [layer_grads(g) for g in grads]
    g_all = {k: jnp.stack([pl_[k] for pl_ in per_layer]) for k in LAYER_PARAMS if k not in BIG}
    g_all["meta_tokens"] = dh0[LEAD:LEAD + N_META]
    g_all["norm_final"] = d_norm_final[0]
    big = [big_grad_shards(g) for g in grads]
    packs = pack_grads([jnp.stack([b[k] for b in big], axis=1) for k in BIG],
                       {k: g_all[k] for k in SMALL}, {k: g_all[k] for k in REPLICATED})
    reduced = all_reduce_to_shards(packs, [GRAD_WIRE] * len(BIG) + [F32], BIG + ("side",), core)
    g_mine = unpack_grads(reduced, shard_shapes, repl_shapes)

    loss_sum = lax.psum(loss[0, 0], ("x", "y", "c"))
    deltas, new_m, new_v = {}, {}, {}
    for k in WEIGHT_ORDER:
        deltas[k], new_m[k], new_v[k] = adamw(weights[k], g_mine[k], m_in[k], v_in[k], f"adamw_{k}")
    return (loss_sum, grad_x, *[g_mine[k] for k in WEIGHT_ORDER], *[deltas[k] for k in WEIGHT_ORDER],
            *[new_m[k] for k in WEIGHT_ORDER], *[new_v[k] for k in WEIGHT_ORDER])
```

```python
import functools

import jax
import jax.numpy as jnp
from jax import lax
from jax.experimental import pallas as pl
from jax.experimental.pallas import tpu as pltpu

F32 = jnp.float32
BF16 = jnp.bfloat16
WIRE = jnp.bfloat16
GRAD_WIRE = jnp.bfloat16

D_MODEL = 1024
HEADS = 8
HEAD_DIM = 128
CHUNK = 64
N_META = 16
LEAD = 48
TAIL = 64
QKV_DIM = 3072
D_FF = 2816
POOL_WIDTH = 512
POOL_WINDOWS = (2, 4, 8, 16)
BA_BLOCK = POOL_WIDTH // 128
DEPTH = 2
NORM_EPS = 1e-6
ADAM_LR, ADAM_B1, ADAM_B2, ADAM_EPS, ADAM_WD, ADAM_STEP = 0.001, 0.9, 0.999, 1e-08, 0.01, 10
VMEM_LIMIT_BYTES = 48 * 1024 * 1024


def _params(*sem):
    return pltpu.CompilerParams(dimension_semantics=sem if sem else None, vmem_limit_bytes=VMEM_LIMIT_BYTES)


def _tile(n, cap, mult):
    best = None
    for t in range(mult, min(n, cap) + 1, mult):
        if n % t == 0:
            best = t
    assert best is not None, (n, cap, mult)
    return best


def _silu(x):
    return x * jax.nn.sigmoid(x)


def _softplus(x):
    return jnp.maximum(x, 0.0) + jnp.log(1.0 + jnp.exp(-jnp.abs(x)))


def _split_bf16(a):
    hi = a.astype(BF16)
    return hi, (a - hi.astype(F32)).astype(BF16)


def _dg(a, b, ca, cb, hi):
    dims = (((ca,), (cb,)), ((), ()))
    if hi is True:
        return lax.dot_general(a, b, dims, precision=lax.Precision.HIGHEST, preferred_element_type=F32)
    if hi == 3:
        (ah, al), (bh, bl) = _split_bf16(a), _split_bf16(b)
        dot = lambda x, y: lax.dot_general(x, y, dims, preferred_element_type=F32)
        return dot(ah, bh) + (dot(ah, bl) + dot(al, bh))
    return lax.dot_general(a.astype(BF16), b.astype(BF16), dims, preferred_element_type=F32)


def _make_dots(hi):
    @jax.custom_vjp
    def nn(a, b):
        return _dg(a, b, 1, 0, hi)

    @jax.custom_vjp
    def nt(a, b):
        return _dg(a, b, 1, 1, hi)

    @jax.custom_vjp
    def tn(a, b):
        return _dg(a, b, 0, 0, hi)

    nn.defvjp(lambda a, b: (nn(a, b), (a, b)), lambda r, g: (nt(g, r[1]), tn(r[0], g)))
    nt.defvjp(lambda a, b: (nt(a, b), (a, b)), lambda r, g: (nn(g, r[1]), tn(g, r[0])))
    tn.defvjp(lambda a, b: (tn(a, b), (a, b)), lambda r, g: (nt(r[1], g), nn(r[0], g)))
    return nn, nt, tn


_nn, _nt, _tn = _make_dots(False)
_hnn, _hnt, _htn = _make_dots(True)


def _neumann(a):
    n = a[0].shape[0]
    eye = (lax.broadcasted_iota(jnp.int32, (n, n), 0) == lax.broadcasted_iota(jnp.int32, (n, n), 1)).astype(F32)
    hd = range(len(a))
    p = [_dg(a[h], a[h], 1, 0, 3) for h in hd]
    x = [(eye - a[h]) + p[h] - _dg(a[h], p[h], 1, 0, False) for h in hd]
    for _ in range(4):
        p = [_dg(p[h], p[h], 1, 0, False) for h in hd]
        x = [x[h] + p[h] + _dg(x[h] - eye, p[h], 1, 0, False) for h in hd]
    return tuple(x)


@jax.custom_vjp
def _inv_unit_lower(a):
    return _neumann(a)


def _inv_unit_lower_bwd(x, g):
    t = [_dg(x[h], g[h], 0, 0, 3) for h in range(len(x))]
    return (tuple(-_dg(t[h], x[h], 1, 1, 3) for h in range(len(x))),)


_inv_unit_lower.defvjp(lambda a: (_neumann(a),) * 2, _inv_unit_lower_bwd)


@jax.custom_vjp
def _kept_inverse(a, x):
    return x


_kept_inverse.defvjp(lambda a, x: (x, x),
                     lambda x, g: _inv_unit_lower_bwd(x, g) + (tuple(jnp.zeros_like(e) for e in x),))


def mm(a, b, *, tb=False, add=None, norm_gain=None, out_dtype=F32, name):
    m, kdim = a.shape
    (n, kb) = b.shape if tb else b.shape[::-1]
    assert kdim == kb, (a.shape, b.shape, tb)
    tm = _tile(m, 1408, 128) if m % 128 == 0 and m <= 4096 else _tile(m, 640, 64)
    tn = _tile(n, 3072 if (kdim <= 1024 and not tb and add is None) else 1536, 128)
    tk = kdim if kdim <= 3072 else _tile(kdim, 1664 if a.dtype == b.dtype == BF16 else 640, 128)
    nk = kdim // tk
    dims = (((1,), (1 if tb else 0,)), ((), ()))

    normed = norm_gain is not None
    assert not normed or (tn == n and nk == 1 and tm % 128 == 0), (name, tm, tn, nk)

    def body(*refs):
        refs = list(refs)
        a_ref, b_ref = refs[0], refs[1]
        add_ref = refs.pop(2) if add is not None else None
        gain_ref = refs.pop(2) if normed else None
        o_ref, acc = refs[2], refs[-1]
        k = pl.program_id(2)
        part = lax.dot_general(a_ref[...].astype(BF16), b_ref[...].astype(BF16), dims, preferred_element_type=F32)

        def finish(r):
            if add is not None:
                r = r + add_ref[...]
            o_ref[...] = r.astype(out_dtype)
            if normed:
                u = _rms(r, gain_ref[...])
                refs[3][...] = u.astype(BF16)
                refs[4][...] = u.T.astype(BF16)

        if nk == 1:
            finish(part)
        else:
            @pl.when(k == 0)
            def _():
                acc[...] = part

            @pl.when(jnp.logical_and(k > 0, k < nk - 1))
            def _():
                acc[...] += part

            @pl.when(k == nk - 1)
            def _():
                finish(acc[...] + part)

    a_spec = pl.BlockSpec((tm, tk), lambda j, i, k: (i, k))
    b_spec = pl.BlockSpec((tn, tk), lambda j, i, k: (j, k)) if tb else pl.BlockSpec((tk, tn), lambda j, i, k: (k, j))
    in_specs = [a_spec, b_spec]
    args = [a, b]
    if add is not None:
        in_specs.append(pl.BlockSpec((tm, tn), lambda j, i, k: (i, j)))
        args.append(add)
    out_specs = pl.BlockSpec((tm, tn), lambda j, i, k: (i, j))
    out_shape = jax.ShapeDtypeStruct((m, n), out_dtype)
    if normed:
        in_specs.append(pl.BlockSpec((1, n), lambda j, i, k: (0, 0)))
        args.append(norm_gain)
        out_specs = (out_specs, out_specs, pl.BlockSpec((n, tm), lambda j, i, k: (0, i)))
        out_shape = (out_shape, jax.ShapeDtypeStruct((m, n), BF16), jax.ShapeDtypeStruct((n, m), BF16))
    return pl.pallas_call(
        body, grid=(n // tn, m // tm, nk), in_specs=in_specs, out_specs=out_specs, out_shape=out_shape,
        scratch_shapes=[pltpu.VMEM((tm, tn) if nk > 1 else (8, 128), F32)],
        compiler_params=_params("parallel", "parallel", "arbitrary"), name=name,
    )(*args)


def _rms(x, gain):
    return x * lax.rsqrt(jnp.mean(x * x, axis=-1, keepdims=True) + NORM_EPS) * gain


def rms_fwd(h, gain, name):
    t = h.shape[0]
    ts = _tile(t, 640, 128)

    def body(h_ref, g_ref, u_ref, ut_ref):
        u = _rms(h_ref[...], g_ref[...])
        u_ref[...] = u.astype(BF16)
        ut_ref[...] = u.T.astype(BF16)

    return pl.pallas_call(
        body, grid=(t // ts,),
        in_specs=[pl.BlockSpec((ts, D_MODEL), lambda i: (i, 0)), pl.BlockSpec((1, D_MODEL), lambda i: (0, 0))],
        out_specs=(pl.BlockSpec((ts, D_MODEL), lambda i: (i, 0)), pl.BlockSpec((D_MODEL, ts), lambda i: (0, i))),
        out_shape=(jax.ShapeDtypeStruct((t, D_MODEL), BF16), jax.ShapeDtypeStruct((D_MODEL, t), BF16)),
        compiler_params=_params("parallel"), name=name,
    )(h, gain)


def rms_bwd(h, gain, du, dres, name):
    t = h.shape[0]
    ts = _tile(t, 640, 64)

    def body(h_ref, g_ref, du_ref, dres_ref, dh_ref, dhb_ref, dg_ref):
        i = pl.program_id(0)
        _, vjp = jax.vjp(_rms, h_ref[...], g_ref[...])
        dx, dg = vjp(du_ref[...])
        row = i * ts + lax.broadcasted_iota(jnp.int32, (ts, 1), 0)
        dh = jnp.where(row >= LEAD, dx + dres_ref[...], 0.0)
        dh_ref[...] = dh
        dhb_ref[...] = dh.astype(BF16)

        @pl.when(i == 0)
        def _():
            dg_ref[...] = jnp.zeros_like(dg_ref)

        dg_ref[...] += dg

    row_spec = pl.BlockSpec((ts, D_MODEL), lambda i: (i, 0))
    vec_spec = pl.BlockSpec((1, D_MODEL), lambda i: (0, 0))
    return pl.pallas_call(
        body, grid=(t // ts,), in_specs=[row_spec, vec_spec, row_spec, row_spec],
        out_specs=(row_spec, row_spec, vec_spec),
        out_shape=(jax.ShapeDtypeStruct((t, D_MODEL), F32), jax.ShapeDtypeStruct((t, D_MODEL), BF16),
                   jax.ShapeDtypeStruct((1, D_MODEL), F32)),
        compiler_params=_params("arbitrary"), name=name,
    )(h, gain, du, dres)


def loss_head(h, gain, target, name):
    t = h.shape[0]
    ts = _tile(t, 640, 64)

    def body(h_ref, g_ref, t_ref, loss_ref, dh_ref, dhb_ref, dg_ref):
        i = pl.program_id(0)
        row = i * ts + lax.broadcasted_iota(jnp.int32, (ts, 1), 0)
        keep = jnp.logical_and(row >= LEAD + N_META, row < t - TAIL)
        tgt = t_ref[...]

        def f(x, g):
            err = jnp.where(keep, _rms(x, g) - tgt, 0.0)
            per_row = jnp.mean(err * err, axis=-1, keepdims=True)
            return 0.5 * jnp.sum(per_row, axis=0, keepdims=True)

        val, vjp = jax.vjp(f, h_ref[...], g_ref[...])
        dx, dg = vjp(jnp.ones((1, 1), F32))
        dh_ref[...] = dx
        dhb_ref[...] = dx.astype(BF16)

        @pl.when(i == 0)
        def _():
            dg_ref[...] = jnp.zeros_like(dg_ref)
            loss_ref[...] = jnp.zeros_like(loss_ref)

        dg_ref[...] += dg
        loss_ref[...] += jnp.broadcast_to(val, (1, 128))

    row_spec = pl.BlockSpec((ts, D_MODEL), lambda i: (i, 0))
    vec_spec = pl.BlockSpec((1, D_MODEL), lambda i: (0, 0))
    return pl.pallas_call(
        body, grid=(t // ts,), in_specs=[row_spec, vec_spec, row_spec],
        out_specs=(pl.BlockSpec((1, 128), lambda i: (0, 0)), row_spec, row_spec, vec_spec),
        out_shape=(jax.ShapeDtypeStruct((1, 128), F32), jax.ShapeDtypeStruct((t, D_MODEL), F32),
                   jax.ShapeDtypeStruct((t, D_MODEL), BF16), jax.ShapeDtypeStruct((1, D_MODEL), F32)),
        compiler_params=_params("arbitrary"), name=name,
    )(h, gain, target)


def _rows_down(a, s):
    return a if s == 0 else pltpu.roll(a, s, axis=0)


def _rows_up(a, s):
    return a if s == 0 else pltpu.roll(a, a.shape[0] - s, axis=0)


def conv_fwd(x, w, name):
    t, width = x.shape
    k = w.shape[0]
    ts = _tile(t, 640, 64)
    tw = _tile(width, 1536, 128)
    hb = ts // 8

    def body(x_ref, halo_ref, w_ref, o_ref, buf):
        i = pl.program_id(0)
        buf[0:8, :] = jnp.where(i > 0, halo_ref[...], 0.0)
        buf[8:, :] = x_ref[...]
        ext = buf[...]
        wv = w_ref[...]
        acc = _rows_down(ext, k - 1)[8:, :] * wv[0:1, :]
        for j in range(1, k):
            acc = acc + _rows_down(ext, k - 1 - j)[8:, :] * wv[j:j + 1, :]
        o_ref[...] = acc

    return pl.pallas_call(
        body, grid=(t // ts, width // tw),
        in_specs=[pl.BlockSpec((ts, tw), lambda i, j: (i, j)),
                  pl.BlockSpec((8, tw), lambda i, j: (jnp.maximum(i * hb - 1, 0), j)),
                  pl.BlockSpec((k, tw), lambda i, j: (0, j))],
        out_specs=pl.BlockSpec((ts, tw), lambda i, j: (i, j)),
        out_shape=jax.ShapeDtypeStruct((t, width), F32),
        scratch_shapes=[pltpu.VMEM((ts + 8, tw), F32)],
        compiler_params=_params("parallel", "parallel"), name=name,
    )(x, x, w)


def conv_bwd(x, dc, w, name):
    t, width = x.shape
    k = w.shape[0]
    ts = _tile(t, 640, 64)
    tw = _tile(width, 1536, 128)
    hb = ts // 8
    nt = t // ts

    def body(x_ref, xh_ref, dc_ref, dch_ref, w_ref, dx_ref, dw_ref, xbuf, dbuf):
        i = pl.program_id(1)
        xbuf[0:8, :] = jnp.where(i > 0, xh_ref[...], 0.0)
        xbuf[8:, :] = x_ref[...]
        d = dc_ref[...]
        dbuf[0:ts, :] = d
        dbuf[ts:, :] = jnp.where(i < nt - 1, dch_ref[...], 0.0)
        wv = w_ref[...]
        ext_x, ext_d = xbuf[...], dbuf[...]
        acc = _rows_up(ext_d, k - 1)[0:ts, :] * wv[0:1, :]
        for j in range(1, k):
            acc = acc + _rows_up(ext_d, k - 1 - j)[0:ts, :] * wv[j:j + 1, :]
        dx_ref[...] = acc.astype(BF16)

        @pl.when(i == 0)
        def _():
            dw_ref[...] = jnp.zeros_like(dw_ref)

        for j in range(k):
            dw_ref[j:j + 1, :] += jnp.sum(d * _rows_down(ext_x, k - 1 - j)[8:, :], axis=0, keepdims=True)

    return pl.pallas_call(
        body, grid=(width // tw, nt),
        in_specs=[pl.BlockSpec((ts, tw), lambda j, i: (i, j)),
                  pl.BlockSpec((8, tw), lambda j, i: (jnp.maximum(i * hb - 1, 0), j)),
                  pl.BlockSpec((ts, tw), lambda j, i: (i, j)),
                  pl.BlockSpec((8, tw), lambda j, i: (jnp.minimum((i + 1) * hb, t // 8 - 1), j)),
                  pl.BlockSpec((k, tw), lambda j, i: (0, j))],
        out_specs=(pl.BlockSpec((ts, tw), lambda j, i: (i, j)), pl.BlockSpec((8, tw), lambda j, i: (0, j))),
        out_shape=(jax.ShapeDtypeStruct((t, width), BF16), jax.ShapeDtypeStruct((8, width), F32)),
        scratch_shapes=[pltpu.VMEM((ts + 8, tw), F32), pltpu.VMEM((ts + 8, tw), F32)],
        compiler_params=_params("parallel", "arbitrary"), name=name,
    )(x, x, dc, dc, w)


def _pool_count(pos, win):
    return jnp.clip(pos + 1, 1, win).astype(F32)


def poolwin_fwd(p, name):
    t = p.shape[0]
    ts = _tile(t, 640, 64)
    hb = ts // 16

    def body(p_ref, halo_ref, o_ref, buf):
        i = pl.program_id(0)
        buf[0:16, :] = jnp.where(i > 0, halo_ref[...], 0.0)
        buf[16:, :] = p_ref[...]
        pos = i * ts + lax.broadcasted_iota(jnp.int32, (ts, 1), 0) - LEAD
        ext = buf[...]
        own = ext[16:, :]
        sums, span = ext, 1
        for gi, win in enumerate(POOL_WINDOWS):
            while span < win:
                sums = sums + _rows_down(sums, span)
                span *= 2
            cols = slice(gi * 128, (gi + 1) * 128)
            o_ref[:, cols] = sums[16:, cols] / _pool_count(pos, win) - own[:, cols]

    return pl.pallas_call(
        body, grid=(t // ts,),
        in_specs=[pl.BlockSpec((ts, POOL_WIDTH), lambda i: (i, 0)),
                  pl.BlockSpec((16, POOL_WIDTH), lambda i: (jnp.maximum(i * hb - 1, 0), 0))],
        out_specs=pl.BlockSpec((ts, POOL_WIDTH), lambda i: (i, 0)),
        out_shape=jax.ShapeDtypeStruct((t, POOL_WIDTH), F32),
        scratch_shapes=[pltpu.VMEM((ts + 16, POOL_WIDTH), F32)],
        compiler_params=_params("parallel"), name=name,
    )(p, p)


def poolwin_bwd(dpooled, name):
    t = dpooled.shape[0]
    ts = _tile(t, 640, 64)
    hb = ts // 16
    nt = t // ts

    def body(d_ref, halo_ref, o_ref, buf):
        i = pl.program_id(0)
        buf[0:ts, :] = d_ref[...]
        buf[ts:, :] = jnp.where(i < nt - 1, halo_ref[...], 0.0)
        pos = i * ts + lax.broadcasted_iota(jnp.int32, (ts + 16, 1), 0) - LEAD
        ext = buf[...]
        for gi, win in enumerate(POOL_WINDOWS):
            cols = slice(gi * 128, (gi + 1) * 128)
            sums, span = ext[:, cols] / _pool_count(pos, win), 1
            while span < win:
                sums = sums + _rows_up(sums, span)
                span *= 2
            o_ref[:, cols] = (sums[0:ts, :] - ext[0:ts, cols]).astype(BF16)

    return pl.pallas_call(
        body, grid=(nt,),
        in_specs=[pl.BlockSpec((ts, POOL_WIDTH), lambda i: (i, 0)),
                  pl.BlockSpec((16, POOL_WIDTH), lambda i: (jnp.minimum((i + 1) * hb, t // 16 - 1), 0))],
        out_specs=pl.BlockSpec((ts, POOL_WIDTH), lambda i: (i, 0)),
        out_shape=jax.ShapeDtypeStruct((t, POOL_WIDTH), BF16),
        scratch_shapes=[pltpu.VMEM((ts + 16, POOL_WIDTH), F32)],
        compiler_params=_params("parallel"), name=name,
    )(dpooled, dpooled)


def _mix(y_a, gpre, pooled, w_pool, scale):
    parts = [_nn(pooled[:, g * 128:(g + 1) * 128], w_pool[g]) for g in range(4)]
    y_b = jnp.concatenate(parts, axis=1) * scale
    return jax.nn.sigmoid(gpre[:, :D_MODEL]) * y_a + jax.nn.sigmoid(gpre[:, D_MODEL:]) * y_b


def _mix_specs(ts):
    return [pl.BlockSpec((ts, D_MODEL), lambda i: (i, 0)), pl.BlockSpec((ts, 2 * D_MODEL), lambda i: (i, 0)),
            pl.BlockSpec((ts, POOL_WIDTH), lambda i: (i, 0)), pl.BlockSpec((4, 128, 256), lambda i: (0, 0, 0)),
            pl.BlockSpec((1, D_MODEL), lambda i: (0, 0))]


def mix_fwd(y_a, gpre, pooled, w_pool, scale, name):
    t = y_a.shape[0]
    ts = _tile(t, 640, 128)

    def body(ya_ref, g_ref, p_ref, w_ref, s_ref, o_ref, ot_ref):
        y = _mix(ya_ref[...], g_ref[...], p_ref[...], w_ref[...], s_ref[...])
        o_ref[...] = y.astype(BF16)
        ot_ref[...] = y.T.astype(BF16)

    return pl.pallas_call(
        body, grid=(t // ts,), in_specs=_mix_specs(ts),
        out_specs=(pl.BlockSpec((ts, D_MODEL), lambda i: (i, 0)), pl.BlockSpec((D_MODEL, ts), lambda i: (0, i))),
        out_shape=(jax.ShapeDtypeStruct((t, D_MODEL), BF16), jax.ShapeDtypeStruct((D_MODEL, t), BF16)),
        compiler_params=_params("parallel"), name=name,
    )(y_a, gpre, pooled, w_pool, scale)


def mix_bwd(y_a, gpre, pooled, w_pool, scale, dy, name):
    t = y_a.shape[0]
    ts = _tile(t, 320, 64)

    def body(ya_ref, g_ref, p_ref, w_ref, s_ref, dy_ref, dya_ref, dg_ref, dp_ref, dw_ref, ds_ref):
        i = pl.program_id(0)
        _, vjp = jax.vjp(_mix, ya_ref[...], g_ref[...], p_ref[...], w_ref[...], s_ref[...])
        dya, dg, dp, dw, ds = vjp(dy_ref[...])
        dya_ref[...] = dya
        dg_ref[...] = dg.astype(BF16)
        dp_ref[...] = dp

        @pl.when(i == 0)
        def _():
            dw_ref[...] = jnp.zeros_like(dw_ref)
            ds_ref[...] = jnp.zeros_like(ds_ref)

        dw_ref[...] += dw
        ds_ref[...] += ds

    specs = _mix_specs(ts)
    return pl.pallas_call(
        body, grid=(t // ts,), in_specs=specs + [specs[0]],
        out_specs=(specs[0], specs[1], specs[2], specs[3], specs[4]),
        out_shape=(jax.ShapeDtypeStruct((t, D_MODEL), F32), jax.ShapeDtypeStruct((t, 2 * D_MODEL), BF16),
                   jax.ShapeDtypeStruct((t, POOL_WIDTH), F32), jax.ShapeDtypeStruct((4, 128, 256), F32),
                   jax.ShapeDtypeStruct((1, D_MODEL), F32)),
        compiler_params=_params("arbitrary"), name=name,
    )(y_a, gpre, pooled, w_pool, scale, dy)


def _ffn_act(cg, cv):
    return _silu(cg) * cv


FFN_MID_VMEM_BYTES = 58 * 1024 * 1024


def ffn_mid_fwd(hg, hv, wg, wv, name):
    t, width = hg.shape
    k = wg.shape[0]
    ts = _tile(t, 640, 128)
    tw = _tile(width, 1536, 128)
    hb = ts // 8

    def body(hg_ref, hgp_ref, hv_ref, hvp_ref, wg_ref, wv_ref, cg_ref, cv_ref, act_ref, actt_ref, xg, xv):
        i = pl.program_id(0)
        convs = []
        for x_ref, xp_ref, buf, w_ref, c_ref in ((hg_ref, hgp_ref, xg, wg_ref, cg_ref),
                                                 (hv_ref, hvp_ref, xv, wv_ref, cv_ref)):
            buf[0:8, :] = jnp.where(i > 0, xp_ref[...], 0.0)
            buf[8:, :] = x_ref[...]
            ext, wt = buf[...], w_ref[...]
            acc = _rows_down(ext, k - 1)[8:, :] * wt[0:1, :]
            for j in range(1, k):
                acc = acc + _rows_down(ext, k - 1 - j)[8:, :] * wt[j:j + 1, :]
            c_ref[...] = acc
            convs.append(acc)
        act = _ffn_act(*convs)
        act_ref[...] = act.astype(BF16)
        actt_ref[...] = act.T.astype(BF16)

    tile = pl.BlockSpec((ts, tw), lambda i, j: (i, j))
    prev = pl.BlockSpec((8, tw), lambda i, j: (jnp.maximum(i * hb - 1, 0), j))
    taps = pl.BlockSpec((k, tw), lambda i, j: (0, j))
    return pl.pallas_call(
        body, grid=(t // ts, width // tw), in_specs=[tile, prev, tile, prev, taps, taps],
        out_specs=(tile, tile, tile, pl.BlockSpec((tw, ts), lambda i, j: (j, i))),
        out_shape=(jax.ShapeDtypeStruct((t, width), F32), jax.ShapeDtypeStruct((t, width), F32),
                   jax.ShapeDtypeStruct((t, width), BF16), jax.ShapeDtypeStruct((width, t), BF16)),
        scratch_shapes=[pltpu.VMEM((ts + 8, tw), F32)] * 2,
        compiler_params=pltpu.CompilerParams(dimension_semantics=("parallel", "parallel"),
                                             vmem_limit_bytes=FFN_MID_VMEM_BYTES), name=name,
    )(hg, hg, hv, hv, wg, wv)


def ffn_mid_bwd(hg, hv, cg, cv, dact, wg, wv, name):
    t, width = hg.shape
    k = wg.shape[0]
    ts = _tile(t, 320, 64)
    tw = _tile(width, 1536, 128)
    hb = ts // 8
    nt = t // ts

    def body(hg_ref, hgp_ref, hv_ref, hvp_ref, cg_ref, cgn_ref, cv_ref, cvn_ref, da_ref, dan_ref, wg_ref, wv_ref,
             dhg_ref, dhv_ref, dwg_ref, dwv_ref, xg, xv, dg, dv):
        i = pl.program_id(1)
        behind = i < nt - 1

        def d_conv(c_g, c_v, d_a):
            _, vjp = jax.vjp(_ffn_act, c_g, c_v)
            return vjp(d_a)

        dcg, dcv = d_conv(cg_ref[...], cv_ref[...], da_ref[...])
        dcg_n, dcv_n = d_conv(cgn_ref[...], cvn_ref[...], jnp.where(behind, dan_ref[...], 0.0))

        @pl.when(i == 0)
        def _():
            dwg_ref[...] = jnp.zeros_like(dwg_ref)
            dwv_ref[...] = jnp.zeros_like(dwv_ref)

        for x_ref, xp_ref, xbuf, dbuf, d, d_n, w_ref, dx_ref, dw_ref in (
                (hg_ref, hgp_ref, xg, dg, dcg, dcg_n, wg_ref, dhg_ref, dwg_ref),
                (hv_ref, hvp_ref, xv, dv, dcv, dcv_n, wv_ref, dhv_ref, dwv_ref)):
            xbuf[0:8, :] = jnp.where(i > 0, xp_ref[...], 0.0)
            xbuf[8:, :] = x_ref[...]
            dbuf[0:ts, :] = d
            dbuf[ts:, :] = jnp.where(behind, d_n, 0.0)
            wt = w_ref[...]
            ext_x, ext_d = xbuf[...], dbuf[...]
            acc = _rows_up(ext_d, k - 1)[0:ts, :] * wt[0:1, :]
            for j in range(1, k):
                acc = acc + _rows_up(ext_d, k - 1 - j)[0:ts, :] * wt[j:j + 1, :]
            dx_ref[...] = acc.astype(BF16)
            for j in range(k):
                dw_ref[j:j + 1, :] += jnp.sum(d * _rows_down(ext_x, k - 1 - j)[8:, :], axis=0, keepdims=True)

    tile = pl.BlockSpec((ts, tw), lambda j, i: (i, j))
    prev = pl.BlockSpec((8, tw), lambda j, i: (jnp.maximum(i * hb - 1, 0), j))
    nxt = pl.BlockSpec((8, tw), lambda j, i: (jnp.minimum((i + 1) * hb, t // 8 - 1), j))
    taps = pl.BlockSpec((k, tw), lambda j, i: (0, j))
    dw_spec = pl.BlockSpec((8, tw), lambda j, i: (0, j))
    return pl.pallas_call(
        body, grid=(width // tw, nt),
        in_specs=[tile, prev, tile, prev, tile, nxt, tile, nxt, tile, nxt, taps, taps],
        out_specs=(tile, tile, dw_spec, dw_spec),
        out_shape=(jax.ShapeDtypeStruct((t, width), BF16), jax.ShapeDtypeStruct((t, width), BF16),
                   jax.ShapeDtypeStruct((8, width), F32), jax.ShapeDtypeStruct((8, width), F32)),
        scratch_shapes=[pltpu.VMEM((ts + 8, tw), F32)] * 4,
        compiler_params=_params("parallel", "arbitrary"), name=name,
    )(hg, hg, hv, hv, cg, cg, cv, cv, dact, dact, wg, wv)


def _gdn_chunk(c, z, ba, pa, pdt, hn, s, *, valid, inverse=None, with_inverse=False):
    r = lax.broadcasted_iota(jnp.int32, (CHUNK, CHUNK), 0)
    q_ = lax.broadcasted_iota(jnp.int32, (CHUNK, CHUNK), 1)
    causal = r >= q_
    strict = r > q_
    tril = causal.astype(F32)
    triu = (r <= q_).astype(F32)
    lane = lax.broadcasted_iota(jnp.int32, (CHUNK, 128), 1)

    decay_log = -jnp.exp(pa) * _softplus(ba + pdt)
    bg = jnp.where(lane < HEADS, jax.nn.sigmoid(ba), jnp.where(lane < 2 * HEADS, decay_log, 0.0))
    bg = jnp.where(valid, bg, 0.0)
    gc = _hnn(tril, bg)
    gct = _hnn(bg.T, triu)
    eg = jnp.exp(gc)
    glast = gc[CHUNK - 1:CHUNK, :]
    ekd = jnp.exp(glast - gc)
    gtot = jnp.exp(glast)

    hd = range(HEADS)
    hs = [slice(h * HEAD_DIM, (h + 1) * HEAD_DIM) for h in hd]
    gl = [slice(HEADS + h, HEADS + h + 1) for h in hd]
    q = [_silu(c[:, hs[h]]) for h in hd]
    k = [_silu(c[:, D_MODEL + h * HEAD_DIM:D_MODEL + (h + 1) * HEAD_DIM]) for h in hd]
    v = [_silu(c[:, 2 * D_MODEL + h * HEAD_DIM:2 * D_MODEL + (h + 1) * HEAD_DIM]) for h in hd]
    q = [q[h] * lax.rsqrt(jnp.sum(q[h] * q[h], axis=-1, keepdims=True) + NORM_EPS) * (HEAD_DIM ** -0.5) for h in hd]
    k = [k[h] * lax.rsqrt(jnp.sum(k[h] * k[h], axis=-1, keepdims=True) + NORM_EPS) for h in hd]
    beta = [bg[:, h:h + 1] for h in hd]
    decay = [jnp.exp(jnp.where(causal, gc[:, gl[h]] - gct[gl[h], :], -1e30)) for h in hd]
    kb = [k[h] * beta[h] for h in hd]
    a = [jnp.where(strict, _nt(kb[h], k[h]) * decay[h], 0.0) for h in hd]
    qk = [jnp.where(causal, _nt(q[h], k[h]) * decay[h], 0.0) for h in hd]
    x = _inv_unit_lower(tuple(a)) if inverse is None else _kept_inverse(tuple(a), tuple(inverse))
    u = [_nn(x[h], v[h] * beta[h]) for h in hd]
    w = [_nn(x[h], kb[h] * eg[:, gl[h]]) for h in hd]
    v_new = [u[h] - _nn(w[h], s[h]) for h in hd]
    o = [_nn(q[h] * eg[:, gl[h]], s[h]) + _nn(qk[h], v_new[h]) for h in hd]
    states = [s[h] * gtot[:, gl[h]] + _tn(k[h] * ekd[:, gl[h]], v_new[h]) for h in hd]
    o = [o[h] * lax.rsqrt(jnp.mean(o[h] * o[h], axis=-1, keepdims=True) + NORM_EPS) * hn * _silu(z[:, hs[h]])
         for h in hd]
    if with_inverse:
        return jnp.concatenate(o, axis=1), tuple(states), x
    return jnp.concatenate(o, axis=1), tuple(states)


GDN_FWD_CHUNKS = 5
GDN_BWD_CHUNKS = 2


def _chunk_valid(n, t):
    row = n * CHUNK + lax.broadcasted_iota(jnp.int32, (CHUNK, 1), 0)
    return jnp.logical_and(row >= LEAD, row < t - TAIL)


def gdn_fwd(c, z, ba, pa, pdt, hn, name, ba_block=0):
    t = c.shape[0]
    n_chunks = t // CHUNK
    per_step = GDN_FWD_CHUNKS if n_chunks % GDN_FWD_CHUNKS == 0 else 1
    rows_per_step = per_step * CHUNK

    def body(c_ref, z_ref, ba_ref, pa_ref, pdt_ref, hn_ref, y_ref, ss_ref, inv_ref, state):
        step = pl.program_id(0)

        @pl.when(step == 0)
        def _():
            state[...] = jnp.zeros_like(state)

        s = tuple(state[h] for h in range(HEADS))
        for j in range(per_step):
            rows = pl.ds(j * CHUNK, CHUNK)
            for h in range(HEADS):
                ss_ref[j, h] = s[h]
            y, s, inv = _gdn_chunk(c_ref[rows, :], z_ref[rows, :], ba_ref[rows, :], pa_ref[...], pdt_ref[...],
                                   hn_ref[...], s, valid=_chunk_valid(step * per_step + j, t), with_inverse=True)
            y_ref[rows, :] = y
            for h in range(HEADS):
                inv_ref[j, h] = inv[h]
        for h in range(HEADS):
            state[h] = s[h]

    vec = pl.BlockSpec((1, 128), lambda n: (0, 0))
    return pl.pallas_call(
        body, grid=(n_chunks // per_step,),
        in_specs=[pl.BlockSpec((rows_per_step, QKV_DIM), lambda n: (n, 0)),
                  pl.BlockSpec((rows_per_step, D_MODEL), lambda n: (n, 0)),
                  pl.BlockSpec((rows_per_step, 128), lambda n: (n, ba_block)), vec, vec, vec],
        out_specs=(pl.BlockSpec((rows_per_step, D_MODEL), lambda n: (n, 0)),
                   pl.BlockSpec((per_step, HEADS, HEAD_DIM, HEAD_DIM), lambda n: (n, 0, 0, 0)),
                   pl.BlockSpec((per_step, HEADS, CHUNK, CHUNK), lambda n: (n, 0, 0, 0))),
        out_shape=(jax.ShapeDtypeStruct((t, D_MODEL), F32),
                   jax.ShapeDtypeStruct((n_chunks, HEADS, HEAD_DIM, HEAD_DIM), F32),
                   jax.ShapeDtypeStruct((n_chunks, HEADS, CHUNK, CHUNK), F32)),
        scratch_shapes=[pltpu.VMEM((HEADS, HEAD_DIM, HEAD_DIM), F32)],
        compiler_params=_params("arbitrary"), name=name,
    )(c, z, ba, pa, pdt, hn)


def gdn_bwd(c, z, ba, pa, pdt, hn, starts, inverses, dy, name, ba_block=0):
    t = c.shape[0]
    per_step = GDN_BWD_CHUNKS if (t // CHUNK) % GDN_BWD_CHUNKS == 0 else 1
    n_steps = t // CHUNK // per_step
    rows_per_step = per_step * CHUNK

    def body(c_ref, z_ref, ba_ref, pa_ref, pdt_ref, hn_ref, ss_ref, inv_ref, dy_ref,
             dc_ref, dz_ref, dba_ref, dpa_ref, dpdt_ref, dhn_ref, dstate):
        step = pl.program_id(0)

        @pl.when(step == 0)
        def _():
            dstate[...] = jnp.zeros_like(dstate)
            dpa_ref[...] = jnp.zeros_like(dpa_ref)
            dpdt_ref[...] = jnp.zeros_like(dpdt_ref)
            dhn_ref[...] = jnp.zeros_like(dhn_ref)

        ds = tuple(dstate[h] for h in range(HEADS))
        for j in reversed(range(per_step)):
            rows = pl.ds(j * CHUNK, CHUNK)
            f = functools.partial(_gdn_chunk, valid=_chunk_valid((n_steps - 1 - step) * per_step + j, t),
                                  inverse=tuple(inv_ref[j, h] for h in range(HEADS)))
            _, vjp = jax.vjp(f, c_ref[rows, :], z_ref[rows, :], ba_ref[rows, :], pa_ref[...], pdt_ref[...], hn_ref[...],
                             tuple(ss_ref[j, h] for h in range(HEADS)))
            dc, dz, dba, dpa, dpdt, dhn, ds = vjp((dy_ref[rows, :], ds))
            dc_ref[rows, :] = dc
            dz_ref[rows, :] = dz.astype(BF16)
            dba_ref[rows, :] = dba.astype(BF16)
            dpa_ref[...] += dpa
            dpdt_ref[...] += dpdt
            dhn_ref[...] += dhn
        for h in range(HEADS):
            dstate[h] = ds[h]

    def rev(width, block=0):
        return pl.BlockSpec((rows_per_step, width), lambda s: (n_steps - 1 - s, block))

    vec = pl.BlockSpec((1, 128), lambda s: (0, 0))
    vec_shape = jax.ShapeDtypeStruct((1, 128), F32)
    return pl.pallas_call(
        body, grid=(n_steps,),
        in_specs=[rev(QKV_DIM), rev(D_MODEL), rev(128, ba_block), vec, vec, vec,
                  pl.BlockSpec((per_step, HEADS, HEAD_DIM, HEAD_DIM), lambda s: (n_steps - 1 - s, 0, 0, 0)),
                  pl.BlockSpec((per_step, HEADS, CHUNK, CHUNK), lambda s: (n_steps - 1 - s, 0, 0, 0)),
                  rev(D_MODEL)],
        out_specs=(rev(QKV_DIM), rev(D_MODEL), rev(128), vec, vec, vec),
        out_shape=(jax.ShapeDtypeStruct((t, QKV_DIM), F32), jax.ShapeDtypeStruct((t, D_MODEL), BF16),
                   jax.ShapeDtypeStruct((t, 128), BF16), vec_shape, vec_shape, vec_shape),
        scratch_shapes=[pltpu.VMEM((HEADS, HEAD_DIM, HEAD_DIM), F32)],
        compiler_params=_params("arbitrary"), name=name,
    )(c, z, ba, pa, pdt, hn, starts, inverses, dy)


def _layer_fwd(h, u, ut, w, next_gain, tag):
    pq = mm(u, w["wqkv"], name=f"{tag}_mm_qkv")
    pz = mm(u, w["wz"], name=f"{tag}_mm_z")
    pg = mm(u, w["wg"], name=f"{tag}_mm_gate")
    pba = mm(u, w["wpb"], name=f"{tag}_mm_pool_ba")
    cq = conv_fwd(pq, w["conv_qkv"], f"{tag}_conv_qkv")
    ya, starts, inverses = gdn_fwd(cq, pz, pba, w["pa"], w["pdt"], w["head_norm"], f"{tag}_gdn", ba_block=BA_BLOCK)
    pooled = poolwin_fwd(pba, f"{tag}_poolwin")
    y, yt = mix_fwd(ya, pg, pooled, w["w_pool"], w["pool_scale"], f"{tag}_mix")
    h1, u2, u2t = mm(y, w["wout"], add=h, norm_gain=w["norm_ffn"], name=f"{tag}_mm_out")
    hg = mm(u2, w["wupg"], name=f"{tag}_mm_upg")
    hv = mm(u2, w["wupv"], name=f"{tag}_mm_upv")
    cg, cv, act, actt = ffn_mid_fwd(hg, hv, w["conv_g"], w["conv_v"], f"{tag}_ffn_mid")
    if next_gain is None:
        h2, nxt = mm(act, w["wdown"], add=h1, name=f"{tag}_mm_down"), None
    else:
        h2, *nxt = mm(act, w["wdown"], add=h1, norm_gain=next_gain, name=f"{tag}_mm_down")
    saved = dict(h=h, ut=ut, pq=pq, pz=pz, pg=pg, pba=pba, cq=cq, ya=ya, starts=starts, inverses=inverses,
                 pooled=pooled, yt=yt, h1=h1, u2t=u2t, hg=hg, hv=hv, cg=cg, cv=cv, actt=actt)
    return h2, nxt, saved


def _layer_bwd(dh2, dh2b, w, s, tag):
    g = {}
    dact = mm(dh2b, w["wdown"], tb=True, name=f"{tag}_bmm_down_x")
    g["wdown"] = mm(s["actt"], dh2b, name=f"{tag}_bmm_down_w")
    dhg, dhv, g["conv_g"], g["conv_v"] = ffn_mid_bwd(s["hg"], s["hv"], s["cg"], s["cv"], dact, w["conv_g"], w["conv_v"],
                                                     f"{tag}_ffn_mid_b")
    du2 = mm(dhg, w["wupg"], tb=True, name=f"{tag}_bmm_upg_x")
    du2 = mm(dhv, w["wupv"], tb=True, add=du2, name=f"{tag}_bmm_upv_x")
    g["wupg"] = mm(s["u2t"], dhg, name=f"{tag}_bmm_upg_w")
    g["wupv"] = mm(s["u2t"], dhv, name=f"{tag}_bmm_upv_w")
    dh1, dh1b, g["norm_ffn"] = rms_bwd(s["h1"], w["norm_ffn"], du2, dh2, f"{tag}_rms_ffn_b")
    dy = mm(dh1b, w["wout"], tb=True, name=f"{tag}_bmm_out_x")
    g["wout"] = mm(s["yt"], dh1b, name=f"{tag}_bmm_out_w")
    dya, dpg, dpooled, g["w_pool"], g["pool_scale"] = mix_bwd(
        s["ya"], s["pg"], s["pooled"], w["w_pool"], w["pool_scale"], dy, f"{tag}_mix_b")
    dpp = poolwin_bwd(dpooled, f"{tag}_poolwin_b")
    dcq, dpz, dpba, g["pa"], g["pdt"], g["head_norm"] = gdn_bwd(
        s["cq"], s["pz"], s["pba"], w["pa"], w["pdt"], w["head_norm"], s["starts"], s["inverses"], dya, f"{tag}_gdn_b",
        ba_block=BA_BLOCK)
    dpb = jnp.concatenate([dpp, dpba], axis=1)
    dpq, g["conv_qkv"] = conv_bwd(s["pq"], dcq, w["conv_qkv"], f"{tag}_conv_qkv_b")
    du = mm(dpq, w["wqkv"], tb=True, name=f"{tag}_bmm_qkv_x")
    du = mm(dpz, w["wz"], tb=True, add=du, name=f"{tag}_bmm_z_x")
    du = mm(dpg, w["wg"], tb=True, add=du, name=f"{tag}_bmm_gate_x")
    du = mm(dpb, w["wpb"], tb=True, add=du, name=f"{tag}_bmm_pool_ba_x")
    g["wqkv"] = mm(s["ut"], dpq, name=f"{tag}_bmm_qkv_w")
    g["wz"] = mm(s["ut"], dpz, name=f"{tag}_bmm_z_w")
    g["wg"] = mm(s["ut"], dpg, name=f"{tag}_bmm_gate_w")
    dwpb = mm(s["ut"], dpb, name=f"{tag}_bmm_pool_ba_w")
    g["wpl"], g["wba"] = dwpb[:, :POOL_WIDTH], dwpb[:, POOL_WIDTH:]
    dh, dhb, g["norm_mix"] = rms_bwd(s["h"], w["norm_mix"], du, dh1, f"{tag}_rms_mix_b")
    return dh, dhb, g


def local_step(h0, target, layers, norm_final):
    h = h0
    normed = rms_fwd(h0, layers[0]["norm_mix"], "l0_rms_mix")
    saved = []
    for li, w in enumerate(layers):
        next_gain = layers[li + 1]["norm_mix"] if li + 1 < len(layers) else None
        h, normed, s = _layer_fwd(h, *normed, w, next_gain, f"l{li}")
        saved.append(s)
    loss, dh, dhb, dnf = loss_head(h, norm_final, target, "loss_head")
    grads = [None] * len(layers)
    for li in reversed(range(len(layers))):
        dh, dhb, grads[li] = _layer_bwd(dh, dhb, layers[li], saved[li], f"l{li}")
    return loss, dh, grads, dnf


_Z0, _B0, _P0, _G0, _IN_DIM = 3072, 4096, 4112, 4624, 6672


def _lanes_8_to_15(v):
    return jnp.pad(v.reshape(1, HEADS).astype(F32), ((0, 0), (HEADS, 128 - 2 * HEADS)))


IN_PIECES = (("wqkv", 0, _Z0), ("wz", _Z0, _B0), ("wba", _B0, _P0), ("wpl", _P0, _G0), ("wg", _G0, _IN_DIM))
IN_SHARD = _IN_DIM // 4


def _overlaps(a, b, spans):
    return [(name, max(a, lo) - lo, min(b, hi) - max(a, lo)) for name, lo, hi in spans if max(a, lo) < min(b, hi)]


def _cat(parts):
    return parts[0] if len(parts) == 1 else jnp.concatenate(parts, axis=1)


def prep_layer(p):
    row = lambda v: v.reshape(1, -1).astype(F32)
    w_in, w_up = p["w_in"], p["w_up"]
    if not isinstance(w_in, (list, tuple)):
        w_in = [w_in[:, s * IN_SHARD:(s + 1) * IN_SHARD] for s in range(4)]
        w_up = [w_up[:, s * (D_FF // 2):(s + 1) * (D_FF // 2)] for s in range(4)]
    shards = [(s, s * IN_SHARD, (s + 1) * IN_SHARD) for s in range(4)]
    piece = {name: _cat([w_in[s][:, off:off + width].astype(BF16) for s, off, width in _overlaps(lo, hi, shards)])
             for name, lo, hi in IN_PIECES}
    return dict(
        wqkv=piece["wqkv"], wz=piece["wz"],
        wpb=jnp.concatenate([piece["wpl"], jnp.pad(piece["wba"], ((0, 0), (0, 128 - 2 * HEADS)))], axis=1),
        wg=piece["wg"], wout=p["w_out"].astype(BF16),
        wupg=_cat([w_up[0].astype(BF16), w_up[1].astype(BF16)]), wupv=_cat([w_up[2].astype(BF16), w_up[3].astype(BF16)]),
        wdown=p["w_down"].astype(BF16),
        conv_qkv=p["conv_qkv"].astype(F32), conv_g=p["conv_ffn"][:, :D_FF].astype(F32),
        conv_v=p["conv_ffn"][:, D_FF:].astype(F32), w_pool=p["w_pool"].astype(F32),
        pool_scale=row(p["pool_scale"]), head_norm=row(p["head_norm"]), norm_mix=row(p["norm_mix"]),
        norm_ffn=row(p["norm_ffn"]), pa=_lanes_8_to_15(p["a_log"]), pdt=_lanes_8_to_15(p["dt_bias"]))


def layer_grads(g):
    return dict(
        w_in=jnp.concatenate([g["wqkv"], g["wz"], g["wba"][:, :2 * HEADS], g["wpl"], g["wg"]], axis=1),
        conv_qkv=g["conv_qkv"][:4], a_log=g["pa"][0, HEADS:2 * HEADS], dt_bias=g["pdt"][0, HEADS:2 * HEADS],
        head_norm=g["head_norm"][0], w_pool=g["w_pool"], pool_scale=g["pool_scale"][0], w_out=g["wout"],
        norm_mix=g["norm_mix"][0], norm_ffn=g["norm_ffn"][0],
        w_up=jnp.concatenate([g["wupg"], g["wupv"]], axis=1),
        conv_ffn=jnp.concatenate([g["conv_g"][:3], g["conv_v"][:3]], axis=1), w_down=g["wdown"])


def big_grad_shards(g):
    in_shards = [_cat([g[name][:, off:off + width] for name, off, width in
                       _overlaps(s * IN_SHARD, (s + 1) * IN_SHARD, IN_PIECES)]) for s in range(4)]
    half = D_FF // 2
    up_shards = [g["wupg"][:, :half], g["wupg"][:, half:], g["wupv"][:, :half], g["wupv"][:, half:]]
    return dict(w_in=jnp.stack(in_shards), w_up=jnp.stack(up_shards),
                w_down=g["wdown"].reshape(4, D_FF // 4, D_MODEL), w_out=g["wout"].reshape(4, D_MODEL // 4, D_MODEL))


LAYER_PARAMS = ("norm_mix", "w_in", "conv_qkv", "a_log", "dt_bias", "head_norm", "w_pool", "pool_scale", "w_out",
                "norm_ffn", "w_up", "conv_ffn", "w_down")


def pad_rows(meta, x):
    return jnp.concatenate([jnp.zeros((LEAD, D_MODEL), F32), meta.astype(F32), x.astype(F32),
                            jnp.zeros((TAIL, D_MODEL), F32)], axis=0)


MESH = pl.DeviceIdType.MESH
ANY = pl.BlockSpec(memory_space=pl.ANY)


def _place():
    x, y, c = lax.axis_index("x"), lax.axis_index("y"), lax.axis_index("c")
    return x, y, c, [(1 - x, y), (x, 1 - y), (1 - x, 1 - y)]


def _my_chip():
    return 2 * lax.axis_index("x") + lax.axis_index("y")


def gather_shards(packs):
    n = len(packs)

    def body(*refs):
        p_refs, o_refs, (send_sems, recv_sems) = refs[:n], refs[n:2 * n], refs[2 * n:]
        x, y, c, chips = _place()

        def copy(a, k, chip, half, to, src=None):
            dst = o_refs[a].at[2 * chip[0] + chip[1], half]
            return pltpu.make_async_remote_copy(src_ref=dst if src is None else src, dst_ref=dst,
                                                send_sem=send_sems.at[6 * a + k], recv_sem=recv_sems.at[6 * a + k],
                                                device_id=to, device_id_type=MESH)

        first = [copy(a, j, (x, y), c, (*chip, c), src=p_refs[a].at[c]) for a in range(n) for j, chip in enumerate(chips)]
        for cp in first:
            cp.start()
        passed = []
        for a in range(n):
            for j, chip in enumerate(chips):
                copy(a, j, chip, c, (x, y, c)).wait_recv()
                passed.append(copy(a, 3 + j, chip, c, (x, y, 1 - c)))
                passed[-1].start()
        for a in range(n):
            for j, chip in enumerate(chips):
                copy(a, 3 + j, chip, 1 - c, (x, y, c)).wait_recv()
        for cp in first + passed:
            cp.wait_send()

    gathered = pl.pallas_call(
        body, in_specs=[ANY] * n, out_specs=[ANY] * n,
        out_shape=[jax.ShapeDtypeStruct((4,) + p.shape, p.dtype) for p in packs],
        scratch_shapes=[pltpu.SemaphoreType.DMA((6 * n,)), pltpu.SemaphoreType.DMA((6 * n,))],
        name="gather_shards",
    )(*packs)
    me = _my_chip()
    return [lax.dynamic_update_slice(g, p[None], (me,) + (0,) * p.ndim) for g, p in zip(gathered, packs)]


def swap_other_halves(ps):
    n = len(ps)

    def body(*refs):
        p_refs, o_refs, (send_sems, recv_sems) = refs[:n], refs[n:2 * n], refs[2 * n:]
        x, y, c, _ = _place()
        copies = [pltpu.make_async_remote_copy(src_ref=p_refs[a].at[s, 1 - c], dst_ref=o_refs[a].at[s],
                                               send_sem=send_sems.at[4 * a + s], recv_sem=recv_sems.at[4 * a + s],
                                               device_id=(x, y, 1 - c), device_id_type=MESH)
                  for a in range(n) for s in range(4)]
        for cp in copies:
            cp.start()
        for cp in copies:
            cp.wait()

    return pl.pallas_call(
        body, in_specs=[ANY] * n, out_specs=[ANY] * n,
        out_shape=[jax.ShapeDtypeStruct((4,) + p.shape[2:], p.dtype) for p in ps],
        scratch_shapes=[pltpu.SemaphoreType.DMA((4 * n,)), pltpu.SemaphoreType.DMA((4 * n,))],
        name="swap_other_halves",
    )(*ps)


def scatter_to_chips(qs):
    n = len(qs)

    def body(*refs):
        q_refs, o_refs, (send_sems, recv_sems) = refs[:n], refs[n:2 * n], refs[2 * n:]
        x, y, c, chips = _place()
        me = 2 * x + y
        copies = [pltpu.make_async_remote_copy(src_ref=q_refs[a].at[2 * chip[0] + chip[1]], dst_ref=o_refs[a].at[me],
                                               send_sem=send_sems.at[3 * a + j], recv_sem=recv_sems.at[3 * a + j],
                                               device_id=(*chip, c), device_id_type=MESH)
                  for a in range(n) for j, chip in enumerate(chips)]
        for cp in copies:
            cp.start()
        for a in range(n):
            for j, chip in enumerate(chips):
                slot = o_refs[a].at[2 * chip[0] + chip[1]]
                pltpu.make_async_remote_copy(src_ref=slot, dst_ref=slot, send_sem=send_sems.at[3 * a + j],
                                             recv_sem=recv_sems.at[3 * a + j],
                                             device_id=(x, y, c), device_id_type=MESH).wait_recv()
        for cp in copies:
            cp.wait_send()

    received = pl.pallas_call(
        body, in_specs=[ANY] * n, out_specs=[ANY] * n,
        out_shape=[jax.ShapeDtypeStruct(q.shape, q.dtype) for q in qs],
        scratch_shapes=[pltpu.SemaphoreType.DMA((3 * n,)), pltpu.SemaphoreType.DMA((3 * n,))],
        name="scatter_to_chips",
    )(*qs)
    me = _my_chip()
    return [lax.dynamic_update_slice(r, lax.dynamic_slice_in_dim(q, me, 1, axis=0), (me, 0, 0))
            for r, q in zip(received, qs)]


def join_halves(boths):
    n = len(boths)

    def body(*refs):
        o_refs, (send_sems, recv_sems) = refs[n:2 * n], refs[2 * n:]
        x, y, c, _ = _place()
        copies = [pltpu.make_async_remote_copy(src_ref=o_refs[a].at[c], dst_ref=o_refs[a].at[c],
                                               send_sem=send_sems.at[a], recv_sem=recv_sems.at[a],
                                               device_id=(x, y, 1 - c), device_id_type=MESH) for a in range(n)]
        for cp in copies:
            cp.start()
        for a in range(n):
            other = o_refs[a].at[1 - c]
            pltpu.make_async_remote_copy(src_ref=other, dst_ref=other, send_sem=send_sems.at[a],
                                         recv_sem=recv_sems.at[a], device_id=(x, y, c), device_id_type=MESH).wait_recv()
        for cp in copies:
            cp.wait_send()

    return pl.pallas_call(
        body, in_specs=[ANY] * n, out_specs=[ANY] * n,
        out_shape=[jax.ShapeDtypeStruct(b.shape, b.dtype) for b in boths],
        input_output_aliases={a: a for a in range(n)},
        scratch_shapes=[pltpu.SemaphoreType.DMA((n,)), pltpu.SemaphoreType.DMA((n,))], name="join_halves",
    )(*boths)


def add_own_half(p, other, c, out_dtype, name):
    _, _, rows, lanes = p.shape
    tr = _tile(rows, max(16, 524288 // lanes), 16)

    def body(c_ref, p_ref, o_ref, out_ref):
        out_ref[...] = (p_ref[...] + o_ref[...]).astype(out_dtype)

    return pl.pallas_call(
        body,
        grid_spec=pltpu.PrefetchScalarGridSpec(
            num_scalar_prefetch=1, grid=(4, rows // tr),
            in_specs=[pl.BlockSpec((None, None, tr, lanes), lambda s, i, c_ref: (s, c_ref[0], i, 0)),
                      pl.BlockSpec((None, tr, lanes), lambda s, i, c_ref: (s, i, 0))],
            out_specs=pl.BlockSpec((None, tr, lanes), lambda s, i, c_ref: (s, i, 0))),
        out_shape=jax.ShapeDtypeStruct((4, rows, lanes), out_dtype),
        compiler_params=_params("parallel", "parallel"), name=name,
    )(c, p, other)


def sum_chips(b, c, name):
    _, rows, lanes = b.shape
    tr = _tile(rows, max(16, 524288 // lanes), 16)

    def body(c_ref, b_ref, out_ref):
        b0, b1, b2, b3 = (b_ref[k].astype(F32) for k in range(4))
        out_ref[...] = ((b0 + b1) + b2) + b3

    return pl.pallas_call(
        body,
        grid_spec=pltpu.PrefetchScalarGridSpec(
            num_scalar_prefetch=1, grid=(rows // tr,),
            in_specs=[pl.BlockSpec((4, tr, lanes), lambda i, c_ref: (0, i, 0))],
            out_specs=pl.BlockSpec((None, tr, lanes), lambda i, c_ref: (c_ref[0], i, 0))),
        out_shape=jax.ShapeDtypeStruct((2, rows, lanes), F32),
        compiler_params=_params("parallel"), name=name,
    )(c, b)


def all_reduce_to_shards(packs, wires, tags, c):
    others = swap_other_halves(packs)
    qs = [add_own_half(p, o, c, wire, f"add_own_half_{tag}") for p, o, wire, tag in zip(packs, others, wires, tags)]
    return join_halves([sum_chips(r, c, f"sum_chips_{tag}") for r, tag in zip(scatter_to_chips(qs), tags)])


def adamw(w, g, m, v, name):
    shape = w.shape
    cols = shape[-1]
    w2, g2, m2, v2 = (a.reshape(-1, cols) for a in (w, g, m, v))
    rows = w2.shape[0]
    tr = _tile(rows, max(8, 262144 // cols), 8) if rows % 8 == 0 else rows
    c1 = 1.0 - ADAM_B1 ** ADAM_STEP
    c2 = 1.0 - ADAM_B2 ** ADAM_STEP

    def body(w_ref, g_ref, m_ref, v_ref, d_ref, mo_ref, vo_ref):
        gv = g_ref[...]
        mn = ADAM_B1 * m_ref[...] + (1.0 - ADAM_B1) * gv
        vn = ADAM_B2 * v_ref[...] + (1.0 - ADAM_B2) * jnp.square(gv)
        d_ref[...] = -ADAM_LR * ((mn / c1) / (jnp.sqrt(vn / c2) + ADAM_EPS) + ADAM_WD * w_ref[...])
        mo_ref[...] = mn
        vo_ref[...] = vn

    spec = pl.BlockSpec((tr, cols), lambda i: (i, 0))
    out = jax.ShapeDtypeStruct((rows, cols), F32)
    d, mn, vn = pl.pallas_call(
        body, grid=(rows // tr,), in_specs=[spec] * 4, out_specs=(spec,) * 3, out_shape=(out,) * 3,
        compiler_params=_params("parallel"), name=name,
    )(w2, g2, m2, v2)
    return d.reshape(shape), mn.reshape(shape), vn.reshape(shape)


BIG = ("w_in", "w_up", "w_down", "w_out")
SMALL = ("w_pool", "conv_qkv", "conv_ffn", "meta_tokens")
SHARDED = BIG + SMALL
MATMUL_WEIGHTS = BIG + ("w_pool",)
REPLICATED = ("norm_mix", "a_log", "dt_bias", "head_norm", "pool_scale", "norm_ffn", "norm_final")
SHARD_AXIS = {"w_in": 2, "w_up": 2, "w_out": 1, "w_down": 1, "w_pool": 3, "conv_qkv": 2, "conv_ffn": 2, "meta_tokens": 1}


def _rows_of(a):
    return a.reshape(-1, 128)


SEGMENT_ROWS = 16


def _segment(n_rows):
    return -(-n_rows // SEGMENT_ROWS) * SEGMENT_ROWS


def _pad_segment(a):
    pad = [(0, 0)] * a.ndim
    pad[-2] = (0, _segment(a.shape[-2]) - a.shape[-2])
    return jnp.pad(a, pad)


def _unshard(stacked, axis):
    full = jnp.moveaxis(stacked, 0, axis)
    shape = list(full.shape)
    shape[axis:axis + 2] = [shape[axis] * shape[axis + 1]]
    return full.reshape(shape)


def _shard_stack(full, axis):
    shape = list(full.shape)
    shape[axis:axis + 1] = [4, shape[axis] // 4]
    return jnp.moveaxis(full.reshape(shape), axis, 0)


def pack_weights(shards):
    parts = [_rows_of(shards[k].astype(WIRE)) if k in MATMUL_WEIGHTS else
             lax.bitcast_convert_type(_rows_of(shards[k].astype(F32)), WIRE).reshape(-1, 128) for k in SMALL]
    parts = [_pad_segment(p) for p in parts]
    rows = sum(p.shape[0] for p in parts)
    if rows % (2 * SEGMENT_ROWS):
        parts.append(jnp.zeros((SEGMENT_ROWS, 128), WIRE))
        rows += SEGMENT_ROWS
    return [shards[k].astype(WIRE) for k in BIG] + [jnp.concatenate(parts, axis=0).reshape(2, rows // 2, 128)]


def unpack_weights(gathered, shard_shapes):
    out = {k: _unshard(g, SHARD_AXIS[k]) for k, g in zip(BIG, gathered)}
    flat = gathered[-1].reshape(4, -1, 128)
    at = 0
    for k in SMALL:
        shp = shard_shapes[k]
        n = 1
        for e in shp:
            n *= e
        if k in MATMUL_WEIGHTS:
            r = n // 128
            stacked = flat[:, at:at + r].reshape((4,) + tuple(shp))
        else:
            r = 2 * n // 128
            stacked = lax.bitcast_convert_type(flat[:, at:at + r].reshape(4, n // 128, 128, 2), F32).reshape((4,) + tuple(shp))
        out[k] = _unshard(stacked, SHARD_AXIS[k])
        at += _segment(r)
    return out


def pack_grads(big, full, repl):
    r = jnp.concatenate([repl[k].reshape(-1) for k in REPLICATED])
    r = jnp.pad(r, (0, -r.shape[0] % 128)).reshape(1, -1, 128)
    parts = [_shard_stack(full[k], SHARD_AXIS[k]).reshape(4, -1, 128) for k in SMALL]
    parts = [_pad_segment(p) for p in parts + [jnp.broadcast_to(r, (4,) + r.shape[1:])]]
    rows = sum(p.shape[1] for p in parts)
    if rows % (2 * SEGMENT_ROWS):
        parts.append(jnp.zeros((4, SEGMENT_ROWS, 128), F32))
        rows += SEGMENT_ROWS
    side = jnp.concatenate(parts, axis=1).reshape(4, 2, rows // 2, 128)
    return list(big) + [side]


def unpack_grads(reduced, shard_shapes, repl_shapes):
    out = dict(zip(BIG, reduced))
    side = reduced[-1].reshape(-1, 128)
    at = 0
    for k in SMALL:
        n = 1
        for e in shard_shapes[k]:
            n *= e
        out[k] = side[at:at + n // 128].reshape(shard_shapes[k])
        at += _segment(n // 128)
    r = side[at:].reshape(-1)
    at = 0
    for k in REPLICATED:
        n = 1
        for e in repl_shapes[k]:
            n *= e
        out[k] = r[at:at + n].reshape(repl_shapes[k])
        at += n
    return out


WEIGHT_ORDER = ("meta_tokens", "norm_mix", "w_in", "conv_qkv", "a_log", "dt_bias", "head_norm", "w_pool", "pool_scale",
                "w_out", "norm_ffn", "w_up", "conv_ffn", "w_down", "norm_final")


def kernel(x, meta_tokens, norm_mix, w_in, conv_qkv, a_log, dt_bias, head_norm, w_pool, pool_scale, w_out, norm_ffn, w_up, conv_ffn, w_down, norm_final, loss_target, m_meta_tokens, m_norm_mix, m_w_in, m_conv_qkv, m_a_log, m_dt_bias, m_head_norm, m_w_pool, m_pool_scale, m_w_out, m_norm_ffn, m_w_up, m_conv_ffn, m_w_down, m_norm_final, v_meta_tokens, v_norm_mix, v_w_in, v_conv_qkv, v_a_log, v_dt_bias, v_head_norm, v_w_pool, v_pool_scale, v_w_out, v_norm_ffn, v_w_up, v_conv_ffn, v_w_down, v_norm_final):
    weights = dict(meta_tokens=meta_tokens, norm_mix=norm_mix, w_in=w_in, conv_qkv=conv_qkv, a_log=a_log,
                   dt_bias=dt_bias, head_norm=head_norm, w_pool=w_pool, pool_scale=pool_scale, w_out=w_out,
                   norm_ffn=norm_ffn, w_up=w_up, conv_ffn=conv_ffn, w_down=w_down, norm_final=norm_final)
    m_in = dict(zip(WEIGHT_ORDER, (m_meta_tokens, m_norm_mix, m_w_in, m_conv_qkv, m_a_log, m_dt_bias, m_head_norm,
                                   m_w_pool, m_pool_scale, m_w_out, m_norm_ffn, m_w_up, m_conv_ffn, m_w_down, m_norm_final)))
    v_in = dict(zip(WEIGHT_ORDER, (v_meta_tokens, v_norm_mix, v_w_in, v_conv_qkv, v_a_log, v_dt_bias, v_head_norm,
                                   v_w_pool, v_pool_scale, v_w_out, v_norm_ffn, v_w_up, v_conv_ffn, v_w_down, v_norm_final)))
    shard_shapes = {k: weights[k].shape for k in SHARDED}
    repl_shapes = {k: weights[k].shape for k in REPLICATED}
    core = lax.axis_index("c").astype(jnp.int32).reshape(1)

    gathered = gather_shards(pack_weights({k: weights[k] for k in SHARDED}))
    full = unpack_weights(gathered, shard_shapes)
    shards = dict(zip(BIG, gathered))
    layers = []
    for li in range(DEPTH):
        p = {k: (full[k][li] if k in SMALL else weights[k][li]) for k in LAYER_PARAMS if k not in BIG}
        p.update(w_in=[shards["w_in"][s, li] for s in range(4)], w_up=[shards["w_up"][s, li] for s in range(4)],
                 w_down=shards["w_down"][:, li].reshape(D_FF, D_MODEL),
                 w_out=shards["w_out"][:, li].reshape(D_MODEL, D_MODEL))
        layers.append(prep_layer(p))

    h0 = pad_rows(full["meta_tokens"], x[0])
    target = pad_rows(jnp.zeros((N_META, D_MODEL), F32), loss_target[0])
    loss, dh0, grads, d_norm_final = local_step(h0, target, layers, norm_final.reshape(1, D_MODEL))
    seq = x.shape[1]
    grad_x = dh0[LEAD + N_META:LEAD + N_META + seq][None]

    per_layer = [layer_grads(g) for g in grads]
    g_all = {k: jnp.stack([pl_[k] for pl_ in per_layer]) for k in LAYER_PARAMS if k not in BIG}
    g_all["meta_tokens"] = dh0[LEAD:LEAD + N_META]
    g_all["norm_final"] = d_norm_final[0]
    big = [big_grad_shards(g) for g in grads]
    packs = pack_grads([jnp.stack([b[k] for b in big], axis=1) for k in BIG],
                       {k: g_all[k] for k in SMALL}, {k: g_all[k] for k in REPLICATED})
    reduced = all_reduce_to_shards(packs, [GRAD_WIRE] * len(BIG) + [F32], BIG + ("side",), core)
    g_mine = unpack_grads(reduced, shard_shapes, repl_shapes)

    loss_sum = lax.psum(loss[0, 0], ("x", "y", "c"))
    deltas, new_m, new_v = {}, {}, {}
    for k in WEIGHT_ORDER:
        deltas[k], new_m[k], new_v[k] = adamw(weights[k], g_mine[k], m_in[k], v_in[k], f"adamw_{k}")
    return (loss_sum, grad_x, *[g_mine[k] for k in WEIGHT_ORDER], *[deltas[k] for k in WEIGHT_ORDER],
            *[new_m[k] for k in WEIGHT_ORDER], *[new_v[k] for k in WEIGHT_ORDER])
```

```python
import functools

import jax
import jax.numpy as jnp
from jax import lax
from jax.experimental import pallas as pl
from jax.experimental.pallas import tpu as pltpu

F32 = jnp.float32
BF16 = jnp.bfloat16
WIRE = jnp.bfloat16
GRAD_WIRE = jnp.bfloat16

D_MODEL = 1024
HEADS = 8
HEAD_DIM = 128
CHUNK = 64
N_META = 16
LEAD = 48
TAIL = 64
QKV_DIM = 3072
D_FF = 2816
POOL_WIDTH = 512
POOL_WINDOWS = (2, 4, 8, 16)
BA_BLOCK = POOL_WIDTH // 128
DEPTH = 2
NORM_EPS = 1e-6
ADAM_LR, ADAM_B1, ADAM_B2, ADAM_EPS, ADAM_WD, ADAM_STEP = 0.001, 0.9, 0.999, 1e-08, 0.01, 10
VMEM_LIMIT_BYTES = 48 * 1024 * 1024


def _params(*sem):
    return pltpu.CompilerParams(dimension_semantics=sem if sem else None, vmem_limit_bytes=VMEM_LIMIT_BYTES)


def _tile(n, cap, mult):
    best = None
    for t in range(mult, min(n, cap) + 1, mult):
        if n % t == 0:
            best = t
    assert best is not None, (n, cap, mult)
    return best


def _silu(x):
    return x * jax.nn.sigmoid(x)


def _softplus(x):
    return jnp.maximum(x, 0.0) + jnp.log(1.0 + jnp.exp(-jnp.abs(x)))


def _split_bf16(a):
    hi = a.astype(BF16)
    return hi, (a - hi.astype(F32)).astype(BF16)


def _dg(a, b, ca, cb, hi):
    dims = (((ca,), (cb,)), ((), ()))
    if hi is True:
        return lax.dot_general(a, b, dims, precision=lax.Precision.HIGHEST, preferred_element_type=F32)
    if hi == 3:
        (ah, al), (bh, bl) = _split_bf16(a), _split_bf16(b)
        dot = lambda x, y: lax.dot_general(x, y, dims, preferred_element_type=F32)
        return dot(ah, bh) + (dot(ah, bl) + dot(al, bh))
    return lax.dot_general(a.astype(BF16), b.astype(BF16), dims, preferred_element_type=F32)


def _make_dots(hi):
    @jax.custom_vjp
    def nn(a, b):
        return _dg(a, b, 1, 0, hi)

    @jax.custom_vjp
    def nt(a, b):
        return _dg(a, b, 1, 1, hi)

    @jax.custom_vjp
    def tn(a, b):
        return _dg(a, b, 0, 0, hi)

    nn.defvjp(lambda a, b: (nn(a, b), (a, b)), lambda r, g: (nt(g, r[1]), tn(r[0], g)))
    nt.defvjp(lambda a, b: (nt(a, b), (a, b)), lambda r, g: (nn(g, r[1]), tn(g, r[0])))
    tn.defvjp(lambda a, b: (tn(a, b), (a, b)), lambda r, g: (nt(r[1], g), nn(r[0], g)))
    return nn, nt, tn


_nn, _nt, _tn = _make_dots(False)
_hnn, _hnt, _htn = _make_dots(True)


def _neumann(a):
    n = a[0].shape[0]
    eye = (lax.broadcasted_iota(jnp.int32, (n, n), 0) == lax.broadcasted_iota(jnp.int32, (n, n), 1)).astype(F32)
    hd = range(len(a))
    p = [_dg(a[h], a[h], 1, 0, 3) for h in hd]
    x = [(eye - a[h]) + p[h] - _dg(a[h], p[h], 1, 0, False) for h in hd]
    for _ in range(4):
        p = [_dg(p[h], p[h], 1, 0, False) for h in hd]
        x = [x[h] + p[h] + _dg(x[h] - eye, p[h], 1, 0, False) for h in hd]
    return tuple(x)


@jax.custom_vjp
def _inv_unit_lower(a):
    return _neumann(a)


def _inv_unit_lower_bwd(x, g):
    t = [_dg(x[h], g[h], 0, 0, 3) for h in range(len(x))]
    return (tuple(-_dg(t[h], x[h], 1, 1, 3) for h in range(len(x))),)


_inv_unit_lower.defvjp(lambda a: (_neumann(a),) * 2, _inv_unit_lower_bwd)


@jax.custom_vjp
def _kept_inverse(a, x):
    return x


_kept_inverse.defvjp(lambda a, x: (x, x),
                     lambda x, g: _inv_unit_lower_bwd(x, g) + (tuple(jnp.zeros_like(e) for e in x),))


def mm(a, b, *, tb=False, add=None, norm_gain=None, out_dtype=F32, name):
    m, kdim = a.shape
    (n, kb) = b.shape if tb else b.shape[::-1]
    assert kdim == kb, (a.shape, b.shape, tb)
    tm = _tile(m, 1408, 128) if m % 128 == 0 and m <= 4096 else _tile(m, 640, 64)
    tn = _tile(n, 3072 if (kdim <= 1024 and not tb and add is None) else 1536, 128)
    tk = kdim if kdim <= 3072 else _tile(kdim, 1664 if a.dtype == b.dtype == BF16 else 640, 128)
    nk = kdim // tk
    dims = (((1,), (1 if tb else 0,)), ((), ()))

    normed = norm_gain is not None
    assert not normed or (tn == n and nk == 1 and tm % 128 == 0), (name, tm, tn, nk)

    def body(*refs):
        refs = list(refs)
        a_ref, b_ref = refs[0], refs[1]
        add_ref = refs.pop(2) if add is not None else None
        gain_ref = refs.pop(2) if normed else None
        o_ref, acc = refs[2], refs[-1]
        k = pl.program_id(2)
        part = lax.dot_general(a_ref[...].astype(BF16), b_ref[...].astype(BF16), dims, preferred_element_type=F32)

        def finish(r):
            if add is not None:
                r = r + add_ref[...]
            o_ref[...] = r.astype(out_dtype)
            if normed:
                u = _rms(r, gain_ref[...])
                refs[3][...] = u.astype(BF16)
                refs[4][...] = u.T.astype(BF16)

        if nk == 1:
            finish(part)
        else:
            @pl.when(k == 0)
            def _():
                acc[...] = part

            @pl.when(jnp.logical_and(k > 0, k < nk - 1))
            def _():
                acc[...] += part

            @pl.when(k == nk - 1)
            def _():
                finish(acc[...] + part)

    a_spec = pl.BlockSpec((tm, tk), lambda j, i, k: (i, k))
    b_spec = pl.BlockSpec((tn, tk), lambda j, i, k: (j, k)) if tb else pl.BlockSpec((tk, tn), lambda j, i, k: (k, j))
    in_specs = [a_spec, b_spec]
    args = [a, b]
    if add is not None:
        in_specs.append(pl.BlockSpec((tm, tn), lambda j, i, k: (i, j)))
        args.append(add)
    out_specs = pl.BlockSpec((tm, tn), lambda j, i, k: (i, j))
    out_shape = jax.ShapeDtypeStruct((m, n), out_dtype)
    if normed:
        in_specs.append(pl.BlockSpec((1, n), lambda j, i, k: (0, 0)))
        args.append(norm_gain)
        out_specs = (out_specs, out_specs, pl.BlockSpec((n, tm), lambda j, i, k: (0, i)))
        out_shape = (out_shape, jax.ShapeDtypeStruct((m, n), BF16), jax.ShapeDtypeStruct((n, m), BF16))
    return pl.pallas_call(
        body, grid=(n // tn, m // tm, nk), in_specs=in_specs, out_specs=out_specs, out_shape=out_shape,
        scratch_shapes=[pltpu.VMEM((tm, tn) if nk > 1 else (8, 128), F32)],
        compiler_params=_params("parallel", "parallel", "arbitrary"), name=name,
    )(*args)


def mm_nt_sum(pairs, name):
    m, n = pairs[0][0].shape[0], pairs[0][1].shape[0]
    tm = _tile(m, 320, 64)
    count = len(pairs)

    def body(*refs):
        o_ref = refs[2 * count]
        total = None
        for p in range(count):
            part = lax.dot_general(refs[2 * p][...].astype(BF16), refs[2 * p + 1][...].astype(BF16),
                                   (((1,), (1,)), ((), ())), preferred_element_type=F32)
            total = part if total is None else total + part
        o_ref[...] = total

    in_specs, args = [], []
    for a, b in pairs:
        assert a.shape == (m, b.shape[1]) and b.shape[0] == n, (a.shape, b.shape)
        in_specs += [pl.BlockSpec((tm, a.shape[1]), lambda i: (i, 0)), pl.BlockSpec(b.shape, lambda i: (0, 0))]
        args += [a, b]
    return pl.pallas_call(
        body, grid=(m // tm,), in_specs=in_specs, out_specs=pl.BlockSpec((tm, n), lambda i: (i, 0)),
        out_shape=jax.ShapeDtypeStruct((m, n), F32), compiler_params=_params("parallel"), name=name,
    )(*args)


def _rms(x, gain):
    return x * lax.rsqrt(jnp.mean(x * x, axis=-1, keepdims=True) + NORM_EPS) * gain


def rms_fwd(h, gain, name):
    t = h.shape[0]
    ts = _tile(t, 640, 128)

    def body(h_ref, g_ref, u_ref, ut_ref):
        u = _rms(h_ref[...], g_ref[...])
        u_ref[...] = u.astype(BF16)
        ut_ref[...] = u.T.astype(BF16)

    return pl.pallas_call(
        body, grid=(t // ts,),
        in_specs=[pl.BlockSpec((ts, D_MODEL), lambda i: (i, 0)), pl.BlockSpec((1, D_MODEL), lambda i: (0, 0))],
        out_specs=(pl.BlockSpec((ts, D_MODEL), lambda i: (i, 0)), pl.BlockSpec((D_MODEL, ts), lambda i: (0, i))),
        out_shape=(jax.ShapeDtypeStruct((t, D_MODEL), BF16), jax.ShapeDtypeStruct((D_MODEL, t), BF16)),
        compiler_params=_params("parallel"), name=name,
    )(h, gain)


def rms_bwd(h, gain, du, dres, name):
    t = h.shape[0]
    ts = _tile(t, 640, 64)

    def body(h_ref, g_ref, du_ref, dres_ref, dh_ref, dhb_ref, dg_ref):
        i = pl.program_id(0)
        _, vjp = jax.vjp(_rms, h_ref[...], g_ref[...])
        dx, dg = vjp(du_ref[...])
        row = i * ts + lax.broadcasted_iota(jnp.int32, (ts, 1), 0)
        dh = jnp.where(row >= LEAD, dx + dres_ref[...], 0.0)
        dh_ref[...] = dh
        dhb_ref[...] = dh.astype(BF16)

        @pl.when(i == 0)
        def _():
            dg_ref[...] = jnp.zeros_like(dg_ref)

        dg_ref[...] += dg

    row_spec = pl.BlockSpec((ts, D_MODEL), lambda i: (i, 0))
    vec_spec = pl.BlockSpec((1, D_MODEL), lambda i: (0, 0))
    return pl.pallas_call(
        body, grid=(t // ts,), in_specs=[row_spec, vec_spec, row_spec, row_spec],
        out_specs=(row_spec, row_spec, vec_spec),
        out_shape=(jax.ShapeDtypeStruct((t, D_MODEL), F32), jax.ShapeDtypeStruct((t, D_MODEL), BF16),
                   jax.ShapeDtypeStruct((1, D_MODEL), F32)),
        compiler_params=_params("arbitrary"), name=name,
    )(h, gain, du, dres)


def loss_head(h, gain, target, name):
    t = h.shape[0]
    ts = _tile(t, 640, 64)

    def body(h_ref, g_ref, t_ref, loss_ref, dh_ref, dhb_ref, dg_ref):
        i = pl.program_id(0)
        row = i * ts + lax.broadcasted_iota(jnp.int32, (ts, 1), 0)
        keep = jnp.logical_and(row >= LEAD + N_META, row < t - TAIL)
        tgt = t_ref[...]

        def f(x, g):
            err = jnp.where(keep, _rms(x, g) - tgt, 0.0)
            per_row = jnp.mean(err * err, axis=-1, keepdims=True)
            return 0.5 * jnp.sum(per_row, axis=0, keepdims=True)

        val, vjp = jax.vjp(f, h_ref[...], g_ref[...])
        dx, dg = vjp(jnp.ones((1, 1), F32))
        dh_ref[...] = dx
        dhb_ref[...] = dx.astype(BF16)

        @pl.when(i == 0)
        def _():
            dg_ref[...] = jnp.zeros_like(dg_ref)
            loss_ref[...] = jnp.zeros_like(loss_ref)

        dg_ref[...] += dg
        loss_ref[...] += jnp.broadcast_to(val, (1, 128))

    row_spec = pl.BlockSpec((ts, D_MODEL), lambda i: (i, 0))
    vec_spec = pl.BlockSpec((1, D_MODEL), lambda i: (0, 0))
    return pl.pallas_call(
        body, grid=(t // ts,), in_specs=[row_spec, vec_spec, row_spec],
        out_specs=(pl.BlockSpec((1, 128), lambda i: (0, 0)), row_spec, row_spec, vec_spec),
        out_shape=(jax.ShapeDtypeStruct((1, 128), F32), jax.ShapeDtypeStruct((t, D_MODEL), F32),
                   jax.ShapeDtypeStruct((t, D_MODEL), BF16), jax.ShapeDtypeStruct((1, D_MODEL), F32)),
        compiler_params=_params("arbitrary"), name=name,
    )(h, gain, target)


def _rows_down(a, s):
    return a if s == 0 else pltpu.roll(a, s, axis=0)


def _rows_up(a, s):
    return a if s == 0 else pltpu.roll(a, a.shape[0] - s, axis=0)


def conv_fwd(x, w, name):
    t, width = x.shape
    k = w.shape[0]
    ts = _tile(t, 640, 64)
    tw = _tile(width, 1536, 128)
    hb = ts // 8

    def body(x_ref, halo_ref, w_ref, o_ref, buf):
        i = pl.program_id(0)
        buf[0:8, :] = jnp.where(i > 0, halo_ref[...], 0.0)
        buf[8:, :] = x_ref[...]
        ext = buf[...]
        wv = w_ref[...]
        acc = _rows_down(ext, k - 1)[8:, :] * wv[0:1, :]
        for j in range(1, k):
            acc = acc + _rows_down(ext, k - 1 - j)[8:, :] * wv[j:j + 1, :]
        o_ref[...] = acc

    return pl.pallas_call(
        body, grid=(t // ts, width // tw),
        in_specs=[pl.BlockSpec((ts, tw), lambda i, j: (i, j)),
                  pl.BlockSpec((8, tw), lambda i, j: (jnp.maximum(i * hb - 1, 0), j)),
                  pl.BlockSpec((k, tw), lambda i, j: (0, j))],
        out_specs=pl.BlockSpec((ts, tw), lambda i, j: (i, j)),
        out_shape=jax.ShapeDtypeStruct((t, width), F32),
        scratch_shapes=[pltpu.VMEM((ts + 8, tw), F32)],
        compiler_params=_params("parallel", "parallel"), name=name,
    )(x, x, w)


def conv_bwd(x, dc, w, name):
    t, width = x.shape
    k = w.shape[0]
    ts = _tile(t, 640, 64)
    tw = _tile(width, 1536, 128)
    hb = ts // 8
    nt = t // ts

    def body(x_ref, xh_ref, dc_ref, dch_ref, w_ref, dx_ref, dw_ref, xbuf, dbuf):
        i = pl.program_id(1)
        xbuf[0:8, :] = jnp.where(i > 0, xh_ref[...], 0.0)
        xbuf[8:, :] = x_ref[...]
        d = dc_ref[...]
        dbuf[0:ts, :] = d
        dbuf[ts:, :] = jnp.where(i < nt - 1, dch_ref[...], 0.0)
        wv = w_ref[...]
        ext_x, ext_d = xbuf[...], dbuf[...]
        acc = _rows_up(ext_d, k - 1)[0:ts, :] * wv[0:1, :]
        for j in range(1, k):
            acc = acc + _rows_up(ext_d, k - 1 - j)[0:ts, :] * wv[j:j + 1, :]
        dx_ref[...] = acc.astype(BF16)

        @pl.when(i == 0)
        def _():
            dw_ref[...] = jnp.zeros_like(dw_ref)

        for j in range(k):
            dw_ref[j:j + 1, :] += jnp.sum(d * _rows_down(ext_x, k - 1 - j)[8:, :], axis=0, keepdims=True)

    return pl.pallas_call(
        body, grid=(width // tw, nt),
        in_specs=[pl.BlockSpec((ts, tw), lambda j, i: (i, j)),
                  pl.BlockSpec((8, tw), lambda j, i: (jnp.maximum(i * hb - 1, 0), j)),
                  pl.BlockSpec((ts, tw), lambda j, i: (i, j)),
                  pl.BlockSpec((8, tw), lambda j, i: (jnp.minimum((i + 1) * hb, t // 8 - 1), j)),
                  pl.BlockSpec((k, tw), lambda j, i: (0, j))],
        out_specs=(pl.BlockSpec((ts, tw), lambda j, i: (i, j)), pl.BlockSpec((8, tw), lambda j, i: (0, j))),
        out_shape=(jax.ShapeDtypeStruct((t, width), BF16), jax.ShapeDtypeStruct((8, width), F32)),
        scratch_shapes=[pltpu.VMEM((ts + 8, tw), F32), pltpu.VMEM((ts + 8, tw), F32)],
        compiler_params=_params("parallel", "arbitrary"), name=name,
    )(x, x, dc, dc, w)


def _pool_count(pos, win):
    return jnp.clip(pos + 1, 1, win).astype(F32)


def poolwin_fwd(p, name):
    t = p.shape[0]
    ts = _tile(t, 640, 64)
    hb = ts // 16

    def body(p_ref, halo_ref, o_ref, buf):
        i = pl.program_id(0)
        buf[0:16, :] = jnp.where(i > 0, halo_ref[...], 0.0)
        buf[16:, :] = p_ref[...]
        pos = i * ts + lax.broadcasted_iota(jnp.int32, (ts, 1), 0) - LEAD
        ext = buf[...]
        own = ext[16:, :]
        sums, span = ext, 1
        for gi, win in enumerate(POOL_WINDOWS):
            while span < win:
                sums = sums + _rows_down(sums, span)
                span *= 2
            cols = slice(gi * 128, (gi + 1) * 128)
            o_ref[:, cols] = sums[16:, cols] / _pool_count(pos, win) - own[:, cols]

    return pl.pallas_call(
        body, grid=(t // ts,),
        in_specs=[pl.BlockSpec((ts, POOL_WIDTH), lambda i: (i, 0)),
                  pl.BlockSpec((16, POOL_WIDTH), lambda i: (jnp.maximum(i * hb - 1, 0), 0))],
        out_specs=pl.BlockSpec((ts, POOL_WIDTH), lambda i: (i, 0)),
        out_shape=jax.ShapeDtypeStruct((t, POOL_WIDTH), F32),
        scratch_shapes=[pltpu.VMEM((ts + 16, POOL_WIDTH), F32)],
        compiler_params=_params("parallel"), name=name,
    )(p, p)


def poolwin_bwd(dpooled, name):
    t = dpooled.shape[0]
    ts = _tile(t, 640, 64)
    hb = ts // 16
    nt = t // ts

    def body(d_ref, halo_ref, o_ref, buf):
        i = pl.program_id(0)
        buf[0:ts, :] = d_ref[...]
        buf[ts:, :] = jnp.where(i < nt - 1, halo_ref[...], 0.0)
        pos = i * ts + lax.broadcasted_iota(jnp.int32, (ts + 16, 1), 0) - LEAD
        ext = buf[...]
        for gi, win in enumerate(POOL_WINDOWS):
            cols = slice(gi * 128, (gi + 1) * 128)
            sums, span = ext[:, cols] / _pool_count(pos, win), 1
            while span < win:
                sums = sums + _rows_up(sums, span)
                span *= 2
            o_ref[:, cols] = (sums[0:ts, :] - ext[0:ts, cols]).astype(BF16)

    return pl.pallas_call(
        body, grid=(nt,),
        in_specs=[pl.BlockSpec((ts, POOL_WIDTH), lambda i: (i, 0)),
                  pl.BlockSpec((16, POOL_WIDTH), lambda i: (jnp.minimum((i + 1) * hb, t // 16 - 1), 0))],
        out_specs=pl.BlockSpec((ts, POOL_WIDTH), lambda i: (i, 0)),
        out_shape=jax.ShapeDtypeStruct((t, POOL_WIDTH), BF16),
        scratch_shapes=[pltpu.VMEM((ts + 16, POOL_WIDTH), F32)],
        compiler_params=_params("parallel"), name=name,
    )(dpooled, dpooled)


def _mix(y_a, gpre, pooled, w_pool, scale):
    parts = [_nn(pooled[:, g * 128:(g + 1) * 128], w_pool[g]) for g in range(4)]
    y_b = jnp.concatenate(parts, axis=1) * scale
    return jax.nn.sigmoid(gpre[:, :D_MODEL]) * y_a + jax.nn.sigmoid(gpre[:, D_MODEL:]) * y_b


def _mix_specs(ts):
    return [pl.BlockSpec((ts, D_MODEL), lambda i: (i, 0)), pl.BlockSpec((ts, 2 * D_MODEL), lambda i: (i, 0)),
            pl.BlockSpec((ts, POOL_WIDTH), lambda i: (i, 0)), pl.BlockSpec((4, 128, 256), lambda i: (0, 0, 0)),
            pl.BlockSpec((1, D_MODEL), lambda i: (0, 0))]


def mix_fwd(y_a, gpre, pooled, w_pool, scale, name):
    t = y_a.shape[0]
    ts = _tile(t, 640, 128)

    def body(ya_ref, g_ref, p_ref, w_ref, s_ref, o_ref, ot_ref):
        y = _mix(ya_ref[...], g_ref[...], p_ref[...], w_ref[...], s_ref[...])
        o_ref[...] = y.astype(BF16)
        ot_ref[...] = y.T.astype(BF16)

    return pl.pallas_call(
        body, grid=(t // ts,), in_specs=_mix_specs(ts),
        out_specs=(pl.BlockSpec((ts, D_MODEL), lambda i: (i, 0)), pl.BlockSpec((D_MODEL, ts), lambda i: (0, i))),
        out_shape=(jax.ShapeDtypeStruct((t, D_MODEL), BF16), jax.ShapeDtypeStruct((D_MODEL, t), BF16)),
        compiler_params=_params("parallel"), name=name,
    )(y_a, gpre, pooled, w_pool, scale)


def mix_bwd(y_a, gpre, pooled, w_pool, scale, dy, name):
    t = y_a.shape[0]
    ts = _tile(t, 320, 64)

    def body(ya_ref, g_ref, p_ref, w_ref, s_ref, dy_ref, dya_ref, dg_ref, dp_ref, dw_ref, ds_ref):
        i = pl.program_id(0)
        _, vjp = jax.vjp(_mix, ya_ref[...], g_ref[...], p_ref[...], w_ref[...], s_ref[...])
        dya, dg, dp, dw, ds = vjp(dy_ref[...])
        dya_ref[...] = dya
        dg_ref[...] = dg.astype(BF16)
        dp_ref[...] = dp

        @pl.when(i == 0)
        def _():
            dw_ref[...] = jnp.zeros_like(dw_ref)
            ds_ref[...] = jnp.zeros_like(ds_ref)

        dw_ref[...] += dw
        ds_ref[...] += ds

    specs = _mix_specs(ts)
    return pl.pallas_call(
        body, grid=(t // ts,), in_specs=specs + [specs[0]],
        out_specs=(specs[0], specs[1], specs[2], specs[3], specs[4]),
        out_shape=(jax.ShapeDtypeStruct((t, D_MODEL), F32), jax.ShapeDtypeStruct((t, 2 * D_MODEL), BF16),
                   jax.ShapeDtypeStruct((t, POOL_WIDTH), F32), jax.ShapeDtypeStruct((4, 128, 256), F32),
                   jax.ShapeDtypeStruct((1, D_MODEL), F32)),
        compiler_params=_params("arbitrary"), name=name,
    )(y_a, gpre, pooled, w_pool, scale, dy)


def _ffn_act(cg, cv):
    return _silu(cg) * cv


FFN_MID_VMEM_BYTES = 58 * 1024 * 1024


def ffn_mid_fwd(hg, hv, wg, wv, name):
    t, width = hg.shape
    k = wg.shape[0]
    ts = _tile(t, 640, 128)
    tw = _tile(width, 1536, 128)
    hb = ts // 8

    def body(hg_ref, hgp_ref, hv_ref, hvp_ref, wg_ref, wv_ref, cg_ref, cv_ref, act_ref, actt_ref, xg, xv):
        i = pl.program_id(0)
        convs = []
        for x_ref, xp_ref, buf, w_ref, c_ref in ((hg_ref, hgp_ref, xg, wg_ref, cg_ref),
                                                 (hv_ref, hvp_ref, xv, wv_ref, cv_ref)):
            buf[0:8, :] = jnp.where(i > 0, xp_ref[...], 0.0)
            buf[8:, :] = x_ref[...]
            ext, wt = buf[...], w_ref[...]
            acc = _rows_down(ext, k - 1)[8:, :] * wt[0:1, :]
            for j in range(1, k):
                acc = acc + _rows_down(ext, k - 1 - j)[8:, :] * wt[j:j + 1, :]
            c_ref[...] = acc
            convs.append(acc)
        act = _ffn_act(*convs)
        act_ref[...] = act.astype(BF16)
        actt_ref[...] = act.T.astype(BF16)

    tile = pl.BlockSpec((ts, tw), lambda i, j: (i, j))
    prev = pl.BlockSpec((8, tw), lambda i, j: (jnp.maximum(i * hb - 1, 0), j))
    taps = pl.BlockSpec((k, tw), lambda i, j: (0, j))
    return pl.pallas_call(
        body, grid=(t // ts, width // tw), in_specs=[tile, prev, tile, prev, taps, taps],
        out_specs=(tile, tile, tile, pl.BlockSpec((tw, ts), lambda i, j: (j, i))),
        out_shape=(jax.ShapeDtypeStruct((t, width), F32), jax.ShapeDtypeStruct((t, width), F32),
                   jax.ShapeDtypeStruct((t, width), BF16), jax.ShapeDtypeStruct((width, t), BF16)),
        scratch_shapes=[pltpu.VMEM((ts + 8, tw), F32)] * 2,
        compiler_params=pltpu.CompilerParams(dimension_semantics=("parallel", "parallel"),
                                             vmem_limit_bytes=FFN_MID_VMEM_BYTES), name=name,
    )(hg, hg, hv, hv, wg, wv)


def ffn_mid_bwd(hg, hv, cg, cv, dact, wg, wv, name):
    t, width = hg.shape
    k = wg.shape[0]
    ts = _tile(t, 320, 64)
    tw = _tile(width, 1536, 128)
    hb = ts // 8
    nt = t // ts

    def body(hg_ref, hgp_ref, hv_ref, hvp_ref, cg_ref, cgn_ref, cv_ref, cvn_ref, da_ref, dan_ref, wg_ref, wv_ref,
             dhg_ref, dhv_ref, dwg_ref, dwv_ref, xg, xv, dg, dv):
        i = pl.program_id(1)
        behind = i < nt - 1

        def d_conv(c_g, c_v, d_a):
            _, vjp = jax.vjp(_ffn_act, c_g, c_v)
            return vjp(d_a)

        dcg, dcv = d_conv(cg_ref[...], cv_ref[...], da_ref[...])
        dcg_n, dcv_n = d_conv(cgn_ref[...], cvn_ref[...], jnp.where(behind, dan_ref[...], 0.0))

        @pl.when(i == 0)
        def _():
            dwg_ref[...] = jnp.zeros_like(dwg_ref)
            dwv_ref[...] = jnp.zeros_like(dwv_ref)

        for x_ref, xp_ref, xbuf, dbuf, d, d_n, w_ref, dx_ref, dw_ref in (
                (hg_ref, hgp_ref, xg, dg, dcg, dcg_n, wg_ref, dhg_ref, dwg_ref),
                (hv_ref, hvp_ref, xv, dv, dcv, dcv_n, wv_ref, dhv_ref, dwv_ref)):
            xbuf[0:8, :] = jnp.where(i > 0, xp_ref[...], 0.0)
            xbuf[8:, :] = x_ref[...]
            dbuf[0:ts, :] = d
            dbuf[ts:, :] = jnp.where(behind, d_n, 0.0)
            wt = w_ref[...]
            ext_x, ext_d = xbuf[...], dbuf[...]
            acc = _rows_up(ext_d, k - 1)[0:ts, :] * wt[0:1, :]
            for j in range(1, k):
                acc = acc + _rows_up(ext_d, k - 1 - j)[0:ts, :] * wt[j:j + 1, :]
            dx_ref[...] = acc.astype(BF16)
            for j in range(k):
                dw_ref[j:j + 1, :] += jnp.sum(d * _rows_down(ext_x, k - 1 - j)[8:, :], axis=0, keepdims=True)

    tile = pl.BlockSpec((ts, tw), lambda j, i: (i, j))
    prev = pl.BlockSpec((8, tw), lambda j, i: (jnp.maximum(i * hb - 1, 0), j))
    nxt = pl.BlockSpec((8, tw), lambda j, i: (jnp.minimum((i + 1) * hb, t // 8 - 1), j))
    taps = pl.BlockSpec((k, tw), lambda j, i: (0, j))
    dw_spec = pl.BlockSpec((8, tw), lambda j, i: (0, j))
    return pl.pallas_call(
        body, grid=(width // tw, nt),
        in_specs=[tile, prev, tile, prev, tile, nxt, tile, nxt, tile, nxt, taps, taps],
        out_specs=(tile, tile, dw_spec, dw_spec),
        out_shape=(jax.ShapeDtypeStruct((t, width), BF16), jax.ShapeDtypeStruct((t, width), BF16),
                   jax.ShapeDtypeStruct((8, width), F32), jax.ShapeDtypeStruct((8, width), F32)),
        scratch_shapes=[pltpu.VMEM((ts + 8, tw), F32)] * 4,
        compiler_params=_params("parallel", "arbitrary"), name=name,
    )(hg, hg, hv, hv, cg, cg, cv, cv, dact, dact, wg, wv)


def _gdn_chunk(c, z, ba, pa, pdt, hn, s, *, valid, inverse=None, with_inverse=False):
    r = lax.broadcasted_iota(jnp.int32, (CHUNK, CHUNK), 0)
    q_ = lax.broadcasted_iota(jnp.int32, (CHUNK, CHUNK), 1)
    causal = r >= q_
    strict = r > q_
    tril = causal.astype(F32)
    triu = (r <= q_).astype(F32)
    lane = lax.broadcasted_iota(jnp.int32, (CHUNK, 128), 1)

    decay_log = -jnp.exp(pa) * _softplus(ba + pdt)
    bg = jnp.where(lane < HEADS, jax.nn.sigmoid(ba), jnp.where(lane < 2 * HEADS, decay_log, 0.0))
    bg = jnp.where(valid, bg, 0.0)
    gc = _hnn(tril, bg)
    gct = _hnn(bg.T, triu)
    eg = jnp.exp(gc)
    glast = gc[CHUNK - 1:CHUNK, :]
    ekd = jnp.exp(glast - gc)
    gtot = jnp.exp(glast)

    hd = range(HEADS)
    hs = [slice(h * HEAD_DIM, (h + 1) * HEAD_DIM) for h in hd]
    gl = [slice(HEADS + h, HEADS + h + 1) for h in hd]
    q = [_silu(c[:, hs[h]]) for h in hd]
    k = [_silu(c[:, D_MODEL + h * HEAD_DIM:D_MODEL + (h + 1) * HEAD_DIM]) for h in hd]
    v = [_silu(c[:, 2 * D_MODEL + h * HEAD_DIM:2 * D_MODEL + (h + 1) * HEAD_DIM]) for h in hd]
    q = [q[h] * lax.rsqrt(jnp.sum(q[h] * q[h], axis=-1, keepdims=True) + NORM_EPS) * (HEAD_DIM ** -0.5) for h in hd]
    k = [k[h] * lax.rsqrt(jnp.sum(k[h] * k[h], axis=-1, keepdims=True) + NORM_EPS) for h in hd]
    beta = [bg[:, h:h + 1] for h in hd]
    decay = [jnp.exp(jnp.where(causal, gc[:, gl[h]] - gct[gl[h], :], -1e30)) for h in hd]
    kb = [k[h] * beta[h] for h in hd]
    a = [jnp.where(strict, _nt(kb[h], k[h]) * decay[h], 0.0) for h in hd]
    qk = [jnp.where(causal, _nt(q[h], k[h]) * decay[h], 0.0) for h in hd]
    x = _inv_unit_lower(tuple(a)) if inverse is None else _kept_inverse(tuple(a), tuple(inverse))
    u = [_nn(x[h], v[h] * beta[h]) for h in hd]
    w = [_nn(x[h], kb[h] * eg[:, gl[h]]) for h in hd]
    v_new = [u[h] - _nn(w[h], s[h]) for h in hd]
    o = [_nn(q[h] * eg[:, gl[h]], s[h]) + _nn(qk[h], v_new[h]) for h in hd]
    states = [s[h] * gtot[:, gl[h]] + _tn(k[h] * ekd[:, gl[h]], v_new[h]) for h in hd]
    o = [o[h] * lax.rsqrt(jnp.mean(o[h] * o[h], axis=-1, keepdims=True) + NORM_EPS) * hn * _silu(z[:, hs[h]])
         for h in hd]
    if with_inverse:
        return jnp.concatenate(o, axis=1), tuple(states), x
    return jnp.concatenate(o, axis=1), tuple(states)


GDN_FWD_CHUNKS = 5
GDN_BWD_CHUNKS = 2


def _chunk_valid(n, t):
    row = n * CHUNK + lax.broadcasted_iota(jnp.int32, (CHUNK, 1), 0)
    return jnp.logical_and(row >= LEAD, row < t - TAIL)


def gdn_fwd(c, z, ba, pa, pdt, hn, name, ba_block=0):
    t = c.shape[0]
    n_chunks = t // CHUNK
    per_step = GDN_FWD_CHUNKS if n_chunks % GDN_FWD_CHUNKS == 0 else 1
    rows_per_step = per_step * CHUNK

    def body(c_ref, z_ref, ba_ref, pa_ref, pdt_ref, hn_ref, y_ref, ss_ref, inv_ref, state):
        step = pl.program_id(0)

        @pl.when(step == 0)
        def _():
            state[...] = jnp.zeros_like(state)

        s = tuple(state[h] for h in range(HEADS))
        for j in range(per_step):
            rows = pl.ds(j * CHUNK, CHUNK)
            for h in range(HEADS):
                ss_ref[j, h] = s[h]
            y, s, inv = _gdn_chunk(c_ref[rows, :], z_ref[rows, :], ba_ref[rows, :], pa_ref[...], pdt_ref[...],
                                   hn_ref[...], s, valid=_chunk_valid(step * per_step + j, t), with_inverse=True)
            y_ref[rows, :] = y
            for h in range(HEADS):
                inv_ref[j, h] = inv[h]
        for h in range(HEADS):
            state[h] = s[h]

    vec = pl.BlockSpec((1, 128), lambda n: (0, 0))
    return pl.pallas_call(
        body, grid=(n_chunks // per_step,),
        in_specs=[pl.BlockSpec((rows_per_step, QKV_DIM), lambda n: (n, 0)),
                  pl.BlockSpec((rows_per_step, D_MODEL), lambda n: (n, 0)),
                  pl.BlockSpec((rows_per_step, 128), lambda n: (n, ba_block)), vec, vec, vec],
        out_specs=(pl.BlockSpec((rows_per_step, D_MODEL), lambda n: (n, 0)),
                   pl.BlockSpec((per_step, HEADS, HEAD_DIM, HEAD_DIM), lambda n: (n, 0, 0, 0)),
                   pl.BlockSpec((per_step, HEADS, CHUNK, CHUNK), lambda n: (n, 0, 0, 0))),
        out_shape=(jax.ShapeDtypeStruct((t, D_MODEL), F32),
                   jax.ShapeDtypeStruct((n_chunks, HEADS, HEAD_DIM, HEAD_DIM), F32),
                   jax.ShapeDtypeStruct((n_chunks, HEADS, CHUNK, CHUNK), F32)),
        scratch_shapes=[pltpu.VMEM((HEADS, HEAD_DIM, HEAD_DIM), F32)],
        compiler_params=_params("arbitrary"), name=name,
    )(c, z, ba, pa, pdt, hn)


def gdn_bwd(c, z, ba, pa, pdt, hn, starts, inverses, dy, name, ba_block=0):
    t = c.shape[0]
    per_step = GDN_BWD_CHUNKS if (t // CHUNK) % GDN_BWD_CHUNKS == 0 else 1
    n_steps = t // CHUNK // per_step
    rows_per_step = per_step * CHUNK

    def body(c_ref, z_ref, ba_ref, pa_ref, pdt_ref, hn_ref, ss_ref, inv_ref, dy_ref,
             dc_ref, dz_ref, dba_ref, dpa_ref, dpdt_ref, dhn_ref, dstate):
        step = pl.program_id(0)

        @pl.when(step == 0)
        def _():
            dstate[...] = jnp.zeros_like(dstate)
            dpa_ref[...] = jnp.zeros_like(dpa_ref)
            dpdt_ref[...] = jnp.zeros_like(dpdt_ref)
            dhn_ref[...] = jnp.zeros_like(dhn_ref)

        ds = tuple(dstate[h] for h in range(HEADS))
        for j in reversed(range(per_step)):
            rows = pl.ds(j * CHUNK, CHUNK)
            f = functools.partial(_gdn_chunk, valid=_chunk_valid((n_steps - 1 - step) * per_step + j, t),
                                  inverse=tuple(inv_ref[j, h] for h in range(HEADS)))
            _, vjp = jax.vjp(f, c_ref[rows, :], z_ref[rows, :], ba_ref[rows, :], pa_ref[...], pdt_ref[...], hn_ref[...],
                             tuple(ss_ref[j, h] for h in range(HEADS)))
            dc, dz, dba, dpa, dpdt, dhn, ds = vjp((dy_ref[rows, :], ds))
            dc_ref[rows, :] = dc
            dz_ref[rows, :] = dz.astype(BF16)
            dba_ref[rows, :] = dba.astype(BF16)
            dpa_ref[...] += dpa
            dpdt_ref[...] += dpdt
            dhn_ref[...] += dhn
        for h in range(HEADS):
            dstate[h] = ds[h]

    def rev(width, block=0):
        return pl.BlockSpec((rows_per_step, width), lambda s: (n_steps - 1 - s, block))

    vec = pl.BlockSpec((1, 128), lambda s: (0, 0))
    vec_shape = jax.ShapeDtypeStruct((1, 128), F32)
    return pl.pallas_call(
        body, grid=(n_steps,),
        in_specs=[rev(QKV_DIM), rev(D_MODEL), rev(128, ba_block), vec, vec, vec,
                  pl.BlockSpec((per_step, HEADS, HEAD_DIM, HEAD_DIM), lambda s: (n_steps - 1 - s, 0, 0, 0)),
                  pl.BlockSpec((per_step, HEADS, CHUNK, CHUNK), lambda s: (n_steps - 1 - s, 0, 0, 0)),
                  rev(D_MODEL)],
        out_specs=(rev(QKV_DIM), rev(D_MODEL), rev(128), vec, vec, vec),
        out_shape=(jax.ShapeDtypeStruct((t, QKV_DIM), F32), jax.ShapeDtypeStruct((t, D_MODEL), BF16),
                   jax.ShapeDtypeStruct((t, 128), BF16), vec_shape, vec_shape, vec_shape),
        scratch_shapes=[pltpu.VMEM((HEADS, HEAD_DIM, HEAD_DIM), F32)],
        compiler_params=_params("arbitrary"), name=name,
    )(c, z, ba, pa, pdt, hn, starts, inverses, dy)


def _layer_fwd(h, u, ut, w, next_gain, tag):
    pq = mm(u, w["wqkv"], name=f"{tag}_mm_qkv")
    pz = mm(u, w["wz"], name=f"{tag}_mm_z")
    pg = mm(u, w["wg"], name=f"{tag}_mm_gate")
    pba = mm(u, w["wpb"], name=f"{tag}_mm_pool_ba")
    cq = conv_fwd(pq, w["conv_qkv"], f"{tag}_conv_qkv")
    ya, starts, inverses = gdn_fwd(cq, pz, pba, w["pa"], w["pdt"], w["head_norm"], f"{tag}_gdn", ba_block=BA_BLOCK)
    pooled = poolwin_fwd(pba, f"{tag}_poolwin")
    y, yt = mix_fwd(ya, pg, pooled, w["w_pool"], w["pool_scale"], f"{tag}_mix")
    h1, u2, u2t = mm(y, w["wout"], add=h, norm_gain=w["norm_ffn"], name=f"{tag}_mm_out")
    hg = mm(u2, w["wupg"], name=f"{tag}_mm_upg")
    hv = mm(u2, w["wupv"], name=f"{tag}_mm_upv")
    cg, cv, act, actt = ffn_mid_fwd(hg, hv, w["conv_g"], w["conv_v"], f"{tag}_ffn_mid")
    if next_gain is None:
        h2, nxt = mm(act, w["wdown"], add=h1, name=f"{tag}_mm_down"), None
    else:
        h2, *nxt = mm(act, w["wdown"], add=h1, norm_gain=next_gain, name=f"{tag}_mm_down")
    saved = dict(h=h, ut=ut, pq=pq, pz=pz, pg=pg, pba=pba, cq=cq, ya=ya, starts=starts, inverses=inverses,
                 pooled=pooled, yt=yt, h1=h1, u2t=u2t, hg=hg, hv=hv, cg=cg, cv=cv, actt=actt)
    return h2, nxt, saved


def _layer_bwd(dh2, dh2b, w, s, tag):
    g = {}
    dact = mm(dh2b, w["wdown"], tb=True, name=f"{tag}_bmm_down_x")
    g["wdown"] = mm(s["actt"], dh2b, name=f"{tag}_bmm_down_w")
    dhg, dhv, g["conv_g"], g["conv_v"] = ffn_mid_bwd(s["hg"], s["hv"], s["cg"], s["cv"], dact, w["conv_g"], w["conv_v"],
                                                     f"{tag}_ffn_mid_b")
    du2 = mm_nt_sum([(dhg, w["wupg"]), (dhv, w["wupv"])], f"{tag}_bmm_up_x")
    g["wupg"] = mm(s["u2t"], dhg, name=f"{tag}_bmm_upg_w")
    g["wupv"] = mm(s["u2t"], dhv, name=f"{tag}_bmm_upv_w")
    dh1, dh1b, g["norm_ffn"] = rms_bwd(s["h1"], w["norm_ffn"], du2, dh2, f"{tag}_rms_ffn_b")
    dy = mm(dh1b, w["wout"], tb=True, name=f"{tag}_bmm_out_x")
    g["wout"] = mm(s["yt"], dh1b, name=f"{tag}_bmm_out_w")
    dya, dpg, dpooled, g["w_pool"], g["pool_scale"] = mix_bwd(
        s["ya"], s["pg"], s["pooled"], w["w_pool"], w["pool_scale"], dy, f"{tag}_mix_b")
    dpp = poolwin_bwd(dpooled, f"{tag}_poolwin_b")
    dcq, dpz, dpba, g["pa"], g["pdt"], g["head_norm"] = gdn_bwd(
        s["cq"], s["pz"], s["pba"], w["pa"], w["pdt"], w["head_norm"], s["starts"], s["inverses"], dya, f"{tag}_gdn_b",
        ba_block=BA_BLOCK)
    dpb = jnp.concatenate([dpp, dpba], axis=1)
    dpq, g["conv_qkv"] = conv_bwd(s["pq"], dcq, w["conv_qkv"], f"{tag}_conv_qkv_b")
    du = mm_nt_sum([(dpq, w["wqkv"]), (dpz, w["wz"]), (dpg, w["wg"]), (dpb, w["wpb"])], f"{tag}_bmm_in_x")
    g["wqkv"] = mm(s["ut"], dpq, name=f"{tag}_bmm_qkv_w")
    g["wz"] = mm(s["ut"], dpz, name=f"{tag}_bmm_z_w")
    g["wg"] = mm(s["ut"], dpg, name=f"{tag}_bmm_gate_w")
    dwpb = mm(s["ut"], dpb, name=f"{tag}_bmm_pool_ba_w")
    g["wpl"], g["wba"] = dwpb[:, :POOL_WIDTH], dwpb[:, POOL_WIDTH:]
    dh, dhb, g["norm_mix"] = rms_bwd(s["h"], w["norm_mix"], du, dh1, f"{tag}_rms_mix_b")
    return dh, dhb, g


def local_step(h0, target, layers, norm_final):
    h = h0
    normed = rms_fwd(h0, layers[0]["norm_mix"], "l0_rms_mix")
    saved = []
    for li, w in enumerate(layers):
        next_gain = layers[li + 1]["norm_mix"] if li + 1 < len(layers) else None
        h, normed, s = _layer_fwd(h, *normed, w, next_gain, f"l{li}")
        saved.append(s)
    loss, dh, dhb, dnf = loss_head(h, norm_final, target, "loss_head")
    grads = [None] * len(layers)
    for li in reversed(range(len(layers))):
        dh, dhb, grads[li] = _layer_bwd(dh, dhb, layers[li], saved[li], f"l{li}")
    return loss, dh, grads, dnf


_Z0, _B0, _P0, _G0, _IN_DIM = 3072, 4096, 4112, 4624, 6672


def _lanes_8_to_15(v):
    return jnp.pad(v.reshape(1, HEADS).astype(F32), ((0, 0), (HEADS, 128 - 2 * HEADS)))


IN_PIECES = (("wqkv", 0, _Z0), ("wz", _Z0, _B0), ("wba", _B0, _P0), ("wpl", _P0, _G0), ("wg", _G0, _IN_DIM))
IN_SHARD = _IN_DIM // 4


def _overlaps(a, b, spans):
    return [(name, max(a, lo) - lo, min(b, hi) - max(a, lo)) for name, lo, hi in spans if max(a, lo) < min(b, hi)]


def _cat(parts):
    return parts[0] if len(parts) == 1 else jnp.concatenate(parts, axis=1)


def prep_layer(p):
    row = lambda v: v.reshape(1, -1).astype(F32)
    w_in, w_up = p["w_in"], p["w_up"]
    if not isinstance(w_in, (list, tuple)):
        w_in = [w_in[:, s * IN_SHARD:(s + 1) * IN_SHARD] for s in range(4)]
        w_up = [w_up[:, s * (D_FF // 2):(s + 1) * (D_FF // 2)] for s in range(4)]
    shards = [(s, s * IN_SHARD, (s + 1) * IN_SHARD) for s in range(4)]
    piece = {name: _cat([w_in[s][:, off:off + width].astype(BF16) for s, off, width in _overlaps(lo, hi, shards)])
             for name, lo, hi in IN_PIECES}
    return dict(
        wqkv=piece["wqkv"], wz=piece["wz"],
        wpb=jnp.concatenate([piece["wpl"], jnp.pad(piece["wba"], ((0, 0), (0, 128 - 2 * HEADS)))], axis=1),
        wg=piece["wg"], wout=p["w_out"].astype(BF16),
        wupg=_cat([w_up[0].astype(BF16), w_up[1].astype(BF16)]), wupv=_cat([w_up[2].astype(BF16), w_up[3].astype(BF16)]),
        wdown=p["w_down"].astype(BF16),
        conv_qkv=p["conv_qkv"].astype(F32), conv_g=p["conv_ffn"][:, :D_FF].astype(F32),
        conv_v=p["conv_ffn"][:, D_FF:].astype(F32), w_pool=p["w_pool"].astype(F32),
        pool_scale=row(p["pool_scale"]), head_norm=row(p["head_norm"]), norm_mix=row(p["norm_mix"]),
        norm_ffn=row(p["norm_ffn"]), pa=_lanes_8_to_15(p["a_log"]), pdt=_lanes_8_to_15(p["dt_bias"]))


def layer_grads(g):
    return dict(
        w_in=jnp.concatenate([g["wqkv"], g["wz"], g["wba"][:, :2 * HEADS], g["wpl"], g["wg"]], axis=1),
        conv_qkv=g["conv_qkv"][:4], a_log=g["pa"][0, HEADS:2 * HEADS], dt_bias=g["pdt"][0, HEADS:2 * HEADS],
        head_norm=g["head_norm"][0], w_pool=g["w_pool"], pool_scale=g["pool_scale"][0], w_out=g["wout"],
        norm_mix=g["norm_mix"][0], norm_ffn=g["norm_ffn"][0],
        w_up=jnp.concatenate([g["wupg"], g["wupv"]], axis=1),
        conv_ffn=jnp.concatenate([g["conv_g"][:3], g["conv_v"][:3]], axis=1), w_down=g["wdown"])


def big_grad_shards(g):
    in_shards = [_cat([g[name][:, off:off + width] for name, off, width in
                       _overlaps(s * IN_SHARD, (s + 1) * IN_SHARD, IN_PIECES)]) for s in range(4)]
    half = D_FF // 2
    up_shards = [g["wupg"][:, :half], g["wupg"][:, half:], g["wupv"][:, :half], g["wupv"][:, half:]]
    return dict(w_in=jnp.stack(in_shards), w_up=jnp.stack(up_shards),
                w_down=g["wdown"].reshape(4, D_FF // 4, D_MODEL), w_out=g["wout"].reshape(4, D_MODEL // 4, D_MODEL))


LAYER_PARAMS = ("norm_mix", "w_in", "conv_qkv", "a_log", "dt_bias", "head_norm", "w_pool", "pool_scale", "w_out",
                "norm_ffn", "w_up", "conv_ffn", "w_down")


def pad_rows(meta, x):
    return jnp.concatenate([jnp.zeros((LEAD, D_MODEL), F32), meta.astype(F32), x.astype(F32),
                            jnp.zeros((TAIL, D_MODEL), F32)], axis=0)


MESH = pl.DeviceIdType.MESH
ANY = pl.BlockSpec(memory_space=pl.ANY)


def _place():
    x, y, c = lax.axis_index("x"), lax.axis_index("y"), lax.axis_index("c")
    return x, y, c, [(1 - x, y), (x, 1 - y), (1 - x, 1 - y)]


def _my_chip():
    return 2 * lax.axis_index("x") + lax.axis_index("y")


def gather_shards(packs):
    n = len(packs)

    def body(*refs):
        p_refs, o_refs, (send_sems, recv_sems) = refs[:n], refs[n:2 * n], refs[2 * n:]
        x, y, c, chips = _place()

        def copy(a, k, chip, half, to, src=None):
            dst = o_refs[a].at[2 * chip[0] + chip[1], half]
            return pltpu.make_async_remote_copy(src_ref=dst if src is None else src, dst_ref=dst,
                                                send_sem=send_sems.at[6 * a + k], recv_sem=recv_sems.at[6 * a + k],
                                                device_id=to, device_id_type=MESH)

        first = [copy(a, j, (x, y), c, (*chip, c), src=p_refs[a].at[c]) for a in range(n) for j, chip in enumerate(chips)]
        for cp in first:
            cp.start()
        passed = []
        for a in range(n):
            for j, chip in enumerate(chips):
                copy(a, j, chip, c, (x, y, c)).wait_recv()
                passed.append(copy(a, 3 + j, chip, c, (x, y, 1 - c)))
                passed[-1].start()
        for a in range(n):
            for j, chip in enumerate(chips):
                copy(a, 3 + j, chip, 1 - c, (x, y, c)).wait_recv()
        for cp in first + passed:
            cp.wait_send()

    gathered = pl.pallas_call(
        body, in_specs=[ANY] * n, out_specs=[ANY] * n,
        out_shape=[jax.ShapeDtypeStruct((4,) + p.shape, p.dtype) for p in packs],
        scratch_shapes=[pltpu.SemaphoreType.DMA((6 * n,)), pltpu.SemaphoreType.DMA((6 * n,))],
        name="gather_shards",
    )(*packs)
    me = _my_chip()
    return [lax.dynamic_update_slice(g, p[None], (me,) + (0,) * p.ndim) for g, p in zip(gathered, packs)]


def swap_other_halves(ps):
    n = len(ps)

    def body(*refs):
        p_refs, o_refs, (send_sems, recv_sems) = refs[:n], refs[n:2 * n], refs[2 * n:]
        x, y, c, _ = _place()
        copies = [pltpu.make_async_remote_copy(src_ref=p_refs[a].at[s, 1 - c], dst_ref=o_refs[a].at[s],
                                               send_sem=send_sems.at[4 * a + s], recv_sem=recv_sems.at[4 * a + s],
                                               device_id=(x, y, 1 - c), device_id_type=MESH)
                  for a in range(n) for s in range(4)]
        for cp in copies:
            cp.start()
        for cp in copies:
            cp.wait()

    return pl.pallas_call(
        body, in_specs=[ANY] * n, out_specs=[ANY] * n,
        out_shape=[jax.ShapeDtypeStruct((4,) + p.shape[2:], p.dtype) for p in ps],
        scratch_shapes=[pltpu.SemaphoreType.DMA((4 * n,)), pltpu.SemaphoreType.DMA((4 * n,))],
        name="swap_other_halves",
    )(*ps)


def scatter_to_chips(qs):
    n = len(qs)

    def body(*refs):
        q_refs, o_refs, (send_sems, recv_sems) = refs[:n], refs[n:2 * n], refs[2 * n:]
        x, y, c, chips = _place()
        me = 2 * x + y
        copies = [pltpu.make_async_remote_copy(src_ref=q_refs[a].at[2 * chip[0] + chip[1]], dst_ref=o_refs[a].at[me],
                                               send_sem=send_sems.at[3 * a + j], recv_sem=recv_sems.at[3 * a + j],
                                               device_id=(*chip, c), device_id_type=MESH)
                  for a in range(n) for j, chip in enumerate(chips)]
        for cp in copies:
            cp.start()
        for a in range(n):
            for j, chip in enumerate(chips):
                slot = o_refs[a].at[2 * chip[0] + chip[1]]
                pltpu.make_async_remote_copy(src_ref=slot, dst_ref=slot, send_sem=send_sems.at[3 * a + j],
                                             recv_sem=recv_sems.at[3 * a + j],
                                             device_id=(x, y, c), device_id_type=MESH).wait_recv()
        for cp in copies:
            cp.wait_send()

    received = pl.pallas_call(
        body, in_specs=[ANY] * n, out_specs=[ANY] * n,
        out_shape=[jax.ShapeDtypeStruct(q.shape, q.dtype) for q in qs],
        scratch_shapes=[pltpu.SemaphoreType.DMA((3 * n,)), pltpu.SemaphoreType.DMA((3 * n,))],
        name="scatter_to_chips",
    )(*qs)
    me = _my_chip()
    return [lax.dynamic_update_slice(r, lax.dynamic_slice_in_dim(q, me, 1, axis=0), (me, 0, 0))
            for r, q in zip(received, qs)]


def join_halves(boths):
    n = len(boths)

    def body(*refs):
        o_refs, (send_sems, recv_sems) = refs[n:2 * n], refs[2 * n:]
        x, y, c, _ = _place()
        copies = [pltpu.make_async_remote_copy(src_ref=o_refs[a].at[c], dst_ref=o_refs[a].at[c],
                                               send_sem=send_sems.at[a], recv_sem=recv_sems.at[a],
                                               device_id=(x, y, 1 - c), device_id_type=MESH) for a in range(n)]
        for cp in copies:
            cp.start()
        for a in range(n):
            other = o_refs[a].at[1 - c]
            pltpu.make_async_remote_copy(src_ref=other, dst_ref=other, send_sem=send_sems.at[a],
                                         recv_sem=recv_sems.at[a], device_id=(x, y, c), device_id_type=MESH).wait_recv()
        for cp in copies:
            cp.wait_send()

    return pl.pallas_call(
        body, in_specs=[ANY] * n, out_specs=[ANY] * n,
        out_shape=[jax.ShapeDtypeStruct(b.shape, b.dtype) for b in boths],
        input_output_aliases={a: a for a in range(n)},
        scratch_shapes=[pltpu.SemaphoreType.DMA((n,)), pltpu.SemaphoreType.DMA((n,))], name="join_halves",
    )(*boths)


def add_own_half(p, other, c, out_dtype, name):
    _, _, rows, lanes = p.shape
    tr = _tile(rows, max(16, 524288 // lanes), 16)

    def body(c_ref, p_ref, o_ref, out_ref):
        out_ref[...] = (p_ref[...] + o_ref[...]).astype(out_dtype)

    return pl.pallas_call(
        body,
        grid_spec=pltpu.PrefetchScalarGridSpec(
            num_scalar_prefetch=1, grid=(4, rows // tr),
            in_specs=[pl.BlockSpec((None, None, tr, lanes), lambda s, i, c_ref: (s, c_ref[0], i, 0)),
                      pl.BlockSpec((None, tr, lanes), lambda s, i, c_ref: (s, i, 0))],
            out_specs=pl.BlockSpec((None, tr, lanes), lambda s, i, c_ref: (s, i, 0))),
        out_shape=jax.ShapeDtypeStruct((4, rows, lanes), out_dtype),
        compiler_params=_params("parallel", "parallel"), name=name,
    )(c, p, other)


def sum_chips(b, c, name):
    _, rows, lanes = b.shape
    tr = _tile(rows, max(16, 524288 // lanes), 16)

    def body(c_ref, b_ref, out_ref):
        b0, b1, b2, b3 = (b_ref[k].astype(F32) for k in range(4))
        out_ref[...] = ((b0 + b1) + b2) + b3

    return pl.pallas_call(
        body,
        grid_spec=pltpu.PrefetchScalarGridSpec(
            num_scalar_prefetch=1, grid=(rows // tr,),
            in_specs=[pl.BlockSpec((4, tr, lanes), lambda i, c_ref: (0, i, 0))],
            out_specs=pl.BlockSpec((None, tr, lanes), lambda i, c_ref: (c_ref[0], i, 0))),
        out_shape=jax.ShapeDtypeStruct((2, rows, lanes), F32),
        compiler_params=_params("parallel"), name=name,
    )(c, b)


def all_reduce_to_shards(packs, wires, tags, c):
    others = swap_other_halves(packs)
    qs = [add_own_half(p, o, c, wire, f"add_own_half_{tag}") for p, o, wire, tag in zip(packs, others, wires, tags)]
    return join_halves([sum_chips(r, c, f"sum_chips_{tag}") for r, tag in zip(scatter_to_chips(qs), tags)])


def adamw(w, g, m, v, name):
    shape = w.shape
    cols = shape[-1]
    w2, g2, m2, v2 = (a.reshape(-1, cols) for a in (w, g, m, v))
    rows = w2.shape[0]
    tr = _tile(rows, max(8, 262144 // cols), 8) if rows % 8 == 0 else rows
    c1 = 1.0 - ADAM_B1 ** ADAM_STEP
    c2 = 1.0 - ADAM_B2 ** ADAM_STEP

    def body(w_ref, g_ref, m_ref, v_ref, d_ref, mo_ref, vo_ref):
        gv = g_ref[...]
        mn = ADAM_B1 * m_ref[...] + (1.0 - ADAM_B1) * gv
        vn = ADAM_B2 * v_ref[...] + (1.0 - ADAM_B2) * jnp.square(gv)
        d_ref[...] = -ADAM_LR * ((mn / c1) / (jnp.sqrt(vn / c2) + ADAM_EPS) + ADAM_WD * w_ref[...])
        mo_ref[...] = mn
        vo_ref[...] = vn

    spec = pl.BlockSpec((tr, cols), lambda i: (i, 0))
    out = jax.ShapeDtypeStruct((rows, cols), F32)
    d, mn, vn = pl.pallas_call(
        body, grid=(rows // tr,), in_specs=[spec] * 4, out_specs=(spec,) * 3, out_shape=(out,) * 3,
        compiler_params=_params("parallel"), name=name,
    )(w2, g2, m2, v2)
    return d.reshape(shape), mn.reshape(shape), vn.reshape(shape)


BIG = ("w_in", "w_up", "w_down", "w_out")
SMALL = ("w_pool", "conv_qkv", "conv_ffn", "meta_tokens")
SHARDED = BIG + SMALL
MATMUL_WEIGHTS = BIG + ("w_pool",)
REPLICATED = ("norm_mix", "a_log", "dt_bias", "head_norm", "pool_scale", "norm_ffn", "norm_final")
SHARD_AXIS = {"w_in": 2, "w_up": 2, "w_out": 1, "w_down": 1, "w_pool": 3, "conv_qkv": 2, "conv_ffn": 2, "meta_tokens": 1}


def _rows_of(a):
    return a.reshape(-1, 128)


SEGMENT_ROWS = 16


def _segment(n_rows):
    return -(-n_rows // SEGMENT_ROWS) * SEGMENT_ROWS


def _pad_segment(a):
    pad = [(0, 0)] * a.ndim
    pad[-2] = (0, _segment(a.shape[-2]) - a.shape[-2])
    return jnp.pad(a, pad)


def _unshard(stacked, axis):
    full = jnp.moveaxis(stacked, 0, axis)
    shape = list(full.shape)
    shape[axis:axis + 2] = [shape[axis] * shape[axis + 1]]
    return full.reshape(shape)


def _shard_stack(full, axis):
    shape = list(full.shape)
    shape[axis:axis + 1] = [4, shape[axis] // 4]
    return jnp.moveaxis(full.reshape(shape), axis, 0)


def pack_weights(shards):
    parts = [_rows_of(shards[k].astype(WIRE)) if k in MATMUL_WEIGHTS else
             lax.bitcast_convert_type(_rows_of(shards[k].astype(F32)), WIRE).reshape(-1, 128) for k in SMALL]
    parts = [_pad_segment(p) for p in parts]
    rows = sum(p.shape[0] for p in parts)
    if rows % (2 * SEGMENT_ROWS):
        parts.append(jnp.zeros((SEGMENT_ROWS, 128), WIRE))
        rows += SEGMENT_ROWS
    return [shards[k].astype(WIRE) for k in BIG] + [jnp.concatenate(parts, axis=0).reshape(2, rows // 2, 128)]


def unpack_weights(gathered, shard_shapes):
    out = {k: _unshard(g, SHARD_AXIS[k]) for k, g in zip(BIG, gathered)}
    flat = gathered[-1].reshape(4, -1, 128)
    at = 0
    for k in SMALL:
        shp = shard_shapes[k]
        n = 1
        for e in shp:
            n *= e
        if k in MATMUL_WEIGHTS:
            r = n // 128
            stacked = flat[:, at:at + r].reshape((4,) + tuple(shp))
        else:
            r = 2 * n // 128
            stacked = lax.bitcast_convert_type(flat[:, at:at + r].reshape(4, n // 128, 128, 2), F32).reshape((4,) + tuple(shp))
        out[k] = _unshard(stacked, SHARD_AXIS[k])
        at += _segment(r)
    return out


def pack_grads(big, full, repl):
    r = jnp.concatenate([repl[k].reshape(-1) for k in REPLICATED])
    r = jnp.pad(r, (0, -r.shape[0] % 128)).reshape(1, -1, 128)
    parts = [_shard_stack(full[k], SHARD_AXIS[k]).reshape(4, -1, 128) for k in SMALL]
    parts = [_pad_segment(p) for p in parts + [jnp.broadcast_to(r, (4,) + r.shape[1:])]]
    rows = sum(p.shape[1] for p in parts)
    if rows % (2 * SEGMENT_ROWS):
        parts.append(jnp.zeros((4, SEGMENT_ROWS, 128), F32))
        rows += SEGMENT_ROWS
    side = jnp.concatenate(parts, axis=1).reshape(4, 2, rows // 2, 128)
    return list(big) + [side]


def unpack_grads(reduced, shard_shapes, repl_shapes):
    out = dict(zip(BIG, reduced))
    side = reduced[-1].reshape(-1, 128)
    at = 0
    for k in SMALL:
        n = 1
        for e in shard_shapes[k]:
            n *= e
        out[k] = side[at:at + n // 128].reshape(shard_shapes[k])
        at += _segment(n // 128)
    r = side[at:].reshape(-1)
    at = 0
    for k in REPLICATED:
        n = 1
        for e in repl_shapes[k]:
            n *= e
        out[k] = r[at:at + n].reshape(repl_shapes[k])
        at += n
    return out


WEIGHT_ORDER = ("meta_tokens", "norm_mix", "w_in", "conv_qkv", "a_log", "dt_bias", "head_norm", "w_pool", "pool_scale",
                "w_out", "norm_ffn", "w_up", "conv_ffn", "w_down", "norm_final")


def kernel(x, meta_tokens, norm_mix, w_in, conv_qkv, a_log, dt_bias, head_norm, w_pool, pool_scale, w_out, norm_ffn, w_up, conv_ffn, w_down, norm_final, loss_target, m_meta_tokens, m_norm_mix, m_w_in, m_conv_qkv, m_a_log, m_dt_bias, m_head_norm, m_w_pool, m_pool_scale, m_w_out, m_norm_ffn, m_w_up, m_conv_ffn, m_w_down, m_norm_final, v_meta_tokens, v_norm_mix, v_w_in, v_conv_qkv, v_a_log, v_dt_bias, v_head_norm, v_w_pool, v_pool_scale, v_w_out, v_norm_ffn, v_w_up, v_conv_ffn, v_w_down, v_norm_final):
    weights = dict(meta_tokens=meta_tokens, norm_mix=norm_mix, w_in=w_in, conv_qkv=conv_qkv, a_log=a_log,
                   dt_bias=dt_bias, head_norm=head_norm, w_pool=w_pool, pool_scale=pool_scale, w_out=w_out,
                   norm_ffn=norm_ffn, w_up=w_up, conv_ffn=conv_ffn, w_down=w_down, norm_final=norm_final)
    m_in = dict(zip(WEIGHT_ORDER, (m_meta_tokens, m_norm_mix, m_w_in, m_conv_qkv, m_a_log, m_dt_bias, m_head_norm,
                                   m_w_pool, m_pool_scale, m_w_out, m_norm_ffn, m_w_up, m_conv_ffn, m_w_down, m_norm_final)))
    v_in = dict(zip(WEIGHT_ORDER, (v_meta_tokens, v_norm_mix, v_w_in, v_conv_qkv, v_a_log, v_dt_bias, v_head_norm,
                                   v_w_pool, v_pool_scale, v_w_out, v_norm_ffn, v_w_up, v_conv_ffn, v_w_down, v_norm_final)))
    shard_shapes = {k: weights[k].shape for k in SHARDED}
    repl_shapes = {k: weights[k].shape for k in REPLICATED}
    core = lax.axis_index("c").astype(jnp.int32).reshape(1)

    gathered = gather_shards(pack_weights({k: weights[k] for k in SHARDED}))
    full = unpack_weights(gathered, shard_shapes)
    shards = dict(zip(BIG, gathered))
    layers = []
    for li in range(DEPTH):
        p = {k: (full[k][li] if k in SMALL else weights[k][li]) for k in LAYER_PARAMS if k not in BIG}
        p.update(w_in=[shards["w_in"][s, li] for s in range(4)], w_up=[shards["w_up"][s, li] for s in range(4)],
                 w_down=shards["w_down"][:, li].reshape(D_FF, D_MODEL),
                 w_out=shards["w_out"][:, li].reshape(D_MODEL, D_MODEL))
        layers.append(prep_layer(p))

    h0 = pad_rows(full["meta_tokens"], x[0])
    target = pad_rows(jnp.zeros((N_META, D_MODEL), F32), loss_target[0])
    loss, dh0, grads, d_norm_final = local_step(h0, target, layers, norm_final.reshape(1, D_MODEL))
    seq = x.shape[1]
    grad_x = dh0[LEAD + N_META:LEAD + N_META + seq][None]

    per_layer = [layer_grads(g) for g in grads]
    g_all = {k: jnp.stack([pl_[k] for pl_ in per_layer]) for k in LAYER_PARAMS if k not in BIG}
    g_all["meta_tokens"] = dh0[LEAD:LEAD + N_META]
    g_all["norm_final"] = d_norm_final[0]
    big = [big_grad_shards(g) for g in grads]
    packs = pack_grads([jnp.stack([b[k] for b in big], axis=1) for k in BIG],
                       {k: g_all[k] for k in SMALL}, {k: g_all[k] for k in REPLICATED})
    reduced = all_reduce_to_shards(packs, [GRAD_WIRE] * len(BIG) + [F32], BIG + ("side",), core)
    g_mine = unpack_grads(reduced, shard_shapes, repl_shapes)

    loss_sum = lax.psum(loss[0, 0], ("x", "y", "c"))
    deltas, new_m, new_v = {}, {}, {}
    for k in WEIGHT_ORDER:
        deltas[k], new_m[k], new_v[k] = adamw(weights[k], g_mine[k], m_in[k], v_in[k], f"adamw_{k}")
    return (loss_sum, grad_x, *[g_mine[k] for k in WEIGHT_ORDER], *[deltas[k] for k in WEIGHT_ORDER],
            *[new_m[k] for k in WEIGHT_ORDER], *[new_v[k] for k in WEIGHT_ORDER])
```

```python
import functools

import jax
import jax.numpy as jnp
from jax import lax
from jax.experimental import pallas as pl
from jax.experimental.pallas import tpu as pltpu

F32 = jnp.float32
BF16 = jnp.bfloat16
WIRE = jnp.bfloat16
GRAD_WIRE = jnp.bfloat16

D_MODEL = 1024
HEADS = 8
HEAD_DIM = 128
CHUNK = 64
N_META = 16
LEAD = 48
TAIL = 64
QKV_DIM = 3072
D_FF = 2816
POOL_WIDTH = 512
POOL_WINDOWS = (2, 4, 8, 16)
BA_BLOCK = POOL_WIDTH // 128
DEPTH = 2
NORM_EPS = 1e-6
ADAM_LR, ADAM_B1, ADAM_B2, ADAM_EPS, ADAM_WD, ADAM_STEP = 0.001, 0.9, 0.999, 1e-08, 0.01, 10
VMEM_LIMIT_BYTES = 48 * 1024 * 1024


def _params(*sem):
    return pltpu.CompilerParams(dimension_semantics=sem if sem else None, vmem_limit_bytes=VMEM_LIMIT_BYTES)


def _tile(n, cap, mult):
    best = None
    for t in range(mult, min(n, cap) + 1, mult):
        if n % t == 0:
            best = t
    assert best is not None, (n, cap, mult)
    return best


def _silu(x):
    return x * jax.nn.sigmoid(x)


def _softplus(x):
    return jnp.maximum(x, 0.0) + jnp.log(1.0 + jnp.exp(-jnp.abs(x)))


def _split_bf16(a):
    hi = a.astype(BF16)
    return hi, (a - hi.astype(F32)).astype(BF16)


def _dg(a, b, ca, cb, hi):
    dims = (((ca,), (cb,)), ((), ()))
    if hi is True:
        return lax.dot_general(a, b, dims, precision=lax.Precision.HIGHEST, preferred_element_type=F32)
    if hi == 3:
        (ah, al), (bh, bl) = _split_bf16(a), _split_bf16(b)
        dot = lambda x, y: lax.dot_general(x, y, dims, preferred_element_type=F32)
        return dot(ah, bh) + (dot(ah, bl) + dot(al, bh))
    return lax.dot_general(a.astype(BF16), b.astype(BF16), dims, preferred_element_type=F32)


def _make_dots(hi):
    @jax.custom_vjp
    def nn(a, b):
        return _dg(a, b, 1, 0, hi)

    @jax.custom_vjp
    def nt(a, b):
        return _dg(a, b, 1, 1, hi)

    @jax.custom_vjp
    def tn(a, b):
        return _dg(a, b, 0, 0, hi)

    nn.defvjp(lambda a, b: (nn(a, b), (a, b)), lambda r, g: (nt(g, r[1]), tn(r[0], g)))
    nt.defvjp(lambda a, b: (nt(a, b), (a, b)), lambda r, g: (nn(g, r[1]), tn(g, r[0])))
    tn.defvjp(lambda a, b: (tn(a, b), (a, b)), lambda r, g: (nt(r[1], g), nn(r[0], g)))
    return nn, nt, tn


_nn, _nt, _tn = _make_dots(False)
_hnn, _hnt, _htn = _make_dots(True)


def _neumann(a):
    n = a[0].shape[0]
    eye = (lax.broadcasted_iota(jnp.int32, (n, n), 0) == lax.broadcasted_iota(jnp.int32, (n, n), 1)).astype(F32)
    hd = range(len(a))
    p = [_dg(a[h], a[h], 1, 0, 3) for h in hd]
    x = [(eye - a[h]) + p[h] - _dg(a[h], p[h], 1, 0, False) for h in hd]
    for _ in range(4):
        p = [_dg(p[h], p[h], 1, 0, False) for h in hd]
        x = [x[h] + p[h] + _dg(x[h] - eye, p[h], 1, 0, False) for h in hd]
    return tuple(x)


@jax.custom_vjp
def _inv_unit_lower(a):
    return _neumann(a)


def _inv_unit_lower_bwd(x, g):
    t = [_dg(x[h], g[h], 0, 0, 3) for h in range(len(x))]
    return (tuple(-_dg(t[h], x[h], 1, 1, 3) for h in range(len(x))),)


_inv_unit_lower.defvjp(lambda a: (_neumann(a),) * 2, _inv_unit_lower_bwd)


@jax.custom_vjp
def _kept_inverse(a, x):
    return x


_kept_inverse.defvjp(lambda a, x: (x, x),
                     lambda x, g: _inv_unit_lower_bwd(x, g) + (tuple(jnp.zeros_like(e) for e in x),))


def mm(a, b, *, tb=False, add=None, norm_gain=None, out_dtype=F32, name):
    m, kdim = a.shape
    (n, kb) = b.shape if tb else b.shape[::-1]
    assert kdim == kb, (a.shape, b.shape, tb)
    tm = _tile(m, 1408, 128) if m % 128 == 0 and m <= 4096 else _tile(m, 640, 64)
    tn = _tile(n, 3072 if (kdim <= 1024 and not tb and add is None) else 1536, 128)
    tk = kdim if kdim <= 3072 else _tile(kdim, 1664 if a.dtype == b.dtype == BF16 else 640, 128)
    nk = kdim // tk
    dims = (((1,), (1 if tb else 0,)), ((), ()))

    normed = norm_gain is not None
    assert not normed or (tn == n and nk == 1 and tm % 128 == 0), (name, tm, tn, nk)

    def body(*refs):
        refs = list(refs)
        a_ref, b_ref = refs[0], refs[1]
        add_ref = refs.pop(2) if add is not None else None
        gain_ref = refs.pop(2) if normed else None
        o_ref, acc = refs[2], refs[-1]
        k = pl.program_id(2)
        part = lax.dot_general(a_ref[...].astype(BF16), b_ref[...].astype(BF16), dims, preferred_element_type=F32)

        def finish(r):
            if add is not None:
                r = r + add_ref[...]
            o_ref[...] = r.astype(out_dtype)
            if normed:
                u = _rms(r, gain_ref[...])
                refs[3][...] = u.astype(BF16)
                refs[4][...] = u.T.astype(BF16)

        if nk == 1:
            finish(part)
        else:
            @pl.when(k == 0)
            def _():
                acc[...] = part

            @pl.when(jnp.logical_and(k > 0, k < nk - 1))
            def _():
                acc[...] += part

            @pl.when(k == nk - 1)
            def _():
                finish(acc[...] + part)

    a_spec = pl.BlockSpec((tm, tk), lambda j, i, k: (i, k))
    b_spec = pl.BlockSpec((tn, tk), lambda j, i, k: (j, k)) if tb else pl.BlockSpec((tk, tn), lambda j, i, k: (k, j))
    in_specs = [a_spec, b_spec]
    args = [a, b]
    if add is not None:
        in_specs.append(pl.BlockSpec((tm, tn), lambda j, i, k: (i, j)))
        args.append(add)
    out_specs = pl.BlockSpec((tm, tn), lambda j, i, k: (i, j))
    out_shape = jax.ShapeDtypeStruct((m, n), out_dtype)
    if normed:
        in_specs.append(pl.BlockSpec((1, n), lambda j, i, k: (0, 0)))
        args.append(norm_gain)
        out_specs = (out_specs, out_specs, pl.BlockSpec((n, tm), lambda j, i, k: (0, i)))
        out_shape = (out_shape, jax.ShapeDtypeStruct((m, n), BF16), jax.ShapeDtypeStruct((n, m), BF16))
    return pl.pallas_call(
        body, grid=(n // tn, m // tm, nk), in_specs=in_specs, out_specs=out_specs, out_shape=out_shape,
        scratch_shapes=[pltpu.VMEM((tm, tn) if nk > 1 else (8, 128), F32)],
        compiler_params=_params("parallel", "parallel", "arbitrary"), name=name,
    )(*args)


def mm_nt_sum(pairs, name):
    m, n = pairs[0][0].shape[0], pairs[0][1].shape[0]
    tm = _tile(m, 320, 64)
    count = len(pairs)

    def body(*refs):
        o_ref = refs[2 * count]
        total = None
        for p in range(count):
            part = lax.dot_general(refs[2 * p][...].astype(BF16), refs[2 * p + 1][...].astype(BF16),
                                   (((1,), (1,)), ((), ())), preferred_element_type=F32)
            total = part if total is None else total + part
        o_ref[...] = total

    in_specs, args = [], []
    for a, b in pairs:
        assert a.shape == (m, b.shape[1]) and b.shape[0] == n, (a.shape, b.shape)
        in_specs += [pl.BlockSpec((tm, a.shape[1]), lambda i: (i, 0)), pl.BlockSpec(b.shape, lambda i: (0, 0))]
        args += [a, b]
    return pl.pallas_call(
        body, grid=(m // tm,), in_specs=in_specs, out_specs=pl.BlockSpec((tm, n), lambda i: (i, 0)),
        out_shape=jax.ShapeDtypeStruct((m, n), F32), compiler_params=_params("parallel"), name=name,
    )(*args)


def _rms(x, gain):
    return x * lax.rsqrt(jnp.mean(x * x, axis=-1, keepdims=True) + NORM_EPS) * gain


def rms_fwd(h, gain, name):
    t = h.shape[0]
    ts = _tile(t, 640, 128)

    def body(h_ref, g_ref, u_ref, ut_ref):
        u = _rms(h_ref[...], g_ref[...])
        u_ref[...] = u.astype(BF16)
        ut_ref[...] = u.T.astype(BF16)

    return pl.pallas_call(
        body, grid=(t // ts,),
        in_specs=[pl.BlockSpec((ts, D_MODEL), lambda i: (i, 0)), pl.BlockSpec((1, D_MODEL), lambda i: (0, 0))],
        out_specs=(pl.BlockSpec((ts, D_MODEL), lambda i: (i, 0)), pl.BlockSpec((D_MODEL, ts), lambda i: (0, i))),
        out_shape=(jax.ShapeDtypeStruct((t, D_MODEL), BF16), jax.ShapeDtypeStruct((D_MODEL, t), BF16)),
        compiler_params=_params("parallel"), name=name,
    )(h, gain)


def rms_bwd(h, gain, du, dres, name):
    t = h.shape[0]
    ts = _tile(t, 640, 64)

    def body(h_ref, g_ref, du_ref, dres_ref, dh_ref, dhb_ref, dg_ref):
        i = pl.program_id(0)
        _, vjp = jax.vjp(_rms, h_ref[...], g_ref[...])
        dx, dg = vjp(du_ref[...])
        row = i * ts + lax.broadcasted_iota(jnp.int32, (ts, 1), 0)
        dh = jnp.where(row >= LEAD, dx + dres_ref[...], 0.0)
        dh_ref[...] = dh
        dhb_ref[...] = dh.astype(BF16)

        @pl.when(i == 0)
        def _():
            dg_ref[...] = jnp.zeros_like(dg_ref)

        dg_ref[...] += dg

    row_spec = pl.BlockSpec((ts, D_MODEL), lambda i: (i, 0))
    vec_spec = pl.BlockSpec((1, D_MODEL), lambda i: (0, 0))
    return pl.pallas_call(
        body, grid=(t // ts,), in_specs=[row_spec, vec_spec, row_spec, row_spec],
        out_specs=(row_spec, row_spec, vec_spec),
        out_shape=(jax.ShapeDtypeStruct((t, D_MODEL), F32), jax.ShapeDtypeStruct((t, D_MODEL), BF16),
                   jax.ShapeDtypeStruct((1, D_MODEL), F32)),
        compiler_params=_params("arbitrary"), name=name,
    )(h, gain, du, dres)


def loss_head(h, gain, target, name):
    t = h.shape[0]
    ts = _tile(t, 640, 64)

    def body(h_ref, g_ref, t_ref, loss_ref, dh_ref, dhb_ref, dg_ref):
        i = pl.program_id(0)
        row = i * ts + lax.broadcasted_iota(jnp.int32, (ts, 1), 0)
        keep = jnp.logical_and(row >= LEAD + N_META, row < t - TAIL)
        tgt = t_ref[...]

        def f(x, g):
            err = jnp.where(keep, _rms(x, g) - tgt, 0.0)
            per_row = jnp.mean(err * err, axis=-1, keepdims=True)
            return 0.5 * jnp.sum(per_row, axis=0, keepdims=True)

        val, vjp = jax.vjp(f, h_ref[...], g_ref[...])
        dx, dg = vjp(jnp.ones((1, 1), F32))
        dh_ref[...] = dx
        dhb_ref[...] = dx.astype(BF16)

        @pl.when(i == 0)
        def _():
            dg_ref[...] = jnp.zeros_like(dg_ref)
            loss_ref[...] = jnp.zeros_like(loss_ref)

        dg_ref[...] += dg
        loss_ref[...] += jnp.broadcast_to(val, (1, 128))

    row_spec = pl.BlockSpec((ts, D_MODEL), lambda i: (i, 0))
    vec_spec = pl.BlockSpec((1, D_MODEL), lambda i: (0, 0))
    return pl.pallas_call(
        body, grid=(t // ts,), in_specs=[row_spec, vec_spec, row_spec],
        out_specs=(pl.BlockSpec((1, 128), lambda i: (0, 0)), row_spec, row_spec, vec_spec),
        out_shape=(jax.ShapeDtypeStruct((1, 128), F32), jax.ShapeDtypeStruct((t, D_MODEL), F32),
                   jax.ShapeDtypeStruct((t, D_MODEL), BF16), jax.ShapeDtypeStruct((1, D_MODEL), F32)),
        compiler_params=_params("arbitrary"), name=name,
    )(h, gain, target)


def _rows_down(a, s):
    return a if s == 0 else pltpu.roll(a, s, axis=0)


def _rows_up(a, s):
    return a if s == 0 else pltpu.roll(a, a.shape[0] - s, axis=0)


def conv_fwd(x, w, name):
    t, width = x.shape
    k = w.shape[0]
    ts = _tile(t, 640, 64)
    tw = _tile(width, 1536, 128)
    hb = ts // 8

    def body(x_ref, halo_ref, w_ref, o_ref, buf):
        i = pl.program_id(0)
        buf[0:8, :] = jnp.where(i > 0, halo_ref[...], 0.0)
        buf[8:, :] = x_ref[...]
        ext = buf[...]
        wv = w_ref[...]
        acc = _rows_down(ext, k - 1)[8:, :] * wv[0:1, :]
        for j in range(1, k):
            acc = acc + _rows_down(ext, k - 1 - j)[8:, :] * wv[j:j + 1, :]
        o_ref[...] = acc

    return pl.pallas_call(
        body, grid=(t // ts, width // tw),
        in_specs=[pl.BlockSpec((ts, tw), lambda i, j: (i, j)),
                  pl.BlockSpec((8, tw), lambda i, j: (jnp.maximum(i * hb - 1, 0), j)),
                  pl.BlockSpec((k, tw), lambda i, j: (0, j))],
        out_specs=pl.BlockSpec((ts, tw), lambda i, j: (i, j)),
        out_shape=jax.ShapeDtypeStruct((t, width), F32),
        scratch_shapes=[pltpu.VMEM((ts + 8, tw), F32)],
        compiler_params=_params("parallel", "parallel"), name=name,
    )(x, x, w)


def conv_bwd(x, dc, w, name):
    t, width = x.shape
    k = w.shape[0]
    ts = _tile(t, 640, 64)
    tw = _tile(width, 1536, 128)
    hb = ts // 8
    nt = t // ts

    def body(x_ref, xh_ref, dc_ref, dch_ref, w_ref, dx_ref, dw_ref, xbuf, dbuf):
        i = pl.program_id(1)
        xbuf[0:8, :] = jnp.where(i > 0, xh_ref[...], 0.0)
        xbuf[8:, :] = x_ref[...]
        d = dc_ref[...]
        dbuf[0:ts, :] = d
        dbuf[ts:, :] = jnp.where(i < nt - 1, dch_ref[...], 0.0)
        wv = w_ref[...]
        ext_x, ext_d = xbuf[...], dbuf[...]
        acc = _rows_up(ext_d, k - 1)[0:ts, :] * wv[0:1, :]
        for j in range(1, k):
            acc = acc + _rows_up(ext_d, k - 1 - j)[0:ts, :] * wv[j:j + 1, :]
        dx_ref[...] = acc.astype(BF16)

        @pl.when(i == 0)
        def _():
            dw_ref[...] = jnp.zeros_like(dw_ref)

        for j in range(k):
            dw_ref[j:j + 1, :] += jnp.sum(d * _rows_down(ext_x, k - 1 - j)[8:, :], axis=0, keepdims=True)

    return pl.pallas_call(
        body, grid=(width // tw, nt),
        in_specs=[pl.BlockSpec((ts, tw), lambda j, i: (i, j)),
                  pl.BlockSpec((8, tw), lambda j, i: (jnp.maximum(i * hb - 1, 0), j)),
                  pl.BlockSpec((ts, tw), lambda j, i: (i, j)),
                  pl.BlockSpec((8, tw), lambda j, i: (jnp.minimum((i + 1) * hb, t // 8 - 1), j)),
                  pl.BlockSpec((k, tw), lambda j, i: (0, j))],
        out_specs=(pl.BlockSpec((ts, tw), lambda j, i: (i, j)), pl.BlockSpec((8, tw), lambda j, i: (0, j))),
        out_shape=(jax.ShapeDtypeStruct((t, width), BF16), jax.ShapeDtypeStruct((8, width), F32)),
        scratch_shapes=[pltpu.VMEM((ts + 8, tw), F32), pltpu.VMEM((ts + 8, tw), F32)],
        compiler_params=_params("parallel", "arbitrary"), name=name,
    )(x, x, dc, dc, w)


def _pool_count(pos, win):
    return jnp.clip(pos + 1, 1, win).astype(F32)


def poolwin_fwd(p, name):
    t = p.shape[0]
    ts = _tile(t, 640, 64)
    hb = ts // 16

    def body(p_ref, halo_ref, o_ref, buf):
        i = pl.program_id(0)
        buf[0:16, :] = jnp.where(i > 0, halo_ref[...], 0.0)
        buf[16:, :] = p_ref[...]
        pos = i * ts + lax.broadcasted_iota(jnp.int32, (ts, 1), 0) - LEAD
        ext = buf[...]
        own = ext[16:, :]
        sums, span = ext, 1
        for gi, win in enumerate(POOL_WINDOWS):
            while span < win:
                sums = sums + _rows_down(sums, span)
                span *= 2
            cols = slice(gi * 128, (gi + 1) * 128)
            o_ref[:, cols] = sums[16:, cols] / _pool_count(pos, win) - own[:, cols]

    return pl.pallas_call(
        body, grid=(t // ts,),
        in_specs=[pl.BlockSpec((ts, POOL_WIDTH), lambda i: (i, 0)),
                  pl.BlockSpec((16, POOL_WIDTH), lambda i: (jnp.maximum(i * hb - 1, 0), 0))],
        out_specs=pl.BlockSpec((ts, POOL_WIDTH), lambda i: (i, 0)),
        out_shape=jax.ShapeDtypeStruct((t, POOL_WIDTH), F32),
        scratch_shapes=[pltpu.VMEM((ts + 16, POOL_WIDTH), F32)],
        compiler_params=_params("parallel"), name=name,
    )(p, p)


def poolwin_bwd(dpooled, name):
    t = dpooled.shape[0]
    ts = _tile(t, 640, 64)
    hb = ts // 16
    nt = t // ts

    def body(d_ref, halo_ref, o_ref, buf):
        i = pl.program_id(0)
        buf[0:ts, :] = d_ref[...]
        buf[ts:, :] = jnp.where(i < nt - 1, halo_ref[...], 0.0)
        pos = i * ts + lax.broadcasted_iota(jnp.int32, (ts + 16, 1), 0) - LEAD
        ext = buf[...]
        for gi, win in enumerate(POOL_WINDOWS):
            cols = slice(gi * 128, (gi + 1) * 128)
            sums, span = ext[:, cols] / _pool_count(pos, win), 1
            while span < win:
                sums = sums + _rows_up(sums, span)
                span *= 2
            o_ref[:, cols] = (sums[0:ts, :] - ext[0:ts, cols]).astype(BF16)

    return pl.pallas_call(
        body, grid=(nt,),
        in_specs=[pl.BlockSpec((ts, POOL_WIDTH), lambda i: (i, 0)),
                  pl.BlockSpec((16, POOL_WIDTH), lambda i: (jnp.minimum((i + 1) * hb, t // 16 - 1), 0))],
        out_specs=pl.BlockSpec((ts, POOL_WIDTH), lambda i: (i, 0)),
        out_shape=jax.ShapeDtypeStruct((t, POOL_WIDTH), BF16),
        scratch_shapes=[pltpu.VMEM((ts + 16, POOL_WIDTH), F32)],
        compiler_params=_params("parallel"), name=name,
    )(dpooled, dpooled)


def _mix(y_a, gpre, pooled, w_pool, scale):
    parts = [_nn(pooled[:, g * 128:(g + 1) * 128], w_pool[g]) for g in range(4)]
    y_b = jnp.concatenate(parts, axis=1) * scale
    return jax.nn.sigmoid(gpre[:, :D_MODEL]) * y_a + jax.nn.sigmoid(gpre[:, D_MODEL:]) * y_b


def _mix_specs(ts):
    return [pl.BlockSpec((ts, D_MODEL), lambda i: (i, 0)), pl.BlockSpec((ts, 2 * D_MODEL), lambda i: (i, 0)),
            pl.BlockSpec((ts, POOL_WIDTH), lambda i: (i, 0)), pl.BlockSpec((4, 128, 256), lambda i: (0, 0, 0)),
            pl.BlockSpec((1, D_MODEL), lambda i: (0, 0))]


def mix_fwd(y_a, gpre, pooled, w_pool, scale, name):
    t = y_a.shape[0]
    ts = _tile(t, 640, 128)

    def body(ya_ref, g_ref, p_ref, w_ref, s_ref, o_ref, ot_ref):
        y = _mix(ya_ref[...], g_ref[...], p_ref[...], w_ref[...], s_ref[...])
        o_ref[...] = y.astype(BF16)
        ot_ref[...] = y.T.astype(BF16)

    return pl.pallas_call(
        body, grid=(t // ts,), in_specs=_mix_specs(ts),
        out_specs=(pl.BlockSpec((ts, D_MODEL), lambda i: (i, 0)), pl.BlockSpec((D_MODEL, ts), lambda i: (0, i))),
        out_shape=(jax.ShapeDtypeStruct((t, D_MODEL), BF16), jax.ShapeDtypeStruct((D_MODEL, t), BF16)),
        compiler_params=_params("parallel"), name=name,
    )(y_a, gpre, pooled, w_pool, scale)


def mix_bwd(y_a, gpre, pooled, w_pool, scale, dy, name):
    t = y_a.shape[0]
    ts = _tile(t, 320, 64)

    def body(ya_ref, g_ref, p_ref, w_ref, s_ref, dy_ref, dya_ref, dg_ref, dp_ref, dw_ref, ds_ref):
        i = pl.program_id(0)
        _, vjp = jax.vjp(_mix, ya_ref[...], g_ref[...], p_ref[...], w_ref[...], s_ref[...])
        dya, dg, dp, dw, ds = vjp(dy_ref[...])
        dya_ref[...] = dya
        dg_ref[...] = dg.astype(BF16)
        dp_ref[...] = dp

        @pl.when(i == 0)
        def _():
            dw_ref[...] = jnp.zeros_like(dw_ref)
            ds_ref[...] = jnp.zeros_like(ds_ref)

        dw_ref[...] += dw
        ds_ref[...] += ds

    specs = _mix_specs(ts)
    return pl.pallas_call(
        body, grid=(t // ts,), in_specs=specs + [specs[0]],
        out_specs=(specs[0], specs[1], specs[2], specs[3], specs[4]),
        out_shape=(jax.ShapeDtypeStruct((t, D_MODEL), F32), jax.ShapeDtypeStruct((t, 2 * D_MODEL), BF16),
                   jax.ShapeDtypeStruct((t, POOL_WIDTH), F32), jax.ShapeDtypeStruct((4, 128, 256), F32),
                   jax.ShapeDtypeStruct((1, D_MODEL), F32)),
        compiler_params=_params("arbitrary"), name=name,
    )(y_a, gpre, pooled, w_pool, scale, dy)


def _ffn_act(cg, cv):
    return _silu(cg) * cv


FFN_MID_VMEM_BYTES = 58 * 1024 * 1024


def ffn_mid_fwd(hg, hv, wg, wv, name):
    t, width = hg.shape
    k = wg.shape[0]
    ts = _tile(t, 640, 128)
    tw = _tile(width, 1536, 128)
    hb = ts // 8

    def body(hg_ref, hgp_ref, hv_ref, hvp_ref, wg_ref, wv_ref, cg_ref, cv_ref, act_ref, actt_ref, xg, xv):
        i = pl.program_id(0)
        convs = []
        for x_ref, xp_ref, buf, w_ref, c_ref in ((hg_ref, hgp_ref, xg, wg_ref, cg_ref),
                                                 (hv_ref, hvp_ref, xv, wv_ref, cv_ref)):
            buf[0:8, :] = jnp.where(i > 0, xp_ref[...], 0.0)
            buf[8:, :] = x_ref[...]
            ext, wt = buf[...], w_ref[...]
            acc = _rows_down(ext, k - 1)[8:, :] * wt[0:1, :]
            for j in range(1, k):
                acc = acc + _rows_down(ext, k - 1 - j)[8:, :] * wt[j:j + 1, :]
            c_ref[...] = acc
            convs.append(acc)
        act = _ffn_act(*convs)
        act_ref[...] = act.astype(BF16)
        actt_ref[...] = act.T.astype(BF16)

    tile = pl.BlockSpec((ts, tw), lambda i, j: (i, j))
    prev = pl.BlockSpec((8, tw), lambda i, j: (jnp.maximum(i * hb - 1, 0), j))
    taps = pl.BlockSpec((k, tw), lambda i, j: (0, j))
    return pl.pallas_call(
        body, grid=(t // ts, width // tw), in_specs=[tile, prev, tile, prev, taps, taps],
        out_specs=(tile, tile, tile, pl.BlockSpec((tw, ts), lambda i, j: (j, i))),
        out_shape=(jax.ShapeDtypeStruct((t, width), F32), jax.ShapeDtypeStruct((t, width), F32),
                   jax.ShapeDtypeStruct((t, width), BF16), jax.ShapeDtypeStruct((width, t), BF16)),
        scratch_shapes=[pltpu.VMEM((ts + 8, tw), F32)] * 2,
        compiler_params=pltpu.CompilerParams(dimension_semantics=("parallel", "parallel"),
                                             vmem_limit_bytes=FFN_MID_VMEM_BYTES), name=name,
    )(hg, hg, hv, hv, wg, wv)


def ffn_mid_bwd(hg, hv, cg, cv, dact, wg, wv, name):
    t, width = hg.shape
    k = wg.shape[0]
    ts = _tile(t, 320, 64)
    tw = _tile(width, 1536, 128)
    hb = ts // 8
    nt = t // ts

    def body(hg_ref, hgp_ref, hv_ref, hvp_ref, cg_ref, cgn_ref, cv_ref, cvn_ref, da_ref, dan_ref, wg_ref, wv_ref,
             dhg_ref, dhv_ref, dwg_ref, dwv_ref, xg, xv, dg, dv):
        i = pl.program_id(1)
        behind = i < nt - 1

        def d_conv(c_g, c_v, d_a):
            _, vjp = jax.vjp(_ffn_act, c_g, c_v)
            return vjp(d_a)

        dcg, dcv = d_conv(cg_ref[...], cv_ref[...], da_ref[...])
        dcg_n, dcv_n = d_conv(cgn_ref[...], cvn_ref[...], jnp.where(behind, dan_ref[...], 0.0))

        @pl.when(i == 0)
        def _():
            dwg_ref[...] = jnp.zeros_like(dwg_ref)
            dwv_ref[...] = jnp.zeros_like(dwv_ref)

        for x_ref, xp_ref, xbuf, dbuf, d, d_n, w_ref, dx_ref, dw_ref in (
                (hg_ref, hgp_ref, xg, dg, dcg, dcg_n, wg_ref, dhg_ref, dwg_ref),
                (hv_ref, hvp_ref, xv, dv, dcv, dcv_n, wv_ref, dhv_ref, dwv_ref)):
            xbuf[0:8, :] = jnp.where(i > 0, xp_ref[...], 0.0)
            xbuf[8:, :] = x_ref[...]
            dbuf[0:ts, :] = d
            dbuf[ts:, :] = jnp.where(behind, d_n, 0.0)
            wt = w_ref[...]
            ext_x, ext_d = xbuf[...], dbuf[...]
            acc = _rows_up(ext_d, k - 1)[0:ts, :] * wt[0:1, :]
            for j in range(1, k):
                acc = acc + _rows_up(ext_d, k - 1 - j)[0:ts, :] * wt[j:j + 1, :]
            dx_ref[...] = acc.astype(BF16)
            for j in range(k):
                dw_ref[j:j + 1, :] += jnp.sum(d * _rows_down(ext_x, k - 1 - j)[8:, :], axis=0, keepdims=True)

    tile = pl.BlockSpec((ts, tw), lambda j, i: (i, j))
    prev = pl.BlockSpec((8, tw), lambda j, i: (jnp.maximum(i * hb - 1, 0), j))
    nxt = pl.BlockSpec((8, tw), lambda j, i: (jnp.minimum((i + 1) * hb, t // 8 - 1), j))
    taps = pl.BlockSpec((k, tw), lambda j, i: (0, j))
    dw_spec = pl.BlockSpec((8, tw), lambda j, i: (0, j))
    return pl.pallas_call(
        body, grid=(width // tw, nt),
        in_specs=[tile, prev, tile, prev, tile, nxt, tile, nxt, tile, nxt, taps, taps],
        out_specs=(tile, tile, dw_spec, dw_spec),
        out_shape=(jax.ShapeDtypeStruct((t, width), BF16), jax.ShapeDtypeStruct((t, width), BF16),
                   jax.ShapeDtypeStruct((8, width), F32), jax.ShapeDtypeStruct((8, width), F32)),
        scratch_shapes=[pltpu.VMEM((ts + 8, tw), F32)] * 4,
        compiler_params=_params("parallel", "arbitrary"), name=name,
    )(hg, hg, hv, hv, cg, cg, cv, cv, dact, dact, wg, wv)


def _gdn_chunk(c, z, ba, pa, pdt, hn, s, *, valid, inverse=None, with_inverse=False):
    r = lax.broadcasted_iota(jnp.int32, (CHUNK, CHUNK), 0)
    q_ = lax.broadcasted_iota(jnp.int32, (CHUNK, CHUNK), 1)
    causal = r >= q_
    strict = r > q_
    tril = causal.astype(F32)
    triu = (r <= q_).astype(F32)
    lane = lax.broadcasted_iota(jnp.int32, (CHUNK, 128), 1)

    decay_log = -jnp.exp(pa) * _softplus(ba + pdt)
    bg = jnp.where(lane < HEADS, jax.nn.sigmoid(ba), jnp.where(lane < 2 * HEADS, decay_log, 0.0))
    bg = jnp.where(valid, bg, 0.0)
    gc = _hnn(tril, bg)
    gct = _hnn(bg.T, triu)
    eg = jnp.exp(gc)
    glast = gc[CHUNK - 1:CHUNK, :]
    ekd = jnp.exp(glast - gc)
    gtot = jnp.exp(glast)

    hd = range(HEADS)
    hs = [slice(h * HEAD_DIM, (h + 1) * HEAD_DIM) for h in hd]
    gl = [slice(HEADS + h, HEADS + h + 1) for h in hd]
    q = [_silu(c[:, hs[h]]) for h in hd]
    k = [_silu(c[:, D_MODEL + h * HEAD_DIM:D_MODEL + (h + 1) * HEAD_DIM]) for h in hd]
    v = [_silu(c[:, 2 * D_MODEL + h * HEAD_DIM:2 * D_MODEL + (h + 1) * HEAD_DIM]) for h in hd]
    q = [q[h] * lax.rsqrt(jnp.sum(q[h] * q[h], axis=-1, keepdims=True) + NORM_EPS) * (HEAD_DIM ** -0.5) for h in hd]
    k = [k[h] * lax.rsqrt(jnp.sum(k[h] * k[h], axis=-1, keepdims=True) + NORM_EPS) for h in hd]
    beta = [bg[:, h:h + 1] for h in hd]
    decay = [jnp.exp(jnp.where(causal, gc[:, gl[h]] - gct[gl[h], :], -1e30)) for h in hd]
    kb = [k[h] * beta[h] for h in hd]
    a = [jnp.where(strict, _nt(kb[h], k[h]) * decay[h], 0.0) for h in hd]
    qk = [jnp.where(causal, _nt(q[h], k[h]) * decay[h], 0.0) for h in hd]
    x = _inv_unit_lower(tuple(a)) if inverse is None else _kept_inverse(tuple(a), tuple(inverse))
    u = [_nn(x[h], v[h] * beta[h]) for h in hd]
    w = [_nn(x[h], kb[h] * eg[:, gl[h]]) for h in hd]
    v_new = [u[h] - _nn(w[h], s[h]) for h in hd]
    o = [_nn(q[h] * eg[:, gl[h]], s[h]) + _nn(qk[h], v_new[h]) for h in hd]
    states = [s[h] * gtot[:, gl[h]] + _tn(k[h] * ekd[:, gl[h]], v_new[h]) for h in hd]
    o = [o[h] * lax.rsqrt(jnp.mean(o[h] * o[h], axis=-1, keepdims=True) + NORM_EPS) * hn * _silu(z[:, hs[h]])
         for h in hd]
    if with_inverse:
        return jnp.concatenate(o, axis=1), tuple(states), x
    return jnp.concatenate(o, axis=1), tuple(states)


GDN_FWD_CHUNKS = 10
GDN_BWD_CHUNKS = 2


def _chunk_valid(n, t):
    row = n * CHUNK + lax.broadcasted_iota(jnp.int32, (CHUNK, 1), 0)
    return jnp.logical_and(row >= LEAD, row < t - TAIL)


def gdn_fwd(c, z, ba, pa, pdt, hn, name, ba_block=0):
    t = c.shape[0]
    n_chunks = t // CHUNK
    per_step = GDN_FWD_CHUNKS if n_chunks % GDN_FWD_CHUNKS == 0 else 1
    rows_per_step = per_step * CHUNK

    def body(c_ref, z_ref, ba_ref, pa_ref, pdt_ref, hn_ref, y_ref, ss_ref, inv_ref, state):
        step = pl.program_id(0)

        @pl.when(step == 0)
        def _():
            state[...] = jnp.zeros_like(state)

        s = tuple(state[h] for h in range(HEADS))
        for j in range(per_step):
            rows = pl.ds(j * CHUNK, CHUNK)
            for h in range(HEADS):
                ss_ref[j, h] = s[h]
            y, s, inv = _gdn_chunk(c_ref[rows, :], z_ref[rows, :], ba_ref[rows, :], pa_ref[...], pdt_ref[...],
                                   hn_ref[...], s, valid=_chunk_valid(step * per_step + j, t), with_inverse=True)
            y_ref[rows, :] = y
            for h in range(HEADS):
                inv_ref[j, h] = inv[h]
        for h in range(HEADS):
            state[h] = s[h]

    vec = pl.BlockSpec((1, 128), lambda n: (0, 0))
    return pl.pallas_call(
        body, grid=(n_chunks // per_step,),
        in_specs=[pl.BlockSpec((rows_per_step, QKV_DIM), lambda n: (n, 0)),
                  pl.BlockSpec((rows_per_step, D_MODEL), lambda n: (n, 0)),
                  pl.BlockSpec((rows_per_step, 128), lambda n: (n, ba_block)), vec, vec, vec],
        out_specs=(pl.BlockSpec((rows_per_step, D_MODEL), lambda n: (n, 0)),
                   pl.BlockSpec((per_step, HEADS, HEAD_DIM, HEAD_DIM), lambda n: (n, 0, 0, 0)),
                   pl.BlockSpec((per_step, HEADS, CHUNK, CHUNK), lambda n: (n, 0, 0, 0))),
        out_shape=(jax.ShapeDtypeStruct((t, D_MODEL), F32),
                   jax.ShapeDtypeStruct((n_chunks, HEADS, HEAD_DIM, HEAD_DIM), F32),
                   jax.ShapeDtypeStruct((n_chunks, HEADS, CHUNK, CHUNK), F32)),
        scratch_shapes=[pltpu.VMEM((HEADS, HEAD_DIM, HEAD_DIM), F32)],
        compiler_params=_params("arbitrary"), name=name,
    )(c, z, ba, pa, pdt, hn)


def gdn_bwd(c, z, ba, pa, pdt, hn, starts, inverses, dy, name, ba_block=0):
    t = c.shape[0]
    per_step = GDN_BWD_CHUNKS if (t // CHUNK) % GDN_BWD_CHUNKS == 0 else 1
    n_steps = t // CHUNK // per_step
    rows_per_step = per_step * CHUNK

    def body(c_ref, z_ref, ba_ref, pa_ref, pdt_ref, hn_ref, ss_ref, inv_ref, dy_ref,
             dc_ref, dz_ref, dba_ref, dpa_ref, dpdt_ref, dhn_ref, dstate):
        step = pl.program_id(0)

        @pl.when(step == 0)
        def _():
            dstate[...] = jnp.zeros_like(dstate)
            dpa_ref[...] = jnp.zeros_like(dpa_ref)
            dpdt_ref[...] = jnp.zeros_like(dpdt_ref)
            dhn_ref[...] = jnp.zeros_like(dhn_ref)

        ds = tuple(dstate[h] for h in range(HEADS))
        for j in reversed(range(per_step)):
            rows = pl.ds(j * CHUNK, CHUNK)
            f = functools.partial(_gdn_chunk, valid=_chunk_valid((n_steps - 1 - step) * per_step + j, t),
                                  inverse=tuple(inv_ref[j, h] for h in range(HEADS)))
            _, vjp = jax.vjp(f, c_ref[rows, :], z_ref[rows, :], ba_ref[rows, :], pa_ref[...], pdt_ref[...], hn_ref[...],
                             tuple(ss_ref[j, h] for h in range(HEADS)))
            dc, dz, dba, dpa, dpdt, dhn, ds = vjp((dy_ref[rows, :], ds))
            dc_ref[rows, :] = dc
            dz_ref[rows, :] = dz.astype(BF16)
            dba_ref[rows, :] = dba.astype(BF16)
            dpa_ref[...] += dpa
            dpdt_ref[...] += dpdt
            dhn_ref[...] += dhn
        for h in range(HEADS):
            dstate[h] = ds[h]

    def rev(width, block=0):
        return pl.BlockSpec((rows_per_step, width), lambda s: (n_steps - 1 - s, block))

    vec = pl.BlockSpec((1, 128), lambda s: (0, 0))
    vec_shape = jax.ShapeDtypeStruct((1, 128), F32)
    return pl.pallas_call(
        body, grid=(n_steps,),
        in_specs=[rev(QKV_DIM), rev(D_MODEL), rev(128, ba_block), vec, vec, vec,
                  pl.BlockSpec((per_step, HEADS, HEAD_DIM, HEAD_DIM), lambda s: (n_steps - 1 - s, 0, 0, 0)),
                  pl.BlockSpec((per_step, HEADS, CHUNK, CHUNK), lambda s: (n_steps - 1 - s, 0, 0, 0)),
                  rev(D_MODEL)],
        out_specs=(rev(QKV_DIM), rev(D_MODEL), rev(128), vec, vec, vec),
        out_shape=(jax.ShapeDtypeStruct((t, QKV_DIM), F32), jax.ShapeDtypeStruct((t, D_MODEL), BF16),
                   jax.ShapeDtypeStruct((t, 128), BF16), vec_shape, vec_shape, vec_shape),
        scratch_shapes=[pltpu.VMEM((HEADS, HEAD_DIM, HEAD_DIM), F32)],
        compiler_params=_params("arbitrary"), name=name,
    )(c, z, ba, pa, pdt, hn, starts, inverses, dy)


def _layer_fwd(h, u, ut, w, next_gain, tag):
    pq = mm(u, w["wqkv"], name=f"{tag}_mm_qkv")
    pz = mm(u, w["wz"], name=f"{tag}_mm_z")
    pg = mm(u, w["wg"], name=f"{tag}_mm_gate")
    pba = mm(u, w["wpb"], name=f"{tag}_mm_pool_ba")
    cq = conv_fwd(pq, w["conv_qkv"], f"{tag}_conv_qkv")
    ya, starts, inverses = gdn_fwd(cq, pz, pba, w["pa"], w["pdt"], w["head_norm"], f"{tag}_gdn", ba_block=BA_BLOCK)
    pooled = poolwin_fwd(pba, f"{tag}_poolwin")
    y, yt = mix_fwd(ya, pg, pooled, w["w_pool"], w["pool_scale"], f"{tag}_mix")
    h1, u2, u2t = mm(y, w["wout"], add=h, norm_gain=w["norm_ffn"], name=f"{tag}_mm_out")
    hg = mm(u2, w["wupg"], name=f"{tag}_mm_upg")
    hv = mm(u2, w["wupv"], name=f"{tag}_mm_upv")
    cg, cv, act, actt = ffn_mid_fwd(hg, hv, w["conv_g"], w["conv_v"], f"{tag}_ffn_mid")
    if next_gain is None:
        h2, nxt = mm(act, w["wdown"], add=h1, name=f"{tag}_mm_down"), None
    else:
        h2, *nxt = mm(act, w["wdown"], add=h1, norm_gain=next_gain, name=f"{tag}_mm_down")
    saved = dict(h=h, ut=ut, pq=pq, pz=pz, pg=pg, pba=pba, cq=cq, ya=ya, starts=starts, inverses=inverses,
                 pooled=pooled, yt=yt, h1=h1, u2t=u2t, hg=hg, hv=hv, cg=cg, cv=cv, actt=actt)
    return h2, nxt, saved


def _layer_bwd(dh2, dh2b, w, s, tag):
    g = {}
    dact = mm(dh2b, w["wdown"], tb=True, name=f"{tag}_bmm_down_x")
    g["wdown"] = mm(s["actt"], dh2b, name=f"{tag}_bmm_down_w")
    dhg, dhv, g["conv_g"], g["conv_v"] = ffn_mid_bwd(s["hg"], s["hv"], s["cg"], s["cv"], dact, w["conv_g"], w["conv_v"],
                                                     f"{tag}_ffn_mid_b")
    du2 = mm_nt_sum([(dhg, w["wupg"]), (dhv, w["wupv"])], f"{tag}_bmm_up_x")
    g["wupg"] = mm(s["u2t"], dhg, name=f"{tag}_bmm_upg_w")
    g["wupv"] = mm(s["u2t"], dhv, name=f"{tag}_bmm_upv_w")
    dh1, dh1b, g["norm_ffn"] = rms_bwd(s["h1"], w["norm_ffn"], du2, dh2, f"{tag}_rms_ffn_b")
    dy = mm(dh1b, w["wout"], tb=True, name=f"{tag}_bmm_out_x")
    g["wout"] = mm(s["yt"], dh1b, name=f"{tag}_bmm_out_w")
    dya, dpg, dpooled, g["w_pool"], g["pool_scale"] = mix_bwd(
        s["ya"], s["pg"], s["pooled"], w["w_pool"], w["pool_scale"], dy, f"{tag}_mix_b")
    dpp = poolwin_bwd(dpooled, f"{tag}_poolwin_b")
    dcq, dpz, dpba, g["pa"], g["pdt"], g["head_norm"] = gdn_bwd(
        s["cq"], s["pz"], s["pba"], w["pa"], w["pdt"], w["head_norm"], s["starts"], s["inverses"], dya, f"{tag}_gdn_b",
        ba_block=BA_BLOCK)
    dpb = jnp.concatenate([dpp, dpba], axis=1)
    dpq, g["conv_qkv"] = conv_bwd(s["pq"], dcq, w["conv_qkv"], f"{tag}_conv_qkv_b")
    du = mm_nt_sum([(dpq, w["wqkv"]), (dpz, w["wz"]), (dpg, w["wg"]), (dpb, w["wpb"])], f"{tag}_bmm_in_x")
    g["wqkv"] = mm(s["ut"], dpq, name=f"{tag}_bmm_qkv_w")
    g["wz"] = mm(s["ut"], dpz, name=f"{tag}_bmm_z_w")
    g["wg"] = mm(s["ut"], dpg, name=f"{tag}_bmm_gate_w")
    dwpb = mm(s["ut"], dpb, name=f"{tag}_bmm_pool_ba_w")
    g["wpl"], g["wba"] = dwpb[:, :POOL_WIDTH], dwpb[:, POOL_WIDTH:]
    dh, dhb, g["norm_mix"] = rms_bwd(s["h"], w["norm_mix"], du, dh1, f"{tag}_rms_mix_b")
    return dh, dhb, g


def local_step(h0, target, layers, norm_final):
    h = h0
    normed = rms_fwd(h0, layers[0]["norm_mix"], "l0_rms_mix")
    saved = []
    for li, w in enumerate(layers):
        next_gain = layers[li + 1]["norm_mix"] if li + 1 < len(layers) else None
        h, normed, s = _layer_fwd(h, *normed, w, next_gain, f"l{li}")
        saved.append(s)
    loss, dh, dhb, dnf = loss_head(h, norm_final, target, "loss_head")
    grads = [None] * len(layers)
    for li in reversed(range(len(layers))):
        dh, dhb, grads[li] = _layer_bwd(dh, dhb, layers[li], saved[li], f"l{li}")
    return loss, dh, grads, dnf


_Z0, _B0, _P0, _G0, _IN_DIM = 3072, 4096, 4112, 4624, 6672


def _lanes_8_to_15(v):
    return jnp.pad(v.reshape(1, HEADS).astype(F32), ((0, 0), (HEADS, 128 - 2 * HEADS)))


IN_PIECES = (("wqkv", 0, _Z0), ("wz", _Z0, _B0), ("wba", _B0, _P0), ("wpl", _P0, _G0), ("wg", _G0, _IN_DIM))
IN_SHARD = _IN_DIM // 4


def _overlaps(a, b, spans):
    return [(name, max(a, lo) - lo, min(b, hi) - max(a, lo)) for name, lo, hi in spans if max(a, lo) < min(b, hi)]


def _cat(parts):
    return parts[0] if len(parts) == 1 else jnp.concatenate(parts, axis=1)


def prep_layer(p):
    row = lambda v: v.reshape(1, -1).astype(F32)
    w_in, w_up = p["w_in"], p["w_up"]
    if not isinstance(w_in, (list, tuple)):
        w_in = [w_in[:, s * IN_SHARD:(s + 1) * IN_SHARD] for s in range(4)]
        w_up = [w_up[:, s * (D_FF // 2):(s + 1) * (D_FF // 2)] for s in range(4)]
    shards = [(s, s * IN_SHARD, (s + 1) * IN_SHARD) for s in range(4)]
    piece = {name: _cat([w_in[s][:, off:off + width].astype(BF16) for s, off, width in _overlaps(lo, hi, shards)])
             for name, lo, hi in IN_PIECES}
    return dict(
        wqkv=piece["wqkv"], wz=piece["wz"],
        wpb=jnp.concatenate([piece["wpl"], jnp.pad(piece["wba"], ((0, 0), (0, 128 - 2 * HEADS)))], axis=1),
        wg=piece["wg"], wout=p["w_out"].astype(BF16),
        wupg=_cat([w_up[0].astype(BF16), w_up[1].astype(BF16)]), wupv=_cat([w_up[2].astype(BF16), w_up[3].astype(BF16)]),
        wdown=p["w_down"].astype(BF16),
        conv_qkv=p["conv_qkv"].astype(F32), conv_g=p["conv_ffn"][:, :D_FF].astype(F32),
        conv_v=p["conv_ffn"][:, D_FF:].astype(F32), w_pool=p["w_pool"].astype(F32),
        pool_scale=row(p["pool_scale"]), head_norm=row(p["head_norm"]), norm_mix=row(p["norm_mix"]),
        norm_ffn=row(p["norm_ffn"]), pa=_lanes_8_to_15(p["a_log"]), pdt=_lanes_8_to_15(p["dt_bias"]))


def layer_grads(g):
    return dict(
        w_in=jnp.concatenate([g["wqkv"], g["wz"], g["wba"][:, :2 * HEADS], g["wpl"], g["wg"]], axis=1),
        conv_qkv=g["conv_qkv"][:4], a_log=g["pa"][0, HEADS:2 * HEADS], dt_bias=g["pdt"][0, HEADS:2 * HEADS],
        head_norm=g["head_norm"][0], w_pool=g["w_pool"], pool_scale=g["pool_scale"][0], w_out=g["wout"],
        norm_mix=g["norm_mix"][0], norm_ffn=g["norm_ffn"][0],
        w_up=jnp.concatenate([g["wupg"], g["wupv"]], axis=1),
        conv_ffn=jnp.concatenate([g["conv_g"][:3], g["conv_v"][:3]], axis=1), w_down=g["wdown"])


def big_grad_shards(g):
    in_shards = [_cat([g[name][:, off:off + width] for name, off, width in
                       _overlaps(s * IN_SHARD, (s + 1) * IN_SHARD, IN_PIECES)]) for s in range(4)]
    half = D_FF // 2
    up_shards = [g["wupg"][:, :half], g["wupg"][:, half:], g["wupv"][:, :half], g["wupv"][:, half:]]
    return dict(w_in=jnp.stack(in_shards), w_up=jnp.stack(up_shards),
                w_down=g["wdown"].reshape(4, D_FF // 4, D_MODEL), w_out=g["wout"].reshape(4, D_MODEL // 4, D_MODEL))


LAYER_PARAMS = ("norm_mix", "w_in", "conv_qkv", "a_log", "dt_bias", "head_norm", "w_pool", "pool_scale", "w_out",
                "norm_ffn", "w_up", "conv_ffn", "w_down")


def pad_rows(meta, x):
    return jnp.concatenate([jnp.zeros((LEAD, D_MODEL), F32), meta.astype(F32), x.astype(F32),
                            jnp.zeros((TAIL, D_MODEL), F32)], axis=0)


MESH = pl.DeviceIdType.MESH
ANY = pl.BlockSpec(memory_space=pl.ANY)


def _place():
    x, y, c = lax.axis_index("x"), lax.axis_index("y"), lax.axis_index("c")
    return x, y, c, [(1 - x, y), (x, 1 - y), (1 - x, 1 - y)]


def _my_chip():
    return 2 * lax.axis_index("x") + lax.axis_index("y")


def gather_shards(packs):
    n = len(packs)

    def body(*refs):
        p_refs, o_refs, (send_sems, recv_sems) = refs[:n], refs[n:2 * n], refs[2 * n:]
        x, y, c, chips = _place()

        def copy(a, k, chip, half, to, src=None):
            dst = o_refs[a].at[2 * chip[0] + chip[1], half]
            return pltpu.make_async_remote_copy(src_ref=dst if src is None else src, dst_ref=dst,
                                                send_sem=send_sems.at[6 * a + k], recv_sem=recv_sems.at[6 * a + k],
                                                device_id=to, device_id_type=MESH)

        first = [copy(a, j, (x, y), c, (*chip, c), src=p_refs[a].at[c]) for a in range(n) for j, chip in enumerate(chips)]
        for cp in first:
            cp.start()
        passed = []
        for a in range(n):
            for j, chip in enumerate(chips):
                copy(a, j, chip, c, (x, y, c)).wait_recv()
                passed.append(copy(a, 3 + j, chip, c, (x, y, 1 - c)))
                passed[-1].start()
        for a in range(n):
            for j, chip in enumerate(chips):
                copy(a, 3 + j, chip, 1 - c, (x, y, c)).wait_recv()
        for cp in first + passed:
            cp.wait_send()

    gathered = pl.pallas_call(
        body, in_specs=[ANY] * n, out_specs=[ANY] * n,
        out_shape=[jax.ShapeDtypeStruct((4,) + p.shape, p.dtype) for p in packs],
        scratch_shapes=[pltpu.SemaphoreType.DMA((6 * n,)), pltpu.SemaphoreType.DMA((6 * n,))],
        name="gather_shards",
    )(*packs)
    me = _my_chip()
    return [lax.dynamic_update_slice(g, p[None], (me,) + (0,) * p.ndim) for g, p in zip(gathered, packs)]


def swap_other_halves(ps):
    n = len(ps)

    def body(*refs):
        p_refs, o_refs, (send_sems, recv_sems) = refs[:n], refs[n:2 * n], refs[2 * n:]
        x, y, c, _ = _place()
        copies = [pltpu.make_async_remote_copy(src_ref=p_refs[a].at[s, 1 - c], dst_ref=o_refs[a].at[s],
                                               send_sem=send_sems.at[4 * a + s], recv_sem=recv_sems.at[4 * a + s],
                                               device_id=(x, y, 1 - c), device_id_type=MESH)
                  for a in range(n) for s in range(4)]
        for cp in copies:
            cp.start()
        for cp in copies:
            cp.wait()

    return pl.pallas_call(
        body, in_specs=[ANY] * n, out_specs=[ANY] * n,
        out_shape=[jax.ShapeDtypeStruct((4,) + p.shape[2:], p.dtype) for p in ps],
        scratch_shapes=[pltpu.SemaphoreType.DMA((4 * n,)), pltpu.SemaphoreType.DMA((4 * n,))],
        name="swap_other_halves",
    )(*ps)


def scatter_to_chips(qs):
    n = len(qs)

    def body(*refs):
        q_refs, o_refs, (send_sems, recv_sems) = refs[:n], refs[n:2 * n], refs[2 * n:]
        x, y, c, chips = _place()
        me = 2 * x + y
        copies = [pltpu.make_async_remote_copy(src_ref=q_refs[a].at[2 * chip[0] + chip[1]], dst_ref=o_refs[a].at[me],
                                               send_sem=send_sems.at[3 * a + j], recv_sem=recv_sems.at[3 * a + j],
                                               device_id=(*chip, c), device_id_type=MESH)
                  for a in range(n) for j, chip in enumerate(chips)]
        for cp in copies:
            cp.start()
        for a in range(n):
            for j, chip in enumerate(chips):
                slot = o_refs[a].at[2 * chip[0] + chip[1]]
                pltpu.make_async_remote_copy(src_ref=slot, dst_ref=slot, send_sem=send_sems.at[3 * a + j],
                                             recv_sem=recv_sems.at[3 * a + j],
                                             device_id=(x, y, c), device_id_type=MESH).wait_recv()
        for cp in copies:
            cp.wait_send()

    received = pl.pallas_call(
        body, in_specs=[ANY] * n, out_specs=[ANY] * n,
        out_shape=[jax.ShapeDtypeStruct(q.shape, q.dtype) for q in qs],
        scratch_shapes=[pltpu.SemaphoreType.DMA((3 * n,)), pltpu.SemaphoreType.DMA((3 * n,))],
        name="scatter_to_chips",
    )(*qs)
    me = _my_chip()
    return [lax.dynamic_update_slice(r, lax.dynamic_slice_in_dim(q, me, 1, axis=0), (me, 0, 0))
            for r, q in zip(received, qs)]


def join_halves(boths):
    n = len(boths)

    def body(*refs):
        o_refs, (send_sems, recv_sems) = refs[n:2 * n], refs[2 * n:]
        x, y, c, _ = _place()
        copies = [pltpu.make_async_remote_copy(src_ref=o_refs[a].at[c], dst_ref=o_refs[a].at[c],
                                               send_sem=send_sems.at[a], recv_sem=recv_sems.at[a],
                                               device_id=(x, y, 1 - c), device_id_type=MESH) for a in range(n)]
        for cp in copies:
            cp.start()
        for a in range(n):
            other = o_refs[a].at[1 - c]
            pltpu.make_async_remote_copy(src_ref=other, dst_ref=other, send_sem=send_sems.at[a],
                                         recv_sem=recv_sems.at[a], device_id=(x, y, c), device_id_type=MESH).wait_recv()
        for cp in copies:
            cp.wait_send()

    return pl.pallas_call(
        body, in_specs=[ANY] * n, out_specs=[ANY] * n,
        out_shape=[jax.ShapeDtypeStruct(b.shape, b.dtype) for b in boths],
        input_output_aliases={a: a for a in range(n)},
        scratch_shapes=[pltpu.SemaphoreType.DMA((n,)), pltpu.SemaphoreType.DMA((n,))], name="join_halves",
    )(*boths)


def add_own_half(p, other, c, out_dtype, name):
    _, _, rows, lanes = p.shape
    tr = _tile(rows, max(16, 524288 // lanes), 16)

    def body(c_ref, p_ref, o_ref, out_ref):
        out_ref[...] = (p_ref[...] + o_ref[...]).astype(out_dtype)

    return pl.pallas_call(
        body,
        grid_spec=pltpu.PrefetchScalarGridSpec(
            num_scalar_prefetch=1, grid=(4, rows // tr),
            in_specs=[pl.BlockSpec((None, None, tr, lanes), lambda s, i, c_ref: (s, c_ref[0], i, 0)),
                      pl.BlockSpec((None, tr, lanes), lambda s, i, c_ref: (s, i, 0))],
            out_specs=pl.BlockSpec((None, tr, lanes), lambda s, i, c_ref: (s, i, 0))),
        out_shape=jax.ShapeDtypeStruct((4, rows, lanes), out_dtype),
        compiler_params=_params("parallel", "parallel"), name=name,
    )(c, p, other)


def sum_chips(b, c, name):
    _, rows, lanes = b.shape
    tr = _tile(rows, max(16, 524288 // lanes), 16)

    def body(c_ref, b_ref, out_ref):
        b0, b1, b2, b3 = (b_ref[k].astype(F32) for k in range(4))
        out_ref[...] = ((b0 + b1) + b2) + b3

    return pl.pallas_call(
        body,
        grid_spec=pltpu.PrefetchScalarGridSpec(
            num_scalar_prefetch=1, grid=(rows // tr,),
            in_specs=[pl.BlockSpec((4, tr, lanes), lambda i, c_ref: (0, i, 0))],
            out_specs=pl.BlockSpec((None, tr, lanes), lambda i, c_ref: (c_ref[0], i, 0))),
        out_shape=jax.ShapeDtypeStruct((2, rows, lanes), F32),
        compiler_params=_params("parallel"), name=name,
    )(c, b)


def all_reduce_to_shards(packs, wires, tags, c):
    others = swap_other_halves(packs)
    qs = [add_own_half(p, o, c, wire, f"add_own_half_{tag}") for p, o, wire, tag in zip(packs, others, wires, tags)]
    return join_halves([sum_chips(r, c, f"sum_chips_{tag}") for r, tag in zip(scatter_to_chips(qs), tags)])


def adamw(w, g, m, v, name):
    shape = w.shape
    cols = shape[-1]
    w2, g2, m2, v2 = (a.reshape(-1, cols) for a in (w, g, m, v))
    rows = w2.shape[0]
    tr = _tile(rows, max(8, 262144 // cols), 8) if rows % 8 == 0 else rows
    c1 = 1.0 - ADAM_B1 ** ADAM_STEP
    c2 = 1.0 - ADAM_B2 ** ADAM_STEP

    def body(w_ref, g_ref, m_ref, v_ref, d_ref, mo_ref, vo_ref):
        gv = g_ref[...]
        mn = ADAM_B1 * m_ref[...] + (1.0 - ADAM_B1) * gv
        vn = ADAM_B2 * v_ref[...] + (1.0 - ADAM_B2) * jnp.square(gv)
        d_ref[...] = -ADAM_LR * ((mn / c1) / (jnp.sqrt(vn / c2) + ADAM_EPS) + ADAM_WD * w_ref[...])
        mo_ref[...] = mn
        vo_ref[...] = vn

    spec = pl.BlockSpec((tr, cols), lambda i: (i, 0))
    out = jax.ShapeDtypeStruct((rows, cols), F32)
    d, mn, vn = pl.pallas_call(
        body, grid=(rows // tr,), in_specs=[spec] * 4, out_specs=(spec,) * 3, out_shape=(out,) * 3,
        compiler_params=_params("parallel"), name=name,
    )(w2, g2, m2, v2)
    return d.reshape(shape), mn.reshape(shape), vn.reshape(shape)


BIG = ("w_in", "w_up", "w_down", "w_out")
SMALL = ("w_pool", "conv_qkv", "conv_ffn", "meta_tokens")
SHARDED = BIG + SMALL
MATMUL_WEIGHTS = BIG + ("w_pool",)
REPLICATED = ("norm_mix", "a_log", "dt_bias", "head_norm", "pool_scale", "norm_ffn", "norm_final")
SHARD_AXIS = {"w_in": 2, "w_up": 2, "w_out": 1, "w_down": 1, "w_pool": 3, "conv_qkv": 2, "conv_ffn": 2, "meta_tokens": 1}


def _rows_of(a):
    return a.reshape(-1, 128)


SEGMENT_ROWS = 16


def _segment(n_rows):
    return -(-n_rows // SEGMENT_ROWS) * SEGMENT_ROWS


def _pad_segment(a):
    pad = [(0, 0)] * a.ndim
    pad[-2] = (0, _segment(a.shape[-2]) - a.shape[-2])
    return jnp.pad(a, pad)


def _unshard(stacked, axis):
    full = jnp.moveaxis(stacked, 0, axis)
    shape = list(full.shape)
    shape[axis:axis + 2] = [shape[axis] * shape[axis + 1]]
    return full.reshape(shape)


def _shard_stack(full, axis):
    shape = list(full.shape)
    shape[axis:axis + 1] = [4, shape[axis] // 4]
    return jnp.moveaxis(full.reshape(shape), axis, 0)


def pack_weights(shards):
    parts = [_rows_of(shards[k].astype(WIRE)) if k in MATMUL_WEIGHTS else
             lax.bitcast_convert_type(_rows_of(shards[k].astype(F32)), WIRE).reshape(-1, 128) for k in SMALL]
    parts = [_pad_segment(p) for p in parts]
    rows = sum(p.shape[0] for p in parts)
    if rows % (2 * SEGMENT_ROWS):
        parts.append(jnp.zeros((SEGMENT_ROWS, 128), WIRE))
        rows += SEGMENT_ROWS
    return [shards[k].astype(WIRE) for k in BIG] + [jnp.concatenate(parts, axis=0).reshape(2, rows // 2, 128)]


def unpack_weights(gathered, shard_shapes):
    out = {k: _unshard(g, SHARD_AXIS[k]) for k, g in zip(BIG, gathered)}
    flat = gathered[-1].reshape(4, -1, 128)
    at = 0
    for k in SMALL:
        shp = shard_shapes[k]
        n = 1
        for e in shp:
            n *= e
        if k in MATMUL_WEIGHTS:
            r = n // 128
            stacked = flat[:, at:at + r].reshape((4,) + tuple(shp))
        else:
            r = 2 * n // 128
            stacked = lax.bitcast_convert_type(flat[:, at:at + r].reshape(4, n // 128, 128, 2), F32).reshape((4,) + tuple(shp))
        out[k] = _unshard(stacked, SHARD_AXIS[k])
        at += _segment(r)
    return out


def pack_grads(big, full, repl):
    r = jnp.concatenate([repl[k].reshape(-1) for k in REPLICATED])
    r = jnp.pad(r, (0, -r.shape[0] % 128)).reshape(1, -1, 128)
    parts = [_shard_stack(full[k], SHARD_AXIS[k]).reshape(4, -1, 128) for k in SMALL]
    parts = [_pad_segment(p) for p in parts + [jnp.broadcast_to(r, (4,) + r.shape[1:])]]
    rows = sum(p.shape[1] for p in parts)
    if rows % (2 * SEGMENT_ROWS):
        parts.append(jnp.zeros((4, SEGMENT_ROWS, 128), F32))
        rows += SEGMENT_ROWS
    side = jnp.concatenate(parts, axis=1).reshape(4, 2, rows // 2, 128)
    return list(big) + [side]


def unpack_grads(reduced, shard_shapes, repl_shapes):
    out = dict(zip(BIG, reduced))
    side = reduced[-1].reshape(-1, 128)
    at = 0
    for k in SMALL:
        n = 1
        for e in shard_shapes[k]:
            n *= e
        out[k] = side[at:at + n // 128].reshape(shard_shapes[k])
        at += _segment(n // 128)
    r = side[at:].reshape(-1)
    at = 0
    for k in REPLICATED:
        n = 1
        for e in repl_shapes[k]:
            n *= e
        out[k] = r[at:at + n].reshape(repl_shapes[k])
        at += n
    return out


WEIGHT_ORDER = ("meta_tokens", "norm_mix", "w_in", "conv_qkv", "a_log", "dt_bias", "head_norm", "w_pool", "pool_scale",
                "w_out", "norm_ffn", "w_up", "conv_ffn", "w_down", "norm_final")


def kernel(x, meta_tokens, norm_mix, w_in, conv_qkv, a_log, dt_bias, head_norm, w_pool, pool_scale, w_out, norm_ffn, w_up, conv_ffn, w_down, norm_final, loss_target, m_meta_tokens, m_norm_mix, m_w_in, m_conv_qkv, m_a_log, m_dt_bias, m_head_norm, m_w_pool, m_pool_scale, m_w_out, m_norm_ffn, m_w_up, m_conv_ffn, m_w_down, m_norm_final, v_meta_tokens, v_norm_mix, v_w_in, v_conv_qkv, v_a_log, v_dt_bias, v_head_norm, v_w_pool, v_pool_scale, v_w_out, v_norm_ffn, v_w_up, v_conv_ffn, v_w_down, v_norm_final):
    weights = dict(meta_tokens=meta_tokens, norm_mix=norm_mix, w_in=w_in, conv_qkv=conv_qkv, a_log=a_log,
                   dt_bias=dt_bias, head_norm=head_norm, w_pool=w_pool, pool_scale=pool_scale, w_out=w_out,
                   norm_ffn=norm_ffn, w_up=w_up, conv_ffn=conv_ffn, w_down=w_down, norm_final=norm_final)
    m_in = dict(zip(WEIGHT_ORDER, (m_meta_tokens, m_norm_mix, m_w_in, m_conv_qkv, m_a_log, m_dt_bias, m_head_norm,
                                   m_w_pool, m_pool_scale, m_w_out, m_norm_ffn, m_w_up, m_conv_ffn, m_w_down, m_norm_final)))
    v_in = dict(zip(WEIGHT_ORDER, (v_meta_tokens, v_norm_mix, v_w_in, v_conv_qkv, v_a_log, v_dt_bias, v_head_norm,
                                   v_w_pool, v_pool_scale, v_w_out, v_norm_ffn, v_w_up, v_conv_ffn, v_w_down, v_norm_final)))
    shard_shapes = {k: weights[k].shape for k in SHARDED}
    repl_shapes = {k: weights[k].shape for k in REPLICATED}
    core = lax.axis_index("c").astype(jnp.int32).reshape(1)

    gathered = gather_shards(pack_weights({k: weights[k] for k in SHARDED}))
    full = unpack_weights(gathered, shard_shapes)
    shards = dict(zip(BIG, gathered))
    layers = []
    for li in range(DEPTH):
        p = {k: (full[k][li] if k in SMALL else weights[k][li]) for k in LAYER_PARAMS if k not in BIG}
        p.update(w_in=[shards["w_in"][s, li] for s in range(4)], w_up=[shards["w_up"][s, li] for s in range(4)],
                 w_down=shards["w_down"][:, li].reshape(D_FF, D_MODEL),
                 w_out=shards["w_out"][:, li].reshape(D_MODEL, D_MODEL))
        layers.append(prep_layer(p))

    h0 = pad_rows(full["meta_tokens"], x[0])
    target = pad_rows(jnp.zeros((N_META, D_MODEL), F32), loss_target[0])
    loss, dh0, grads, d_norm_final = local_step(h0, target, layers, norm_final.reshape(1, D_MODEL))
    seq = x.shape[1]
    grad_x = dh0[LEAD + N_META:LEAD + N_META + seq][None]

    per_layer = [layer_grads(g) for g in grads]
    g_all = {k: jnp.stack([pl_[k] for pl_ in per_layer]) for k in LAYER_PARAMS if k not in BIG}
    g_all["meta_tokens"] = dh0[LEAD:LEAD + N_META]
    g_all["norm_final"] = d_norm_final[0]
    big = [big_grad_shards(g) for g in grads]
    packs = pack_grads([jnp.stack([b[k] for b in big], axis=1) for k in BIG],
                       {k: g_all[k] for k in SMALL}, {k: g_all[k] for k in REPLICATED})
    reduced = all_reduce_to_shards(packs, [GRAD_WIRE] * len(BIG) + [F32], BIG + ("side",), core)
    g_mine = unpack_grads(reduced, shard_shapes, repl_shapes)

    loss_sum = lax.psum(loss[0, 0], ("x", "y", "c"))
    deltas, new_m, new_v = {}, {}, {}
    for k in WEIGHT_ORDER:
        deltas[k], new_m[k], new_v[k] = adamw(weights[k], g_mine[k], m_in[k], v_in[k], f"adamw_{k}")
    return (loss_sum, grad_x, *[g_mine[k] for k in WEIGHT_ORDER], *[deltas[k] for k in WEIGHT_ORDER],
            *[new_m[k] for k in WEIGHT_ORDER], *[new_v[k] for k in WEIGHT_ORDER])
```

```python
import functools

import jax
import jax.numpy as jnp
from jax import lax
from jax.experimental import pallas as pl
from jax.experimental.pallas import tpu as pltpu

F32 = jnp.float32
BF16 = jnp.bfloat16
WIRE = jnp.bfloat16
GRAD_WIRE = jnp.bfloat16

D_MODEL = 1024
HEADS = 8
HEAD_DIM = 128
CHUNK = 64
N_META = 16
LEAD = 48
TAIL = 64
QKV_DIM = 3072
D_FF = 2816
POOL_WIDTH = 512
POOL_WINDOWS = (2, 4, 8, 16)
BA_BLOCK = POOL_WIDTH // 128
DEPTH = 2
NORM_EPS = 1e-6
ADAM_LR, ADAM_B1, ADAM_B2, ADAM_EPS, ADAM_WD, ADAM_STEP = 0.001, 0.9, 0.999, 1e-08, 0.01, 10
VMEM_LIMIT_BYTES = 48 * 1024 * 1024


def _params(*sem):
    return pltpu.CompilerParams(dimension_semantics=sem if sem else None, vmem_limit_bytes=VMEM_LIMIT_BYTES)


def _tile(n, cap, mult):
    best = None
    for t in range(mult, min(n, cap) + 1, mult):
        if n % t == 0:
            best = t
    assert best is not None, (n, cap, mult)
    return best


def _silu(x):
    return x * jax.nn.sigmoid(x)


def _softplus(x):
    return jnp.maximum(x, 0.0) + jnp.log(1.0 + jnp.exp(-jnp.abs(x)))


def _split_bf16(a):
    hi = a.astype(BF16)
    return hi, (a - hi.astype(F32)).astype(BF16)


def _dg(a, b, ca, cb, hi):
    dims = (((ca,), (cb,)), ((), ()))
    if hi is True:
        return lax.dot_general(a, b, dims, precision=lax.Precision.HIGHEST, preferred_element_type=F32)
    if hi == 3:
        (ah, al), (bh, bl) = _split_bf16(a), _split_bf16(b)
        dot = lambda x, y: lax.dot_general(x, y, dims, preferred_element_type=F32)
        return dot(ah, bh) + (dot(ah, bl) + dot(al, bh))
    return lax.dot_general(a.astype(BF16), b.astype(BF16), dims, preferred_element_type=F32)


def _make_dots(hi):
    @jax.custom_vjp
    def nn(a, b):
        return _dg(a, b, 1, 0, hi)

    @jax.custom_vjp
    def nt(a, b):
        return _dg(a, b, 1, 1, hi)

    @jax.custom_vjp
    def tn(a, b):
        return _dg(a, b, 0, 0, hi)

    nn.defvjp(lambda a, b: (nn(a, b), (a, b)), lambda r, g: (nt(g, r[1]), tn(r[0], g)))
    nt.defvjp(lambda a, b: (nt(a, b), (a, b)), lambda r, g: (nn(g, r[1]), tn(g, r[0])))
    tn.defvjp(lambda a, b: (tn(a, b), (a, b)), lambda r, g: (nt(r[1], g), nn(r[0], g)))
    return nn, nt, tn


_nn, _nt, _tn = _make_dots(False)
_hnn, _hnt, _htn = _make_dots(True)


def _neumann(a):
    n = a[0].shape[0]
    eye = (lax.broadcasted_iota(jnp.int32, (n, n), 0) == lax.broadcasted_iota(jnp.int32, (n, n), 1)).astype(F32)
    hd = range(len(a))
    p = [_dg(a[h], a[h], 1, 0, 3) for h in hd]
    x = [(eye - a[h]) + p[h] - _dg(a[h], p[h], 1, 0, False) for h in hd]
    for _ in range(4):
        p = [_dg(p[h], p[h], 1, 0, False) for h in hd]
        x = [x[h] + p[h] + _dg(x[h] - eye, p[h], 1, 0, False) for h in hd]
    return tuple(x)


@jax.custom_vjp
def _inv_unit_lower(a):
    return _neumann(a)


def _inv_unit_lower_bwd(x, g):
    t = [_dg(x[h], g[h], 0, 0, 3) for h in range(len(x))]
    return (tuple(-_dg(t[h], x[h], 1, 1, 3) for h in range(len(x))),)


_inv_unit_lower.defvjp(lambda a: (_neumann(a),) * 2, _inv_unit_lower_bwd)


@jax.custom_vjp
def _kept_inverse(a, x):
    return x


_kept_inverse.defvjp(lambda a, x: (x, x),
                     lambda x, g: _inv_unit_lower_bwd(x, g) + (tuple(jnp.zeros_like(e) for e in x),))


def mm(a, b, *, tb=False, add=None, norm_gain=None, out_dtype=F32, name):
    m, kdim = a.shape
    (n, kb) = b.shape if tb else b.shape[::-1]
    assert kdim == kb, (a.shape, b.shape, tb)
    tm = _tile(m, 1408, 128) if m % 128 == 0 and m <= 4096 else _tile(m, 640, 64)
    tn = _tile(n, 3072 if (kdim <= 1024 and not tb and add is None) else 1536, 128)
    tk = kdim if kdim <= 3072 else _tile(kdim, 1664 if a.dtype == b.dtype == BF16 else 640, 128)
    nk = kdim // tk
    dims = (((1,), (1 if tb else 0,)), ((), ()))

    normed = norm_gain is not None
    assert not normed or (tn == n and nk == 1 and tm % 128 == 0), (name, tm, tn, nk)

    def body(*refs):
        refs = list(refs)
        a_ref, b_ref = refs[0], refs[1]
        add_ref = refs.pop(2) if add is not None else None
        gain_ref = refs.pop(2) if normed else None
        o_ref, acc = refs[2], refs[-1]
        k = pl.program_id(2)
        part = lax.dot_general(a_ref[...].astype(BF16), b_ref[...].astype(BF16), dims, preferred_element_type=F32)

        def finish(r):
            if add is not None:
                r = r + add_ref[...]
            o_ref[...] = r.astype(out_dtype)
            if normed:
                u = _rms(r, gain_ref[...])
                refs[3][...] = u.astype(BF16)
                refs[4][...] = u.T.astype(BF16)

        if nk == 1:
            finish(part)
        else:
            @pl.when(k == 0)
            def _():
                acc[...] = part

            @pl.when(jnp.logical_and(k > 0, k < nk - 1))
            def _():
                acc[...] += part

            @pl.when(k == nk - 1)
            def _():
                finish(acc[...] + part)

    a_spec = pl.BlockSpec((tm, tk), lambda j, i, k: (i, k))
    b_spec = pl.BlockSpec((tn, tk), lambda j, i, k: (j, k)) if tb else pl.BlockSpec((tk, tn), lambda j, i, k: (k, j))
    in_specs = [a_spec, b_spec]
    args = [a, b]
    if add is not None:
        in_specs.append(pl.BlockSpec((tm, tn), lambda j, i, k: (i, j)))
        args.append(add)
    out_specs = pl.BlockSpec((tm, tn), lambda j, i, k: (i, j))
    out_shape = jax.ShapeDtypeStruct((m, n), out_dtype)
    if normed:
        in_specs.append(pl.BlockSpec((1, n), lambda j, i, k: (0, 0)))
        args.append(norm_gain)
        out_specs = (out_specs, out_specs, pl.BlockSpec((n, tm), lambda j, i, k: (0, i)))
        out_shape = (out_shape, jax.ShapeDtypeStruct((m, n), BF16), jax.ShapeDtypeStruct((n, m), BF16))
    return pl.pallas_call(
        body, grid=(n // tn, m // tm, nk), in_specs=in_specs, out_specs=out_specs, out_shape=out_shape,
        scratch_shapes=[pltpu.VMEM((tm, tn) if nk > 1 else (8, 128), F32)],
        compiler_params=_params("parallel", "parallel", "arbitrary"), name=name,
    )(*args)


def mm_nt_sum(pairs, name):
    m, n = pairs[0][0].shape[0], pairs[0][1].shape[0]
    tm = _tile(m, 640, 64)
    count = len(pairs)

    def body(*refs):
        o_ref = refs[2 * count]
        total = None
        for p in range(count):
            part = lax.dot_general(refs[2 * p][...].astype(BF16), refs[2 * p + 1][...].astype(BF16),
                                   (((1,), (1,)), ((), ())), preferred_element_type=F32)
            total = part if total is None else total + part
        o_ref[...] = total

    in_specs, args = [], []
    for a, b in pairs:
        assert a.shape == (m, b.shape[1]) and b.shape[0] == n, (a.shape, b.shape)
        in_specs += [pl.BlockSpec((tm, a.shape[1]), lambda i: (i, 0)),
                     pl.BlockSpec(b.shape, lambda i: (0, 0), pipeline_mode=pl.Buffered(1))]
        args += [a, b]
    return pl.pallas_call(
        body, grid=(m // tm,), in_specs=in_specs, out_specs=pl.BlockSpec((tm, n), lambda i: (i, 0)),
        out_shape=jax.ShapeDtypeStruct((m, n), F32), compiler_params=_params("parallel"), name=name,
    )(*args)


def _rms(x, gain):
    return x * lax.rsqrt(jnp.mean(x * x, axis=-1, keepdims=True) + NORM_EPS) * gain


def rms_fwd(h, gain, name):
    t = h.shape[0]
    ts = _tile(t, 640, 128)

    def body(h_ref, g_ref, u_ref, ut_ref):
        u = _rms(h_ref[...], g_ref[...])
        u_ref[...] = u.astype(BF16)
        ut_ref[...] = u.T.astype(BF16)

    return pl.pallas_call(
        body, grid=(t // ts,),
        in_specs=[pl.BlockSpec((ts, D_MODEL), lambda i: (i, 0)), pl.BlockSpec((1, D_MODEL), lambda i: (0, 0))],
        out_specs=(pl.BlockSpec((ts, D_MODEL), lambda i: (i, 0)), pl.BlockSpec((D_MODEL, ts), lambda i: (0, i))),
        out_shape=(jax.ShapeDtypeStruct((t, D_MODEL), BF16), jax.ShapeDtypeStruct((D_MODEL, t), BF16)),
        compiler_params=_params("parallel"), name=name,
    )(h, gain)


def rms_bwd(h, gain, du, dres, name):
    t = h.shape[0]
    ts = _tile(t, 640, 64)

    def body(h_ref, g_ref, du_ref, dres_ref, dh_ref, dhb_ref, dg_ref):
        i = pl.program_id(0)
        _, vjp = jax.vjp(_rms, h_ref[...], g_ref[...])
        dx, dg = vjp(du_ref[...])
        row = i * ts + lax.broadcasted_iota(jnp.int32, (ts, 1), 0)
        dh = jnp.where(row >= LEAD, dx + dres_ref[...], 0.0)
        dh_ref[...] = dh
        dhb_ref[...] = dh.astype(BF16)

        @pl.when(i == 0)
        def _():
            dg_ref[...] = jnp.zeros_like(dg_ref)

        dg_ref[...] += dg

    row_spec = pl.BlockSpec((ts, D_MODEL), lambda i: (i, 0))
    vec_spec = pl.BlockSpec((1, D_MODEL), lambda i: (0, 0))
    return pl.pallas_call(
        body, grid=(t // ts,), in_specs=[row_spec, vec_spec, row_spec, row_spec],
        out_specs=(row_spec, row_spec, vec_spec),
        out_shape=(jax.ShapeDtypeStruct((t, D_MODEL), F32), jax.ShapeDtypeStruct((t, D_MODEL), BF16),
                   jax.ShapeDtypeStruct((1, D_MODEL), F32)),
        compiler_params=_params("arbitrary"), name=name,
    )(h, gain, du, dres)


def loss_head(h, gain, target, name):
    t = h.shape[0]
    ts = _tile(t, 640, 64)

    def body(h_ref, g_ref, t_ref, loss_ref, dh_ref, dhb_ref, dg_ref):
        i = pl.program_id(0)
        row = i * ts + lax.broadcasted_iota(jnp.int32, (ts, 1), 0)
        keep = jnp.logical_and(row >= LEAD + N_META, row < t - TAIL)
        tgt = t_ref[...]

        def f(x, g):
            err = jnp.where(keep, _rms(x, g) - tgt, 0.0)
            per_row = jnp.mean(err * err, axis=-1, keepdims=True)
            return 0.5 * jnp.sum(per_row, axis=0, keepdims=True)

        val, vjp = jax.vjp(f, h_ref[...], g_ref[...])
        dx, dg = vjp(jnp.ones((1, 1), F32))
        dh_ref[...] = dx
        dhb_ref[...] = dx.astype(BF16)

        @pl.when(i == 0)
        def _():
            dg_ref[...] = jnp.zeros_like(dg_ref)
            loss_ref[...] = jnp.zeros_like(loss_ref)

        dg_ref[...] += dg
        loss_ref[...] += jnp.broadcast_to(val, (1, 128))

    row_spec = pl.BlockSpec((ts, D_MODEL), lambda i: (i, 0))
    vec_spec = pl.BlockSpec((1, D_MODEL), lambda i: (0, 0))
    return pl.pallas_call(
        body, grid=(t // ts,), in_specs=[row_spec, vec_spec, row_spec],
        out_specs=(pl.BlockSpec((1, 128), lambda i: (0, 0)), row_spec, row_spec, vec_spec),
        out_shape=(jax.ShapeDtypeStruct((1, 128), F32), jax.ShapeDtypeStruct((t, D_MODEL), F32),
                   jax.ShapeDtypeStruct((t, D_MODEL), BF16), jax.ShapeDtypeStruct((1, D_MODEL), F32)),
        compiler_params=_params("arbitrary"), name=name,
    )(h, gain, target)


def _rows_down(a, s):
    return a if s == 0 else pltpu.roll(a, s, axis=0)


def _rows_up(a, s):
    return a if s == 0 else pltpu.roll(a, a.shape[0] - s, axis=0)


def conv_fwd(x, w, name):
    t, width = x.shape
    k = w.shape[0]
    ts = _tile(t, 640, 64)
    tw = _tile(width, 1536, 128)
    hb = ts // 8

    def body(x_ref, halo_ref, w_ref, o_ref, buf):
        i = pl.program_id(0)
        buf[0:8, :] = jnp.where(i > 0, halo_ref[...], 0.0)
        buf[8:, :] = x_ref[...]
        ext = buf[...]
        wv = w_ref[...]
        acc = _rows_down(ext, k - 1)[8:, :] * wv[0:1, :]
        for j in range(1, k):
            acc = acc + _rows_down(ext, k - 1 - j)[8:, :] * wv[j:j + 1, :]
        o_ref[...] = acc

    return pl.pallas_call(
        body, grid=(t // ts, width // tw),
        in_specs=[pl.BlockSpec((ts, tw), lambda i, j: (i, j)),
                  pl.BlockSpec((8, tw), lambda i, j: (jnp.maximum(i * hb - 1, 0), j)),
                  pl.BlockSpec((k, tw), lambda i, j: (0, j))],
        out_specs=pl.BlockSpec((ts, tw), lambda i, j: (i, j)),
        out_shape=jax.ShapeDtypeStruct((t, width), F32),
        scratch_shapes=[pltpu.VMEM((ts + 8, tw), F32)],
        compiler_params=_params("parallel", "parallel"), name=name,
    )(x, x, w)


def conv_bwd(x, dc, w, name):
    t, width = x.shape
    k = w.shape[0]
    ts = _tile(t, 640, 64)
    tw = _tile(width, 1536, 128)
    hb = ts // 8
    nt = t // ts

    def body(x_ref, xh_ref, dc_ref, dch_ref, w_ref, dx_ref, dw_ref, xbuf, dbuf):
        i = pl.program_id(1)
        xbuf[0:8, :] = jnp.where(i > 0, xh_ref[...], 0.0)
        xbuf[8:, :] = x_ref[...]
        d = dc_ref[...]
        dbuf[0:ts, :] = d
        dbuf[ts:, :] = jnp.where(i < nt - 1, dch_ref[...], 0.0)
        wv = w_ref[...]
        ext_x, ext_d = xbuf[...], dbuf[...]
        acc = _rows_up(ext_d, k - 1)[0:ts, :] * wv[0:1, :]
        for j in range(1, k):
            acc = acc + _rows_up(ext_d, k - 1 - j)[0:ts, :] * wv[j:j + 1, :]
        dx_ref[...] = acc.astype(BF16)

        @pl.when(i == 0)
        def _():
            dw_ref[...] = jnp.zeros_like(dw_ref)

        for j in range(k):
            dw_ref[j:j + 1, :] += jnp.sum(d * _rows_down(ext_x, k - 1 - j)[8:, :], axis=0, keepdims=True)

    return pl.pallas_call(
        body, grid=(width // tw, nt),
        in_specs=[pl.BlockSpec((ts, tw), lambda j, i: (i, j)),
                  pl.BlockSpec((8, tw), lambda j, i: (jnp.maximum(i * hb - 1, 0), j)),
                  pl.BlockSpec((ts, tw), lambda j, i: (i, j)),
                  pl.BlockSpec((8, tw), lambda j, i: (jnp.minimum((i + 1) * hb, t // 8 - 1), j)),
                  pl.BlockSpec((k, tw), lambda j, i: (0, j))],
        out_specs=(pl.BlockSpec((ts, tw), lambda j, i: (i, j)), pl.BlockSpec((8, tw), lambda j, i: (0, j))),
        out_shape=(jax.ShapeDtypeStruct((t, width), BF16), jax.ShapeDtypeStruct((8, width), F32)),
        scratch_shapes=[pltpu.VMEM((ts + 8, tw), F32), pltpu.VMEM((ts + 8, tw), F32)],
        compiler_params=_params("parallel", "arbitrary"), name=name,
    )(x, x, dc, dc, w)


def _pool_count(pos, win):
    return jnp.clip(pos + 1, 1, win).astype(F32)


def poolwin_fwd(p, name):
    t = p.shape[0]
    ts = _tile(t, 640, 64)
    hb = ts // 16

    def body(p_ref, halo_ref, o_ref, buf):
        i = pl.program_id(0)
        buf[0:16, :] = jnp.where(i > 0, halo_ref[...], 0.0)
        buf[16:, :] = p_ref[...]
        pos = i * ts + lax.broadcasted_iota(jnp.int32, (ts, 1), 0) - LEAD
        ext = buf[...]
        own = ext[16:, :]
        sums, span = ext, 1
        for gi, win in enumerate(POOL_WINDOWS):
            while span < win:
                sums = sums + _rows_down(sums, span)
                span *= 2
            cols = slice(gi * 128, (gi + 1) * 128)
            o_ref[:, cols] = sums[16:, cols] / _pool_count(pos, win) - own[:, cols]

    return pl.pallas_call(
        body, grid=(t // ts,),
        in_specs=[pl.BlockSpec((ts, POOL_WIDTH), lambda i: (i, 0)),
                  pl.BlockSpec((16, POOL_WIDTH), lambda i: (jnp.maximum(i * hb - 1, 0), 0))],
        out_specs=pl.BlockSpec((ts, POOL_WIDTH), lambda i: (i, 0)),
        out_shape=jax.ShapeDtypeStruct((t, POOL_WIDTH), F32),
        scratch_shapes=[pltpu.VMEM((ts + 16, POOL_WIDTH), F32)],
        compiler_params=_params("parallel"), name=name,
    )(p, p)


def poolwin_bwd(dpooled, name):
    t = dpooled.shape[0]
    ts = _tile(t, 640, 64)
    hb = ts // 16
    nt = t // ts

    def body(d_ref, halo_ref, o_ref, buf):
        i = pl.program_id(0)
        buf[0:ts, :] = d_ref[...]
        buf[ts:, :] = jnp.where(i < nt - 1, halo_ref[...], 0.0)
        pos = i * ts + lax.broadcasted_iota(jnp.int32, (ts + 16, 1), 0) - LEAD
        ext = buf[...]
        for gi, win in enumerate(POOL_WINDOWS):
            cols = slice(gi * 128, (gi + 1) * 128)
            sums, span = ext[:, cols] / _pool_count(pos, win), 1
            while span < win:
                sums = sums + _rows_up(sums, span)
                span *= 2
            o_ref[:, cols] = (sums[0:ts, :] - ext[0:ts, cols]).astype(BF16)

    return pl.pallas_call(
        body, grid=(nt,),
        in_specs=[pl.BlockSpec((ts, POOL_WIDTH), lambda i: (i, 0)),
                  pl.BlockSpec((16, POOL_WIDTH), lambda i: (jnp.minimum((i + 1) * hb, t // 16 - 1), 0))],
        out_specs=pl.BlockSpec((ts, POOL_WIDTH), lambda i: (i, 0)),
        out_shape=jax.ShapeDtypeStruct((t, POOL_WIDTH), BF16),
        scratch_shapes=[pltpu.VMEM((ts + 16, POOL_WIDTH), F32)],
        compiler_params=_params("parallel"), name=name,
    )(dpooled, dpooled)


def _mix(y_a, gpre, pooled, w_pool, scale):
    parts = [_nn(pooled[:, g * 128:(g + 1) * 128], w_pool[g]) for g in range(4)]
    y_b = jnp.concatenate(parts, axis=1) * scale
    return jax.nn.sigmoid(gpre[:, :D_MODEL]) * y_a + jax.nn.sigmoid(gpre[:, D_MODEL:]) * y_b


def _mix_specs(ts):
    return [pl.BlockSpec((ts, D_MODEL), lambda i: (i, 0)), pl.BlockSpec((ts, 2 * D_MODEL), lambda i: (i, 0)),
            pl.BlockSpec((ts, POOL_WIDTH), lambda i: (i, 0)), pl.BlockSpec((4, 128, 256), lambda i: (0, 0, 0)),
            pl.BlockSpec((1, D_MODEL), lambda i: (0, 0))]


def mix_fwd(y_a, gpre, pooled, w_pool, scale, name):
    t = y_a.shape[0]
    ts = _tile(t, 640, 128)

    def body(ya_ref, g_ref, p_ref, w_ref, s_ref, o_ref, ot_ref):
        y = _mix(ya_ref[...], g_ref[...], p_ref[...], w_ref[...], s_ref[...])
        o_ref[...] = y.astype(BF16)
        ot_ref[...] = y.T.astype(BF16)

    return pl.pallas_call(
        body, grid=(t // ts,), in_specs=_mix_specs(ts),
        out_specs=(pl.BlockSpec((ts, D_MODEL), lambda i: (i, 0)), pl.BlockSpec((D_MODEL, ts), lambda i: (0, i))),
        out_shape=(jax.ShapeDtypeStruct((t, D_MODEL), BF16), jax.ShapeDtypeStruct((D_MODEL, t), BF16)),
        compiler_params=_params("parallel"), name=name,
    )(y_a, gpre, pooled, w_pool, scale)


def mix_bwd(y_a, gpre, pooled, w_pool, scale, dy, name):
    t = y_a.shape[0]
    ts = _tile(t, 320, 64)

    def body(ya_ref, g_ref, p_ref, w_ref, s_ref, dy_ref, dya_ref, dg_ref, dp_ref, dw_ref, ds_ref):
        i = pl.program_id(0)
        _, vjp = jax.vjp(_mix, ya_ref[...], g_ref[...], p_ref[...], w_ref[...], s_ref[...])
        dya, dg, dp, dw, ds = vjp(dy_ref[...])
        dya_ref[...] = dya
        dg_ref[...] = dg.astype(BF16)
        dp_ref[...] = dp

        @pl.when(i == 0)
        def _():
            dw_ref[...] = jnp.zeros_like(dw_ref)
            ds_ref[...] = jnp.zeros_like(ds_ref)

        dw_ref[...] += dw
        ds_ref[...] += ds

    specs = _mix_specs(ts)
    return pl.pallas_call(
        body, grid=(t // ts,), in_specs=specs + [specs[0]],
        out_specs=(specs[0], specs[1], specs[2], specs[3], specs[4]),
        out_shape=(jax.ShapeDtypeStruct((t, D_MODEL), F32), jax.ShapeDtypeStruct((t, 2 * D_MODEL), BF16),
                   jax.ShapeDtypeStruct((t, POOL_WIDTH), F32), jax.ShapeDtypeStruct((4, 128, 256), F32),
                   jax.ShapeDtypeStruct((1, D_MODEL), F32)),
        compiler_params=_params("arbitrary"), name=name,
    )(y_a, gpre, pooled, w_pool, scale, dy)


def _ffn_act(cg, cv):
    return _silu(cg) * cv


FFN_MID_VMEM_BYTES = 58 * 1024 * 1024


def ffn_mid_fwd(hg, hv, wg, wv, name):
    t, width = hg.shape
    k = wg.shape[0]
    ts = _tile(t, 640, 128)
    tw = _tile(width, 1536, 128)
    hb = ts // 8

    def body(hg_ref, hgp_ref, hv_ref, hvp_ref, wg_ref, wv_ref, cg_ref, cv_ref, act_ref, actt_ref, xg, xv):
        i = pl.program_id(0)
        convs = []
        for x_ref, xp_ref, buf, w_ref, c_ref in ((hg_ref, hgp_ref, xg, wg_ref, cg_ref),
                                                 (hv_ref, hvp_ref, xv, wv_ref, cv_ref)):
            buf[0:8, :] = jnp.where(i > 0, xp_ref[...], 0.0)
            buf[8:, :] = x_ref[...]
            ext, wt = buf[...], w_ref[...]
            acc = _rows_down(ext, k - 1)[8:, :] * wt[0:1, :]
            for j in range(1, k):
                acc = acc + _rows_down(ext, k - 1 - j)[8:, :] * wt[j:j + 1, :]
            c_ref[...] = acc
            convs.append(acc)
        act = _ffn_act(*convs)
        act_ref[...] = act.astype(BF16)
        actt_ref[...] = act.T.astype(BF16)

    tile = pl.BlockSpec((ts, tw), lambda i, j: (i, j))
    prev = pl.BlockSpec((8, tw), lambda i, j: (jnp.maximum(i * hb - 1, 0), j))
    taps = pl.BlockSpec((k, tw), lambda i, j: (0, j))
    return pl.pallas_call(
        body, grid=(t // ts, width // tw), in_specs=[tile, prev, tile, prev, taps, taps],
        out_specs=(tile, tile, tile, pl.BlockSpec((tw, ts), lambda i, j: (j, i))),
        out_shape=(jax.ShapeDtypeStruct((t, width), F32), jax.ShapeDtypeStruct((t, width), F32),
                   jax.ShapeDtypeStruct((t, width), BF16), jax.ShapeDtypeStruct((width, t), BF16)),
        scratch_shapes=[pltpu.VMEM((ts + 8, tw), F32)] * 2,
        compiler_params=pltpu.CompilerParams(dimension_semantics=("parallel", "parallel"),
                                             vmem_limit_bytes=FFN_MID_VMEM_BYTES), name=name,
    )(hg, hg, hv, hv, wg, wv)


def ffn_mid_bwd(hg, hv, cg, cv, dact, wg, wv, name):
    t, width = hg.shape
    k = wg.shape[0]
    ts = _tile(t, 320, 64)
    tw = _tile(width, 1536, 128)
    hb = ts // 8
    nt = t // ts

    def body(hg_ref, hgp_ref, hv_ref, hvp_ref, cg_ref, cgn_ref, cv_ref, cvn_ref, da_ref, dan_ref, wg_ref, wv_ref,
             dhg_ref, dhv_ref, dwg_ref, dwv_ref, xg, xv, dg, dv):
        i = pl.program_id(1)
        behind = i < nt - 1

        def d_conv(c_g, c_v, d_a):
            _, vjp = jax.vjp(_ffn_act, c_g, c_v)
            return vjp(d_a)

        dcg, dcv = d_conv(cg_ref[...], cv_ref[...], da_ref[...])
        dcg_n, dcv_n = d_conv(cgn_ref[...], cvn_ref[...], jnp.where(behind, dan_ref[...], 0.0))

        @pl.when(i == 0)
        def _():
            dwg_ref[...] = jnp.zeros_like(dwg_ref)
            dwv_ref[...] = jnp.zeros_like(dwv_ref)

        for x_ref, xp_ref, xbuf, dbuf, d, d_n, w_ref, dx_ref, dw_ref in (
                (hg_ref, hgp_ref, xg, dg, dcg, dcg_n, wg_ref, dhg_ref, dwg_ref),
                (hv_ref, hvp_ref, xv, dv, dcv, dcv_n, wv_ref, dhv_ref, dwv_ref)):
            xbuf[0:8, :] = jnp.where(i > 0, xp_ref[...], 0.0)
            xbuf[8:, :] = x_ref[...]
            dbuf[0:ts, :] = d
            dbuf[ts:, :] = jnp.where(behind, d_n, 0.0)
            wt = w_ref[...]
            ext_x, ext_d = xbuf[...], dbuf[...]
            acc = _rows_up(ext_d, k - 1)[0:ts, :] * wt[0:1, :]
            for j in range(1, k):
                acc = acc + _rows_up(ext_d, k - 1 - j)[0:ts, :] * wt[j:j + 1, :]
            dx_ref[...] = acc.astype(BF16)
            for j in range(k):
                dw_ref[j:j + 1, :] += jnp.sum(d * _rows_down(ext_x, k - 1 - j)[8:, :], axis=0, keepdims=True)

    tile = pl.BlockSpec((ts, tw), lambda j, i: (i, j))
    prev = pl.BlockSpec((8, tw), lambda j, i: (jnp.maximum(i * hb - 1, 0), j))
    nxt = pl.BlockSpec((8, tw), lambda j, i: (jnp.minimum((i + 1) * hb, t // 8 - 1), j))
    taps = pl.BlockSpec((k, tw), lambda j, i: (0, j))
    dw_spec = pl.BlockSpec((8, tw), lambda j, i: (0, j))
    return pl.pallas_call(
        body, grid=(width // tw, nt),
        in_specs=[tile, prev, tile, prev, tile, nxt, tile, nxt, tile, nxt, taps, taps],
        out_specs=(tile, tile, dw_spec, dw_spec),
        out_shape=(jax.ShapeDtypeStruct((t, width), BF16), jax.ShapeDtypeStruct((t, width), BF16),
                   jax.ShapeDtypeStruct((8, width), F32), jax.ShapeDtypeStruct((8, width), F32)),
        scratch_shapes=[pltpu.VMEM((ts + 8, tw), F32)] * 4,
        compiler_params=_params("parallel", "arbitrary"), name=name,
    )(hg, hg, hv, hv, cg, cg, cv, cv, dact, dact, wg, wv)


def _gdn_chunk(c, z, ba, pa, pdt, hn, s, *, valid, inverse=None, with_inverse=False):
    r = lax.broadcasted_iota(jnp.int32, (CHUNK, CHUNK), 0)
    q_ = lax.broadcasted_iota(jnp.int32, (CHUNK, CHUNK), 1)
    causal = r >= q_
    strict = r > q_
    tril = causal.astype(F32)
    triu = (r <= q_).astype(F32)
    lane = lax.broadcasted_iota(jnp.int32, (CHUNK, 128), 1)

    decay_log = -jnp.exp(pa) * _softplus(ba + pdt)
    bg = jnp.where(lane < HEADS, jax.nn.sigmoid(ba), jnp.where(lane < 2 * HEADS, decay_log, 0.0))
    bg = jnp.where(valid, bg, 0.0)
    gc = _hnn(tril, bg)
    gct = _hnn(bg.T, triu)
    eg = jnp.exp(gc)
    glast = gc[CHUNK - 1:CHUNK, :]
    ekd = jnp.exp(glast - gc)
    gtot = jnp.exp(glast)

    hd = range(HEADS)
    hs = [slice(h * HEAD_DIM, (h + 1) * HEAD_DIM) for h in hd]
    gl = [slice(HEADS + h, HEADS + h + 1) for h in hd]
    q = [_silu(c[:, hs[h]]) for h in hd]
    k = [_silu(c[:, D_MODEL + h * HEAD_DIM:D_MODEL + (h + 1) * HEAD_DIM]) for h in hd]
    v = [_silu(c[:, 2 * D_MODEL + h * HEAD_DIM:2 * D_MODEL + (h + 1) * HEAD_DIM]) for h in hd]
    q = [q[h] * lax.rsqrt(jnp.sum(q[h] * q[h], axis=-1, keepdims=True) + NORM_EPS) * (HEAD_DIM ** -0.5) for h in hd]
    k = [k[h] * lax.rsqrt(jnp.sum(k[h] * k[h], axis=-1, keepdims=True) + NORM_EPS) for h in hd]
    beta = [bg[:, h:h + 1] for h in hd]
    decay = [jnp.exp(jnp.where(causal, gc[:, gl[h]] - gct[gl[h], :], -1e30)) for h in hd]
    kb = [k[h] * beta[h] for h in hd]
    a = [jnp.where(strict, _nt(kb[h], k[h]) * decay[h], 0.0) for h in hd]
    qk = [jnp.where(causal, _nt(q[h], k[h]) * decay[h], 0.0) for h in hd]
    x = _inv_unit_lower(tuple(a)) if inverse is None else _kept_inverse(tuple(a), tuple(inverse))
    u = [_nn(x[h], v[h] * beta[h]) for h in hd]
    w = [_nn(x[h], kb[h] * eg[:, gl[h]]) for h in hd]
    v_new = [u[h] - _nn(w[h], s[h]) for h in hd]
    o = [_nn(q[h] * eg[:, gl[h]], s[h]) + _nn(qk[h], v_new[h]) for h in hd]
    states = [s[h] * gtot[:, gl[h]] + _tn(k[h] * ekd[:, gl[h]], v_new[h]) for h in hd]
    o = [o[h] * lax.rsqrt(jnp.mean(o[h] * o[h], axis=-1, keepdims=True) + NORM_EPS) * hn * _silu(z[:, hs[h]])
         for h in hd]
    if with_inverse:
        return jnp.concatenate(o, axis=1), tuple(states), x
    return jnp.concatenate(o, axis=1), tuple(states)


GDN_FWD_CHUNKS = 10
GDN_BWD_CHUNKS = 2


def _chunk_valid(n, t):
    row = n * CHUNK + lax.broadcasted_iota(jnp.int32, (CHUNK, 1), 0)
    return jnp.logical_and(row >= LEAD, row < t - TAIL)


def gdn_fwd(c, z, ba, pa, pdt, hn, name, ba_block=0):
    t = c.shape[0]
    n_chunks = t // CHUNK
    per_step = GDN_FWD_CHUNKS if n_chunks % GDN_FWD_CHUNKS == 0 else 1
    rows_per_step = per_step * CHUNK

    def body(c_ref, z_ref, ba_ref, pa_ref, pdt_ref, hn_ref, y_ref, ss_ref, inv_ref, state):
        step = pl.program_id(0)

        @pl.when(step == 0)
        def _():
            state[...] = jnp.zeros_like(state)

        s = tuple(state[h] for h in range(HEADS))
        for j in range(per_step):
            rows = pl.ds(j * CHUNK, CHUNK)
            for h in range(HEADS):
                ss_ref[j, h] = s[h]
            y, s, inv = _gdn_chunk(c_ref[rows, :], z_ref[rows, :], ba_ref[rows, :], pa_ref[...], pdt_ref[...],
                                   hn_ref[...], s, valid=_chunk_valid(step * per_step + j, t), with_inverse=True)
            y_ref[rows, :] = y
            for h in range(HEADS):
                inv_ref[j, h] = inv[h]
        for h in range(HEADS):
            state[h] = s[h]

    vec = pl.BlockSpec((1, 128), lambda n: (0, 0))
    return pl.pallas_call(
        body, grid=(n_chunks // per_step,),
        in_specs=[pl.BlockSpec((rows_per_step, QKV_DIM), lambda n: (n, 0)),
                  pl.BlockSpec((rows_per_step, D_MODEL), lambda n: (n, 0)),
                  pl.BlockSpec((rows_per_step, 128), lambda n: (n, ba_block)), vec, vec, vec],
        out_specs=(pl.BlockSpec((rows_per_step, D_MODEL), lambda n: (n, 0)),
                   pl.BlockSpec((per_step, HEADS, HEAD_DIM, HEAD_DIM), lambda n: (n, 0, 0, 0)),
                   pl.BlockSpec((per_step, HEADS, CHUNK, CHUNK), lambda n: (n, 0, 0, 0))),
        out_shape=(jax.ShapeDtypeStruct((t, D_MODEL), F32),
                   jax.ShapeDtypeStruct((n_chunks, HEADS, HEAD_DIM, HEAD_DIM), F32),
                   jax.ShapeDtypeStruct((n_chunks, HEADS, CHUNK, CHUNK), F32)),
        scratch_shapes=[pltpu.VMEM((HEADS, HEAD_DIM, HEAD_DIM), F32)],
        compiler_params=_params("arbitrary"), name=name,
    )(c, z, ba, pa, pdt, hn)


def gdn_bwd(c, z, ba, pa, pdt, hn, starts, inverses, dy, name, ba_block=0):
    t = c.shape[0]
    per_step = GDN_BWD_CHUNKS if (t // CHUNK) % GDN_BWD_CHUNKS == 0 else 1
    n_steps = t // CHUNK // per_step
    rows_per_step = per_step * CHUNK

    def body(c_ref, z_ref, ba_ref, pa_ref, pdt_ref, hn_ref, ss_ref, inv_ref, dy_ref,
             dc_ref, dz_ref, dba_ref, dpa_ref, dpdt_ref, dhn_ref, dstate):
        step = pl.program_id(0)

        @pl.when(step == 0)
        def _():
            dstate[...] = jnp.zeros_like(dstate)
            dpa_ref[...] = jnp.zeros_like(dpa_ref)
            dpdt_ref[...] = jnp.zeros_like(dpdt_ref)
            dhn_ref[...] = jnp.zeros_like(dhn_ref)

        ds = tuple(dstate[h] for h in range(HEADS))
        for j in reversed(range(per_step)):
            rows = pl.ds(j * CHUNK, CHUNK)
            f = functools.partial(_gdn_chunk, valid=_chunk_valid((n_steps - 1 - step) * per_step + j, t),
                                  inverse=tuple(inv_ref[j, h] for h in range(HEADS)))
            _, vjp = jax.vjp(f, c_ref[rows, :], z_ref[rows, :], ba_ref[rows, :], pa_ref[...], pdt_ref[...], hn_ref[...],
                             tuple(ss_ref[j, h] for h in range(HEADS)))
            dc, dz, dba, dpa, dpdt, dhn, ds = vjp((dy_ref[rows, :], ds))
            dc_ref[rows, :] = dc
            dz_ref[rows, :] = dz.astype(BF16)
            dba_ref[rows, :] = dba.astype(BF16)
            dpa_ref[...] += dpa
            dpdt_ref[...] += dpdt
            dhn_ref[...] += dhn
        for h in range(HEADS):
            dstate[h] = ds[h]

    def rev(width, block=0):
        return pl.BlockSpec((rows_per_step, width), lambda s: (n_steps - 1 - s, block))

    vec = pl.BlockSpec((1, 128), lambda s: (0, 0))
    vec_shape = jax.ShapeDtypeStruct((1, 128), F32)
    return pl.pallas_call(
        body, grid=(n_steps,),
        in_specs=[rev(QKV_DIM), rev(D_MODEL), rev(128, ba_block), vec, vec, vec,
                  pl.BlockSpec((per_step, HEADS, HEAD_DIM, HEAD_DIM), lambda s: (n_steps - 1 - s, 0, 0, 0)),
                  pl.BlockSpec((per_step, HEADS, CHUNK, CHUNK), lambda s: (n_steps - 1 - s, 0, 0, 0)),
                  rev(D_MODEL)],
        out_specs=(rev(QKV_DIM), rev(D_MODEL), rev(128), vec, vec, vec),
        out_shape=(jax.ShapeDtypeStruct((t, QKV_DIM), F32), jax.ShapeDtypeStruct((t, D_MODEL), BF16),
                   jax.ShapeDtypeStruct((t, 128), BF16), vec_shape, vec_shape, vec_shape),
        scratch_shapes=[pltpu.VMEM((HEADS, HEAD_DIM, HEAD_DIM), F32)],
        compiler_params=_params("arbitrary"), name=name,
    )(c, z, ba, pa, pdt, hn, starts, inverses, dy)


def _layer_fwd(h, u, ut, w, next_gain, tag):
    pq = mm(u, w["wqkv"], name=f"{tag}_mm_qkv")
    pz = mm(u, w["wz"], name=f"{tag}_mm_z")
    pg = mm(u, w["wg"], name=f"{tag}_mm_gate")
    pba = mm(u, w["wpb"], name=f"{tag}_mm_pool_ba")
    cq = conv_fwd(pq, w["conv_qkv"], f"{tag}_conv_qkv")
    ya, starts, inverses = gdn_fwd(cq, pz, pba, w["pa"], w["pdt"], w["head_norm"], f"{tag}_gdn", ba_block=BA_BLOCK)
    pooled = poolwin_fwd(pba, f"{tag}_poolwin")
    y, yt = mix_fwd(ya, pg, pooled, w["w_pool"], w["pool_scale"], f"{tag}_mix")
    h1, u2, u2t = mm(y, w["wout"], add=h, norm_gain=w["norm_ffn"], name=f"{tag}_mm_out")
    hg = mm(u2, w["wupg"], name=f"{tag}_mm_upg")
    hv = mm(u2, w["wupv"], name=f"{tag}_mm_upv")
    cg, cv, act, actt = ffn_mid_fwd(hg, hv, w["conv_g"], w["conv_v"], f"{tag}_ffn_mid")
    if next_gain is None:
        h2, nxt = mm(act, w["wdown"], add=h1, name=f"{tag}_mm_down"), None
    else:
        h2, *nxt = mm(act, w["wdown"], add=h1, norm_gain=next_gain, name=f"{tag}_mm_down")
    saved = dict(h=h, ut=ut, pq=pq, pz=pz, pg=pg, pba=pba, cq=cq, ya=ya, starts=starts, inverses=inverses,
                 pooled=pooled, yt=yt, h1=h1, u2t=u2t, hg=hg, hv=hv, cg=cg, cv=cv, actt=actt)
    return h2, nxt, saved


def _layer_bwd(dh2, dh2b, w, s, tag):
    g = {}
    dact = mm(dh2b, w["wdown"], tb=True, name=f"{tag}_bmm_down_x")
    g["wdown"] = mm(s["actt"], dh2b, name=f"{tag}_bmm_down_w")
    dhg, dhv, g["conv_g"], g["conv_v"] = ffn_mid_bwd(s["hg"], s["hv"], s["cg"], s["cv"], dact, w["conv_g"], w["conv_v"],
                                                     f"{tag}_ffn_mid_b")
    du2 = mm_nt_sum([(dhg, w["wupg"]), (dhv, w["wupv"])], f"{tag}_bmm_up_x")
    g["wupg"] = mm(s["u2t"], dhg, name=f"{tag}_bmm_upg_w")
    g["wupv"] = mm(s["u2t"], dhv, name=f"{tag}_bmm_upv_w")
    dh1, dh1b, g["norm_ffn"] = rms_bwd(s["h1"], w["norm_ffn"], du2, dh2, f"{tag}_rms_ffn_b")
    dy = mm(dh1b, w["wout"], tb=True, name=f"{tag}_bmm_out_x")
    g["wout"] = mm(s["yt"], dh1b, name=f"{tag}_bmm_out_w")
    dya, dpg, dpooled, g["w_pool"], g["pool_scale"] = mix_bwd(
        s["ya"], s["pg"], s["pooled"], w["w_pool"], w["pool_scale"], dy, f"{tag}_mix_b")
    dpp = poolwin_bwd(dpooled, f"{tag}_poolwin_b")
    dcq, dpz, dpba, g["pa"], g["pdt"], g["head_norm"] = gdn_bwd(
        s["cq"], s["pz"], s["pba"], w["pa"], w["pdt"], w["head_norm"], s["starts"], s["inverses"], dya, f"{tag}_gdn_b",
        ba_block=BA_BLOCK)
    dpb = jnp.concatenate([dpp, dpba], axis=1)
    dpq, g["conv_qkv"] = conv_bwd(s["pq"], dcq, w["conv_qkv"], f"{tag}_conv_qkv_b")
    du = mm_nt_sum([(dpq, w["wqkv"]), (dpz, w["wz"]), (dpg, w["wg"]), (dpb, w["wpb"])], f"{tag}_bmm_in_x")
    g["wqkv"] = mm(s["ut"], dpq, name=f"{tag}_bmm_qkv_w")
    g["wz"] = mm(s["ut"], dpz, name=f"{tag}_bmm_z_w")
    g["wg"] = mm(s["ut"], dpg, name=f"{tag}_bmm_gate_w")
    dwpb = mm(s["ut"], dpb, name=f"{tag}_bmm_pool_ba_w")
    g["wpl"], g["wba"] = dwpb[:, :POOL_WIDTH], dwpb[:, POOL_WIDTH:]
    dh, dhb, g["norm_mix"] = rms_bwd(s["h"], w["norm_mix"], du, dh1, f"{tag}_rms_mix_b")
    return dh, dhb, g


def local_step(h0, target, layers, norm_final):
    h = h0
    normed = rms_fwd(h0, layers[0]["norm_mix"], "l0_rms_mix")
    saved = []
    for li, w in enumerate(layers):
        next_gain = layers[li + 1]["norm_mix"] if li + 1 < len(layers) else None
        h, normed, s = _layer_fwd(h, *normed, w, next_gain, f"l{li}")
        saved.append(s)
    loss, dh, dhb, dnf = loss_head(h, norm_final, target, "loss_head")
    grads = [None] * len(layers)
    for li in reversed(range(len(layers))):
        dh, dhb, grads[li] = _layer_bwd(dh, dhb, layers[li], saved[li], f"l{li}")
    return loss, dh, grads, dnf


_Z0, _B0, _P0, _G0, _IN_DIM = 3072, 4096, 4112, 4624, 6672


def _lanes_8_to_15(v):
    return jnp.pad(v.reshape(1, HEADS).astype(F32), ((0, 0), (HEADS, 128 - 2 * HEADS)))


IN_PIECES = (("wqkv", 0, _Z0), ("wz", _Z0, _B0), ("wba", _B0, _P0), ("wpl", _P0, _G0), ("wg", _G0, _IN_DIM))
IN_SHARD = _IN_DIM // 4


def _overlaps(a, b, spans):
    return [(name, max(a, lo) - lo, min(b, hi) - max(a, lo)) for name, lo, hi in spans if max(a, lo) < min(b, hi)]


def _cat(parts):
    return parts[0] if len(parts) == 1 else jnp.concatenate(parts, axis=1)


def prep_layer(p):
    row = lambda v: v.reshape(1, -1).astype(F32)
    w_in, w_up = p["w_in"], p["w_up"]
    if not isinstance(w_in, (list, tuple)):
        w_in = [w_in[:, s * IN_SHARD:(s + 1) * IN_SHARD] for s in range(4)]
        w_up = [w_up[:, s * (D_FF // 2):(s + 1) * (D_FF // 2)] for s in range(4)]
    shards = [(s, s * IN_SHARD, (s + 1) * IN_SHARD) for s in range(4)]
    piece = {name: _cat([w_in[s][:, off:off + width].astype(BF16) for s, off, width in _overlaps(lo, hi, shards)])
             for name, lo, hi in IN_PIECES}
    return dict(
        wqkv=piece["wqkv"], wz=piece["wz"],
        wpb=jnp.concatenate([piece["wpl"], jnp.pad(piece["wba"], ((0, 0), (0, 128 - 2 * HEADS)))], axis=1),
        wg=piece["wg"], wout=p["w_out"].astype(BF16),
        wupg=_cat([w_up[0].astype(BF16), w_up[1].astype(BF16)]), wupv=_cat([w_up[2].astype(BF16), w_up[3].astype(BF16)]),
        wdown=p["w_down"].astype(BF16),
        conv_qkv=p["conv_qkv"].astype(F32), conv_g=p["conv_ffn"][:, :D_FF].astype(F32),
        conv_v=p["conv_ffn"][:, D_FF:].astype(F32), w_pool=p["w_pool"].astype(F32),
        pool_scale=row(p["pool_scale"]), head_norm=row(p["head_norm"]), norm_mix=row(p["norm_mix"]),
        norm_ffn=row(p["norm_ffn"]), pa=_lanes_8_to_15(p["a_log"]), pdt=_lanes_8_to_15(p["dt_bias"]))


def layer_grads(g):
    return dict(
        w_in=jnp.concatenate([g["wqkv"], g["wz"], g["wba"][:, :2 * HEADS], g["wpl"], g["wg"]], axis=1),
        conv_qkv=g["conv_qkv"][:4], a_log=g["pa"][0, HEADS:2 * HEADS], dt_bias=g["pdt"][0, HEADS:2 * HEADS],
        head_norm=g["head_norm"][0], w_pool=g["w_pool"], pool_scale=g["pool_scale"][0], w_out=g["wout"],
        norm_mix=g["norm_mix"][0], norm_ffn=g["norm_ffn"][0],
        w_up=jnp.concatenate([g["wupg"], g["wupv"]], axis=1),
        conv_ffn=jnp.concatenate([g["conv_g"][:3], g["conv_v"][:3]], axis=1), w_down=g["wdown"])


def big_grad_shards(g):
    in_shards = [_cat([g[name][:, off:off + width] for name, off, width in
                       _overlaps(s * IN_SHARD, (s + 1) * IN_SHARD, IN_PIECES)]) for s in range(4)]
    half = D_FF // 2
    up_shards = [g["wupg"][:, :half], g["wupg"][:, half:], g["wupv"][:, :half], g["wupv"][:, half:]]
    return dict(w_in=jnp.stack(in_shards), w_up=jnp.stack(up_shards),
                w_down=g["wdown"].reshape(4, D_FF // 4, D_MODEL), w_out=g["wout"].reshape(4, D_MODEL // 4, D_MODEL))


LAYER_PARAMS = ("norm_mix", "w_in", "conv_qkv", "a_log", "dt_bias", "head_norm", "w_pool", "pool_scale", "w_out",
                "norm_ffn", "w_up", "conv_ffn", "w_down")


def pad_rows(meta, x):
    return jnp.concatenate([jnp.zeros((LEAD, D_MODEL), F32), meta.astype(F32), x.astype(F32),
                            jnp.zeros((TAIL, D_MODEL), F32)], axis=0)


MESH = pl.DeviceIdType.MESH
ANY = pl.BlockSpec(memory_space=pl.ANY)


def _place():
    x, y, c = lax.axis_index("x"), lax.axis_index("y"), lax.axis_index("c")
    return x, y, c, [(1 - x, y), (x, 1 - y), (1 - x, 1 - y)]


def _my_chip():
    return 2 * lax.axis_index("x") + lax.axis_index("y")


def gather_shards(packs):
    n = len(packs)

    def body(*refs):
        p_refs, o_refs, (send_sems, recv_sems) = refs[:n], refs[n:2 * n], refs[2 * n:]
        x, y, c, chips = _place()

        def copy(a, k, chip, half, to, src=None):
            dst = o_refs[a].at[2 * chip[0] + chip[1], half]
            return pltpu.make_async_remote_copy(src_ref=dst if src is None else src, dst_ref=dst,
                                                send_sem=send_sems.at[6 * a + k], recv_sem=recv_sems.at[6 * a + k],
                                                device_id=to, device_id_type=MESH)

        first = [copy(a, j, (x, y), c, (*chip, c), src=p_refs[a].at[c]) for a in range(n) for j, chip in enumerate(chips)]
        for cp in first:
            cp.start()
        passed = []
        for a in range(n):
            for j, chip in enumerate(chips):
                copy(a, j, chip, c, (x, y, c)).wait_recv()
                passed.append(copy(a, 3 + j, chip, c, (x, y, 1 - c)))
                passed[-1].start()
        for a in range(n):
            for j, chip in enumerate(chips):
                copy(a, 3 + j, chip, 1 - c, (x, y, c)).wait_recv()
        for cp in first + passed:
            cp.wait_send()

    gathered = pl.pallas_call(
        body, in_specs=[ANY] * n, out_specs=[ANY] * n,
        out_shape=[jax.ShapeDtypeStruct((4,) + p.shape, p.dtype) for p in packs],
        scratch_shapes=[pltpu.SemaphoreType.DMA((6 * n,)), pltpu.SemaphoreType.DMA((6 * n,))],
        name="gather_shards",
    )(*packs)
    me = _my_chip()
    return [lax.dynamic_update_slice(g, p[None], (me,) + (0,) * p.ndim) for g, p in zip(gathered, packs)]


def swap_other_halves(ps):
    n = len(ps)

    def body(*refs):
        p_refs, o_refs, (send_sems, recv_sems) = refs[:n], refs[n:2 * n], refs[2 * n:]
        x, y, c, _ = _place()
        copies = [pltpu.make_async_remote_copy(src_ref=p_refs[a].at[s, 1 - c], dst_ref=o_refs[a].at[s],
                                               send_sem=send_sems.at[4 * a + s], recv_sem=recv_sems.at[4 * a + s],
                                               device_id=(x, y, 1 - c), device_id_type=MESH)
                  for a in range(n) for s in range(4)]
        for cp in copies:
            cp.start()
        for cp in copies:
            cp.wait()

    return pl.pallas_call(
        body, in_specs=[ANY] * n, out_specs=[ANY] * n,
        out_shape=[jax.ShapeDtypeStruct((4,) + p.shape[2:], p.dtype) for p in ps],
        scratch_shapes=[pltpu.SemaphoreType.DMA((4 * n,)), pltpu.SemaphoreType.DMA((4 * n,))],
        name="swap_other_halves",
    )(*ps)


def scatter_to_chips(qs):
    n = len(qs)

    def body(*refs):
        q_refs, o_refs, (send_sems, recv_sems) = refs[:n], refs[n:2 * n], refs[2 * n:]
        x, y, c, chips = _place()
        me = 2 * x + y
        copies = [pltpu.make_async_remote_copy(src_ref=q_refs[a].at[2 * chip[0] + chip[1]], dst_ref=o_refs[a].at[me],
                                               send_sem=send_sems.at[3 * a + j], recv_sem=recv_sems.at[3 * a + j],
                                               device_id=(*chip, c), device_id_type=MESH)
                  for a in range(n) for j, chip in enumerate(chips)]
        for cp in copies:
            cp.start()
        for a in range(n):
            for j, chip in enumerate(chips):
                slot = o_refs[a].at[2 * chip[0] + chip[1]]
                pltpu.make_async_remote_copy(src_ref=slot, dst_ref=slot, send_sem=send_sems.at[3 * a + j],
                                             recv_sem=recv_sems.at[3 * a + j],
                                             device_id=(x, y, c), device_id_type=MESH).wait_recv()
        for cp in copies:
            cp.wait_send()

    received = pl.pallas_call(
        body, in_specs=[ANY] * n, out_specs=[ANY] * n,
        out_shape=[jax.ShapeDtypeStruct(q.shape, q.dtype) for q in qs],
        scratch_shapes=[pltpu.SemaphoreType.DMA((3 * n,)), pltpu.SemaphoreType.DMA((3 * n,))],
        name="scatter_to_chips",
    )(*qs)
    me = _my_chip()
    return [lax.dynamic_update_slice(r, lax.dynamic_slice_in_dim(q, me, 1, axis=0), (me, 0, 0))
            for r, q in zip(received, qs)]


def join_halves(boths):
    n = len(boths)

    def body(*refs):
        o_refs, (send_sems, recv_sems) = refs[n:2 * n], refs[2 * n:]
        x, y, c, _ = _place()
        copies = [pltpu.make_async_remote_copy(src_ref=o_refs[a].at[c], dst_ref=o_refs[a].at[c],
                                               send_sem=send_sems.at[a], recv_sem=recv_sems.at[a],
                                               device_id=(x, y, 1 - c), device_id_type=MESH) for a in range(n)]
        for cp in copies:
            cp.start()
        for a in range(n):
            other = o_refs[a].at[1 - c]
            pltpu.make_async_remote_copy(src_ref=other, dst_ref=other, send_sem=send_sems.at[a],
                                         recv_sem=recv_sems.at[a], device_id=(x, y, c), device_id_type=MESH).wait_recv()
        for cp in copies:
            cp.wait_send()

    return pl.pallas_call(
        body, in_specs=[ANY] * n, out_specs=[ANY] * n,
        out_shape=[jax.ShapeDtypeStruct(b.shape, b.dtype) for b in boths],
        input_output_aliases={a: a for a in range(n)},
        scratch_shapes=[pltpu.SemaphoreType.DMA((n,)), pltpu.SemaphoreType.DMA((n,))], name="join_halves",
    )(*boths)


def add_own_half(p, other, c, out_dtype, name):
    _, _, rows, lanes = p.shape
    tr = _tile(rows, max(16, 524288 // lanes), 16)

    def body(c_ref, p_ref, o_ref, out_ref):
        out_ref[...] = (p_ref[...] + o_ref[...]).astype(out_dtype)

    return pl.pallas_call(
        body,
        grid_spec=pltpu.PrefetchScalarGridSpec(
            num_scalar_prefetch=1, grid=(4, rows // tr),
            in_specs=[pl.BlockSpec((None, None, tr, lanes), lambda s, i, c_ref: (s, c_ref[0], i, 0)),
                      pl.BlockSpec((None, tr, lanes), lambda s, i, c_ref: (s, i, 0))],
            out_specs=pl.BlockSpec((None, tr, lanes), lambda s, i, c_ref: (s, i, 0))),
        out_shape=jax.ShapeDtypeStruct((4, rows, lanes), out_dtype),
        compiler_params=_params("parallel", "parallel"), name=name,
    )(c, p, other)


def sum_chips(b, c, name):
    _, rows, lanes = b.shape
    tr = _tile(rows, max(16, 524288 // lanes), 16)

    def body(c_ref, b_ref, out_ref):
        b0, b1, b2, b3 = (b_ref[k].astype(F32) for k in range(4))
        out_ref[...] = ((b0 + b1) + b2) + b3

    return pl.pallas_call(
        body,
        grid_spec=pltpu.PrefetchScalarGridSpec(
            num_scalar_prefetch=1, grid=(rows // tr,),
            in_specs=[pl.BlockSpec((4, tr, lanes), lambda i, c_ref: (0, i, 0))],
            out_specs=pl.BlockSpec((None, tr, lanes), lambda i, c_ref: (c_ref[0], i, 0))),
        out_shape=jax.ShapeDtypeStruct((2, rows, lanes), F32),
        compiler_params=_params("parallel"), name=name,
    )(c, b)


def all_reduce_to_shards(packs, wires, tags, c):
    others = swap_other_halves(packs)
    qs = [add_own_half(p, o, c, wire, f"add_own_half_{tag}") for p, o, wire, tag in zip(packs, others, wires, tags)]
    return join_halves([sum_chips(r, c, f"sum_chips_{tag}") for r, tag in zip(scatter_to_chips(qs), tags)])


def adamw(w, g, m, v, name):
    shape = w.shape
    cols = shape[-1]
    w2, g2, m2, v2 = (a.reshape(-1, cols) for a in (w, g, m, v))
    rows = w2.shape[0]
    tr = _tile(rows, max(8, 262144 // cols), 8) if rows % 8 == 0 else rows
    c1 = 1.0 - ADAM_B1 ** ADAM_STEP
    c2 = 1.0 - ADAM_B2 ** ADAM_STEP

    def body(w_ref, g_ref, m_ref, v_ref, d_ref, mo_ref, vo_ref):
        gv = g_ref[...]
        mn = ADAM_B1 * m_ref[...] + (1.0 - ADAM_B1) * gv
        vn = ADAM_B2 * v_ref[...] + (1.0 - ADAM_B2) * jnp.square(gv)
        d_ref[...] = -ADAM_LR * ((mn / c1) / (jnp.sqrt(vn / c2) + ADAM_EPS) + ADAM_WD * w_ref[...])
        mo_ref[...] = mn
        vo_ref[...] = vn

    spec = pl.BlockSpec((tr, cols), lambda i: (i, 0))
    out = jax.ShapeDtypeStruct((rows, cols), F32)
    d, mn, vn = pl.pallas_call(
        body, grid=(rows // tr,), in_specs=[spec] * 4, out_specs=(spec,) * 3, out_shape=(out,) * 3,
        compiler_params=_params("parallel"), name=name,
    )(w2, g2, m2, v2)
    return d.reshape(shape), mn.reshape(shape), vn.reshape(shape)


BIG = ("w_in", "w_up", "w_down", "w_out")
SMALL = ("w_pool", "conv_qkv", "conv_ffn", "meta_tokens")
SHARDED = BIG + SMALL
MATMUL_WEIGHTS = BIG + ("w_pool",)
REPLICATED = ("norm_mix", "a_log", "dt_bias", "head_norm", "pool_scale", "norm_ffn", "norm_final")
SHARD_AXIS = {"w_in": 2, "w_up": 2, "w_out": 1, "w_down": 1, "w_pool": 3, "conv_qkv": 2, "conv_ffn": 2, "meta_tokens": 1}


def _rows_of(a):
    return a.reshape(-1, 128)


SEGMENT_ROWS = 16


def _segment(n_rows):
    return -(-n_rows // SEGMENT_ROWS) * SEGMENT_ROWS


def _pad_segment(a):
    pad = [(0, 0)] * a.ndim
    pad[-2] = (0, _segment(a.shape[-2]) - a.shape[-2])
    return jnp.pad(a, pad)


def _unshard(stacked, axis):
    full = jnp.moveaxis(stacked, 0, axis)
    shape = list(full.shape)
    shape[axis:axis + 2] = [shape[axis] * shape[axis + 1]]
    return full.reshape(shape)


def _shard_stack(full, axis):
    shape = list(full.shape)
    shape[axis:axis + 1] = [4, shape[axis] // 4]
    return jnp.moveaxis(full.reshape(shape), axis, 0)


def pack_weights(shards):
    parts = [_rows_of(shards[k].astype(WIRE)) if k in MATMUL_WEIGHTS else
             lax.bitcast_convert_type(_rows_of(shards[k].astype(F32)), WIRE).reshape(-1, 128) for k in SMALL]
    parts = [_pad_segment(p) for p in parts]
    rows = sum(p.shape[0] for p in parts)
    if rows % (2 * SEGMENT_ROWS):
        parts.append(jnp.zeros((SEGMENT_ROWS, 128), WIRE))
        rows += SEGMENT_ROWS
    return [shards[k].astype(WIRE) for k in BIG] + [jnp.concatenate(parts, axis=0).reshape(2, rows // 2, 128)]


def unpack_weights(gathered, shard_shapes):
    out = {k: _unshard(g, SHARD_AXIS[k]) for k, g in zip(BIG, gathered)}
    flat = gathered[-1].reshape(4, -1, 128)
    at = 0
    for k in SMALL:
        shp = shard_shapes[k]
        n = 1
        for e in shp:
            n *= e
        if k in MATMUL_WEIGHTS:
            r = n // 128
            stacked = flat[:, at:at + r].reshape((4,) + tuple(shp))
        else:
            r = 2 * n // 128
            stacked = lax.bitcast_convert_type(flat[:, at:at + r].reshape(4, n // 128, 128, 2), F32).reshape((4,) + tuple(shp))
        out[k] = _unshard(stacked, SHARD_AXIS[k])
        at += _segment(r)
    return out


def pack_grads(big, full, repl):
    r = jnp.concatenate([repl[k].reshape(-1) for k in REPLICATED])
    r = jnp.pad(r, (0, -r.shape[0] % 128)).reshape(1, -1, 128)
    parts = [_shard_stack(full[k], SHARD_AXIS[k]).reshape(4, -1, 128) for k in SMALL]
    parts = [_pad_segment(p) for p in parts + [jnp.broadcast_to(r, (4,) + r.shape[1:])]]
    rows = sum(p.shape[1] for p in parts)
    if rows % (2 * SEGMENT_ROWS):
        parts.append(jnp.zeros((4, SEGMENT_ROWS, 128), F32))
        rows += SEGMENT_ROWS
    side = jnp.concatenate(parts, axis=1).reshape(4, 2, rows // 2, 128)
    return list(big) + [side]


def unpack_grads(reduced, shard_shapes, repl_shapes):
    out = dict(zip(BIG, reduced))
    side = reduced[-1].reshape(-1, 128)
    at = 0
    for k in SMALL:
        n = 1
        for e in shard_shapes[k]:
            n *= e
        out[k] = side[at:at + n // 128].reshape(shard_shapes[k])
        at += _segment(n // 128)
    r = side[at:].reshape(-1)
    at = 0
    for k in REPLICATED:
        n = 1
        for e in repl_shapes[k]:
            n *= e
        out[k] = r[at:at + n].reshape(repl_shapes[k])
        at += n
    return out


WEIGHT_ORDER = ("meta_tokens", "norm_mix", "w_in", "conv_qkv", "a_log", "dt_bias", "head_norm", "w_pool", "pool_scale",
                "w_out", "norm_ffn", "w_up", "conv_ffn", "w_down", "norm_final")


def kernel(x, meta_tokens, norm_mix, w_in, conv_qkv, a_log, dt_bias, head_norm, w_pool, pool_scale, w_out, norm_ffn, w_up, conv_ffn, w_down, norm_final, loss_target, m_meta_tokens, m_norm_mix, m_w_in, m_conv_qkv, m_a_log, m_dt_bias, m_head_norm, m_w_pool, m_pool_scale, m_w_out, m_norm_ffn, m_w_up, m_conv_ffn, m_w_down, m_norm_final, v_meta_tokens, v_norm_mix, v_w_in, v_conv_qkv, v_a_log, v_dt_bias, v_head_norm, v_w_pool, v_pool_scale, v_w_out, v_norm_ffn, v_w_up, v_conv_ffn, v_w_down, v_norm_final):
    weights = dict(meta_tokens=meta_tokens, norm_mix=norm_mix, w_in=w_in, conv_qkv=conv_qkv, a_log=a_log,
                   dt_bias=dt_bias, head_norm=head_norm, w_pool=w_pool, pool_scale=pool_scale, w_out=w_out,
                   norm_ffn=norm_ffn, w_up=w_up, conv_ffn=conv_ffn, w_down=w_down, norm_final=norm_final)
    m_in = dict(zip(WEIGHT_ORDER, (m_meta_tokens, m_norm_mix, m_w_in, m_conv_qkv, m_a_log, m_dt_bias, m_head_norm,
                                   m_w_pool, m_pool_scale, m_w_out, m_norm_ffn, m_w_up, m_conv_ffn, m_w_down, m_norm_final)))
    v_in = dict(zip(WEIGHT_ORDER, (v_meta_tokens, v_norm_mix, v_w_in, v_conv_qkv, v_a_log, v_dt_bias, v_head_norm,
                                   v_w_pool, v_pool_scale, v_w_out, v_norm_ffn, v_w_up, v_conv_ffn, v_w_down, v_norm_final)))
    shard_shapes = {k: weights[k].shape for k in SHARDED}
    repl_shapes = {k: weights[k].shape for k in REPLICATED}
    core = lax.axis_index("c").astype(jnp.int32).reshape(1)

    gathered = gather_shards(pack_weights({k: weights[k] for k in SHARDED}))
    full = unpack_weights(gathered, shard_shapes)
    shards = dict(zip(BIG, gathered))
    layers = []
    for li in range(DEPTH):
        p = {k: (full[k][li] if k in SMALL else weights[k][li]) for k in LAYER_PARAMS if k not in BIG}
        p.update(w_in=[shards["w_in"][s, li] for s in range(4)], w_up=[shards["w_up"][s, li] for s in range(4)],
                 w_down=shards["w_down"][:, li].reshape(D_FF, D_MODEL),
                 w_out=shards["w_out"][:, li].reshape(D_MODEL, D_MODEL))
        layers.append(prep_layer(p))

    h0 = pad_rows(full["meta_tokens"], x[0])
    target = pad_rows(jnp.zeros((N_META, D_MODEL), F32), loss_target[0])
    loss, dh0, grads, d_norm_final = local_step(h0, target, layers, norm_final.reshape(1, D_MODEL))
    seq = x.shape[1]
    grad_x = dh0[LEAD + N_META:LEAD + N_META + seq][None]

    per_layer = [layer_grads(g) for g in grads]
    g_all = {k: jnp.stack([pl_[k] for pl_ in per_layer]) for k in LAYER_PARAMS if k not in BIG}
    g_all["meta_tokens"] = dh0[LEAD:LEAD + N_META]
    g_all["norm_final"] = d_norm_final[0]
    big = [big_grad_shards(g) for g in grads]
    packs = pack_grads([jnp.stack([b[k] for b in big], axis=1) for k in BIG],
                       {k: g_all[k] for k in SMALL}, {k: g_all[k] for k in REPLICATED})
    reduced = all_reduce_to_shards(packs, [GRAD_WIRE] * len(BIG) + [F32], BIG + ("side",), core)
    g_mine = unpack_grads(reduced, shard_shapes, repl_shapes)

    loss_sum = lax.psum(loss[0, 0], ("x", "y", "c"))
    deltas, new_m, new_v = {}, {}, {}
    for k in WEIGHT_ORDER:
        deltas[k], new_m[k], new_v[k] = adamw(weights[k], g_mine[k], m_in[k], v_in[k], f"adamw_{k}")
    return (loss_sum, grad_x, *[g_mine[k] for k in WEIGHT_ORDER], *[deltas[k] for k in WEIGHT_ORDER],
            *[new_m[k] for k in WEIGHT_ORDER], *[new_v[k] for k in WEIGHT_ORDER])
```
